```python
import jax, jax.numpy as jnp
from jax import lax
import numpy as np

D_MODEL = 1024
BATCH = 32
SEQ = 2048
DEPTH = 1

D_CONV = D_MODEL
CONV_WIDTH = 3
HEAD_DIM = 64
N_HEADS = D_MODEL // HEAD_DIM
N_KV_HEADS = 2
GROUP = N_HEADS // N_KV_HEADS
D_ATTN = N_HEADS * HEAD_DIM
D_KV = N_KV_HEADS * HEAD_DIM
WINDOW = 128
BLOCK = WINDOW
ROPE_THETA = 10000.0
RMS_EPS = 1e-6
SPLITS = (D_CONV, D_CONV, D_CONV, D_CONV, D_ATTN, D_KV, D_KV, D_ATTN, D_MODEL, D_MODEL)
D_IN = int(sum(SPLITS))
SPLIT_IDX = [int(s) for s in np.cumsum(SPLITS)[:-1]]

kernel_name = "hybrid_shortconv_swa_sink_gated_merge"


def rmsnorm(x, g):
    xf = x.astype(jnp.float32)
    y = xf * lax.rsqrt(jnp.mean(xf * xf, axis=-1, keepdims=True) + RMS_EPS)
    return (y * g.astype(jnp.float32)).astype(x.dtype)


def short_gated_conv(xc, bg, cg, w_conv):
    u = cg * xc
    t = u.shape[1]
    u_pad = jnp.pad(u, ((0, 0), (CONV_WIDTH - 1, 0), (0, 0)))
    y = w_conv[0] * u_pad[:, 0:t]
    for tap in range(1, CONV_WIDTH):
        y = y + w_conv[tap] * u_pad[:, tap:tap + t]
    return bg * y


def rope(z, positions):
    inv_freq = ROPE_THETA ** (-jnp.arange(0, HEAD_DIM, 2, dtype=jnp.float32) / HEAD_DIM)
    ang = positions.astype(jnp.float32)[:, None] * inv_freq[None, :]
    cos = jnp.cos(ang)[:, None, :]
    sin = jnp.sin(ang)[:, None, :]
    zf = z.astype(jnp.float32)
    z1, z2 = zf[..., :HEAD_DIM // 2], zf[..., HEAD_DIM // 2:]
    out = jnp.concatenate([z1 * cos - z2 * sin, z2 * cos + z1 * sin], axis=-1)
    return out.astype(z.dtype)


def sliding_window_attention(q, k, v, sinks):
    b, t = q.shape[0], q.shape[1]
    nblk = t // BLOCK
    qb = q.reshape(b, nblk, BLOCK, N_KV_HEADS, GROUP, HEAD_DIM).transpose(1, 0, 2, 3, 4, 5)

    def band(z):
        zb = z.reshape(b, nblk, BLOCK, N_KV_HEADS, HEAD_DIM)
        prev = jnp.concatenate([jnp.zeros_like(zb[:, :1]), zb[:, :-1]], axis=1)
        return jnp.concatenate([prev, zb], axis=2).transpose(1, 0, 2, 3, 4)

    kb, vb = band(k), band(v)
    qi = jnp.arange(BLOCK)[:, None]
    kj = jnp.arange(2 * BLOCK)[None, :]
    in_band = (kj > qi) & (kj <= qi + BLOCK)
    sink = sinks.astype(jnp.float32).reshape(N_KV_HEADS, GROUP, 1, 1)
    scale = HEAD_DIM ** -0.5

    def one_block(args):
        blk, qblk, kblk, vblk = args
        logits = jnp.einsum('bqkgd,bskd->bkgqs', qblk, kblk).astype(jnp.float32) * scale
        valid = in_band & (kj >= BLOCK - blk * BLOCK)
        logits = jnp.where(valid, logits, -jnp.inf)
        m = jnp.maximum(jnp.max(logits, axis=-1, keepdims=True), sink)
        p = jnp.exp(logits - m)
        denom = jnp.sum(p, axis=-1, keepdims=True) + jnp.exp(sink - m)
        probs = (p / denom).astype(vblk.dtype)
        return jnp.einsum('bkgqs,bskd->bqkgd', probs, vblk)

    out = lax.map(one_block, (jnp.arange(nblk), qb, kb, vb))
    return out.transpose(1, 0, 2, 3, 4, 5).reshape(b, t, D_ATTN)


def hybrid_layer(x, g_pre, g_post, w_in, w_conv, sinks, w_proj_conv, w_proj_attn, w_out):
    b, t, _ = x.shape
    h = rmsnorm(x, g_pre)
    proj = jnp.einsum('btd,de->bte', h, w_in)
    xc, bg, cg, zc, q, k, v, za, ga, gb = jnp.split(proj, SPLIT_IDX, axis=-1)

    ua = jax.nn.silu(zc) * short_gated_conv(xc, bg, cg, w_conv)
    ya = jnp.einsum('btc,cd->btd', ua, w_proj_conv)

    positions = jnp.arange(t)
    q = rope(q.reshape(b, t, N_HEADS, HEAD_DIM), positions)
    k = rope(k.reshape(b, t, N_KV_HEADS, HEAD_DIM), positions)
    v = v.reshape(b, t, N_KV_HEADS, HEAD_DIM)
    ub = jax.nn.silu(za) * sliding_window_attention(q, k, v, sinks)
    yb = jnp.einsum('bta,ad->btd', ub, w_proj_attn)

    merged = jax.nn.sigmoid(ga) * ya + jax.nn.sigmoid(gb) * yb
    y = jnp.einsum('btd,de->bte', merged, w_out)
    return x + rmsnorm(y, g_post)


def _fwd_setup_inputs(seed: int = 0) -> dict:
    key = jax.random.key(seed)
    ks = jax.random.split(key, 10)
    f32 = jnp.float32
    x = jax.random.normal(ks[0], (BATCH, SEQ, D_MODEL), f32)
    g_pre = 1.0 + 0.05 * jax.random.normal(ks[1], (DEPTH, D_MODEL), f32)
    g_post = 1.0 + 0.05 * jax.random.normal(ks[2], (DEPTH, D_MODEL), f32)
    w_in = jax.random.normal(ks[3], (DEPTH, D_MODEL, D_IN), f32) * D_MODEL ** -0.5
    w_conv = jax.random.normal(ks[4], (DEPTH, CONV_WIDTH, D_CONV), f32) * CONV_WIDTH ** -0.5
    sinks = 0.5 * jax.random.normal(ks[5], (DEPTH, N_HEADS), f32)
    w_proj_conv = jax.random.normal(ks[6], (DEPTH, D_CONV, D_MODEL), f32) * D_CONV ** -0.5
    w_proj_attn = jax.random.normal(ks[7], (DEPTH, D_ATTN, D_MODEL), f32) * D_ATTN ** -0.5
    w_out = jax.random.normal(ks[8], (DEPTH, D_MODEL, D_MODEL), f32) * D_MODEL ** -0.5
    return {"x": x, "g_pre": g_pre, "g_post": g_post, "w_in": w_in, "w_conv": w_conv,
            "sinks": sinks, "w_proj_conv": w_proj_conv, "w_proj_attn": w_proj_attn, "w_out": w_out}


def _fwd_reference(x, g_pre, g_post, w_in, w_conv, sinks, w_proj_conv, w_proj_attn, w_out):
    for layer in range(DEPTH):
        x = hybrid_layer(x, g_pre[layer], g_post[layer], w_in[layer], w_conv[layer], sinks[layer],
                         w_proj_conv[layer], w_proj_attn[layer], w_out[layer])
    return x


import jax as _jax
import jax.numpy as _jnp

TWIN_FORMAT = 'train_step'
FWD_PARAMS = ['x', 'g_pre', 'g_post', 'w_in', 'w_conv', 'sinks', 'w_proj_conv', 'w_proj_attn', 'w_out']
TWIN_WEIGHTS = ['g_pre', 'g_post', 'w_in', 'w_conv', 'sinks', 'w_proj_conv', 'w_proj_attn', 'w_out']
TWIN_DIFF_INPUT = 'x'
TWIN_INPUTS = ['x', 'g_pre', 'g_post', 'w_in', 'w_conv', 'sinks', 'w_proj_conv', 'w_proj_attn', 'w_out', 'loss_target', 'm_g_pre', 'm_g_post', 'm_w_in', 'm_w_conv', 'm_sinks', 'm_w_proj_conv', 'm_w_proj_attn', 'm_w_out', 'v_g_pre', 'v_g_post', 'v_w_in', 'v_w_conv', 'v_sinks', 'v_w_proj_conv', 'v_w_proj_attn', 'v_w_out']
TWIN_OUTPUTS = ['loss', 'grad_x', 'grad_g_pre', 'grad_g_post', 'grad_w_in', 'grad_w_conv', 'grad_sinks', 'grad_w_proj_conv', 'grad_w_proj_attn', 'grad_w_out', 'delta_g_pre', 'delta_g_post', 'delta_w_in', 'delta_w_conv', 'delta_sinks', 'delta_w_proj_conv', 'delta_w_proj_attn', 'delta_w_out', 'new_m_g_pre', 'new_m_g_post', 'new_m_w_in', 'new_m_w_conv', 'new_m_sinks', 'new_m_w_proj_conv', 'new_m_w_proj_attn', 'new_m_w_out', 'new_v_g_pre', 'new_v_g_post', 'new_v_w_in', 'new_v_w_conv', 'new_v_sinks', 'new_v_w_proj_conv', 'new_v_w_proj_attn', 'new_v_w_out']
TWIN_LEAF_KINDS = {'loss': 'loss', 'grad_x': 'grad_x', 'grad_g_pre': 'grad_w', 'grad_g_post': 'grad_w', 'grad_w_in': 'grad_w', 'grad_w_conv': 'grad_w', 'grad_sinks': 'grad_w', 'grad_w_proj_conv': 'grad_w', 'grad_w_proj_attn': 'grad_w', 'grad_w_out': 'grad_w', 'delta_g_pre': 'delta_w', 'delta_g_post': 'delta_w', 'delta_w_in': 'delta_w', 'delta_w_conv': 'delta_w', 'delta_sinks': 'delta_w', 'delta_w_proj_conv': 'delta_w', 'delta_w_proj_attn': 'delta_w', 'delta_w_out': 'delta_w', 'new_m_g_pre': 'new_m', 'new_m_g_post': 'new_m', 'new_m_w_in': 'new_m', 'new_m_w_conv': 'new_m', 'new_m_sinks': 'new_m', 'new_m_w_proj_conv': 'new_m', 'new_m_w_proj_attn': 'new_m', 'new_m_w_out': 'new_m', 'new_v_g_pre': 'new_v', 'new_v_g_post': 'new_v', 'new_v_w_in': 'new_v', 'new_v_w_conv': 'new_v', 'new_v_sinks': 'new_v', 'new_v_w_proj_conv': 'new_v', 'new_v_w_proj_attn': 'new_v', 'new_v_w_out': 'new_v'}


def _forward(args):
    return _fwd_reference(*[args[k] for k in FWD_PARAMS])


def _output_shape():
    out = _jax.eval_shape(lambda: _forward(_fwd_setup_inputs(0)))
    return out.shape, out.dtype

N_MICROBATCH = 1
ADAM_LR = 0.001
ADAM_B1 = 0.9
ADAM_B2 = 0.999
ADAM_EPS = 1e-08
ADAM_WD = 0.01
ADAM_STEP = 10
PER_EXAMPLE_BATCH_AXIS = {'x': 0, 'loss_target': 0}
SHARED_INPUTS = []
_WEIGHT_DTYPES = {'g_pre': _jnp.float32, 'g_post': _jnp.float32, 'w_in': _jnp.float32, 'w_conv': _jnp.float32, 'sinks': _jnp.float32, 'w_proj_conv': _jnp.float32, 'w_proj_attn': _jnp.float32, 'w_out': _jnp.float32}
MOMENT_SCALE = {'g_pre': 7.423994e-01, 'g_post': 6.375405e+01, 'w_in': 2.594205e-01, 'w_conv': 3.834176e-01, 'sinks': 7.186906e-02, 'w_proj_conv': 3.663713e-01, 'w_proj_attn': 7.478818e-02, 'w_out': 3.750107e-01}


def _to_microbatches(a, axis):
    t = _jnp.moveaxis(a, axis, 0)
    t = t.reshape((N_MICROBATCH, t.shape[0] // N_MICROBATCH) + t.shape[1:])
    return _jnp.moveaxis(t, 1, axis + 1)


def setup_inputs(seed: int = 0) -> dict:
    inp = _fwd_setup_inputs(seed)
    key = _jax.random.fold_in(_jax.random.key(seed), 7919)
    shape, _ = _output_shape()
    out = dict(inp)
    out["loss_target"] = _jax.random.normal(_jax.random.fold_in(key, 0), shape, _jnp.float32)
    for i, name in enumerate(TWIN_WEIGHTS):
        w = inp[name].astype(_jnp.float32)
        if MOMENT_SCALE is None:
            s = _jnp.sqrt(_jnp.mean(_jnp.square(w)) + 1e-30)
        else:
            s = MOMENT_SCALE[name]
        km, kv = _jax.random.split(_jax.random.fold_in(key, i + 1))
        out[name] = w
        out["m_" + name] = s * _jax.random.normal(km, w.shape, _jnp.float32)
        out["v_" + name] = (s * s) * _jax.random.uniform(kv, w.shape, _jnp.float32, 0.5, 1.5)
    if N_MICROBATCH > 1:
        for name, axis in PER_EXAMPLE_BATCH_AXIS.items():
            out[name] = _to_microbatches(out[name], axis)
    return {'x': out['x'], 'g_pre': out['g_pre'], 'g_post': out['g_post'], 'w_in': out['w_in'], 'w_conv': out['w_conv'], 'sinks': out['sinks'], 'w_proj_conv': out['w_proj_conv'], 'w_proj_attn': out['w_proj_attn'], 'w_out': out['w_out'], 'loss_target': out['loss_target'], 'm_g_pre': out['m_g_pre'], 'm_g_post': out['m_g_post'], 'm_w_in': out['m_w_in'], 'm_w_conv': out['m_w_conv'], 'm_sinks': out['m_sinks'], 'm_w_proj_conv': out['m_w_proj_conv'], 'm_w_proj_attn': out['m_w_proj_attn'], 'm_w_out': out['m_w_out'], 'v_g_pre': out['v_g_pre'], 'v_g_post': out['v_g_post'], 'v_w_in': out['v_w_in'], 'v_w_conv': out['v_w_conv'], 'v_sinks': out['v_sinks'], 'v_w_proj_conv': out['v_w_proj_conv'], 'v_w_proj_attn': out['v_w_proj_attn'], 'v_w_out': out['v_w_out']}


def _loss(weights, diff, rest, loss_target):
    with _jax.named_scope("forward"):
        args = {**rest, TWIN_DIFF_INPUT: diff, **{k: w.astype(_WEIGHT_DTYPES[k]) for k, w in weights.items()}}
        y = _forward(args)
    with _jax.named_scope("loss_head"):
        err = _jnp.square(y.astype(_jnp.float32) - loss_target)
        return 0.5 * _jnp.sum(_jnp.mean(err, axis=-1)) if err.ndim else 0.5 * err


def _adamw(w, g, m, v):
    m = ADAM_B1 * m + (1.0 - ADAM_B1) * g
    v = ADAM_B2 * v + (1.0 - ADAM_B2) * _jnp.square(g)
    m_hat = m / (1.0 - ADAM_B1 ** ADAM_STEP)
    v_hat = v / (1.0 - ADAM_B2 ** ADAM_STEP)
    delta = -ADAM_LR * (m_hat / (_jnp.sqrt(v_hat) + ADAM_EPS) + ADAM_WD * w)
    return delta, m, v


def reference(x, g_pre, g_post, w_in, w_conv, sinks, w_proj_conv, w_proj_attn, w_out, loss_target, m_g_pre, m_g_post, m_w_in, m_w_conv, m_sinks, m_w_proj_conv, m_w_proj_attn, m_w_out, v_g_pre, v_g_post, v_w_in, v_w_conv, v_sinks, v_w_proj_conv, v_w_proj_attn, v_w_out):
    given = dict(x=x, g_pre=g_pre, g_post=g_post, w_in=w_in, w_conv=w_conv, sinks=sinks, w_proj_conv=w_proj_conv, w_proj_attn=w_proj_attn, w_out=w_out, loss_target=loss_target, m_g_pre=m_g_pre, m_g_post=m_g_post, m_w_in=m_w_in, m_w_conv=m_w_conv, m_sinks=m_sinks, m_w_proj_conv=m_w_proj_conv, m_w_proj_attn=m_w_proj_attn, m_w_out=m_w_out, v_g_pre=v_g_pre, v_g_post=v_g_post, v_w_in=v_w_in, v_w_conv=v_w_conv, v_sinks=v_sinks, v_w_proj_conv=v_w_proj_conv, v_w_proj_attn=v_w_proj_attn, v_w_out=v_w_out)
    weights = {n: given[n] for n in TWIN_WEIGHTS}
    shared = {n: given[n] for n in SHARED_INPUTS}
    per_example = {n: given[n] for n in ['x']}
    grad_fn = _jax.value_and_grad(_loss, argnums=(0, 1))

    def one_microbatch(ex, loss_target):
        ex = dict(ex)
        diff = ex.pop(TWIN_DIFF_INPUT)
        return grad_fn(weights, diff, {**shared, **ex}, loss_target)

    if N_MICROBATCH == 1:
        loss, (grad_w, grad_x) = one_microbatch(per_example, given["loss_target"])
    else:
        def body(carry, xs):
            loss_sum, grad_sum = carry
            l_k, (gw_k, gx_k) = one_microbatch(xs[0], xs[1])
            with _jax.named_scope("update"):
                return (loss_sum + l_k, _jax.tree.map(_jnp.add, grad_sum, gw_k)), gx_k

        init = (_jnp.zeros((), _jnp.float32), _jax.tree.map(_jnp.zeros_like, weights))
        (loss, grad_w), grad_x = _jax.lax.scan(body, init, (per_example, given["loss_target"]))
    with _jax.named_scope("update"):
        delta_w, new_m, new_v = {}, {}, {}
        for n in TWIN_WEIGHTS:
            delta_w[n], new_m[n], new_v[n] = _adamw(weights[n], grad_w[n], given["m_" + n], given["v_" + n])
    return (loss, grad_x, *[grad_w[n] for n in TWIN_WEIGHTS], *[delta_w[n] for n in TWIN_WEIGHTS],
            *[new_m[n] for n in TWIN_WEIGHTS], *[new_v[n] for n in TWIN_WEIGHTS])
```

```python
import functools

import jax
import jax.numpy as jnp
from jax import lax
from jax.experimental import pallas as pl
from jax.experimental.pallas import tpu as pltpu

D = 1024
N_HEADS = 16
HEAD_DIM = 64
BLK = 128
SEQ_LEN = 2048
D_IN = 8448
N_DEV = 8
ROPE_THETA = 10000.0
RMS_EPS = 1e-6
NEG = -1e30
ADAM_LR, ADAM_B1, ADAM_B2, ADAM_EPS, ADAM_WD, ADAM_STEP = 0.001, 0.9, 0.999, 1e-08, 0.01, 10

F32 = jnp.float32
BF16 = jnp.bfloat16
MESH_ID = pl.DeviceIdType.MESH


def _dot(a, b):
    return jnp.dot(a, b, preferred_element_type=F32)


def _dot_nt(a, b):
    return lax.dot_general(a, b, (((1,), (1,)), ((), ())), preferred_element_type=F32)


def _dot_tn(a, b):
    return lax.dot_general(a, b, (((0,), (0,)), ((), ())), preferred_element_type=F32)


def _sig(z):
    return 1.0 / (1.0 + jnp.exp(-z))


def _swap_halves(z):
    lane = lax.broadcasted_iota(jnp.int32, z.shape, 1)
    return jnp.where((lane & 63) < 32, pltpu.roll(z, 96, 1), pltpu.roll(z, 32, 1))


def _row_spec(tm, width, col=0):
    return pl.BlockSpec((tm, width), lambda i: (i, col))


def _whole_vmem():
    return pl.BlockSpec(memory_space=pltpu.VMEM)


def _params(*sem):
    return pltpu.CompilerParams(dimension_semantics=sem)


def _fwd_in_a(x, g_pre, wa, tm):
    t = x.shape[0]

    def body(x_ref, g_ref, w_ref, h_ref, a4_ref):
        xf = x_ref[...]
        r = lax.rsqrt(jnp.mean(xf * xf, axis=-1, keepdims=True) + RMS_EPS)
        h = ((xf * r) * g_ref[...]).astype(BF16)
        h_ref[...] = h
        for j in range(4):
            a4_ref[:, j * D:(j + 1) * D] = _dot(h, w_ref[:, j * D:(j + 1) * D])

    return pl.pallas_call(
        body, name="fwd_in_a", grid=(t // tm,),
        in_specs=[_row_spec(tm, D), pl.BlockSpec((1, D), lambda i: (0, 0)), _whole_vmem()],
        out_specs=[_row_spec(tm, D), _row_spec(tm, 4 * D)],
        out_shape=[jax.ShapeDtypeStruct((t, D), BF16), jax.ShapeDtypeStruct((t, 4 * D), F32)],
        compiler_params=_params("parallel"),
    )(x, g_pre, wa)


def _fwd_in_b(h, wb, cos_t, sin_t, tm):
    t = h.shape[0]
    seq_tiles = SEQ_LEN // tm

    def body(h_ref, w_ref, c_ref, s_ref, q_ref, kv_ref, g3_ref):
        hh = h_ref[...]
        c = c_ref[...]
        s = s_ref[...]

        def rope(z):
            return z * c + _swap_halves(z) * s

        q = _dot(hh, w_ref[:, 0:D])
        for j in range(D // 128):
            q_ref[:, j * 128:(j + 1) * 128] = rope(q[:, j * 128:(j + 1) * 128]).astype(BF16)
        kv = _dot(hh, w_ref[:, D:D + 256])
        kv_ref[:, 0:128] = rope(kv[:, 0:128]).astype(BF16)
        kv_ref[:, 128:256] = kv[:, 128:256].astype(BF16)
        for j in range(3):
            g3_ref[:, j * D:(j + 1) * D] = _dot(hh, w_ref[:, D + 256 + j * D:D + 256 + (j + 1) * D])

    tab = pl.BlockSpec((tm, 128), lambda i: (i % seq_tiles, 0))
    return pl.pallas_call(
        body, name="fwd_in_b", grid=(t // tm,),
        in_specs=[_row_spec(tm, D), _whole_vmem(), tab, tab],
        out_specs=[_row_spec(tm, D), _row_spec(tm, 256), _row_spec(tm, 3 * D)],
        out_shape=[jax.ShapeDtypeStruct((t, D), BF16), jax.ShapeDtypeStruct((t, 256), BF16),
                   jax.ShapeDtypeStruct((t, 3 * D), F32)],
        compiler_params=_params("parallel"),
    )(h, wb, cos_t, sin_t)


def _conv_forward(xc, bg, cg, zc, up6, up7, w_ref):
    tm = xc.shape[0]
    rows = lax.broadcasted_iota(jnp.int32, xc.shape, 0)
    u = cg * xc
    u_m1 = jnp.where(rows == 0, up7, pltpu.roll(u, 1, 0))
    u_m2 = jnp.where(rows == 0, up6, jnp.where(rows == 1, up7, pltpu.roll(u, 2, 0)))
    yconv = w_ref[0:1, :] * u_m2 + w_ref[1:2, :] * u_m1 + w_ref[2:3, :] * u
    sg = _sig(zc)
    sz = zc * sg
    co = bg * yconv
    del tm
    return u, u_m1, u_m2, yconv, sg, sz, co


def _fwd_conv(a4, wconv8, wpc, tm):
    t = a4.shape[0]
    seq_tiles = SEQ_LEN // tm

    def body(xc_ref, bg_ref, cg_ref, zc_ref, xcp_ref, cgp_ref, w_ref, wpc_ref, ya_ref):
        i = pl.program_id(0)
        keep = jnp.where(i % seq_tiles == 0, 0.0, 1.0)
        up6 = cgp_ref[6:7, :] * xcp_ref[6:7, :] * keep
        up7 = cgp_ref[7:8, :] * xcp_ref[7:8, :] * keep
        _, _, _, _, _, sz, co = _conv_forward(xc_ref[...], bg_ref[...], cg_ref[...], zc_ref[...], up6, up7, w_ref)
        ya_ref[...] = _dot((sz * co).astype(BF16), wpc_ref[...])

    def prev(col):
        return pl.BlockSpec((8, D), lambda i: (jnp.maximum(i * (tm // 8) - 1, 0), col))

    return pl.pallas_call(
        body, name="fwd_conv", grid=(t // tm,),
        in_specs=[_row_spec(tm, D, 0), _row_spec(tm, D, 1), _row_spec(tm, D, 2), _row_spec(tm, D, 3),
                  prev(0), prev(2), pl.BlockSpec((8, D), lambda i: (0, 0)), _whole_vmem()],
        out_specs=_row_spec(tm, D),
        out_shape=jax.ShapeDtypeStruct((t, D), F32),
        compiler_params=_params("parallel"),
    )(a4, a4, a4, a4, a4, a4, wconv8, wpc)


def _band_mask(first):
    qi = lax.broadcasted_iota(jnp.int32, (BLK, 2 * BLK), 0)
    kj = lax.broadcasted_iota(jnp.int32, (BLK, 2 * BLK), 1)
    return (kj > qi) & (kj <= qi + BLK) & (kj >= jnp.where(first, BLK, 0))


def _padded_pair(kvp_ref, kvc_ref, col):
    z = jnp.concatenate([kvp_ref[:, col:col + 128], kvc_ref[:, col:col + 128]], axis=0).astype(F32)
    zs = pltpu.roll(z, 64, 1)
    lane = lax.broadcasted_iota(jnp.int32, z.shape, 1)
    lo = lane < 64
    zero = jnp.zeros_like(z)
    left = [jnp.where(lo, z, zero).astype(BF16), jnp.where(lo, zs, zero).astype(BF16)]
    right = [jnp.where(lo, zero, zs).astype(BF16), jnp.where(lo, zero, z).astype(BF16)]
    return left, right


def _softmax_with_sink(s, valid, sink):
    s = jnp.where(valid, s * (HEAD_DIM ** -0.5), NEG)
    m = jnp.maximum(jnp.max(s, axis=-1, keepdims=True), sink)
    p = jnp.exp(s - m)
    e_sink = jnp.exp(sink - m)
    denom = jnp.sum(p, axis=-1, keepdims=True) + e_sink
    return p / denom, e_sink / denom


def _fwd_attn(sinks, q, kv, g3):
    t = q.shape[0]
    seq_blocks = SEQ_LEN // BLK

    def body(sink_ref, q_ref, kvc_ref, kvp_ref, za_ref, attn_ref, ub_ref):
        i = pl.program_id(0)
        valid = _band_mask(i % seq_blocks == 0)
        k_left, k_right = _padded_pair(kvp_ref, kvc_ref, 0)
        v_left, v_right = _padded_pair(kvp_ref, kvc_ref, 128)
        for j in range(N_HEADS // 2):
            g = j // 4
            qp = q_ref[:, j * 128:(j + 1) * 128]
            p0, _ = _softmax_with_sink(_dot_nt(qp, k_left[g]), valid, sink_ref[0, 2 * j])
            p1, _ = _softmax_with_sink(_dot_nt(qp, k_right[g]), valid, sink_ref[0, 2 * j + 1])
            o = _dot(p0.astype(BF16), v_left[g]) + _dot(p1.astype(BF16), v_right[g])
            attn_ref[:, j * 128:(j + 1) * 128] = o
            za = za_ref[:, j * 128:(j + 1) * 128]
            ub_ref[:, j * 128:(j + 1) * 128] = (za * _sig(za) * o).astype(BF16)

    return pl.pallas_call(
        body, name="fwd_attn", grid=(t // BLK,),
        in_specs=[pl.BlockSpec(memory_space=pltpu.SMEM), _row_spec(BLK, D), _row_spec(BLK, 256),
                  pl.BlockSpec((BLK, 256), lambda i: (jnp.maximum(i - 1, 0), 0)), _row_spec(BLK, D, 0)],
        out_specs=[_row_spec(BLK, D), _row_spec(BLK, D)],
        out_shape=[jax.ShapeDtypeStruct((t, D), F32), jax.ShapeDtypeStruct((t, D), BF16)],
        compiler_params=_params("parallel"),
    )(sinks, q, kv, kv, g3)


def _fwd_out_bwd_head(ya, ub, g3, x, target, g_post, wpa, wout, tm):
    t = x.shape[0]

    def body(ya_ref, ub_ref, ga_ref, gb_ref, x_ref, tgt_ref, gp_ref, wpa_ref, wout_ref,
             loss_ref, dout_ref, dya_ref, dub_ref, dgab_ref, dwout_ref, dwpa_ref, dgp_ref):
        @pl.when(pl.program_id(0) == 0)
        def _():
            loss_ref[...] = jnp.zeros_like(loss_ref)
            dwout_ref[...] = jnp.zeros_like(dwout_ref)
            dwpa_ref[...] = jnp.zeros_like(dwpa_ref)
            dgp_ref[...] = jnp.zeros_like(dgp_ref)

        ub = ub_ref[...]
        ya = ya_ref[...]
        yb = _dot(ub, wpa_ref[...])
        sa = _sig(ga_ref[...])
        sb = _sig(gb_ref[...])
        mb = (sa * ya + sb * yb).astype(BF16)
        y = _dot(mb, wout_ref[...])
        r = lax.rsqrt(jnp.mean(y * y, axis=-1, keepdims=True) + RMS_EPS)
        n = y * r
        g = gp_ref[...]
        err = (x_ref[...] + n * g) - tgt_ref[...]
        sq = jnp.sum(jnp.sum(err * err, axis=0, keepdims=True), axis=1, keepdims=True)
        loss_ref[...] += sq * (0.5 / D)
        dout = err * (1.0 / D)
        dout_ref[...] = dout
        dgp_ref[0:1, :] += jnp.sum(dout * n, axis=0, keepdims=True)
        dn = dout * g
        dy = (r * (dn - n * jnp.mean(dn * n, axis=-1, keepdims=True))).astype(BF16)
        dwout_ref[...] += _dot_tn(mb, dy)
        dm = _dot_nt(dy, wout_ref[...])
        dya_ref[...] = (dm * sa).astype(BF16)
        dyb = (dm * sb).astype(BF16)
        dgab_ref[:, 0:D] = (dm * ya * (sa * (1.0 - sa))).astype(BF16)
        dgab_ref[:, D:2 * D] = (dm * yb * (sb * (1.0 - sb))).astype(BF16)
        dwpa_ref[...] += _dot_tn(ub, dyb)
        dub_ref[...] = _dot_nt(dyb, wpa_ref[...])

    return pl.pallas_call(
        body, name="fwd_out_bwd_head", grid=(t // tm,),
        in_specs=[_row_spec(tm, D), _row_spec(tm, D), _row_spec(tm, D, 1), _row_spec(tm, D, 2),
                  _row_spec(tm, D), _row_spec(tm, D), pl.BlockSpec((1, D), lambda i: (0, 0)),
                  _whole_vmem(), _whole_vmem()],
        out_specs=[pl.BlockSpec((8, 128), lambda i: (0, 0)), _row_spec(tm, D), _row_spec(tm, D), _row_spec(tm, D),
                   _row_spec(tm, 2 * D), _whole_vmem(), _whole_vmem(), pl.BlockSpec((8, D), lambda i: (0, 0))],
        out_shape=[jax.ShapeDtypeStruct((8, 128), F32), jax.ShapeDtypeStruct((t, D), F32),
                   jax.ShapeDtypeStruct((t, D), BF16), jax.ShapeDtypeStruct((t, D), F32),
                   jax.ShapeDtypeStruct((t, 2 * D), BF16), jax.ShapeDtypeStruct((D, D), F32),
                   jax.ShapeDtypeStruct((D, D), F32), jax.ShapeDtypeStruct((8, D), F32)],
        compiler_params=_params("arbitrary"),
    )(ya, ub, g3, g3, x, target, g_post, wpa, wout)


def _bwd_attn(sinks, q, kv, attn, dub, g3, cos_t, sin_t):
    t = q.shape[0]
    seq_blocks = SEQ_LEN // BLK

    def body(sink_ref, q_ref, kvc_ref, kvp_ref, attn_ref, dub_ref, za_ref, c_ref, s_ref,
             dq_ref, dza_ref, dkv_own_ref, dkv_prev_ref, dsink_ref):
        i = pl.program_id(0)

        @pl.when(i == 0)
        def _():
            dsink_ref[...] = jnp.zeros_like(dsink_ref)

        valid = _band_mask(i % seq_blocks == 0)
        k_left, k_right = _padded_pair(kvp_ref, kvc_ref, 0)
        v_left, v_right = _padded_pair(kvp_ref, kvc_ref, 128)
        lane = lax.broadcasted_iota(jnp.int32, (BLK, 128), 1)
        lo = lane < 64
        lane8 = lax.broadcasted_iota(jnp.int32, (8, 128), 1)
        c = c_ref[...]
        s = s_ref[...]
        dk_acc = [jnp.zeros((2 * BLK, 128), F32), jnp.zeros((2 * BLK, 128), F32)]
        dv_acc = [jnp.zeros((2 * BLK, 128), F32), jnp.zeros((2 * BLK, 128), F32)]
        dsink = jnp.zeros((8, 128), F32)
        for j in range(N_HEADS // 2):
            g = j // 4
            cols = slice(j * 128, (j + 1) * 128)
            qp = q_ref[:, cols]
            za = za_ref[:, cols]
            sg = _sig(za)
            dub = dub_ref[:, cols]
            dza_ref[:, cols] = (dub * attn_ref[:, cols] * (sg * (1.0 + za * (1.0 - sg)))).astype(BF16)
            do = (dub * (za * sg)).astype(BF16)
            dq = jnp.zeros((BLK, 128), F32)
            ds_pair, p_pair = [], []
            for e, (kpad, vpad) in enumerate(((k_left[g], v_left[g]), (k_right[g], v_right[g]))):
                p, p_sink = _softmax_with_sink(_dot_nt(qp, kpad), valid, sink_ref[0, 2 * j + e])
                dp = _dot_nt(do, vpad)
                drow = jnp.sum(p * dp, axis=-1, keepdims=True)
                ds = (p * (dp - drow) * (HEAD_DIM ** -0.5)).astype(BF16)
                dq = dq + _dot(ds, kpad)
                ds_pair.append(ds)
                p_pair.append(p.astype(BF16))
                dsk = jnp.sum(-p_sink * drow, axis=0, keepdims=True)
                dsink = dsink + jnp.where(lane8 == 2 * j + e, dsk, 0.0)
            zero = jnp.zeros_like(qp)
            q2 = jnp.concatenate([jnp.where(lo, qp, zero), jnp.where(lo, zero, qp)], axis=0)
            do2 = jnp.concatenate([jnp.where(lo, do, zero), jnp.where(lo, zero, do)], axis=0)
            dk_acc[g] = dk_acc[g] + _dot_tn(jnp.concatenate(ds_pair, axis=0), q2)
            dv_acc[g] = dv_acc[g] + _dot_tn(jnp.concatenate(p_pair, axis=0), do2)
            dq_ref[:, cols] = (dq * c - _swap_halves(dq) * s).astype(BF16)
        dsink_ref[...] += dsink
        lane2 = lax.broadcasted_iota(jnp.int32, (2 * BLK, 128), 1)
        lo2 = lane2 < 64
        for col, acc in ((0, dk_acc), (128, dv_acc)):
            both = jnp.where(lo2, acc[0] + pltpu.roll(acc[0], 64, 1), acc[1] + pltpu.roll(acc[1], 64, 1))
            dkv_prev_ref[:, col:col + 128] = both[0:BLK, :]
            dkv_own_ref[:, col:col + 128] = both[BLK:2 * BLK, :]

    tab = pl.BlockSpec((BLK, 128), lambda i: (i % seq_blocks, 0))
    return pl.pallas_call(
        body, name="bwd_attn", grid=(t // BLK,),
        in_specs=[pl.BlockSpec(memory_space=pltpu.SMEM), _row_spec(BLK, D), _row_spec(BLK, 256),
                  pl.BlockSpec((BLK, 256), lambda i: (jnp.maximum(i - 1, 0), 0)),
                  _row_spec(BLK, D), _row_spec(BLK, D), _row_spec(BLK, D, 0), tab, tab],
        out_specs=[_row_spec(BLK, D), _row_spec(BLK, D), _row_spec(BLK, 256), _row_spec(BLK, 256),
                   pl.BlockSpec((8, 128), lambda i: (0, 0))],
        out_shape=[jax.ShapeDtypeStruct((t, D), BF16), jax.ShapeDtypeStruct((t, D), BF16),
                   jax.ShapeDtypeStruct((t, 256), F32), jax.ShapeDtypeStruct((t, 256), F32),
                   jax.ShapeDtypeStruct((8, 128), F32)],
        compiler_params=_params("arbitrary"),
    )(sinks, q, kv, kv, attn, dub, g3, cos_t, sin_t)


def _bwd_kv_finish(dkv_own, dkv_prev, cos_t, sin_t):
    t = dkv_own.shape[0]
    seq_blocks = SEQ_LEN // BLK
    n_blocks = t // BLK

    def body(own_ref, nxt_ref, c_ref, s_ref, out_ref):
        keep = jnp.where(pl.program_id(0) % seq_blocks == seq_blocks - 1, 0.0, 1.0)
        tot = own_ref[...] + nxt_ref[...] * keep
        dk = tot[:, 0:128]
        out_ref[:, 0:128] = (dk * c_ref[...] - _swap_halves(dk) * s_ref[...]).astype(BF16)
        out_ref[:, 128:256] = tot[:, 128:256].astype(BF16)

    tab = pl.BlockSpec((BLK, 128), lambda i: (i % seq_blocks, 0))
    return pl.pallas_call(
        body, name="bwd_kv_finish", grid=(n_blocks,),
        in_specs=[_row_spec(BLK, 256), pl.BlockSpec((BLK, 256), lambda i: (jnp.minimum(i + 1, n_blocks - 1), 0)),
                  tab, tab],
        out_specs=_row_spec(BLK, 256),
        out_shape=jax.ShapeDtypeStruct((t, 256), BF16),
        compiler_params=_params("parallel"),
    )(dkv_own, dkv_prev, cos_t, sin_t)


def _bwd_conv(dya, a4, wconv8, wpc, tm):
    t = a4.shape[0]
    seq_tiles = SEQ_LEN // tm
    last8 = t // 8 - 1
    last16 = t // 16 - 1

    def body(dya_ref, xc_ref, bg_ref, cg_ref, zc_ref, xcp_ref, cgp_ref, dyan_ref, bgn_ref, zcn_ref,
             w_ref, wpc_ref, da4_ref, dwpc_ref, dwc_ref):
        i = pl.program_id(0)

        @pl.when(i == 0)
        def _():
            dwpc_ref[...] = jnp.zeros_like(dwpc_ref)
            dwc_ref[...] = jnp.zeros_like(dwc_ref)

        keep_prev = jnp.where(i % seq_tiles == 0, 0.0, 1.0)
        keep_next = jnp.where(i % seq_tiles == seq_tiles - 1, 0.0, 1.0)
        up6 = cgp_ref[6:7, :] * xcp_ref[6:7, :] * keep_prev
        up7 = cgp_ref[7:8, :] * xcp_ref[7:8, :] * keep_prev
        xc = xc_ref[...]
        bg = bg_ref[...]
        cg = cg_ref[...]
        zc = zc_ref[...]
        u, u_m1, u_m2, yconv, sg, sz, co = _conv_forward(xc, bg, cg, zc, up6, up7, w_ref)
        dya = dya_ref[...]
        dwpc_ref[...] += _dot_tn((sz * co).astype(BF16), dya)
        dua = _dot_nt(dya, wpc_ref[...])
        da4_ref[:, 3 * D:4 * D] = (dua * co * (sg * (1.0 + zc * (1.0 - sg)))).astype(BF16)
        dco = dua * sz
        da4_ref[:, D:2 * D] = (dco * yconv).astype(BF16)
        dyc = dco * bg
        dwc_ref[0:1, :] += jnp.sum(dyc * u_m2, axis=0, keepdims=True)
        dwc_ref[1:2, :] += jnp.sum(dyc * u_m1, axis=0, keepdims=True)
        dwc_ref[2:3, :] += jnp.sum(dyc * u, axis=0, keepdims=True)
        zcn = zcn_ref[...]
        dyc_n = _dot_nt(dyan_ref[...], wpc_ref[...])[0:8, :] * (zcn * _sig(zcn)) * bgn_ref[...] * keep_next
        rows = lax.broadcasted_iota(jnp.int32, xc.shape, 0)
        n0 = dyc_n[0:1, :]
        n1 = dyc_n[1:2, :]
        dyc_p1 = jnp.where(rows == tm - 1, n0, pltpu.roll(dyc, tm - 1, 0))
        dyc_p2 = jnp.where(rows == tm - 2, n0, jnp.where(rows == tm - 1, n1, pltpu.roll(dyc, tm - 2, 0)))
        du = w_ref[2:3, :] * dyc + w_ref[1:2, :] * dyc_p1 + w_ref[0:1, :] * dyc_p2
        da4_ref[:, 0:D] = (du * cg).astype(BF16)
        da4_ref[:, 2 * D:3 * D] = (du * xc).astype(BF16)

    def prev(col):
        return pl.BlockSpec((8, D), lambda i: (jnp.maximum(i * (tm // 8) - 1, 0), col))

    def nxt(col):
        return pl.BlockSpec((8, D), lambda i: (jnp.minimum((i + 1) * (tm // 8), last8), col))

    return pl.pallas_call(
        body, name="bwd_conv", grid=(t // tm,),
        in_specs=[_row_spec(tm, D), _row_spec(tm, D, 0), _row_spec(tm, D, 1), _row_spec(tm, D, 2), _row_spec(tm, D, 3),
                  prev(0), prev(2),
                  pl.BlockSpec((16, D), lambda i: (jnp.minimum((i + 1) * (tm // 16), last16), 0)), nxt(1), nxt(3),
                  pl.BlockSpec((8, D), lambda i: (0, 0)), _whole_vmem()],
        out_specs=[_row_spec(tm, 4 * D), _whole_vmem(), pl.BlockSpec((8, D), lambda i: (0, 0))],
        out_shape=[jax.ShapeDtypeStruct((t, 4 * D), BF16), jax.ShapeDtypeStruct((D, D), F32),
                   jax.ShapeDtypeStruct((8, D), F32)],
        compiler_params=_params("arbitrary"),
    )(dya, a4, a4, a4, a4, a4, a4, dya, a4, a4, wconv8, wpc)


def _bwd_dh(da4, dq, dkv, dza, dgab, wa, wb, x, g_pre, dout, tm):
    t = x.shape[0]

    def body(da4_ref, dq_ref, dkv_ref, dza_ref, dgab_ref, wa_ref, wb_ref, x_ref, g_ref, dout_ref, gx_ref, dg_ref):
        @pl.when(pl.program_id(0) == 0)
        def _():
            dg_ref[...] = jnp.zeros_like(dg_ref)

        dh = _dot_nt(da4_ref[...], wa_ref[...])
        dh += _dot_nt(dq_ref[...], wb_ref[:, 0:D])
        dh += _dot_nt(dkv_ref[...], wb_ref[:, D:D + 256])
        dh += _dot_nt(dza_ref[...], wb_ref[:, D + 256:2 * D + 256])
        dh += _dot_nt(dgab_ref[...], wb_ref[:, 2 * D + 256:4 * D + 256])
        xf = x_ref[...]
        r = lax.rsqrt(jnp.mean(xf * xf, axis=-1, keepdims=True) + RMS_EPS)
        xn = xf * r
        dg_ref[0:1, :] += jnp.sum(dh * xn, axis=0, keepdims=True)
        dxn = dh * g_ref[...]
        gx_ref[...] = dout_ref[...] + r * (dxn - xn * jnp.mean(dxn * xn, axis=-1, keepdims=True))

    return pl.pallas_call(
        body, name="bwd_dh", grid=(t // tm,),
        in_specs=[_row_spec(tm, 4 * D), _row_spec(tm, D), _row_spec(tm, 256), _row_spec(tm, D), _row_spec(tm, 2 * D),
                  _whole_vmem(), _whole_vmem(), _row_spec(tm, D), pl.BlockSpec((1, D), lambda i: (0, 0)),
                  _row_spec(tm, D)],
        out_specs=[_row_spec(tm, D), pl.BlockSpec((8, D), lambda i: (0, 0))],
        out_shape=[jax.ShapeDtypeStruct((t, D), F32), jax.ShapeDtypeStruct((8, D), F32)],
        compiler_params=_params("arbitrary"),
    )(da4, dq, dkv, dza, dgab, wa, wb, x, g_pre, dout)


def _bwd_dw_in(h, piece, nb, tm, name):
    t, n = piece.shape

    def body(h_ref, p_ref, o_ref):
        @pl.when(pl.program_id(1) == 0)
        def _():
            o_ref[...] = jnp.zeros_like(o_ref)

        o_ref[...] += _dot_tn(h_ref[...], p_ref[...])

    return pl.pallas_call(
        body, name=name, grid=(n // nb, t // tm),
        in_specs=[pl.BlockSpec((tm, D), lambda j, i: (i, 0)), pl.BlockSpec((tm, nb), lambda j, i: (i, j))],
        out_specs=pl.BlockSpec((D, nb), lambda j, i: (0, j)),
        out_shape=jax.ShapeDtypeStruct((D, n), F32),
        compiler_params=_params("parallel", "arbitrary"),
    )(h, piece)


def _place():
    x, y, c = lax.axis_index("x"), lax.axis_index("y"), lax.axis_index("c")
    return x, y, c, 4 * x + 2 * y + c


def _peer(x, y, c, k):
    return (1 - x if k & 4 else x, 1 - y if k & 2 else y, 1 - c if k & 1 else c)


def _all_gather(shards):
    n = len(shards)

    def body(*refs):
        src, dst = refs[:n], refs[n:2 * n]
        send_sems, recv_sems, local_sems = refs[2 * n:]
        x, y, c, me = _place()
        local = [pltpu.make_async_copy(src[a], dst[a].at[me], local_sems.at[a]) for a in range(n)]
        for cp in local:
            cp.start()
        sends = []
        for k in range(1, N_DEV):
            for a in range(n):
                cp = pltpu.make_async_remote_copy(
                    src_ref=src[a], dst_ref=dst[a].at[me], send_sem=send_sems.at[a * 7 + k - 1],
                    recv_sem=recv_sems.at[a * 7 + k - 1], device_id=_peer(x, y, c, k), device_id_type=MESH_ID)
                cp.start()
                sends.append(cp)
        for k in range(1, N_DEV):
            for a in range(n):
                pltpu.make_async_remote_copy(
                    src_ref=src[a], dst_ref=dst[a].at[me ^ k], send_sem=send_sems.at[a * 7 + k - 1],
                    recv_sem=recv_sems.at[a * 7 + k - 1], device_id=_peer(x, y, c, k), device_id_type=MESH_ID,
                ).wait_recv()
        for cp in sends:
            cp.wait_send()
        for cp in local:
            cp.wait()

    hbm = pl.BlockSpec(memory_space=pl.ANY)
    return pl.pallas_call(
        body, name="all_gather_weights",
        in_specs=[hbm] * n, out_specs=[hbm] * n,
        out_shape=[jax.ShapeDtypeStruct((N_DEV,) + s.shape, s.dtype) for s in shards],
        scratch_shapes=[pltpu.SemaphoreType.DMA((7 * n,)), pltpu.SemaphoreType.DMA((7 * n,)),
                        pltpu.SemaphoreType.DMA((n,))],
    )(*shards)


def _scatter_partials(by_dest, small):
    n = len(by_dest)

    def body(*refs):
        src, small_src = refs[:n], refs[n]
        dst, small_dst = refs[n + 1:2 * n + 1], refs[2 * n + 1]
        send_sems, recv_sems, local_sem = refs[2 * n + 2:]
        x, y, c, me = _place()
        local = pltpu.make_async_copy(small_src, small_dst.at[me], local_sem)
        local.start()
        sends = []
        for k in range(1, N_DEV):
            peer = _peer(x, y, c, k)
            for a in range(n + 1):
                s = src[a].at[me ^ k] if a < n else small_src
                d = dst[a].at[k - 1] if a < n else small_dst.at[me]
                cp = pltpu.make_async_remote_copy(
                    src_ref=s, dst_ref=d, send_sem=send_sems.at[a * 7 + k - 1], recv_sem=recv_sems.at[a * 7 + k - 1],
                    device_id=peer, device_id_type=MESH_ID)
                cp.start()
                sends.append(cp)
        for k in range(1, N_DEV):
            peer = _peer(x, y, c, k)
            for a in range(n + 1):
                s = src[a].at[me ^ k] if a < n else small_src
                d = dst[a].at[k - 1] if a < n else small_dst.at[me ^ k]
                pltpu.make_async_remote_copy(
                    src_ref=s, dst_ref=d, send_sem=send_sems.at[a * 7 + k - 1], recv_sem=recv_sems.at[a * 7 + k - 1],
                    device_id=peer, device_id_type=MESH_ID).wait_recv()
        for cp in sends:
            cp.wait_send()
        local.wait()

    hbm = pl.BlockSpec(memory_space=pl.ANY)
    return pl.pallas_call(
        body, name="scatter_gradients",
        in_specs=[hbm] * (n + 1), out_specs=[hbm] * (n + 1),
        out_shape=[jax.ShapeDtypeStruct((7,) + s.shape[1:], s.dtype) for s in by_dest]
        + [jax.ShapeDtypeStruct((N_DEV,) + small.shape, small.dtype)],
        scratch_shapes=[pltpu.SemaphoreType.DMA((7 * (n + 1),)), pltpu.SemaphoreType.DMA((7 * (n + 1),)),
                        pltpu.SemaphoreType.DMA],
    )(*by_dest, small)


def _adamw_math(w, g, m, v):
    m = ADAM_B1 * m + (1.0 - ADAM_B1) * g
    v = ADAM_B2 * v + (1.0 - ADAM_B2) * (g * g)
    m_hat = m / (1.0 - ADAM_B1 ** ADAM_STEP)
    v_hat = v / (1.0 - ADAM_B2 ** ADAM_STEP)
    return -ADAM_LR * (m_hat / (jnp.sqrt(v_hat) + ADAM_EPS) + ADAM_WD * w), m, v


def _sum_adamw(own, recv, w, m, v, tr, name):
    rows, cols = w.shape

    def body(own_ref, recv_ref, w_ref, m_ref, v_ref, g_ref, d_ref, nm_ref, nv_ref):
        g = own_ref[...]
        for k in range(7):
            g = g + recv_ref[k].astype(F32)
        g_ref[...] = g
        d_ref[...], nm_ref[...], nv_ref[...] = _adamw_math(w_ref[...], g, m_ref[...], v_ref[...])

    blk = pl.BlockSpec((tr, cols), lambda i: (i, 0))
    return pl.pallas_call(
        body, name=name, grid=(rows // tr,),
        in_specs=[blk, pl.BlockSpec((7, tr, cols), lambda i: (0, i, 0)), blk, blk, blk],
        out_specs=[blk] * 4, out_shape=[jax.ShapeDtypeStruct((rows, cols), F32)] * 4,
        compiler_params=_params("parallel"),
    )(own, recv, w, m, v)


def _sum_small(small_all):
    def body(s_ref, o_ref):
        g = s_ref[0]
        for d in range(1, N_DEV):
            g = g + s_ref[d]
        o_ref[...] = g

    return pl.pallas_call(body, name="sum_small", out_shape=jax.ShapeDtypeStruct(small_all.shape[1:], F32))(small_all)


def _adamw_small(w, g, m, v, name):
    def body(w_ref, g_ref, m_ref, v_ref, d_ref, nm_ref, nv_ref):
        d_ref[...], nm_ref[...], nv_ref[...] = _adamw_math(w_ref[...], g_ref[...], m_ref[...], v_ref[...])

    return pl.pallas_call(body, name=name, out_shape=[jax.ShapeDtypeStruct(w.shape, F32)] * 3)(w, g, m, v)


def _rope_tables():
    inv_freq = ROPE_THETA ** (-jnp.arange(0, HEAD_DIM, 2, dtype=F32) / HEAD_DIM)
    ang = jnp.arange(SEQ_LEN).astype(F32)[:, None] * inv_freq[None, :]
    cos, sin = jnp.cos(ang), jnp.sin(ang)
    return jnp.tile(cos, (1, 4)), jnp.tile(jnp.concatenate([-sin, sin], axis=1), (1, 2))


def _local_step(x, target, g_pre, g_post, sinks, wa, wb, wconv, wpc, wpa, wout):
    cos_t, sin_t = _rope_tables()
    wconv8 = jnp.pad(wconv, ((0, 5), (0, 0)))
    h, a4 = _fwd_in_a(x, g_pre, wa, 512)
    q, kv, g3 = _fwd_in_b(h, wb, cos_t, sin_t, 512)
    ya = _fwd_conv(a4, wconv8, wpc, 512)
    attn, ub = _fwd_attn(sinks, q, kv, g3)
    loss8, dout, dya, dub, dgab, dwout, dwpa, dgpost8 = _fwd_out_bwd_head(ya, ub, g3, x, target, g_post, wpa, wout, 256)
    dq, dza, dkv_own, dkv_prev, dsink8 = _bwd_attn(sinks, q, kv, attn, dub, g3, cos_t, sin_t)
    dkv = _bwd_kv_finish(dkv_own, dkv_prev, cos_t, sin_t)
    da4, dwpc, dwconv8 = _bwd_conv(dya, a4, wconv8, wpc, 512)
    grad_x, dgpre8 = _bwd_dh(da4, dq, dkv, dza, dgab, wa, wb, x, g_pre, dout, 256)
    dw_in = jnp.concatenate([
        _bwd_dw_in(h, da4, 1024, 512, "bwd_dw_in_conv"), _bwd_dw_in(h, dq, 1024, 512, "bwd_dw_in_q"),
        _bwd_dw_in(h, dkv, 256, 512, "bwd_dw_in_kv"), _bwd_dw_in(h, dza, 1024, 512, "bwd_dw_in_za"),
        _bwd_dw_in(h, dgab, 1024, 512, "bwd_dw_in_gates")], axis=1)
    small = jnp.concatenate([dgpre8, dgpost8, jnp.pad(dsink8, ((0, 0), (0, D - 128))), dwconv8], axis=0)
    return loss8[0, 0], grad_x, dw_in, dwpc, dwpa, dwout, small


def kernel(x, g_pre, g_post, w_in, w_conv, sinks, w_proj_conv, w_proj_attn, w_out, loss_target, m_g_pre, m_g_post, m_w_in, m_w_conv, m_sinks, m_w_proj_conv, m_w_proj_attn, m_w_out, v_g_pre, v_g_post, v_w_in, v_w_conv, v_sinks, v_w_proj_conv, v_w_proj_attn, v_w_out):
    batch = x.shape[0]
    shard_in = w_in.shape[2]
    _, _, _, me = _place()

    g_in, g_conv, g_pc, g_pa, g_out = _all_gather([
        w_in[0].astype(BF16), jnp.pad(w_conv[0], ((0, 5), (0, 0))), w_proj_conv[0].astype(BF16),
        w_proj_attn[0].astype(BF16), w_out[0].astype(BF16)])
    w_full = g_in.transpose(1, 0, 2).reshape(D, D_IN)
    wa, wb = w_full[:, :4 * D], w_full[:, 4 * D:]
    wconv = g_conv[:, 0:3, :].transpose(1, 0, 2).reshape(3, D)
    wpc, wpa, wout = g_pc.reshape(D, D), g_pa.reshape(D, D), g_out.reshape(D, D)

    loss, grad_x, dw_in, dwpc, dwpa, dwout, small = _local_step(
        x.reshape(batch * SEQ_LEN, D), loss_target.reshape(batch * SEQ_LEN, D), g_pre, g_post, sinks,
        wa, wb, wconv, wpc, wpa, wout)
    loss = lax.psum(loss, ("x", "y", "c"))

    dw_in_by_dest = dw_in.reshape(D, N_DEV, shard_in).transpose(1, 0, 2)
    by_dest = [dw_in_by_dest, dwpc.reshape(N_DEV, D // N_DEV, D), dwpa.reshape(N_DEV, D // N_DEV, D),
               dwout.reshape(N_DEV, D // N_DEV, D)]
    r_in, r_pc, r_pa, r_out, small_all = _scatter_partials([b.astype(BF16) for b in by_dest], small)
    own = [lax.dynamic_index_in_dim(b, me, 0, keepdims=False) for b in by_dest]

    o_in = _sum_adamw(own[0], r_in, w_in[0], m_w_in[0], v_w_in[0], 128, "adamw_w_in")
    o_pc = _sum_adamw(own[1], r_pc, w_proj_conv[0], m_w_proj_conv[0], v_w_proj_conv[0], 128, "adamw_w_proj_conv")
    o_pa = _sum_adamw(own[2], r_pa, w_proj_attn[0], m_w_proj_attn[0], v_w_proj_attn[0], 128, "adamw_w_proj_attn")
    o_out = _sum_adamw(own[3], r_out, w_out[0], m_w_out[0], v_w_out[0], 128, "adamw_w_out")
    gs = _sum_small(small_all)
    g_g_pre, g_g_post, g_sinks = gs[0:1], gs[8:9], gs[16:17, 0:N_HEADS]
    g_conv_mine = lax.dynamic_slice_in_dim(gs[24:27], me * (D // N_DEV), D // N_DEV, axis=1)
    o_gpre = _adamw_small(g_pre, g_g_pre, m_g_pre, v_g_pre, "adamw_g_pre")
    o_gpost = _adamw_small(g_post, g_g_post, m_g_post, v_g_post, "adamw_g_post")
    o_sinks = _adamw_small(sinks, g_sinks, m_sinks, v_sinks, "adamw_sinks")
    o_conv = _adamw_small(w_conv[0], g_conv_mine, m_w_conv[0], v_w_conv[0], "adamw_w_conv")

    grads = [g_g_pre, g_g_post, o_in[0][None], g_conv_mine[None], g_sinks, o_pc[0][None], o_pa[0][None], o_out[0][None]]
    rest = []
    for idx in (0, 1, 2):
        rest += [o_gpre[idx], o_gpost[idx], o_in[idx + 1][None], o_conv[idx][None], o_sinks[idx],
                 o_pc[idx + 1][None], o_pa[idx + 1][None], o_out[idx + 1][None]]
    return (loss, grad_x.reshape(batch, SEQ_LEN, D), *grads, *rest)
```

```python
import functools

import jax
import jax.numpy as jnp
from jax import lax
from jax.experimental import pallas as pl
from jax.experimental.pallas import tpu as pltpu

D = 1024
N_HEADS = 16
HEAD_DIM = 64
BLK = 128
SEQ_LEN = 2048
D_IN = 8448
ROW_Q, ROW_KV, ROW_ZA, ROW_GA = 4 * D, 5 * D, 5 * D + 256, 6 * D + 256
SHARD_IN = D_IN // 8
SHARD_SQ = D // 8
N_DEV = 8
ROPE_THETA = 10000.0
RMS_EPS = 1e-6
NEG = -1e30
ADAM_LR, ADAM_B1, ADAM_B2, ADAM_EPS, ADAM_WD, ADAM_STEP = 0.001, 0.9, 0.999, 1e-08, 0.01, 10

F32 = jnp.float32
BF16 = jnp.bfloat16
MESH_ID = pl.DeviceIdType.MESH


def _dot(a, b):
    return jnp.dot(a, b, preferred_element_type=F32)


def _dot_nt(a, b):
    return lax.dot_general(a, b, (((1,), (1,)), ((), ())), preferred_element_type=F32)


def _dot_tn(a, b):
    return lax.dot_general(a, b, (((0,), (0,)), ((), ())), preferred_element_type=F32)


def _sig(z):
    return 1.0 / (1.0 + jnp.exp(-z))


def _swap_halves(z):
    lane = lax.broadcasted_iota(jnp.int32, z.shape, 1)
    return jnp.where((lane & 63) < 32, pltpu.roll(z, 96, 1), pltpu.roll(z, 32, 1))


def _row_spec(tm, width, col=0):
    return pl.BlockSpec((tm, width), lambda i: (i, col))


def _whole_vmem():
    return pl.BlockSpec(memory_space=pltpu.VMEM)


def _params(*sem):
    return pltpu.CompilerParams(dimension_semantics=sem)


def _fwd_in_a(x, g_pre, wt, tm):
    t = x.shape[0]

    def body(x_ref, g_ref, w_ref, h_ref, a4_ref):
        xf = x_ref[...]
        r = lax.rsqrt(jnp.mean(xf * xf, axis=-1, keepdims=True) + RMS_EPS)
        h = ((xf * r) * g_ref[...]).astype(BF16)
        h_ref[...] = h
        for j in range(4):
            a4_ref[:, j * D:(j + 1) * D] = _dot_nt(h, w_ref[j * D:(j + 1) * D, :])

    return pl.pallas_call(
        body, name="fwd_in_a", grid=(t // tm,),
        in_specs=[_row_spec(tm, D), pl.BlockSpec((1, D), lambda i: (0, 0)), _whole_vmem()],
        out_specs=[_row_spec(tm, D), _row_spec(tm, 4 * D)],
        out_shape=[jax.ShapeDtypeStruct((t, D), BF16), jax.ShapeDtypeStruct((t, 4 * D), F32)],
        compiler_params=_params("parallel"),
    )(x, g_pre, wt)


def _fwd_in_b(h, wt, cos_t, sin_t, tm):
    t = h.shape[0]
    seq_tiles = SEQ_LEN // tm

    def body(h_ref, w_ref, c_ref, s_ref, q_ref, kv_ref, g3_ref):
        hh = h_ref[...]
        c = c_ref[...]
        s = s_ref[...]

        def rope(z):
            return z * c + _swap_halves(z) * s

        q = _dot_nt(hh, w_ref[ROW_Q:ROW_Q + D, :])
        for j in range(D // 128):
            q_ref[:, j * 128:(j + 1) * 128] = rope(q[:, j * 128:(j + 1) * 128]).astype(BF16)
        kv = _dot_nt(hh, w_ref[ROW_KV:ROW_KV + 256, :])
        kv_ref[:, 0:128] = rope(kv[:, 0:128]).astype(BF16)
        kv_ref[:, 128:256] = kv[:, 128:256].astype(BF16)
        for j in range(3):
            g3_ref[:, j * D:(j + 1) * D] = _dot_nt(hh, w_ref[ROW_ZA + j * D:ROW_ZA + (j + 1) * D, :])

    tab = pl.BlockSpec((tm, 128), lambda i: (i % seq_tiles, 0))
    return pl.pallas_call(
        body, name="fwd_in_b", grid=(t // tm,),
        in_specs=[_row_spec(tm, D), _whole_vmem(), tab, tab],
        out_specs=[_row_spec(tm, D), _row_spec(tm, 256), _row_spec(tm, 3 * D)],
        out_shape=[jax.ShapeDtypeStruct((t, D), BF16), jax.ShapeDtypeStruct((t, 256), BF16),
                   jax.ShapeDtypeStruct((t, 3 * D), F32)],
        compiler_params=_params("parallel"),
    )(h, wt, cos_t, sin_t)


def _conv_forward(xc, bg, cg, zc, up6, up7, w_ref):
    tm = xc.shape[0]
    rows = lax.broadcasted_iota(jnp.int32, xc.shape, 0)
    u = cg * xc
    u_m1 = jnp.where(rows == 0, up7, pltpu.roll(u, 1, 0))
    u_m2 = jnp.where(rows == 0, up6, jnp.where(rows == 1, up7, pltpu.roll(u, 2, 0)))
    yconv = w_ref[0:1, :] * u_m2 + w_ref[1:2, :] * u_m1 + w_ref[2:3, :] * u
    sg = _sig(zc)
    sz = zc * sg
    co = bg * yconv
    del tm
    return u, u_m1, u_m2, yconv, sg, sz, co


def _fwd_conv(a4, wconv8, wpc, tm):
    t = a4.shape[0]
    seq_tiles = SEQ_LEN // tm

    def body(xc_ref, bg_ref, cg_ref, zc_ref, xcp_ref, cgp_ref, w_ref, wpc_ref, ya_ref):
        i = pl.program_id(0)
        keep = jnp.where(i % seq_tiles == 0, 0.0, 1.0)
        up6 = cgp_ref[6:7, :] * xcp_ref[6:7, :] * keep
        up7 = cgp_ref[7:8, :] * xcp_ref[7:8, :] * keep
        _, _, _, _, _, sz, co = _conv_forward(xc_ref[...], bg_ref[...], cg_ref[...], zc_ref[...], up6, up7, w_ref)
        ya_ref[...] = _dot((sz * co).astype(BF16), wpc_ref[...])

    def prev(col):
        return pl.BlockSpec((8, D), lambda i: (jnp.maximum(i * (tm // 8) - 1, 0), col))

    return pl.pallas_call(
        body, name="fwd_conv", grid=(t // tm,),
        in_specs=[_row_spec(tm, D, 0), _row_spec(tm, D, 1), _row_spec(tm, D, 2), _row_spec(tm, D, 3),
                  prev(0), prev(2), pl.BlockSpec((8, D), lambda i: (0, 0)), _whole_vmem()],
        out_specs=_row_spec(tm, D),
        out_shape=jax.ShapeDtypeStruct((t, D), F32),
        compiler_params=_params("parallel"),
    )(a4, a4, a4, a4, a4, a4, wconv8, wpc)


def _band_mask(first):
    qi = lax.broadcasted_iota(jnp.int32, (BLK, 2 * BLK), 0)
    kj = lax.broadcasted_iota(jnp.int32, (BLK, 2 * BLK), 1)
    return (kj > qi) & (kj <= qi + BLK) & (kj >= jnp.where(first, BLK, 0))


def _padded_pair(kvp_ref, kvc_ref, col):
    z = jnp.concatenate([kvp_ref[:, col:col + 128], kvc_ref[:, col:col + 128]], axis=0).astype(F32)
    zs = pltpu.roll(z, 64, 1)
    lane = lax.broadcasted_iota(jnp.int32, z.shape, 1)
    lo = lane < 64
    zero = jnp.zeros_like(z)
    left = [jnp.where(lo, z, zero).astype(BF16), jnp.where(lo, zs, zero).astype(BF16)]
    right = [jnp.where(lo, zero, zs).astype(BF16), jnp.where(lo, zero, z).astype(BF16)]
    return left, right


def _softmax_with_sink(s, valid, sink):
    s = jnp.where(valid, s * (HEAD_DIM ** -0.5), NEG)
    m = jnp.maximum(jnp.max(s, axis=-1, keepdims=True), sink)
    p = jnp.exp(s - m)
    e_sink = jnp.exp(sink - m)
    denom = jnp.sum(p, axis=-1, keepdims=True) + e_sink
    return p / denom, e_sink / denom


def _fwd_attn(sinks, q, kv, g3):
    t = q.shape[0]
    seq_blocks = SEQ_LEN // BLK

    def body(sink_ref, q_ref, kvc_ref, kvp_ref, za_ref, attn_ref, ub_ref):
        i = pl.program_id(0)
        valid = _band_mask(i % seq_blocks == 0)
        k_left, k_right = _padded_pair(kvp_ref, kvc_ref, 0)
        v_left, v_right = _padded_pair(kvp_ref, kvc_ref, 128)
        for j in range(N_HEADS // 2):
            g = j // 4
            qp = q_ref[:, j * 128:(j + 1) * 128]
            p0, _ = _softmax_with_sink(_dot_nt(qp, k_left[g]), valid, sink_ref[0, 2 * j])
            p1, _ = _softmax_with_sink(_dot_nt(qp, k_right[g]), valid, sink_ref[0, 2 * j + 1])
            o = _dot(p0.astype(BF16), v_left[g]) + _dot(p1.astype(BF16), v_right[g])
            attn_ref[:, j * 128:(j + 1) * 128] = o
            za = za_ref[:, j * 128:(j + 1) * 128]
            ub_ref[:, j * 128:(j + 1) * 128] = (za * _sig(za) * o).astype(BF16)

    return pl.pallas_call(
        body, name="fwd_attn", grid=(t // BLK,),
        in_specs=[pl.BlockSpec(memory_space=pltpu.SMEM), _row_spec(BLK, D), _row_spec(BLK, 256),
                  pl.BlockSpec((BLK, 256), lambda i: (jnp.maximum(i - 1, 0), 0)), _row_spec(BLK, D, 0)],
        out_specs=[_row_spec(BLK, D), _row_spec(BLK, D)],
        out_shape=[jax.ShapeDtypeStruct((t, D), F32), jax.ShapeDtypeStruct((t, D), BF16)],
        compiler_params=_params("parallel"),
    )(sinks, q, kv, kv, g3)


def _fwd_out_bwd_head(ya, ub, g3, x, target, g_post, wpa, wout, tm):
    t = x.shape[0]

    def body(ya_ref, ub_ref, ga_ref, gb_ref, x_ref, tgt_ref, gp_ref, wpa_ref, wout_ref,
             loss_ref, dout_ref, dya_ref, dub_ref, dgab_ref, dwout_ref, dwpa_ref, dgp_ref):
        @pl.when(pl.program_id(0) == 0)
        def _():
            loss_ref[...] = jnp.zeros_like(loss_ref)
            dwout_ref[...] = jnp.zeros_like(dwout_ref)
            dwpa_ref[...] = jnp.zeros_like(dwpa_ref)
            dgp_ref[...] = jnp.zeros_like(dgp_ref)

        ub = ub_ref[...]
        ya = ya_ref[...]
        yb = _dot(ub, wpa_ref[...])
        sa = _sig(ga_ref[...])
        sb = _sig(gb_ref[...])
        mb = (sa * ya + sb * yb).astype(BF16)
        y = _dot(mb, wout_ref[...])
        r = lax.rsqrt(jnp.mean(y * y, axis=-1, keepdims=True) + RMS_EPS)
        n = y * r
        g = gp_ref[...]
        err = (x_ref[...] + n * g) - tgt_ref[...]
        sq = jnp.sum(jnp.sum(err * err, axis=0, keepdims=True), axis=1, keepdims=True)
        loss_ref[...] += sq * (0.5 / D)
        dout = err * (1.0 / D)
        dout_ref[...] = dout
        dgp_ref[0:1, :] += jnp.sum(dout * n, axis=0, keepdims=True)
        dn = dout * g
        dy = (r * (dn - n * jnp.mean(dn * n, axis=-1, keepdims=True))).astype(BF16)
        dwout_ref[...] += _dot_tn(mb, dy)
        dm = _dot_nt(dy, wout_ref[...])
        dya_ref[...] = (dm * sa).astype(BF16)
        dyb = (dm * sb).astype(BF16)
        dgab_ref[:, 0:D] = (dm * ya * (sa * (1.0 - sa))).astype(BF16)
        dgab_ref[:, D:2 * D] = (dm * yb * (sb * (1.0 - sb))).astype(BF16)
        dwpa_ref[...] += _dot_tn(ub, dyb)
        dub_ref[...] = _dot_nt(dyb, wpa_ref[...])

    return pl.pallas_call(
        body, name="fwd_out_bwd_head", grid=(t // tm,),
        in_specs=[_row_spec(tm, D), _row_spec(tm, D), _row_spec(tm, D, 1), _row_spec(tm, D, 2),
                  _row_spec(tm, D), _row_spec(tm, D), pl.BlockSpec((1, D), lambda i: (0, 0)),
                  _whole_vmem(), _whole_vmem()],
        out_specs=[pl.BlockSpec((8, 128), lambda i: (0, 0)), _row_spec(tm, D), _row_spec(tm, D), _row_spec(tm, D),
                   _row_spec(tm, 2 * D), _whole_vmem(), _whole_vmem(), pl.BlockSpec((8, D), lambda i: (0, 0))],
        out_shape=[jax.ShapeDtypeStruct((8, 128), F32), jax.ShapeDtypeStruct((t, D), F32),
                   jax.ShapeDtypeStruct((t, D), BF16), jax.ShapeDtypeStruct((t, D), F32),
                   jax.ShapeDtypeStruct((t, 2 * D), BF16), jax.ShapeDtypeStruct((D, D), F32),
                   jax.ShapeDtypeStruct((D, D), F32), jax.ShapeDtypeStruct((8, D), F32)],
        compiler_params=_params("arbitrary"),
    )(ya, ub, g3, g3, x, target, g_post, wpa, wout)


def _bwd_attn(sinks, q, kv, attn, dub, g3, cos_t, sin_t):
    t = q.shape[0]
    seq_blocks = SEQ_LEN // BLK

    def body(sink_ref, q_ref, kvc_ref, kvp_ref, attn_ref, dub_ref, za_ref, c_ref, s_ref,
             dq_ref, dza_ref, dkv_own_ref, dkv_prev_ref, dsink_ref):
        i = pl.program_id(0)

        @pl.when(i == 0)
        def _():
            dsink_ref[...] = jnp.zeros_like(dsink_ref)

        valid = _band_mask(i % seq_blocks == 0)
        k_left, k_right = _padded_pair(kvp_ref, kvc_ref, 0)
        v_left, v_right = _padded_pair(kvp_ref, kvc_ref, 128)
        lane = lax.broadcasted_iota(jnp.int32, (BLK, 128), 1)
        lo = lane < 64
        lane8 = lax.broadcasted_iota(jnp.int32, (8, 128), 1)
        c = c_ref[...]
        s = s_ref[...]
        dk_acc = [jnp.zeros((2 * BLK, 128), F32), jnp.zeros((2 * BLK, 128), F32)]
        dv_acc = [jnp.zeros((2 * BLK, 128), F32), jnp.zeros((2 * BLK, 128), F32)]
        dsink = jnp.zeros((8, 128), F32)
        for j in range(N_HEADS // 2):
            g = j // 4
            cols = slice(j * 128, (j + 1) * 128)
            qp = q_ref[:, cols]
            za = za_ref[:, cols]
            sg = _sig(za)
            dub = dub_ref[:, cols]
            dza_ref[:, cols] = (dub * attn_ref[:, cols] * (sg * (1.0 + za * (1.0 - sg)))).astype(BF16)
            do = (dub * (za * sg)).astype(BF16)
            dq = jnp.zeros((BLK, 128), F32)
            ds_pair, p_pair = [], []
            for e, (kpad, vpad) in enumerate(((k_left[g], v_left[g]), (k_right[g], v_right[g]))):
                p, p_sink = _softmax_with_sink(_dot_nt(qp, kpad), valid, sink_ref[0, 2 * j + e])
                dp = _dot_nt(do, vpad)
                drow = jnp.sum(p * dp, axis=-1, keepdims=True)
                ds = (p * (dp - drow) * (HEAD_DIM ** -0.5)).astype(BF16)
                dq = dq + _dot(ds, kpad)
                ds_pair.append(ds)
                p_pair.append(p.astype(BF16))
                dsk = jnp.sum(-p_sink * drow, axis=0, keepdims=True)
                dsink = dsink + jnp.where(lane8 == 2 * j + e, dsk, 0.0)
            zero = jnp.zeros_like(qp)
            q2 = jnp.concatenate([jnp.where(lo, qp, zero), jnp.where(lo, zero, qp)], axis=0)
            do2 = jnp.concatenate([jnp.where(lo, do, zero), jnp.where(lo, zero, do)], axis=0)
            dk_acc[g] = dk_acc[g] + _dot_tn(jnp.concatenate(ds_pair, axis=0), q2)
            dv_acc[g] = dv_acc[g] + _dot_tn(jnp.concatenate(p_pair, axis=0), do2)
            dq_ref[:, cols] = (dq * c - _swap_halves(dq) * s).astype(BF16)
        dsink_ref[...] += dsink
        lane2 = lax.broadcasted_iota(jnp.int32, (2 * BLK, 128), 1)
        lo2 = lane2 < 64
        for col, acc in ((0, dk_acc), (128, dv_acc)):
            both = jnp.where(lo2, acc[0] + pltpu.roll(acc[0], 64, 1), acc[1] + pltpu.roll(acc[1], 64, 1))
            dkv_prev_ref[:, col:col + 128] = both[0:BLK, :]
            dkv_own_ref[:, col:col + 128] = both[BLK:2 * BLK, :]

    tab = pl.BlockSpec((BLK, 128), lambda i: (i % seq_blocks, 0))
    return pl.pallas_call(
        body, name="bwd_attn", grid=(t // BLK,),
        in_specs=[pl.BlockSpec(memory_space=pltpu.SMEM), _row_spec(BLK, D), _row_spec(BLK, 256),
                  pl.BlockSpec((BLK, 256), lambda i: (jnp.maximum(i - 1, 0), 0)),
                  _row_spec(BLK, D), _row_spec(BLK, D), _row_spec(BLK, D, 0), tab, tab],
        out_specs=[_row_spec(BLK, D), _row_spec(BLK, D), _row_spec(BLK, 256), _row_spec(BLK, 256),
                   pl.BlockSpec((8, 128), lambda i: (0, 0))],
        out_shape=[jax.ShapeDtypeStruct((t, D), BF16), jax.ShapeDtypeStruct((t, D), BF16),
                   jax.ShapeDtypeStruct((t, 256), F32), jax.ShapeDtypeStruct((t, 256), F32),
                   jax.ShapeDtypeStruct((8, 128), F32)],
        compiler_params=_params("arbitrary"),
    )(sinks, q, kv, kv, attn, dub, g3, cos_t, sin_t)


def _bwd_kv_finish(dkv_own, dkv_prev, cos_t, sin_t):
    t = dkv_own.shape[0]
    tm = 512
    seq_tiles = SEQ_LEN // tm
    n_blocks = t // BLK

    def body(own_ref, same_ref, nxt_ref, c_ref, s_ref, out_ref):
        keep = jnp.where(pl.program_id(0) % seq_tiles == seq_tiles - 1, 0.0, 1.0)
        shifted = jnp.concatenate([same_ref[BLK:tm, :], nxt_ref[...] * keep], axis=0)
        tot = own_ref[...] + shifted
        dk = tot[:, 0:128]
        out_ref[:, 0:128] = (dk * c_ref[...] - _swap_halves(dk) * s_ref[...]).astype(BF16)
        out_ref[:, 128:256] = tot[:, 128:256].astype(BF16)

    tab = pl.BlockSpec((tm, 128), lambda i: (i % seq_tiles, 0))
    return pl.pallas_call(
        body, name="bwd_kv_finish", grid=(t // tm,),
        in_specs=[_row_spec(tm, 256), _row_spec(tm, 256),
                  pl.BlockSpec((BLK, 256), lambda i: (jnp.minimum((i + 1) * (tm // BLK), n_blocks - 1), 0)), tab, tab],
        out_specs=_row_spec(tm, 256),
        out_shape=jax.ShapeDtypeStruct((t, 256), BF16),
        compiler_params=_params("parallel"),
    )(dkv_own, dkv_prev, dkv_prev, cos_t, sin_t)


def _bwd_conv(dya, a4, wconv8, wpc, tm):
    t = a4.shape[0]
    seq_tiles = SEQ_LEN // tm
    last8 = t // 8 - 1
    last16 = t // 16 - 1

    def body(dya_ref, xc_ref, bg_ref, cg_ref, zc_ref, xcp_ref, cgp_ref, dyan_ref, bgn_ref, zcn_ref,
             w_ref, wpc_ref, da4_ref, dwpc_ref, dwc_ref):
        i = pl.program_id(0)

        @pl.when(i == 0)
        def _():
            dwpc_ref[...] = jnp.zeros_like(dwpc_ref)
            dwc_ref[...] = jnp.zeros_like(dwc_ref)

        keep_prev = jnp.where(i % seq_tiles == 0, 0.0, 1.0)
        keep_next = jnp.where(i % seq_tiles == seq_tiles - 1, 0.0, 1.0)
        up6 = cgp_ref[6:7, :] * xcp_ref[6:7, :] * keep_prev
        up7 = cgp_ref[7:8, :] * xcp_ref[7:8, :] * keep_prev
        xc = xc_ref[...]
        bg = bg_ref[...]
        cg = cg_ref[...]
        zc = zc_ref[...]
        u, u_m1, u_m2, yconv, sg, sz, co = _conv_forward(xc, bg, cg, zc, up6, up7, w_ref)
        dya = dya_ref[...]
        dwpc_ref[...] += _dot_tn((sz * co).astype(BF16), dya)
        dua = _dot_nt(dya, wpc_ref[...])
        da4_ref[:, 3 * D:4 * D] = (dua * co * (sg * (1.0 + zc * (1.0 - sg)))).astype(BF16)
        dco = dua * sz
        da4_ref[:, D:2 * D] = (dco * yconv).astype(BF16)
        dyc = dco * bg
        dwc_ref[0:1, :] += jnp.sum(dyc * u_m2, axis=0, keepdims=True)
        dwc_ref[1:2, :] += jnp.sum(dyc * u_m1, axis=0, keepdims=True)
        dwc_ref[2:3, :] += jnp.sum(dyc * u, axis=0, keepdims=True)
        zcn = zcn_ref[...]
        dyc_n = _dot_nt(dyan_ref[...], wpc_ref[...])[0:8, :] * (zcn * _sig(zcn)) * bgn_ref[...] * keep_next
        rows = lax.broadcasted_iota(jnp.int32, xc.shape, 0)
        n0 = dyc_n[0:1, :]
        n1 = dyc_n[1:2, :]
        dyc_p1 = jnp.where(rows == tm - 1, n0, pltpu.roll(dyc, tm - 1, 0))
        dyc_p2 = jnp.where(rows == tm - 2, n0, jnp.where(rows == tm - 1, n1, pltpu.roll(dyc, tm - 2, 0)))
        du = w_ref[2:3, :] * dyc + w_ref[1:2, :] * dyc_p1 + w_ref[0:1, :] * dyc_p2
        da4_ref[:, 0:D] = (du * cg).astype(BF16)
        da4_ref[:, 2 * D:3 * D] = (du * xc).astype(BF16)

    def prev(col):
        return pl.BlockSpec((8, D), lambda i: (jnp.maximum(i * (tm // 8) - 1, 0), col))

    def nxt(col):
        return pl.BlockSpec((8, D), lambda i: (jnp.minimum((i + 1) * (tm // 8), last8), col))

    return pl.pallas_call(
        body, name="bwd_conv", grid=(t // tm,),
        in_specs=[_row_spec(tm, D), _row_spec(tm, D, 0), _row_spec(tm, D, 1), _row_spec(tm, D, 2), _row_spec(tm, D, 3),
                  prev(0), prev(2),
                  pl.BlockSpec((16, D), lambda i: (jnp.minimum((i + 1) * (tm // 16), last16), 0)), nxt(1), nxt(3),
                  pl.BlockSpec((8, D), lambda i: (0, 0)), _whole_vmem()],
        out_specs=[_row_spec(tm, 4 * D), _whole_vmem(), pl.BlockSpec((8, D), lambda i: (0, 0))],
        out_shape=[jax.ShapeDtypeStruct((t, 4 * D), BF16), jax.ShapeDtypeStruct((D, D), F32),
                   jax.ShapeDtypeStruct((8, D), F32)],
        compiler_params=_params("arbitrary"),
    )(dya, a4, a4, a4, a4, a4, a4, dya, a4, a4, wconv8, wpc)


def _bwd_dh(da4, dq, dkv, dza, dgab, wt, x, g_pre, dout, tm):
    t = x.shape[0]

    def body(da4_ref, dq_ref, dkv_ref, dza_ref, dgab_ref, w_ref, x_ref, g_ref, dout_ref, gx_ref, dg_ref):
        @pl.when(pl.program_id(0) == 0)
        def _():
            dg_ref[...] = jnp.zeros_like(dg_ref)

        dh = _dot(da4_ref[...], w_ref[0:ROW_Q, :])
        dh += _dot(dq_ref[...], w_ref[ROW_Q:ROW_KV, :])
        dh += _dot(dkv_ref[...], w_ref[ROW_KV:ROW_ZA, :])
        dh += _dot(dza_ref[...], w_ref[ROW_ZA:ROW_GA, :])
        dh += _dot(dgab_ref[...], w_ref[ROW_GA:D_IN, :])
        xf = x_ref[...]
        r = lax.rsqrt(jnp.mean(xf * xf, axis=-1, keepdims=True) + RMS_EPS)
        xn = xf * r
        dg_ref[0:1, :] += jnp.sum(dh * xn, axis=0, keepdims=True)
        dxn = dh * g_ref[...]
        gx_ref[...] = dout_ref[...] + r * (dxn - xn * jnp.mean(dxn * xn, axis=-1, keepdims=True))

    return pl.pallas_call(
        body, name="bwd_dh", grid=(t // tm,),
        in_specs=[_row_spec(tm, 4 * D), _row_spec(tm, D), _row_spec(tm, 256), _row_spec(tm, D), _row_spec(tm, 2 * D),
                  _whole_vmem(), _row_spec(tm, D), pl.BlockSpec((1, D), lambda i: (0, 0)), _row_spec(tm, D)],
        out_specs=[_row_spec(tm, D), pl.BlockSpec((8, D), lambda i: (0, 0))],
        out_shape=[jax.ShapeDtypeStruct((t, D), F32), jax.ShapeDtypeStruct((8, D), F32)],
        compiler_params=_params("arbitrary"),
    )(da4, dq, dkv, dza, dgab, wt, x, g_pre, dout)


def _bwd_dw_in(h, piece, row0, nb, tm, name, prev):
    t, n = piece.shape
    n_t = t // tm

    def body(*refs):
        h_ref, p_ref = refs[0], refs[1]
        o32_ref, o16_ref, acc_ref, acc16_ref, sems = refs[-5:]
        j, i = pl.program_id(0), pl.program_id(1)

        @pl.when(i == 0)
        def _():
            acc_ref[...] = jnp.zeros_like(acc_ref)

        acc_ref[...] += _dot_tn(p_ref[...], h_ref[...])

        @pl.when(i == n_t - 1)
        def _():
            acc16_ref[...] = acc_ref[...].astype(BF16)
            rows = pl.ds(pl.multiple_of(row0 + j * nb, 16), nb)
            c32 = pltpu.make_async_copy(acc_ref, o32_ref.at[rows], sems.at[0])
            c16 = pltpu.make_async_copy(acc16_ref, o16_ref.at[rows], sems.at[1])
            c32.start()
            c16.start()
            c32.wait()
            c16.wait()

    hbm = pl.BlockSpec(memory_space=pl.ANY)
    carried = [] if prev is None else list(prev)
    return pl.pallas_call(
        body, name=name, grid=(n // nb, n_t),
        in_specs=[pl.BlockSpec((tm, D), lambda j, i: (i, 0)), pl.BlockSpec((tm, nb), lambda j, i: (i, j))]
        + [hbm] * len(carried),
        out_specs=[hbm, hbm],
        out_shape=[jax.ShapeDtypeStruct((D_IN, D), F32), jax.ShapeDtypeStruct((D_IN, D), BF16)],
        scratch_shapes=[pltpu.VMEM((nb, D), F32), pltpu.VMEM((nb, D), BF16), pltpu.SemaphoreType.DMA((2,))],
        input_output_aliases={2: 0, 3: 1} if carried else {},
        compiler_params=_params("arbitrary", "arbitrary"),
    )(h, piece, *carried)


def _place():
    x, y, c = lax.axis_index("x"), lax.axis_index("y"), lax.axis_index("c")
    return x, y, c, 4 * x + 2 * y + c


def _peer(x, y, c, k):
    return (1 - x if k & 4 else x, 1 - y if k & 2 else y, 1 - c if k & 1 else c)


ICI_MASKS = (4, 2, 6)


def _all_gather(shards):
    n = len(shards)

    def body(*refs):
        src, dst = refs[:n], refs[n:2 * n]
        send_sems, recv_sems, local_sems = refs[2 * n:]
        x, y, c, me = _place()
        sibling = _peer(x, y, c, 1)

        def copy(a, s, block, to, own=False):
            return pltpu.make_async_remote_copy(
                src_ref=src[a] if own else dst[a].at[block], dst_ref=dst[a].at[block],
                send_sem=send_sems.at[a * 7 + s], recv_sem=recv_sems.at[a * 7 + s], device_id=to, device_id_type=MESH_ID)

        local = [pltpu.make_async_copy(src[a], dst[a].at[me], local_sems.at[a]) for a in range(n)]
        for cp in local:
            cp.start()
        started = [copy(a, 0, me, sibling, own=True) for a in range(n)]
        started += [copy(a, 1 + j, me, _peer(x, y, c, k), own=True) for j, k in enumerate(ICI_MASKS) for a in range(n)]
        for cp in started:
            cp.start()
        for j, k in enumerate(ICI_MASKS):
            for a in range(n):
                copy(a, 1 + j, me ^ k, sibling).wait_recv()
                fwd = copy(a, 4 + j, me ^ k, sibling)
                fwd.start()
                started.append(fwd)
        for a in range(n):
            copy(a, 0, me ^ 1, sibling).wait_recv()
        for j, k in enumerate(ICI_MASKS):
            for a in range(n):
                copy(a, 4 + j, me ^ 1 ^ k, sibling).wait_recv()
        for cp in started:
            cp.wait_send()
        for cp in local:
            cp.wait()

    hbm = pl.BlockSpec(memory_space=pl.ANY)
    return pl.pallas_call(
        body, name="all_gather_weights",
        in_specs=[hbm] * n, out_specs=[hbm] * n,
        out_shape=[jax.ShapeDtypeStruct((N_DEV,) + s.shape, s.dtype) for s in shards],
        scratch_shapes=[pltpu.SemaphoreType.DMA((7 * n,)), pltpu.SemaphoreType.DMA((7 * n,)),
                        pltpu.SemaphoreType.DMA((n,))],
    )(*shards)


def _exchange_sibling(by_dest):
    n = len(by_dest)

    def body(*refs):
        src, dst = refs[:n], refs[n:2 * n]
        send_sems, recv_sems = refs[2 * n:]
        x, y, c, _ = _place()
        sibling = _peer(x, y, c, 1)
        copies = [pltpu.make_async_remote_copy(
            src_ref=src[a].at[2 * p + (1 - c)], dst_ref=dst[a].at[p], send_sem=send_sems.at[a * 4 + p],
            recv_sem=recv_sems.at[a * 4 + p], device_id=sibling, device_id_type=MESH_ID)
            for a in range(n) for p in range(4)]
        for cp in copies:
            cp.start()
        for cp in copies:
            cp.wait_recv()
        for cp in copies:
            cp.wait_send()

    hbm = pl.BlockSpec(memory_space=pl.ANY)
    return pl.pallas_call(
        body, name="exchange_sibling", in_specs=[hbm] * n, out_specs=[hbm] * n,
        out_shape=[jax.ShapeDtypeStruct((4,) + s.shape[1:], s.dtype) for s in by_dest],
        scratch_shapes=[pltpu.SemaphoreType.DMA((4 * n,)), pltpu.SemaphoreType.DMA((4 * n,))],
    )(*by_dest)


def _exchange_chips(by_chip, small):
    n = len(by_chip)

    def body(*refs):
        src, small_src = refs[:n], refs[n]
        dst, small_dst = refs[n + 1:2 * n + 1], refs[2 * n + 1]
        send_sems, recv_sems, local_sem = refs[2 * n + 2:]
        x, y, c, me = _place()
        chip = 2 * x + y
        local = pltpu.make_async_copy(small_src, small_dst.at[me], local_sem)
        local.start()
        copies = [pltpu.make_async_remote_copy(
            src_ref=src[a].at[chip ^ (k >> 1)], dst_ref=dst[a].at[j], send_sem=send_sems.at[a * 3 + j],
            recv_sem=recv_sems.at[a * 3 + j], device_id=_peer(x, y, c, k), device_id_type=MESH_ID)
            for j, k in enumerate(ICI_MASKS) for a in range(n)]
        for cp in copies:
            cp.start()
        to_all = [pltpu.make_async_remote_copy(
            src_ref=small_src, dst_ref=small_dst.at[me], send_sem=send_sems.at[3 * n + k - 1],
            recv_sem=recv_sems.at[3 * n + k - 1], device_id=_peer(x, y, c, k), device_id_type=MESH_ID)
            for k in range(1, N_DEV)]
        for cp in to_all:
            cp.start()
        for cp in copies:
            cp.wait_recv()
        for k in range(1, N_DEV):
            pltpu.make_async_remote_copy(
                src_ref=small_src, dst_ref=small_dst.at[me ^ k], send_sem=send_sems.at[3 * n + k - 1],
                recv_sem=recv_sems.at[3 * n + k - 1], device_id=_peer(x, y, c, k), device_id_type=MESH_ID).wait_recv()
        for cp in copies + to_all:
            cp.wait_send()
        local.wait()

    hbm = pl.BlockSpec(memory_space=pl.ANY)
    return pl.pallas_call(
        body, name="exchange_chips", in_specs=[hbm] * (n + 1), out_specs=[hbm] * (n + 1),
        out_shape=[jax.ShapeDtypeStruct((3,) + s.shape[1:], s.dtype) for s in by_chip]
        + [jax.ShapeDtypeStruct((N_DEV,) + small.shape, small.dtype)],
        scratch_shapes=[pltpu.SemaphoreType.DMA((3 * n + 7,)), pltpu.SemaphoreType.DMA((3 * n + 7,)),
                        pltpu.SemaphoreType.DMA],
    )(*by_chip, small)


def _adamw_math(w, g, m, v):
    m = ADAM_B1 * m + (1.0 - ADAM_B1) * g
    v = ADAM_B2 * v + (1.0 - ADAM_B2) * (g * g)
    m_hat = m / (1.0 - ADAM_B1 ** ADAM_STEP)
    v_hat = v / (1.0 - ADAM_B2 ** ADAM_STEP)
    return -ADAM_LR * (m_hat / (jnp.sqrt(v_hat) + ADAM_EPS) + ADAM_WD * w), m, v


def _pair_sum(own, recv, c_arr, tr, name):
    _, rows, cols = own.shape

    def body(c_ref, own_ref, recv_ref, o32_ref, o16_ref):
        s = own_ref[...] + recv_ref[...].astype(F32)
        o32_ref[...] = s
        o16_ref[...] = s.astype(BF16)

    by_chip = pl.BlockSpec((None, tr, cols), lambda p, i, c_ref: (p, i, 0))
    return pl.pallas_call(
        body, name=name,
        grid_spec=pltpu.PrefetchScalarGridSpec(
            num_scalar_prefetch=1, grid=(4, rows // tr),
            in_specs=[pl.BlockSpec((None, tr, cols), lambda p, i, c_ref: (2 * p + c_ref[0], i, 0)), by_chip],
            out_specs=[by_chip, by_chip]),
        out_shape=[jax.ShapeDtypeStruct((4, rows, cols), F32), jax.ShapeDtypeStruct((4, rows, cols), BF16)],
        compiler_params=_params("parallel", "parallel"),
    )(c_arr, own, recv)


def _chip_sum(pair, recv, chip_arr, tr, name):
    _, rows, cols = pair.shape

    def body(chip_ref, pair_ref, recv_ref, g_ref):
        g = pair_ref[...]
        for j in range(3):
            g = g + recv_ref[j].astype(F32)
        g_ref[...] = g

    return pl.pallas_call(
        body, name=name,
        grid_spec=pltpu.PrefetchScalarGridSpec(
            num_scalar_prefetch=1, grid=(rows // tr,),
            in_specs=[pl.BlockSpec((None, tr, cols), lambda i, chip_ref: (chip_ref[0], i, 0)),
                      pl.BlockSpec((3, tr, cols), lambda i, chip_ref: (0, i, 0))],
            out_specs=pl.BlockSpec((tr, cols), lambda i, chip_ref: (i, 0))),
        out_shape=jax.ShapeDtypeStruct((rows, cols), F32),
        compiler_params=_params("parallel"),
    )(chip_arr, pair, recv)


def _adamw(w, g, m, v, tr, name):
    rows, cols = w.shape

    def body(w_ref, g_ref, m_ref, v_ref, d_ref, nm_ref, nv_ref):
        d_ref[...], nm_ref[...], nv_ref[...] = _adamw_math(w_ref[...], g_ref[...], m_ref[...], v_ref[...])

    blk = pl.BlockSpec((tr, cols), lambda i: (i, 0))
    return pl.pallas_call(
        body, name=name, grid=(rows // tr,), in_specs=[blk] * 4, out_specs=[blk] * 3,
        out_shape=[jax.ShapeDtypeStruct((rows, cols), F32)] * 3, compiler_params=_params("parallel"),
    )(w, g, m, v)


def _sum_small(small_all):
    def body(s_ref, o_ref):
        g = s_ref[0]
        for d in range(1, N_DEV):
            g = g + s_ref[d]
        o_ref[...] = g

    return pl.pallas_call(body, name="sum_small", out_shape=jax.ShapeDtypeStruct(small_all.shape[1:], F32))(small_all)


def _adamw_small(w, g, m, v, name):
    def body(w_ref, g_ref, m_ref, v_ref, d_ref, nm_ref, nv_ref):
        d_ref[...], nm_ref[...], nv_ref[...] = _adamw_math(w_ref[...], g_ref[...], m_ref[...], v_ref[...])

    return pl.pallas_call(body, name=name, out_shape=[jax.ShapeDtypeStruct(w.shape, F32)] * 3)(w, g, m, v)


def _rope_tables():
    inv_freq = ROPE_THETA ** (-jnp.arange(0, HEAD_DIM, 2, dtype=F32) / HEAD_DIM)
    ang = jnp.arange(SEQ_LEN).astype(F32)[:, None] * inv_freq[None, :]
    cos, sin = jnp.cos(ang), jnp.sin(ang)
    return jnp.tile(cos, (1, 4)), jnp.tile(jnp.concatenate([-sin, sin], axis=1), (1, 2))


def _local_step(x, target, g_pre, g_post, sinks, wt, wconv, wpc, wpa, wout):
    cos_t, sin_t = _rope_tables()
    wconv8 = jnp.pad(wconv, ((0, 5), (0, 0)))
    h, a4 = _fwd_in_a(x, g_pre, wt, 512)
    q, kv, g3 = _fwd_in_b(h, wt, cos_t, sin_t, 512)
    ya = _fwd_conv(a4, wconv8, wpc, 512)
    attn, ub = _fwd_attn(sinks, q, kv, g3)
    loss8, dout, dya, dub, dgab, dwout, dwpa, dgpost8 = _fwd_out_bwd_head(ya, ub, g3, x, target, g_post, wpa, wout, 256)
    dq, dza, dkv_own, dkv_prev, dsink8 = _bwd_attn(sinks, q, kv, attn, dub, g3, cos_t, sin_t)
    dkv = _bwd_kv_finish(dkv_own, dkv_prev, cos_t, sin_t)
    da4, dwpc, dwconv8 = _bwd_conv(dya, a4, wconv8, wpc, 512)
    grad_x, dgpre8 = _bwd_dh(da4, dq, dkv, dza, dgab, wt, x, g_pre, dout, 256)
    dwt = _bwd_dw_in(h, da4, 0, 1024, 1024, "bwd_dw_in_conv", None)
    dwt = _bwd_dw_in(h, dq, ROW_Q, 1024, 1024, "bwd_dw_in_q", dwt)
    dwt = _bwd_dw_in(h, dkv, ROW_KV, 256, 1024, "bwd_dw_in_kv", dwt)
    dwt = _bwd_dw_in(h, dza, ROW_ZA, 1024, 1024, "bwd_dw_in_za", dwt)
    dwt32, dwt16 = _bwd_dw_in(h, dgab, ROW_GA, 1024, 1024, "bwd_dw_in_gates", dwt)
    small = jnp.concatenate([dgpre8, dgpost8, jnp.pad(dsink8, ((0, 0), (0, D - 128))), dwconv8], axis=0)
    return loss8[0, 0], grad_x, dwt32, dwt16, dwpc, dwpa, dwout, small


def kernel(x, g_pre, g_post, w_in, w_conv, sinks, w_proj_conv, w_proj_attn, w_out, loss_target, m_g_pre, m_g_post, m_w_in, m_w_conv, m_sinks, m_w_proj_conv, m_w_proj_attn, m_w_out, v_g_pre, v_g_post, v_w_in, v_w_conv, v_sinks, v_w_proj_conv, v_w_proj_attn, v_w_out):
    batch = x.shape[0]
    mx, my, mc, me = _place()
    c_arr = jnp.reshape(mc, (1,)).astype(jnp.int32)
    chip_arr = jnp.reshape(2 * mx + my, (1,)).astype(jnp.int32)

    g_wt, g_conv, g_pc, g_pa, g_out = _all_gather([
        w_in[0].T.astype(BF16), jnp.pad(w_conv[0], ((0, 5), (0, 0))), w_proj_conv[0].astype(BF16),
        w_proj_attn[0].astype(BF16), w_out[0].astype(BF16)])
    wt = g_wt.reshape(D_IN, D)
    wconv = g_conv[:, 0:3, :].transpose(1, 0, 2).reshape(3, D)
    wpc, wpa, wout = g_pc.reshape(D, D), g_pa.reshape(D, D), g_out.reshape(D, D)

    loss, grad_x, dwt32, dwt16, dwpc, dwpa, dwout, small = _local_step(
        x.reshape(batch * SEQ_LEN, D), loss_target.reshape(batch * SEQ_LEN, D), g_pre, g_post, sinks,
        wt, wconv, wpc, wpa, wout)
    loss = lax.psum(loss, ("x", "y", "c"))

    own = [dwt32.reshape(N_DEV, SHARD_IN, D)] + [g.reshape(N_DEV, SHARD_SQ, D) for g in (dwpc, dwpa, dwout)]
    sent = [dwt16.reshape(N_DEV, SHARD_IN, D)] + [g.astype(BF16) for g in own[1:]]
    from_sibling = _exchange_sibling(sent)
    names = ("w_in", "w_proj_conv", "w_proj_attn", "w_out")
    tiles = (SHARD_IN // 2, SHARD_SQ, SHARD_SQ, SHARD_SQ)
    pairs = [_pair_sum(own[a], from_sibling[a], c_arr, tiles[a], "pair_sum_" + names[a]) for a in range(4)]
    *from_chips, small_all = _exchange_chips([p[1] for p in pairs], small)
    g_wt_mine, g_pc_mine, g_pa_mine, g_out_mine = [
        _chip_sum(pairs[a][0], from_chips[a], chip_arr, tiles[a], "chip_sum_" + names[a]) for a in range(4)]
    g_in_mine = g_wt_mine.T
    gs = _sum_small(small_all)
    g_g_pre, g_g_post, g_sinks = gs[0:1], gs[8:9], gs[16:17, 0:N_HEADS]
    g_conv_mine = lax.dynamic_slice_in_dim(gs[24:27], me * SHARD_SQ, SHARD_SQ, axis=1)

    o_in = _adamw(w_in[0], g_in_mine, m_w_in[0], v_w_in[0], 128, "adamw_w_in")
    o_pc = _adamw(w_proj_conv[0], g_pc_mine, m_w_proj_conv[0], v_w_proj_conv[0], SHARD_SQ, "adamw_w_proj_conv")
    o_pa = _adamw(w_proj_attn[0], g_pa_mine, m_w_proj_attn[0], v_w_proj_attn[0], SHARD_SQ, "adamw_w_proj_attn")
    o_out = _adamw(w_out[0], g_out_mine, m_w_out[0], v_w_out[0], SHARD_SQ, "adamw_w_out")
    o_gpre = _adamw(g_pre, g_g_pre, m_g_pre, v_g_pre, 1, "adamw_g_pre")
    o_gpost = _adamw(g_post, g_g_post, m_g_post, v_g_post, 1, "adamw_g_post")
    o_sinks = _adamw(sinks, g_sinks, m_sinks, v_sinks, 1, "adamw_sinks")
    o_conv = _adamw(w_conv[0], g_conv_mine, m_w_conv[0], v_w_conv[0], 3, "adamw_w_conv")

    grads = [g_g_pre, g_g_post, g_in_mine[None], g_conv_mine[None], g_sinks, g_pc_mine[None], g_pa_mine[None],
             g_out_mine[None]]
    rest = []
    for idx in (0, 1, 2):
        rest += [o_gpre[idx], o_gpost[idx], o_in[idx][None], o_conv[idx][None], o_sinks[idx],
                 o_pc[idx][None], o_pa[idx][None], o_out[idx][None]]
    return (loss, grad_x.reshape(batch, SEQ_LEN, D), *grads, *rest)
```

```python
import functools

import jax
import jax.numpy as jnp
from jax import lax
from jax.experimental import pallas as pl
from jax.experimental.pallas import tpu as pltpu

D = 1024
N_HEADS = 16
HEAD_DIM = 64
BLK = 128
SEQ_LEN = 2048
D_IN = 8448
ROW_Q, ROW_KV, ROW_ZA, ROW_GA = 4 * D, 5 * D, 5 * D + 256, 6 * D + 256
SHARD_IN = D_IN // 8
SHARD_SQ = D // 8
N_DEV = 8
ROPE_THETA = 10000.0
RMS_EPS = 1e-6
NEG = -1e30
ADAM_LR, ADAM_B1, ADAM_B2, ADAM_EPS, ADAM_WD, ADAM_STEP = 0.001, 0.9, 0.999, 1e-08, 0.01, 10

F32 = jnp.float32
BF16 = jnp.bfloat16
MESH_ID = pl.DeviceIdType.MESH


def _dot(a, b):
    return jnp.dot(a, b, preferred_element_type=F32)


def _dot_nt(a, b):
    return lax.dot_general(a, b, (((1,), (1,)), ((), ())), preferred_element_type=F32)


def _dot_tn(a, b):
    return lax.dot_general(a, b, (((0,), (0,)), ((), ())), preferred_element_type=F32)


def _sig(z):
    return 1.0 / (1.0 + jnp.exp(-z))


def _swap_halves(z):
    lane = lax.broadcasted_iota(jnp.int32, z.shape, 1)
    return jnp.where((lane & 63) < 32, pltpu.roll(z, 96, 1), pltpu.roll(z, 32, 1))


def _row_spec(tm, width, col=0):
    return pl.BlockSpec((tm, width), lambda i: (i, col))


def _whole_vmem():
    return pl.BlockSpec(memory_space=pltpu.VMEM)


def _params(*sem):
    return pltpu.CompilerParams(dimension_semantics=sem)


def _fwd_in_a(x, g_pre, wt, tm):
    t = x.shape[0]

    def body(x_ref, g_ref, w_ref, h_ref, a4_ref):
        xf = x_ref[...]
        r = lax.rsqrt(jnp.mean(xf * xf, axis=-1, keepdims=True) + RMS_EPS)
        h = ((xf * r) * g_ref[...]).astype(BF16)
        h_ref[...] = h
        for j in range(4):
            a4_ref[:, j * D:(j + 1) * D] = _dot_nt(h, w_ref[j * D:(j + 1) * D, :])

    return pl.pallas_call(
        body, name="fwd_in_a", grid=(t // tm,),
        in_specs=[_row_spec(tm, D), pl.BlockSpec((1, D), lambda i: (0, 0)), _whole_vmem()],
        out_specs=[_row_spec(tm, D), _row_spec(tm, 4 * D)],
        out_shape=[jax.ShapeDtypeStruct((t, D), BF16), jax.ShapeDtypeStruct((t, 4 * D), F32)],
        compiler_params=_params("parallel"),
    )(x, g_pre, wt)


def _fwd_in_b(h, wt, cos_t, sin_t, tm):
    t = h.shape[0]
    seq_tiles = SEQ_LEN // tm

    def body(h_ref, w_ref, c_ref, s_ref, q_ref, kv_ref, g3_ref):
        hh = h_ref[...]
        c = c_ref[...]
        s = s_ref[...]

        def rope(z):
            return z * c + _swap_halves(z) * s

        q = _dot_nt(hh, w_ref[ROW_Q:ROW_Q + D, :])
        for j in range(D // 128):
            q_ref[:, j * 128:(j + 1) * 128] = rope(q[:, j * 128:(j + 1) * 128]).astype(BF16)
        kv = _dot_nt(hh, w_ref[ROW_KV:ROW_KV + 256, :])
        kv_ref[:, 0:128] = rope(kv[:, 0:128]).astype(BF16)
        kv_ref[:, 128:256] = kv[:, 128:256].astype(BF16)
        for j in range(3):
            g3_ref[:, j * D:(j + 1) * D] = _dot_nt(hh, w_ref[ROW_ZA + j * D:ROW_ZA + (j + 1) * D, :])

    tab = pl.BlockSpec((tm, 128), lambda i: (i % seq_tiles, 0))
    return pl.pallas_call(
        body, name="fwd_in_b", grid=(t // tm,),
        in_specs=[_row_spec(tm, D), _whole_vmem(), tab, tab],
        out_specs=[_row_spec(tm, D), _row_spec(tm, 256), _row_spec(tm, 3 * D)],
        out_shape=[jax.ShapeDtypeStruct((t, D), BF16), jax.ShapeDtypeStruct((t, 256), BF16),
                   jax.ShapeDtypeStruct((t, 3 * D), F32)],
        compiler_params=_params("parallel"),
    )(h, wt, cos_t, sin_t)


def _conv_forward(xc, bg, cg, zc, up6, up7, w_ref):
    tm = xc.shape[0]
    rows = lax.broadcasted_iota(jnp.int32, xc.shape, 0)
    u = cg * xc
    u_m1 = jnp.where(rows == 0, up7, pltpu.roll(u, 1, 0))
    u_m2 = jnp.where(rows == 0, up6, jnp.where(rows == 1, up7, pltpu.roll(u, 2, 0)))
    yconv = w_ref[0:1, :] * u_m2 + w_ref[1:2, :] * u_m1 + w_ref[2:3, :] * u
    sg = _sig(zc)
    sz = zc * sg
    co = bg * yconv
    del tm
    return u, u_m1, u_m2, yconv, sg, sz, co


def _fwd_conv(a4, wconv8, wpc, tm):
    t = a4.shape[0]
    seq_tiles = SEQ_LEN // tm

    def body(xc_ref, bg_ref, cg_ref, zc_ref, xcp_ref, cgp_ref, w_ref, wpc_ref, ya_ref):
        i = pl.program_id(0)
        keep = jnp.where(i % seq_tiles == 0, 0.0, 1.0)
        up6 = cgp_ref[6:7, :] * xcp_ref[6:7, :] * keep
        up7 = cgp_ref[7:8, :] * xcp_ref[7:8, :] * keep
        _, _, _, _, _, sz, co = _conv_forward(xc_ref[...], bg_ref[...], cg_ref[...], zc_ref[...], up6, up7, w_ref)
        ya_ref[...] = _dot((sz * co).astype(BF16), wpc_ref[...])

    def prev(col):
        return pl.BlockSpec((8, D), lambda i: (jnp.maximum(i * (tm // 8) - 1, 0), col))

    return pl.pallas_call(
        body, name="fwd_conv", grid=(t // tm,),
        in_specs=[_row_spec(tm, D, 0), _row_spec(tm, D, 1), _row_spec(tm, D, 2), _row_spec(tm, D, 3),
                  prev(0), prev(2), pl.BlockSpec((8, D), lambda i: (0, 0)), _whole_vmem()],
        out_specs=_row_spec(tm, D),
        out_shape=jax.ShapeDtypeStruct((t, D), F32),
        compiler_params=_params("parallel"),
    )(a4, a4, a4, a4, a4, a4, wconv8, wpc)


STACK = 4 * BLK


def _band_mask(first):
    qi = lax.broadcasted_iota(jnp.int32, (STACK, 2 * BLK), 0) & (BLK - 1)
    kj = lax.broadcasted_iota(jnp.int32, (STACK, 2 * BLK), 1)
    return (kj > qi) & (kj <= qi + BLK) & (kj >= jnp.where(first, BLK, 0))


def _masked_fill(sink_ref, g, e):
    kj = lax.broadcasted_iota(jnp.int32, (STACK, 2 * BLK), 1)
    sink = jnp.concatenate([jnp.full((BLK, 2 * BLK), sink_ref[0, 2 * (4 * g + jj) + e], F32) for jj in range(4)], axis=0)
    return jnp.where(kj == 0, sink, NEG)


def _padded_pair(kvp_ref, kvc_ref, col, other=0.0):
    z = jnp.concatenate([kvp_ref[:, col:col + 128], kvc_ref[:, col:col + 128]], axis=0).astype(F32)
    z = jnp.where(lax.broadcasted_iota(jnp.int32, z.shape, 0) == 0, 0.0, z)
    zs = pltpu.roll(z, 64, 1)
    lo = lax.broadcasted_iota(jnp.int32, z.shape, 1) < 64
    fill = jnp.full_like(z, other)
    left = [jnp.where(lo, z, fill).astype(BF16), jnp.where(lo, zs, fill).astype(BF16)]
    right = [jnp.where(lo, fill, zs).astype(BF16), jnp.where(lo, fill, z).astype(BF16)]
    return left, right


def _exp_logits(s, valid, fill):
    s = jnp.where(valid, s * (HEAD_DIM ** -0.5), fill)
    return jnp.exp(s - jnp.max(s, axis=-1, keepdims=True))


def _fwd_attn(sinks, q, kv, g3):
    t = q.shape[0]
    seq_blocks = SEQ_LEN // BLK

    def body(sink_ref, q_ref, kvc_ref, kvp_ref, za_ref, attn_ref, ub_ref):
        i = pl.program_id(0)
        valid = _band_mask(i % seq_blocks == 0)
        k_pad = _padded_pair(kvp_ref, kvc_ref, 0)
        v_one = _padded_pair(kvp_ref, kvc_ref, 128, other=1.0)
        lo = lax.broadcasted_iota(jnp.int32, (STACK, 128), 1) < 64
        for g in range(2):
            qg = jnp.concatenate([q_ref[:, j * 128:(j + 1) * 128] for j in range(4 * g, 4 * g + 4)], axis=0)
            pv = [_dot(_exp_logits(_dot_nt(qg, k_pad[e][g]), valid, _masked_fill(sink_ref, g, e)).astype(BF16),
                       v_one[e][g]) for e in range(2)]
            o = jnp.where(lo, pv[0], pv[1]) / pltpu.roll(jnp.where(lo, pv[1], pv[0]), 64, 1)
            for jj in range(4):
                cols = slice((4 * g + jj) * 128, (4 * g + jj + 1) * 128)
                oj = o[jj * BLK:(jj + 1) * BLK, :]
                attn_ref[:, cols] = oj
                za = za_ref[:, cols]
                ub_ref[:, cols] = (za * _sig(za) * oj).astype(BF16)

    return pl.pallas_call(
        body, name="fwd_attn", grid=(t // BLK,),
        in_specs=[pl.BlockSpec(memory_space=pltpu.SMEM), _row_spec(BLK, D), _row_spec(BLK, 256),
                  pl.BlockSpec((BLK, 256), lambda i: (jnp.maximum(i - 1, 0), 0)), _row_spec(BLK, D, 0)],
        out_specs=[_row_spec(BLK, D), _row_spec(BLK, D)],
        out_shape=[jax.ShapeDtypeStruct((t, D), F32), jax.ShapeDtypeStruct((t, D), BF16)],
        compiler_params=_params("parallel"),
    )(sinks, q, kv, kv, g3)


def _fwd_out_bwd_head(ya, ub, g3, x, target, g_post, wpa, wout, tm):
    t = x.shape[0]

    def body(ya_ref, ub_ref, ga_ref, gb_ref, x_ref, tgt_ref, gp_ref, wpa_ref, wout_ref,
             loss_ref, dout_ref, dya_ref, dub_ref, dgab_ref, dwout_ref, dwpa_ref, dgp_ref):
        @pl.when(pl.program_id(0) == 0)
        def _():
            loss_ref[...] = jnp.zeros_like(loss_ref)
            dwout_ref[...] = jnp.zeros_like(dwout_ref)
            dwpa_ref[...] = jnp.zeros_like(dwpa_ref)
            dgp_ref[...] = jnp.zeros_like(dgp_ref)

        ub = ub_ref[...]
        ya = ya_ref[...]
        yb = _dot(ub, wpa_ref[...])
        sa = _sig(ga_ref[...])
        sb = _sig(gb_ref[...])
        mb = (sa * ya + sb * yb).astype(BF16)
        y = _dot(mb, wout_ref[...])
        r = lax.rsqrt(jnp.mean(y * y, axis=-1, keepdims=True) + RMS_EPS)
        n = y * r
        g = gp_ref[...]
        err = (x_ref[...] + n * g) - tgt_ref[...]
        sq = jnp.sum(jnp.sum(err * err, axis=0, keepdims=True), axis=1, keepdims=True)
        loss_ref[...] += sq * (0.5 / D)
        dout = err * (1.0 / D)
        dout_ref[...] = dout
        dgp_ref[0:1, :] += jnp.sum(dout * n, axis=0, keepdims=True)
        dn = dout * g
        dy = (r * (dn - n * jnp.mean(dn * n, axis=-1, keepdims=True))).astype(BF16)
        dwout_ref[...] += _dot_tn(mb, dy)
        dm = _dot_nt(dy, wout_ref[...])
        dya_ref[...] = (dm * sa).astype(BF16)
        dyb = (dm * sb).astype(BF16)
        dgab_ref[:, 0:D] = (dm * ya * (sa * (1.0 - sa))).astype(BF16)
        dgab_ref[:, D:2 * D] = (dm * yb * (sb * (1.0 - sb))).astype(BF16)
        dwpa_ref[...] += _dot_tn(ub, dyb)
        dub_ref[...] = _dot_nt(dyb, wpa_ref[...])

    return pl.pallas_call(
        body, name="fwd_out_bwd_head", grid=(t // tm,),
        in_specs=[_row_spec(tm, D), _row_spec(tm, D), _row_spec(tm, D, 1), _row_spec(tm, D, 2),
                  _row_spec(tm, D), _row_spec(tm, D), pl.BlockSpec((1, D), lambda i: (0, 0)),
                  _whole_vmem(), _whole_vmem()],
        out_specs=[pl.BlockSpec((8, 128), lambda i: (0, 0)), _row_spec(tm, D), _row_spec(tm, D), _row_spec(tm, D),
                   _row_spec(tm, 2 * D), _whole_vmem(), _whole_vmem(), pl.BlockSpec((8, D), lambda i: (0, 0))],
        out_shape=[jax.ShapeDtypeStruct((8, 128), F32), jax.ShapeDtypeStruct((t, D), F32),
                   jax.ShapeDtypeStruct((t, D), BF16), jax.ShapeDtypeStruct((t, D), F32),
                   jax.ShapeDtypeStruct((t, 2 * D), BF16), jax.ShapeDtypeStruct((D, D), F32),
                   jax.ShapeDtypeStruct((D, D), F32), jax.ShapeDtypeStruct((8, D), F32)],
        compiler_params=_params("arbitrary"),
    )(ya, ub, g3, g3, x, target, g_post, wpa, wout)


def _bwd_attn(sinks, q, kv, attn, dub, g3, cos_t, sin_t):
    t = q.shape[0]
    seq_blocks = SEQ_LEN // BLK

    def body(sink_ref, q_ref, kvc_ref, kvp_ref, attn_ref, dub_ref, za_ref, c_ref, s_ref,
             dq_ref, dza_ref, dkv_own_ref, dkv_prev_ref, dsink_ref):
        i = pl.program_id(0)

        @pl.when(i == 0)
        def _():
            dsink_ref[...] = jnp.zeros_like(dsink_ref)

        valid = _band_mask(i % seq_blocks == 0)
        k_pad = _padded_pair(kvp_ref, kvc_ref, 0)
        v_pad = _padded_pair(kvp_ref, kvc_ref, 128)
        lo = lax.broadcasted_iota(jnp.int32, (STACK, 128), 1) < 64
        lane8 = lax.broadcasted_iota(jnp.int32, (8, 128), 1)
        c = c_ref[...]
        s = s_ref[...]
        dk_acc, dv_acc = [], []
        dsink = jnp.zeros((8, 128), F32)
        for g in range(2):
            qg, dog = [], []
            for j in range(4 * g, 4 * g + 4):
                cols = slice(j * 128, (j + 1) * 128)
                za = za_ref[:, cols]
                sg = _sig(za)
                dub = dub_ref[:, cols]
                dza_ref[:, cols] = (dub * attn_ref[:, cols] * (sg * (1.0 + za * (1.0 - sg)))).astype(BF16)
                dog.append((dub * (za * sg)).astype(BF16))
                qg.append(q_ref[:, cols])
            qg = jnp.concatenate(qg, axis=0)
            dog = jnp.concatenate(dog, axis=0)
            dq = jnp.zeros((STACK, 128), F32)
            ds_both, p_both = [], []
            for e in range(2):
                p = _exp_logits(_dot_nt(qg, k_pad[e][g]), valid, _masked_fill(sink_ref, g, e))
                p = p / jnp.sum(p, axis=-1, keepdims=True)
                dp = _dot_nt(dog, v_pad[e][g])
                ds = p * (dp - jnp.sum(p * dp, axis=-1, keepdims=True))
                for jj in range(4):
                    tot = jnp.sum(ds[jj * BLK:(jj + 1) * BLK, 0:1], axis=0, keepdims=True)
                    dsink = dsink + jnp.where(lane8 == 2 * (4 * g + jj) + e, tot, 0.0)
                ds = (ds * (HEAD_DIM ** -0.5)).astype(BF16)
                dq = dq + _dot(ds, k_pad[e][g])
                ds_both.append(ds)
                p_both.append(p.astype(BF16))
            zero = jnp.zeros_like(qg)
            q2 = jnp.concatenate([jnp.where(lo, qg, zero), jnp.where(lo, zero, qg)], axis=0)
            do2 = jnp.concatenate([jnp.where(lo, dog, zero), jnp.where(lo, zero, dog)], axis=0)
            dk_acc.append(_dot_tn(jnp.concatenate(ds_both, axis=0), q2))
            dv_acc.append(_dot_tn(jnp.concatenate(p_both, axis=0), do2))
            for jj in range(4):
                cols = slice((4 * g + jj) * 128, (4 * g + jj + 1) * 128)
                dqj = dq[jj * BLK:(jj + 1) * BLK, :]
                dq_ref[:, cols] = (dqj * c - _swap_halves(dqj) * s).astype(BF16)
        dsink_ref[...] += dsink
        lo2 = lax.broadcasted_iota(jnp.int32, (2 * BLK, 128), 1) < 64
        sink_row = lax.broadcasted_iota(jnp.int32, (2 * BLK, 128), 0) == 0
        for col, acc in ((0, dk_acc), (128, dv_acc)):
            both = jnp.where(lo2, acc[0] + pltpu.roll(acc[0], 64, 1), acc[1] + pltpu.roll(acc[1], 64, 1))
            both = jnp.where(sink_row, 0.0, both)
            dkv_prev_ref[:, col:col + 128] = both[0:BLK, :]
            dkv_own_ref[:, col:col + 128] = both[BLK:2 * BLK, :]

    tab = pl.BlockSpec((BLK, 128), lambda i: (i % seq_blocks, 0))
    return pl.pallas_call(
        body, name="bwd_attn", grid=(t // BLK,),
        in_specs=[pl.BlockSpec(memory_space=pltpu.SMEM), _row_spec(BLK, D), _row_spec(BLK, 256),
                  pl.BlockSpec((BLK, 256), lambda i: (jnp.maximum(i - 1, 0), 0)),
                  _row_spec(BLK, D), _row_spec(BLK, D), _row_spec(BLK, D, 0), tab, tab],
        out_specs=[_row_spec(BLK, D), _row_spec(BLK, D), _row_spec(BLK, 256), _row_spec(BLK, 256),
                   pl.BlockSpec((8, 128), lambda i: (0, 0))],
        out_shape=[jax.ShapeDtypeStruct((t, D), BF16), jax.ShapeDtypeStruct((t, D), BF16),
                   jax.ShapeDtypeStruct((t, 256), F32), jax.ShapeDtypeStruct((t, 256), F32),
                   jax.ShapeDtypeStruct((8, 128), F32)],
        compiler_params=_params("arbitrary"),
    )(sinks, q, kv, kv, attn, dub, g3, cos_t, sin_t)


def _bwd_kv_finish(dkv_own, dkv_prev, cos_t, sin_t):
    t = dkv_own.shape[0]
    tm = 512
    seq_tiles = SEQ_LEN // tm
    n_blocks = t // BLK

    def body(own_ref, same_ref, nxt_ref, c_ref, s_ref, out_ref):
        keep = jnp.where(pl.program_id(0) % seq_tiles == seq_tiles - 1, 0.0, 1.0)
        shifted = jnp.concatenate([same_ref[BLK:tm, :], nxt_ref[...] * keep], axis=0)
        tot = own_ref[...] + shifted
        dk = tot[:, 0:128]
        out_ref[:, 0:128] = (dk * c_ref[...] - _swap_halves(dk) * s_ref[...]).astype(BF16)
        out_ref[:, 128:256] = tot[:, 128:256].astype(BF16)

    tab = pl.BlockSpec((tm, 128), lambda i: (i % seq_tiles, 0))
    return pl.pallas_call(
        body, name="bwd_kv_finish", grid=(t // tm,),
        in_specs=[_row_spec(tm, 256), _row_spec(tm, 256),
                  pl.BlockSpec((BLK, 256), lambda i: (jnp.minimum((i + 1) * (tm // BLK), n_blocks - 1), 0)), tab, tab],
        out_specs=_row_spec(tm, 256),
        out_shape=jax.ShapeDtypeStruct((t, 256), BF16),
        compiler_params=_params("parallel"),
    )(dkv_own, dkv_prev, dkv_prev, cos_t, sin_t)


def _bwd_conv(dya, a4, wconv8, wpc, tm):
    t = a4.shape[0]
    seq_tiles = SEQ_LEN // tm
    last8 = t // 8 - 1
    last16 = t // 16 - 1

    def body(dya_ref, xc_ref, bg_ref, cg_ref, zc_ref, xcp_ref, cgp_ref, dyan_ref, bgn_ref, zcn_ref,
             w_ref, wpc_ref, da4_ref, dwpc_ref, dwc_ref):
        i = pl.program_id(0)

        @pl.when(i == 0)
        def _():
            dwpc_ref[...] = jnp.zeros_like(dwpc_ref)
            dwc_ref[...] = jnp.zeros_like(dwc_ref)

        keep_prev = jnp.where(i % seq_tiles == 0, 0.0, 1.0)
        keep_next = jnp.where(i % seq_tiles == seq_tiles - 1, 0.0, 1.0)
        up6 = cgp_ref[6:7, :] * xcp_ref[6:7, :] * keep_prev
        up7 = cgp_ref[7:8, :] * xcp_ref[7:8, :] * keep_prev
        xc = xc_ref[...]
        bg = bg_ref[...]
        cg = cg_ref[...]
        zc = zc_ref[...]
        u, u_m1, u_m2, yconv, sg, sz, co = _conv_forward(xc, bg, cg, zc, up6, up7, w_ref)
        dya = dya_ref[...]
        dwpc_ref[...] += _dot_tn((sz * co).astype(BF16), dya)
        dua = _dot_nt(dya, wpc_ref[...])
        da4_ref[:, 3 * D:4 * D] = (dua * co * (sg * (1.0 + zc * (1.0 - sg)))).astype(BF16)
        dco = dua * sz
        da4_ref[:, D:2 * D] = (dco * yconv).astype(BF16)
        dyc = dco * bg
        dwc_ref[0:1, :] += jnp.sum(dyc * u_m2, axis=0, keepdims=True)
        dwc_ref[1:2, :] += jnp.sum(dyc * u_m1, axis=0, keepdims=True)
        dwc_ref[2:3, :] += jnp.sum(dyc * u, axis=0, keepdims=True)
        zcn = zcn_ref[...]
        dyc_n = _dot_nt(dyan_ref[...], wpc_ref[...])[0:8, :] * (zcn * _sig(zcn)) * bgn_ref[...] * keep_next
        rows = lax.broadcasted_iota(jnp.int32, xc.shape, 0)
        n0 = dyc_n[0:1, :]
        n1 = dyc_n[1:2, :]
        dyc_p1 = jnp.where(rows == tm - 1, n0, pltpu.roll(dyc, tm - 1, 0))
        dyc_p2 = jnp.where(rows == tm - 2, n0, jnp.where(rows == tm - 1, n1, pltpu.roll(dyc, tm - 2, 0)))
        du = w_ref[2:3, :] * dyc + w_ref[1:2, :] * dyc_p1 + w_ref[0:1, :] * dyc_p2
        da4_ref[:, 0:D] = (du * cg).astype(BF16)
        da4_ref[:, 2 * D:3 * D] = (du * xc).astype(BF16)

    def prev(col):
        return pl.BlockSpec((8, D), lambda i: (jnp.maximum(i * (tm // 8) - 1, 0), col))

    def nxt(col):
        return pl.BlockSpec((8, D), lambda i: (jnp.minimum((i + 1) * (tm // 8), last8), col))

    return pl.pallas_call(
        body, name="bwd_conv", grid=(t // tm,),
        in_specs=[_row_spec(tm, D), _row_spec(tm, D, 0), _row_spec(tm, D, 1), _row_spec(tm, D, 2), _row_spec(tm, D, 3),
                  prev(0), prev(2),
                  pl.BlockSpec((16, D), lambda i: (jnp.minimum((i + 1) * (tm // 16), last16), 0)), nxt(1), nxt(3),
                  pl.BlockSpec((8, D), lambda i: (0, 0)), _whole_vmem()],
        out_specs=[_row_spec(tm, 4 * D), _whole_vmem(), pl.BlockSpec((8, D), lambda i: (0, 0))],
        out_shape=[jax.ShapeDtypeStruct((t, 4 * D), BF16), jax.ShapeDtypeStruct((D, D), F32),
                   jax.ShapeDtypeStruct((8, D), F32)],
        compiler_params=_params("arbitrary"),
    )(dya, a4, a4, a4, a4, a4, a4, dya, a4, a4, wconv8, wpc)


def _bwd_dh(da4, dq, dkv, dza, dgab, wt, x, g_pre, dout, tm):
    t = x.shape[0]

    def body(da4_ref, dq_ref, dkv_ref, dza_ref, dgab_ref, w_ref, x_ref, g_ref, dout_ref, gx_ref, dg_ref):
        @pl.when(pl.program_id(0) == 0)
        def _():
            dg_ref[...] = jnp.zeros_like(dg_ref)

        dh = _dot(da4_ref[...], w_ref[0:ROW_Q, :])
        dh += _dot(dq_ref[...], w_ref[ROW_Q:ROW_KV, :])
        dh += _dot(dkv_ref[...], w_ref[ROW_KV:ROW_ZA, :])
        dh += _dot(dza_ref[...], w_ref[ROW_ZA:ROW_GA, :])
        dh += _dot(dgab_ref[...], w_ref[ROW_GA:D_IN, :])
        xf = x_ref[...]
        r = lax.rsqrt(jnp.mean(xf * xf, axis=-1, keepdims=True) + RMS_EPS)
        xn = xf * r
        dg_ref[0:1, :] += jnp.sum(dh * xn, axis=0, keepdims=True)
        dxn = dh * g_ref[...]
        gx_ref[...] = dout_ref[...] + r * (dxn - xn * jnp.mean(dxn * xn, axis=-1, keepdims=True))

    return pl.pallas_call(
        body, name="bwd_dh", grid=(t // tm,),
        in_specs=[_row_spec(tm, 4 * D), _row_spec(tm, D), _row_spec(tm, 256), _row_spec(tm, D), _row_spec(tm, 2 * D),
                  _whole_vmem(), _row_spec(tm, D), pl.BlockSpec((1, D), lambda i: (0, 0)), _row_spec(tm, D)],
        out_specs=[_row_spec(tm, D), pl.BlockSpec((8, D), lambda i: (0, 0))],
        out_shape=[jax.ShapeDtypeStruct((t, D), F32), jax.ShapeDtypeStruct((8, D), F32)],
        compiler_params=_params("arbitrary"),
    )(da4, dq, dkv, dza, dgab, wt, x, g_pre, dout)


def _bwd_dw_in(h, piece, row0, nb, tm, name, prev):
    t, n = piece.shape
    n_t = t // tm

    def body(*refs):
        h_ref, p_ref = refs[0], refs[1]
        o32_ref, o16_ref, acc_ref, acc16_ref, sems = refs[-5:]
        j, i = pl.program_id(0), pl.program_id(1)

        @pl.when(i == 0)
        def _():
            acc_ref[...] = jnp.zeros_like(acc_ref)

        acc_ref[...] += _dot_tn(p_ref[...], h_ref[...])

        @pl.when(i == n_t - 1)
        def _():
            acc16_ref[...] = acc_ref[...].astype(BF16)
            rows = pl.ds(pl.multiple_of(row0 + j * nb, 16), nb)
            c32 = pltpu.make_async_copy(acc_ref, o32_ref.at[rows], sems.at[0])
            c16 = pltpu.make_async_copy(acc16_ref, o16_ref.at[rows], sems.at[1])
            c32.start()
            c16.start()
            c32.wait()
            c16.wait()

    hbm = pl.BlockSpec(memory_space=pl.ANY)
    carried = [] if prev is None else list(prev)
    return pl.pallas_call(
        body, name=name, grid=(n // nb, n_t),
        in_specs=[pl.BlockSpec((tm, D), lambda j, i: (i, 0)), pl.BlockSpec((tm, nb), lambda j, i: (i, j))]
        + [hbm] * len(carried),
        out_specs=[hbm, hbm],
        out_shape=[jax.ShapeDtypeStruct((D_IN, D), F32), jax.ShapeDtypeStruct((D_IN, D), BF16)],
        scratch_shapes=[pltpu.VMEM((nb, D), F32), pltpu.VMEM((nb, D), BF16), pltpu.SemaphoreType.DMA((2,))],
        input_output_aliases={2: 0, 3: 1} if carried else {},
        compiler_params=_params("arbitrary", "arbitrary"),
    )(h, piece, *carried)


def _place():
    x, y, c = lax.axis_index("x"), lax.axis_index("y"), lax.axis_index("c")
    return x, y, c, 4 * x + 2 * y + c


def _peer(x, y, c, k):
    return (1 - x if k & 4 else x, 1 - y if k & 2 else y, 1 - c if k & 1 else c)


ICI_MASKS = (4, 2, 6)


def _all_gather(shards):
    n = len(shards)

    def body(*refs):
        src, dst = refs[:n], refs[n:2 * n]
        send_sems, recv_sems, local_sems = refs[2 * n:]
        x, y, c, me = _place()
        sibling = _peer(x, y, c, 1)

        def copy(a, s, block, to, own=False):
            return pltpu.make_async_remote_copy(
                src_ref=src[a] if own else dst[a].at[block], dst_ref=dst[a].at[block],
                send_sem=send_sems.at[a * 7 + s], recv_sem=recv_sems.at[a * 7 + s], device_id=to, device_id_type=MESH_ID)

        local = [pltpu.make_async_copy(src[a], dst[a].at[me], local_sems.at[a]) for a in range(n)]
        for cp in local:
            cp.start()
        started = [copy(a, 0, me, sibling, own=True) for a in range(n)]
        started += [copy(a, 1 + j, me, _peer(x, y, c, k), own=True) for j, k in enumerate(ICI_MASKS) for a in range(n)]
        for cp in started:
            cp.start()
        for j, k in enumerate(ICI_MASKS):
            for a in range(n):
                copy(a, 1 + j, me ^ k, sibling).wait_recv()
                fwd = copy(a, 4 + j, me ^ k, sibling)
                fwd.start()
                started.append(fwd)
        for a in range(n):
            copy(a, 0, me ^ 1, sibling).wait_recv()
        for j, k in enumerate(ICI_MASKS):
            for a in range(n):
                copy(a, 4 + j, me ^ 1 ^ k, sibling).wait_recv()
        for cp in started:
            cp.wait_send()
        for cp in local:
            cp.wait()

    hbm = pl.BlockSpec(memory_space=pl.ANY)
    return pl.pallas_call(
        body, name="all_gather_weights",
        in_specs=[hbm] * n, out_specs=[hbm] * n,
        out_shape=[jax.ShapeDtypeStruct((N_DEV,) + s.shape, s.dtype) for s in shards],
        scratch_shapes=[pltpu.SemaphoreType.DMA((7 * n,)), pltpu.SemaphoreType.DMA((7 * n,)),
                        pltpu.SemaphoreType.DMA((n,))],
    )(*shards)


def _exchange_sibling(by_dest):
    n = len(by_dest)

    def body(*refs):
        src, dst = refs[:n], refs[n:2 * n]
        send_sems, recv_sems = refs[2 * n:]
        x, y, c, _ = _place()
        sibling = _peer(x, y, c, 1)
        copies = [pltpu.make_async_remote_copy(
            src_ref=src[a].at[2 * p + (1 - c)], dst_ref=dst[a].at[p], send_sem=send_sems.at[a * 4 + p],
            recv_sem=recv_sems.at[a * 4 + p], device_id=sibling, device_id_type=MESH_ID)
            for a in range(n) for p in range(4)]
        for cp in copies:
            cp.start()
        for cp in copies:
            cp.wait_recv()
        for cp in copies:
            cp.wait_send()

    hbm = pl.BlockSpec(memory_space=pl.ANY)
    return pl.pallas_call(
        body, name="exchange_sibling", in_specs=[hbm] * n, out_specs=[hbm] * n,
        out_shape=[jax.ShapeDtypeStruct((4,) + s.shape[1:], s.dtype) for s in by_dest],
        scratch_shapes=[pltpu.SemaphoreType.DMA((4 * n,)), pltpu.SemaphoreType.DMA((4 * n,))],
    )(*by_dest)


def _exchange_chips(by_chip, small):
    n = len(by_chip)

    def body(*refs):
        src, small_src = refs[:n], refs[n]
        dst, small_dst = refs[n + 1:2 * n + 1], refs[2 * n + 1]
        send_sems, recv_sems, local_sem = refs[2 * n + 2:]
        x, y, c, me = _place()
        chip = 2 * x + y
        local = pltpu.make_async_copy(small_src, small_dst.at[me], local_sem)
        local.start()
        copies = [pltpu.make_async_remote_copy(
            src_ref=src[a].at[chip ^ (k >> 1)], dst_ref=dst[a].at[j], send_sem=send_sems.at[a * 3 + j],
            recv_sem=recv_sems.at[a * 3 + j], device_id=_peer(x, y, c, k), device_id_type=MESH_ID)
            for j, k in enumerate(ICI_MASKS) for a in range(n)]
        for cp in copies:
            cp.start()
        to_all = [pltpu.make_async_remote_copy(
            src_ref=small_src, dst_ref=small_dst.at[me], send_sem=send_sems.at[3 * n + k - 1],
            recv_sem=recv_sems.at[3 * n + k - 1], device_id=_peer(x, y, c, k), device_id_type=MESH_ID)
            for k in range(1, N_DEV)]
        for cp in to_all:
            cp.start()
        for cp in copies:
            cp.wait_recv()
        for k in range(1, N_DEV):
            pltpu.make_async_remote_copy(
                src_ref=small_src, dst_ref=small_dst.at[me ^ k], send_sem=send_sems.at[3 * n + k - 1],
                recv_sem=recv_sems.at[3 * n + k - 1], device_id=_peer(x, y, c, k), device_id_type=MESH_ID).wait_recv()
        for cp in copies + to_all:
            cp.wait_send()
        local.wait()

    hbm = pl.BlockSpec(memory_space=pl.ANY)
    return pl.pallas_call(
        body, name="exchange_chips", in_specs=[hbm] * (n + 1), out_specs=[hbm] * (n + 1),
        out_shape=[jax.ShapeDtypeStruct((3,) + s.shape[1:], s.dtype) for s in by_chip]
        + [jax.ShapeDtypeStruct((N_DEV,) + small.shape, small.dtype)],
        scratch_shapes=[pltpu.SemaphoreType.DMA((3 * n + 7,)), pltpu.SemaphoreType.DMA((3 * n + 7,)),
                        pltpu.SemaphoreType.DMA],
    )(*by_chip, small)


def _adamw_math(w, g, m, v):
    m = ADAM_B1 * m + (1.0 - ADAM_B1) * g
    v = ADAM_B2 * v + (1.0 - ADAM_B2) * (g * g)
    m_hat = m / (1.0 - ADAM_B1 ** ADAM_STEP)
    v_hat = v / (1.0 - ADAM_B2 ** ADAM_STEP)
    return -ADAM_LR * (m_hat / (jnp.sqrt(v_hat) + ADAM_EPS) + ADAM_WD * w), m, v


def _pair_sum(own, recv, c_arr, tr, name):
    _, rows, cols = own.shape

    def body(c_ref, own_ref, recv_ref, o32_ref, o16_ref):
        s = own_ref[...] + recv_ref[...].astype(F32)
        o32_ref[...] = s
        o16_ref[...] = s.astype(BF16)

    by_chip = pl.BlockSpec((None, tr, cols), lambda p, i, c_ref: (p, i, 0))
    return pl.pallas_call(
        body, name=name,
        grid_spec=pltpu.PrefetchScalarGridSpec(
            num_scalar_prefetch=1, grid=(4, rows // tr),
            in_specs=[pl.BlockSpec((None, tr, cols), lambda p, i, c_ref: (2 * p + c_ref[0], i, 0)), by_chip],
            out_specs=[by_chip, by_chip]),
        out_shape=[jax.ShapeDtypeStruct((4, rows, cols), F32), jax.ShapeDtypeStruct((4, rows, cols), BF16)],
        compiler_params=_params("parallel", "parallel"),
    )(c_arr, own, recv)


def _chip_sum(pair, recv, chip_arr, tr, name):
    _, rows, cols = pair.shape

    def body(chip_ref, pair_ref, recv_ref, g_ref):
        g = pair_ref[...]
        for j in range(3):
            g = g + recv_ref[j].astype(F32)
        g_ref[...] = g

    return pl.pallas_call(
        body, name=name,
        grid_spec=pltpu.PrefetchScalarGridSpec(
            num_scalar_prefetch=1, grid=(rows // tr,),
            in_specs=[pl.BlockSpec((None, tr, cols), lambda i, chip_ref: (chip_ref[0], i, 0)),
                      pl.BlockSpec((3, tr, cols), lambda i, chip_ref: (0, i, 0))],
            out_specs=pl.BlockSpec((tr, cols), lambda i, chip_ref: (i, 0))),
        out_shape=jax.ShapeDtypeStruct((rows, cols), F32),
        compiler_params=_params("parallel"),
    )(chip_arr, pair, recv)


def _adamw(w, g, m, v, tr, name):
    rows, cols = w.shape

    def body(w_ref, g_ref, m_ref, v_ref, d_ref, nm_ref, nv_ref):
        d_ref[...], nm_ref[...], nv_ref[...] = _adamw_math(w_ref[...], g_ref[...], m_ref[...], v_ref[...])

    blk = pl.BlockSpec((tr, cols), lambda i: (i, 0))
    return pl.pallas_call(
        body, name=name, grid=(rows // tr,), in_specs=[blk] * 4, out_specs=[blk] * 3,
        out_shape=[jax.ShapeDtypeStruct((rows, cols), F32)] * 3, compiler_params=_params("parallel"),
    )(w, g, m, v)


def _sum_small(small_all):
    def body(s_ref, o_ref):
        g = s_ref[0]
        for d in range(1, N_DEV):
            g = g + s_ref[d]
        o_ref[...] = g

    return pl.pallas_call(body, name="sum_small", out_shape=jax.ShapeDtypeStruct(small_all.shape[1:], F32))(small_all)


def _adamw_small(w, g, m, v, name):
    def body(w_ref, g_ref, m_ref, v_ref, d_ref, nm_ref, nv_ref):
        d_ref[...], nm_ref[...], nv_ref[...] = _adamw_math(w_ref[...], g_ref[...], m_ref[...], v_ref[...])

    return pl.pallas_call(body, name=name, out_shape=[jax.ShapeDtypeStruct(w.shape, F32)] * 3)(w, g, m, v)


def _rope_tables():
    inv_freq = ROPE_THETA ** (-jnp.arange(0, HEAD_DIM, 2, dtype=F32) / HEAD_DIM)
    ang = jnp.arange(SEQ_LEN).astype(F32)[:, None] * inv_freq[None, :]
    cos, sin = jnp.cos(ang), jnp.sin(ang)
    return jnp.tile(cos, (1, 4)), jnp.tile(jnp.concatenate([-sin, sin], axis=1), (1, 2))


def _local_step(x, target, g_pre, g_post, sinks, wt, wconv, wpc, wpa, wout):
    cos_t, sin_t = _rope_tables()
    wconv8 = jnp.pad(wconv, ((0, 5), (0, 0)))
    h, a4 = _fwd_in_a(x, g_pre, wt, 512)
    q, kv, g3 = _fwd_in_b(h, wt, cos_t, sin_t, 512)
    ya = _fwd_conv(a4, wconv8, wpc, 512)
    attn, ub = _fwd_attn(sinks, q, kv, g3)
    loss8, dout, dya, dub, dgab, dwout, dwpa, dgpost8 = _fwd_out_bwd_head(ya, ub, g3, x, target, g_post, wpa, wout, 256)
    dq, dza, dkv_own, dkv_prev, dsink8 = _bwd_attn(sinks, q, kv, attn, dub, g3, cos_t, sin_t)
    dkv = _bwd_kv_finish(dkv_own, dkv_prev, cos_t, sin_t)
    da4, dwpc, dwconv8 = _bwd_conv(dya, a4, wconv8, wpc, 512)
    grad_x, dgpre8 = _bwd_dh(da4, dq, dkv, dza, dgab, wt, x, g_pre, dout, 256)
    dwt = _bwd_dw_in(h, da4, 0, 1024, 1024, "bwd_dw_in_conv", None)
    dwt = _bwd_dw_in(h, dq, ROW_Q, 1024, 1024, "bwd_dw_in_q", dwt)
    dwt = _bwd_dw_in(h, dkv, ROW_KV, 256, 1024, "bwd_dw_in_kv", dwt)
    dwt = _bwd_dw_in(h, dza, ROW_ZA, 1024, 1024, "bwd_dw_in_za", dwt)
    dwt32, dwt16 = _bwd_dw_in(h, dgab, ROW_GA, 1024, 1024, "bwd_dw_in_gates", dwt)
    small = jnp.concatenate([dgpre8, dgpost8, jnp.pad(dsink8, ((0, 0), (0, D - 128))), dwconv8], axis=0)
    return loss8[0, 0], grad_x, dwt32, dwt16, dwpc, dwpa, dwout, small


def kernel(x, g_pre, g_post, w_in, w_conv, sinks, w_proj_conv, w_proj_attn, w_out, loss_target, m_g_pre, m_g_post, m_w_in, m_w_conv, m_sinks, m_w_proj_conv, m_w_proj_attn, m_w_out, v_g_pre, v_g_post, v_w_in, v_w_conv, v_sinks, v_w_proj_conv, v_w_proj_attn, v_w_out):
    batch = x.shape[0]
    mx, my, mc, me = _place()
    c_arr = jnp.reshape(mc, (1,)).astype(jnp.int32)
    chip_arr = jnp.reshape(2 * mx + my, (1,)).astype(jnp.int32)

    g_wt, g_conv, g_pc, g_pa, g_out = _all_gather([
        w_in[0].T.astype(BF16), jnp.pad(w_conv[0], ((0, 5), (0, 0))), w_proj_conv[0].astype(BF16),
        w_proj_attn[0].astype(BF16), w_out[0].astype(BF16)])
    wt = g_wt.reshape(D_IN, D)
    wconv = g_conv[:, 0:3, :].transpose(1, 0, 2).reshape(3, D)
    wpc, wpa, wout = g_pc.reshape(D, D), g_pa.reshape(D, D), g_out.reshape(D, D)

    loss, grad_x, dwt32, dwt16, dwpc, dwpa, dwout, small = _local_step(
        x.reshape(batch * SEQ_LEN, D), loss_target.reshape(batch * SEQ_LEN, D), g_pre, g_post, sinks,
        wt, wconv, wpc, wpa, wout)
    loss = lax.psum(loss, ("x", "y", "c"))

    own = [dwt32.reshape(N_DEV, SHARD_IN, D)] + [g.reshape(N_DEV, SHARD_SQ, D) for g in (dwpc, dwpa, dwout)]
    sent = [dwt16.reshape(N_DEV, SHARD_IN, D)] + [g.astype(BF16) for g in own[1:]]
    from_sibling = _exchange_sibling(sent)
    names = ("w_in", "w_proj_conv", "w_proj_attn", "w_out")
    tiles = (SHARD_IN // 2, SHARD_SQ, SHARD_SQ, SHARD_SQ)
    pairs = [_pair_sum(own[a], from_sibling[a], c_arr, tiles[a], "pair_sum_" + names[a]) for a in range(4)]
    *from_chips, small_all = _exchange_chips([p[1] for p in pairs], small)
    g_wt_mine, g_pc_mine, g_pa_mine, g_out_mine = [
        _chip_sum(pairs[a][0], from_chips[a], chip_arr, tiles[a], "chip_sum_" + names[a]) for a in range(4)]
    g_in_mine = g_wt_mine.T
    gs = _sum_small(small_all)
    g_g_pre, g_g_post, g_sinks = gs[0:1], gs[8:9], gs[16:17, 0:N_HEADS]
    g_conv_mine = lax.dynamic_slice_in_dim(gs[24:27], me * SHARD_SQ, SHARD_SQ, axis=1)

    o_in = _adamw(w_in[0], g_in_mine, m_w_in[0], v_w_in[0], 128, "adamw_w_in")
    o_pc = _adamw(w_proj_conv[0], g_pc_mine, m_w_proj_conv[0], v_w_proj_conv[0], SHARD_SQ, "adamw_w_proj_conv")
    o_pa = _adamw(w_proj_attn[0], g_pa_mine, m_w_proj_attn[0], v_w_proj_attn[0], SHARD_SQ, "adamw_w_proj_attn")
    o_out = _adamw(w_out[0], g_out_mine, m_w_out[0], v_w_out[0], SHARD_SQ, "adamw_w_out")
    o_gpre = _adamw(g_pre, g_g_pre, m_g_pre, v_g_pre, 1, "adamw_g_pre")
    o_gpost = _adamw(g_post, g_g_post, m_g_post, v_g_post, 1, "adamw_g_post")
    o_sinks = _adamw(sinks, g_sinks, m_sinks, v_sinks, 1, "adamw_sinks")
    o_conv = _adamw(w_conv[0], g_conv_mine, m_w_conv[0], v_w_conv[0], 3, "adamw_w_conv")

    grads = [g_g_pre, g_g_post, g_in_mine[None], g_conv_mine[None], g_sinks, g_pc_mine[None], g_pa_mine[None],
             g_out_mine[None]]
    rest = []
    for idx in (0, 1, 2):
        rest += [o_gpre[idx], o_gpost[idx], o_in[idx][None], o_conv[idx][None], o_sinks[idx],
                 o_pc[idx][None], o_pa[idx][None], o_out[idx][None]]
    return (loss, grad_x.reshape(batch, SEQ_LEN, D), *grads, *rest)
```

```python
import functools

import jax
import jax.numpy as jnp
from jax import lax
from jax.experimental import pallas as pl
from jax.experimental.pallas import tpu as pltpu

D = 1024
N_HEADS = 16
HEAD_DIM = 64
BLK = 128
SEQ_LEN = 2048
D_IN = 8448
ROW_Q, ROW_KV, ROW_ZA, ROW_GA = 4 * D, 5 * D, 5 * D + 256, 6 * D + 256
SHARD_IN = D_IN // 8
SHARD_SQ = D // 8
N_DEV = 8
ROPE_THETA = 10000.0
RMS_EPS = 1e-6
NEG = -1e30
ADAM_LR, ADAM_B1, ADAM_B2, ADAM_EPS, ADAM_WD, ADAM_STEP = 0.001, 0.9, 0.999, 1e-08, 0.01, 10

F32 = jnp.float32
BF16 = jnp.bfloat16
MESH_ID = pl.DeviceIdType.MESH


def _dot(a, b):
    return jnp.dot(a, b, preferred_element_type=F32)


def _dot_nt(a, b):
    return lax.dot_general(a, b, (((1,), (1,)), ((), ())), preferred_element_type=F32)


def _dot_tn(a, b):
    return lax.dot_general(a, b, (((0,), (0,)), ((), ())), preferred_element_type=F32)


def _sig(z):
    return 1.0 / (1.0 + jnp.exp(-z))


def _swap_halves(z):
    lane = lax.broadcasted_iota(jnp.int32, z.shape, 1)
    return jnp.where((lane & 63) < 32, pltpu.roll(z, 96, 1), pltpu.roll(z, 32, 1))


def _row_spec(tm, width, col=0):
    return pl.BlockSpec((tm, width), lambda i: (i, col))


def _whole_vmem():
    return pl.BlockSpec(memory_space=pltpu.VMEM)


def _params(*sem):
    return pltpu.CompilerParams(dimension_semantics=sem)


def _fwd_in_a(x, g_pre, wt, tm):
    t = x.shape[0]

    def body(x_ref, g_ref, w_ref, h_ref, a4_ref):
        xf = x_ref[...]
        r = lax.rsqrt(jnp.mean(xf * xf, axis=-1, keepdims=True) + RMS_EPS)
        h = ((xf * r) * g_ref[...]).astype(BF16)
        h_ref[...] = h
        for j in range(4):
            a4_ref[:, j * D:(j + 1) * D] = _dot_nt(h, w_ref[j * D:(j + 1) * D, :])

    return pl.pallas_call(
        body, name="fwd_in_a", grid=(t // tm,),
        in_specs=[_row_spec(tm, D), pl.BlockSpec((1, D), lambda i: (0, 0)), _whole_vmem()],
        out_specs=[_row_spec(tm, D), _row_spec(tm, 4 * D)],
        out_shape=[jax.ShapeDtypeStruct((t, D), BF16), jax.ShapeDtypeStruct((t, 4 * D), F32)],
        compiler_params=_params("parallel"),
    )(x, g_pre, wt)


def _fwd_in_b(h, wt, cos_t, sin_t, tm):
    t = h.shape[0]
    seq_tiles = SEQ_LEN // tm

    def body(h_ref, w_ref, c_ref, s_ref, q_ref, kv_ref, g3_ref):
        hh = h_ref[...]
        c = c_ref[...]
        s = s_ref[...]

        def rope(z):
            return z * c + _swap_halves(z) * s

        q = _dot_nt(hh, w_ref[ROW_Q:ROW_Q + D, :])
        for j in range(D // 128):
            q_ref[:, j * 128:(j + 1) * 128] = rope(q[:, j * 128:(j + 1) * 128]).astype(BF16)
        kv = _dot_nt(hh, w_ref[ROW_KV:ROW_KV + 256, :])
        kv_ref[:, 0:128] = rope(kv[:, 0:128]).astype(BF16)
        kv_ref[:, 128:256] = kv[:, 128:256].astype(BF16)
        for j in range(3):
            g3_ref[:, j * D:(j + 1) * D] = _dot_nt(hh, w_ref[ROW_ZA + j * D:ROW_ZA + (j + 1) * D, :])

    tab = pl.BlockSpec((tm, 128), lambda i: (i % seq_tiles, 0))
    return pl.pallas_call(
        body, name="fwd_in_b", grid=(t // tm,),
        in_specs=[_row_spec(tm, D), _whole_vmem(), tab, tab],
        out_specs=[_row_spec(tm, D), _row_spec(tm, 256), _row_spec(tm, 3 * D)],
        out_shape=[jax.ShapeDtypeStruct((t, D), BF16), jax.ShapeDtypeStruct((t, 256), BF16),
                   jax.ShapeDtypeStruct((t, 3 * D), F32)],
        compiler_params=_params("parallel"),
    )(h, wt, cos_t, sin_t)


def _conv_forward(xc, bg, cg, zc, up6, up7, w_ref):
    tm = xc.shape[0]
    rows = lax.broadcasted_iota(jnp.int32, xc.shape, 0)
    u = cg * xc
    u_m1 = jnp.where(rows == 0, up7, pltpu.roll(u, 1, 0))
    u_m2 = jnp.where(rows == 0, up6, jnp.where(rows == 1, up7, pltpu.roll(u, 2, 0)))
    yconv = w_ref[0:1, :] * u_m2 + w_ref[1:2, :] * u_m1 + w_ref[2:3, :] * u
    sg = _sig(zc)
    sz = zc * sg
    co = bg * yconv
    del tm
    return u, u_m1, u_m2, yconv, sg, sz, co


def _fwd_conv(a4, wconv8, wpc, tm):
    t = a4.shape[0]
    seq_tiles = SEQ_LEN // tm

    def body(xc_ref, bg_ref, cg_ref, zc_ref, xcp_ref, cgp_ref, w_ref, wpc_ref, ya_ref):
        i = pl.program_id(0)
        keep = jnp.where(i % seq_tiles == 0, 0.0, 1.0)
        up6 = cgp_ref[6:7, :] * xcp_ref[6:7, :] * keep
        up7 = cgp_ref[7:8, :] * xcp_ref[7:8, :] * keep
        _, _, _, _, _, sz, co = _conv_forward(xc_ref[...], bg_ref[...], cg_ref[...], zc_ref[...], up6, up7, w_ref)
        ya_ref[...] = _dot((sz * co).astype(BF16), wpc_ref[...])

    def prev(col):
        return pl.BlockSpec((8, D), lambda i: (jnp.maximum(i * (tm // 8) - 1, 0), col))

    return pl.pallas_call(
        body, name="fwd_conv", grid=(t // tm,),
        in_specs=[_row_spec(tm, D, 0), _row_spec(tm, D, 1), _row_spec(tm, D, 2), _row_spec(tm, D, 3),
                  prev(0), prev(2), pl.BlockSpec((8, D), lambda i: (0, 0)), _whole_vmem()],
        out_specs=_row_spec(tm, D),
        out_shape=jax.ShapeDtypeStruct((t, D), F32),
        compiler_params=_params("parallel"),
    )(a4, a4, a4, a4, a4, a4, wconv8, wpc)


STACK = 4 * BLK


def _band_mask(first):
    qi = lax.broadcasted_iota(jnp.int32, (STACK, 2 * BLK), 0) & (BLK - 1)
    kj = lax.broadcasted_iota(jnp.int32, (STACK, 2 * BLK), 1)
    return (kj > qi) & (kj <= qi + BLK) & (kj >= jnp.where(first, BLK, 0))


def _masked_fill(sink_ref, g, e):
    kj = lax.broadcasted_iota(jnp.int32, (STACK, 2 * BLK), 1)
    sink = jnp.concatenate([jnp.full((BLK, 2 * BLK), sink_ref[0, 2 * (4 * g + jj) + e], F32) for jj in range(4)], axis=0)
    return jnp.where(kj == 0, sink, NEG)


def _padded_pair(kvp_ref, kvc_ref, col, other=0.0):
    z = jnp.concatenate([kvp_ref[:, col:col + 128], kvc_ref[:, col:col + 128]], axis=0).astype(F32)
    z = jnp.where(lax.broadcasted_iota(jnp.int32, z.shape, 0) == 0, 0.0, z)
    zs = pltpu.roll(z, 64, 1)
    lo = lax.broadcasted_iota(jnp.int32, z.shape, 1) < 64
    fill = jnp.full_like(z, other)
    left = [jnp.where(lo, z, fill).astype(BF16), jnp.where(lo, zs, fill).astype(BF16)]
    right = [jnp.where(lo, fill, zs).astype(BF16), jnp.where(lo, fill, z).astype(BF16)]
    return left, right


def _exp_logits(s, valid, fill):
    s = jnp.where(valid, s * (HEAD_DIM ** -0.5), fill)
    return jnp.exp(s - jnp.max(s, axis=-1, keepdims=True))


def _fwd_attn(sinks, q, kv, g3):
    t = q.shape[0]
    seq_blocks = SEQ_LEN // BLK

    def body(sink_ref, q_ref, kvc_ref, kvp_ref, za_ref, attn_ref, ub_ref):
        i = pl.program_id(0)
        valid = _band_mask(i % seq_blocks == 0)
        k_pad = _padded_pair(kvp_ref, kvc_ref, 0)
        v_one = _padded_pair(kvp_ref, kvc_ref, 128, other=1.0)
        lo = lax.broadcasted_iota(jnp.int32, (STACK, 128), 1) < 64
        for g in range(2):
            qg = jnp.concatenate([q_ref[:, j * 128:(j + 1) * 128] for j in range(4 * g, 4 * g + 4)], axis=0)
            pv = [_dot(_exp_logits(_dot_nt(qg, k_pad[e][g]), valid, _masked_fill(sink_ref, g, e)).astype(BF16),
                       v_one[e][g]) for e in range(2)]
            o = jnp.where(lo, pv[0], pv[1]) / pltpu.roll(jnp.where(lo, pv[1], pv[0]), 64, 1)
            for jj in range(4):
                cols = slice((4 * g + jj) * 128, (4 * g + jj + 1) * 128)
                oj = o[jj * BLK:(jj + 1) * BLK, :]
                attn_ref[:, cols] = oj
                za = za_ref[:, cols]
                ub_ref[:, cols] = (za * _sig(za) * oj).astype(BF16)

    return pl.pallas_call(
        body, name="fwd_attn", grid=(t // BLK,),
        in_specs=[pl.BlockSpec(memory_space=pltpu.SMEM), _row_spec(BLK, D), _row_spec(BLK, 256),
                  pl.BlockSpec((BLK, 256), lambda i: (jnp.maximum(i - 1, 0), 0)), _row_spec(BLK, D, 0)],
        out_specs=[_row_spec(BLK, D), _row_spec(BLK, D)],
        out_shape=[jax.ShapeDtypeStruct((t, D), F32), jax.ShapeDtypeStruct((t, D), BF16)],
        compiler_params=_params("parallel"),
    )(sinks, q, kv, kv, g3)


def _fwd_out_bwd_head(ya, ub, g3, x, target, g_post, wpa, wout, tm):
    t = x.shape[0]

    def body(ya_ref, ub_ref, ga_ref, gb_ref, x_ref, tgt_ref, gp_ref, wpa_ref, wout_ref,
             loss_ref, dout_ref, dya_ref, dub_ref, dgab_ref, dwout_ref, dwpa_ref, dgp_ref):
        @pl.when(pl.program_id(0) == 0)
        def _():
            loss_ref[...] = jnp.zeros_like(loss_ref)
            dwout_ref[...] = jnp.zeros_like(dwout_ref)
            dwpa_ref[...] = jnp.zeros_like(dwpa_ref)
            dgp_ref[...] = jnp.zeros_like(dgp_ref)

        ub = ub_ref[...]
        ya = ya_ref[...]
        yb = _dot(ub, wpa_ref[...])
        sa = _sig(ga_ref[...])
        sb = _sig(gb_ref[...])
        mb = (sa * ya + sb * yb).astype(BF16)
        y = _dot(mb, wout_ref[...])
        r = lax.rsqrt(jnp.mean(y * y, axis=-1, keepdims=True) + RMS_EPS)
        n = y * r
        g = gp_ref[...]
        err = (x_ref[...] + n * g) - tgt_ref[...]
        sq = jnp.sum(jnp.sum(err * err, axis=0, keepdims=True), axis=1, keepdims=True)
        loss_ref[...] += sq * (0.5 / D)
        dout = err * (1.0 / D)
        dout_ref[...] = dout
        dgp_ref[0:1, :] += jnp.sum(dout * n, axis=0, keepdims=True)
        dn = dout * g
        dy = (r * (dn - n * jnp.mean(dn * n, axis=-1, keepdims=True))).astype(BF16)
        dwout_ref[...] += _dot_tn(mb, dy)
        dm = _dot_nt(dy, wout_ref[...])
        dya_ref[...] = (dm * sa).astype(BF16)
        dyb = (dm * sb).astype(BF16)
        dgab_ref[:, 0:D] = (dm * ya * (sa * (1.0 - sa))).astype(BF16)
        dgab_ref[:, D:2 * D] = (dm * yb * (sb * (1.0 - sb))).astype(BF16)
        dwpa_ref[...] += _dot_tn(ub, dyb)
        dub_ref[...] = _dot_nt(dyb, wpa_ref[...])

    return pl.pallas_call(
        body, name="fwd_out_bwd_head", grid=(t // tm,),
        in_specs=[_row_spec(tm, D), _row_spec(tm, D), _row_spec(tm, D, 1), _row_spec(tm, D, 2),
                  _row_spec(tm, D), _row_spec(tm, D), pl.BlockSpec((1, D), lambda i: (0, 0)),
                  _whole_vmem(), _whole_vmem()],
        out_specs=[pl.BlockSpec((8, 128), lambda i: (0, 0)), _row_spec(tm, D), _row_spec(tm, D), _row_spec(tm, D),
                   _row_spec(tm, 2 * D), _whole_vmem(), _whole_vmem(), pl.BlockSpec((8, D), lambda i: (0, 0))],
        out_shape=[jax.ShapeDtypeStruct((8, 128), F32), jax.ShapeDtypeStruct((t, D), F32),
                   jax.ShapeDtypeStruct((t, D), BF16), jax.ShapeDtypeStruct((t, D), F32),
                   jax.ShapeDtypeStruct((t, 2 * D), BF16), jax.ShapeDtypeStruct((D, D), F32),
                   jax.ShapeDtypeStruct((D, D), F32), jax.ShapeDtypeStruct((8, D), F32)],
        compiler_params=_params("arbitrary"),
    )(ya, ub, g3, g3, x, target, g_post, wpa, wout)


def _bwd_attn(sinks, q, kv, attn, dub, g3, cos_t, sin_t):
    t = q.shape[0]
    seq_blocks = SEQ_LEN // BLK

    def body(sink_ref, q_ref, kvc_ref, kvp_ref, attn_ref, dub_ref, za_ref, c_ref, s_ref,
             dq_ref, dza_ref, dkv_own_ref, dkv_prev_ref, dsink_ref):
        i = pl.program_id(0)

        @pl.when(i == 0)
        def _():
            dsink_ref[...] = jnp.zeros_like(dsink_ref)

        valid = _band_mask(i % seq_blocks == 0)
        k_pad = _padded_pair(kvp_ref, kvc_ref, 0)
        v_pad = _padded_pair(kvp_ref, kvc_ref, 128)
        lo = lax.broadcasted_iota(jnp.int32, (STACK, 128), 1) < 64
        lane8 = lax.broadcasted_iota(jnp.int32, (8, 128), 1)
        c = c_ref[...]
        s = s_ref[...]
        dk_acc, dv_acc = [], []
        dsink = jnp.zeros((8, 128), F32)
        for g in range(2):
            qg, dog = [], []
            for j in range(4 * g, 4 * g + 4):
                cols = slice(j * 128, (j + 1) * 128)
                za = za_ref[:, cols]
                sg = _sig(za)
                dub = dub_ref[:, cols]
                dza_ref[:, cols] = (dub * attn_ref[:, cols] * (sg * (1.0 + za * (1.0 - sg)))).astype(BF16)
                dog.append((dub * (za * sg)).astype(BF16))
                qg.append(q_ref[:, cols])
            qg = jnp.concatenate(qg, axis=0)
            dog = jnp.concatenate(dog, axis=0)
            dq = jnp.zeros((STACK, 128), F32)
            ds_both, p_both = [], []
            for e in range(2):
                p = _exp_logits(_dot_nt(qg, k_pad[e][g]), valid, _masked_fill(sink_ref, g, e))
                p = p / jnp.sum(p, axis=-1, keepdims=True)
                dp = _dot_nt(dog, v_pad[e][g])
                ds = p * (dp - jnp.sum(p * dp, axis=-1, keepdims=True))
                for jj in range(4):
                    tot = jnp.sum(ds[jj * BLK:(jj + 1) * BLK, 0:1], axis=0, keepdims=True)
                    dsink = dsink + jnp.where(lane8 == 2 * (4 * g + jj) + e, tot, 0.0)
                ds = (ds * (HEAD_DIM ** -0.5)).astype(BF16)
                dq = dq + _dot(ds, k_pad[e][g])
                ds_both.append(ds)
                p_both.append(p.astype(BF16))
            zero = jnp.zeros_like(qg)
            q2 = jnp.concatenate([jnp.where(lo, qg, zero), jnp.where(lo, zero, qg)], axis=0)
            do2 = jnp.concatenate([jnp.where(lo, dog, zero), jnp.where(lo, zero, dog)], axis=0)
            dk_acc.append(_dot_tn(jnp.concatenate(ds_both, axis=0), q2))
            dv_acc.append(_dot_tn(jnp.concatenate(p_both, axis=0), do2))
            for jj in range(4):
                cols = slice((4 * g + jj) * 128, (4 * g + jj + 1) * 128)
                dqj = dq[jj * BLK:(jj + 1) * BLK, :]
                dq_ref[:, cols] = (dqj * c - _swap_halves(dqj) * s).astype(BF16)
        dsink_ref[...] += dsink
        lo2 = lax.broadcasted_iota(jnp.int32, (2 * BLK, 128), 1) < 64
        sink_row = lax.broadcasted_iota(jnp.int32, (2 * BLK, 128), 0) == 0
        for col, acc in ((0, dk_acc), (128, dv_acc)):
            both = jnp.where(lo2, acc[0] + pltpu.roll(acc[0], 64, 1), acc[1] + pltpu.roll(acc[1], 64, 1))
            both = jnp.where(sink_row, 0.0, both)
            dkv_prev_ref[:, col:col + 128] = both[0:BLK, :]
            dkv_own_ref[:, col:col + 128] = both[BLK:2 * BLK, :]

    tab = pl.BlockSpec((BLK, 128), lambda i: (i % seq_blocks, 0))
    return pl.pallas_call(
        body, name="bwd_attn", grid=(t // BLK,),
        in_specs=[pl.BlockSpec(memory_space=pltpu.SMEM), _row_spec(BLK, D), _row_spec(BLK, 256),
                  pl.BlockSpec((BLK, 256), lambda i: (jnp.maximum(i - 1, 0), 0)),
                  _row_spec(BLK, D), _row_spec(BLK, D), _row_spec(BLK, D, 0), tab, tab],
        out_specs=[_row_spec(BLK, D), _row_spec(BLK, D), _row_spec(BLK, 256), _row_spec(BLK, 256),
                   pl.BlockSpec((8, 128), lambda i: (0, 0))],
        out_shape=[jax.ShapeDtypeStruct((t, D), BF16), jax.ShapeDtypeStruct((t, D), BF16),
                   jax.ShapeDtypeStruct((t, 256), F32), jax.ShapeDtypeStruct((t, 256), F32),
                   jax.ShapeDtypeStruct((8, 128), F32)],
        compiler_params=_params("arbitrary"),
    )(sinks, q, kv, kv, attn, dub, g3, cos_t, sin_t)


def _bwd_kv_finish(dkv_own, dkv_prev, cos_t, sin_t):
    t = dkv_own.shape[0]
    tm = 512
    seq_tiles = SEQ_LEN // tm
    n_blocks = t // BLK

    def body(own_ref, same_ref, nxt_ref, c_ref, s_ref, out_ref):
        keep = jnp.where(pl.program_id(0) % seq_tiles == seq_tiles - 1, 0.0, 1.0)
        shifted = jnp.concatenate([same_ref[BLK:tm, :], nxt_ref[...] * keep], axis=0)
        tot = own_ref[...] + shifted
        dk = tot[:, 0:128]
        out_ref[:, 0:128] = (dk * c_ref[...] - _swap_halves(dk) * s_ref[...]).astype(BF16)
        out_ref[:, 128:256] = tot[:, 128:256].astype(BF16)

    tab = pl.BlockSpec((tm, 128), lambda i: (i % seq_tiles, 0))
    return pl.pallas_call(
        body, name="bwd_kv_finish", grid=(t // tm,),
        in_specs=[_row_spec(tm, 256), _row_spec(tm, 256),
                  pl.BlockSpec((BLK, 256), lambda i: (jnp.minimum((i + 1) * (tm // BLK), n_blocks - 1), 0)), tab, tab],
        out_specs=_row_spec(tm, 256),
        out_shape=jax.ShapeDtypeStruct((t, 256), BF16),
        compiler_params=_params("parallel"),
    )(dkv_own, dkv_prev, dkv_prev, cos_t, sin_t)


def _bwd_conv(dya, a4, wconv8, wpc, tm):
    t = a4.shape[0]
    seq_tiles = SEQ_LEN // tm
    last8 = t // 8 - 1
    last16 = t // 16 - 1

    def body(dya_ref, xc_ref, bg_ref, cg_ref, zc_ref, xcp_ref, cgp_ref, dyan_ref, bgn_ref, zcn_ref,
             w_ref, wpc_ref, da4_ref, dwpc_ref, dwc_ref):
        i = pl.program_id(0)

        @pl.when(i == 0)
        def _():
            dwpc_ref[...] = jnp.zeros_like(dwpc_ref)
            dwc_ref[...] = jnp.zeros_like(dwc_ref)

        keep_prev = jnp.where(i % seq_tiles == 0, 0.0, 1.0)
        keep_next = jnp.where(i % seq_tiles == seq_tiles - 1, 0.0, 1.0)
        up6 = cgp_ref[6:7, :] * xcp_ref[6:7, :] * keep_prev
        up7 = cgp_ref[7:8, :] * xcp_ref[7:8, :] * keep_prev
        xc = xc_ref[...]
        bg = bg_ref[...]
        cg = cg_ref[...]
        zc = zc_ref[...]
        u, u_m1, u_m2, yconv, sg, sz, co = _conv_forward(xc, bg, cg, zc, up6, up7, w_ref)
        dya = dya_ref[...]
        dwpc_ref[...] += _dot_tn((sz * co).astype(BF16), dya)
        dua = _dot_nt(dya, wpc_ref[...])
        da4_ref[:, 3 * D:4 * D] = (dua * co * (sg * (1.0 + zc * (1.0 - sg)))).astype(BF16)
        dco = dua * sz
        da4_ref[:, D:2 * D] = (dco * yconv).astype(BF16)
        dyc = dco * bg
        dwc_ref[0:1, :] += jnp.sum(dyc * u_m2, axis=0, keepdims=True)
        dwc_ref[1:2, :] += jnp.sum(dyc * u_m1, axis=0, keepdims=True)
        dwc_ref[2:3, :] += jnp.sum(dyc * u, axis=0, keepdims=True)
        zcn = zcn_ref[...]
        dyc_n = _dot_nt(dyan_ref[...], wpc_ref[...])[0:8, :] * (zcn * _sig(zcn)) * bgn_ref[...] * keep_next
        rows = lax.broadcasted_iota(jnp.int32, xc.shape, 0)
        n0 = dyc_n[0:1, :]
        n1 = dyc_n[1:2, :]
        dyc_p1 = jnp.where(rows == tm - 1, n0, pltpu.roll(dyc, tm - 1, 0))
        dyc_p2 = jnp.where(rows == tm - 2, n0, jnp.where(rows == tm - 1, n1, pltpu.roll(dyc, tm - 2, 0)))
        du = w_ref[2:3, :] * dyc + w_ref[1:2, :] * dyc_p1 + w_ref[0:1, :] * dyc_p2
        da4_ref[:, 0:D] = (du * cg).astype(BF16)
        da4_ref[:, 2 * D:3 * D] = (du * xc).astype(BF16)

    def prev(col):
        return pl.BlockSpec((8, D), lambda i: (jnp.maximum(i * (tm // 8) - 1, 0), col))

    def nxt(col):
        return pl.BlockSpec((8, D), lambda i: (jnp.minimum((i + 1) * (tm // 8), last8), col))

    return pl.pallas_call(
        body, name="bwd_conv", grid=(t // tm,),
        in_specs=[_row_spec(tm, D), _row_spec(tm, D, 0), _row_spec(tm, D, 1), _row_spec(tm, D, 2), _row_spec(tm, D, 3),
                  prev(0), prev(2),
                  pl.BlockSpec((16, D), lambda i: (jnp.minimum((i + 1) * (tm // 16), last16), 0)), nxt(1), nxt(3),
                  pl.BlockSpec((8, D), lambda i: (0, 0)), _whole_vmem()],
        out_specs=[_row_spec(tm, 4 * D), _whole_vmem(), pl.BlockSpec((8, D), lambda i: (0, 0))],
        out_shape=[jax.ShapeDtypeStruct((t, 4 * D), BF16), jax.ShapeDtypeStruct((D, D), F32),
                   jax.ShapeDtypeStruct((8, D), F32)],
        compiler_params=_params("arbitrary"),
    )(dya, a4, a4, a4, a4, a4, a4, dya, a4, a4, wconv8, wpc)


def _bwd_dh(da4, dq, dkv, dza, dgab, wt, x, g_pre, dout, tm):
    t = x.shape[0]

    def body(da4_ref, dq_ref, dkv_ref, dza_ref, dgab_ref, w_ref, x_ref, g_ref, dout_ref, gx_ref, dg_ref):
        @pl.when(pl.program_id(0) == 0)
        def _():
            dg_ref[...] = jnp.zeros_like(dg_ref)

        dh = _dot(da4_ref[...], w_ref[0:ROW_Q, :])
        dh += _dot(dq_ref[...], w_ref[ROW_Q:ROW_KV, :])
        dh += _dot(dkv_ref[...], w_ref[ROW_KV:ROW_ZA, :])
        dh += _dot(dza_ref[...], w_ref[ROW_ZA:ROW_GA, :])
        dh += _dot(dgab_ref[...], w_ref[ROW_GA:D_IN, :])
        xf = x_ref[...]
        r = lax.rsqrt(jnp.mean(xf * xf, axis=-1, keepdims=True) + RMS_EPS)
        xn = xf * r
        dg_ref[0:1, :] += jnp.sum(dh * xn, axis=0, keepdims=True)
        dxn = dh * g_ref[...]
        gx_ref[...] = dout_ref[...] + r * (dxn - xn * jnp.mean(dxn * xn, axis=-1, keepdims=True))

    return pl.pallas_call(
        body, name="bwd_dh", grid=(t // tm,),
        in_specs=[_row_spec(tm, 4 * D), _row_spec(tm, D), _row_spec(tm, 256), _row_spec(tm, D), _row_spec(tm, 2 * D),
                  _whole_vmem(), _row_spec(tm, D), pl.BlockSpec((1, D), lambda i: (0, 0)), _row_spec(tm, D)],
        out_specs=[_row_spec(tm, D), pl.BlockSpec((8, D), lambda i: (0, 0))],
        out_shape=[jax.ShapeDtypeStruct((t, D), F32), jax.ShapeDtypeStruct((8, D), F32)],
        compiler_params=_params("arbitrary"),
    )(da4, dq, dkv, dza, dgab, wt, x, g_pre, dout)


def _bwd_dw_in(h, piece, row0, nb, tm, name, prev):
    t, n = piece.shape
    n_t = t // tm

    def body(*refs):
        h_ref, p_ref = refs[0], refs[1]
        o32_ref, o16_ref, acc_ref, acc16_ref, sems = refs[-5:]
        j, i = pl.program_id(0), pl.program_id(1)

        @pl.when(i == 0)
        def _():
            acc_ref[...] = jnp.zeros_like(acc_ref)

        acc_ref[...] += _dot_tn(p_ref[...], h_ref[...])

        @pl.when(i == n_t - 1)
        def _():
            acc16_ref[...] = acc_ref[...].astype(BF16)
            rows = pl.ds(pl.multiple_of(row0 + j * nb, 16), nb)
            c32 = pltpu.make_async_copy(acc_ref, o32_ref.at[rows], sems.at[0])
            c16 = pltpu.make_async_copy(acc16_ref, o16_ref.at[rows], sems.at[1])
            c32.start()
            c16.start()
            c32.wait()
            c16.wait()

    hbm = pl.BlockSpec(memory_space=pl.ANY)
    carried = [] if prev is None else list(prev)
    return pl.pallas_call(
        body, name=name, grid=(n // nb, n_t),
        in_specs=[pl.BlockSpec((tm, D), lambda j, i: (i, 0)), pl.BlockSpec((tm, nb), lambda j, i: (i, j))]
        + [hbm] * len(carried),
        out_specs=[hbm, hbm],
        out_shape=[jax.ShapeDtypeStruct((D_IN, D), F32), jax.ShapeDtypeStruct((D_IN, D), BF16)],
        scratch_shapes=[pltpu.VMEM((nb, D), F32), pltpu.VMEM((nb, D), BF16), pltpu.SemaphoreType.DMA((2,))],
        input_output_aliases={2: 0, 3: 1} if carried else {},
        compiler_params=_params("arbitrary", "arbitrary"),
    )(h, piece, *carried)


def _place():
    x, y, c = lax.axis_index("x"), lax.axis_index("y"), lax.axis_index("c")
    return x, y, c, 4 * x + 2 * y + c


def _peer(x, y, c, k):
    return (1 - x if k & 4 else x, 1 - y if k & 2 else y, 1 - c if k & 1 else c)


ICI_MASKS = (4, 2, 6)


def _all_gather(shards):
    n = len(shards)

    def body(*refs):
        src, dst = refs[:n], refs[n:2 * n]
        send_sems, recv_sems, local_sems = refs[2 * n:]
        x, y, c, me = _place()
        sibling = _peer(x, y, c, 1)

        def copy(a, s, block, to, own=False):
            return pltpu.make_async_remote_copy(
                src_ref=src[a] if own else dst[a].at[block], dst_ref=dst[a].at[block],
                send_sem=send_sems.at[a * 7 + s], recv_sem=recv_sems.at[a * 7 + s], device_id=to, device_id_type=MESH_ID)

        local = [pltpu.make_async_copy(src[a], dst[a].at[me], local_sems.at[a]) for a in range(n)]
        for cp in local:
            cp.start()
        started = [copy(a, 0, me, sibling, own=True) for a in range(n)]
        started += [copy(a, 1 + j, me, _peer(x, y, c, k), own=True) for j, k in enumerate(ICI_MASKS) for a in range(n)]
        for cp in started:
            cp.start()
        for j, k in enumerate(ICI_MASKS):
            for a in range(n):
                copy(a, 1 + j, me ^ k, sibling).wait_recv()
                fwd = copy(a, 4 + j, me ^ k, sibling)
                fwd.start()
                started.append(fwd)
        for a in range(n):
            copy(a, 0, me ^ 1, sibling).wait_recv()
        for j, k in enumerate(ICI_MASKS):
            for a in range(n):
                copy(a, 4 + j, me ^ 1 ^ k, sibling).wait_recv()
        for cp in started:
            cp.wait_send()
        for cp in local:
            cp.wait()

    hbm = pl.BlockSpec(memory_space=pl.ANY)
    return pl.pallas_call(
        body, name="all_gather_weights",
        in_specs=[hbm] * n, out_specs=[hbm] * n,
        out_shape=[jax.ShapeDtypeStruct((N_DEV,) + s.shape, s.dtype) for s in shards],
        scratch_shapes=[pltpu.SemaphoreType.DMA((7 * n,)), pltpu.SemaphoreType.DMA((7 * n,)),
                        pltpu.SemaphoreType.DMA((n,))],
    )(*shards)


def _exchange_sibling(by_dest):
    n = len(by_dest)

    def body(*refs):
        src, dst = refs[:n], refs[n:2 * n]
        send_sems, recv_sems = refs[2 * n:]
        x, y, c, _ = _place()
        sibling = _peer(x, y, c, 1)
        copies = [pltpu.make_async_remote_copy(
            src_ref=src[a].at[2 * p + (1 - c)], dst_ref=dst[a].at[p], send_sem=send_sems.at[a * 4 + p],
            recv_sem=recv_sems.at[a * 4 + p], device_id=sibling, device_id_type=MESH_ID)
            for a in range(n) for p in range(4)]
        for cp in copies:
            cp.start()
        for cp in copies:
            cp.wait_recv()
        for cp in copies:
            cp.wait_send()

    hbm = pl.BlockSpec(memory_space=pl.ANY)
    return pl.pallas_call(
        body, name="exchange_sibling", in_specs=[hbm] * n, out_specs=[hbm] * n,
        out_shape=[jax.ShapeDtypeStruct((4,) + s.shape[1:], s.dtype) for s in by_dest],
        scratch_shapes=[pltpu.SemaphoreType.DMA((4 * n,)), pltpu.SemaphoreType.DMA((4 * n,))],
    )(*by_dest)


def _gather_small(small):
    def body(src, dst, send_sems, recv_sems, local_sem):
        x, y, c, me = _place()
        local = pltpu.make_async_copy(src, dst.at[me], local_sem)
        local.start()
        to_all = [pltpu.make_async_remote_copy(
            src_ref=src, dst_ref=dst.at[me], send_sem=send_sems.at[k - 1], recv_sem=recv_sems.at[k - 1],
            device_id=_peer(x, y, c, k), device_id_type=MESH_ID) for k in range(1, N_DEV)]
        for cp in to_all:
            cp.start()
        for k in range(1, N_DEV):
            pltpu.make_async_remote_copy(
                src_ref=src, dst_ref=dst.at[me ^ k], send_sem=send_sems.at[k - 1], recv_sem=recv_sems.at[k - 1],
                device_id=_peer(x, y, c, k), device_id_type=MESH_ID).wait_recv()
        for cp in to_all:
            cp.wait_send()
        local.wait()

    hbm = pl.BlockSpec(memory_space=pl.ANY)
    return pl.pallas_call(
        body, name="gather_small", in_specs=[hbm], out_specs=hbm,
        out_shape=jax.ShapeDtypeStruct((N_DEV,) + small.shape, small.dtype),
        scratch_shapes=[pltpu.SemaphoreType.DMA((7,)), pltpu.SemaphoreType.DMA((7,)), pltpu.SemaphoreType.DMA],
    )(small)


def _chip_copies(src, land, send_sems, recv_sems):
    x, y, c, _ = _place()
    chip = 2 * x + y
    return [pltpu.make_async_remote_copy(
        src_ref=src[a].at[chip ^ (k >> 1)], dst_ref=land[a].at[j], send_sem=send_sems.at[a * 3 + j],
        recv_sem=recv_sems.at[a * 3 + j], device_id=_peer(x, y, c, k), device_id_type=MESH_ID)
        for j, k in enumerate(ICI_MASKS) for a in range(len(src))]


def _exchange_chips_start(by_chip):
    n = len(by_chip)

    def body(*refs):
        src, land = refs[:n], refs[n:2 * n]
        send_sems, recv_sems = refs[2 * n], refs[2 * n + 1]
        token_ref = refs[-1]
        for cp in _chip_copies(src, land, send_sems, recv_sems):
            cp.start()
        token_ref[...] = jnp.zeros_like(token_ref)

    hbm = pl.BlockSpec(memory_space=pltpu.HBM)
    sem = pl.BlockSpec(memory_space=pltpu.SEMAPHORE)
    lands = [lax.empty((3,) + s.shape[1:], s.dtype) for s in by_chip]
    out = pl.pallas_call(
        body, name="exchange_chips_start",
        out_shape=(pltpu.SemaphoreType.DMA((3 * n,)), pltpu.SemaphoreType.DMA((3 * n,)),
                   *[pltpu.HBM(s.shape, s.dtype) for s in by_chip], *[pltpu.HBM(s.shape, s.dtype) for s in lands],
                   jax.ShapeDtypeStruct((8, 128), F32)),
        in_specs=[hbm] * (2 * n), out_specs=(sem, sem, *[hbm] * (2 * n), _whole_vmem()),
        input_output_aliases={a: 2 + a for a in range(2 * n)},
        compiler_params=pltpu.CompilerParams(has_side_effects=pltpu.SideEffectType.DATAFLOW_SIDE_EFFECTING),
    )(*[pltpu.with_memory_space_constraint(s, pltpu.HBM) for s in list(by_chip) + lands])
    return out[0], out[1], out[2:2 + n], out[2 + n:2 + 2 * n], out[-1]


def _exchange_chips_wait(send_sems, recv_sems, flying, lands, after):
    n = len(flying)

    def body(*refs):
        src, land = refs[:n], refs[n:2 * n]
        send_sems_ref, recv_sems_ref = refs[2 * n], refs[2 * n + 1]
        for cp in _chip_copies(src, land, send_sems_ref, recv_sems_ref):
            cp.wait_send()
            cp.wait_recv()

    hbm = pl.BlockSpec(memory_space=pltpu.HBM)
    sem = pl.BlockSpec(memory_space=pltpu.SEMAPHORE)
    out = pl.pallas_call(
        body, name="exchange_chips_wait",
        out_shape=tuple(pltpu.HBM(s.shape, s.dtype) for s in list(flying) + list(lands)),
        in_specs=[hbm] * (2 * n) + [sem, sem, pl.BlockSpec(memory_space=pl.ANY)], out_specs=tuple([hbm] * (2 * n)),
        input_output_aliases={a: a for a in range(2 * n)},
        compiler_params=pltpu.CompilerParams(has_side_effects=pltpu.SideEffectType.DATAFLOW_SIDE_EFFECTING),
    )(*flying, *lands, send_sems, recv_sems, after)
    return out[n:]


def _adamw_math(w, g, m, v):
    m = ADAM_B1 * m + (1.0 - ADAM_B1) * g
    v = ADAM_B2 * v + (1.0 - ADAM_B2) * (g * g)
    m_hat = m / (1.0 - ADAM_B1 ** ADAM_STEP)
    v_hat = v / (1.0 - ADAM_B2 ** ADAM_STEP)
    return -ADAM_LR * (m_hat / (jnp.sqrt(v_hat) + ADAM_EPS) + ADAM_WD * w), m, v


def _pair_sum(own, recv, c_arr, tr, name):
    _, rows, cols = own.shape

    def body(c_ref, own_ref, recv_ref, o32_ref, o16_ref):
        s = own_ref[...] + recv_ref[...].astype(F32)
        o32_ref[...] = s
        o16_ref[...] = s.astype(BF16)

    by_chip = pl.BlockSpec((None, tr, cols), lambda p, i, c_ref: (p, i, 0))
    return pl.pallas_call(
        body, name=name,
        grid_spec=pltpu.PrefetchScalarGridSpec(
            num_scalar_prefetch=1, grid=(4, rows // tr),
            in_specs=[pl.BlockSpec((None, tr, cols), lambda p, i, c_ref: (2 * p + c_ref[0], i, 0)), by_chip],
            out_specs=[by_chip, by_chip]),
        out_shape=[jax.ShapeDtypeStruct((4, rows, cols), F32), jax.ShapeDtypeStruct((4, rows, cols), BF16)],
        compiler_params=_params("parallel", "parallel"),
    )(c_arr, own, recv)


def _chip_sum(pair, recv, chip_arr, tr, name):
    _, rows, cols = pair.shape

    def body(chip_ref, pair_ref, recv_ref, g_ref):
        g = pair_ref[...]
        for j in range(3):
            g = g + recv_ref[j].astype(F32)
        g_ref[...] = g

    return pl.pallas_call(
        body, name=name,
        grid_spec=pltpu.PrefetchScalarGridSpec(
            num_scalar_prefetch=1, grid=(rows // tr,),
            in_specs=[pl.BlockSpec((None, tr, cols), lambda i, chip_ref: (chip_ref[0], i, 0)),
                      pl.BlockSpec((3, tr, cols), lambda i, chip_ref: (0, i, 0))],
            out_specs=pl.BlockSpec((tr, cols), lambda i, chip_ref: (i, 0))),
        out_shape=jax.ShapeDtypeStruct((rows, cols), F32),
        compiler_params=_params("parallel"),
    )(chip_arr, pair, recv)


def _adamw(w, g, m, v, tr, name):
    rows, cols = w.shape

    def body(w_ref, g_ref, m_ref, v_ref, d_ref, nm_ref, nv_ref):
        d_ref[...], nm_ref[...], nv_ref[...] = _adamw_math(w_ref[...], g_ref[...], m_ref[...], v_ref[...])

    blk = pl.BlockSpec((tr, cols), lambda i: (i, 0))
    return pl.pallas_call(
        body, name=name, grid=(rows // tr,), in_specs=[blk] * 4, out_specs=[blk] * 3,
        out_shape=[jax.ShapeDtypeStruct((rows, cols), F32)] * 3, compiler_params=_params("parallel"),
    )(w, g, m, v)


def _sum_small(small_all):
    def body(s_ref, o_ref):
        g = s_ref[0]
        for d in range(1, N_DEV):
            g = g + s_ref[d]
        o_ref[...] = g

    return pl.pallas_call(body, name="sum_small", out_shape=jax.ShapeDtypeStruct(small_all.shape[1:], F32))(small_all)


def _adamw_small(w, g, m, v, name):
    def body(w_ref, g_ref, m_ref, v_ref, d_ref, nm_ref, nv_ref):
        d_ref[...], nm_ref[...], nv_ref[...] = _adamw_math(w_ref[...], g_ref[...], m_ref[...], v_ref[...])

    return pl.pallas_call(body, name=name, out_shape=[jax.ShapeDtypeStruct(w.shape, F32)] * 3)(w, g, m, v)


def _rope_tables():
    inv_freq = ROPE_THETA ** (-jnp.arange(0, HEAD_DIM, 2, dtype=F32) / HEAD_DIM)
    ang = jnp.arange(SEQ_LEN).astype(F32)[:, None] * inv_freq[None, :]
    cos, sin = jnp.cos(ang), jnp.sin(ang)
    return jnp.tile(cos, (1, 4)), jnp.tile(jnp.concatenate([-sin, sin], axis=1), (1, 2))


def _local_step(x, target, g_pre, g_post, sinks, wt, wconv, wpc, wpa, wout, start_exchange=None):
    cos_t, sin_t = _rope_tables()
    wconv8 = jnp.pad(wconv, ((0, 5), (0, 0)))
    h, a4 = _fwd_in_a(x, g_pre, wt, 512)
    q, kv, g3 = _fwd_in_b(h, wt, cos_t, sin_t, 512)
    ya = _fwd_conv(a4, wconv8, wpc, 512)
    attn, ub = _fwd_attn(sinks, q, kv, g3)
    loss8, dout, dya, dub, dgab, dwout, dwpa, dgpost8 = _fwd_out_bwd_head(ya, ub, g3, x, target, g_post, wpa, wout, 256)
    dq, dza, dkv_own, dkv_prev, dsink8 = _bwd_attn(sinks, q, kv, attn, dub, g3, cos_t, sin_t)
    dkv = _bwd_kv_finish(dkv_own, dkv_prev, cos_t, sin_t)
    da4, dwpc, dwconv8 = _bwd_conv(dya, a4, wconv8, wpc, 512)
    dwt = _bwd_dw_in(h, da4, 0, 1024, 1024, "bwd_dw_in_conv", None)
    dwt = _bwd_dw_in(h, dq, ROW_Q, 1024, 1024, "bwd_dw_in_q", dwt)
    dwt = _bwd_dw_in(h, dkv, ROW_KV, 256, 1024, "bwd_dw_in_kv", dwt)
    dwt = _bwd_dw_in(h, dza, ROW_ZA, 1024, 1024, "bwd_dw_in_za", dwt)
    dwt32, dwt16 = _bwd_dw_in(h, dgab, ROW_GA, 1024, 1024, "bwd_dw_in_gates", dwt)
    token, pending = (None, None) if start_exchange is None else start_exchange(dwt32, dwt16, dwpc, dwpa, dwout)
    g_pre_after = g_pre if token is None else g_pre + token[0:1, 0:1]
    grad_x, dgpre8 = _bwd_dh(da4, dq, dkv, dza, dgab, wt, x, g_pre_after, dout, 256)
    small = jnp.concatenate([dgpre8, dgpost8, jnp.pad(dsink8, ((0, 0), (0, D - 128))), dwconv8], axis=0)
    return loss8[0, 0], grad_x, dwt32, dwt16, dwpc, dwpa, dwout, small, pending


def kernel(x, g_pre, g_post, w_in, w_conv, sinks, w_proj_conv, w_proj_attn, w_out, loss_target, m_g_pre, m_g_post, m_w_in, m_w_conv, m_sinks, m_w_proj_conv, m_w_proj_attn, m_w_out, v_g_pre, v_g_post, v_w_in, v_w_conv, v_sinks, v_w_proj_conv, v_w_proj_attn, v_w_out):
    batch = x.shape[0]
    mx, my, mc, me = _place()
    c_arr = jnp.reshape(mc, (1,)).astype(jnp.int32)
    chip_arr = jnp.reshape(2 * mx + my, (1,)).astype(jnp.int32)

    g_wt, g_conv, g_pc, g_pa, g_out = _all_gather([
        w_in[0].T.astype(BF16), jnp.pad(w_conv[0], ((0, 5), (0, 0))), w_proj_conv[0].astype(BF16),
        w_proj_attn[0].astype(BF16), w_out[0].astype(BF16)])
    wt = g_wt.reshape(D_IN, D)
    wconv = g_conv[:, 0:3, :].transpose(1, 0, 2).reshape(3, D)
    wpc, wpa, wout = g_pc.reshape(D, D), g_pa.reshape(D, D), g_out.reshape(D, D)

    names = ("w_in", "w_proj_conv", "w_proj_attn", "w_out")
    tiles = (SHARD_IN // 2, SHARD_SQ, SHARD_SQ, SHARD_SQ)

    def start_exchange(dwt32, dwt16, dwpc, dwpa, dwout):
        own = [dwt32.reshape(N_DEV, SHARD_IN, D)] + [g.reshape(N_DEV, SHARD_SQ, D) for g in (dwpc, dwpa, dwout)]
        sent = [dwt16.reshape(N_DEV, SHARD_IN, D)] + [g.astype(BF16) for g in own[1:]]
        from_sibling = _exchange_sibling(sent)
        pairs = [_pair_sum(own[a], from_sibling[a], c_arr, tiles[a], "pair_sum_" + names[a]) for a in range(4)]
        send_sems, recv_sems, flying, lands, token = _exchange_chips_start([p[1] for p in pairs])
        return token, (send_sems, recv_sems, flying, lands, [p[0] for p in pairs])

    loss, grad_x, _, _, _, _, _, small, pending = _local_step(
        x.reshape(batch * SEQ_LEN, D), loss_target.reshape(batch * SEQ_LEN, D), g_pre, g_post, sinks,
        wt, wconv, wpc, wpa, wout, start_exchange)
    loss = lax.psum(loss, ("x", "y", "c"))
    send_sems, recv_sems, flying, lands, pair_sums = pending
    from_chips = _exchange_chips_wait(send_sems, recv_sems, flying, lands, small)
    g_wt_mine, g_pc_mine, g_pa_mine, g_out_mine = [
        _chip_sum(pair_sums[a], from_chips[a], chip_arr, tiles[a], "chip_sum_" + names[a]) for a in range(4)]
    g_in_mine = g_wt_mine.T
    gs = _sum_small(_gather_small(small))
    g_g_pre, g_g_post, g_sinks = gs[0:1], gs[8:9], gs[16:17, 0:N_HEADS]
    g_conv_mine = lax.dynamic_slice_in_dim(gs[24:27], me * SHARD_SQ, SHARD_SQ, axis=1)

    o_in = _adamw(w_in[0], g_in_mine, m_w_in[0], v_w_in[0], 128, "adamw_w_in")
    o_pc = _adamw(w_proj_conv[0], g_pc_mine, m_w_proj_conv[0], v_w_proj_conv[0], SHARD_SQ, "adamw_w_proj_conv")
    o_pa = _adamw(w_proj_attn[0], g_pa_mine, m_w_proj_attn[0], v_w_proj_attn[0], SHARD_SQ, "adamw_w_proj_attn")
    o_out = _adamw(w_out[0], g_out_mine, m_w_out[0], v_w_out[0], SHARD_SQ, "adamw_w_out")
    o_gpre = _adamw(g_pre, g_g_pre, m_g_pre, v_g_pre, 1, "adamw_g_pre")
    o_gpost = _adamw(g_post, g_g_post, m_g_post, v_g_post, 1, "adamw_g_post")
    o_sinks = _adamw(sinks, g_sinks, m_sinks, v_sinks, 1, "adamw_sinks")
    o_conv = _adamw(w_conv[0], g_conv_mine, m_w_conv[0], v_w_conv[0], 3, "adamw_w_conv")

    grads = [g_g_pre, g_g_post, g_in_mine[None], g_conv_mine[None], g_sinks, g_pc_mine[None], g_pa_mine[None],
             g_out_mine[None]]
    rest = []
    for idx in (0, 1, 2):
        rest += [o_gpre[idx], o_gpost[idx], o_in[idx][None], o_conv[idx][None], o_sinks[idx],
                 o_pc[idx][None], o_pa[idx][None], o_out[idx][None]]
    return (loss, grad_x.reshape(batch, SEQ_LEN, D), *grads, *rest)
```

```python
import functools

import jax
import jax.numpy as jnp
from jax import lax
from jax.experimental import pallas as pl
from jax.experimental.pallas import tpu as pltpu

D = 1024
N_HEADS = 16
HEAD_DIM = 64
BLK = 128
SEQ_LEN = 2048
D_IN = 8448
ROW_Q, ROW_KV, ROW_ZA, ROW_GA = 4 * D, 5 * D, 5 * D + 256, 6 * D + 256
SHARD_IN = D_IN // 8
SHARD_SQ = D // 8
N_DEV = 8
ROPE_THETA = 10000.0
RMS_EPS = 1e-6
NEG = -1e30
ADAM_LR, ADAM_B1, ADAM_B2, ADAM_EPS, ADAM_WD, ADAM_STEP = 0.001, 0.9, 0.999, 1e-08, 0.01, 10

F32 = jnp.float32
BF16 = jnp.bfloat16
MESH_ID = pl.DeviceIdType.MESH


def _dot(a, b):
    return jnp.dot(a, b, preferred_element_type=F32)


def _dot_nt(a, b):
    return lax.dot_general(a, b, (((1,), (1,)), ((), ())), preferred_element_type=F32)


def _dot_tn(a, b):
    return lax.dot_general(a, b, (((0,), (0,)), ((), ())), preferred_element_type=F32)


def _sig(z):
    return 1.0 / (1.0 + jnp.exp(-z))


def _swap_halves(z):
    lane = lax.broadcasted_iota(jnp.int32, z.shape, 1)
    return jnp.where((lane & 63) < 32, pltpu.roll(z, 96, 1), pltpu.roll(z, 32, 1))


def _row_spec(tm, width, col=0):
    return pl.BlockSpec((tm, width), lambda i: (i, col))


def _whole_vmem():
    return pl.BlockSpec(memory_space=pltpu.VMEM)


def _params(*sem):
    return pltpu.CompilerParams(dimension_semantics=sem)


def _fwd_in_a(x, g_pre, wt, tm):
    t = x.shape[0]

    def body(x_ref, g_ref, w_ref, h_ref, a4_ref):
        xf = x_ref[...]
        r = lax.rsqrt(jnp.mean(xf * xf, axis=-1, keepdims=True) + RMS_EPS)
        h = ((xf * r) * g_ref[...]).astype(BF16)
        h_ref[...] = h
        for j in range(4):
            a4_ref[:, j * D:(j + 1) * D] = _dot_nt(h, w_ref[j * D:(j + 1) * D, :])

    return pl.pallas_call(
        body, name="fwd_in_a", grid=(t // tm,),
        in_specs=[_row_spec(tm, D), pl.BlockSpec((1, D), lambda i: (0, 0)), _whole_vmem()],
        out_specs=[_row_spec(tm, D), _row_spec(tm, 4 * D)],
        out_shape=[jax.ShapeDtypeStruct((t, D), BF16), jax.ShapeDtypeStruct((t, 4 * D), F32)],
        compiler_params=_params("parallel"),
    )(x, g_pre, wt)


def _fwd_in_b(h, wt, cos_t, sin_t, tm):
    t = h.shape[0]
    seq_tiles = SEQ_LEN // tm

    def body(h_ref, w_ref, c_ref, s_ref, q_ref, kv_ref, g3_ref):
        hh = h_ref[...]
        c = c_ref[...]
        s = s_ref[...]

        def rope(z):
            return z * c + _swap_halves(z) * s

        q = _dot_nt(hh, w_ref[ROW_Q:ROW_Q + D, :])
        for j in range(D // 128):
            q_ref[:, j * 128:(j + 1) * 128] = rope(q[:, j * 128:(j + 1) * 128]).astype(BF16)
        kv = _dot_nt(hh, w_ref[ROW_KV:ROW_KV + 256, :])
        kv_ref[:, 0:128] = rope(kv[:, 0:128]).astype(BF16)
        kv_ref[:, 128:256] = kv[:, 128:256].astype(BF16)
        for j in range(3):
            g3_ref[:, j * D:(j + 1) * D] = _dot_nt(hh, w_ref[ROW_ZA + j * D:ROW_ZA + (j + 1) * D, :])

    tab = pl.BlockSpec((tm, 128), lambda i: (i % seq_tiles, 0))
    return pl.pallas_call(
        body, name="fwd_in_b", grid=(t // tm,),
        in_specs=[_row_spec(tm, D), _whole_vmem(), tab, tab],
        out_specs=[_row_spec(tm, D), _row_spec(tm, 256), _row_spec(tm, 3 * D)],
        out_shape=[jax.ShapeDtypeStruct((t, D), BF16), jax.ShapeDtypeStruct((t, 256), BF16),
                   jax.ShapeDtypeStruct((t, 3 * D), F32)],
        compiler_params=_params("parallel"),
    )(h, wt, cos_t, sin_t)


def _conv_forward(xc, bg, cg, zc, up6, up7, w_ref):
    tm = xc.shape[0]
    rows = lax.broadcasted_iota(jnp.int32, xc.shape, 0)
    u = cg * xc
    u_m1 = jnp.where(rows == 0, up7, pltpu.roll(u, 1, 0))
    u_m2 = jnp.where(rows == 0, up6, jnp.where(rows == 1, up7, pltpu.roll(u, 2, 0)))
    yconv = w_ref[0:1, :] * u_m2 + w_ref[1:2, :] * u_m1 + w_ref[2:3, :] * u
    sg = _sig(zc)
    sz = zc * sg
    co = bg * yconv
    del tm
    return u, u_m1, u_m2, yconv, sg, sz, co


def _fwd_conv(a4, wconv8, wpc, tm):
    t = a4.shape[0]
    seq_tiles = SEQ_LEN // tm

    def body(xc_ref, bg_ref, cg_ref, zc_ref, xcp_ref, cgp_ref, w_ref, wpc_ref, ya_ref):
        i = pl.program_id(0)
        keep = jnp.where(i % seq_tiles == 0, 0.0, 1.0)
        up6 = cgp_ref[6:7, :] * xcp_ref[6:7, :] * keep
        up7 = cgp_ref[7:8, :] * xcp_ref[7:8, :] * keep
        _, _, _, _, _, sz, co = _conv_forward(xc_ref[...], bg_ref[...], cg_ref[...], zc_ref[...], up6, up7, w_ref)
        ya_ref[...] = _dot((sz * co).astype(BF16), wpc_ref[...])

    def prev(col):
        return pl.BlockSpec((8, D), lambda i: (jnp.maximum(i * (tm // 8) - 1, 0), col))

    return pl.pallas_call(
        body, name="fwd_conv", grid=(t // tm,),
        in_specs=[_row_spec(tm, D, 0), _row_spec(tm, D, 1), _row_spec(tm, D, 2), _row_spec(tm, D, 3),
                  prev(0), prev(2), pl.BlockSpec((8, D), lambda i: (0, 0)), _whole_vmem()],
        out_specs=_row_spec(tm, D),
        out_shape=jax.ShapeDtypeStruct((t, D), F32),
        compiler_params=_params("parallel"),
    )(a4, a4, a4, a4, a4, a4, wconv8, wpc)


STACK = 4 * BLK


def _band_mask(first):
    qi = lax.broadcasted_iota(jnp.int32, (STACK, 2 * BLK), 0) & (BLK - 1)
    kj = lax.broadcasted_iota(jnp.int32, (STACK, 2 * BLK), 1)
    return (kj > qi) & (kj <= qi + BLK) & (kj >= jnp.where(first, BLK, 0))


def _masked_fill(sink_ref, g, e):
    kj = lax.broadcasted_iota(jnp.int32, (STACK, 2 * BLK), 1)
    sink = jnp.concatenate([jnp.full((BLK, 2 * BLK), sink_ref[0, 2 * (4 * g + jj) + e], F32) for jj in range(4)], axis=0)
    return jnp.where(kj == 0, sink, NEG)


def _padded_pair(kvp_ref, kvc_ref, col, other=0.0):
    z = jnp.concatenate([kvp_ref[:, col:col + 128], kvc_ref[:, col:col + 128]], axis=0).astype(F32)
    z = jnp.where(lax.broadcasted_iota(jnp.int32, z.shape, 0) == 0, 0.0, z)
    zs = pltpu.roll(z, 64, 1)
    lo = lax.broadcasted_iota(jnp.int32, z.shape, 1) < 64
    fill = jnp.full_like(z, other)
    left = [jnp.where(lo, z, fill).astype(BF16), jnp.where(lo, zs, fill).astype(BF16)]
    right = [jnp.where(lo, fill, zs).astype(BF16), jnp.where(lo, fill, z).astype(BF16)]
    return left, right


def _exp_logits(s, valid, fill):
    s = jnp.where(valid, s * (HEAD_DIM ** -0.5), fill)
    return jnp.exp(s - jnp.max(s, axis=-1, keepdims=True))


def _fwd_attn(sinks, q, kv, g3):
    t = q.shape[0]
    seq_blocks = SEQ_LEN // BLK

    def body(sink_ref, q_ref, kvc_ref, kvp_ref, za_ref, attn_ref, ub_ref):
        i = pl.program_id(0)
        valid = _band_mask(i % seq_blocks == 0)
        k_pad = _padded_pair(kvp_ref, kvc_ref, 0)
        v_one = _padded_pair(kvp_ref, kvc_ref, 128, other=1.0)
        lo = lax.broadcasted_iota(jnp.int32, (STACK, 128), 1) < 64
        for g in range(2):
            qg = jnp.concatenate([q_ref[:, j * 128:(j + 1) * 128] for j in range(4 * g, 4 * g + 4)], axis=0)
            pv = [_dot(_exp_logits(_dot_nt(qg, k_pad[e][g]), valid, _masked_fill(sink_ref, g, e)).astype(BF16),
                       v_one[e][g]) for e in range(2)]
            o = jnp.where(lo, pv[0], pv[1]) / pltpu.roll(jnp.where(lo, pv[1], pv[0]), 64, 1)
            for jj in range(4):
                cols = slice((4 * g + jj) * 128, (4 * g + jj + 1) * 128)
                oj = o[jj * BLK:(jj + 1) * BLK, :]
                attn_ref[:, cols] = oj
                za = za_ref[:, cols]
                ub_ref[:, cols] = (za * _sig(za) * oj).astype(BF16)

    return pl.pallas_call(
        body, name="fwd_attn", grid=(t // BLK,),
        in_specs=[pl.BlockSpec(memory_space=pltpu.SMEM), _row_spec(BLK, D), _row_spec(BLK, 256),
                  pl.BlockSpec((BLK, 256), lambda i: (jnp.maximum(i - 1, 0), 0)), _row_spec(BLK, D, 0)],
        out_specs=[_row_spec(BLK, D), _row_spec(BLK, D)],
        out_shape=[jax.ShapeDtypeStruct((t, D), F32), jax.ShapeDtypeStruct((t, D), BF16)],
        compiler_params=_params("parallel"),
    )(sinks, q, kv, kv, g3)


def _fwd_out_bwd_head(ya, ub, g3, x, target, g_post, wpa, wout, tm, parts=1):
    t = x.shape[0]

    def body(ya_ref, ub_ref, ga_ref, gb_ref, x_ref, tgt_ref, gp_ref, wpa_ref, wout_ref,
             loss_ref, dout_ref, dya_ref, dub_ref, dgab_ref, dwout_ref, dwpa_ref, dgp_ref):
        @pl.when(pl.program_id(0) == 0)
        def _():
            loss_ref[...] = jnp.zeros_like(loss_ref)
            dwout_ref[...] = jnp.zeros_like(dwout_ref)
            dwpa_ref[...] = jnp.zeros_like(dwpa_ref)
            dgp_ref[...] = jnp.zeros_like(dgp_ref)

        g = gp_ref[...]
        sq = jnp.zeros((1, 1), F32)
        dgp = jnp.zeros((1, D), F32)
        mbs, dys, dybs = [], [], []
        for part in range(parts):
            rows = slice(part * (tm // parts), (part + 1) * (tm // parts))
            ub = ub_ref[rows, :]
            ya = ya_ref[rows, :]
            yb = _dot(ub, wpa_ref[...])
            sa = _sig(ga_ref[rows, :])
            sb = _sig(gb_ref[rows, :])
            mb = (sa * ya + sb * yb).astype(BF16)
            y = _dot(mb, wout_ref[...])
            r = lax.rsqrt(jnp.mean(y * y, axis=-1, keepdims=True) + RMS_EPS)
            n = y * r
            err = (x_ref[rows, :] + n * g) - tgt_ref[rows, :]
            sq = sq + jnp.sum(jnp.sum(err * err, axis=0, keepdims=True), axis=1, keepdims=True)
            dout = err * (1.0 / D)
            dout_ref[rows, :] = dout
            dgp = dgp + jnp.sum(dout * n, axis=0, keepdims=True)
            dn = dout * g
            dy = (r * (dn - n * jnp.mean(dn * n, axis=-1, keepdims=True))).astype(BF16)
            dm = _dot_nt(dy, wout_ref[...])
            dya_ref[rows, :] = (dm * sa).astype(BF16)
            dyb = (dm * sb).astype(BF16)
            dgab_ref[rows, 0:D] = (dm * ya * (sa * (1.0 - sa))).astype(BF16)
            dgab_ref[rows, D:2 * D] = (dm * yb * (sb * (1.0 - sb))).astype(BF16)
            dub_ref[rows, :] = _dot_nt(dyb, wpa_ref[...])
            mbs.append(mb)
            dys.append(dy)
            dybs.append(dyb)
        loss_ref[...] += sq * (0.5 / D)
        dgp_ref[0:1, :] += dgp
        dwout_ref[...] += _dot_tn(jnp.concatenate(mbs, axis=0), jnp.concatenate(dys, axis=0))
        dwpa_ref[...] += _dot_tn(ub_ref[...], jnp.concatenate(dybs, axis=0))

    return pl.pallas_call(
        body, name="fwd_out_bwd_head", grid=(t // tm,),
        in_specs=[_row_spec(tm, D), _row_spec(tm, D), _row_spec(tm, D, 1), _row_spec(tm, D, 2),
                  _row_spec(tm, D), _row_spec(tm, D), pl.BlockSpec((1, D), lambda i: (0, 0)),
                  _whole_vmem(), _whole_vmem()],
        out_specs=[pl.BlockSpec((8, 128), lambda i: (0, 0)), _row_spec(tm, D), _row_spec(tm, D), _row_spec(tm, D),
                   _row_spec(tm, 2 * D), _whole_vmem(), _whole_vmem(), pl.BlockSpec((8, D), lambda i: (0, 0))],
        out_shape=[jax.ShapeDtypeStruct((8, 128), F32), jax.ShapeDtypeStruct((t, D), F32),
                   jax.ShapeDtypeStruct((t, D), BF16), jax.ShapeDtypeStruct((t, D), F32),
                   jax.ShapeDtypeStruct((t, 2 * D), BF16), jax.ShapeDtypeStruct((D, D), F32),
                   jax.ShapeDtypeStruct((D, D), F32), jax.ShapeDtypeStruct((8, D), F32)],
        compiler_params=_params("arbitrary"),
    )(ya, ub, g3, g3, x, target, g_post, wpa, wout)


def _bwd_attn(sinks, q, kv, attn, dub, g3, cos_t, sin_t):
    t = q.shape[0]
    seq_blocks = SEQ_LEN // BLK

    def body(sink_ref, q_ref, kvc_ref, kvp_ref, attn_ref, dub_ref, za_ref, c_ref, s_ref,
             dq_ref, dza_ref, dkv_own_ref, dkv_prev_ref, dsink_ref):
        i = pl.program_id(0)

        @pl.when(i == 0)
        def _():
            dsink_ref[...] = jnp.zeros_like(dsink_ref)

        valid = _band_mask(i % seq_blocks == 0)
        k_pad = _padded_pair(kvp_ref, kvc_ref, 0)
        v_pad = _padded_pair(kvp_ref, kvc_ref, 128)
        lo = lax.broadcasted_iota(jnp.int32, (STACK, 128), 1) < 64
        lane8 = lax.broadcasted_iota(jnp.int32, (8, 128), 1)
        c = c_ref[...]
        s = s_ref[...]
        dk_acc, dv_acc = [], []
        dsink = jnp.zeros((8, 128), F32)
        for g in range(2):
            qg, dog = [], []
            for j in range(4 * g, 4 * g + 4):
                cols = slice(j * 128, (j + 1) * 128)
                za = za_ref[:, cols]
                sg = _sig(za)
                dub = dub_ref[:, cols]
                dza_ref[:, cols] = (dub * attn_ref[:, cols] * (sg * (1.0 + za * (1.0 - sg)))).astype(BF16)
                dog.append((dub * (za * sg)).astype(BF16))
                qg.append(q_ref[:, cols])
            qg = jnp.concatenate(qg, axis=0)
            dog = jnp.concatenate(dog, axis=0)
            dq = jnp.zeros((STACK, 128), F32)
            ds_both, p_both = [], []
            for e in range(2):
                p = _exp_logits(_dot_nt(qg, k_pad[e][g]), valid, _masked_fill(sink_ref, g, e))
                p = p / jnp.sum(p, axis=-1, keepdims=True)
                dp = _dot_nt(dog, v_pad[e][g])
                ds = p * (dp - jnp.sum(p * dp, axis=-1, keepdims=True))
                for jj in range(4):
                    tot = jnp.sum(ds[jj * BLK:(jj + 1) * BLK, 0:1], axis=0, keepdims=True)
                    dsink = dsink + jnp.where(lane8 == 2 * (4 * g + jj) + e, tot, 0.0)
                ds = (ds * (HEAD_DIM ** -0.5)).astype(BF16)
                dq = dq + _dot(ds, k_pad[e][g])
                ds_both.append(ds)
                p_both.append(p.astype(BF16))
            zero = jnp.zeros_like(qg)
            q2 = jnp.concatenate([jnp.where(lo, qg, zero), jnp.where(lo, zero, qg)], axis=0)
            do2 = jnp.concatenate([jnp.where(lo, dog, zero), jnp.where(lo, zero, dog)], axis=0)
            dk_acc.append(_dot_tn(jnp.concatenate(ds_both, axis=0), q2))
            dv_acc.append(_dot_tn(jnp.concatenate(p_both, axis=0), do2))
            for jj in range(4):
                cols = slice((4 * g + jj) * 128, (4 * g + jj + 1) * 128)
                dqj = dq[jj * BLK:(jj + 1) * BLK, :]
                dq_ref[:, cols] = (dqj * c - _swap_halves(dqj) * s).astype(BF16)
        dsink_ref[...] += dsink
        lo2 = lax.broadcasted_iota(jnp.int32, (2 * BLK, 128), 1) < 64
        sink_row = lax.broadcasted_iota(jnp.int32, (2 * BLK, 128), 0) == 0
        for col, acc in ((0, dk_acc), (128, dv_acc)):
            both = jnp.where(lo2, acc[0] + pltpu.roll(acc[0], 64, 1), acc[1] + pltpu.roll(acc[1], 64, 1))
            both = jnp.where(sink_row, 0.0, both)
            dkv_prev_ref[:, col:col + 128] = both[0:BLK, :]
            dkv_own_ref[:, col:col + 128] = both[BLK:2 * BLK, :]

    tab = pl.BlockSpec((BLK, 128), lambda i: (i % seq_blocks, 0))
    return pl.pallas_call(
        body, name="bwd_attn", grid=(t // BLK,),
        in_specs=[pl.BlockSpec(memory_space=pltpu.SMEM), _row_spec(BLK, D), _row_spec(BLK, 256),
                  pl.BlockSpec((BLK, 256), lambda i: (jnp.maximum(i - 1, 0), 0)),
                  _row_spec(BLK, D), _row_spec(BLK, D), _row_spec(BLK, D, 0), tab, tab],
        out_specs=[_row_spec(BLK, D), _row_spec(BLK, D), _row_spec(BLK, 256), _row_spec(BLK, 256),
                   pl.BlockSpec((8, 128), lambda i: (0, 0))],
        out_shape=[jax.ShapeDtypeStruct((t, D), BF16), jax.ShapeDtypeStruct((t, D), BF16),
                   jax.ShapeDtypeStruct((t, 256), F32), jax.ShapeDtypeStruct((t, 256), F32),
                   jax.ShapeDtypeStruct((8, 128), F32)],
        compiler_params=_params("arbitrary"),
    )(sinks, q, kv, kv, attn, dub, g3, cos_t, sin_t)


def _bwd_kv_finish(dkv_own, dkv_prev, cos_t, sin_t):
    t = dkv_own.shape[0]
    tm = 512
    seq_tiles = SEQ_LEN // tm
    n_blocks = t // BLK

    def body(own_ref, same_ref, nxt_ref, c_ref, s_ref, out_ref):
        keep = jnp.where(pl.program_id(0) % seq_tiles == seq_tiles - 1, 0.0, 1.0)
        shifted = jnp.concatenate([same_ref[BLK:tm, :], nxt_ref[...] * keep], axis=0)
        tot = own_ref[...] + shifted
        dk = tot[:, 0:128]
        out_ref[:, 0:128] = (dk * c_ref[...] - _swap_halves(dk) * s_ref[...]).astype(BF16)
        out_ref[:, 128:256] = tot[:, 128:256].astype(BF16)

    tab = pl.BlockSpec((tm, 128), lambda i: (i % seq_tiles, 0))
    return pl.pallas_call(
        body, name="bwd_kv_finish", grid=(t // tm,),
        in_specs=[_row_spec(tm, 256), _row_spec(tm, 256),
                  pl.BlockSpec((BLK, 256), lambda i: (jnp.minimum((i + 1) * (tm // BLK), n_blocks - 1), 0)), tab, tab],
        out_specs=_row_spec(tm, 256),
        out_shape=jax.ShapeDtypeStruct((t, 256), BF16),
        compiler_params=_params("parallel"),
    )(dkv_own, dkv_prev, dkv_prev, cos_t, sin_t)


def _bwd_conv(dya, a4, wconv8, wpc, tm):
    t = a4.shape[0]
    seq_tiles = SEQ_LEN // tm
    last8 = t // 8 - 1
    last16 = t // 16 - 1

    def body(dya_ref, xc_ref, bg_ref, cg_ref, zc_ref, xcp_ref, cgp_ref, dyan_ref, bgn_ref, zcn_ref,
             w_ref, wpc_ref, da4_ref, dwpc_ref, dwc_ref):
        i = pl.program_id(0)

        @pl.when(i == 0)
        def _():
            dwpc_ref[...] = jnp.zeros_like(dwpc_ref)
            dwc_ref[...] = jnp.zeros_like(dwc_ref)

        keep_prev = jnp.where(i % seq_tiles == 0, 0.0, 1.0)
        keep_next = jnp.where(i % seq_tiles == seq_tiles - 1, 0.0, 1.0)
        up6 = cgp_ref[6:7, :] * xcp_ref[6:7, :] * keep_prev
        up7 = cgp_ref[7:8, :] * xcp_ref[7:8, :] * keep_prev
        xc = xc_ref[...]
        bg = bg_ref[...]
        cg = cg_ref[...]
        zc = zc_ref[...]
        u, u_m1, u_m2, yconv, sg, sz, co = _conv_forward(xc, bg, cg, zc, up6, up7, w_ref)
        dya = dya_ref[...]
        dwpc_ref[...] += _dot_tn((sz * co).astype(BF16), dya)
        dua = _dot_nt(dya, wpc_ref[...])
        da4_ref[:, 3 * D:4 * D] = (dua * co * (sg * (1.0 + zc * (1.0 - sg)))).astype(BF16)
        dco = dua * sz
        da4_ref[:, D:2 * D] = (dco * yconv).astype(BF16)
        dyc = dco * bg
        dwc_ref[0:1, :] += jnp.sum(dyc * u_m2, axis=0, keepdims=True)
        dwc_ref[1:2, :] += jnp.sum(dyc * u_m1, axis=0, keepdims=True)
        dwc_ref[2:3, :] += jnp.sum(dyc * u, axis=0, keepdims=True)
        zcn = zcn_ref[...]
        dyc_n = _dot_nt(dyan_ref[...], wpc_ref[...])[0:8, :] * (zcn * _sig(zcn)) * bgn_ref[...] * keep_next
        rows = lax.broadcasted_iota(jnp.int32, xc.shape, 0)
        n0 = dyc_n[0:1, :]
        n1 = dyc_n[1:2, :]
        dyc_p1 = jnp.where(rows == tm - 1, n0, pltpu.roll(dyc, tm - 1, 0))
        dyc_p2 = jnp.where(rows == tm - 2, n0, jnp.where(rows == tm - 1, n1, pltpu.roll(dyc, tm - 2, 0)))
        du = w_ref[2:3, :] * dyc + w_ref[1:2, :] * dyc_p1 + w_ref[0:1, :] * dyc_p2
        da4_ref[:, 0:D] = (du * cg).astype(BF16)
        da4_ref[:, 2 * D:3 * D] = (du * xc).astype(BF16)

    def prev(col):
        return pl.BlockSpec((8, D), lambda i: (jnp.maximum(i * (tm // 8) - 1, 0), col))

    def nxt(col):
        return pl.BlockSpec((8, D), lambda i: (jnp.minimum((i + 1) * (tm // 8), last8), col))

    return pl.pallas_call(
        body, name="bwd_conv", grid=(t // tm,),
        in_specs=[_row_spec(tm, D), _row_spec(tm, D, 0), _row_spec(tm, D, 1), _row_spec(tm, D, 2), _row_spec(tm, D, 3),
                  prev(0), prev(2),
                  pl.BlockSpec((16, D), lambda i: (jnp.minimum((i + 1) * (tm // 16), last16), 0)), nxt(1), nxt(3),
                  pl.BlockSpec((8, D), lambda i: (0, 0)), _whole_vmem()],
        out_specs=[_row_spec(tm, 4 * D), _whole_vmem(), pl.BlockSpec((8, D), lambda i: (0, 0))],
        out_shape=[jax.ShapeDtypeStruct((t, 4 * D), BF16), jax.ShapeDtypeStruct((D, D), F32),
                   jax.ShapeDtypeStruct((8, D), F32)],
        compiler_params=_params("arbitrary"),
    )(dya, a4, a4, a4, a4, a4, a4, dya, a4, a4, wconv8, wpc)


def _bwd_dh(da4, dq, dkv, dza, dgab, wt, x, g_pre, dout, tm):
    t = x.shape[0]

    def body(da4_ref, dq_ref, dkv_ref, dza_ref, dgab_ref, w_ref, x_ref, g_ref, dout_ref, gx_ref, dg_ref):
        @pl.when(pl.program_id(0) == 0)
        def _():
            dg_ref[...] = jnp.zeros_like(dg_ref)

        dh = _dot(da4_ref[...], w_ref[0:ROW_Q, :])
        dh += _dot(dq_ref[...], w_ref[ROW_Q:ROW_KV, :])
        dh += _dot(dkv_ref[...], w_ref[ROW_KV:ROW_ZA, :])
        dh += _dot(dza_ref[...], w_ref[ROW_ZA:ROW_GA, :])
        dh += _dot(dgab_ref[...], w_ref[ROW_GA:D_IN, :])
        xf = x_ref[...]
        r = lax.rsqrt(jnp.mean(xf * xf, axis=-1, keepdims=True) + RMS_EPS)
        xn = xf * r
        dg_ref[0:1, :] += jnp.sum(dh * xn, axis=0, keepdims=True)
        dxn = dh * g_ref[...]
        gx_ref[...] = dout_ref[...] + r * (dxn - xn * jnp.mean(dxn * xn, axis=-1, keepdims=True))

    return pl.pallas_call(
        body, name="bwd_dh", grid=(t // tm,),
        in_specs=[_row_spec(tm, 4 * D), _row_spec(tm, D), _row_spec(tm, 256), _row_spec(tm, D), _row_spec(tm, 2 * D),
                  _whole_vmem(), _row_spec(tm, D), pl.BlockSpec((1, D), lambda i: (0, 0)), _row_spec(tm, D)],
        out_specs=[_row_spec(tm, D), pl.BlockSpec((8, D), lambda i: (0, 0))],
        out_shape=[jax.ShapeDtypeStruct((t, D), F32), jax.ShapeDtypeStruct((8, D), F32)],
        compiler_params=_params("arbitrary"),
    )(da4, dq, dkv, dza, dgab, wt, x, g_pre, dout)


def _bwd_dw_in(h, piece, row0, nb, tm, name, prev):
    t, n = piece.shape
    n_t = t // tm

    def body(*refs):
        h_ref, p_ref = refs[0], refs[1]
        o32_ref, o16_ref, acc_ref, acc16_ref, sems = refs[-5:]
        j, i = pl.program_id(0), pl.program_id(1)

        @pl.when(i == 0)
        def _():
            acc_ref[...] = jnp.zeros_like(acc_ref)

        acc_ref[...] += _dot_tn(p_ref[...], h_ref[...])

        @pl.when(i == n_t - 1)
        def _():
            acc16_ref[...] = acc_ref[...].astype(BF16)
            rows = pl.ds(pl.multiple_of(row0 + j * nb, 16), nb)
            c32 = pltpu.make_async_copy(acc_ref, o32_ref.at[rows], sems.at[0])
            c16 = pltpu.make_async_copy(acc16_ref, o16_ref.at[rows], sems.at[1])
            c32.start()
            c16.start()
            c32.wait()
            c16.wait()

    hbm = pl.BlockSpec(memory_space=pl.ANY)
    carried = [] if prev is None else list(prev)
    return pl.pallas_call(
        body, name=name, grid=(n // nb, n_t),
        in_specs=[pl.BlockSpec((tm, D), lambda j, i: (i, 0)), pl.BlockSpec((tm, nb), lambda j, i: (i, j))]
        + [hbm] * len(carried),
        out_specs=[hbm, hbm],
        out_shape=[jax.ShapeDtypeStruct((D_IN, D), F32), jax.ShapeDtypeStruct((D_IN, D), BF16)],
        scratch_shapes=[pltpu.VMEM((nb, D), F32), pltpu.VMEM((nb, D), BF16), pltpu.SemaphoreType.DMA((2,))],
        input_output_aliases={2: 0, 3: 1} if carried else {},
        compiler_params=_params("arbitrary", "arbitrary"),
    )(h, piece, *carried)


def _place():
    x, y, c = lax.axis_index("x"), lax.axis_index("y"), lax.axis_index("c")
    return x, y, c, 4 * x + 2 * y + c


def _peer(x, y, c, k):
    return (1 - x if k & 4 else x, 1 - y if k & 2 else y, 1 - c if k & 1 else c)


ICI_MASKS = (4, 2, 6)


def _all_gather(shards):
    n = len(shards)

    def body(*refs):
        src, dst = refs[:n], refs[n:2 * n]
        send_sems, recv_sems, local_sems = refs[2 * n:]
        x, y, c, me = _place()
        sibling = _peer(x, y, c, 1)

        def copy(a, s, block, to, own=False):
            return pltpu.make_async_remote_copy(
                src_ref=src[a] if own else dst[a].at[block], dst_ref=dst[a].at[block],
                send_sem=send_sems.at[a * 7 + s], recv_sem=recv_sems.at[a * 7 + s], device_id=to, device_id_type=MESH_ID)

        local = [pltpu.make_async_copy(src[a], dst[a].at[me], local_sems.at[a]) for a in range(n)]
        for cp in local:
            cp.start()
        started = [copy(a, 0, me, sibling, own=True) for a in range(n)]
        started += [copy(a, 1 + j, me, _peer(x, y, c, k), own=True) for j, k in enumerate(ICI_MASKS) for a in range(n)]
        for cp in started:
            cp.start()
        for j, k in enumerate(ICI_MASKS):
            for a in range(n):
                copy(a, 1 + j, me ^ k, sibling).wait_recv()
                fwd = copy(a, 4 + j, me ^ k, sibling)
                fwd.start()
                started.append(fwd)
        for a in range(n):
            copy(a, 0, me ^ 1, sibling).wait_recv()
        for j, k in enumerate(ICI_MASKS):
            for a in range(n):
                copy(a, 4 + j, me ^ 1 ^ k, sibling).wait_recv()
        for cp in started:
            cp.wait_send()
        for cp in local:
            cp.wait()

    hbm = pl.BlockSpec(memory_space=pl.ANY)
    return pl.pallas_call(
        body, name="all_gather_weights",
        in_specs=[hbm] * n, out_specs=[hbm] * n,
        out_shape=[jax.ShapeDtypeStruct((N_DEV,) + s.shape, s.dtype) for s in shards],
        scratch_shapes=[pltpu.SemaphoreType.DMA((7 * n,)), pltpu.SemaphoreType.DMA((7 * n,)),
                        pltpu.SemaphoreType.DMA((n,))],
    )(*shards)


def _direct_copies(src, land, send_sems, recv_sems):
    x, y, c, me = _place()
    return [pltpu.make_async_remote_copy(
        src_ref=src[a], dst_ref=land[a].at[me], send_sem=send_sems.at[a * 7 + k - 1],
        recv_sem=recv_sems.at[a * 7 + k - 1], device_id=_peer(x, y, c, k), device_id_type=MESH_ID)
        for k in range(1, N_DEV) for a in range(len(src))]


def _gather_start(shards):
    n = len(shards)

    def body(*refs):
        src, land = refs[:n], refs[n:2 * n]
        send_sems, recv_sems = refs[2 * n], refs[2 * n + 1]
        token_ref = refs[-1]
        for cp in _direct_copies(src, land, send_sems, recv_sems):
            cp.start()
        token_ref[...] = jnp.zeros_like(token_ref)

    hbm = pl.BlockSpec(memory_space=pltpu.HBM)
    sem = pl.BlockSpec(memory_space=pltpu.SEMAPHORE)
    lands = [lax.empty((N_DEV,) + s.shape, s.dtype) for s in shards]
    out = pl.pallas_call(
        body, name="gather_squares_start",
        out_shape=(pltpu.SemaphoreType.DMA((7 * n,)), pltpu.SemaphoreType.DMA((7 * n,)),
                   *[pltpu.HBM(s.shape, s.dtype) for s in shards], *[pltpu.HBM(s.shape, s.dtype) for s in lands],
                   jax.ShapeDtypeStruct((8, 128), F32)),
        in_specs=[hbm] * (2 * n), out_specs=(sem, sem, *[hbm] * (2 * n), _whole_vmem()),
        input_output_aliases={a: 2 + a for a in range(2 * n)},
        compiler_params=pltpu.CompilerParams(has_side_effects=pltpu.SideEffectType.DATAFLOW_SIDE_EFFECTING),
    )(*[pltpu.with_memory_space_constraint(s, pltpu.HBM) for s in list(shards) + lands])
    return out[0], out[1], out[2:2 + n], out[2 + n:2 + 2 * n], out[-1]


def _gather_wait(send_sems, recv_sems, flying, lands, after):
    n = len(flying)

    def body(*refs):
        src, land = refs[:n], refs[n:2 * n]
        for cp in _direct_copies(src, land, refs[2 * n], refs[2 * n + 1]):
            cp.wait_send()
            cp.wait_recv()

    hbm = pl.BlockSpec(memory_space=pltpu.HBM)
    sem = pl.BlockSpec(memory_space=pltpu.SEMAPHORE)
    out = pl.pallas_call(
        body, name="gather_squares_wait",
        out_shape=tuple(pltpu.HBM(s.shape, s.dtype) for s in list(flying) + list(lands)),
        in_specs=[hbm] * (2 * n) + [sem, sem, pl.BlockSpec(memory_space=pl.ANY)], out_specs=tuple([hbm] * (2 * n)),
        input_output_aliases={a: a for a in range(2 * n)},
        compiler_params=pltpu.CompilerParams(has_side_effects=pltpu.SideEffectType.DATAFLOW_SIDE_EFFECTING),
    )(*flying, *lands, send_sems, recv_sems, after)
    return out[n:]


def _exchange_sibling(by_dest):
    n = len(by_dest)

    def body(*refs):
        src, dst = refs[:n], refs[n:2 * n]
        send_sems, recv_sems = refs[2 * n:]
        x, y, c, _ = _place()
        sibling = _peer(x, y, c, 1)
        copies = [pltpu.make_async_remote_copy(
            src_ref=src[a].at[2 * p + (1 - c)], dst_ref=dst[a].at[p], send_sem=send_sems.at[a * 4 + p],
            recv_sem=recv_sems.at[a * 4 + p], device_id=sibling, device_id_type=MESH_ID)
            for a in range(n) for p in range(4)]
        for cp in copies:
            cp.start()
        for cp in copies:
            cp.wait_recv()
        for cp in copies:
            cp.wait_send()

    hbm = pl.BlockSpec(memory_space=pl.ANY)
    return pl.pallas_call(
        body, name="exchange_sibling", in_specs=[hbm] * n, out_specs=[hbm] * n,
        out_shape=[jax.ShapeDtypeStruct((4,) + s.shape[1:], s.dtype) for s in by_dest],
        scratch_shapes=[pltpu.SemaphoreType.DMA((4 * n,)), pltpu.SemaphoreType.DMA((4 * n,))],
    )(*by_dest)


def _gather_small(small):
    def body(src, dst, send_sems, recv_sems, local_sem):
        x, y, c, me = _place()
        local = pltpu.make_async_copy(src, dst.at[me], local_sem)
        local.start()
        to_all = [pltpu.make_async_remote_copy(
            src_ref=src, dst_ref=dst.at[me], send_sem=send_sems.at[k - 1], recv_sem=recv_sems.at[k - 1],
            device_id=_peer(x, y, c, k), device_id_type=MESH_ID) for k in range(1, N_DEV)]
        for cp in to_all:
            cp.start()
        for k in range(1, N_DEV):
            pltpu.make_async_remote_copy(
                src_ref=src, dst_ref=dst.at[me ^ k], send_sem=send_sems.at[k - 1], recv_sem=recv_sems.at[k - 1],
                device_id=_peer(x, y, c, k), device_id_type=MESH_ID).wait_recv()
        for cp in to_all:
            cp.wait_send()
        local.wait()

    hbm = pl.BlockSpec(memory_space=pl.ANY)
    return pl.pallas_call(
        body, name="gather_small", in_specs=[hbm], out_specs=hbm,
        out_shape=jax.ShapeDtypeStruct((N_DEV,) + small.shape, small.dtype),
        scratch_shapes=[pltpu.SemaphoreType.DMA((7,)), pltpu.SemaphoreType.DMA((7,)), pltpu.SemaphoreType.DMA],
    )(small)


def _chip_copies(src, land, send_sems, recv_sems):
    x, y, c, _ = _place()
    chip = 2 * x + y
    return [pltpu.make_async_remote_copy(
        src_ref=src[a].at[chip ^ (k >> 1)], dst_ref=land[a].at[j], send_sem=send_sems.at[a * 3 + j],
        recv_sem=recv_sems.at[a * 3 + j], device_id=_peer(x, y, c, k), device_id_type=MESH_ID)
        for j, k in enumerate(ICI_MASKS) for a in range(len(src))]


def _exchange_chips_start(by_chip):
    n = len(by_chip)

    def body(*refs):
        src, land = refs[:n], refs[n:2 * n]
        send_sems, recv_sems = refs[2 * n], refs[2 * n + 1]
        token_ref = refs[-1]
        for cp in _chip_copies(src, land, send_sems, recv_sems):
            cp.start()
        token_ref[...] = jnp.zeros_like(token_ref)

    hbm = pl.BlockSpec(memory_space=pltpu.HBM)
    sem = pl.BlockSpec(memory_space=pltpu.SEMAPHORE)
    lands = [lax.empty((3,) + s.shape[1:], s.dtype) for s in by_chip]
    out = pl.pallas_call(
        body, name="exchange_chips_start",
        out_shape=(pltpu.SemaphoreType.DMA((3 * n,)), pltpu.SemaphoreType.DMA((3 * n,)),
                   *[pltpu.HBM(s.shape, s.dtype) for s in by_chip], *[pltpu.HBM(s.shape, s.dtype) for s in lands],
                   jax.ShapeDtypeStruct((8, 128), F32)),
        in_specs=[hbm] * (2 * n), out_specs=(sem, sem, *[hbm] * (2 * n), _whole_vmem()),
        input_output_aliases={a: 2 + a for a in range(2 * n)},
        compiler_params=pltpu.CompilerParams(has_side_effects=pltpu.SideEffectType.DATAFLOW_SIDE_EFFECTING),
    )(*[pltpu.with_memory_space_constraint(s, pltpu.HBM) for s in list(by_chip) + lands])
    return out[0], out[1], out[2:2 + n], out[2 + n:2 + 2 * n], out[-1]


def _exchange_chips_wait(send_sems, recv_sems, flying, lands, after):
    n = len(flying)

    def body(*refs):
        src, land = refs[:n], refs[n:2 * n]
        send_sems_ref, recv_sems_ref = refs[2 * n], refs[2 * n + 1]
        for cp in _chip_copies(src, land, send_sems_ref, recv_sems_ref):
            cp.wait_send()
            cp.wait_recv()

    hbm = pl.BlockSpec(memory_space=pltpu.HBM)
    sem = pl.BlockSpec(memory_space=pltpu.SEMAPHORE)
    out = pl.pallas_call(
        body, name="exchange_chips_wait",
        out_shape=tuple(pltpu.HBM(s.shape, s.dtype) for s in list(flying) + list(lands)),
        in_specs=[hbm] * (2 * n) + [sem, sem, pl.BlockSpec(memory_space=pl.ANY)], out_specs=tuple([hbm] * (2 * n)),
        input_output_aliases={a: a for a in range(2 * n)},
        compiler_params=pltpu.CompilerParams(has_side_effects=pltpu.SideEffectType.DATAFLOW_SIDE_EFFECTING),
    )(*flying, *lands, send_sems, recv_sems, after)
    return out[n:]


def _adamw_math(w, g, m, v):
    m = ADAM_B1 * m + (1.0 - ADAM_B1) * g
    v = ADAM_B2 * v + (1.0 - ADAM_B2) * (g * g)
    m_hat = m / (1.0 - ADAM_B1 ** ADAM_STEP)
    v_hat = v / (1.0 - ADAM_B2 ** ADAM_STEP)
    return -ADAM_LR * (m_hat / (jnp.sqrt(v_hat) + ADAM_EPS) + ADAM_WD * w), m, v


def _pair_sum(owns, recvs, c_arr, tr, name):
    n = len(owns)
    _, rows, cols = owns[0].shape

    def body(c_ref, *refs):
        for a in range(n):
            s = refs[a][...] + refs[n + a][...].astype(F32)
            refs[2 * n + a][...] = s
            refs[3 * n + a][...] = s.astype(BF16)

    by_chip = pl.BlockSpec((None, tr, cols), lambda p, i, c_ref: (p, i, 0))
    mine = pl.BlockSpec((None, tr, cols), lambda p, i, c_ref: (2 * p + c_ref[0], i, 0))
    out = pl.pallas_call(
        body, name=name,
        grid_spec=pltpu.PrefetchScalarGridSpec(
            num_scalar_prefetch=1, grid=(4, rows // tr), in_specs=[mine] * n + [by_chip] * n, out_specs=[by_chip] * (2 * n)),
        out_shape=[jax.ShapeDtypeStruct((4, rows, cols), F32)] * n + [jax.ShapeDtypeStruct((4, rows, cols), BF16)] * n,
        compiler_params=_params("parallel", "parallel"),
    )(c_arr, *owns, *recvs)
    return out[:n], out[n:]


def _chip_sum(pairs, recvs, chip_arr, tr, name, adam=None):
    n = len(pairs)
    _, rows, cols = pairs[0].shape
    n_state = 0 if adam is None else 3 * n

    def body(chip_ref, *refs):
        outs = refs[2 * n + n_state:]
        for a in range(n):
            g = refs[a][...]
            for j in range(3):
                g = g + refs[n + a][j].astype(F32)
            outs[a][...] = g
            if adam is not None:
                w_ref, m_ref, v_ref = (refs[2 * n + s * n + a] for s in range(3))
                outs[n + a][...], outs[2 * n + a][...], outs[3 * n + a][...] = _adamw_math(w_ref[...], g, m_ref[...], v_ref[...])

    blk = pl.BlockSpec((tr, cols), lambda i, chip_ref: (i, 0))
    n_out = n if adam is None else 4 * n
    out = pl.pallas_call(
        body, name=name,
        grid_spec=pltpu.PrefetchScalarGridSpec(
            num_scalar_prefetch=1, grid=(rows // tr,),
            in_specs=[pl.BlockSpec((None, tr, cols), lambda i, chip_ref: (chip_ref[0], i, 0))] * n
            + [pl.BlockSpec((3, tr, cols), lambda i, chip_ref: (0, i, 0))] * n + [blk] * n_state,
            out_specs=[blk] * n_out),
        out_shape=[jax.ShapeDtypeStruct((rows, cols), F32)] * n_out,
        compiler_params=_params("parallel"),
    )(chip_arr, *pairs, *recvs, *([] if adam is None else [t for group in adam for t in group]))
    return out if adam is None else (out[:n], out[n:2 * n], out[2 * n:3 * n], out[3 * n:])


def _adamw(ws, gs, ms, vs, name, tr=None):
    n = len(ws)

    def body(*refs):
        for a in range(n):
            w_ref, g_ref, m_ref, v_ref = (refs[s * n + a] for s in range(4))
            refs[4 * n + a][...], refs[5 * n + a][...], refs[6 * n + a][...] = _adamw_math(
                w_ref[...], g_ref[...], m_ref[...], v_ref[...])

    out_shape = [jax.ShapeDtypeStruct(w.shape, F32) for w in ws] * 3
    if tr is None:
        out = pl.pallas_call(body, name=name, out_shape=out_shape)(*ws, *gs, *ms, *vs)
    else:
        blk = pl.BlockSpec((tr, ws[0].shape[1]), lambda i: (i, 0))
        out = pl.pallas_call(
            body, name=name, grid=(ws[0].shape[0] // tr,), in_specs=[blk] * (4 * n), out_specs=[blk] * (3 * n),
            out_shape=out_shape, compiler_params=_params("parallel"))(*ws, *gs, *ms, *vs)
    return out[:n], out[n:2 * n], out[2 * n:]


def _sum_small(small_all):
    def body(s_ref, o_ref):
        g = s_ref[0]
        for d in range(1, N_DEV):
            g = g + s_ref[d]
        o_ref[...] = g

    return pl.pallas_call(body, name="sum_small", out_shape=jax.ShapeDtypeStruct(small_all.shape[1:], F32))(small_all)


def _rope_tables():
    inv_freq = ROPE_THETA ** (-jnp.arange(0, HEAD_DIM, 2, dtype=F32) / HEAD_DIM)
    ang = jnp.arange(SEQ_LEN).astype(F32)[:, None] * inv_freq[None, :]
    cos, sin = jnp.cos(ang), jnp.sin(ang)
    return jnp.tile(cos, (1, 4)), jnp.tile(jnp.concatenate([-sin, sin], axis=1), (1, 2))


def _local_step(x, target, g_pre, g_post, sinks, wt, wconv, squares, start_exchange=None):
    cos_t, sin_t = _rope_tables()
    wconv8 = jnp.pad(wconv, ((0, 5), (0, 0)))
    h, a4 = _fwd_in_a(x, g_pre, wt, 512)
    q, kv, g3 = _fwd_in_b(h, wt, cos_t, sin_t, 512)
    wpc, wpa, wout = squares(kv)
    ya = _fwd_conv(a4, wconv8, wpc, 512)
    attn, ub = _fwd_attn(sinks, q, kv, g3)
    loss8, dout, dya, dub, dgab, dwout, dwpa, dgpost8 = _fwd_out_bwd_head(ya, ub, g3, x, target, g_post, wpa, wout, 256)
    dq, dza, dkv_own, dkv_prev, dsink8 = _bwd_attn(sinks, q, kv, attn, dub, g3, cos_t, sin_t)
    dkv = _bwd_kv_finish(dkv_own, dkv_prev, cos_t, sin_t)
    da4, dwpc, dwconv8 = _bwd_conv(dya, a4, wconv8, wpc, 512)
    dwt = _bwd_dw_in(h, da4, 0, 1024, 1024, "bwd_dw_in_conv", None)
    dwt = _bwd_dw_in(h, dq, ROW_Q, 1024, 1024, "bwd_dw_in_q", dwt)
    dwt = _bwd_dw_in(h, dkv, ROW_KV, 256, 1024, "bwd_dw_in_kv", dwt)
    dwt = _bwd_dw_in(h, dza, ROW_ZA, 1024, 1024, "bwd_dw_in_za", dwt)
    dwt32, dwt16 = _bwd_dw_in(h, dgab, ROW_GA, 1024, 1024, "bwd_dw_in_gates", dwt)
    token, pending = (None, None) if start_exchange is None else start_exchange(dwt32, dwt16, dwpc, dwpa, dwout)
    g_pre_after = g_pre if token is None else g_pre + token[0:1, 0:1]
    grad_x, dgpre8 = _bwd_dh(da4, dq, dkv, dza, dgab, wt, x, g_pre_after, dout, 256)
    small = jnp.concatenate([dgpre8, dgpost8, jnp.pad(dsink8, ((0, 0), (0, D - 128))), dwconv8], axis=0)
    return loss8[0, 0], grad_x, dwt32, dwt16, dwpc, dwpa, dwout, small, pending


def kernel(x, g_pre, g_post, w_in, w_conv, sinks, w_proj_conv, w_proj_attn, w_out, loss_target, m_g_pre, m_g_post, m_w_in, m_w_conv, m_sinks, m_w_proj_conv, m_w_proj_attn, m_w_out, v_g_pre, v_g_post, v_w_in, v_w_conv, v_sinks, v_w_proj_conv, v_w_proj_attn, v_w_out):
    batch = x.shape[0]
    mx, my, mc, me = _place()
    c_arr = jnp.reshape(mc, (1,)).astype(jnp.int32)
    chip_arr = jnp.reshape(2 * mx + my, (1,)).astype(jnp.int32)

    g_wt, g_conv = _all_gather([w_in[0].T.astype(BF16), jnp.pad(w_conv[0], ((0, 5), (0, 0)))])
    wt = g_wt.reshape(D_IN, D)
    wconv = g_conv[:, 0:3, :].transpose(1, 0, 2).reshape(3, D)
    sq_mine = [w.astype(BF16) for w in (w_proj_conv[0], w_proj_attn[0], w_out[0])]
    wt, sq_mine = lax.optimization_barrier((wt, sq_mine))
    sq_send, sq_recv, sq_flying, sq_lands, sq_token = _gather_start(sq_mine)

    def squares(after):
        got = _gather_wait(sq_send, sq_recv, sq_flying, sq_lands, after)
        return [lax.dynamic_update_index_in_dim(full, mine, me, 0).reshape(D, D) for full, mine in zip(got, sq_mine)]

    def start_exchange(dwt32, dwt16, dwpc, dwpa, dwout):
        own_sq = [g.reshape(N_DEV, SHARD_SQ, D) for g in (dwpc, dwpa, dwout)]
        own_in = dwt32.reshape(N_DEV, SHARD_IN, D)
        from_sibling = _exchange_sibling([dwt16.reshape(N_DEV, SHARD_IN, D)] + [g.astype(BF16) for g in own_sq])
        in32, in16 = _pair_sum([own_in], from_sibling[:1], c_arr, SHARD_IN // 2, "pair_sum_w_in")
        sq32, sq16 = _pair_sum(own_sq, from_sibling[1:], c_arr, SHARD_SQ, "pair_sum_squares")
        send_sems, recv_sems, flying, lands, token = _exchange_chips_start(list(in16) + list(sq16))
        return token, (send_sems, recv_sems, flying, lands, in32, sq32)

    loss, grad_x, _, _, _, _, _, small, pending = _local_step(
        x.reshape(batch * SEQ_LEN, D), loss_target.reshape(batch * SEQ_LEN, D), g_pre + sq_token[0:1, 0:1], g_post,
        sinks, wt, wconv, squares, start_exchange)
    loss = lax.psum(loss, ("x", "y", "c"))
    send_sems, recv_sems, flying, lands, in32, sq32 = pending
    from_chips = _exchange_chips_wait(send_sems, recv_sems, flying, lands, small)
    (g_wt_mine,) = _chip_sum(in32, from_chips[:1], chip_arr, SHARD_IN // 2, "chip_sum_w_in")
    g_in_mine = g_wt_mine.T
    gs = _sum_small(_gather_small(small))
    g_g_pre, g_g_post, g_sinks = gs[0:1], gs[8:9], gs[16:17, 0:N_HEADS]
    g_conv_mine = lax.dynamic_slice_in_dim(gs[24:27], me * SHARD_SQ, SHARD_SQ, axis=1)

    g_sq, d_sq, m_sq, v_sq = _chip_sum(
        sq32, from_chips[1:], chip_arr, SHARD_SQ, "chip_sum_adamw_squares",
        adam=([w_proj_conv[0], w_proj_attn[0], w_out[0]], [m_w_proj_conv[0], m_w_proj_attn[0], m_w_out[0]],
              [v_w_proj_conv[0], v_w_proj_attn[0], v_w_out[0]]))
    o_in = [o[0] for o in _adamw([w_in[0]], [g_in_mine], [m_w_in[0]], [v_w_in[0]], "adamw_w_in", tr=128)]
    o_small = _adamw([g_pre, g_post, sinks, w_conv[0]], [g_g_pre, g_g_post, g_sinks, g_conv_mine],
                     [m_g_pre, m_g_post, m_sinks, m_w_conv[0]], [v_g_pre, v_g_post, v_sinks, v_w_conv[0]], "adamw_small")

    grads = [g_g_pre, g_g_post, g_in_mine[None], g_conv_mine[None], g_sinks] + [g[None] for g in g_sq]
    rest = []
    for idx, sq in enumerate((d_sq, m_sq, v_sq)):
        gp, gq, sk, cv = o_small[idx]
        rest += [gp, gq, o_in[idx][None], cv[None], sk] + [s[None] for s in sq]
    return (loss, grad_x.reshape(batch, SEQ_LEN, D), *grads, *rest)
```

```python
import functools

import jax
import jax.numpy as jnp
from jax import lax
from jax.experimental import pallas as pl
from jax.experimental.pallas import tpu as pltpu

D = 1024
N_HEADS = 16
HEAD_DIM = 64
LOGIT_SCALE = HEAD_DIM ** -0.5
BLK = 128
SEQ_LEN = 2048
D_IN = 8448
ROW_Q, ROW_KV, ROW_ZA, ROW_GA = 4 * D, 5 * D, 5 * D + 256, 6 * D + 256
SHARD_IN = D_IN // 8
SHARD_SQ = D // 8
N_DEV = 8
ROPE_THETA = 10000.0
RMS_EPS = 1e-6
NEG = -1e30
ADAM_LR, ADAM_B1, ADAM_B2, ADAM_EPS, ADAM_WD, ADAM_STEP = 0.001, 0.9, 0.999, 1e-08, 0.01, 10

F32 = jnp.float32
BF16 = jnp.bfloat16
MESH_ID = pl.DeviceIdType.MESH


def _dot(a, b):
    return jnp.dot(a, b, preferred_element_type=F32)


def _dot_nt(a, b):
    return lax.dot_general(a, b, (((1,), (1,)), ((), ())), preferred_element_type=F32)


def _dot_tn(a, b):
    return lax.dot_general(a, b, (((0,), (0,)), ((), ())), preferred_element_type=F32)


def _sig(z):
    return 1.0 / (1.0 + jnp.exp(-z))


def _swap_halves(z):
    lane = lax.broadcasted_iota(jnp.int32, z.shape, 1)
    return jnp.where((lane & 63) < 32, pltpu.roll(z, 96, 1), pltpu.roll(z, 32, 1))


def _row_spec(tm, width, col=0):
    return pl.BlockSpec((tm, width), lambda i: (i, col))


def _whole_vmem():
    return pl.BlockSpec(memory_space=pltpu.VMEM)


def _params(*sem):
    return pltpu.CompilerParams(dimension_semantics=sem)


def _fwd_in_attn(x, g_pre, wt, cos_t, sin_t, tm):
    t = x.shape[0]
    seq_tiles = SEQ_LEN // tm

    def body(x_ref, g_ref, w_ref, c_ref, s_ref, h_ref, q_ref, kv_ref, g3_ref):
        xf = x_ref[...]
        r = lax.rsqrt(jnp.mean(xf * xf, axis=-1, keepdims=True) + RMS_EPS)
        hh = ((xf * r) * g_ref[...]).astype(BF16)
        h_ref[...] = hh
        c = c_ref[...]
        s = s_ref[...]

        def rope(z):
            return z * c + _swap_halves(z) * s

        q = _dot_nt(hh, w_ref[ROW_Q:ROW_Q + D, :])
        for j in range(D // 128):
            q_ref[:, j * 128:(j + 1) * 128] = (rope(q[:, j * 128:(j + 1) * 128]) * LOGIT_SCALE).astype(BF16)
        kv = _dot_nt(hh, w_ref[ROW_KV:ROW_KV + 256, :])
        kv_ref[:, 0:128] = rope(kv[:, 0:128]).astype(BF16)
        kv_ref[:, 128:256] = kv[:, 128:256].astype(BF16)
        for j in range(3):
            g3_ref[:, j * D:(j + 1) * D] = _dot_nt(hh, w_ref[ROW_ZA + j * D:ROW_ZA + (j + 1) * D, :])

    tab = pl.BlockSpec((tm, 128), lambda i: (i % seq_tiles, 0))
    return pl.pallas_call(
        body, name="fwd_in_attn", grid=(t // tm,),
        in_specs=[_row_spec(tm, D), pl.BlockSpec((1, D), lambda i: (0, 0)), _whole_vmem(), tab, tab],
        out_specs=[_row_spec(tm, D), _row_spec(tm, D), _row_spec(tm, 256), _row_spec(tm, 3 * D)],
        out_shape=[jax.ShapeDtypeStruct((t, D), BF16), jax.ShapeDtypeStruct((t, D), BF16),
                   jax.ShapeDtypeStruct((t, 256), BF16), jax.ShapeDtypeStruct((t, 3 * D), F32)],
        compiler_params=_params("parallel"),
    )(x, g_pre, wt, cos_t, sin_t)


def _conv_forward(xc, bg, cg, zc, up6, up7, w_ref):
    tm = xc.shape[0]
    rows = lax.broadcasted_iota(jnp.int32, xc.shape, 0)
    u = cg * xc
    u_m1 = jnp.where(rows == 0, up7, pltpu.roll(u, 1, 0))
    u_m2 = jnp.where(rows == 0, up6, jnp.where(rows == 1, up7, pltpu.roll(u, 2, 0)))
    yconv = w_ref[0:1, :] * u_m2 + w_ref[1:2, :] * u_m1 + w_ref[2:3, :] * u
    sg = _sig(zc)
    sz = zc * sg
    co = bg * yconv
    del tm
    return u, u_m1, u_m2, yconv, sg, sz, co


def _fwd_in_conv(h, wt, wconv8, wpc, tm):
    t = h.shape[0]
    seq_tiles = SEQ_LEN // tm

    def body(h_ref, w_ref, wc_ref, wpc_ref, a4_ref, ya_ref, last_u_ref):
        hh = h_ref[...]
        xc, bg, cg, zc = (_dot_nt(hh, w_ref[j * D:(j + 1) * D, :]) for j in range(4))
        for j, z in enumerate((xc, bg, cg, zc)):
            a4_ref[:, j * D:(j + 1) * D] = z
        first = pl.program_id(0) % seq_tiles == 0
        up6 = jnp.where(first, 0.0, last_u_ref[6:7, :])
        up7 = jnp.where(first, 0.0, last_u_ref[7:8, :])
        u, _, _, _, _, sz, co = _conv_forward(xc, bg, cg, zc, up6, up7, wc_ref)
        last_u_ref[...] = u[tm - 8:tm, :]
        ya_ref[...] = _dot((sz * co).astype(BF16), wpc_ref[...])

    return pl.pallas_call(
        body, name="fwd_in_conv", grid=(t // tm,),
        in_specs=[_row_spec(tm, D), _whole_vmem(), pl.BlockSpec((8, D), lambda i: (0, 0)), _whole_vmem()],
        out_specs=[_row_spec(tm, 4 * D), _row_spec(tm, D)],
        out_shape=[jax.ShapeDtypeStruct((t, 4 * D), F32), jax.ShapeDtypeStruct((t, D), F32)],
        scratch_shapes=[pltpu.VMEM((8, D), F32)],
        compiler_params=_params("arbitrary"),
    )(h, wt, wconv8, wpc)


STACK = 4 * BLK


def _band_mask(first):
    qi = lax.broadcasted_iota(jnp.int32, (STACK, 2 * BLK), 0) & (BLK - 1)
    kj = lax.broadcasted_iota(jnp.int32, (STACK, 2 * BLK), 1)
    return (kj > qi) & (kj <= qi + BLK) & (kj >= jnp.where(first, BLK, 0))


def _masked_fill(sink_ref, g, e):
    kj = lax.broadcasted_iota(jnp.int32, (STACK, 2 * BLK), 1)
    sink = jnp.concatenate([jnp.full((BLK, 2 * BLK), sink_ref[0, 2 * (4 * g + jj) + e], F32) for jj in range(4)], axis=0)
    return jnp.where(kj == 0, sink, NEG)


def _padded_pair(kvp_ref, kvc_ref, col, other=0.0):
    z = jnp.concatenate([kvp_ref[:, col:col + 128], kvc_ref[:, col:col + 128]], axis=0).astype(F32)
    z = jnp.where(lax.broadcasted_iota(jnp.int32, z.shape, 0) == 0, 0.0, z)
    zs = pltpu.roll(z, 64, 1)
    lo = lax.broadcasted_iota(jnp.int32, z.shape, 1) < 64
    fill = jnp.full_like(z, other)
    left = [jnp.where(lo, z, fill).astype(BF16), jnp.where(lo, zs, fill).astype(BF16)]
    right = [jnp.where(lo, fill, zs).astype(BF16), jnp.where(lo, fill, z).astype(BF16)]
    return left, right


def _exp_logits(s, valid, fill):
    s = jnp.where(valid, s, fill)
    return jnp.exp(s - jnp.max(s, axis=-1, keepdims=True))


def _fwd_attn(sinks, q, kv, g3):
    t = q.shape[0]
    seq_blocks = SEQ_LEN // BLK

    def body(sink_ref, q_ref, kvc_ref, kvp_ref, za_ref, attn_ref, ub_ref):
        i = pl.program_id(0)
        valid = _band_mask(i % seq_blocks == 0)
        k_pad = _padded_pair(kvp_ref, kvc_ref, 0)
        v_one = _padded_pair(kvp_ref, kvc_ref, 128, other=1.0)
        lo = lax.broadcasted_iota(jnp.int32, (STACK, 128), 1) < 64
        for g in range(2):
            qg = jnp.concatenate([q_ref[:, j * 128:(j + 1) * 128] for j in range(4 * g, 4 * g + 4)], axis=0)
            pv = [_dot(_exp_logits(_dot_nt(qg, k_pad[e][g]), valid, _masked_fill(sink_ref, g, e)).astype(BF16),
                       v_one[e][g]) for e in range(2)]
            o = jnp.where(lo, pv[0], pv[1]) / pltpu.roll(jnp.where(lo, pv[1], pv[0]), 64, 1)
            for jj in range(4):
                cols = slice((4 * g + jj) * 128, (4 * g + jj + 1) * 128)
                oj = o[jj * BLK:(jj + 1) * BLK, :]
                attn_ref[:, cols] = oj
                za = za_ref[:, cols]
                ub_ref[:, cols] = (za * _sig(za) * oj).astype(BF16)

    return pl.pallas_call(
        body, name="fwd_attn", grid=(t // BLK,),
        in_specs=[pl.BlockSpec(memory_space=pltpu.SMEM), _row_spec(BLK, D), _row_spec(BLK, 256),
                  pl.BlockSpec((BLK, 256), lambda i: (jnp.maximum(i - 1, 0), 0)), _row_spec(BLK, D, 0)],
        out_specs=[_row_spec(BLK, D), _row_spec(BLK, D)],
        out_shape=[jax.ShapeDtypeStruct((t, D), F32), jax.ShapeDtypeStruct((t, D), BF16)],
        compiler_params=_params("parallel"),
    )(sinks, q, kv, kv, g3)


def _fwd_out_bwd_head(ya, ub, g3, x, target, g_post, wpa, wout, tm, parts=1):
    t = x.shape[0]

    def body(ya_ref, ub_ref, ga_ref, gb_ref, x_ref, tgt_ref, gp_ref, wpa_ref, wout_ref,
             loss_ref, dout_ref, dya_ref, dub_ref, dgab_ref, dwout_ref, dwpa_ref, dgp_ref):
        @pl.when(pl.program_id(0) == 0)
        def _():
            loss_ref[...] = jnp.zeros_like(loss_ref)
            dwout_ref[...] = jnp.zeros_like(dwout_ref)
            dwpa_ref[...] = jnp.zeros_like(dwpa_ref)
            dgp_ref[...] = jnp.zeros_like(dgp_ref)

        g = gp_ref[...]
        sq = jnp.zeros((1, 1), F32)
        dgp = jnp.zeros((1, D), F32)
        mbs, dys, dybs = [], [], []
        for part in range(parts):
            rows = slice(part * (tm // parts), (part + 1) * (tm // parts))
            ub = ub_ref[rows, :]
            ya = ya_ref[rows, :]
            yb = _dot(ub, wpa_ref[...])
            sa = _sig(ga_ref[rows, :])
            sb = _sig(gb_ref[rows, :])
            mb = (sa * ya + sb * yb).astype(BF16)
            y = _dot(mb, wout_ref[...])
            r = lax.rsqrt(jnp.mean(y * y, axis=-1, keepdims=True) + RMS_EPS)
            n = y * r
            err = (x_ref[rows, :] + n * g) - tgt_ref[rows, :]
            sq = sq + jnp.sum(jnp.sum(err * err, axis=0, keepdims=True), axis=1, keepdims=True)
            dout = err * (1.0 / D)
            dout_ref[rows, :] = dout
            dgp = dgp + jnp.sum(dout * n, axis=0, keepdims=True)
            dn = dout * g
            dy = (r * (dn - n * jnp.mean(dn * n, axis=-1, keepdims=True))).astype(BF16)
            dm = _dot_nt(dy, wout_ref[...])
            dya_ref[rows, :] = (dm * sa).astype(BF16)
            dyb = (dm * sb).astype(BF16)
            dgab_ref[rows, 0:D] = (dm * ya * (sa * (1.0 - sa))).astype(BF16)
            dgab_ref[rows, D:2 * D] = (dm * yb * (sb * (1.0 - sb))).astype(BF16)
            dub_ref[rows, :] = _dot_nt(dyb, wpa_ref[...])
            mbs.append(mb)
            dys.append(dy)
            dybs.append(dyb)
        loss_ref[...] += sq * (0.5 / D)
        dgp_ref[0:1, :] += dgp
        dwout_ref[...] += _dot_tn(jnp.concatenate(mbs, axis=0), jnp.concatenate(dys, axis=0))
        dwpa_ref[...] += _dot_tn(ub_ref[...], jnp.concatenate(dybs, axis=0))

    return pl.pallas_call(
        body, name="fwd_out_bwd_head", grid=(t // tm,),
        in_specs=[_row_spec(tm, D), _row_spec(tm, D), _row_spec(tm, D, 1), _row_spec(tm, D, 2),
                  _row_spec(tm, D), _row_spec(tm, D), pl.BlockSpec((1, D), lambda i: (0, 0)),
                  _whole_vmem(), _whole_vmem()],
        out_specs=[pl.BlockSpec((8, 128), lambda i: (0, 0)), _row_spec(tm, D), _row_spec(tm, D), _row_spec(tm, D),
                   _row_spec(tm, 2 * D), _whole_vmem(), _whole_vmem(), pl.BlockSpec((8, D), lambda i: (0, 0))],
        out_shape=[jax.ShapeDtypeStruct((8, 128), F32), jax.ShapeDtypeStruct((t, D), F32),
                   jax.ShapeDtypeStruct((t, D), BF16), jax.ShapeDtypeStruct((t, D), F32),
                   jax.ShapeDtypeStruct((t, 2 * D), BF16), jax.ShapeDtypeStruct((D, D), F32),
                   jax.ShapeDtypeStruct((D, D), F32), jax.ShapeDtypeStruct((8, D), F32)],
        compiler_params=_params("arbitrary"),
    )(ya, ub, g3, g3, x, target, g_post, wpa, wout)


def _bwd_attn(sinks, q, kv, attn, dub, g3, cos_t, sin_t):
    t = q.shape[0]
    seq_blocks = SEQ_LEN // BLK

    def body(sink_ref, q_ref, kvc_ref, kvp_ref, attn_ref, dub_ref, za_ref, c_ref, s_ref,
             dq_ref, dza_ref, dkv_own_ref, dkv_prev_ref, dsink_ref):
        i = pl.program_id(0)

        @pl.when(i == 0)
        def _():
            dsink_ref[...] = jnp.zeros_like(dsink_ref)

        valid = _band_mask(i % seq_blocks == 0)
        k_pad = _padded_pair(kvp_ref, kvc_ref, 0)
        v_pad = _padded_pair(kvp_ref, kvc_ref, 128)
        lo = lax.broadcasted_iota(jnp.int32, (STACK, 128), 1) < 64
        lane8 = lax.broadcasted_iota(jnp.int32, (8, 128), 1)
        c = c_ref[...]
        s = s_ref[...]
        dk_acc, dv_acc = [], []
        dsink = jnp.zeros((8, 128), F32)
        for g in range(2):
            qg, dog = [], []
            for j in range(4 * g, 4 * g + 4):
                cols = slice(j * 128, (j + 1) * 128)
                za = za_ref[:, cols]
                sg = _sig(za)
                dub = dub_ref[:, cols]
                dza_ref[:, cols] = (dub * attn_ref[:, cols] * (sg * (1.0 + za * (1.0 - sg)))).astype(BF16)
                dog.append((dub * (za * sg)).astype(BF16))
                qg.append(q_ref[:, cols])
            qg = jnp.concatenate(qg, axis=0)
            dog = jnp.concatenate(dog, axis=0)
            dq = jnp.zeros((STACK, 128), F32)
            ds_both, p_both = [], []
            for e in range(2):
                p = _exp_logits(_dot_nt(qg, k_pad[e][g]), valid, _masked_fill(sink_ref, g, e))
                p = p / jnp.sum(p, axis=-1, keepdims=True)
                dp = _dot_nt(dog, v_pad[e][g])
                ds = p * (dp - jnp.sum(p * dp, axis=-1, keepdims=True))
                for jj in range(4):
                    tot = jnp.sum(ds[jj * BLK:(jj + 1) * BLK, 0:1], axis=0, keepdims=True)
                    dsink = dsink + jnp.where(lane8 == 2 * (4 * g + jj) + e, tot, 0.0)
                ds = ds.astype(BF16)
                dq = dq + _dot(ds, k_pad[e][g])
                ds_both.append(ds)
                p_both.append(p.astype(BF16))
            zero = jnp.zeros_like(qg)
            q2 = jnp.concatenate([jnp.where(lo, qg, zero), jnp.where(lo, zero, qg)], axis=0)
            do2 = jnp.concatenate([jnp.where(lo, dog, zero), jnp.where(lo, zero, dog)], axis=0)
            dk_acc.append(_dot_tn(jnp.concatenate(ds_both, axis=0), q2))
            dv_acc.append(_dot_tn(jnp.concatenate(p_both, axis=0), do2))
            for jj in range(4):
                cols = slice((4 * g + jj) * 128, (4 * g + jj + 1) * 128)
                dqj = dq[jj * BLK:(jj + 1) * BLK, :] * LOGIT_SCALE
                dq_ref[:, cols] = (dqj * c - _swap_halves(dqj) * s).astype(BF16)
        dsink_ref[...] += dsink
        lo2 = lax.broadcasted_iota(jnp.int32, (2 * BLK, 128), 1) < 64
        sink_row = lax.broadcasted_iota(jnp.int32, (2 * BLK, 128), 0) == 0
        for col, acc in ((0, dk_acc), (128, dv_acc)):
            both = jnp.where(lo2, acc[0] + pltpu.roll(acc[0], 64, 1), acc[1] + pltpu.roll(acc[1], 64, 1))
            both = jnp.where(sink_row, 0.0, both)
            dkv_prev_ref[:, col:col + 128] = both[0:BLK, :]
            dkv_own_ref[:, col:col + 128] = both[BLK:2 * BLK, :]

    tab = pl.BlockSpec((BLK, 128), lambda i: (i % seq_blocks, 0))
    return pl.pallas_call(
        body, name="bwd_attn", grid=(t // BLK,),
        in_specs=[pl.BlockSpec(memory_space=pltpu.SMEM), _row_spec(BLK, D), _row_spec(BLK, 256),
                  pl.BlockSpec((BLK, 256), lambda i: (jnp.maximum(i - 1, 0), 0)),
                  _row_spec(BLK, D), _row_spec(BLK, D), _row_spec(BLK, D, 0), tab, tab],
        out_specs=[_row_spec(BLK, D), _row_spec(BLK, D), _row_spec(BLK, 256), _row_spec(BLK, 256),
                   pl.BlockSpec((8, 128), lambda i: (0, 0))],
        out_shape=[jax.ShapeDtypeStruct((t, D), BF16), jax.ShapeDtypeStruct((t, D), BF16),
                   jax.ShapeDtypeStruct((t, 256), F32), jax.ShapeDtypeStruct((t, 256), F32),
                   jax.ShapeDtypeStruct((8, 128), F32)],
        compiler_params=_params("arbitrary"),
    )(sinks, q, kv, kv, attn, dub, g3, cos_t, sin_t)


def _bwd_kv_finish(dkv_own, dkv_prev, cos_t, sin_t):
    t = dkv_own.shape[0]
    tm = 512
    seq_tiles = SEQ_LEN // tm
    n_blocks = t // BLK

    def body(own_ref, same_ref, nxt_ref, c_ref, s_ref, out_ref):
        keep = jnp.where(pl.program_id(0) % seq_tiles == seq_tiles - 1, 0.0, 1.0)
        shifted = jnp.concatenate([same_ref[BLK:tm, :], nxt_ref[...] * keep], axis=0)
        tot = own_ref[...] + shifted
        dk = tot[:, 0:128]
        out_ref[:, 0:128] = (dk * c_ref[...] - _swap_halves(dk) * s_ref[...]).astype(BF16)
        out_ref[:, 128:256] = tot[:, 128:256].astype(BF16)

    tab = pl.BlockSpec((tm, 128), lambda i: (i % seq_tiles, 0))
    return pl.pallas_call(
        body, name="bwd_kv_finish", grid=(t // tm,),
        in_specs=[_row_spec(tm, 256), _row_spec(tm, 256),
                  pl.BlockSpec((BLK, 256), lambda i: (jnp.minimum((i + 1) * (tm // BLK), n_blocks - 1), 0)), tab, tab],
        out_specs=_row_spec(tm, 256),
        out_shape=jax.ShapeDtypeStruct((t, 256), BF16),
        compiler_params=_params("parallel"),
    )(dkv_own, dkv_prev, dkv_prev, cos_t, sin_t)


def _bwd_conv(dya, a4, wconv8, wpc, tm):
    t = a4.shape[0]
    seq_tiles = SEQ_LEN // tm
    last8 = t // 8 - 1
    last16 = t // 16 - 1

    def body(dya_ref, xc_ref, bg_ref, cg_ref, zc_ref, xcp_ref, cgp_ref, dyan_ref, bgn_ref, zcn_ref,
             w_ref, wpc_ref, da4_ref, dwpc_ref, dwc_ref):
        i = pl.program_id(0)

        @pl.when(i == 0)
        def _():
            dwpc_ref[...] = jnp.zeros_like(dwpc_ref)
            dwc_ref[...] = jnp.zeros_like(dwc_ref)

        keep_prev = jnp.where(i % seq_tiles == 0, 0.0, 1.0)
        keep_next = jnp.where(i % seq_tiles == seq_tiles - 1, 0.0, 1.0)
        up6 = cgp_ref[6:7, :] * xcp_ref[6:7, :] * keep_prev
        up7 = cgp_ref[7:8, :] * xcp_ref[7:8, :] * keep_prev
        xc = xc_ref[...]
        bg = bg_ref[...]
        cg = cg_ref[...]
        zc = zc_ref[...]
        u, u_m1, u_m2, yconv, sg, sz, co = _conv_forward(xc, bg, cg, zc, up6, up7, w_ref)
        dya = dya_ref[...]
        dwpc_ref[...] += _dot_tn((sz * co).astype(BF16), dya)
        dua = _dot_nt(dya, wpc_ref[...])
        da4_ref[:, 3 * D:4 * D] = (dua * co * (sg * (1.0 + zc * (1.0 - sg)))).astype(BF16)
        dco = dua * sz
        da4_ref[:, D:2 * D] = (dco * yconv).astype(BF16)
        dyc = dco * bg
        dwc_ref[0:1, :] += jnp.sum(dyc * u_m2, axis=0, keepdims=True)
        dwc_ref[1:2, :] += jnp.sum(dyc * u_m1, axis=0, keepdims=True)
        dwc_ref[2:3, :] += jnp.sum(dyc * u, axis=0, keepdims=True)
        zcn = zcn_ref[...]
        dyc_n = _dot_nt(dyan_ref[...], wpc_ref[...])[0:8, :] * (zcn * _sig(zcn)) * bgn_ref[...] * keep_next
        rows = lax.broadcasted_iota(jnp.int32, xc.shape, 0)
        n0 = dyc_n[0:1, :]
        n1 = dyc_n[1:2, :]
        dyc_p1 = jnp.where(rows == tm - 1, n0, pltpu.roll(dyc, tm - 1, 0))
        dyc_p2 = jnp.where(rows == tm - 2, n0, jnp.where(rows == tm - 1, n1, pltpu.roll(dyc, tm - 2, 0)))
        du = w_ref[2:3, :] * dyc + w_ref[1:2, :] * dyc_p1 + w_ref[0:1, :] * dyc_p2
        da4_ref[:, 0:D] = (du * cg).astype(BF16)
        da4_ref[:, 2 * D:3 * D] = (du * xc).astype(BF16)

    def prev(col):
        return pl.BlockSpec((8, D), lambda i: (jnp.maximum(i * (tm // 8) - 1, 0), col))

    def nxt(col):
        return pl.BlockSpec((8, D), lambda i: (jnp.minimum((i + 1) * (tm // 8), last8), col))

    return pl.pallas_call(
        body, name="bwd_conv", grid=(t // tm,),
        in_specs=[_row_spec(tm, D), _row_spec(tm, D, 0), _row_spec(tm, D, 1), _row_spec(tm, D, 2), _row_spec(tm, D, 3),
                  prev(0), prev(2),
                  pl.BlockSpec((16, D), lambda i: (jnp.minimum((i + 1) * (tm // 16), last16), 0)), nxt(1), nxt(3),
                  pl.BlockSpec((8, D), lambda i: (0, 0)), _whole_vmem()],
        out_specs=[_row_spec(tm, 4 * D), _whole_vmem(), pl.BlockSpec((8, D), lambda i: (0, 0))],
        out_shape=[jax.ShapeDtypeStruct((t, 4 * D), BF16), jax.ShapeDtypeStruct((D, D), F32),
                   jax.ShapeDtypeStruct((8, D), F32)],
        compiler_params=_params("arbitrary"),
    )(dya, a4, a4, a4, a4, a4, a4, dya, a4, a4, wconv8, wpc)


def _bwd_dh(da4, dq, dkv, dza, dgab, wt, x, g_pre, dout, tm):
    t = x.shape[0]

    def body(da4_ref, dq_ref, dkv_ref, dza_ref, dgab_ref, w_ref, x_ref, g_ref, dout_ref, gx_ref, dg_ref):
        @pl.when(pl.program_id(0) == 0)
        def _():
            dg_ref[...] = jnp.zeros_like(dg_ref)

        dh = _dot(da4_ref[...], w_ref[0:ROW_Q, :])
        dh += _dot(dq_ref[...], w_ref[ROW_Q:ROW_KV, :])
        dh += _dot(dkv_ref[...], w_ref[ROW_KV:ROW_ZA, :])
        dh += _dot(dza_ref[...], w_ref[ROW_ZA:ROW_GA, :])
        dh += _dot(dgab_ref[...], w_ref[ROW_GA:D_IN, :])
        xf = x_ref[...]
        r = lax.rsqrt(jnp.mean(xf * xf, axis=-1, keepdims=True) + RMS_EPS)
        xn = xf * r
        dg_ref[0:1, :] += jnp.sum(dh * xn, axis=0, keepdims=True)
        dxn = dh * g_ref[...]
        gx_ref[...] = dout_ref[...] + r * (dxn - xn * jnp.mean(dxn * xn, axis=-1, keepdims=True))

    return pl.pallas_call(
        body, name="bwd_dh", grid=(t // tm,),
        in_specs=[_row_spec(tm, 4 * D), _row_spec(tm, D), _row_spec(tm, 256), _row_spec(tm, D), _row_spec(tm, 2 * D),
                  _whole_vmem(), _row_spec(tm, D), pl.BlockSpec((1, D), lambda i: (0, 0)), _row_spec(tm, D)],
        out_specs=[_row_spec(tm, D), pl.BlockSpec((8, D), lambda i: (0, 0))],
        out_shape=[jax.ShapeDtypeStruct((t, D), F32), jax.ShapeDtypeStruct((8, D), F32)],
        compiler_params=_params("arbitrary"),
    )(da4, dq, dkv, dza, dgab, wt, x, g_pre, dout)


def _bwd_dw_in(h, piece, row0, nb, tm, name, prev):
    t, n = piece.shape
    n_t = t // tm

    def body(*refs):
        h_ref, p_ref = refs[0], refs[1]
        o32_ref, o16_ref, acc_ref, acc16_ref, sems = refs[-5:]
        j, i = pl.program_id(0), pl.program_id(1)

        @pl.when(i == 0)
        def _():
            acc_ref[...] = jnp.zeros_like(acc_ref)

        acc_ref[...] += _dot_tn(p_ref[...], h_ref[...])

        @pl.when(i == n_t - 1)
        def _():
            acc16_ref[...] = acc_ref[...].astype(BF16)
            rows = pl.ds(pl.multiple_of(row0 + j * nb, 16), nb)
            c32 = pltpu.make_async_copy(acc_ref, o32_ref.at[rows], sems.at[0])
            c16 = pltpu.make_async_copy(acc16_ref, o16_ref.at[rows], sems.at[1])
            c32.start()
            c16.start()
            c32.wait()
            c16.wait()

    hbm = pl.BlockSpec(memory_space=pl.ANY)
    carried = [] if prev is None else list(prev)
    return pl.pallas_call(
        body, name=name, grid=(n // nb, n_t),
        in_specs=[pl.BlockSpec((tm, D), lambda j, i: (i, 0)), pl.BlockSpec((tm, nb), lambda j, i: (i, j))]
        + [hbm] * len(carried),
        out_specs=[hbm, hbm],
        out_shape=[jax.ShapeDtypeStruct((D_IN, D), F32), jax.ShapeDtypeStruct((D_IN, D), BF16)],
        scratch_shapes=[pltpu.VMEM((nb, D), F32), pltpu.VMEM((nb, D), BF16), pltpu.SemaphoreType.DMA((2,))],
        input_output_aliases={2: 0, 3: 1} if carried else {},
        compiler_params=_params("arbitrary", "arbitrary"),
    )(h, piece, *carried)


def _place():
    x, y, c = lax.axis_index("x"), lax.axis_index("y"), lax.axis_index("c")
    return x, y, c, 4 * x + 2 * y + c


def _peer(x, y, c, k):
    return (1 - x if k & 4 else x, 1 - y if k & 2 else y, 1 - c if k & 1 else c)


ICI_MASKS = (4, 2, 6)


def _all_gather(shards):
    n = len(shards)

    def body(*refs):
        src, dst = refs[:n], refs[n:2 * n]
        send_sems, recv_sems, local_sems = refs[2 * n:]
        x, y, c, me = _place()
        sibling = _peer(x, y, c, 1)

        def copy(a, s, block, to, own=False):
            return pltpu.make_async_remote_copy(
                src_ref=src[a] if own else dst[a].at[block], dst_ref=dst[a].at[block],
                send_sem=send_sems.at[a * 7 + s], recv_sem=recv_sems.at[a * 7 + s], device_id=to, device_id_type=MESH_ID)

        local = [pltpu.make_async_copy(src[a], dst[a].at[me], local_sems.at[a]) for a in range(n)]
        for cp in local:
            cp.start()
        started = [copy(a, 0, me, sibling, own=True) for a in range(n)]
        started += [copy(a, 1 + j, me, _peer(x, y, c, k), own=True) for j, k in enumerate(ICI_MASKS) for a in range(n)]
        for cp in started:
            cp.start()
        for j, k in enumerate(ICI_MASKS):
            for a in range(n):
                copy(a, 1 + j, me ^ k, sibling).wait_recv()
                fwd = copy(a, 4 + j, me ^ k, sibling)
                fwd.start()
                started.append(fwd)
        for a in range(n):
            copy(a, 0, me ^ 1, sibling).wait_recv()
        for j, k in enumerate(ICI_MASKS):
            for a in range(n):
                copy(a, 4 + j, me ^ 1 ^ k, sibling).wait_recv()
        for cp in started:
            cp.wait_send()
        for cp in local:
            cp.wait()

    hbm = pl.BlockSpec(memory_space=pl.ANY)
    return pl.pallas_call(
        body, name="all_gather_weights",
        in_specs=[hbm] * n, out_specs=[hbm] * n,
        out_shape=[jax.ShapeDtypeStruct((N_DEV,) + s.shape, s.dtype) for s in shards],
        scratch_shapes=[pltpu.SemaphoreType.DMA((7 * n,)), pltpu.SemaphoreType.DMA((7 * n,)),
                        pltpu.SemaphoreType.DMA((n,))],
    )(*shards)


def _direct_copies(src, land, send_sems, recv_sems):
    x, y, c, me = _place()
    return [pltpu.make_async_remote_copy(
        src_ref=src[a], dst_ref=land[a].at[me], send_sem=send_sems.at[a * 7 + k - 1],
        recv_sem=recv_sems.at[a * 7 + k - 1], device_id=_peer(x, y, c, k), device_id_type=MESH_ID)
        for k in range(1, N_DEV) for a in range(len(src))]


def _gather_start(shards, name):
    n = len(shards)

    def body(*refs):
        src, land = refs[:n], refs[n:2 * n]
        send_sems, recv_sems = refs[2 * n], refs[2 * n + 1]
        token_ref = refs[-1]
        for cp in _direct_copies(src, land, send_sems, recv_sems):
            cp.start()
        token_ref[...] = jnp.zeros_like(token_ref)

    hbm = pl.BlockSpec(memory_space=pltpu.HBM)
    sem = pl.BlockSpec(memory_space=pltpu.SEMAPHORE)
    lands = [lax.empty((N_DEV,) + s.shape, s.dtype) for s in shards]
    out = pl.pallas_call(
        body, name=name + "_start",
        out_shape=(pltpu.SemaphoreType.DMA((7 * n,)), pltpu.SemaphoreType.DMA((7 * n,)),
                   *[pltpu.HBM(s.shape, s.dtype) for s in shards], *[pltpu.HBM(s.shape, s.dtype) for s in lands],
                   jax.ShapeDtypeStruct((8, 128), F32)),
        in_specs=[hbm] * (2 * n), out_specs=(sem, sem, *[hbm] * (2 * n), _whole_vmem()),
        input_output_aliases={a: 2 + a for a in range(2 * n)},
        compiler_params=pltpu.CompilerParams(has_side_effects=pltpu.SideEffectType.DATAFLOW_SIDE_EFFECTING),
    )(*[pltpu.with_memory_space_constraint(s, pltpu.HBM) for s in list(shards) + lands])
    return out[0], out[1], out[2:2 + n], out[2 + n:2 + 2 * n], out[-1]


def _gather_wait(send_sems, recv_sems, flying, lands, after, name):
    n = len(flying)

    def body(*refs):
        src, land = refs[:n], refs[n:2 * n]
        for cp in _direct_copies(src, land, refs[2 * n], refs[2 * n + 1]):
            cp.wait_send()
            cp.wait_recv()

    hbm = pl.BlockSpec(memory_space=pltpu.HBM)
    sem = pl.BlockSpec(memory_space=pltpu.SEMAPHORE)
    out = pl.pallas_call(
        body, name=name + "_wait",
        out_shape=tuple(pltpu.HBM(s.shape, s.dtype) for s in list(flying) + list(lands)),
        in_specs=[hbm] * (2 * n) + [sem, sem, pl.BlockSpec(memory_space=pl.ANY)], out_specs=tuple([hbm] * (2 * n)),
        input_output_aliases={a: a for a in range(2 * n)},
        compiler_params=pltpu.CompilerParams(has_side_effects=pltpu.SideEffectType.DATAFLOW_SIDE_EFFECTING),
    )(*flying, *lands, send_sems, recv_sems, after)
    return out[n:]


def _exchange_sibling(by_dest):
    n = len(by_dest)

    def body(*refs):
        src, dst = refs[:n], refs[n:2 * n]
        send_sems, recv_sems = refs[2 * n:]
        x, y, c, _ = _place()
        sibling = _peer(x, y, c, 1)
        copies = [pltpu.make_async_remote_copy(
            src_ref=src[a].at[2 * p + (1 - c)], dst_ref=dst[a].at[p], send_sem=send_sems.at[a * 4 + p],
            recv_sem=recv_sems.at[a * 4 + p], device_id=sibling, device_id_type=MESH_ID)
            for a in range(n) for p in range(4)]
        for cp in copies:
            cp.start()
        for cp in copies:
            cp.wait_recv()
        for cp in copies:
            cp.wait_send()

    hbm = pl.BlockSpec(memory_space=pl.ANY)
    return pl.pallas_call(
        body, name="exchange_sibling", in_specs=[hbm] * n, out_specs=[hbm] * n,
        out_shape=[jax.ShapeDtypeStruct((4,) + s.shape[1:], s.dtype) for s in by_dest],
        scratch_shapes=[pltpu.SemaphoreType.DMA((4 * n,)), pltpu.SemaphoreType.DMA((4 * n,))],
    )(*by_dest)


def _chip_copies(src, land, send_sems, recv_sems):
    x, y, c, _ = _place()
    chip = 2 * x + y
    return [pltpu.make_async_remote_copy(
        src_ref=src[a].at[chip ^ (k >> 1)], dst_ref=land[a].at[j], send_sem=send_sems.at[a * 3 + j],
        recv_sem=recv_sems.at[a * 3 + j], device_id=_peer(x, y, c, k), device_id_type=MESH_ID)
        for j, k in enumerate(ICI_MASKS) for a in range(len(src))]


def _exchange_chips_start(by_chip):
    n = len(by_chip)

    def body(*refs):
        src, land = refs[:n], refs[n:2 * n]
        send_sems, recv_sems = refs[2 * n], refs[2 * n + 1]
        token_ref = refs[-1]
        for cp in _chip_copies(src, land, send_sems, recv_sems):
            cp.start()
        token_ref[...] = jnp.zeros_like(token_ref)

    hbm = pl.BlockSpec(memory_space=pltpu.HBM)
    sem = pl.BlockSpec(memory_space=pltpu.SEMAPHORE)
    lands = [lax.empty((3,) + s.shape[1:], s.dtype) for s in by_chip]
    out = pl.pallas_call(
        body, name="exchange_chips_start",
        out_shape=(pltpu.SemaphoreType.DMA((3 * n,)), pltpu.SemaphoreType.DMA((3 * n,)),
                   *[pltpu.HBM(s.shape, s.dtype) for s in by_chip], *[pltpu.HBM(s.shape, s.dtype) for s in lands],
                   jax.ShapeDtypeStruct((8, 128), F32)),
        in_specs=[hbm] * (2 * n), out_specs=(sem, sem, *[hbm] * (2 * n), _whole_vmem()),
        input_output_aliases={a: 2 + a for a in range(2 * n)},
        compiler_params=pltpu.CompilerParams(has_side_effects=pltpu.SideEffectType.DATAFLOW_SIDE_EFFECTING),
    )(*[pltpu.with_memory_space_constraint(s, pltpu.HBM) for s in list(by_chip) + lands])
    return out[0], out[1], out[2:2 + n], out[2 + n:2 + 2 * n], out[-1]


def _exchange_chips_wait(send_sems, recv_sems, flying, lands, after):
    n = len(flying)

    def body(*refs):
        src, land = refs[:n], refs[n:2 * n]
        send_sems_ref, recv_sems_ref = refs[2 * n], refs[2 * n + 1]
        for cp in _chip_copies(src, land, send_sems_ref, recv_sems_ref):
            cp.wait_send()
            cp.wait_recv()

    hbm = pl.BlockSpec(memory_space=pltpu.HBM)
    sem = pl.BlockSpec(memory_space=pltpu.SEMAPHORE)
    out = pl.pallas_call(
        body, name="exchange_chips_wait",
        out_shape=tuple(pltpu.HBM(s.shape, s.dtype) for s in list(flying) + list(lands)),
        in_specs=[hbm] * (2 * n) + [sem, sem, pl.BlockSpec(memory_space=pl.ANY)], out_specs=tuple([hbm] * (2 * n)),
        input_output_aliases={a: a for a in range(2 * n)},
        compiler_params=pltpu.CompilerParams(has_side_effects=pltpu.SideEffectType.DATAFLOW_SIDE_EFFECTING),
    )(*flying, *lands, send_sems, recv_sems, after)
    return out[n:]


def _adamw_math(w, g, m, v):
    m = ADAM_B1 * m + (1.0 - ADAM_B1) * g
    v = ADAM_B2 * v + (1.0 - ADAM_B2) * (g * g)
    m_hat = m / (1.0 - ADAM_B1 ** ADAM_STEP)
    v_hat = v / (1.0 - ADAM_B2 ** ADAM_STEP)
    return -ADAM_LR * (m_hat / (jnp.sqrt(v_hat) + ADAM_EPS) + ADAM_WD * w), m, v


def _pair_sum(owns, recvs, c_arr, tr, name):
    n = len(owns)
    _, rows, cols = owns[0].shape

    def body(c_ref, *refs):
        for a in range(n):
            s = refs[a][...] + refs[n + a][...].astype(F32)
            refs[2 * n + a][...] = s
            refs[3 * n + a][...] = s.astype(BF16)

    by_chip = pl.BlockSpec((None, tr, cols), lambda p, i, c_ref: (p, i, 0))
    mine = pl.BlockSpec((None, tr, cols), lambda p, i, c_ref: (2 * p + c_ref[0], i, 0))
    out = pl.pallas_call(
        body, name=name,
        grid_spec=pltpu.PrefetchScalarGridSpec(
            num_scalar_prefetch=1, grid=(4, rows // tr), in_specs=[mine] * n + [by_chip] * n, out_specs=[by_chip] * (2 * n)),
        out_shape=[jax.ShapeDtypeStruct((4, rows, cols), F32)] * n + [jax.ShapeDtypeStruct((4, rows, cols), BF16)] * n,
        compiler_params=_params("parallel", "parallel"),
    )(c_arr, *owns, *recvs)
    return out[:n], out[n:]


def _chip_sum(pairs, recvs, chip_arr, tr, name, adam=None):
    n = len(pairs)
    _, rows, cols = pairs[0].shape
    n_state = 0 if adam is None else 3 * n

    def body(chip_ref, *refs):
        outs = refs[2 * n + n_state:]
        for a in range(n):
            g = refs[a][...]
            for j in range(3):
                g = g + refs[n + a][j].astype(F32)
            outs[a][...] = g
            if adam is not None:
                w_ref, m_ref, v_ref = (refs[2 * n + s * n + a] for s in range(3))
                outs[n + a][...], outs[2 * n + a][...], outs[3 * n + a][...] = _adamw_math(w_ref[...], g, m_ref[...], v_ref[...])

    blk = pl.BlockSpec((tr, cols), lambda i, chip_ref: (i, 0))
    n_out = n if adam is None else 4 * n
    out = pl.pallas_call(
        body, name=name,
        grid_spec=pltpu.PrefetchScalarGridSpec(
            num_scalar_prefetch=1, grid=(rows // tr,),
            in_specs=[pl.BlockSpec((None, tr, cols), lambda i, chip_ref: (chip_ref[0], i, 0))] * n
            + [pl.BlockSpec((3, tr, cols), lambda i, chip_ref: (0, i, 0))] * n + [blk] * n_state,
            out_specs=[blk] * n_out),
        out_shape=[jax.ShapeDtypeStruct((rows, cols), F32)] * n_out,
        compiler_params=_params("parallel"),
    )(chip_arr, *pairs, *recvs, *([] if adam is None else [t for group in adam for t in group]))
    return out if adam is None else (out[:n], out[n:2 * n], out[2 * n:3 * n], out[3 * n:])


def _adamw(ws, gs, ms, vs, name, tr=None):
    n = len(ws)

    def body(*refs):
        for a in range(n):
            w_ref, g_ref, m_ref, v_ref = (refs[s * n + a] for s in range(4))
            refs[4 * n + a][...], refs[5 * n + a][...], refs[6 * n + a][...] = _adamw_math(
                w_ref[...], g_ref[...], m_ref[...], v_ref[...])

    out_shape = [jax.ShapeDtypeStruct(w.shape, F32) for w in ws] * 3
    if tr is None:
        out = pl.pallas_call(body, name=name, out_shape=out_shape)(*ws, *gs, *ms, *vs)
    else:
        blk = pl.BlockSpec((tr, ws[0].shape[1]), lambda i: (i, 0))
        out = pl.pallas_call(
            body, name=name, grid=(ws[0].shape[0] // tr,), in_specs=[blk] * (4 * n), out_specs=[blk] * (3 * n),
            out_shape=out_shape, compiler_params=_params("parallel"))(*ws, *gs, *ms, *vs)
    return out[:n], out[n:2 * n], out[2 * n:]


def _sum_small(small_all):
    def body(s_ref, o_ref):
        g = s_ref[0]
        for d in range(1, N_DEV):
            g = g + s_ref[d]
        o_ref[...] = g

    return pl.pallas_call(body, name="sum_small", out_shape=jax.ShapeDtypeStruct(small_all.shape[1:], F32))(small_all)


def _rope_tables():
    inv_freq = ROPE_THETA ** (-jnp.arange(0, HEAD_DIM, 2, dtype=F32) / HEAD_DIM)
    ang = jnp.arange(SEQ_LEN).astype(F32)[:, None] * inv_freq[None, :]
    cos, sin = jnp.cos(ang), jnp.sin(ang)
    return jnp.tile(cos, (1, 4)), jnp.tile(jnp.concatenate([-sin, sin], axis=1), (1, 2))


def _local_step(x, target, g_pre, g_post, sinks, wt, wconv, squares, start_exchange=None):
    cos_t, sin_t = _rope_tables()
    wconv8 = jnp.pad(wconv, ((0, 5), (0, 0)))
    h, q, kv, g3 = _fwd_in_attn(x, g_pre, wt, cos_t, sin_t, 512)
    wpc, wpa, wout = squares(kv)
    a4, ya = _fwd_in_conv(h, wt, wconv8, wpc, 512)
    attn, ub = _fwd_attn(sinks, q, kv, g3)
    loss8, dout, dya, dub, dgab, dwout, dwpa, dgpost8 = _fwd_out_bwd_head(ya, ub, g3, x, target, g_post, wpa, wout, 256)
    dq, dza, dkv_own, dkv_prev, dsink8 = _bwd_attn(sinks, q, kv, attn, dub, g3, cos_t, sin_t)
    dkv = _bwd_kv_finish(dkv_own, dkv_prev, cos_t, sin_t)
    da4, dwpc, dwconv8 = _bwd_conv(dya, a4, wconv8, wpc, 512)
    dwt = _bwd_dw_in(h, da4, 0, 1024, 1024, "bwd_dw_in_conv", None)
    dwt = _bwd_dw_in(h, dq, ROW_Q, 1024, 1024, "bwd_dw_in_q", dwt)
    dwt = _bwd_dw_in(h, dkv, ROW_KV, 256, 1024, "bwd_dw_in_kv", dwt)
    dwt = _bwd_dw_in(h, dza, ROW_ZA, 1024, 1024, "bwd_dw_in_za", dwt)
    dwt32, dwt16 = _bwd_dw_in(h, dgab, ROW_GA, 1024, 1024, "bwd_dw_in_gates", dwt)
    token, pending = (None, None) if start_exchange is None else start_exchange(dwt32, dwt16, dwpc, dwpa, dwout)
    g_pre_after = g_pre if token is None else g_pre + token[0:1, 0:1]
    grad_x, dgpre8 = _bwd_dh(da4, dq, dkv, dza, dgab, wt, x, g_pre_after, dout, 256)
    small = jnp.concatenate([dgpre8, dgpost8, jnp.pad(dsink8, ((0, 0), (0, D - 128))), dwconv8,
                             jnp.pad(loss8, ((0, 0), (0, D - 128)))], axis=0)
    return loss8[0, 0], grad_x, dwt32, dwt16, dwpc, dwpa, dwout, small, pending


def kernel(x, g_pre, g_post, w_in, w_conv, sinks, w_proj_conv, w_proj_attn, w_out, loss_target, m_g_pre, m_g_post, m_w_in, m_w_conv, m_sinks, m_w_proj_conv, m_w_proj_attn, m_w_out, v_g_pre, v_g_post, v_w_in, v_w_conv, v_sinks, v_w_proj_conv, v_w_proj_attn, v_w_out):
    batch = x.shape[0]
    mx, my, mc, me = _place()
    c_arr = jnp.reshape(mc, (1,)).astype(jnp.int32)
    chip_arr = jnp.reshape(2 * mx + my, (1,)).astype(jnp.int32)

    g_wt, g_conv = _all_gather([w_in[0].T.astype(BF16), jnp.pad(w_conv[0], ((0, 5), (0, 0)))])
    wt = g_wt.reshape(D_IN, D)
    wconv = g_conv[:, 0:3, :].transpose(1, 0, 2).reshape(3, D)
    sq_mine = [w.astype(BF16) for w in (w_proj_conv[0], w_proj_attn[0], w_out[0])]
    wt, sq_mine = lax.optimization_barrier((wt, sq_mine))
    sq_send, sq_recv, sq_flying, sq_lands, sq_token = _gather_start(sq_mine, "gather_squares")

    def squares(after):
        got = _gather_wait(sq_send, sq_recv, sq_flying, sq_lands, after, "gather_squares")
        return [lax.dynamic_update_index_in_dim(full, mine, me, 0).reshape(D, D) for full, mine in zip(got, sq_mine)]

    def start_exchange(dwt32, dwt16, dwpc, dwpa, dwout):
        own_sq = [g.reshape(N_DEV, SHARD_SQ, D) for g in (dwpc, dwpa, dwout)]
        own_in = dwt32.reshape(N_DEV, SHARD_IN, D)
        from_sibling = _exchange_sibling([dwt16.reshape(N_DEV, SHARD_IN, D)] + [g.astype(BF16) for g in own_sq])
        in32, in16 = _pair_sum([own_in], from_sibling[:1], c_arr, SHARD_IN // 2, "pair_sum_w_in")
        sq32, sq16 = _pair_sum(own_sq, from_sibling[1:], c_arr, SHARD_SQ, "pair_sum_squares")
        send_sems, recv_sems, flying, lands, token = _exchange_chips_start(list(in16) + list(sq16))
        return token, (send_sems, recv_sems, flying, lands, in32, sq32)

    _, grad_x, _, _, _, _, _, small, pending = _local_step(
        x.reshape(batch * SEQ_LEN, D), loss_target.reshape(batch * SEQ_LEN, D), g_pre + sq_token[0:1, 0:1], g_post,
        sinks, wt, wconv, squares, start_exchange)
    sm_send, sm_recv, sm_flying, sm_lands, _ = _gather_start([small], "gather_small")
    send_sems, recv_sems, flying, lands, in32, sq32 = pending
    from_chips = _exchange_chips_wait(send_sems, recv_sems, flying, lands, small)
    (g_wt_mine,) = _chip_sum(in32, from_chips[:1], chip_arr, SHARD_IN // 2, "chip_sum_w_in")
    g_in_mine = g_wt_mine.T

    g_sq, d_sq, m_sq, v_sq = _chip_sum(
        sq32, from_chips[1:], chip_arr, SHARD_SQ, "chip_sum_adamw_squares",
        adam=([w_proj_conv[0], w_proj_attn[0], w_out[0]], [m_w_proj_conv[0], m_w_proj_attn[0], m_w_out[0]],
              [v_w_proj_conv[0], v_w_proj_attn[0], v_w_out[0]]))
    o_in = [o[0] for o in _adamw([w_in[0]], [g_in_mine], [m_w_in[0]], [v_w_in[0]], "adamw_w_in", tr=128)]
    (small_all,) = _gather_wait(sm_send, sm_recv, sm_flying, sm_lands, o_in[0], "gather_small")
    gs = _sum_small(lax.dynamic_update_index_in_dim(small_all, small, me, 0))
    g_g_pre, g_g_post, g_sinks, loss = gs[0:1], gs[8:9], gs[16:17, 0:N_HEADS], gs[32, 0]
    g_conv_mine = lax.dynamic_slice_in_dim(gs[24:27], me * SHARD_SQ, SHARD_SQ, axis=1)
    o_small = _adamw([g_pre, g_post, sinks, w_conv[0]], [g_g_pre, g_g_post, g_sinks, g_conv_mine],
                     [m_g_pre, m_g_post, m_sinks, m_w_conv[0]], [v_g_pre, v_g_post, v_sinks, v_w_conv[0]], "adamw_small")

    grads = [g_g_pre, g_g_post, g_in_mine[None], g_conv_mine[None], g_sinks] + [g[None] for g in g_sq]
    rest = []
    for idx, sq in enumerate((d_sq, m_sq, v_sq)):
        gp, gq, sk, cv = o_small[idx]
        rest += [gp, gq, o_in[idx][None], cv[None], sk] + [s[None] for s in sq]
    return (loss, grad_x.reshape(batch, SEQ_LEN, D), *grads, *rest)
```

```python
import functools

import jax
import jax.numpy as jnp
from jax import lax
from jax.experimental import pallas as pl
from jax.experimental.pallas import tpu as pltpu

D = 1024
N_HEADS = 16
HEAD_DIM = 64
LOGIT_SCALE = HEAD_DIM ** -0.5
BLK = 128
SEQ_LEN = 2048
D_IN = 8448
ROW_Q, ROW_KV, ROW_ZA, ROW_GA = 4 * D, 5 * D, 5 * D + 256, 6 * D + 256
SHARD_IN = D_IN // 8
SHARD_SQ = D // 8
N_DEV = 8
ROPE_THETA = 10000.0
RMS_EPS = 1e-6
NEG = -1e30
ADAM_LR, ADAM_B1, ADAM_B2, ADAM_EPS, ADAM_WD, ADAM_STEP = 0.001, 0.9, 0.999, 1e-08, 0.01, 10

F32 = jnp.float32
BF16 = jnp.bfloat16
MESH_ID = pl.DeviceIdType.MESH


def _dot(a, b):
    return jnp.dot(a, b, preferred_element_type=F32)


def _dot_nt(a, b):
    return lax.dot_general(a, b, (((1,), (1,)), ((), ())), preferred_element_type=F32)


def _dot_tn(a, b):
    return lax.dot_general(a, b, (((0,), (0,)), ((), ())), preferred_element_type=F32)


def _sig(z):
    return 1.0 / (1.0 + jnp.exp(-z))


def _swap_halves(z):
    lane = lax.broadcasted_iota(jnp.int32, z.shape, 1)
    return jnp.where((lane & 63) < 32, pltpu.roll(z, 96, 1), pltpu.roll(z, 32, 1))


def _row_spec(tm, width, col=0):
    return pl.BlockSpec((tm, width), lambda i: (i, col))


def _whole_vmem():
    return pl.BlockSpec(memory_space=pltpu.VMEM)


def _params(*sem):
    return pltpu.CompilerParams(dimension_semantics=sem)


def _fwd_in_attn(x, g_pre, wt, cos_t, sin_t, tm):
    t = x.shape[0]
    seq_tiles = SEQ_LEN // tm

    def body(x_ref, g_ref, w_ref, c_ref, s_ref, h_ref, q_ref, kv_ref, g3_ref):
        xf = x_ref[...]
        r = lax.rsqrt(jnp.mean(xf * xf, axis=-1, keepdims=True) + RMS_EPS)
        hh = ((xf * r) * g_ref[...]).astype(BF16)
        h_ref[...] = hh
        c = c_ref[...]
        s = s_ref[...]

        def rope(z):
            return z * c + _swap_halves(z) * s

        q = _dot_nt(hh, w_ref[ROW_Q:ROW_Q + D, :])
        for j in range(D // 128):
            q_ref[:, j * 128:(j + 1) * 128] = (rope(q[:, j * 128:(j + 1) * 128]) * LOGIT_SCALE).astype(BF16)
        kv = _dot_nt(hh, w_ref[ROW_KV:ROW_KV + 256, :])
        kv_ref[:, 0:128] = rope(kv[:, 0:128]).astype(BF16)
        kv_ref[:, 128:256] = kv[:, 128:256].astype(BF16)
        for j in range(3):
            g3_ref[:, j * D:(j + 1) * D] = _dot_nt(hh, w_ref[ROW_ZA + j * D:ROW_ZA + (j + 1) * D, :])

    tab = pl.BlockSpec((tm, 128), lambda i: (i % seq_tiles, 0))
    return pl.pallas_call(
        body, name="fwd_in_attn", grid=(t // tm,),
        in_specs=[_row_spec(tm, D), pl.BlockSpec((1, D), lambda i: (0, 0)), _whole_vmem(), tab, tab],
        out_specs=[_row_spec(tm, D), _row_spec(tm, D), _row_spec(tm, 256), _row_spec(tm, 3 * D)],
        out_shape=[jax.ShapeDtypeStruct((t, D), BF16), jax.ShapeDtypeStruct((t, D), BF16),
                   jax.ShapeDtypeStruct((t, 256), BF16), jax.ShapeDtypeStruct((t, 3 * D), F32)],
        compiler_params=_params("parallel"),
    )(x, g_pre, wt, cos_t, sin_t)


def _conv_forward(xc, bg, cg, zc, up6, up7, w_ref):
    tm = xc.shape[0]
    rows = lax.broadcasted_iota(jnp.int32, xc.shape, 0)
    u = cg * xc
    u_m1 = jnp.where(rows == 0, up7, pltpu.roll(u, 1, 0))
    u_m2 = jnp.where(rows == 0, up6, jnp.where(rows == 1, up7, pltpu.roll(u, 2, 0)))
    yconv = w_ref[0:1, :] * u_m2 + w_ref[1:2, :] * u_m1 + w_ref[2:3, :] * u
    sg = _sig(zc)
    sz = zc * sg
    co = bg * yconv
    del tm
    return u, u_m1, u_m2, yconv, sg, sz, co


def _fwd_in_conv(h, wt, wconv8, wpc, tm):
    t = h.shape[0]
    seq_tiles = SEQ_LEN // tm

    def body(h_ref, w_ref, wc_ref, wpc_ref, a4_ref, ya_ref, last_u_ref):
        hh = h_ref[...]
        xc, bg, cg, zc = (_dot_nt(hh, w_ref[j * D:(j + 1) * D, :]) for j in range(4))
        for j, z in enumerate((xc, bg, cg, zc)):
            a4_ref[:, j * D:(j + 1) * D] = z
        first = pl.program_id(0) % seq_tiles == 0
        up6 = jnp.where(first, 0.0, last_u_ref[6:7, :])
        up7 = jnp.where(first, 0.0, last_u_ref[7:8, :])
        u, _, _, _, _, sz, co = _conv_forward(xc, bg, cg, zc, up6, up7, wc_ref)
        last_u_ref[...] = u[tm - 8:tm, :]
        ya_ref[...] = _dot((sz * co).astype(BF16), wpc_ref[...])

    return pl.pallas_call(
        body, name="fwd_in_conv", grid=(t // tm,),
        in_specs=[_row_spec(tm, D), _whole_vmem(), pl.BlockSpec((8, D), lambda i: (0, 0)), _whole_vmem()],
        out_specs=[_row_spec(tm, 4 * D), _row_spec(tm, D)],
        out_shape=[jax.ShapeDtypeStruct((t, 4 * D), F32), jax.ShapeDtypeStruct((t, D), F32)],
        scratch_shapes=[pltpu.VMEM((8, D), F32)],
        compiler_params=_params("arbitrary"),
    )(h, wt, wconv8, wpc)


STACK = 4 * BLK


def _band_mask(first):
    qi = lax.broadcasted_iota(jnp.int32, (STACK, 2 * BLK), 0) & (BLK - 1)
    kj = lax.broadcasted_iota(jnp.int32, (STACK, 2 * BLK), 1)
    return (kj > qi) & (kj <= qi + BLK) & (kj >= jnp.where(first, BLK, 0))


def _masked_fill(sink_ref, g, e):
    kj = lax.broadcasted_iota(jnp.int32, (STACK, 2 * BLK), 1)
    sink = jnp.concatenate([jnp.full((BLK, 2 * BLK), sink_ref[0, 2 * (4 * g + jj) + e], F32) for jj in range(4)], axis=0)
    return jnp.where(kj == 0, sink, NEG)


def _padded_pair(kvp_ref, kvc_ref, col, other=0.0):
    z = jnp.concatenate([kvp_ref[:, col:col + 128], kvc_ref[:, col:col + 128]], axis=0).astype(F32)
    z = jnp.where(lax.broadcasted_iota(jnp.int32, z.shape, 0) == 0, 0.0, z)
    zs = pltpu.roll(z, 64, 1)
    lo = lax.broadcasted_iota(jnp.int32, z.shape, 1) < 64
    fill = jnp.full_like(z, other)
    left = [jnp.where(lo, z, fill).astype(BF16), jnp.where(lo, zs, fill).astype(BF16)]
    right = [jnp.where(lo, fill, zs).astype(BF16), jnp.where(lo, fill, z).astype(BF16)]
    return left, right


def _exp_logits(s, valid, fill):
    s = jnp.where(valid, s, fill)
    return jnp.exp(s - jnp.max(s, axis=-1, keepdims=True))


def _fwd_attn(sinks, q, kv, g3):
    t = q.shape[0]
    seq_blocks = SEQ_LEN // BLK

    def body(sink_ref, q_ref, kvc_ref, kvp_ref, za_ref, attn_ref, ub_ref):
        i = pl.program_id(0)
        valid = _band_mask(i % seq_blocks == 0)
        k_pad = _padded_pair(kvp_ref, kvc_ref, 0)
        v_one = _padded_pair(kvp_ref, kvc_ref, 128, other=1.0)
        lo = lax.broadcasted_iota(jnp.int32, (STACK, 128), 1) < 64
        for g in range(2):
            qg = jnp.concatenate([q_ref[:, j * 128:(j + 1) * 128] for j in range(4 * g, 4 * g + 4)], axis=0)
            pv = [_dot(_exp_logits(_dot_nt(qg, k_pad[e][g]), valid, _masked_fill(sink_ref, g, e)).astype(BF16),
                       v_one[e][g]) for e in range(2)]
            o = jnp.where(lo, pv[0], pv[1]) / pltpu.roll(jnp.where(lo, pv[1], pv[0]), 64, 1)
            for jj in range(4):
                cols = slice((4 * g + jj) * 128, (4 * g + jj + 1) * 128)
                oj = o[jj * BLK:(jj + 1) * BLK, :]
                attn_ref[:, cols] = oj
                za = za_ref[:, cols]
                ub_ref[:, cols] = (za * _sig(za) * oj).astype(BF16)

    return pl.pallas_call(
        body, name="fwd_attn", grid=(t // BLK,),
        in_specs=[pl.BlockSpec(memory_space=pltpu.SMEM), _row_spec(BLK, D), _row_spec(BLK, 256),
                  pl.BlockSpec((BLK, 256), lambda i: (jnp.maximum(i - 1, 0), 0)), _row_spec(BLK, D, 0)],
        out_specs=[_row_spec(BLK, D), _row_spec(BLK, D)],
        out_shape=[jax.ShapeDtypeStruct((t, D), F32), jax.ShapeDtypeStruct((t, D), BF16)],
        compiler_params=_params("parallel"),
    )(sinks, q, kv, kv, g3)


def _fwd_out_bwd_head(ya, ub, g3, x, target, g_post, wpa, wout, tm, parts=1):
    t = x.shape[0]

    def body(ya_ref, ub_ref, ga_ref, gb_ref, x_ref, tgt_ref, gp_ref, wpa_ref, wout_ref,
             loss_ref, dout_ref, dya_ref, dub_ref, dgab_ref, dwout_ref, dwpa_ref, dgp_ref):
        @pl.when(pl.program_id(0) == 0)
        def _():
            loss_ref[...] = jnp.zeros_like(loss_ref)
            dwout_ref[...] = jnp.zeros_like(dwout_ref)
            dwpa_ref[...] = jnp.zeros_like(dwpa_ref)
            dgp_ref[...] = jnp.zeros_like(dgp_ref)

        g = gp_ref[...]
        sq = jnp.zeros((1, 1), F32)
        dgp = jnp.zeros((1, D), F32)
        mbs, dys, dybs = [], [], []
        for part in range(parts):
            rows = slice(part * (tm // parts), (part + 1) * (tm // parts))
            ub = ub_ref[rows, :]
            ya = ya_ref[rows, :]
            yb = _dot(ub, wpa_ref[...])
            sa = _sig(ga_ref[rows, :])
            sb = _sig(gb_ref[rows, :])
            mb = (sa * ya + sb * yb).astype(BF16)
            y = _dot(mb, wout_ref[...])
            r = lax.rsqrt(jnp.mean(y * y, axis=-1, keepdims=True) + RMS_EPS)
            n = y * r
            err = (x_ref[rows, :] + n * g) - tgt_ref[rows, :]
            sq = sq + jnp.sum(jnp.sum(err * err, axis=0, keepdims=True), axis=1, keepdims=True)
            dout = err * (1.0 / D)
            dout_ref[rows, :] = dout
            dgp = dgp + jnp.sum(dout * n, axis=0, keepdims=True)
            dn = dout * g
            dy = (r * (dn - n * jnp.mean(dn * n, axis=-1, keepdims=True))).astype(BF16)
            dm = _dot_nt(dy, wout_ref[...])
            dya_ref[rows, :] = (dm * sa).astype(BF16)
            dyb = (dm * sb).astype(BF16)
            dgab_ref[rows, 0:D] = (dm * ya * (sa * (1.0 - sa))).astype(BF16)
            dgab_ref[rows, D:2 * D] = (dm * yb * (sb * (1.0 - sb))).astype(BF16)
            dub_ref[rows, :] = _dot_nt(dyb, wpa_ref[...])
            mbs.append(mb)
            dys.append(dy)
            dybs.append(dyb)
        loss_ref[...] += sq * (0.5 / D)
        dgp_ref[0:1, :] += dgp
        dwout_ref[...] += _dot_tn(jnp.concatenate(mbs, axis=0), jnp.concatenate(dys, axis=0))
        dwpa_ref[...] += _dot_tn(ub_ref[...], jnp.concatenate(dybs, axis=0))

    return pl.pallas_call(
        body, name="fwd_out_bwd_head", grid=(t // tm,),
        in_specs=[_row_spec(tm, D), _row_spec(tm, D), _row_spec(tm, D, 1), _row_spec(tm, D, 2),
                  _row_spec(tm, D), _row_spec(tm, D), pl.BlockSpec((1, D), lambda i: (0, 0)),
                  _whole_vmem(), _whole_vmem()],
        out_specs=[pl.BlockSpec((8, 128), lambda i: (0, 0)), _row_spec(tm, D), _row_spec(tm, D), _row_spec(tm, D),
                   _row_spec(tm, 2 * D), _whole_vmem(), _whole_vmem(), pl.BlockSpec((8, D), lambda i: (0, 0))],
        out_shape=[jax.ShapeDtypeStruct((8, 128), F32), jax.ShapeDtypeStruct((t, D), F32),
                   jax.ShapeDtypeStruct((t, D), BF16), jax.ShapeDtypeStruct((t, D), F32),
                   jax.ShapeDtypeStruct((t, 2 * D), BF16), jax.ShapeDtypeStruct((D, D), F32),
                   jax.ShapeDtypeStruct((D, D), F32), jax.ShapeDtypeStruct((8, D), F32)],
        compiler_params=_params("arbitrary"),
    )(ya, ub, g3, g3, x, target, g_post, wpa, wout)


def _bwd_attn(sinks, q, kv, attn, dub, g3, cos_t, sin_t):
    t = q.shape[0]
    seq_blocks = SEQ_LEN // BLK

    def body(sink_ref, q_ref, kvc_ref, kvp_ref, attn_ref, dub_ref, za_ref, c_ref, s_ref,
             dq_ref, dza_ref, dkv_own_ref, dkv_prev_ref, dsink_ref):
        i = pl.program_id(0)

        @pl.when(i == 0)
        def _():
            dsink_ref[...] = jnp.zeros_like(dsink_ref)

        valid = _band_mask(i % seq_blocks == 0)
        k_pad = _padded_pair(kvp_ref, kvc_ref, 0)
        v_pad = _padded_pair(kvp_ref, kvc_ref, 128)
        lo = lax.broadcasted_iota(jnp.int32, (STACK, 128), 1) < 64
        lane8 = lax.broadcasted_iota(jnp.int32, (8, 128), 1)
        c = c_ref[...]
        s = s_ref[...]
        dk_acc, dv_acc = [], []
        dsink = jnp.zeros((8, 128), F32)
        for g in range(2):
            qg, dog = [], []
            for j in range(4 * g, 4 * g + 4):
                cols = slice(j * 128, (j + 1) * 128)
                za = za_ref[:, cols]
                sg = _sig(za)
                dub = dub_ref[:, cols]
                dza_ref[:, cols] = (dub * attn_ref[:, cols] * (sg * (1.0 + za * (1.0 - sg)))).astype(BF16)
                dog.append((dub * (za * sg)).astype(BF16))
                qg.append(q_ref[:, cols])
            qg = jnp.concatenate(qg, axis=0)
            dog = jnp.concatenate(dog, axis=0)
            dq = jnp.zeros((STACK, 128), F32)
            ds_both, p_both = [], []
            for e in range(2):
                p = _exp_logits(_dot_nt(qg, k_pad[e][g]), valid, _masked_fill(sink_ref, g, e))
                p = p / jnp.sum(p, axis=-1, keepdims=True)
                dp = _dot_nt(dog, v_pad[e][g])
                ds = p * (dp - jnp.sum(p * dp, axis=-1, keepdims=True))
                for jj in range(4):
                    tot = jnp.sum(ds[jj * BLK:(jj + 1) * BLK, 0:1], axis=0, keepdims=True)
                    dsink = dsink + jnp.where(lane8 == 2 * (4 * g + jj) + e, tot, 0.0)
                ds = ds.astype(BF16)
                dq = dq + _dot(ds, k_pad[e][g])
                ds_both.append(ds)
                p_both.append(p.astype(BF16))
            zero = jnp.zeros_like(qg)
            q2 = jnp.concatenate([jnp.where(lo, qg, zero), jnp.where(lo, zero, qg)], axis=0)
            do2 = jnp.concatenate([jnp.where(lo, dog, zero), jnp.where(lo, zero, dog)], axis=0)
            dk_acc.append(_dot_tn(jnp.concatenate(ds_both, axis=0), q2))
            dv_acc.append(_dot_tn(jnp.concatenate(p_both, axis=0), do2))
            for jj in range(4):
                cols = slice((4 * g + jj) * 128, (4 * g + jj + 1) * 128)
                dqj = dq[jj * BLK:(jj + 1) * BLK, :] * LOGIT_SCALE
                dq_ref[:, cols] = (dqj * c - _swap_halves(dqj) * s).astype(BF16)
        dsink_ref[...] += dsink
        lo2 = lax.broadcasted_iota(jnp.int32, (2 * BLK, 128), 1) < 64
        sink_row = lax.broadcasted_iota(jnp.int32, (2 * BLK, 128), 0) == 0
        for col, acc in ((0, dk_acc), (128, dv_acc)):
            both = jnp.where(lo2, acc[0] + pltpu.roll(acc[0], 64, 1), acc[1] + pltpu.roll(acc[1], 64, 1))
            both = jnp.where(sink_row, 0.0, both)
            dkv_prev_ref[:, col:col + 128] = both[0:BLK, :]
            dkv_own_ref[:, col:col + 128] = both[BLK:2 * BLK, :]

    tab = pl.BlockSpec((BLK, 128), lambda i: (i % seq_blocks, 0))
    return pl.pallas_call(
        body, name="bwd_attn", grid=(t // BLK,),
        in_specs=[pl.BlockSpec(memory_space=pltpu.SMEM), _row_spec(BLK, D), _row_spec(BLK, 256),
                  pl.BlockSpec((BLK, 256), lambda i: (jnp.maximum(i - 1, 0), 0)),
                  _row_spec(BLK, D), _row_spec(BLK, D), _row_spec(BLK, D, 0), tab, tab],
        out_specs=[_row_spec(BLK, D), _row_spec(BLK, D), _row_spec(BLK, 256), _row_spec(BLK, 256),
                   pl.BlockSpec((8, 128), lambda i: (0, 0))],
        out_shape=[jax.ShapeDtypeStruct((t, D), BF16), jax.ShapeDtypeStruct((t, D), BF16),
                   jax.ShapeDtypeStruct((t, 256), F32), jax.ShapeDtypeStruct((t, 256), F32),
                   jax.ShapeDtypeStruct((8, 128), F32)],
        compiler_params=_params("arbitrary"),
    )(sinks, q, kv, kv, attn, dub, g3, cos_t, sin_t)


def _bwd_kv_finish(dkv_own, dkv_prev, cos_t, sin_t):
    t = dkv_own.shape[0]
    tm = 512
    seq_tiles = SEQ_LEN // tm
    n_blocks = t // BLK

    def body(own_ref, same_ref, nxt_ref, c_ref, s_ref, out_ref):
        keep = jnp.where(pl.program_id(0) % seq_tiles == seq_tiles - 1, 0.0, 1.0)
        shifted = jnp.concatenate([same_ref[BLK:tm, :], nxt_ref[...] * keep], axis=0)
        tot = own_ref[...] + shifted
        dk = tot[:, 0:128]
        out_ref[:, 0:128] = (dk * c_ref[...] - _swap_halves(dk) * s_ref[...]).astype(BF16)
        out_ref[:, 128:256] = tot[:, 128:256].astype(BF16)

    tab = pl.BlockSpec((tm, 128), lambda i: (i % seq_tiles, 0))
    return pl.pallas_call(
        body, name="bwd_kv_finish", grid=(t // tm,),
        in_specs=[_row_spec(tm, 256), _row_spec(tm, 256),
                  pl.BlockSpec((BLK, 256), lambda i: (jnp.minimum((i + 1) * (tm // BLK), n_blocks - 1), 0)), tab, tab],
        out_specs=_row_spec(tm, 256),
        out_shape=jax.ShapeDtypeStruct((t, 256), BF16),
        compiler_params=_params("parallel"),
    )(dkv_own, dkv_prev, dkv_prev, cos_t, sin_t)


def _bwd_conv(dya, a4, wconv8, wpc, tm):
    t = a4.shape[0]
    seq_tiles = SEQ_LEN // tm
    last8 = t // 8 - 1
    last16 = t // 16 - 1

    def body(dya_ref, xc_ref, bg_ref, cg_ref, zc_ref, xcp_ref, cgp_ref, dyan_ref, bgn_ref, zcn_ref,
             w_ref, wpc_ref, da4_ref, dwpc_ref, dwc_ref):
        i = pl.program_id(0)

        @pl.when(i == 0)
        def _():
            dwpc_ref[...] = jnp.zeros_like(dwpc_ref)
            dwc_ref[...] = jnp.zeros_like(dwc_ref)

        keep_prev = jnp.where(i % seq_tiles == 0, 0.0, 1.0)
        keep_next = jnp.where(i % seq_tiles == seq_tiles - 1, 0.0, 1.0)
        up6 = cgp_ref[6:7, :] * xcp_ref[6:7, :] * keep_prev
        up7 = cgp_ref[7:8, :] * xcp_ref[7:8, :] * keep_prev
        xc = xc_ref[...]
        bg = bg_ref[...]
        cg = cg_ref[...]
        zc = zc_ref[...]
        u, u_m1, u_m2, yconv, sg, sz, co = _conv_forward(xc, bg, cg, zc, up6, up7, w_ref)
        dya = dya_ref[...]
        dwpc_ref[...] += _dot_tn((sz * co).astype(BF16), dya)
        dua = _dot_nt(dya, wpc_ref[...])
        da4_ref[:, 3 * D:4 * D] = (dua * co * (sg * (1.0 + zc * (1.0 - sg)))).astype(BF16)
        dco = dua * sz
        da4_ref[:, D:2 * D] = (dco * yconv).astype(BF16)
        dyc = dco * bg
        dwc_ref[0:1, :] += jnp.sum(dyc * u_m2, axis=0, keepdims=True)
        dwc_ref[1:2, :] += jnp.sum(dyc * u_m1, axis=0, keepdims=True)
        dwc_ref[2:3, :] += jnp.sum(dyc * u, axis=0, keepdims=True)
        zcn = zcn_ref[...]
        dyc_n = _dot_nt(dyan_ref[...], wpc_ref[...])[0:8, :] * (zcn * _sig(zcn)) * bgn_ref[...] * keep_next
        rows = lax.broadcasted_iota(jnp.int32, xc.shape, 0)
        n0 = dyc_n[0:1, :]
        n1 = dyc_n[1:2, :]
        dyc_p1 = jnp.where(rows == tm - 1, n0, pltpu.roll(dyc, tm - 1, 0))
        dyc_p2 = jnp.where(rows == tm - 2, n0, jnp.where(rows == tm - 1, n1, pltpu.roll(dyc, tm - 2, 0)))
        du = w_ref[2:3, :] * dyc + w_ref[1:2, :] * dyc_p1 + w_ref[0:1, :] * dyc_p2
        da4_ref[:, 0:D] = (du * cg).astype(BF16)
        da4_ref[:, 2 * D:3 * D] = (du * xc).astype(BF16)

    def prev(col):
        return pl.BlockSpec((8, D), lambda i: (jnp.maximum(i * (tm // 8) - 1, 0), col))

    def nxt(col):
        return pl.BlockSpec((8, D), lambda i: (jnp.minimum((i + 1) * (tm // 8), last8), col))

    return pl.pallas_call(
        body, name="bwd_conv", grid=(t // tm,),
        in_specs=[_row_spec(tm, D), _row_spec(tm, D, 0), _row_spec(tm, D, 1), _row_spec(tm, D, 2), _row_spec(tm, D, 3),
                  prev(0), prev(2),
                  pl.BlockSpec((16, D), lambda i: (jnp.minimum((i + 1) * (tm // 16), last16), 0)), nxt(1), nxt(3),
                  pl.BlockSpec((8, D), lambda i: (0, 0)), _whole_vmem()],
        out_specs=[_row_spec(tm, 4 * D), _whole_vmem(), pl.BlockSpec((8, D), lambda i: (0, 0))],
        out_shape=[jax.ShapeDtypeStruct((t, 4 * D), BF16), jax.ShapeDtypeStruct((D, D), F32),
                   jax.ShapeDtypeStruct((8, D), F32)],
        compiler_params=_params("arbitrary"),
    )(dya, a4, a4, a4, a4, a4, a4, dya, a4, a4, wconv8, wpc)


def _bwd_dh(da4, dq, dkv, dza, dgab, wt, x, g_pre, dout, tm):
    t = x.shape[0]

    def body(da4_ref, dq_ref, dkv_ref, dza_ref, dgab_ref, w_ref, x_ref, g_ref, dout_ref, gx_ref, dg_ref):
        @pl.when(pl.program_id(0) == 0)
        def _():
            dg_ref[...] = jnp.zeros_like(dg_ref)

        dh = _dot(da4_ref[...], w_ref[0:ROW_Q, :])
        dh += _dot(dq_ref[...], w_ref[ROW_Q:ROW_KV, :])
        dh += _dot(dkv_ref[...], w_ref[ROW_KV:ROW_ZA, :])
        dh += _dot(dza_ref[...], w_ref[ROW_ZA:ROW_GA, :])
        dh += _dot(dgab_ref[...], w_ref[ROW_GA:D_IN, :])
        xf = x_ref[...]
        r = lax.rsqrt(jnp.mean(xf * xf, axis=-1, keepdims=True) + RMS_EPS)
        xn = xf * r
        dg_ref[0:1, :] += jnp.sum(dh * xn, axis=0, keepdims=True)
        dxn = dh * g_ref[...]
        gx_ref[...] = dout_ref[...] + r * (dxn - xn * jnp.mean(dxn * xn, axis=-1, keepdims=True))

    return pl.pallas_call(
        body, name="bwd_dh", grid=(t // tm,),
        in_specs=[_row_spec(tm, 4 * D), _row_spec(tm, D), _row_spec(tm, 256), _row_spec(tm, D), _row_spec(tm, 2 * D),
                  _whole_vmem(), _row_spec(tm, D), pl.BlockSpec((1, D), lambda i: (0, 0)), _row_spec(tm, D)],
        out_specs=[_row_spec(tm, D), pl.BlockSpec((8, D), lambda i: (0, 0))],
        out_shape=[jax.ShapeDtypeStruct((t, D), F32), jax.ShapeDtypeStruct((8, D), F32)],
        compiler_params=_params("arbitrary"),
    )(da4, dq, dkv, dza, dgab, wt, x, g_pre, dout)


def _bwd_dw_in(h, piece, row0, nb, tm, name, prev):
    t, n = piece.shape
    n_t = t // tm

    def body(*refs):
        h_ref, p_ref = refs[0], refs[1]
        o32_ref, o16_ref, acc_ref, acc16_ref, sems = refs[-5:]
        j, i = pl.program_id(0), pl.program_id(1)

        @pl.when(i == 0)
        def _():
            acc_ref[...] = jnp.zeros_like(acc_ref)

        acc_ref[...] += _dot_tn(p_ref[...], h_ref[...])

        @pl.when(i == n_t - 1)
        def _():
            acc16_ref[...] = acc_ref[...].astype(BF16)
            rows = pl.ds(pl.multiple_of(row0 + j * nb, 16), nb)
            c32 = pltpu.make_async_copy(acc_ref, o32_ref.at[rows], sems.at[0])
            c16 = pltpu.make_async_copy(acc16_ref, o16_ref.at[rows], sems.at[1])
            c32.start()
            c16.start()
            c32.wait()
            c16.wait()

    hbm = pl.BlockSpec(memory_space=pl.ANY)
    carried = [] if prev is None else list(prev)
    return pl.pallas_call(
        body, name=name, grid=(n // nb, n_t),
        in_specs=[pl.BlockSpec((tm, D), lambda j, i: (i, 0)), pl.BlockSpec((tm, nb), lambda j, i: (i, j))]
        + [hbm] * len(carried),
        out_specs=[hbm, hbm],
        out_shape=[jax.ShapeDtypeStruct((D_IN, D), F32), jax.ShapeDtypeStruct((D_IN, D), BF16)],
        scratch_shapes=[pltpu.VMEM((nb, D), F32), pltpu.VMEM((nb, D), BF16), pltpu.SemaphoreType.DMA((2,))],
        input_output_aliases={2: 0, 3: 1} if carried else {},
        compiler_params=_params("arbitrary", "arbitrary"),
    )(h, piece, *carried)


def _place():
    x, y, c = lax.axis_index("x"), lax.axis_index("y"), lax.axis_index("c")
    return x, y, c, 4 * x + 2 * y + c


def _peer(x, y, c, k):
    return (1 - x if k & 4 else x, 1 - y if k & 2 else y, 1 - c if k & 1 else c)


ICI_MASKS = (4, 2, 6)


def _all_gather(shards):
    n = len(shards)

    def body(*refs):
        src, dst = refs[:n], refs[n:2 * n]
        send_sems, recv_sems, local_sems = refs[2 * n:]
        x, y, c, me = _place()
        sibling = _peer(x, y, c, 1)

        def copy(a, s, block, to, own=False):
            return pltpu.make_async_remote_copy(
                src_ref=src[a] if own else dst[a].at[block], dst_ref=dst[a].at[block],
                send_sem=send_sems.at[a * 7 + s], recv_sem=recv_sems.at[a * 7 + s], device_id=to, device_id_type=MESH_ID)

        local = [pltpu.make_async_copy(src[a], dst[a].at[me], local_sems.at[a]) for a in range(n)]
        for cp in local:
            cp.start()
        started = [copy(a, 0, me, sibling, own=True) for a in range(n)]
        started += [copy(a, 1 + j, me, _peer(x, y, c, k), own=True) for j, k in enumerate(ICI_MASKS) for a in range(n)]
        for cp in started:
            cp.start()
        for j, k in enumerate(ICI_MASKS):
            for a in range(n):
                copy(a, 1 + j, me ^ k, sibling).wait_recv()
                fwd = copy(a, 4 + j, me ^ k, sibling)
                fwd.start()
                started.append(fwd)
        for a in range(n):
            copy(a, 0, me ^ 1, sibling).wait_recv()
        for j, k in enumerate(ICI_MASKS):
            for a in range(n):
                copy(a, 4 + j, me ^ 1 ^ k, sibling).wait_recv()
        for cp in started:
            cp.wait_send()
        for cp in local:
            cp.wait()

    hbm = pl.BlockSpec(memory_space=pl.ANY)
    return pl.pallas_call(
        body, name="all_gather_weights",
        in_specs=[hbm] * n, out_specs=[hbm] * n,
        out_shape=[jax.ShapeDtypeStruct((N_DEV,) + s.shape, s.dtype) for s in shards],
        scratch_shapes=[pltpu.SemaphoreType.DMA((7 * n,)), pltpu.SemaphoreType.DMA((7 * n,)),
                        pltpu.SemaphoreType.DMA((n,))],
    )(*shards)


def _direct_copies(src, land, send_sems, recv_sems):
    x, y, c, me = _place()
    return [pltpu.make_async_remote_copy(
        src_ref=src[a], dst_ref=land[a].at[me], send_sem=send_sems.at[a * 7 + k - 1],
        recv_sem=recv_sems.at[a * 7 + k - 1], device_id=_peer(x, y, c, k), device_id_type=MESH_ID)
        for k in range(1, N_DEV) for a in range(len(src))]


def _gather_start(shards, name):
    n = len(shards)

    def body(*refs):
        src, land = refs[:n], refs[n:2 * n]
        send_sems, recv_sems = refs[2 * n], refs[2 * n + 1]
        token_ref = refs[-1]
        for cp in _direct_copies(src, land, send_sems, recv_sems):
            cp.start()
        token_ref[...] = jnp.zeros_like(token_ref)

    hbm = pl.BlockSpec(memory_space=pltpu.HBM)
    sem = pl.BlockSpec(memory_space=pltpu.SEMAPHORE)
    lands = [lax.empty((N_DEV,) + s.shape, s.dtype) for s in shards]
    out = pl.pallas_call(
        body, name=name + "_start",
        out_shape=(pltpu.SemaphoreType.DMA((7 * n,)), pltpu.SemaphoreType.DMA((7 * n,)),
                   *[pltpu.HBM(s.shape, s.dtype) for s in shards], *[pltpu.HBM(s.shape, s.dtype) for s in lands],
                   jax.ShapeDtypeStruct((8, 128), F32)),
        in_specs=[hbm] * (2 * n), out_specs=(sem, sem, *[hbm] * (2 * n), _whole_vmem()),
        input_output_aliases={a: 2 + a for a in range(2 * n)},
        compiler_params=pltpu.CompilerParams(has_side_effects=pltpu.SideEffectType.DATAFLOW_SIDE_EFFECTING),
    )(*[pltpu.with_memory_space_constraint(s, pltpu.HBM) for s in list(shards) + lands])
    return out[0], out[1], out[2:2 + n], out[2 + n:2 + 2 * n], out[-1]


def _gather_wait(send_sems, recv_sems, flying, lands, after, name):
    n = len(flying)

    def body(*refs):
        src, land = refs[:n], refs[n:2 * n]
        for cp in _direct_copies(src, land, refs[2 * n], refs[2 * n + 1]):
            cp.wait_send()
            cp.wait_recv()

    hbm = pl.BlockSpec(memory_space=pltpu.HBM)
    sem = pl.BlockSpec(memory_space=pltpu.SEMAPHORE)
    out = pl.pallas_call(
        body, name=name + "_wait",
        out_shape=tuple(pltpu.HBM(s.shape, s.dtype) for s in list(flying) + list(lands)),
        in_specs=[hbm] * (2 * n) + [sem, sem, pl.BlockSpec(memory_space=pl.ANY)], out_specs=tuple([hbm] * (2 * n)),
        input_output_aliases={a: a for a in range(2 * n)},
        compiler_params=pltpu.CompilerParams(has_side_effects=pltpu.SideEffectType.DATAFLOW_SIDE_EFFECTING),
    )(*flying, *lands, send_sems, recv_sems, after)
    return out[n:]


def _exchange_sibling(by_dest):
    n = len(by_dest)

    def body(*refs):
        src, dst = refs[:n], refs[n:2 * n]
        send_sems, recv_sems = refs[2 * n:]
        x, y, c, _ = _place()
        sibling = _peer(x, y, c, 1)
        copies = [pltpu.make_async_remote_copy(
            src_ref=src[a].at[2 * p + (1 - c)], dst_ref=dst[a].at[p], send_sem=send_sems.at[a * 4 + p],
            recv_sem=recv_sems.at[a * 4 + p], device_id=sibling, device_id_type=MESH_ID)
            for a in range(n) for p in range(4)]
        for cp in copies:
            cp.start()
        for cp in copies:
            cp.wait_recv()
        for cp in copies:
            cp.wait_send()

    hbm = pl.BlockSpec(memory_space=pl.ANY)
    return pl.pallas_call(
        body, name="exchange_sibling", in_specs=[hbm] * n, out_specs=[hbm] * n,
        out_shape=[jax.ShapeDtypeStruct((4,) + s.shape[1:], s.dtype) for s in by_dest],
        scratch_shapes=[pltpu.SemaphoreType.DMA((4 * n,)), pltpu.SemaphoreType.DMA((4 * n,))],
    )(*by_dest)


def _chip_copies(src, land, send_sems, recv_sems):
    x, y, c, _ = _place()
    chip = 2 * x + y
    return [pltpu.make_async_remote_copy(
        src_ref=src[a].at[chip ^ (k >> 1)], dst_ref=land[a].at[j], send_sem=send_sems.at[a * 3 + j],
        recv_sem=recv_sems.at[a * 3 + j], device_id=_peer(x, y, c, k), device_id_type=MESH_ID)
        for j, k in enumerate(ICI_MASKS) for a in range(len(src))]


def _exchange_chips_start(by_chip):
    n = len(by_chip)

    def body(*refs):
        src, land = refs[:n], refs[n:2 * n]
        send_sems, recv_sems = refs[2 * n], refs[2 * n + 1]
        token_ref = refs[-1]
        for cp in _chip_copies(src, land, send_sems, recv_sems):
            cp.start()
        token_ref[...] = jnp.zeros_like(token_ref)

    hbm = pl.BlockSpec(memory_space=pltpu.HBM)
    sem = pl.BlockSpec(memory_space=pltpu.SEMAPHORE)
    lands = [lax.empty((3,) + s.shape[1:], s.dtype) for s in by_chip]
    out = pl.pallas_call(
        body, name="exchange_chips_start",
        out_shape=(pltpu.SemaphoreType.DMA((3 * n,)), pltpu.SemaphoreType.DMA((3 * n,)),
                   *[pltpu.HBM(s.shape, s.dtype) for s in by_chip], *[pltpu.HBM(s.shape, s.dtype) for s in lands],
                   jax.ShapeDtypeStruct((8, 128), F32)),
        in_specs=[hbm] * (2 * n), out_specs=(sem, sem, *[hbm] * (2 * n), _whole_vmem()),
        input_output_aliases={a: 2 + a for a in range(2 * n)},
        compiler_params=pltpu.CompilerParams(has_side_effects=pltpu.SideEffectType.DATAFLOW_SIDE_EFFECTING),
    )(*[pltpu.with_memory_space_constraint(s, pltpu.HBM) for s in list(by_chip) + lands])
    return out[0], out[1], out[2:2 + n], out[2 + n:2 + 2 * n], out[-1]


def _exchange_chips_wait(send_sems, recv_sems, flying, lands, after):
    n = len(flying)

    def body(*refs):
        src, land = refs[:n], refs[n:2 * n]
        send_sems_ref, recv_sems_ref = refs[2 * n], refs[2 * n + 1]
        for cp in _chip_copies(src, land, send_sems_ref, recv_sems_ref):
            cp.wait_send()
            cp.wait_recv()

    hbm = pl.BlockSpec(memory_space=pltpu.HBM)
    sem = pl.BlockSpec(memory_space=pltpu.SEMAPHORE)
    out = pl.pallas_call(
        body, name="exchange_chips_wait",
        out_shape=tuple(pltpu.HBM(s.shape, s.dtype) for s in list(flying) + list(lands)),
        in_specs=[hbm] * (2 * n) + [sem, sem, pl.BlockSpec(memory_space=pl.ANY)], out_specs=tuple([hbm] * (2 * n)),
        input_output_aliases={a: a for a in range(2 * n)},
        compiler_params=pltpu.CompilerParams(has_side_effects=pltpu.SideEffectType.DATAFLOW_SIDE_EFFECTING),
    )(*flying, *lands, send_sems, recv_sems, after)
    return out[n:]


def _adamw_math(w, g, m, v):
    m = ADAM_B1 * m + (1.0 - ADAM_B1) * g
    v = ADAM_B2 * v + (1.0 - ADAM_B2) * (g * g)
    m_hat = m / (1.0 - ADAM_B1 ** ADAM_STEP)
    v_hat = v / (1.0 - ADAM_B2 ** ADAM_STEP)
    return -ADAM_LR * (m_hat / (jnp.sqrt(v_hat) + ADAM_EPS) + ADAM_WD * w), m, v


def _pair_sum(owns, recvs, c_arr, tr, name):
    n = len(owns)
    _, rows, cols = owns[0].shape

    def body(c_ref, *refs):
        for a in range(n):
            s = refs[a][...] + refs[n + a][...].astype(F32)
            refs[2 * n + a][...] = s
            refs[3 * n + a][...] = s.astype(BF16)

    by_chip = pl.BlockSpec((None, tr, cols), lambda p, i, c_ref: (p, i, 0))
    mine = pl.BlockSpec((None, tr, cols), lambda p, i, c_ref: (2 * p + c_ref[0], i, 0))
    out = pl.pallas_call(
        body, name=name,
        grid_spec=pltpu.PrefetchScalarGridSpec(
            num_scalar_prefetch=1, grid=(4, rows // tr), in_specs=[mine] * n + [by_chip] * n, out_specs=[by_chip] * (2 * n)),
        out_shape=[jax.ShapeDtypeStruct((4, rows, cols), F32)] * n + [jax.ShapeDtypeStruct((4, rows, cols), BF16)] * n,
        compiler_params=_params("parallel", "parallel"),
    )(c_arr, *owns, *recvs)
    return out[:n], out[n:]


def _chip_sum(pairs, recvs, chip_arr, tr, name, adam=None):
    n = len(pairs)
    _, rows, cols = pairs[0].shape
    n_state = 0 if adam is None else 3 * n

    def body(chip_ref, *refs):
        outs = refs[2 * n + n_state:]
        for a in range(n):
            g = refs[a][...]
            for j in range(3):
                g = g + refs[n + a][j].astype(F32)
            outs[a][...] = g
            if adam is not None:
                w_ref, m_ref, v_ref = (refs[2 * n + s * n + a] for s in range(3))
                outs[n + a][...], outs[2 * n + a][...], outs[3 * n + a][...] = _adamw_math(w_ref[...], g, m_ref[...], v_ref[...])

    blk = pl.BlockSpec((tr, cols), lambda i, chip_ref: (i, 0))
    n_out = n if adam is None else 4 * n
    out = pl.pallas_call(
        body, name=name,
        grid_spec=pltpu.PrefetchScalarGridSpec(
            num_scalar_prefetch=1, grid=(rows // tr,),
            in_specs=[pl.BlockSpec((None, tr, cols), lambda i, chip_ref: (chip_ref[0], i, 0))] * n
            + [pl.BlockSpec((3, tr, cols), lambda i, chip_ref: (0, i, 0))] * n + [blk] * n_state,
            out_specs=[blk] * n_out),
        out_shape=[jax.ShapeDtypeStruct((rows, cols), F32)] * n_out,
        compiler_params=_params("parallel"),
    )(chip_arr, *pairs, *recvs, *([] if adam is None else [t for group in adam for t in group]))
    return out if adam is None else (out[:n], out[n:2 * n], out[2 * n:3 * n], out[3 * n:])


def _adamw(ws, gs, ms, vs, name):
    n = len(ws)

    def body(*refs):
        for a in range(n):
            w_ref, g_ref, m_ref, v_ref = (refs[s * n + a] for s in range(4))
            refs[4 * n + a][...], refs[5 * n + a][...], refs[6 * n + a][...] = _adamw_math(
                w_ref[...], g_ref[...], m_ref[...], v_ref[...])

    out = pl.pallas_call(body, name=name, out_shape=[jax.ShapeDtypeStruct(w.shape, F32) for w in ws] * 3)(
        *ws, *gs, *ms, *vs)
    return out[:n], out[n:2 * n], out[2 * n:]


def _sum_small(small_all):
    def body(s_ref, o_ref):
        g = s_ref[0]
        for d in range(1, N_DEV):
            g = g + s_ref[d]
        o_ref[...] = g

    return pl.pallas_call(body, name="sum_small", out_shape=jax.ShapeDtypeStruct(small_all.shape[1:], F32))(small_all)


def _rope_tables():
    inv_freq = ROPE_THETA ** (-jnp.arange(0, HEAD_DIM, 2, dtype=F32) / HEAD_DIM)
    ang = jnp.arange(SEQ_LEN).astype(F32)[:, None] * inv_freq[None, :]
    cos, sin = jnp.cos(ang), jnp.sin(ang)
    return jnp.tile(cos, (1, 4)), jnp.tile(jnp.concatenate([-sin, sin], axis=1), (1, 2))


def _local_step(x, target, g_pre, g_post, sinks, wt, wconv, squares, start_exchange=None):
    cos_t, sin_t = _rope_tables()
    wconv8 = jnp.pad(wconv, ((0, 5), (0, 0)))
    h, q, kv, g3 = _fwd_in_attn(x, g_pre, wt, cos_t, sin_t, 512)
    wpc, wpa, wout = squares(kv)
    a4, ya = _fwd_in_conv(h, wt, wconv8, wpc, 512)
    attn, ub = _fwd_attn(sinks, q, kv, g3)
    loss8, dout, dya, dub, dgab, dwout, dwpa, dgpost8 = _fwd_out_bwd_head(ya, ub, g3, x, target, g_post, wpa, wout, 256)
    dq, dza, dkv_own, dkv_prev, dsink8 = _bwd_attn(sinks, q, kv, attn, dub, g3, cos_t, sin_t)
    dkv = _bwd_kv_finish(dkv_own, dkv_prev, cos_t, sin_t)
    da4, dwpc, dwconv8 = _bwd_conv(dya, a4, wconv8, wpc, 512)
    dwt = _bwd_dw_in(h, da4, 0, 1024, 1024, "bwd_dw_in_conv", None)
    dwt = _bwd_dw_in(h, dq, ROW_Q, 1024, 1024, "bwd_dw_in_q", dwt)
    dwt = _bwd_dw_in(h, dkv, ROW_KV, 256, 1024, "bwd_dw_in_kv", dwt)
    dwt = _bwd_dw_in(h, dza, ROW_ZA, 1024, 1024, "bwd_dw_in_za", dwt)
    dwt32, dwt16 = _bwd_dw_in(h, dgab, ROW_GA, 1024, 1024, "bwd_dw_in_gates", dwt)
    token, pending = (None, None) if start_exchange is None else start_exchange(dwt32, dwt16, dwpc, dwpa, dwout)
    g_pre_after = g_pre if token is None else g_pre + token[0:1, 0:1]
    grad_x, dgpre8 = _bwd_dh(da4, dq, dkv, dza, dgab, wt, x, g_pre_after, dout, 256)
    small = jnp.concatenate([dgpre8, dgpost8, jnp.pad(dsink8, ((0, 0), (0, D - 128))), dwconv8,
                             jnp.pad(loss8, ((0, 0), (0, D - 128)))], axis=0)
    return loss8[0, 0], grad_x, dwt32, dwt16, dwpc, dwpa, dwout, small, pending


def kernel(x, g_pre, g_post, w_in, w_conv, sinks, w_proj_conv, w_proj_attn, w_out, loss_target, m_g_pre, m_g_post, m_w_in, m_w_conv, m_sinks, m_w_proj_conv, m_w_proj_attn, m_w_out, v_g_pre, v_g_post, v_w_in, v_w_conv, v_sinks, v_w_proj_conv, v_w_proj_attn, v_w_out):
    batch = x.shape[0]
    mx, my, mc, me = _place()
    c_arr = jnp.reshape(mc, (1,)).astype(jnp.int32)
    chip_arr = jnp.reshape(2 * mx + my, (1,)).astype(jnp.int32)

    g_wt, g_conv = _all_gather([w_in[0].T.astype(BF16), jnp.pad(w_conv[0], ((0, 5), (0, 0)))])
    wt = g_wt.reshape(D_IN, D)
    wconv = g_conv[:, 0:3, :].transpose(1, 0, 2).reshape(3, D)
    sq_mine = [w.astype(BF16) for w in (w_proj_conv[0], w_proj_attn[0], w_out[0])]
    wt, sq_mine = lax.optimization_barrier((wt, sq_mine))
    sq_send, sq_recv, sq_flying, sq_lands, sq_token = _gather_start(sq_mine, "gather_squares")

    def squares(after):
        got = _gather_wait(sq_send, sq_recv, sq_flying, sq_lands, after, "gather_squares")
        return [lax.dynamic_update_index_in_dim(full, mine, me, 0).reshape(D, D) for full, mine in zip(got, sq_mine)]

    def start_exchange(dwt32, dwt16, dwpc, dwpa, dwout):
        own_sq = [g.reshape(N_DEV, SHARD_SQ, D) for g in (dwpc, dwpa, dwout)]
        own_in = dwt32.reshape(N_DEV, SHARD_IN, D)
        from_sibling = _exchange_sibling([dwt16.reshape(N_DEV, SHARD_IN, D)] + [g.astype(BF16) for g in own_sq])
        in32, in16 = _pair_sum([own_in], from_sibling[:1], c_arr, SHARD_IN // 2, "pair_sum_w_in")
        sq32, sq16 = _pair_sum(own_sq, from_sibling[1:], c_arr, SHARD_SQ, "pair_sum_squares")
        send_sems, recv_sems, flying, lands, token = _exchange_chips_start(list(in16) + list(sq16))
        return token, (send_sems, recv_sems, flying, lands, in32, sq32)

    _, grad_x, _, _, _, _, _, small, pending = _local_step(
        x.reshape(batch * SEQ_LEN, D), loss_target.reshape(batch * SEQ_LEN, D), g_pre + sq_token[0:1, 0:1], g_post,
        sinks, wt, wconv, squares, start_exchange)
    sm_send, sm_recv, sm_flying, sm_lands, _ = _gather_start([small], "gather_small")
    send_sems, recv_sems, flying, lands, in32, sq32 = pending
    from_chips = _exchange_chips_wait(send_sems, recv_sems, flying, lands, small)

    o_in = [o[0].T for o in _chip_sum(
        in32, from_chips[:1], chip_arr, SHARD_IN // 3, "chip_sum_adamw_w_in",
        adam=([w_in[0].T], [m_w_in[0].T], [v_w_in[0].T]))]
    g_in_mine, o_in = o_in[0], o_in[1:]
    g_sq, d_sq, m_sq, v_sq = _chip_sum(
        sq32, from_chips[1:], chip_arr, SHARD_SQ, "chip_sum_adamw_squares",
        adam=([w_proj_conv[0], w_proj_attn[0], w_out[0]], [m_w_proj_conv[0], m_w_proj_attn[0], m_w_out[0]],
              [v_w_proj_conv[0], v_w_proj_attn[0], v_w_out[0]]))
    (small_all,) = _gather_wait(sm_send, sm_recv, sm_flying, sm_lands, d_sq[0], "gather_small")
    gs = _sum_small(lax.dynamic_update_index_in_dim(small_all, small, me, 0))
    g_g_pre, g_g_post, g_sinks, loss = gs[0:1], gs[8:9], gs[16:17, 0:N_HEADS], gs[32, 0]
    g_conv_mine = lax.dynamic_slice_in_dim(gs[24:27], me * SHARD_SQ, SHARD_SQ, axis=1)
    o_small = _adamw([g_pre, g_post, sinks, w_conv[0]], [g_g_pre, g_g_post, g_sinks, g_conv_mine],
                     [m_g_pre, m_g_post, m_sinks, m_w_conv[0]], [v_g_pre, v_g_post, v_sinks, v_w_conv[0]], "adamw_small")

    grads = [g_g_pre, g_g_post, g_in_mine[None], g_conv_mine[None], g_sinks] + [g[None] for g in g_sq]
    rest = []
    for idx, sq in enumerate((d_sq, m_sq, v_sq)):
        gp, gq, sk, cv = o_small[idx]
        rest += [gp, gq, o_in[idx][None], cv[None], sk] + [s[None] for s in sq]
    return (loss, grad_x.reshape(batch, SEQ_LEN, D), *grads, *rest)
```

```python
import functools

import jax
import jax.numpy as jnp
from jax import lax
from jax.experimental import pallas as pl
from jax.experimental.pallas import tpu as pltpu

D = 1024
N_HEADS = 16
HEAD_DIM = 64
LOGIT_SCALE = HEAD_DIM ** -0.5
BLK = 128
SEQ_LEN = 2048
D_IN = 8448
ROW_Q, ROW_KV, ROW_ZA, ROW_GA = 4 * D, 5 * D, 5 * D + 256, 6 * D + 256
SHARD_IN = D_IN // 8
SHARD_SQ = D // 8
N_DEV = 8
ROPE_THETA = 10000.0
RMS_EPS = 1e-6
NEG = -1e30
ADAM_LR, ADAM_B1, ADAM_B2, ADAM_EPS, ADAM_WD, ADAM_STEP = 0.001, 0.9, 0.999, 1e-08, 0.01, 10

F32 = jnp.float32
BF16 = jnp.bfloat16
MESH_ID = pl.DeviceIdType.MESH


def _dot(a, b):
    return jnp.dot(a, b, preferred_element_type=F32)


def _dot_nt(a, b):
    return lax.dot_general(a, b, (((1,), (1,)), ((), ())), preferred_element_type=F32)


def _dot_tn(a, b):
    return lax.dot_general(a, b, (((0,), (0,)), ((), ())), preferred_element_type=F32)


def _sig(z):
    return 1.0 / (1.0 + jnp.exp(-z))


def _swap_halves(z):
    lane = lax.broadcasted_iota(jnp.int32, z.shape, 1)
    return jnp.where((lane & 63) < 32, pltpu.roll(z, 96, 1), pltpu.roll(z, 32, 1))


def _row_spec(tm, width, col=0):
    return pl.BlockSpec((tm, width), lambda i: (i, col))


def _whole_vmem():
    return pl.BlockSpec(memory_space=pltpu.VMEM)


def _params(*sem):
    return pltpu.CompilerParams(dimension_semantics=sem)


def _fwd_in_attn(x, g_pre, wt, cos_t, sin_t, tm):
    t = x.shape[0]
    seq_tiles = SEQ_LEN // tm

    def body(x_ref, g_ref, w_ref, c_ref, s_ref, h_ref, q_ref, kv_ref, g3_ref):
        xf = x_ref[...]
        r = lax.rsqrt(jnp.mean(xf * xf, axis=-1, keepdims=True) + RMS_EPS)
        hh = ((xf * r) * g_ref[...]).astype(BF16)
        h_ref[...] = hh
        c = c_ref[...]
        s = s_ref[...]

        def rope(z):
            return z * c + _swap_halves(z) * s

        q = _dot_nt(hh, w_ref[ROW_Q:ROW_Q + D, :])
        for j in range(D // 128):
            q_ref[:, j * 128:(j + 1) * 128] = (rope(q[:, j * 128:(j + 1) * 128]) * LOGIT_SCALE).astype(BF16)
        kv = _dot_nt(hh, w_ref[ROW_KV:ROW_KV + 256, :])
        kv_ref[:, 0:128] = rope(kv[:, 0:128]).astype(BF16)
        kv_ref[:, 128:256] = kv[:, 128:256].astype(BF16)
        for j in range(3):
            g3_ref[:, j * D:(j + 1) * D] = _dot_nt(hh, w_ref[ROW_ZA + j * D:ROW_ZA + (j + 1) * D, :])

    tab = pl.BlockSpec((tm, 128), lambda i: (i % seq_tiles, 0))
    return pl.pallas_call(
        body, name="fwd_in_attn", grid=(t // tm,),
        in_specs=[_row_spec(tm, D), pl.BlockSpec((1, D), lambda i: (0, 0)), _whole_vmem(), tab, tab],
        out_specs=[_row_spec(tm, D), _row_spec(tm, D), _row_spec(tm, 256), _row_spec(tm, 3 * D)],
        out_shape=[jax.ShapeDtypeStruct((t, D), BF16), jax.ShapeDtypeStruct((t, D), BF16),
                   jax.ShapeDtypeStruct((t, 256), BF16), jax.ShapeDtypeStruct((t, 3 * D), F32)],
        compiler_params=_params("parallel"),
    )(x, g_pre, wt, cos_t, sin_t)


def _conv_forward(xc, bg, cg, zc, up6, up7, w_ref):
    tm = xc.shape[0]
    rows = lax.broadcasted_iota(jnp.int32, xc.shape, 0)
    u = cg * xc
    u_m1 = jnp.where(rows == 0, up7, pltpu.roll(u, 1, 0))
    u_m2 = jnp.where(rows == 0, up6, jnp.where(rows == 1, up7, pltpu.roll(u, 2, 0)))
    yconv = w_ref[0:1, :] * u_m2 + w_ref[1:2, :] * u_m1 + w_ref[2:3, :] * u
    sg = _sig(zc)
    sz = zc * sg
    co = bg * yconv
    del tm
    return u, u_m1, u_m2, yconv, sg, sz, co


def _fwd_in_conv(h, wt, wconv8, wpc, tm):
    t = h.shape[0]
    seq_tiles = SEQ_LEN // tm

    def body(h_ref, w_ref, wc_ref, wpc_ref, a4_ref, ya_ref, last_u_ref):
        hh = h_ref[...]
        xc, bg, cg, zc = (_dot_nt(hh, w_ref[j * D:(j + 1) * D, :]) for j in range(4))
        for j, z in enumerate((xc, bg, cg, zc)):
            a4_ref[:, j * D:(j + 1) * D] = z.astype(BF16)
        first = pl.program_id(0) % seq_tiles == 0
        up6 = jnp.where(first, 0.0, last_u_ref[6:7, :])
        up7 = jnp.where(first, 0.0, last_u_ref[7:8, :])
        u, _, _, _, _, sz, co = _conv_forward(xc, bg, cg, zc, up6, up7, wc_ref)
        last_u_ref[...] = u[tm - 8:tm, :]
        ya_ref[...] = _dot((sz * co).astype(BF16), wpc_ref[...])

    return pl.pallas_call(
        body, name="fwd_in_conv", grid=(t // tm,),
        in_specs=[_row_spec(tm, D), _whole_vmem(), pl.BlockSpec((8, D), lambda i: (0, 0)), _whole_vmem()],
        out_specs=[_row_spec(tm, 4 * D), _row_spec(tm, D)],
        out_shape=[jax.ShapeDtypeStruct((t, 4 * D), BF16), jax.ShapeDtypeStruct((t, D), F32)],
        scratch_shapes=[pltpu.VMEM((8, D), F32)],
        compiler_params=_params("arbitrary"),
    )(h, wt, wconv8, wpc)


STACK = 4 * BLK


def _band_mask(first):
    qi = lax.broadcasted_iota(jnp.int32, (STACK, 2 * BLK), 0) & (BLK - 1)
    kj = lax.broadcasted_iota(jnp.int32, (STACK, 2 * BLK), 1)
    return (kj > qi) & (kj <= qi + BLK) & (kj >= jnp.where(first, BLK, 0))


def _masked_fill(sink_ref, g, e):
    kj = lax.broadcasted_iota(jnp.int32, (STACK, 2 * BLK), 1)
    sink = jnp.concatenate([jnp.full((BLK, 2 * BLK), sink_ref[0, 2 * (4 * g + jj) + e], F32) for jj in range(4)], axis=0)
    return jnp.where(kj == 0, sink, NEG)


def _padded_pair(kvp_ref, kvc_ref, col, other=0.0):
    z = jnp.concatenate([kvp_ref[:, col:col + 128], kvc_ref[:, col:col + 128]], axis=0).astype(F32)
    z = jnp.where(lax.broadcasted_iota(jnp.int32, z.shape, 0) == 0, 0.0, z)
    zs = pltpu.roll(z, 64, 1)
    lo = lax.broadcasted_iota(jnp.int32, z.shape, 1) < 64
    fill = jnp.full_like(z, other)
    left = [jnp.where(lo, z, fill).astype(BF16), jnp.where(lo, zs, fill).astype(BF16)]
    right = [jnp.where(lo, fill, zs).astype(BF16), jnp.where(lo, fill, z).astype(BF16)]
    return left, right


def _exp_logits(s, valid, fill):
    s = jnp.where(valid, s, fill)
    return jnp.exp(s - jnp.max(s, axis=-1, keepdims=True))


def _fwd_attn(sinks, q, kv, g3):
    t = q.shape[0]
    seq_blocks = SEQ_LEN // BLK

    def body(sink_ref, q_ref, kvc_ref, kvp_ref, za_ref, attn_ref, ub_ref):
        i = pl.program_id(0)
        valid = _band_mask(i % seq_blocks == 0)
        k_pad = _padded_pair(kvp_ref, kvc_ref, 0)
        v_one = _padded_pair(kvp_ref, kvc_ref, 128, other=1.0)
        lo = lax.broadcasted_iota(jnp.int32, (STACK, 128), 1) < 64
        for g in range(2):
            qg = jnp.concatenate([q_ref[:, j * 128:(j + 1) * 128] for j in range(4 * g, 4 * g + 4)], axis=0)
            pv = [_dot(_exp_logits(_dot_nt(qg, k_pad[e][g]), valid, _masked_fill(sink_ref, g, e)).astype(BF16),
                       v_one[e][g]) for e in range(2)]
            o = jnp.where(lo, pv[0], pv[1]) / pltpu.roll(jnp.where(lo, pv[1], pv[0]), 64, 1)
            for jj in range(4):
                cols = slice((4 * g + jj) * 128, (4 * g + jj + 1) * 128)
                oj = o[jj * BLK:(jj + 1) * BLK, :]
                attn_ref[:, cols] = oj
                za = za_ref[:, cols]
                ub_ref[:, cols] = (za * _sig(za) * oj).astype(BF16)

    return pl.pallas_call(
        body, name="fwd_attn", grid=(t // BLK,),
        in_specs=[pl.BlockSpec(memory_space=pltpu.SMEM), _row_spec(BLK, D), _row_spec(BLK, 256),
                  pl.BlockSpec((BLK, 256), lambda i: (jnp.maximum(i - 1, 0), 0)), _row_spec(BLK, D, 0)],
        out_specs=[_row_spec(BLK, D), _row_spec(BLK, D)],
        out_shape=[jax.ShapeDtypeStruct((t, D), F32), jax.ShapeDtypeStruct((t, D), BF16)],
        compiler_params=_params("parallel"),
    )(sinks, q, kv, kv, g3)


def _fwd_out_bwd_head(ya, ub, g3, x, target, g_post, wpa, wout, tm, parts=1):
    t = x.shape[0]

    def body(ya_ref, ub_ref, ga_ref, gb_ref, x_ref, tgt_ref, gp_ref, wpa_ref, wout_ref,
             loss_ref, dout_ref, dya_ref, dub_ref, dgab_ref, dwout_ref, dwpa_ref, dgp_ref):
        @pl.when(pl.program_id(0) == 0)
        def _():
            loss_ref[...] = jnp.zeros_like(loss_ref)
            dwout_ref[...] = jnp.zeros_like(dwout_ref)
            dwpa_ref[...] = jnp.zeros_like(dwpa_ref)
            dgp_ref[...] = jnp.zeros_like(dgp_ref)

        g = gp_ref[...]
        sq = jnp.zeros((1, 1), F32)
        dgp = jnp.zeros((1, D), F32)
        mbs, dys, dybs = [], [], []
        for part in range(parts):
            rows = slice(part * (tm // parts), (part + 1) * (tm // parts))
            ub = ub_ref[rows, :]
            ya = ya_ref[rows, :]
            yb = _dot(ub, wpa_ref[...])
            sa = _sig(ga_ref[rows, :])
            sb = _sig(gb_ref[rows, :])
            mb = (sa * ya + sb * yb).astype(BF16)
            y = _dot(mb, wout_ref[...])
            r = lax.rsqrt(jnp.mean(y * y, axis=-1, keepdims=True) + RMS_EPS)
            n = y * r
            err = (x_ref[rows, :] + n * g) - tgt_ref[rows, :]
            sq = sq + jnp.sum(jnp.sum(err * err, axis=0, keepdims=True), axis=1, keepdims=True)
            dout = err * (1.0 / D)
            dout_ref[rows, :] = dout
            dgp = dgp + jnp.sum(dout * n, axis=0, keepdims=True)
            dn = dout * g
            dy = (r * (dn - n * jnp.mean(dn * n, axis=-1, keepdims=True))).astype(BF16)
            dm = _dot_nt(dy, wout_ref[...])
            dya_ref[rows, :] = (dm * sa).astype(BF16)
            dyb = (dm * sb).astype(BF16)
            dgab_ref[rows, 0:D] = (dm * ya * (sa * (1.0 - sa))).astype(BF16)
            dgab_ref[rows, D:2 * D] = (dm * yb * (sb * (1.0 - sb))).astype(BF16)
            dub_ref[rows, :] = _dot_nt(dyb, wpa_ref[...])
            mbs.append(mb)
            dys.append(dy)
            dybs.append(dyb)
        loss_ref[...] += sq * (0.5 / D)
        dgp_ref[0:1, :] += dgp
        dwout_ref[...] += _dot_tn(jnp.concatenate(mbs, axis=0), jnp.concatenate(dys, axis=0))
        dwpa_ref[...] += _dot_tn(ub_ref[...], jnp.concatenate(dybs, axis=0))

    return pl.pallas_call(
        body, name="fwd_out_bwd_head", grid=(t // tm,),
        in_specs=[_row_spec(tm, D), _row_spec(tm, D), _row_spec(tm, D, 1), _row_spec(tm, D, 2),
                  _row_spec(tm, D), _row_spec(tm, D), pl.BlockSpec((1, D), lambda i: (0, 0)),
                  _whole_vmem(), _whole_vmem()],
        out_specs=[pl.BlockSpec((8, 128), lambda i: (0, 0)), _row_spec(tm, D), _row_spec(tm, D), _row_spec(tm, D),
                   _row_spec(tm, 2 * D), _whole_vmem(), _whole_vmem(), pl.BlockSpec((8, D), lambda i: (0, 0))],
        out_shape=[jax.ShapeDtypeStruct((8, 128), F32), jax.ShapeDtypeStruct((t, D), F32),
                   jax.ShapeDtypeStruct((t, D), BF16), jax.ShapeDtypeStruct((t, D), F32),
                   jax.ShapeDtypeStruct((t, 2 * D), BF16), jax.ShapeDtypeStruct((D, D), F32),
                   jax.ShapeDtypeStruct((D, D), F32), jax.ShapeDtypeStruct((8, D), F32)],
        compiler_params=_params("arbitrary"),
    )(ya, ub, g3, g3, x, target, g_post, wpa, wout)


def _bwd_attn(sinks, q, kv, attn, dub, g3, cos_t, sin_t):
    t = q.shape[0]
    seq_blocks = SEQ_LEN // BLK

    def body(sink_ref, q_ref, kvc_ref, kvp_ref, attn_ref, dub_ref, za_ref, c_ref, s_ref,
             dq_ref, dza_ref, dkv_own_ref, dkv_prev_ref, dsink_ref):
        i = pl.program_id(0)

        @pl.when(i == 0)
        def _():
            dsink_ref[...] = jnp.zeros_like(dsink_ref)

        valid = _band_mask(i % seq_blocks == 0)
        k_pad = _padded_pair(kvp_ref, kvc_ref, 0)
        v_pad = _padded_pair(kvp_ref, kvc_ref, 128)
        lo = lax.broadcasted_iota(jnp.int32, (STACK, 128), 1) < 64
        lane8 = lax.broadcasted_iota(jnp.int32, (8, 128), 1)
        c = c_ref[...]
        s = s_ref[...]
        dk_acc, dv_acc = [], []
        dsink = jnp.zeros((8, 128), F32)
        for g in range(2):
            qg, dog = [], []
            for j in range(4 * g, 4 * g + 4):
                cols = slice(j * 128, (j + 1) * 128)
                za = za_ref[:, cols]
                sg = _sig(za)
                dub = dub_ref[:, cols]
                dza_ref[:, cols] = (dub * attn_ref[:, cols] * (sg * (1.0 + za * (1.0 - sg)))).astype(BF16)
                dog.append((dub * (za * sg)).astype(BF16))
                qg.append(q_ref[:, cols])
            qg = jnp.concatenate(qg, axis=0)
            dog = jnp.concatenate(dog, axis=0)
            dq = jnp.zeros((STACK, 128), F32)
            ds_both, p_both = [], []
            for e in range(2):
                p = _exp_logits(_dot_nt(qg, k_pad[e][g]), valid, _masked_fill(sink_ref, g, e))
                p = p / jnp.sum(p, axis=-1, keepdims=True)
                dp = _dot_nt(dog, v_pad[e][g])
                ds = p * (dp - jnp.sum(p * dp, axis=-1, keepdims=True))
                for jj in range(4):
                    tot = jnp.sum(ds[jj * BLK:(jj + 1) * BLK, 0:1], axis=0, keepdims=True)
                    dsink = dsink + jnp.where(lane8 == 2 * (4 * g + jj) + e, tot, 0.0)
                ds = ds.astype(BF16)
                dq = dq + _dot(ds, k_pad[e][g])
                ds_both.append(ds)
                p_both.append(p.astype(BF16))
            zero = jnp.zeros_like(qg)
            q2 = jnp.concatenate([jnp.where(lo, qg, zero), jnp.where(lo, zero, qg)], axis=0)
            do2 = jnp.concatenate([jnp.where(lo, dog, zero), jnp.where(lo, zero, dog)], axis=0)
            dk_acc.append(_dot_tn(jnp.concatenate(ds_both, axis=0), q2))
            dv_acc.append(_dot_tn(jnp.concatenate(p_both, axis=0), do2))
            for jj in range(4):
                cols = slice((4 * g + jj) * 128, (4 * g + jj + 1) * 128)
                dqj = dq[jj * BLK:(jj + 1) * BLK, :] * LOGIT_SCALE
                dq_ref[:, cols] = (dqj * c - _swap_halves(dqj) * s).astype(BF16)
        dsink_ref[...] += dsink
        lo2 = lax.broadcasted_iota(jnp.int32, (2 * BLK, 128), 1) < 64
        sink_row = lax.broadcasted_iota(jnp.int32, (2 * BLK, 128), 0) == 0
        for col, acc in ((0, dk_acc), (128, dv_acc)):
            both = jnp.where(lo2, acc[0] + pltpu.roll(acc[0], 64, 1), acc[1] + pltpu.roll(acc[1], 64, 1))
            both = jnp.where(sink_row, 0.0, both)
            dkv_prev_ref[:, col:col + 128] = both[0:BLK, :]
            dkv_own_ref[:, col:col + 128] = both[BLK:2 * BLK, :]

    tab = pl.BlockSpec((BLK, 128), lambda i: (i % seq_blocks, 0))
    return pl.pallas_call(
        body, name="bwd_attn", grid=(t // BLK,),
        in_specs=[pl.BlockSpec(memory_space=pltpu.SMEM), _row_spec(BLK, D), _row_spec(BLK, 256),
                  pl.BlockSpec((BLK, 256), lambda i: (jnp.maximum(i - 1, 0), 0)),
                  _row_spec(BLK, D), _row_spec(BLK, D), _row_spec(BLK, D, 0), tab, tab],
        out_specs=[_row_spec(BLK, D), _row_spec(BLK, D), _row_spec(BLK, 256), _row_spec(BLK, 256),
                   pl.BlockSpec((8, 128), lambda i: (0, 0))],
        out_shape=[jax.ShapeDtypeStruct((t, D), BF16), jax.ShapeDtypeStruct((t, D), BF16),
                   jax.ShapeDtypeStruct((t, 256), F32), jax.ShapeDtypeStruct((t, 256), F32),
                   jax.ShapeDtypeStruct((8, 128), F32)],
        compiler_params=_params("arbitrary"),
    )(sinks, q, kv, kv, attn, dub, g3, cos_t, sin_t)


def _bwd_kv_finish(dkv_own, dkv_prev, cos_t, sin_t):
    t = dkv_own.shape[0]
    tm = 512
    seq_tiles = SEQ_LEN // tm
    n_blocks = t // BLK

    def body(own_ref, same_ref, nxt_ref, c_ref, s_ref, out_ref):
        keep = jnp.where(pl.program_id(0) % seq_tiles == seq_tiles - 1, 0.0, 1.0)
        shifted = jnp.concatenate([same_ref[BLK:tm, :], nxt_ref[...] * keep], axis=0)
        tot = own_ref[...] + shifted
        dk = tot[:, 0:128]
        out_ref[:, 0:128] = (dk * c_ref[...] - _swap_halves(dk) * s_ref[...]).astype(BF16)
        out_ref[:, 128:256] = tot[:, 128:256].astype(BF16)

    tab = pl.BlockSpec((tm, 128), lambda i: (i % seq_tiles, 0))
    return pl.pallas_call(
        body, name="bwd_kv_finish", grid=(t // tm,),
        in_specs=[_row_spec(tm, 256), _row_spec(tm, 256),
                  pl.BlockSpec((BLK, 256), lambda i: (jnp.minimum((i + 1) * (tm // BLK), n_blocks - 1), 0)), tab, tab],
        out_specs=_row_spec(tm, 256),
        out_shape=jax.ShapeDtypeStruct((t, 256), BF16),
        compiler_params=_params("parallel"),
    )(dkv_own, dkv_prev, dkv_prev, cos_t, sin_t)


STAGE_ROWS = 256
V7X_VMEM_BYTES = 64 << 20


def _bwd_conv(dya, a4, h, wconv8, wpc, tm, parts):
    t = a4.shape[0]
    n_t = t // tm
    sub = tm // parts
    seq_tiles = SEQ_LEN // tm

    def body(dya_ref, xc_ref, bg_ref, cg_ref, zc_ref, xcp_ref, cgp_ref, w_ref, wpc_ref, h_ref,
             da4_ref, dwpc_ref, dwc_ref, o32_ref, o16_ref, acc_ref, stage_ref, later_ref, sems):
        step = pl.program_id(0)
        tile = n_t - 1 - step

        @pl.when(step == 0)
        def _():
            dwpc_ref[...] = jnp.zeros_like(dwpc_ref)
            dwc_ref[...] = jnp.zeros_like(dwc_ref)
            acc_ref[...] = jnp.zeros_like(acc_ref)

        keep_prev = jnp.where(tile % seq_tiles == 0, 0.0, 1.0)
        ends_sequence = tile % seq_tiles == seq_tiles - 1

        def part(p, later):
            r0 = p * sub
            here = slice(r0, r0 + sub)
            if p == 0:
                u_prev = cgp_ref[14:16, :].astype(F32) * xcp_ref[14:16, :].astype(F32) * keep_prev
            else:
                u_prev = cg_ref[r0 - 2:r0, :].astype(F32) * xc_ref[r0 - 2:r0, :].astype(F32)
            xc = xc_ref[here, :].astype(F32)
            bg = bg_ref[here, :].astype(F32)
            cg = cg_ref[here, :].astype(F32)
            zc = zc_ref[here, :].astype(F32)
            u, u_m1, u_m2, yconv, sg, sz, co = _conv_forward(xc, bg, cg, zc, u_prev[0:1, :], u_prev[1:2, :], w_ref)
            ua = (sz * co).astype(BF16)
            dua = _dot_nt(dya_ref[here, :], wpc_ref[...])
            da4_ref[here, 3 * D:4 * D] = (dua * co * (sg * (1.0 + zc * (1.0 - sg)))).astype(BF16)
            dco = dua * sz
            da4_ref[here, D:2 * D] = (dco * yconv).astype(BF16)
            dyc = dco * bg
            dwc = jnp.concatenate([jnp.sum(dyc * s, axis=0, keepdims=True) for s in (u_m2, u_m1, u)], axis=0)
            rows = lax.broadcasted_iota(jnp.int32, xc.shape, 0)
            n0 = later[0:1, :]
            n1 = later[1:2, :]
            dyc_p1 = jnp.where(rows == sub - 1, n0, pltpu.roll(dyc, sub - 1, 0))
            dyc_p2 = jnp.where(rows == sub - 2, n0, jnp.where(rows == sub - 1, n1, pltpu.roll(dyc, sub - 2, 0)))
            du = w_ref[2:3, :] * dyc + w_ref[1:2, :] * dyc_p1 + w_ref[0:1, :] * dyc_p2
            da4_ref[here, 0:D] = (du * cg).astype(BF16)
            da4_ref[here, 2 * D:3 * D] = (du * xc).astype(BF16)
            return ua, dwc, dyc[0:8, :]

        later = jnp.where(ends_sequence, 0.0, later_ref[...])
        uas, dwc = [], jnp.zeros((3, D), F32)
        for p in reversed(range(parts)):
            ua, dwc_p, later = part(p, later)
            uas.insert(0, ua)
            dwc = dwc + dwc_p
        later_ref[...] = later
        dwpc_ref[...] += _dot_tn(jnp.concatenate(uas, axis=0), dya_ref[...])
        dwc_ref[0:3, :] += dwc
        for j in range(4):
            acc_ref[j * D:(j + 1) * D, :] += _dot_tn(da4_ref[:, j * D:(j + 1) * D], h_ref[...])

        @pl.when(step == n_t - 1)
        def _():
            c32 = pltpu.make_async_copy(acc_ref, o32_ref.at[pl.ds(0, 4 * D)], sems.at[0])
            c32.start()
            for j in range(4 * D // STAGE_ROWS):
                rows = pl.ds(j * STAGE_ROWS, STAGE_ROWS)
                stage_ref[...] = acc_ref[rows, :].astype(BF16)
                c16 = pltpu.make_async_copy(stage_ref, o16_ref.at[rows], sems.at[1])
                c16.start()
                c16.wait()
            c32.wait()

    def rows_of_tile(width, col=0):
        return pl.BlockSpec((tm, width), lambda s: (n_t - 1 - s, col))

    def prev(col):
        return pl.BlockSpec((16, D), lambda s: (jnp.maximum((n_t - 1 - s) * (tm // 16) - 1, 0), col))

    hbm = pl.BlockSpec(memory_space=pl.ANY)
    out = pl.pallas_call(
        body, name="bwd_conv", grid=(n_t,),
        in_specs=[rows_of_tile(D), rows_of_tile(D, 0), rows_of_tile(D, 1), rows_of_tile(D, 2), rows_of_tile(D, 3),
                  prev(0), prev(2), pl.BlockSpec((8, D), lambda s: (0, 0)), _whole_vmem(), rows_of_tile(D)],
        out_specs=[rows_of_tile(4 * D), _whole_vmem(), pl.BlockSpec((8, D), lambda s: (0, 0)), hbm, hbm],
        out_shape=[jax.ShapeDtypeStruct((t, 4 * D), BF16), jax.ShapeDtypeStruct((D, D), F32),
                   jax.ShapeDtypeStruct((8, D), F32), jax.ShapeDtypeStruct((D_IN, D), F32),
                   jax.ShapeDtypeStruct((D_IN, D), BF16)],
        scratch_shapes=[pltpu.VMEM((4 * D, D), F32), pltpu.VMEM((STAGE_ROWS, D), BF16), pltpu.VMEM((8, D), F32),
                        pltpu.SemaphoreType.DMA((2,))],
        compiler_params=pltpu.CompilerParams(dimension_semantics=("arbitrary",), vmem_limit_bytes=V7X_VMEM_BYTES - (2 << 20)),
    )(dya, a4, a4, a4, a4, a4, a4, wconv8, wpc, h)
    return out[0], out[1], out[2], (out[3], out[4])


def _bwd_dh(da4, dq, dkv, dza, dgab, wt, x, g_pre, dout, tm):
    t = x.shape[0]

    def body(da4_ref, dq_ref, dkv_ref, dza_ref, dgab_ref, w_ref, x_ref, g_ref, dout_ref, gx_ref, dg_ref):
        @pl.when(pl.program_id(0) == 0)
        def _():
            dg_ref[...] = jnp.zeros_like(dg_ref)

        dh = _dot(da4_ref[...], w_ref[0:ROW_Q, :])
        dh += _dot(dq_ref[...], w_ref[ROW_Q:ROW_KV, :])
        dh += _dot(dkv_ref[...], w_ref[ROW_KV:ROW_ZA, :])
        dh += _dot(dza_ref[...], w_ref[ROW_ZA:ROW_GA, :])
        dh += _dot(dgab_ref[...], w_ref[ROW_GA:D_IN, :])
        xf = x_ref[...]
        r = lax.rsqrt(jnp.mean(xf * xf, axis=-1, keepdims=True) + RMS_EPS)
        xn = xf * r
        dg_ref[0:1, :] += jnp.sum(dh * xn, axis=0, keepdims=True)
        dxn = dh * g_ref[...]
        gx_ref[...] = dout_ref[...] + r * (dxn - xn * jnp.mean(dxn * xn, axis=-1, keepdims=True))

    return pl.pallas_call(
        body, name="bwd_dh", grid=(t // tm,),
        in_specs=[_row_spec(tm, 4 * D), _row_spec(tm, D), _row_spec(tm, 256), _row_spec(tm, D), _row_spec(tm, 2 * D),
                  _whole_vmem(), _row_spec(tm, D), pl.BlockSpec((1, D), lambda i: (0, 0)), _row_spec(tm, D)],
        out_specs=[_row_spec(tm, D), pl.BlockSpec((8, D), lambda i: (0, 0))],
        out_shape=[jax.ShapeDtypeStruct((t, D), F32), jax.ShapeDtypeStruct((8, D), F32)],
        compiler_params=_params("arbitrary"),
    )(da4, dq, dkv, dza, dgab, wt, x, g_pre, dout)


def _bwd_dw_in(h, piece, row0, nb, tm, name, prev):
    t, n = piece.shape
    n_t = t // tm

    def body(*refs):
        h_ref, p_ref = refs[0], refs[1]
        o32_ref, o16_ref, acc_ref, acc16_ref, sems = refs[-5:]
        j, i = pl.program_id(0), pl.program_id(1)

        @pl.when(i == 0)
        def _():
            acc_ref[...] = jnp.zeros_like(acc_ref)

        acc_ref[...] += _dot_tn(p_ref[...], h_ref[...])

        @pl.when(i == n_t - 1)
        def _():
            acc16_ref[...] = acc_ref[...].astype(BF16)
            rows = pl.ds(pl.multiple_of(row0 + j * nb, 16), nb)
            c32 = pltpu.make_async_copy(acc_ref, o32_ref.at[rows], sems.at[0])
            c16 = pltpu.make_async_copy(acc16_ref, o16_ref.at[rows], sems.at[1])
            c32.start()
            c16.start()
            c32.wait()
            c16.wait()

    hbm = pl.BlockSpec(memory_space=pl.ANY)
    carried = [] if prev is None else list(prev)
    return pl.pallas_call(
        body, name=name, grid=(n // nb, n_t),
        in_specs=[pl.BlockSpec((tm, D), lambda j, i: (i, 0)), pl.BlockSpec((tm, nb), lambda j, i: (i, j))]
        + [hbm] * len(carried),
        out_specs=[hbm, hbm],
        out_shape=[jax.ShapeDtypeStruct((D_IN, D), F32), jax.ShapeDtypeStruct((D_IN, D), BF16)],
        scratch_shapes=[pltpu.VMEM((nb, D), F32), pltpu.VMEM((nb, D), BF16), pltpu.SemaphoreType.DMA((2,))],
        input_output_aliases={2: 0, 3: 1} if carried else {},
        compiler_params=_params("arbitrary", "arbitrary"),
    )(h, piece, *carried)


def _place():
    x, y, c = lax.axis_index("x"), lax.axis_index("y"), lax.axis_index("c")
    return x, y, c, 4 * x + 2 * y + c


def _peer(x, y, c, k):
    return (1 - x if k & 4 else x, 1 - y if k & 2 else y, 1 - c if k & 1 else c)


ICI_MASKS = (4, 2, 6)


def _all_gather(shards):
    n = len(shards)

    def body(*refs):
        src, dst = refs[:n], refs[n:2 * n]
        send_sems, recv_sems, local_sems = refs[2 * n:]
        x, y, c, me = _place()
        sibling = _peer(x, y, c, 1)

        def copy(a, s, block, to, own=False):
            return pltpu.make_async_remote_copy(
                src_ref=src[a] if own else dst[a].at[block], dst_ref=dst[a].at[block],
                send_sem=send_sems.at[a * 7 + s], recv_sem=recv_sems.at[a * 7 + s], device_id=to, device_id_type=MESH_ID)

        local = [pltpu.make_async_copy(src[a], dst[a].at[me], local_sems.at[a]) for a in range(n)]
        for cp in local:
            cp.start()
        started = [copy(a, 0, me, sibling, own=True) for a in range(n)]
        started += [copy(a, 1 + j, me, _peer(x, y, c, k), own=True) for j, k in enumerate(ICI_MASKS) for a in range(n)]
        for cp in started:
            cp.start()
        for j, k in enumerate(ICI_MASKS):
            for a in range(n):
                copy(a, 1 + j, me ^ k, sibling).wait_recv()
                fwd = copy(a, 4 + j, me ^ k, sibling)
                fwd.start()
                started.append(fwd)
        for a in range(n):
            copy(a, 0, me ^ 1, sibling).wait_recv()
        for j, k in enumerate(ICI_MASKS):
            for a in range(n):
                copy(a, 4 + j, me ^ 1 ^ k, sibling).wait_recv()
        for cp in started:
            cp.wait_send()
        for cp in local:
            cp.wait()

    hbm = pl.BlockSpec(memory_space=pl.ANY)
    return pl.pallas_call(
        body, name="all_gather_weights",
        in_specs=[hbm] * n, out_specs=[hbm] * n,
        out_shape=[jax.ShapeDtypeStruct((N_DEV,) + s.shape, s.dtype) for s in shards],
        scratch_shapes=[pltpu.SemaphoreType.DMA((7 * n,)), pltpu.SemaphoreType.DMA((7 * n,)),
                        pltpu.SemaphoreType.DMA((n,))],
    )(*shards)


def _direct_copies(src, land, send_sems, recv_sems):
    x, y, c, me = _place()
    return [pltpu.make_async_remote_copy(
        src_ref=src[a], dst_ref=land[a].at[me], send_sem=send_sems.at[a * 7 + k - 1],
        recv_sem=recv_sems.at[a * 7 + k - 1], device_id=_peer(x, y, c, k), device_id_type=MESH_ID)
        for k in range(1, N_DEV) for a in range(len(src))]


def _gather_start(shards, name):
    n = len(shards)

    def body(*refs):
        src, land = refs[:n], refs[n:2 * n]
        send_sems, recv_sems = refs[2 * n], refs[2 * n + 1]
        token_ref = refs[-1]
        for cp in _direct_copies(src, land, send_sems, recv_sems):
            cp.start()
        token_ref[...] = jnp.zeros_like(token_ref)

    hbm = pl.BlockSpec(memory_space=pltpu.HBM)
    sem = pl.BlockSpec(memory_space=pltpu.SEMAPHORE)
    lands = [lax.empty((N_DEV,) + s.shape, s.dtype) for s in shards]
    out = pl.pallas_call(
        body, name=name + "_start",
        out_shape=(pltpu.SemaphoreType.DMA((7 * n,)), pltpu.SemaphoreType.DMA((7 * n,)),
                   *[pltpu.HBM(s.shape, s.dtype) for s in shards], *[pltpu.HBM(s.shape, s.dtype) for s in lands],
                   jax.ShapeDtypeStruct((8, 128), F32)),
        in_specs=[hbm] * (2 * n), out_specs=(sem, sem, *[hbm] * (2 * n), _whole_vmem()),
        input_output_aliases={a: 2 + a for a in range(2 * n)},
        compiler_params=pltpu.CompilerParams(has_side_effects=pltpu.SideEffectType.DATAFLOW_SIDE_EFFECTING),
    )(*[pltpu.with_memory_space_constraint(s, pltpu.HBM) for s in list(shards) + lands])
    return out[0], out[1], out[2:2 + n], out[2 + n:2 + 2 * n], out[-1]


def _gather_wait(send_sems, recv_sems, flying, lands, after, name):
    n = len(flying)

    def body(*refs):
        src, land = refs[:n], refs[n:2 * n]
        for cp in _direct_copies(src, land, refs[2 * n], refs[2 * n + 1]):
            cp.wait_send()
            cp.wait_recv()

    hbm = pl.BlockSpec(memory_space=pltpu.HBM)
    sem = pl.BlockSpec(memory_space=pltpu.SEMAPHORE)
    out = pl.pallas_call(
        body, name=name + "_wait",
        out_shape=tuple(pltpu.HBM(s.shape, s.dtype) for s in list(flying) + list(lands)),
        in_specs=[hbm] * (2 * n) + [sem, sem, pl.BlockSpec(memory_space=pl.ANY)], out_specs=tuple([hbm] * (2 * n)),
        input_output_aliases={a: a for a in range(2 * n)},
        compiler_params=pltpu.CompilerParams(has_side_effects=pltpu.SideEffectType.DATAFLOW_SIDE_EFFECTING),
    )(*flying, *lands, send_sems, recv_sems, after)
    return out[n:]


def _exchange_sibling(by_dest):
    n = len(by_dest)

    def body(*refs):
        src, dst = refs[:n], refs[n:2 * n]
        send_sems, recv_sems = refs[2 * n:]
        x, y, c, _ = _place()
        sibling = _peer(x, y, c, 1)
        copies = [pltpu.make_async_remote_copy(
            src_ref=src[a].at[2 * p + (1 - c)], dst_ref=dst[a].at[p], send_sem=send_sems.at[a * 4 + p],
            recv_sem=recv_sems.at[a * 4 + p], device_id=sibling, device_id_type=MESH_ID)
            for a in range(n) for p in range(4)]
        for cp in copies:
            cp.start()
        for cp in copies:
            cp.wait_recv()
        for cp in copies:
            cp.wait_send()

    hbm = pl.BlockSpec(memory_space=pl.ANY)
    return pl.pallas_call(
        body, name="exchange_sibling", in_specs=[hbm] * n, out_specs=[hbm] * n,
        out_shape=[jax.ShapeDtypeStruct((4,) + s.shape[1:], s.dtype) for s in by_dest],
        scratch_shapes=[pltpu.SemaphoreType.DMA((4 * n,)), pltpu.SemaphoreType.DMA((4 * n,))],
    )(*by_dest)


def _chip_copies(src, land, send_sems, recv_sems):
    x, y, c, _ = _place()
    chip = 2 * x + y
    return [pltpu.make_async_remote_copy(
        src_ref=src[a].at[chip ^ (k >> 1)], dst_ref=land[a].at[j], send_sem=send_sems.at[a * 3 + j],
        recv_sem=recv_sems.at[a * 3 + j], device_id=_peer(x, y, c, k), device_id_type=MESH_ID)
        for j, k in enumerate(ICI_MASKS) for a in range(len(src))]


def _exchange_chips_start(by_chip):
    n = len(by_chip)

    def body(*refs):
        src, land = refs[:n], refs[n:2 * n]
        send_sems, recv_sems = refs[2 * n], refs[2 * n + 1]
        token_ref = refs[-1]
        for cp in _chip_copies(src, land, send_sems, recv_sems):
            cp.start()
        token_ref[...] = jnp.zeros_like(token_ref)

    hbm = pl.BlockSpec(memory_space=pltpu.HBM)
    sem = pl.BlockSpec(memory_space=pltpu.SEMAPHORE)
    lands = [lax.empty((3,) + s.shape[1:], s.dtype) for s in by_chip]
    out = pl.pallas_call(
        body, name="exchange_chips_start",
        out_shape=(pltpu.SemaphoreType.DMA((3 * n,)), pltpu.SemaphoreType.DMA((3 * n,)),
                   *[pltpu.HBM(s.shape, s.dtype) for s in by_chip], *[pltpu.HBM(s.shape, s.dtype) for s in lands],
                   jax.ShapeDtypeStruct((8, 128), F32)),
        in_specs=[hbm] * (2 * n), out_specs=(sem, sem, *[hbm] * (2 * n), _whole_vmem()),
        input_output_aliases={a: 2 + a for a in range(2 * n)},
        compiler_params=pltpu.CompilerParams(has_side_effects=pltpu.SideEffectType.DATAFLOW_SIDE_EFFECTING),
    )(*[pltpu.with_memory_space_constraint(s, pltpu.HBM) for s in list(by_chip) + lands])
    return out[0], out[1], out[2:2 + n], out[2 + n:2 + 2 * n], out[-1]


def _exchange_chips_wait(send_sems, recv_sems, flying, lands, after):
    n = len(flying)

    def body(*refs):
        src, land = refs[:n], refs[n:2 * n]
        send_sems_ref, recv_sems_ref = refs[2 * n], refs[2 * n + 1]
        for cp in _chip_copies(src, land, send_sems_ref, recv_sems_ref):
            cp.wait_send()
            cp.wait_recv()

    hbm = pl.BlockSpec(memory_space=pltpu.HBM)
    sem = pl.BlockSpec(memory_space=pltpu.SEMAPHORE)
    out = pl.pallas_call(
        body, name="exchange_chips_wait",
        out_shape=tuple(pltpu.HBM(s.shape, s.dtype) for s in list(flying) + list(lands)),
        in_specs=[hbm] * (2 * n) + [sem, sem, pl.BlockSpec(memory_space=pl.ANY)], out_specs=tuple([hbm] * (2 * n)),
        input_output_aliases={a: a for a in range(2 * n)},
        compiler_params=pltpu.CompilerParams(has_side_effects=pltpu.SideEffectType.DATAFLOW_SIDE_EFFECTING),
    )(*flying, *lands, send_sems, recv_sems, after)
    return out[n:]


def _adamw_math(w, g, m, v):
    m = ADAM_B1 * m + (1.0 - ADAM_B1) * g
    v = ADAM_B2 * v + (1.0 - ADAM_B2) * (g * g)
    m_hat = m / (1.0 - ADAM_B1 ** ADAM_STEP)
    v_hat = v / (1.0 - ADAM_B2 ** ADAM_STEP)
    return -ADAM_LR * (m_hat / (jnp.sqrt(v_hat) + ADAM_EPS) + ADAM_WD * w), m, v


def _pair_sum(owns, recvs, c_arr, tr, name):
    n = len(owns)
    _, rows, cols = owns[0].shape

    def body(c_ref, *refs):
        for a in range(n):
            s = refs[a][...] + refs[n + a][...].astype(F32)
            refs[2 * n + a][...] = s
            refs[3 * n + a][...] = s.astype(BF16)

    by_chip = pl.BlockSpec((None, tr, cols), lambda p, i, c_ref: (p, i, 0))
    mine = pl.BlockSpec((None, tr, cols), lambda p, i, c_ref: (2 * p + c_ref[0], i, 0))
    out = pl.pallas_call(
        body, name=name,
        grid_spec=pltpu.PrefetchScalarGridSpec(
            num_scalar_prefetch=1, grid=(4, rows // tr), in_specs=[mine] * n + [by_chip] * n, out_specs=[by_chip] * (2 * n)),
        out_shape=[jax.ShapeDtypeStruct((4, rows, cols), F32)] * n + [jax.ShapeDtypeStruct((4, rows, cols), BF16)] * n,
        compiler_params=_params("parallel", "parallel"),
    )(c_arr, *owns, *recvs)
    return out[:n], out[n:]


def _chip_sum(pairs, recvs, chip_arr, tr, name, adam=None):
    n = len(pairs)
    _, rows, cols = pairs[0].shape
    n_state = 0 if adam is None else 3 * n

    def body(chip_ref, *refs):
        outs = refs[2 * n + n_state:]
        for a in range(n):
            g = refs[a][...]
            for j in range(3):
                g = g + refs[n + a][j].astype(F32)
            outs[a][...] = g
            if adam is not None:
                w_ref, m_ref, v_ref = (refs[2 * n + s * n + a] for s in range(3))
                outs[n + a][...], outs[2 * n + a][...], outs[3 * n + a][...] = _adamw_math(w_ref[...], g, m_ref[...], v_ref[...])

    blk = pl.BlockSpec((tr, cols), lambda i, chip_ref: (i, 0))
    n_out = n if adam is None else 4 * n
    out = pl.pallas_call(
        body, name=name,
        grid_spec=pltpu.PrefetchScalarGridSpec(
            num_scalar_prefetch=1, grid=(rows // tr,),
            in_specs=[pl.BlockSpec((None, tr, cols), lambda i, chip_ref: (chip_ref[0], i, 0))] * n
            + [pl.BlockSpec((3, tr, cols), lambda i, chip_ref: (0, i, 0))] * n + [blk] * n_state,
            out_specs=[blk] * n_out),
        out_shape=[jax.ShapeDtypeStruct((rows, cols), F32)] * n_out,
        compiler_params=_params("parallel"),
    )(chip_arr, *pairs, *recvs, *([] if adam is None else [t for group in adam for t in group]))
    return out if adam is None else (out[:n], out[n:2 * n], out[2 * n:3 * n], out[3 * n:])


def _adamw(ws, gs, ms, vs, name):
    n = len(ws)

    def body(*refs):
        for a in range(n):
            w_ref, g_ref, m_ref, v_ref = (refs[s * n + a] for s in range(4))
            refs[4 * n + a][...], refs[5 * n + a][...], refs[6 * n + a][...] = _adamw_math(
                w_ref[...], g_ref[...], m_ref[...], v_ref[...])

    out = pl.pallas_call(body, name=name, out_shape=[jax.ShapeDtypeStruct(w.shape, F32) for w in ws] * 3)(
        *ws, *gs, *ms, *vs)
    return out[:n], out[n:2 * n], out[2 * n:]


def _sum_small(small_all):
    def body(s_ref, o_ref):
        g = s_ref[0]
        for d in range(1, N_DEV):
            g = g + s_ref[d]
        o_ref[...] = g

    return pl.pallas_call(body, name="sum_small", out_shape=jax.ShapeDtypeStruct(small_all.shape[1:], F32))(small_all)


def _rope_tables():
    inv_freq = ROPE_THETA ** (-jnp.arange(0, HEAD_DIM, 2, dtype=F32) / HEAD_DIM)
    ang = jnp.arange(SEQ_LEN).astype(F32)[:, None] * inv_freq[None, :]
    cos, sin = jnp.cos(ang), jnp.sin(ang)
    return jnp.tile(cos, (1, 4)), jnp.tile(jnp.concatenate([-sin, sin], axis=1), (1, 2))


def _local_step(x, target, g_pre, g_post, sinks, wt, wconv, squares, start_exchange=None):
    cos_t, sin_t = _rope_tables()
    wconv8 = jnp.pad(wconv, ((0, 5), (0, 0)))
    h, q, kv, g3 = _fwd_in_attn(x, g_pre, wt, cos_t, sin_t, 512)
    wpc, wpa, wout = squares(kv)
    a4, ya = _fwd_in_conv(h, wt, wconv8, wpc, 512)
    attn, ub = _fwd_attn(sinks, q, kv, g3)
    loss8, dout, dya, dub, dgab, dwout, dwpa, dgpost8 = _fwd_out_bwd_head(ya, ub, g3, x, target, g_post, wpa, wout, 256)
    dq, dza, dkv_own, dkv_prev, dsink8 = _bwd_attn(sinks, q, kv, attn, dub, g3, cos_t, sin_t)
    dkv = _bwd_kv_finish(dkv_own, dkv_prev, cos_t, sin_t)
    da4, dwpc, dwconv8, dwt = _bwd_conv(dya, a4, h, wconv8, wpc, 512, 2)
    dwt = _bwd_dw_in(h, dq, ROW_Q, 1024, 1024, "bwd_dw_in_q", dwt)
    dwt = _bwd_dw_in(h, dkv, ROW_KV, 256, 1024, "bwd_dw_in_kv", dwt)
    dwt = _bwd_dw_in(h, dza, ROW_ZA, 1024, 1024, "bwd_dw_in_za", dwt)
    dwt32, dwt16 = _bwd_dw_in(h, dgab, ROW_GA, 1024, 1024, "bwd_dw_in_gates", dwt)
    token, pending = (None, None) if start_exchange is None else start_exchange(dwt32, dwt16, dwpc, dwpa, dwout)
    g_pre_after = g_pre if token is None else g_pre + token[0:1, 0:1]
    grad_x, dgpre8 = _bwd_dh(da4, dq, dkv, dza, dgab, wt, x, g_pre_after, dout, 256)
    small = jnp.concatenate([dgpre8, dgpost8, jnp.pad(dsink8, ((0, 0), (0, D - 128))), dwconv8,
                             jnp.pad(loss8, ((0, 0), (0, D - 128)))], axis=0)
    return loss8[0, 0], grad_x, dwt32, dwt16, dwpc, dwpa, dwout, small, pending


def kernel(x, g_pre, g_post, w_in, w_conv, sinks, w_proj_conv, w_proj_attn, w_out, loss_target, m_g_pre, m_g_post, m_w_in, m_w_conv, m_sinks, m_w_proj_conv, m_w_proj_attn, m_w_out, v_g_pre, v_g_post, v_w_in, v_w_conv, v_sinks, v_w_proj_conv, v_w_proj_attn, v_w_out):
    batch = x.shape[0]
    mx, my, mc, me = _place()
    c_arr = jnp.reshape(mc, (1,)).astype(jnp.int32)
    chip_arr = jnp.reshape(2 * mx + my, (1,)).astype(jnp.int32)

    g_wt, g_conv = _all_gather([w_in[0].T.astype(BF16), jnp.pad(w_conv[0], ((0, 5), (0, 0)))])
    wt = g_wt.reshape(D_IN, D)
    wconv = g_conv[:, 0:3, :].transpose(1, 0, 2).reshape(3, D)
    sq_mine = [w.astype(BF16) for w in (w_proj_conv[0], w_proj_attn[0], w_out[0])]
    wt, sq_mine = lax.optimization_barrier((wt, sq_mine))
    sq_send, sq_recv, sq_flying, sq_lands, sq_token = _gather_start(sq_mine, "gather_squares")

    def squares(after):
        got = _gather_wait(sq_send, sq_recv, sq_flying, sq_lands, after, "gather_squares")
        return [lax.dynamic_update_index_in_dim(full, mine, me, 0).reshape(D, D) for full, mine in zip(got, sq_mine)]

    def start_exchange(dwt32, dwt16, dwpc, dwpa, dwout):
        own_sq = [g.reshape(N_DEV, SHARD_SQ, D) for g in (dwpc, dwpa, dwout)]
        own_in = dwt32.reshape(N_DEV, SHARD_IN, D)
        from_sibling = _exchange_sibling([dwt16.reshape(N_DEV, SHARD_IN, D)] + [g.astype(BF16) for g in own_sq])
        in32, in16 = _pair_sum([own_in], from_sibling[:1], c_arr, SHARD_IN // 2, "pair_sum_w_in")
        sq32, sq16 = _pair_sum(own_sq, from_sibling[1:], c_arr, SHARD_SQ, "pair_sum_squares")
        send_sems, recv_sems, flying, lands, token = _exchange_chips_start(list(in16) + list(sq16))
        return token, (send_sems, recv_sems, flying, lands, in32, sq32)

    _, grad_x, _, _, _, _, _, small, pending = _local_step(
        x.reshape(batch * SEQ_LEN, D), loss_target.reshape(batch * SEQ_LEN, D), g_pre + sq_token[0:1, 0:1], g_post,
        sinks, wt, wconv, squares, start_exchange)
    sm_send, sm_recv, sm_flying, sm_lands, _ = _gather_start([small], "gather_small")
    send_sems, recv_sems, flying, lands, in32, sq32 = pending
    from_chips = _exchange_chips_wait(send_sems, recv_sems, flying, lands, small)

    o_in = [o[0].T for o in _chip_sum(
        in32, from_chips[:1], chip_arr, SHARD_IN // 3, "chip_sum_adamw_w_in",
        adam=([w_in[0].T], [m_w_in[0].T], [v_w_in[0].T]))]
    g_in_mine, o_in = o_in[0], o_in[1:]
    g_sq, d_sq, m_sq, v_sq = _chip_sum(
        sq32, from_chips[1:], chip_arr, SHARD_SQ, "chip_sum_adamw_squares",
        adam=([w_proj_conv[0], w_proj_attn[0], w_out[0]], [m_w_proj_conv[0], m_w_proj_attn[0], m_w_out[0]],
              [v_w_proj_conv[0], v_w_proj_attn[0], v_w_out[0]]))
    (small_all,) = _gather_wait(sm_send, sm_recv, sm_flying, sm_lands, d_sq[0], "gather_small")
    gs = _sum_small(lax.dynamic_update_index_in_dim(small_all, small, me, 0))
    g_g_pre, g_g_post, g_sinks, loss = gs[0:1], gs[8:9], gs[16:17, 0:N_HEADS], gs[32, 0]
    g_conv_mine = lax.dynamic_slice_in_dim(gs[24:27], me * SHARD_SQ, SHARD_SQ, axis=1)
    o_small = _adamw([g_pre, g_post, sinks, w_conv[0]], [g_g_pre, g_g_post, g_sinks, g_conv_mine],
                     [m_g_pre, m_g_post, m_sinks, m_w_conv[0]], [v_g_pre, v_g_post, v_sinks, v_w_conv[0]], "adamw_small")

    grads = [g_g_pre, g_g_post, g_in_mine[None], g_conv_mine[None], g_sinks] + [g[None] for g in g_sq]
    rest = []
    for idx, sq in enumerate((d_sq, m_sq, v_sq)):
        gp, gq, sk, cv = o_small[idx]
        rest += [gp, gq, o_in[idx][None], cv[None], sk] + [s[None] for s in sq]
    return (loss, grad_x.reshape(batch, SEQ_LEN, D), *grads, *rest)
```

```python
import functools

import jax
import jax.numpy as jnp
from jax import lax
from jax.experimental import pallas as pl
from jax.experimental.pallas import tpu as pltpu

D = 1024
N_HEADS = 16
HEAD_DIM = 64
LOGIT_SCALE = HEAD_DIM ** -0.5
BLK = 128
SEQ_LEN = 2048
D_IN = 8448
ROW_Q, ROW_KV, ROW_ZA, ROW_GA = 4 * D, 5 * D, 5 * D + 256, 6 * D + 256
SHARD_IN = D_IN // 8
SHARD_SQ = D // 8
N_DEV = 8
V7X_VMEM_BYTES = 64 << 20
ROPE_THETA = 10000.0
RMS_EPS = 1e-6
NEG = -1e30
ADAM_LR, ADAM_B1, ADAM_B2, ADAM_EPS, ADAM_WD, ADAM_STEP = 0.001, 0.9, 0.999, 1e-08, 0.01, 10

F32 = jnp.float32
BF16 = jnp.bfloat16
MESH_ID = pl.DeviceIdType.MESH


def _dot(a, b):
    return jnp.dot(a, b, preferred_element_type=F32)


def _dot_nt(a, b):
    return lax.dot_general(a, b, (((1,), (1,)), ((), ())), preferred_element_type=F32)


def _dot_tn(a, b):
    return lax.dot_general(a, b, (((0,), (0,)), ((), ())), preferred_element_type=F32)


def _sig(z):
    return 1.0 / (1.0 + jnp.exp(-z))


def _swap_halves(z):
    lane = lax.broadcasted_iota(jnp.int32, z.shape, 1)
    return jnp.where((lane & 63) < 32, pltpu.roll(z, 96, 1), pltpu.roll(z, 32, 1))


def _row_spec(tm, width, col=0):
    return pl.BlockSpec((tm, width), lambda i: (i, col))


def _whole_vmem():
    return pl.BlockSpec(memory_space=pltpu.VMEM)


def _params(*sem, vmem_limit_bytes=None):
    return pltpu.CompilerParams(dimension_semantics=sem, vmem_limit_bytes=vmem_limit_bytes)


def _fwd_in_attn(x, g_pre, wt, cos_t, sin_t, tm):
    t = x.shape[0]
    seq_tiles = SEQ_LEN // tm

    def body(x_ref, g_ref, w_ref, c_ref, s_ref, h_ref, q_ref, kv_ref, g3_ref):
        xf = x_ref[...]
        r = lax.rsqrt(jnp.mean(xf * xf, axis=-1, keepdims=True) + RMS_EPS)
        hh = ((xf * r) * g_ref[...]).astype(BF16)
        h_ref[...] = hh
        c = c_ref[...]
        s = s_ref[...]

        def rope(z):
            return z * c + _swap_halves(z) * s

        q = _dot_nt(hh, w_ref[ROW_Q:ROW_Q + D, :])
        for j in range(D // 128):
            q_ref[:, j * 128:(j + 1) * 128] = (rope(q[:, j * 128:(j + 1) * 128]) * LOGIT_SCALE).astype(BF16)
        kv = _dot_nt(hh, w_ref[ROW_KV:ROW_KV + 256, :])
        kv_ref[:, 0:128] = rope(kv[:, 0:128]).astype(BF16)
        kv_ref[:, 128:256] = kv[:, 128:256].astype(BF16)
        for j in range(3):
            g3_ref[:, j * D:(j + 1) * D] = _dot_nt(hh, w_ref[ROW_ZA + j * D:ROW_ZA + (j + 1) * D, :])

    tab = pl.BlockSpec((tm, 128), lambda i: (i % seq_tiles, 0))
    return pl.pallas_call(
        body, name="fwd_in_attn", grid=(t // tm,),
        in_specs=[_row_spec(tm, D), pl.BlockSpec((1, D), lambda i: (0, 0)), _whole_vmem(), tab, tab],
        out_specs=[_row_spec(tm, D), _row_spec(tm, D), _row_spec(tm, 256), _row_spec(tm, 3 * D)],
        out_shape=[jax.ShapeDtypeStruct((t, D), BF16), jax.ShapeDtypeStruct((t, D), BF16),
                   jax.ShapeDtypeStruct((t, 256), BF16), jax.ShapeDtypeStruct((t, 3 * D), F32)],
        compiler_params=_params("parallel"),
    )(x, g_pre, wt, cos_t, sin_t)


def _conv_forward(xc, bg, cg, zc, up6, up7, w_ref):
    tm = xc.shape[0]
    rows = lax.broadcasted_iota(jnp.int32, xc.shape, 0)
    u = cg * xc
    u_m1 = jnp.where(rows == 0, up7, pltpu.roll(u, 1, 0))
    u_m2 = jnp.where(rows == 0, up6, jnp.where(rows == 1, up7, pltpu.roll(u, 2, 0)))
    yconv = w_ref[0:1, :] * u_m2 + w_ref[1:2, :] * u_m1 + w_ref[2:3, :] * u
    sg = _sig(zc)
    sz = zc * sg
    co = bg * yconv
    del tm
    return u, u_m1, u_m2, yconv, sg, sz, co


def _fwd_in_conv(h, wt, wconv8, wpc, tm):
    t = h.shape[0]
    seq_tiles = SEQ_LEN // tm

    def body(h_ref, w_ref, wc_ref, wpc_ref, a4_ref, ya_ref, last_u_ref):
        hh = h_ref[...]
        xc, bg, cg, zc = (_dot_nt(hh, w_ref[j * D:(j + 1) * D, :]) for j in range(4))
        for j, z in enumerate((xc, bg, cg, zc)):
            a4_ref[:, j * D:(j + 1) * D] = z.astype(BF16)
        first = pl.program_id(0) % seq_tiles == 0
        up6 = jnp.where(first, 0.0, last_u_ref[6:7, :])
        up7 = jnp.where(first, 0.0, last_u_ref[7:8, :])
        u, _, _, _, _, sz, co = _conv_forward(xc, bg, cg, zc, up6, up7, wc_ref)
        last_u_ref[...] = u[tm - 8:tm, :]
        ya_ref[...] = _dot((sz * co).astype(BF16), wpc_ref[...])

    return pl.pallas_call(
        body, name="fwd_in_conv", grid=(t // tm,),
        in_specs=[_row_spec(tm, D), _whole_vmem(), pl.BlockSpec((8, D), lambda i: (0, 0)), _whole_vmem()],
        out_specs=[_row_spec(tm, 4 * D), _row_spec(tm, D)],
        out_shape=[jax.ShapeDtypeStruct((t, 4 * D), BF16), jax.ShapeDtypeStruct((t, D), F32)],
        scratch_shapes=[pltpu.VMEM((8, D), F32)],
        compiler_params=_params("arbitrary"),
    )(h, wt, wconv8, wpc)


STACK = 4 * BLK


def _band_mask(first):
    qi = lax.broadcasted_iota(jnp.int32, (STACK, 2 * BLK), 0) & (BLK - 1)
    kj = lax.broadcasted_iota(jnp.int32, (STACK, 2 * BLK), 1)
    return (kj > qi) & (kj <= qi + BLK) & (kj >= jnp.where(first, BLK, 0))


def _masked_fill(sink_ref, g, e):
    kj = lax.broadcasted_iota(jnp.int32, (STACK, 2 * BLK), 1)
    sink = jnp.concatenate([jnp.full((BLK, 2 * BLK), sink_ref[0, 2 * (4 * g + jj) + e], F32) for jj in range(4)], axis=0)
    return jnp.where(kj == 0, sink, NEG)


def _padded_pair(kvp_ref, kvc_ref, col, other=0.0):
    z = jnp.concatenate([kvp_ref[:, col:col + 128], kvc_ref[:, col:col + 128]], axis=0).astype(F32)
    z = jnp.where(lax.broadcasted_iota(jnp.int32, z.shape, 0) == 0, 0.0, z)
    zs = pltpu.roll(z, 64, 1)
    lo = lax.broadcasted_iota(jnp.int32, z.shape, 1) < 64
    fill = jnp.full_like(z, other)
    left = [jnp.where(lo, z, fill).astype(BF16), jnp.where(lo, zs, fill).astype(BF16)]
    right = [jnp.where(lo, fill, zs).astype(BF16), jnp.where(lo, fill, z).astype(BF16)]
    return left, right


def _exp_logits(s, valid, fill):
    s = jnp.where(valid, s, fill)
    return jnp.exp(s - jnp.max(s, axis=-1, keepdims=True))


def _fwd_attn(sinks, q, kv, g3):
    t = q.shape[0]
    seq_blocks = SEQ_LEN // BLK

    def body(sink_ref, q_ref, kvc_ref, kvp_ref, za_ref, attn_ref, ub_ref):
        i = pl.program_id(0)
        valid = _band_mask(i % seq_blocks == 0)
        k_pad = _padded_pair(kvp_ref, kvc_ref, 0)
        v_one = _padded_pair(kvp_ref, kvc_ref, 128, other=1.0)
        lo = lax.broadcasted_iota(jnp.int32, (STACK, 128), 1) < 64
        for g in range(2):
            qg = jnp.concatenate([q_ref[:, j * 128:(j + 1) * 128] for j in range(4 * g, 4 * g + 4)], axis=0)
            pv = [_dot(_exp_logits(_dot_nt(qg, k_pad[e][g]), valid, _masked_fill(sink_ref, g, e)).astype(BF16),
                       v_one[e][g]) for e in range(2)]
            o = jnp.where(lo, pv[0], pv[1]) / pltpu.roll(jnp.where(lo, pv[1], pv[0]), 64, 1)
            for jj in range(4):
                cols = slice((4 * g + jj) * 128, (4 * g + jj + 1) * 128)
                oj = o[jj * BLK:(jj + 1) * BLK, :]
                attn_ref[:, cols] = oj
                za = za_ref[:, cols]
                ub_ref[:, cols] = (za * _sig(za) * oj).astype(BF16)

    return pl.pallas_call(
        body, name="fwd_attn", grid=(t // BLK,),
        in_specs=[pl.BlockSpec(memory_space=pltpu.SMEM), _row_spec(BLK, D), _row_spec(BLK, 256),
                  pl.BlockSpec((BLK, 256), lambda i: (jnp.maximum(i - 1, 0), 0)), _row_spec(BLK, D, 0)],
        out_specs=[_row_spec(BLK, D), _row_spec(BLK, D)],
        out_shape=[jax.ShapeDtypeStruct((t, D), F32), jax.ShapeDtypeStruct((t, D), BF16)],
        compiler_params=_params("parallel"),
    )(sinks, q, kv, kv, g3)


def _fwd_out_bwd_head(ya, ub, g3, x, target, g_post, wpa, wout, tm, parts=1):
    t = x.shape[0]

    def body(ya_ref, ub_ref, ga_ref, gb_ref, x_ref, tgt_ref, gp_ref, wpa_ref, wout_ref,
             loss_ref, dout_ref, dya_ref, dub_ref, dgab_ref, dwout_ref, dwpa_ref, dgp_ref):
        @pl.when(pl.program_id(0) == 0)
        def _():
            loss_ref[...] = jnp.zeros_like(loss_ref)
            dwout_ref[...] = jnp.zeros_like(dwout_ref)
            dwpa_ref[...] = jnp.zeros_like(dwpa_ref)
            dgp_ref[...] = jnp.zeros_like(dgp_ref)

        g = gp_ref[...]
        sq = jnp.zeros((1, 1), F32)
        dgp = jnp.zeros((1, D), F32)
        mbs, dys, dybs = [], [], []
        for part in range(parts):
            rows = slice(part * (tm // parts), (part + 1) * (tm // parts))
            ub = ub_ref[rows, :]
            ya = ya_ref[rows, :]
            yb = _dot(ub, wpa_ref[...])
            sa = _sig(ga_ref[rows, :])
            sb = _sig(gb_ref[rows, :])
            mb = (sa * ya + sb * yb).astype(BF16)
            y = _dot(mb, wout_ref[...])
            r = lax.rsqrt(jnp.mean(y * y, axis=-1, keepdims=True) + RMS_EPS)
            n = y * r
            err = (x_ref[rows, :] + n * g) - tgt_ref[rows, :]
            sq = sq + jnp.sum(jnp.sum(err * err, axis=0, keepdims=True), axis=1, keepdims=True)
            dout = err * (1.0 / D)
            dout_ref[rows, :] = dout
            dgp = dgp + jnp.sum(dout * n, axis=0, keepdims=True)
            dn = dout * g
            dy = (r * (dn - n * jnp.mean(dn * n, axis=-1, keepdims=True))).astype(BF16)
            dm = _dot_nt(dy, wout_ref[...])
            dya_ref[rows, :] = (dm * sa).astype(BF16)
            dyb = (dm * sb).astype(BF16)
            dgab_ref[rows, 0:D] = (dm * ya * (sa * (1.0 - sa))).astype(BF16)
            dgab_ref[rows, D:2 * D] = (dm * yb * (sb * (1.0 - sb))).astype(BF16)
            dub_ref[rows, :] = _dot_nt(dyb, wpa_ref[...])
            mbs.append(mb)
            dys.append(dy)
            dybs.append(dyb)
        loss_ref[...] += sq * (0.5 / D)
        dgp_ref[0:1, :] += dgp
        dwout_ref[...] += _dot_tn(jnp.concatenate(mbs, axis=0), jnp.concatenate(dys, axis=0))
        dwpa_ref[...] += _dot_tn(ub_ref[...], jnp.concatenate(dybs, axis=0))

    return pl.pallas_call(
        body, name="fwd_out_bwd_head", grid=(t // tm,),
        in_specs=[_row_spec(tm, D), _row_spec(tm, D), _row_spec(tm, D, 1), _row_spec(tm, D, 2),
                  _row_spec(tm, D), _row_spec(tm, D), pl.BlockSpec((1, D), lambda i: (0, 0)),
                  _whole_vmem(), _whole_vmem()],
        out_specs=[pl.BlockSpec((8, 128), lambda i: (0, 0)), _row_spec(tm, D), _row_spec(tm, D), _row_spec(tm, D),
                   _row_spec(tm, 2 * D), _whole_vmem(), _whole_vmem(), pl.BlockSpec((8, D), lambda i: (0, 0))],
        out_shape=[jax.ShapeDtypeStruct((8, 128), F32), jax.ShapeDtypeStruct((t, D), F32),
                   jax.ShapeDtypeStruct((t, D), BF16), jax.ShapeDtypeStruct((t, D), F32),
                   jax.ShapeDtypeStruct((t, 2 * D), BF16), jax.ShapeDtypeStruct((D, D), F32),
                   jax.ShapeDtypeStruct((D, D), F32), jax.ShapeDtypeStruct((8, D), F32)],
        compiler_params=_params("arbitrary", vmem_limit_bytes=V7X_VMEM_BYTES - (2 << 20)),
    )(ya, ub, g3, g3, x, target, g_post, wpa, wout)


def _bwd_attn(sinks, q, kv, attn, dub, g3, cos_t, sin_t, wt):
    t = q.shape[0]
    seq_blocks = SEQ_LEN // BLK

    def body(sink_ref, q_ref, kvc_ref, kvp_ref, attn_ref, dub_ref, za_ref, c_ref, s_ref, w_ref,
             dq_ref, dza_ref, dkv_own_ref, dkv_prev_ref, dsink_ref, dh_ref):
        i = pl.program_id(0)

        @pl.when(i == 0)
        def _():
            dsink_ref[...] = jnp.zeros_like(dsink_ref)

        valid = _band_mask(i % seq_blocks == 0)
        k_pad = _padded_pair(kvp_ref, kvc_ref, 0)
        v_pad = _padded_pair(kvp_ref, kvc_ref, 128)
        lo = lax.broadcasted_iota(jnp.int32, (STACK, 128), 1) < 64
        lane8 = lax.broadcasted_iota(jnp.int32, (8, 128), 1)
        c = c_ref[...]
        s = s_ref[...]
        dk_acc, dv_acc = [], []
        dsink = jnp.zeros((8, 128), F32)
        for g in range(2):
            qg, dog = [], []
            for j in range(4 * g, 4 * g + 4):
                cols = slice(j * 128, (j + 1) * 128)
                za = za_ref[:, cols]
                sg = _sig(za)
                dub = dub_ref[:, cols]
                dza_ref[:, cols] = (dub * attn_ref[:, cols] * (sg * (1.0 + za * (1.0 - sg)))).astype(BF16)
                dog.append((dub * (za * sg)).astype(BF16))
                qg.append(q_ref[:, cols])
            qg = jnp.concatenate(qg, axis=0)
            dog = jnp.concatenate(dog, axis=0)
            dq = jnp.zeros((STACK, 128), F32)
            ds_both, p_both = [], []
            for e in range(2):
                p = _exp_logits(_dot_nt(qg, k_pad[e][g]), valid, _masked_fill(sink_ref, g, e))
                p = p / jnp.sum(p, axis=-1, keepdims=True)
                dp = _dot_nt(dog, v_pad[e][g])
                ds = p * (dp - jnp.sum(p * dp, axis=-1, keepdims=True))
                for jj in range(4):
                    tot = jnp.sum(ds[jj * BLK:(jj + 1) * BLK, 0:1], axis=0, keepdims=True)
                    dsink = dsink + jnp.where(lane8 == 2 * (4 * g + jj) + e, tot, 0.0)
                ds = ds.astype(BF16)
                dq = dq + _dot(ds, k_pad[e][g])
                ds_both.append(ds)
                p_both.append(p.astype(BF16))
            zero = jnp.zeros_like(qg)
            q2 = jnp.concatenate([jnp.where(lo, qg, zero), jnp.where(lo, zero, qg)], axis=0)
            do2 = jnp.concatenate([jnp.where(lo, dog, zero), jnp.where(lo, zero, dog)], axis=0)
            dk_acc.append(_dot_tn(jnp.concatenate(ds_both, axis=0), q2))
            dv_acc.append(_dot_tn(jnp.concatenate(p_both, axis=0), do2))
            for jj in range(4):
                cols = slice((4 * g + jj) * 128, (4 * g + jj + 1) * 128)
                dqj = dq[jj * BLK:(jj + 1) * BLK, :] * LOGIT_SCALE
                dq_ref[:, cols] = (dqj * c - _swap_halves(dqj) * s).astype(BF16)
        dsink_ref[...] += dsink
        lo2 = lax.broadcasted_iota(jnp.int32, (2 * BLK, 128), 1) < 64
        sink_row = lax.broadcasted_iota(jnp.int32, (2 * BLK, 128), 0) == 0
        for col, acc in ((0, dk_acc), (128, dv_acc)):
            both = jnp.where(lo2, acc[0] + pltpu.roll(acc[0], 64, 1), acc[1] + pltpu.roll(acc[1], 64, 1))
            both = jnp.where(sink_row, 0.0, both)
            dkv_prev_ref[:, col:col + 128] = both[0:BLK, :]
            dkv_own_ref[:, col:col + 128] = both[BLK:2 * BLK, :]
        dh_ref[...] = _dot(dq_ref[...], w_ref[ROW_Q:ROW_KV, :]) + _dot(dza_ref[...], w_ref[ROW_ZA:ROW_GA, :])

    tab = pl.BlockSpec((BLK, 128), lambda i: (i % seq_blocks, 0))
    return pl.pallas_call(
        body, name="bwd_attn", grid=(t // BLK,),
        in_specs=[pl.BlockSpec(memory_space=pltpu.SMEM), _row_spec(BLK, D), _row_spec(BLK, 256),
                  pl.BlockSpec((BLK, 256), lambda i: (jnp.maximum(i - 1, 0), 0)),
                  _row_spec(BLK, D), _row_spec(BLK, D), _row_spec(BLK, D, 0), tab, tab, _whole_vmem()],
        out_specs=[_row_spec(BLK, D), _row_spec(BLK, D), _row_spec(BLK, 256), _row_spec(BLK, 256),
                   pl.BlockSpec((8, 128), lambda i: (0, 0)), _row_spec(BLK, D)],
        out_shape=[jax.ShapeDtypeStruct((t, D), BF16), jax.ShapeDtypeStruct((t, D), BF16),
                   jax.ShapeDtypeStruct((t, 256), F32), jax.ShapeDtypeStruct((t, 256), F32),
                   jax.ShapeDtypeStruct((8, 128), F32), jax.ShapeDtypeStruct((t, D), F32)],
        compiler_params=_params("arbitrary"),
    )(sinks, q, kv, kv, attn, dub, g3, cos_t, sin_t, wt)


def _bwd_kv_finish(dkv_own, dkv_prev, cos_t, sin_t):
    t = dkv_own.shape[0]
    tm = 512
    seq_tiles = SEQ_LEN // tm
    n_blocks = t // BLK

    def body(own_ref, same_ref, nxt_ref, c_ref, s_ref, out_ref):
        keep = jnp.where(pl.program_id(0) % seq_tiles == seq_tiles - 1, 0.0, 1.0)
        shifted = jnp.concatenate([same_ref[BLK:tm, :], nxt_ref[...] * keep], axis=0)
        tot = own_ref[...] + shifted
        dk = tot[:, 0:128]
        out_ref[:, 0:128] = (dk * c_ref[...] - _swap_halves(dk) * s_ref[...]).astype(BF16)
        out_ref[:, 128:256] = tot[:, 128:256].astype(BF16)

    tab = pl.BlockSpec((tm, 128), lambda i: (i % seq_tiles, 0))
    return pl.pallas_call(
        body, name="bwd_kv_finish", grid=(t // tm,),
        in_specs=[_row_spec(tm, 256), _row_spec(tm, 256),
                  pl.BlockSpec((BLK, 256), lambda i: (jnp.minimum((i + 1) * (tm // BLK), n_blocks - 1), 0)), tab, tab],
        out_specs=_row_spec(tm, 256),
        out_shape=jax.ShapeDtypeStruct((t, 256), BF16),
        compiler_params=_params("parallel"),
    )(dkv_own, dkv_prev, dkv_prev, cos_t, sin_t)


STAGE_ROWS = 256


def _bwd_conv(dya, a4, h, wconv8, wpc, tm, parts):
    t = a4.shape[0]
    n_t = t // tm
    sub = tm // parts
    seq_tiles = SEQ_LEN // tm

    def body(dya_ref, xc_ref, bg_ref, cg_ref, zc_ref, xcp_ref, cgp_ref, w_ref, wpc_ref, h_ref,
             da4_ref, dwpc_ref, dwc_ref, o32_ref, o16_ref, acc_ref, stage_ref, later_ref, sems):
        step = pl.program_id(0)
        tile = n_t - 1 - step

        @pl.when(step == 0)
        def _():
            dwpc_ref[...] = jnp.zeros_like(dwpc_ref)
            dwc_ref[...] = jnp.zeros_like(dwc_ref)
            acc_ref[...] = jnp.zeros_like(acc_ref)

        keep_prev = jnp.where(tile % seq_tiles == 0, 0.0, 1.0)
        ends_sequence = tile % seq_tiles == seq_tiles - 1

        def part(p, later):
            r0 = p * sub
            here = slice(r0, r0 + sub)
            if p == 0:
                u_prev = cgp_ref[14:16, :].astype(F32) * xcp_ref[14:16, :].astype(F32) * keep_prev
            else:
                u_prev = cg_ref[r0 - 2:r0, :].astype(F32) * xc_ref[r0 - 2:r0, :].astype(F32)
            xc = xc_ref[here, :].astype(F32)
            bg = bg_ref[here, :].astype(F32)
            cg = cg_ref[here, :].astype(F32)
            zc = zc_ref[here, :].astype(F32)
            u, u_m1, u_m2, yconv, sg, sz, co = _conv_forward(xc, bg, cg, zc, u_prev[0:1, :], u_prev[1:2, :], w_ref)
            ua = (sz * co).astype(BF16)
            dua = _dot_nt(dya_ref[here, :], wpc_ref[...])
            da4_ref[here, 3 * D:4 * D] = (dua * co * (sg * (1.0 + zc * (1.0 - sg)))).astype(BF16)
            dco = dua * sz
            da4_ref[here, D:2 * D] = (dco * yconv).astype(BF16)
            dyc = dco * bg
            dwc = jnp.concatenate([jnp.sum(dyc * s, axis=0, keepdims=True) for s in (u_m2, u_m1, u)], axis=0)
            rows = lax.broadcasted_iota(jnp.int32, xc.shape, 0)
            n0 = later[0:1, :]
            n1 = later[1:2, :]
            dyc_p1 = jnp.where(rows == sub - 1, n0, pltpu.roll(dyc, sub - 1, 0))
            dyc_p2 = jnp.where(rows == sub - 2, n0, jnp.where(rows == sub - 1, n1, pltpu.roll(dyc, sub - 2, 0)))
            du = w_ref[2:3, :] * dyc + w_ref[1:2, :] * dyc_p1 + w_ref[0:1, :] * dyc_p2
            da4_ref[here, 0:D] = (du * cg).astype(BF16)
            da4_ref[here, 2 * D:3 * D] = (du * xc).astype(BF16)
            return ua, dwc, dyc[0:8, :]

        later = jnp.where(ends_sequence, 0.0, later_ref[...])
        uas, dwc = [], jnp.zeros((3, D), F32)
        for p in reversed(range(parts)):
            ua, dwc_p, later = part(p, later)
            uas.insert(0, ua)
            dwc = dwc + dwc_p
        later_ref[...] = later
        dwpc_ref[...] += _dot_tn(jnp.concatenate(uas, axis=0), dya_ref[...])
        dwc_ref[0:3, :] += dwc
        for j in range(4):
            acc_ref[j * D:(j + 1) * D, :] += _dot_tn(da4_ref[:, j * D:(j + 1) * D], h_ref[...])

        @pl.when(step == n_t - 1)
        def _():
            c32 = pltpu.make_async_copy(acc_ref, o32_ref.at[pl.ds(0, 4 * D)], sems.at[0])
            c32.start()
            for j in range(4 * D // STAGE_ROWS):
                rows = pl.ds(j * STAGE_ROWS, STAGE_ROWS)
                stage_ref[...] = acc_ref[rows, :].astype(BF16)
                c16 = pltpu.make_async_copy(stage_ref, o16_ref.at[rows], sems.at[1])
                c16.start()
                c16.wait()
            c32.wait()

    def rows_of_tile(width, col=0):
        return pl.BlockSpec((tm, width), lambda s: (n_t - 1 - s, col))

    def prev(col):
        return pl.BlockSpec((16, D), lambda s: (jnp.maximum((n_t - 1 - s) * (tm // 16) - 1, 0), col))

    hbm = pl.BlockSpec(memory_space=pl.ANY)
    out = pl.pallas_call(
        body, name="bwd_conv", grid=(n_t,),
        in_specs=[rows_of_tile(D), rows_of_tile(D, 0), rows_of_tile(D, 1), rows_of_tile(D, 2), rows_of_tile(D, 3),
                  prev(0), prev(2), pl.BlockSpec((8, D), lambda s: (0, 0)), _whole_vmem(), rows_of_tile(D)],
        out_specs=[rows_of_tile(4 * D), _whole_vmem(), pl.BlockSpec((8, D), lambda s: (0, 0)), hbm, hbm],
        out_shape=[jax.ShapeDtypeStruct((t, 4 * D), BF16), jax.ShapeDtypeStruct((D, D), F32),
                   jax.ShapeDtypeStruct((8, D), F32), jax.ShapeDtypeStruct((D_IN, D), F32),
                   jax.ShapeDtypeStruct((D_IN, D), BF16)],
        scratch_shapes=[pltpu.VMEM((4 * D, D), F32), pltpu.VMEM((STAGE_ROWS, D), BF16), pltpu.VMEM((8, D), F32),
                        pltpu.SemaphoreType.DMA((2,))],
        compiler_params=pltpu.CompilerParams(dimension_semantics=("arbitrary",), vmem_limit_bytes=V7X_VMEM_BYTES - (2 << 20)),
    )(dya, a4, a4, a4, a4, a4, a4, wconv8, wpc, h)
    return out[0], out[1], out[2], (out[3], out[4])


def _bwd_dh(da4, dh_part, dkv, dgab, wt, x, g_pre, dout, tm):
    t = x.shape[0]

    def body(da4_ref, dhp_ref, dkv_ref, dgab_ref, w_ref, x_ref, g_ref, dout_ref, gx_ref, dg_ref):
        @pl.when(pl.program_id(0) == 0)
        def _():
            dg_ref[...] = jnp.zeros_like(dg_ref)

        dh = dhp_ref[...] + _dot(da4_ref[...], w_ref[0:ROW_Q, :])
        dh += _dot(dkv_ref[...], w_ref[ROW_KV:ROW_ZA, :])
        dh += _dot(dgab_ref[...], w_ref[ROW_GA:D_IN, :])
        xf = x_ref[...]
        r = lax.rsqrt(jnp.mean(xf * xf, axis=-1, keepdims=True) + RMS_EPS)
        xn = xf * r
        dg_ref[0:1, :] += jnp.sum(dh * xn, axis=0, keepdims=True)
        dxn = dh * g_ref[...]
        gx_ref[...] = dout_ref[...] + r * (dxn - xn * jnp.mean(dxn * xn, axis=-1, keepdims=True))

    return pl.pallas_call(
        body, name="bwd_dh", grid=(t // tm,),
        in_specs=[_row_spec(tm, 4 * D), _row_spec(tm, D), _row_spec(tm, 256), _row_spec(tm, 2 * D),
                  _whole_vmem(), _row_spec(tm, D), pl.BlockSpec((1, D), lambda i: (0, 0)), _row_spec(tm, D)],
        out_specs=[_row_spec(tm, D), pl.BlockSpec((8, D), lambda i: (0, 0))],
        out_shape=[jax.ShapeDtypeStruct((t, D), F32), jax.ShapeDtypeStruct((8, D), F32)],
        compiler_params=_params("arbitrary"),
    )(da4, dh_part, dkv, dgab, wt, x, g_pre, dout)


def _bwd_dw_in(h, piece, row0, nb, tm, name, prev):
    t, n = piece.shape
    n_t = t // tm

    def body(*refs):
        h_ref, p_ref = refs[0], refs[1]
        o32_ref, o16_ref, acc_ref, acc16_ref, sems = refs[-5:]
        j, i = pl.program_id(0), pl.program_id(1)

        @pl.when(i == 0)
        def _():
            acc_ref[...] = jnp.zeros_like(acc_ref)

        acc_ref[...] += _dot_tn(p_ref[...], h_ref[...])

        @pl.when(i == n_t - 1)
        def _():
            acc16_ref[...] = acc_ref[...].astype(BF16)
            rows = pl.ds(pl.multiple_of(row0 + j * nb, 16), nb)
            c32 = pltpu.make_async_copy(acc_ref, o32_ref.at[rows], sems.at[0])
            c16 = pltpu.make_async_copy(acc16_ref, o16_ref.at[rows], sems.at[1])
            c32.start()
            c16.start()
            c32.wait()
            c16.wait()

    hbm = pl.BlockSpec(memory_space=pl.ANY)
    carried = [] if prev is None else list(prev)
    return pl.pallas_call(
        body, name=name, grid=(n // nb, n_t),
        in_specs=[pl.BlockSpec((tm, D), lambda j, i: (i, 0)), pl.BlockSpec((tm, nb), lambda j, i: (i, j))]
        + [hbm] * len(carried),
        out_specs=[hbm, hbm],
        out_shape=[jax.ShapeDtypeStruct((D_IN, D), F32), jax.ShapeDtypeStruct((D_IN, D), BF16)],
        scratch_shapes=[pltpu.VMEM((nb, D), F32), pltpu.VMEM((nb, D), BF16), pltpu.SemaphoreType.DMA((2,))],
        input_output_aliases={2: 0, 3: 1} if carried else {},
        compiler_params=_params("arbitrary", "arbitrary"),
    )(h, piece, *carried)


def _place():
    x, y, c = lax.axis_index("x"), lax.axis_index("y"), lax.axis_index("c")
    return x, y, c, 4 * x + 2 * y + c


def _peer(x, y, c, k):
    return (1 - x if k & 4 else x, 1 - y if k & 2 else y, 1 - c if k & 1 else c)


ICI_MASKS = (4, 2, 6)


def _all_gather(shards):
    n = len(shards)

    def body(*refs):
        src, dst = refs[:n], refs[n:2 * n]
        send_sems, recv_sems, local_sems = refs[2 * n:]
        x, y, c, me = _place()
        sibling = _peer(x, y, c, 1)

        def copy(a, s, block, to, own=False):
            return pltpu.make_async_remote_copy(
                src_ref=src[a] if own else dst[a].at[block], dst_ref=dst[a].at[block],
                send_sem=send_sems.at[a * 7 + s], recv_sem=recv_sems.at[a * 7 + s], device_id=to, device_id_type=MESH_ID)

        local = [pltpu.make_async_copy(src[a], dst[a].at[me], local_sems.at[a]) for a in range(n)]
        for cp in local:
            cp.start()
        started = [copy(a, 0, me, sibling, own=True) for a in range(n)]
        started += [copy(a, 1 + j, me, _peer(x, y, c, k), own=True) for j, k in enumerate(ICI_MASKS) for a in range(n)]
        for cp in started:
            cp.start()
        for j, k in enumerate(ICI_MASKS):
            for a in range(n):
                copy(a, 1 + j, me ^ k, sibling).wait_recv()
                fwd = copy(a, 4 + j, me ^ k, sibling)
                fwd.start()
                started.append(fwd)
        for a in range(n):
            copy(a, 0, me ^ 1, sibling).wait_recv()
        for j, k in enumerate(ICI_MASKS):
            for a in range(n):
                copy(a, 4 + j, me ^ 1 ^ k, sibling).wait_recv()
        for cp in started:
            cp.wait_send()
        for cp in local:
            cp.wait()

    hbm = pl.BlockSpec(memory_space=pl.ANY)
    return pl.pallas_call(
        body, name="all_gather_weights",
        in_specs=[hbm] * n, out_specs=[hbm] * n,
        out_shape=[jax.ShapeDtypeStruct((N_DEV,) + s.shape, s.dtype) for s in shards],
        scratch_shapes=[pltpu.SemaphoreType.DMA((7 * n,)), pltpu.SemaphoreType.DMA((7 * n,)),
                        pltpu.SemaphoreType.DMA((n,))],
    )(*shards)


def _direct_copies(src, land, send_sems, recv_sems):
    x, y, c, me = _place()
    return [pltpu.make_async_remote_copy(
        src_ref=src[a], dst_ref=land[a].at[me], send_sem=send_sems.at[a * 7 + k - 1],
        recv_sem=recv_sems.at[a * 7 + k - 1], device_id=_peer(x, y, c, k), device_id_type=MESH_ID)
        for k in range(1, N_DEV) for a in range(len(src))]


def _gather_start(shards, name):
    n = len(shards)

    def body(*refs):
        src, land = refs[:n], refs[n:2 * n]
        send_sems, recv_sems = refs[2 * n], refs[2 * n + 1]
        token_ref = refs[-1]
        for cp in _direct_copies(src, land, send_sems, recv_sems):
            cp.start()
        token_ref[...] = jnp.zeros_like(token_ref)

    hbm = pl.BlockSpec(memory_space=pltpu.HBM)
    sem = pl.BlockSpec(memory_space=pltpu.SEMAPHORE)
    lands = [lax.empty((N_DEV,) + s.shape, s.dtype) for s in shards]
    out = pl.pallas_call(
        body, name=name + "_start",
        out_shape=(pltpu.SemaphoreType.DMA((7 * n,)), pltpu.SemaphoreType.DMA((7 * n,)),
                   *[pltpu.HBM(s.shape, s.dtype) for s in shards], *[pltpu.HBM(s.shape, s.dtype) for s in lands],
                   jax.ShapeDtypeStruct((8, 128), F32)),
        in_specs=[hbm] * (2 * n), out_specs=(sem, sem, *[hbm] * (2 * n), _whole_vmem()),
        input_output_aliases={a: 2 + a for a in range(2 * n)},
        compiler_params=pltpu.CompilerParams(has_side_effects=pltpu.SideEffectType.DATAFLOW_SIDE_EFFECTING),
    )(*[pltpu.with_memory_space_constraint(s, pltpu.HBM) for s in list(shards) + lands])
    return out[0], out[1], out[2:2 + n], out[2 + n:2 + 2 * n], out[-1]


def _gather_wait(send_sems, recv_sems, flying, lands, after, name):
    n = len(flying)

    def body(*refs):
        src, land = refs[:n], refs[n:2 * n]
        for cp in _direct_copies(src, land, refs[2 * n], refs[2 * n + 1]):
            cp.wait_send()
            cp.wait_recv()

    hbm = pl.BlockSpec(memory_space=pltpu.HBM)
    sem = pl.BlockSpec(memory_space=pltpu.SEMAPHORE)
    out = pl.pallas_call(
        body, name=name + "_wait",
        out_shape=tuple(pltpu.HBM(s.shape, s.dtype) for s in list(flying) + list(lands)),
        in_specs=[hbm] * (2 * n) + [sem, sem, pl.BlockSpec(memory_space=pl.ANY)], out_specs=tuple([hbm] * (2 * n)),
        input_output_aliases={a: a for a in range(2 * n)},
        compiler_params=pltpu.CompilerParams(has_side_effects=pltpu.SideEffectType.DATAFLOW_SIDE_EFFECTING),
    )(*flying, *lands, send_sems, recv_sems, after)
    return out[n:]


def _exchange_sibling(by_dest):
    n = len(by_dest)

    def body(*refs):
        src, dst = refs[:n], refs[n:2 * n]
        send_sems, recv_sems = refs[2 * n:]
        x, y, c, _ = _place()
        sibling = _peer(x, y, c, 1)
        copies = [pltpu.make_async_remote_copy(
            src_ref=src[a].at[2 * p + (1 - c)], dst_ref=dst[a].at[p], send_sem=send_sems.at[a * 4 + p],
            recv_sem=recv_sems.at[a * 4 + p], device_id=sibling, device_id_type=MESH_ID)
            for a in range(n) for p in range(4)]
        for cp in copies:
            cp.start()
        for cp in copies:
            cp.wait_recv()
        for cp in copies:
            cp.wait_send()

    hbm = pl.BlockSpec(memory_space=pl.ANY)
    return pl.pallas_call(
        body, name="exchange_sibling", in_specs=[hbm] * n, out_specs=[hbm] * n,
        out_shape=[jax.ShapeDtypeStruct((4,) + s.shape[1:], s.dtype) for s in by_dest],
        scratch_shapes=[pltpu.SemaphoreType.DMA((4 * n,)), pltpu.SemaphoreType.DMA((4 * n,))],
    )(*by_dest)


def _chip_copies(src, land, send_sems, recv_sems):
    x, y, c, _ = _place()
    chip = 2 * x + y
    return [pltpu.make_async_remote_copy(
        src_ref=src[a].at[chip ^ (k >> 1)], dst_ref=land[a].at[j], send_sem=send_sems.at[a * 3 + j],
        recv_sem=recv_sems.at[a * 3 + j], device_id=_peer(x, y, c, k), device_id_type=MESH_ID)
        for j, k in enumerate(ICI_MASKS) for a in range(len(src))]


def _exchange_chips_start(by_chip):
    n = len(by_chip)

    def body(*refs):
        src, land = refs[:n], refs[n:2 * n]
        send_sems, recv_sems = refs[2 * n], refs[2 * n + 1]
        token_ref = refs[-1]
        for cp in _chip_copies(src, land, send_sems, recv_sems):
            cp.start()
        token_ref[...] = jnp.zeros_like(token_ref)

    hbm = pl.BlockSpec(memory_space=pltpu.HBM)
    sem = pl.BlockSpec(memory_space=pltpu.SEMAPHORE)
    lands = [lax.empty((3,) + s.shape[1:], s.dtype) for s in by_chip]
    out = pl.pallas_call(
        body, name="exchange_chips_start",
        out_shape=(pltpu.SemaphoreType.DMA((3 * n,)), pltpu.SemaphoreType.DMA((3 * n,)),
                   *[pltpu.HBM(s.shape, s.dtype) for s in by_chip], *[pltpu.HBM(s.shape, s.dtype) for s in lands],
                   jax.ShapeDtypeStruct((8, 128), F32)),
        in_specs=[hbm] * (2 * n), out_specs=(sem, sem, *[hbm] * (2 * n), _whole_vmem()),
        input_output_aliases={a: 2 + a for a in range(2 * n)},
        compiler_params=pltpu.CompilerParams(has_side_effects=pltpu.SideEffectType.DATAFLOW_SIDE_EFFECTING),
    )(*[pltpu.with_memory_space_constraint(s, pltpu.HBM) for s in list(by_chip) + lands])
    return out[0], out[1], out[2:2 + n], out[2 + n:2 + 2 * n], out[-1]


def _exchange_chips_wait(send_sems, recv_sems, flying, lands, after):
    n = len(flying)

    def body(*refs):
        src, land = refs[:n], refs[n:2 * n]
        send_sems_ref, recv_sems_ref = refs[2 * n], refs[2 * n + 1]
        for cp in _chip_copies(src, land, send_sems_ref, recv_sems_ref):
            cp.wait_send()
            cp.wait_recv()

    hbm = pl.BlockSpec(memory_space=pltpu.HBM)
    sem = pl.BlockSpec(memory_space=pltpu.SEMAPHORE)
    out = pl.pallas_call(
        body, name="exchange_chips_wait",
        out_shape=tuple(pltpu.HBM(s.shape, s.dtype) for s in list(flying) + list(lands)),
        in_specs=[hbm] * (2 * n) + [sem, sem, pl.BlockSpec(memory_space=pl.ANY)], out_specs=tuple([hbm] * (2 * n)),
        input_output_aliases={a: a for a in range(2 * n)},
        compiler_params=pltpu.CompilerParams(has_side_effects=pltpu.SideEffectType.DATAFLOW_SIDE_EFFECTING),
    )(*flying, *lands, send_sems, recv_sems, after)
    return out[n:]


def _adamw_math(w, g, m, v):
    m = ADAM_B1 * m + (1.0 - ADAM_B1) * g
    v = ADAM_B2 * v + (1.0 - ADAM_B2) * (g * g)
    m_hat = m / (1.0 - ADAM_B1 ** ADAM_STEP)
    v_hat = v / (1.0 - ADAM_B2 ** ADAM_STEP)
    return -ADAM_LR * (m_hat / (jnp.sqrt(v_hat) + ADAM_EPS) + ADAM_WD * w), m, v


def _pair_sum(owns, recvs, c_arr, tr, name):
    n = len(owns)
    _, rows, cols = owns[0].shape

    def body(c_ref, *refs):
        for a in range(n):
            s = refs[a][...] + refs[n + a][...].astype(F32)
            refs[2 * n + a][...] = s
            refs[3 * n + a][...] = s.astype(BF16)

    by_chip = pl.BlockSpec((None, tr, cols), lambda p, i, c_ref: (p, i, 0))
    mine = pl.BlockSpec((None, tr, cols), lambda p, i, c_ref: (2 * p + c_ref[0], i, 0))
    out = pl.pallas_call(
        body, name=name,
        grid_spec=pltpu.PrefetchScalarGridSpec(
            num_scalar_prefetch=1, grid=(4, rows // tr), in_specs=[mine] * n + [by_chip] * n, out_specs=[by_chip] * (2 * n)),
        out_shape=[jax.ShapeDtypeStruct((4, rows, cols), F32)] * n + [jax.ShapeDtypeStruct((4, rows, cols), BF16)] * n,
        compiler_params=_params("parallel", "parallel"),
    )(c_arr, *owns, *recvs)
    return out[:n], out[n:]


def _chip_sum(pairs, recvs, chip_arr, tr, name, adam=None):
    n = len(pairs)
    _, rows, cols = pairs[0].shape
    n_state = 0 if adam is None else 3 * n

    def body(chip_ref, *refs):
        outs = refs[2 * n + n_state:]
        for a in range(n):
            g = refs[a][...]
            for j in range(3):
                g = g + refs[n + a][j].astype(F32)
            outs[a][...] = g
            if adam is not None:
                w_ref, m_ref, v_ref = (refs[2 * n + s * n + a] for s in range(3))
                outs[n + a][...], outs[2 * n + a][...], outs[3 * n + a][...] = _adamw_math(w_ref[...], g, m_ref[...], v_ref[...])

    blk = pl.BlockSpec((tr, cols), lambda i, chip_ref: (i, 0))
    n_out = n if adam is None else 4 * n
    out = pl.pallas_call(
        body, name=name,
        grid_spec=pltpu.PrefetchScalarGridSpec(
            num_scalar_prefetch=1, grid=(rows // tr,),
            in_specs=[pl.BlockSpec((None, tr, cols), lambda i, chip_ref: (chip_ref[0], i, 0))] * n
            + [pl.BlockSpec((3, tr, cols), lambda i, chip_ref: (0, i, 0))] * n + [blk] * n_state,
            out_specs=[blk] * n_out),
        out_shape=[jax.ShapeDtypeStruct((rows, cols), F32)] * n_out,
        compiler_params=_params("parallel"),
    )(chip_arr, *pairs, *recvs, *([] if adam is None else [t for group in adam for t in group]))
    return out if adam is None else (out[:n], out[n:2 * n], out[2 * n:3 * n], out[3 * n:])


def _adamw(ws, gs, ms, vs, name):
    n = len(ws)

    def body(*refs):
        for a in range(n):
            w_ref, g_ref, m_ref, v_ref = (refs[s * n + a] for s in range(4))
            refs[4 * n + a][...], refs[5 * n + a][...], refs[6 * n + a][...] = _adamw_math(
                w_ref[...], g_ref[...], m_ref[...], v_ref[...])

    out = pl.pallas_call(body, name=name, out_shape=[jax.ShapeDtypeStruct(w.shape, F32) for w in ws] * 3)(
        *ws, *gs, *ms, *vs)
    return out[:n], out[n:2 * n], out[2 * n:]


def _sum_small(small_all):
    def body(s_ref, o_ref):
        g = s_ref[0]
        for d in range(1, N_DEV):
            g = g + s_ref[d]
        o_ref[...] = g

    return pl.pallas_call(body, name="sum_small", out_shape=jax.ShapeDtypeStruct(small_all.shape[1:], F32))(small_all)


def _rope_tables():
    inv_freq = ROPE_THETA ** (-jnp.arange(0, HEAD_DIM, 2, dtype=F32) / HEAD_DIM)
    ang = jnp.arange(SEQ_LEN).astype(F32)[:, None] * inv_freq[None, :]
    cos, sin = jnp.cos(ang), jnp.sin(ang)
    return jnp.tile(cos, (1, 4)), jnp.tile(jnp.concatenate([-sin, sin], axis=1), (1, 2))


def _local_step(x, target, g_pre, g_post, sinks, wt, wconv, squares, start_exchange=None):
    cos_t, sin_t = _rope_tables()
    wconv8 = jnp.pad(wconv, ((0, 5), (0, 0)))
    h, q, kv, g3 = _fwd_in_attn(x, g_pre, wt, cos_t, sin_t, 512)
    wpc, wpa, wout = squares(kv)
    a4, ya = _fwd_in_conv(h, wt, wconv8, wpc, 512)
    attn, ub = _fwd_attn(sinks, q, kv, g3)
    loss8, dout, dya, dub, dgab, dwout, dwpa, dgpost8 = _fwd_out_bwd_head(ya, ub, g3, x, target, g_post, wpa, wout, 512)
    dq, dza, dkv_own, dkv_prev, dsink8, dh_part = _bwd_attn(sinks, q, kv, attn, dub, g3, cos_t, sin_t, wt)
    dkv = _bwd_kv_finish(dkv_own, dkv_prev, cos_t, sin_t)
    da4, dwpc, dwconv8, dwt = _bwd_conv(dya, a4, h, wconv8, wpc, 512, 2)
    dwt = _bwd_dw_in(h, dq, ROW_Q, 1024, 1024, "bwd_dw_in_q", dwt)
    dwt = _bwd_dw_in(h, dkv, ROW_KV, 256, 1024, "bwd_dw_in_kv", dwt)
    dwt = _bwd_dw_in(h, dza, ROW_ZA, 1024, 1024, "bwd_dw_in_za", dwt)
    dwt32, dwt16 = _bwd_dw_in(h, dgab, ROW_GA, 1024, 1024, "bwd_dw_in_gates", dwt)
    token, pending = (None, None) if start_exchange is None else start_exchange(dwt32, dwt16, dwpc, dwpa, dwout)
    g_pre_after = g_pre if token is None else g_pre + token[0:1, 0:1]
    grad_x, dgpre8 = _bwd_dh(da4, dh_part, dkv, dgab, wt, x, g_pre_after, dout, 512)
    small = jnp.concatenate([dgpre8, dgpost8, jnp.pad(dsink8, ((0, 0), (0, D - 128))), dwconv8,
                             jnp.pad(loss8, ((0, 0), (0, D - 128)))], axis=0)
    return loss8[0, 0], grad_x, dwt32, dwt16, dwpc, dwpa, dwout, small, pending


def kernel(x, g_pre, g_post, w_in, w_conv, sinks, w_proj_conv, w_proj_attn, w_out, loss_target, m_g_pre, m_g_post, m_w_in, m_w_conv, m_sinks, m_w_proj_conv, m_w_proj_attn, m_w_out, v_g_pre, v_g_post, v_w_in, v_w_conv, v_sinks, v_w_proj_conv, v_w_proj_attn, v_w_out):
    batch = x.shape[0]
    mx, my, mc, me = _place()
    c_arr = jnp.reshape(mc, (1,)).astype(jnp.int32)
    chip_arr = jnp.reshape(2 * mx + my, (1,)).astype(jnp.int32)

    g_wt, g_conv = _all_gather([w_in[0].T.astype(BF16), jnp.pad(w_conv[0], ((0, 5), (0, 0)))])
    wt = g_wt.reshape(D_IN, D)
    wconv = g_conv[:, 0:3, :].transpose(1, 0, 2).reshape(3, D)
    sq_mine = [w.astype(BF16) for w in (w_proj_conv[0], w_proj_attn[0], w_out[0])]
    wt, sq_mine = lax.optimization_barrier((wt, sq_mine))
    sq_send, sq_recv, sq_flying, sq_lands, sq_token = _gather_start(sq_mine, "gather_squares")

    def squares(after):
        got = _gather_wait(sq_send, sq_recv, sq_flying, sq_lands, after, "gather_squares")
        return [lax.dynamic_update_index_in_dim(full, mine, me, 0).reshape(D, D) for full, mine in zip(got, sq_mine)]

    def start_exchange(dwt32, dwt16, dwpc, dwpa, dwout):
        own_sq = [g.reshape(N_DEV, SHARD_SQ, D) for g in (dwpc, dwpa, dwout)]
        own_in = dwt32.reshape(N_DEV, SHARD_IN, D)
        from_sibling = _exchange_sibling([dwt16.reshape(N_DEV, SHARD_IN, D)] + [g.astype(BF16) for g in own_sq])
        in32, in16 = _pair_sum([own_in], from_sibling[:1], c_arr, SHARD_IN // 2, "pair_sum_w_in")
        sq32, sq16 = _pair_sum(own_sq, from_sibling[1:], c_arr, SHARD_SQ, "pair_sum_squares")
        send_sems, recv_sems, flying, lands, token = _exchange_chips_start(list(in16) + list(sq16))
        return token, (send_sems, recv_sems, flying, lands, in32, sq32)

    _, grad_x, _, _, _, _, _, small, pending = _local_step(
        x.reshape(batch * SEQ_LEN, D), loss_target.reshape(batch * SEQ_LEN, D), g_pre + sq_token[0:1, 0:1], g_post,
        sinks, wt, wconv, squares, start_exchange)
    sm_send, sm_recv, sm_flying, sm_lands, _ = _gather_start([small], "gather_small")
    send_sems, recv_sems, flying, lands, in32, sq32 = pending
    from_chips = _exchange_chips_wait(send_sems, recv_sems, flying, lands, small)

    o_in = [o[0].T for o in _chip_sum(
        in32, from_chips[:1], chip_arr, SHARD_IN // 3, "chip_sum_adamw_w_in",
        adam=([w_in[0].T], [m_w_in[0].T], [v_w_in[0].T]))]
    g_in_mine, o_in = o_in[0], o_in[1:]
    g_sq, d_sq, m_sq, v_sq = _chip_sum(
        sq32, from_chips[1:], chip_arr, SHARD_SQ, "chip_sum_adamw_squares",
        adam=([w_proj_conv[0], w_proj_attn[0], w_out[0]], [m_w_proj_conv[0], m_w_proj_attn[0], m_w_out[0]],
              [v_w_proj_conv[0], v_w_proj_attn[0], v_w_out[0]]))
    (small_all,) = _gather_wait(sm_send, sm_recv, sm_flying, sm_lands, d_sq[0], "gather_small")
    gs = _sum_small(lax.dynamic_update_index_in_dim(small_all, small, me, 0))
    g_g_pre, g_g_post, g_sinks, loss = gs[0:1], gs[8:9], gs[16:17, 0:N_HEADS], gs[32, 0]
    g_conv_mine = lax.dynamic_slice_in_dim(gs[24:27], me * SHARD_SQ, SHARD_SQ, axis=1)
    o_small = _adamw([g_pre, g_post, sinks, w_conv[0]], [g_g_pre, g_g_post, g_sinks, g_conv_mine],
                     [m_g_pre, m_g_post, m_sinks, m_w_conv[0]], [v_g_pre, v_g_post, v_sinks, v_w_conv[0]], "adamw_small")

    grads = [g_g_pre, g_g_post, g_in_mine[None], g_conv_mine[None], g_sinks] + [g[None] for g in g_sq]
    rest = []
    for idx, sq in enumerate((d_sq, m_sq, v_sq)):
        gp, gq, sk, cv = o_small[idx]
        rest += [gp, gq, o_in[idx][None], cv[None], sk] + [s[None] for s in sq]
    return (loss, grad_x.reshape(batch, SEQ_LEN, D), *grads, *rest)
```

```python
import functools

import jax
import jax.numpy as jnp
from jax import lax
from jax.experimental import pallas as pl
from jax.experimental.pallas import tpu as pltpu

D = 1024
N_HEADS = 16
HEAD_DIM = 64
LOGIT_SCALE = HEAD_DIM ** -0.5
BLK = 128
SEQ_LEN = 2048
D_IN = 8448
ROW_Q, ROW_KV, ROW_ZA, ROW_GA = 4 * D, 5 * D, 5 * D + 256, 6 * D + 256
SHARD_IN = D_IN // 8
SHARD_SQ = D // 8
N_DEV = 8
V7X_VMEM_BYTES = 64 << 20
ROPE_THETA = 10000.0
RMS_EPS = 1e-6
NEG = -1e30
ADAM_LR, ADAM_B1, ADAM_B2, ADAM_EPS, ADAM_WD, ADAM_STEP = 0.001, 0.9, 0.999, 1e-08, 0.01, 10

F32 = jnp.float32
BF16 = jnp.bfloat16
MESH_ID = pl.DeviceIdType.MESH


def _dot(a, b):
    return jnp.dot(a, b, preferred_element_type=F32)


def _dot_nt(a, b):
    return lax.dot_general(a, b, (((1,), (1,)), ((), ())), preferred_element_type=F32)


def _dot_tn(a, b):
    return lax.dot_general(a, b, (((0,), (0,)), ((), ())), preferred_element_type=F32)


def _sig(z):
    return 1.0 / (1.0 + jnp.exp(-z))


def _swap_halves(z):
    lane = lax.broadcasted_iota(jnp.int32, z.shape, 1)
    return jnp.where((lane & 63) < 32, pltpu.roll(z, 96, 1), pltpu.roll(z, 32, 1))


def _row_spec(tm, width, col=0):
    return pl.BlockSpec((tm, width), lambda i: (i, col))


def _whole_vmem():
    return pl.BlockSpec(memory_space=pltpu.VMEM)


def _params(*sem, vmem_limit_bytes=None):
    return pltpu.CompilerParams(dimension_semantics=sem, vmem_limit_bytes=vmem_limit_bytes)


def _fwd_in_attn(x, g_pre, wt, cos_t, sin_t, tm):
    t = x.shape[0]
    seq_tiles = SEQ_LEN // tm

    def body(x_ref, g_ref, w_ref, c_ref, s_ref, h_ref, q_ref, kv_ref, g3_ref):
        xf = x_ref[...]
        r = lax.rsqrt(jnp.mean(xf * xf, axis=-1, keepdims=True) + RMS_EPS)
        hh = ((xf * r) * g_ref[...]).astype(BF16)
        h_ref[...] = hh
        c = c_ref[...]
        s = s_ref[...]

        def rope(z):
            return z * c + _swap_halves(z) * s

        q = _dot_nt(hh, w_ref[ROW_Q:ROW_Q + D, :])
        for j in range(D // 128):
            q_ref[:, j * 128:(j + 1) * 128] = (rope(q[:, j * 128:(j + 1) * 128]) * LOGIT_SCALE).astype(BF16)
        kv = _dot_nt(hh, w_ref[ROW_KV:ROW_KV + 256, :])
        kv_ref[:, 0:128] = rope(kv[:, 0:128]).astype(BF16)
        kv_ref[:, 128:256] = kv[:, 128:256].astype(BF16)
        for j in range(3):
            g3_ref[:, j * D:(j + 1) * D] = _dot_nt(hh, w_ref[ROW_ZA + j * D:ROW_ZA + (j + 1) * D, :])

    tab = pl.BlockSpec((tm, 128), lambda i: (i % seq_tiles, 0))
    return pl.pallas_call(
        body, name="fwd_in_attn", grid=(t // tm,),
        in_specs=[_row_spec(tm, D), pl.BlockSpec((1, D), lambda i: (0, 0)), _whole_vmem(), tab, tab],
        out_specs=[_row_spec(tm, D), _row_spec(tm, D), _row_spec(tm, 256), _row_spec(tm, 3 * D)],
        out_shape=[jax.ShapeDtypeStruct((t, D), BF16), jax.ShapeDtypeStruct((t, D), BF16),
                   jax.ShapeDtypeStruct((t, 256), BF16), jax.ShapeDtypeStruct((t, 3 * D), F32)],
        compiler_params=_params("parallel"),
    )(x, g_pre, wt, cos_t, sin_t)


def _conv_forward(xc, bg, cg, zc, up6, up7, w_ref):
    tm = xc.shape[0]
    rows = lax.broadcasted_iota(jnp.int32, xc.shape, 0)
    u = cg * xc
    u_m1 = jnp.where(rows == 0, up7, pltpu.roll(u, 1, 0))
    u_m2 = jnp.where(rows == 0, up6, jnp.where(rows == 1, up7, pltpu.roll(u, 2, 0)))
    yconv = w_ref[0:1, :] * u_m2 + w_ref[1:2, :] * u_m1 + w_ref[2:3, :] * u
    sg = _sig(zc)
    sz = zc * sg
    co = bg * yconv
    del tm
    return u, u_m1, u_m2, yconv, sg, sz, co


def _fwd_in_conv(h, wt, wconv8, wpc, tm):
    t = h.shape[0]
    seq_tiles = SEQ_LEN // tm

    def body(h_ref, w_ref, wc_ref, wpc_ref, a4_ref, ya_ref, last_u_ref):
        hh = h_ref[...]
        xc, bg, cg, zc = (_dot_nt(hh, w_ref[j * D:(j + 1) * D, :]) for j in range(4))
        for j, z in enumerate((xc, bg, cg, zc)):
            a4_ref[:, j * D:(j + 1) * D] = z.astype(BF16)
        first = pl.program_id(0) % seq_tiles == 0
        up6 = jnp.where(first, 0.0, last_u_ref[6:7, :])
        up7 = jnp.where(first, 0.0, last_u_ref[7:8, :])
        u, _, _, _, _, sz, co = _conv_forward(xc, bg, cg, zc, up6, up7, wc_ref)
        last_u_ref[...] = u[tm - 8:tm, :]
        ya_ref[...] = _dot((sz * co).astype(BF16), wpc_ref[...])

    return pl.pallas_call(
        body, name="fwd_in_conv", grid=(t // tm,),
        in_specs=[_row_spec(tm, D), _whole_vmem(), pl.BlockSpec((8, D), lambda i: (0, 0)), _whole_vmem()],
        out_specs=[_row_spec(tm, 4 * D), _row_spec(tm, D)],
        out_shape=[jax.ShapeDtypeStruct((t, 4 * D), BF16), jax.ShapeDtypeStruct((t, D), F32)],
        scratch_shapes=[pltpu.VMEM((8, D), F32)],
        compiler_params=_params("arbitrary"),
    )(h, wt, wconv8, wpc)


STACK = 4 * BLK


def _band_mask(first):
    qi = lax.broadcasted_iota(jnp.int32, (STACK, 2 * BLK), 0) & (BLK - 1)
    kj = lax.broadcasted_iota(jnp.int32, (STACK, 2 * BLK), 1)
    return (kj > qi) & (kj <= qi + BLK) & (kj >= jnp.where(first, BLK, 0))


def _masked_fill(sink_ref, g, e):
    kj = lax.broadcasted_iota(jnp.int32, (STACK, 2 * BLK), 1)
    sink = jnp.concatenate([jnp.full((BLK, 2 * BLK), sink_ref[0, 2 * (4 * g + jj) + e], F32) for jj in range(4)], axis=0)
    return jnp.where(kj == 0, sink, NEG)


def _padded_pair(before, own, other=0.0):
    z = jnp.concatenate([before, own], axis=0).astype(F32)
    z = jnp.where(lax.broadcasted_iota(jnp.int32, z.shape, 0) == 0, 0.0, z)
    zs = pltpu.roll(z, 64, 1)
    lo = lax.broadcasted_iota(jnp.int32, z.shape, 1) < 64
    fill = jnp.full_like(z, other)
    left = [jnp.where(lo, z, fill).astype(BF16), jnp.where(lo, zs, fill).astype(BF16)]
    right = [jnp.where(lo, fill, zs).astype(BF16), jnp.where(lo, fill, z).astype(BF16)]
    return left, right


def _exp_logits(s, valid, fill):
    s = jnp.where(valid, s, fill)
    return jnp.exp(s - jnp.max(s, axis=-1, keepdims=True))


ATTN_BLOCKS = 2


def _kv_blocks(kvc_ref, kvp_ref, b, col):
    own = kvc_ref[b * BLK:(b + 1) * BLK, col:col + 128]
    before = kvp_ref[:, col:col + 128] if b == 0 else kvc_ref[(b - 1) * BLK:b * BLK, col:col + 128]
    return before, own


def _fwd_attn(sinks, q, kv, g3):
    t = q.shape[0]
    tq = ATTN_BLOCKS * BLK
    seq_blocks = SEQ_LEN // BLK

    def body(sink_ref, q_ref, kvc_ref, kvp_ref, za_ref, attn_ref, ub_ref):
        lo = lax.broadcasted_iota(jnp.int32, (STACK, 128), 1) < 64
        for b in range(ATTN_BLOCKS):
            rows = slice(b * BLK, (b + 1) * BLK)
            valid = _band_mask((pl.program_id(0) * ATTN_BLOCKS + b) % seq_blocks == 0)
            k_pad = _padded_pair(*_kv_blocks(kvc_ref, kvp_ref, b, 0))
            v_one = _padded_pair(*_kv_blocks(kvc_ref, kvp_ref, b, 128), other=1.0)
            for g in range(2):
                qg = jnp.concatenate([q_ref[rows, j * 128:(j + 1) * 128] for j in range(4 * g, 4 * g + 4)], axis=0)
                pv = [_dot(_exp_logits(_dot_nt(qg, k_pad[e][g]), valid, _masked_fill(sink_ref, g, e)).astype(BF16),
                           v_one[e][g]) for e in range(2)]
                o = jnp.where(lo, pv[0], pv[1]) / pltpu.roll(jnp.where(lo, pv[1], pv[0]), 64, 1)
                for jj in range(4):
                    cols = slice((4 * g + jj) * 128, (4 * g + jj + 1) * 128)
                    oj = o[jj * BLK:(jj + 1) * BLK, :]
                    attn_ref[rows, cols] = oj
                    za = za_ref[rows, cols]
                    ub_ref[rows, cols] = (za * _sig(za) * oj).astype(BF16)

    return pl.pallas_call(
        body, name="fwd_attn", grid=(t // tq,),
        in_specs=[pl.BlockSpec(memory_space=pltpu.SMEM), _row_spec(tq, D), _row_spec(tq, 256),
                  pl.BlockSpec((BLK, 256), lambda i: (jnp.maximum(i * ATTN_BLOCKS - 1, 0), 0)), _row_spec(tq, D, 0)],
        out_specs=[_row_spec(tq, D), _row_spec(tq, D)],
        out_shape=[jax.ShapeDtypeStruct((t, D), F32), jax.ShapeDtypeStruct((t, D), BF16)],
        compiler_params=_params("parallel"),
    )(sinks, q, kv, kv, g3)


def _fwd_out_bwd_head(ya, ub, g3, x, target, g_post, wpa, wout, tm, parts=1):
    t = x.shape[0]

    def body(ya_ref, ub_ref, ga_ref, gb_ref, x_ref, tgt_ref, gp_ref, wpa_ref, wout_ref,
             loss_ref, dout_ref, dya_ref, dub_ref, dgab_ref, dwout_ref, dwpa_ref, dgp_ref):
        @pl.when(pl.program_id(0) == 0)
        def _():
            loss_ref[...] = jnp.zeros_like(loss_ref)
            dwout_ref[...] = jnp.zeros_like(dwout_ref)
            dwpa_ref[...] = jnp.zeros_like(dwpa_ref)
            dgp_ref[...] = jnp.zeros_like(dgp_ref)

        g = gp_ref[...]
        sq = jnp.zeros((1, 1), F32)
        dgp = jnp.zeros((1, D), F32)
        mbs, dys, dybs = [], [], []
        for part in range(parts):
            rows = slice(part * (tm // parts), (part + 1) * (tm // parts))
            ub = ub_ref[rows, :]
            ya = ya_ref[rows, :]
            yb = _dot(ub, wpa_ref[...])
            sa = _sig(ga_ref[rows, :])
            sb = _sig(gb_ref[rows, :])
            mb = (sa * ya + sb * yb).astype(BF16)
            y = _dot(mb, wout_ref[...])
            r = lax.rsqrt(jnp.mean(y * y, axis=-1, keepdims=True) + RMS_EPS)
            n = y * r
            err = (x_ref[rows, :] + n * g) - tgt_ref[rows, :]
            sq = sq + jnp.sum(jnp.sum(err * err, axis=0, keepdims=True), axis=1, keepdims=True)
            dout = err * (1.0 / D)
            dout_ref[rows, :] = dout
            dgp = dgp + jnp.sum(dout * n, axis=0, keepdims=True)
            dn = dout * g
            dy = (r * (dn - n * jnp.mean(dn * n, axis=-1, keepdims=True))).astype(BF16)
            dm = _dot_nt(dy, wout_ref[...])
            dya_ref[rows, :] = (dm * sa).astype(BF16)
            dyb = (dm * sb).astype(BF16)
            dgab_ref[rows, 0:D] = (dm * ya * (sa * (1.0 - sa))).astype(BF16)
            dgab_ref[rows, D:2 * D] = (dm * yb * (sb * (1.0 - sb))).astype(BF16)
            dub_ref[rows, :] = _dot_nt(dyb, wpa_ref[...])
            mbs.append(mb)
            dys.append(dy)
            dybs.append(dyb)
        loss_ref[...] += sq * (0.5 / D)
        dgp_ref[0:1, :] += dgp
        dwout_ref[...] += _dot_tn(jnp.concatenate(mbs, axis=0), jnp.concatenate(dys, axis=0))
        dwpa_ref[...] += _dot_tn(ub_ref[...], jnp.concatenate(dybs, axis=0))

    return pl.pallas_call(
        body, name="fwd_out_bwd_head", grid=(t // tm,),
        in_specs=[_row_spec(tm, D), _row_spec(tm, D), _row_spec(tm, D, 1), _row_spec(tm, D, 2),
                  _row_spec(tm, D), _row_spec(tm, D), pl.BlockSpec((1, D), lambda i: (0, 0)),
                  _whole_vmem(), _whole_vmem()],
        out_specs=[pl.BlockSpec((8, 128), lambda i: (0, 0)), _row_spec(tm, D), _row_spec(tm, D), _row_spec(tm, D),
                   _row_spec(tm, 2 * D), _whole_vmem(), _whole_vmem(), pl.BlockSpec((8, D), lambda i: (0, 0))],
        out_shape=[jax.ShapeDtypeStruct((8, 128), F32), jax.ShapeDtypeStruct((t, D), F32),
                   jax.ShapeDtypeStruct((t, D), BF16), jax.ShapeDtypeStruct((t, D), F32),
                   jax.ShapeDtypeStruct((t, 2 * D), BF16), jax.ShapeDtypeStruct((D, D), F32),
                   jax.ShapeDtypeStruct((D, D), F32), jax.ShapeDtypeStruct((8, D), F32)],
        compiler_params=_params("arbitrary", vmem_limit_bytes=V7X_VMEM_BYTES - (2 << 20)),
    )(ya, ub, g3, g3, x, target, g_post, wpa, wout)


def _bwd_attn(sinks, q, kv, attn, dub, g3, cos_t, sin_t, wt):
    t = q.shape[0]
    tq = ATTN_BLOCKS * BLK
    seq_blocks = SEQ_LEN // BLK

    def body(sink_ref, q_ref, kvc_ref, kvp_ref, attn_ref, dub_ref, za_ref, c_ref, s_ref, w_ref,
             dq_ref, dza_ref, dkv_own_ref, dkv_prev_ref, dsink_ref, dh_ref):
        @pl.when(pl.program_id(0) == 0)
        def _():
            dsink_ref[...] = jnp.zeros_like(dsink_ref)

        lo = lax.broadcasted_iota(jnp.int32, (STACK, 128), 1) < 64
        lane8 = lax.broadcasted_iota(jnp.int32, (8, 128), 1)
        lo2 = lax.broadcasted_iota(jnp.int32, (2 * BLK, 128), 1) < 64
        sink_row = lax.broadcasted_iota(jnp.int32, (2 * BLK, 128), 0) == 0
        dsink = jnp.zeros((8, 128), F32)
        for b in range(ATTN_BLOCKS):
            rows = slice(b * BLK, (b + 1) * BLK)
            valid = _band_mask((pl.program_id(0) * ATTN_BLOCKS + b) % seq_blocks == 0)
            k_pad = _padded_pair(*_kv_blocks(kvc_ref, kvp_ref, b, 0))
            v_pad = _padded_pair(*_kv_blocks(kvc_ref, kvp_ref, b, 128))
            c = c_ref[rows, :]
            s = s_ref[rows, :]
            dk_acc, dv_acc = [], []
            for g in range(2):
                qg, dog = [], []
                for j in range(4 * g, 4 * g + 4):
                    cols = slice(j * 128, (j + 1) * 128)
                    za = za_ref[rows, cols]
                    sg = _sig(za)
                    dub = dub_ref[rows, cols]
                    dza_ref[rows, cols] = (dub * attn_ref[rows, cols] * (sg * (1.0 + za * (1.0 - sg)))).astype(BF16)
                    dog.append((dub * (za * sg)).astype(BF16))
                    qg.append(q_ref[rows, cols])
                qg = jnp.concatenate(qg, axis=0)
                dog = jnp.concatenate(dog, axis=0)
                dq = jnp.zeros((STACK, 128), F32)
                ds_both, p_both = [], []
                for e in range(2):
                    p = _exp_logits(_dot_nt(qg, k_pad[e][g]), valid, _masked_fill(sink_ref, g, e))
                    p = p / jnp.sum(p, axis=-1, keepdims=True)
                    dp = _dot_nt(dog, v_pad[e][g])
                    ds = p * (dp - jnp.sum(p * dp, axis=-1, keepdims=True))
                    for jj in range(4):
                        tot = jnp.sum(ds[jj * BLK:(jj + 1) * BLK, 0:1], axis=0, keepdims=True)
                        dsink = dsink + jnp.where(lane8 == 2 * (4 * g + jj) + e, tot, 0.0)
                    ds = ds.astype(BF16)
                    dq = dq + _dot(ds, k_pad[e][g])
                    ds_both.append(ds)
                    p_both.append(p.astype(BF16))
                zero = jnp.zeros_like(qg)
                q2 = jnp.concatenate([jnp.where(lo, qg, zero), jnp.where(lo, zero, qg)], axis=0)
                do2 = jnp.concatenate([jnp.where(lo, dog, zero), jnp.where(lo, zero, dog)], axis=0)
                dk_acc.append(_dot_tn(jnp.concatenate(ds_both, axis=0), q2))
                dv_acc.append(_dot_tn(jnp.concatenate(p_both, axis=0), do2))
                for jj in range(4):
                    cols = slice((4 * g + jj) * 128, (4 * g + jj + 1) * 128)
                    dqj = dq[jj * BLK:(jj + 1) * BLK, :] * LOGIT_SCALE
                    dq_ref[rows, cols] = (dqj * c - _swap_halves(dqj) * s).astype(BF16)
            for col, acc in ((0, dk_acc), (128, dv_acc)):
                both = jnp.where(lo2, acc[0] + pltpu.roll(acc[0], 64, 1), acc[1] + pltpu.roll(acc[1], 64, 1))
                both = jnp.where(sink_row, 0.0, both)
                dkv_prev_ref[rows, col:col + 128] = both[0:BLK, :]
                dkv_own_ref[rows, col:col + 128] = both[BLK:2 * BLK, :]
        dsink_ref[...] += dsink
        dh_ref[...] = _dot(dq_ref[...], w_ref[ROW_Q:ROW_KV, :]) + _dot(dza_ref[...], w_ref[ROW_ZA:ROW_GA, :])

    tab = pl.BlockSpec((tq, 128), lambda i: (i % (SEQ_LEN // tq), 0))
    return pl.pallas_call(
        body, name="bwd_attn", grid=(t // tq,),
        in_specs=[pl.BlockSpec(memory_space=pltpu.SMEM), _row_spec(tq, D), _row_spec(tq, 256),
                  pl.BlockSpec((BLK, 256), lambda i: (jnp.maximum(i * ATTN_BLOCKS - 1, 0), 0)),
                  _row_spec(tq, D), _row_spec(tq, D), _row_spec(tq, D, 0), tab, tab, _whole_vmem()],
        out_specs=[_row_spec(tq, D), _row_spec(tq, D), _row_spec(tq, 256), _row_spec(tq, 256),
                   pl.BlockSpec((8, 128), lambda i: (0, 0)), _row_spec(tq, D)],
        out_shape=[jax.ShapeDtypeStruct((t, D), BF16), jax.ShapeDtypeStruct((t, D), BF16),
                   jax.ShapeDtypeStruct((t, 256), F32), jax.ShapeDtypeStruct((t, 256), F32),
                   jax.ShapeDtypeStruct((8, 128), F32), jax.ShapeDtypeStruct((t, D), F32)],
        compiler_params=_params("arbitrary"),
    )(sinks, q, kv, kv, attn, dub, g3, cos_t, sin_t, wt)


def _bwd_kv_finish(dkv_own, dkv_prev, cos_t, sin_t):
    t = dkv_own.shape[0]
    tm = 512
    seq_tiles = SEQ_LEN // tm
    n_blocks = t // BLK

    def body(own_ref, same_ref, nxt_ref, c_ref, s_ref, out_ref):
        keep = jnp.where(pl.program_id(0) % seq_tiles == seq_tiles - 1, 0.0, 1.0)
        shifted = jnp.concatenate([same_ref[BLK:tm, :], nxt_ref[...] * keep], axis=0)
        tot = own_ref[...] + shifted
        dk = tot[:, 0:128]
        out_ref[:, 0:128] = (dk * c_ref[...] - _swap_halves(dk) * s_ref[...]).astype(BF16)
        out_ref[:, 128:256] = tot[:, 128:256].astype(BF16)

    tab = pl.BlockSpec((tm, 128), lambda i: (i % seq_tiles, 0))
    return pl.pallas_call(
        body, name="bwd_kv_finish", grid=(t // tm,),
        in_specs=[_row_spec(tm, 256), _row_spec(tm, 256),
                  pl.BlockSpec((BLK, 256), lambda i: (jnp.minimum((i + 1) * (tm // BLK), n_blocks - 1), 0)), tab, tab],
        out_specs=_row_spec(tm, 256),
        out_shape=jax.ShapeDtypeStruct((t, 256), BF16),
        compiler_params=_params("parallel"),
    )(dkv_own, dkv_prev, dkv_prev, cos_t, sin_t)


STAGE_ROWS = 256


def _bwd_conv(dya, a4, h, wconv8, wpc, tm, parts):
    t = a4.shape[0]
    n_t = t // tm
    sub = tm // parts
    seq_tiles = SEQ_LEN // tm

    def body(dya_ref, xc_ref, bg_ref, cg_ref, zc_ref, xcp_ref, cgp_ref, w_ref, wpc_ref, h_ref,
             da4_ref, dwpc_ref, dwc_ref, o32_ref, o16_ref, acc_ref, stage_ref, later_ref, sems):
        step = pl.program_id(0)
        tile = n_t - 1 - step

        @pl.when(step == 0)
        def _():
            dwpc_ref[...] = jnp.zeros_like(dwpc_ref)
            dwc_ref[...] = jnp.zeros_like(dwc_ref)
            acc_ref[...] = jnp.zeros_like(acc_ref)

        keep_prev = jnp.where(tile % seq_tiles == 0, 0.0, 1.0)
        ends_sequence = tile % seq_tiles == seq_tiles - 1

        def part(p, later):
            r0 = p * sub
            here = slice(r0, r0 + sub)
            if p == 0:
                u_prev = cgp_ref[14:16, :].astype(F32) * xcp_ref[14:16, :].astype(F32) * keep_prev
            else:
                u_prev = cg_ref[r0 - 2:r0, :].astype(F32) * xc_ref[r0 - 2:r0, :].astype(F32)
            xc = xc_ref[here, :].astype(F32)
            bg = bg_ref[here, :].astype(F32)
            cg = cg_ref[here, :].astype(F32)
            zc = zc_ref[here, :].astype(F32)
            u, u_m1, u_m2, yconv, sg, sz, co = _conv_forward(xc, bg, cg, zc, u_prev[0:1, :], u_prev[1:2, :], w_ref)
            ua = (sz * co).astype(BF16)
            dua = _dot_nt(dya_ref[here, :], wpc_ref[...])
            da4_ref[here, 3 * D:4 * D] = (dua * co * (sg * (1.0 + zc * (1.0 - sg)))).astype(BF16)
            dco = dua * sz
            da4_ref[here, D:2 * D] = (dco * yconv).astype(BF16)
            dyc = dco * bg
            dwc = jnp.concatenate([jnp.sum(dyc * s, axis=0, keepdims=True) for s in (u_m2, u_m1, u)], axis=0)
            rows = lax.broadcasted_iota(jnp.int32, xc.shape, 0)
            n0 = later[0:1, :]
            n1 = later[1:2, :]
            dyc_p1 = jnp.where(rows == sub - 1, n0, pltpu.roll(dyc, sub - 1, 0))
            dyc_p2 = jnp.where(rows == sub - 2, n0, jnp.where(rows == sub - 1, n1, pltpu.roll(dyc, sub - 2, 0)))
            du = w_ref[2:3, :] * dyc + w_ref[1:2, :] * dyc_p1 + w_ref[0:1, :] * dyc_p2
            da4_ref[here, 0:D] = (du * cg).astype(BF16)
            da4_ref[here, 2 * D:3 * D] = (du * xc).astype(BF16)
            return ua, dwc, dyc[0:8, :]

        later = jnp.where(ends_sequence, 0.0, later_ref[...])
        uas, dwc = [], jnp.zeros((3, D), F32)
        for p in reversed(range(parts)):
            ua, dwc_p, later = part(p, later)
            uas.insert(0, ua)
            dwc = dwc + dwc_p
        later_ref[...] = later
        dwpc_ref[...] += _dot_tn(jnp.concatenate(uas, axis=0), dya_ref[...])
        dwc_ref[0:3, :] += dwc
        for j in range(4):
            acc_ref[j * D:(j + 1) * D, :] += _dot_tn(da4_ref[:, j * D:(j + 1) * D], h_ref[...])

        @pl.when(step == n_t - 1)
        def _():
            c32 = pltpu.make_async_copy(acc_ref, o32_ref.at[pl.ds(0, 4 * D)], sems.at[0])
            c32.start()
            for j in range(4 * D // STAGE_ROWS):
                rows = pl.ds(j * STAGE_ROWS, STAGE_ROWS)
                stage_ref[...] = acc_ref[rows, :].astype(BF16)
                c16 = pltpu.make_async_copy(stage_ref, o16_ref.at[rows], sems.at[1])
                c16.start()
                c16.wait()
            c32.wait()

    def rows_of_tile(width, col=0):
        return pl.BlockSpec((tm, width), lambda s: (n_t - 1 - s, col))

    def prev(col):
        return pl.BlockSpec((16, D), lambda s: (jnp.maximum((n_t - 1 - s) * (tm // 16) - 1, 0), col))

    hbm = pl.BlockSpec(memory_space=pl.ANY)
    out = pl.pallas_call(
        body, name="bwd_conv", grid=(n_t,),
        in_specs=[rows_of_tile(D), rows_of_tile(D, 0), rows_of_tile(D, 1), rows_of_tile(D, 2), rows_of_tile(D, 3),
                  prev(0), prev(2), pl.BlockSpec((8, D), lambda s: (0, 0)), _whole_vmem(), rows_of_tile(D)],
        out_specs=[rows_of_tile(4 * D), _whole_vmem(), pl.BlockSpec((8, D), lambda s: (0, 0)), hbm, hbm],
        out_shape=[jax.ShapeDtypeStruct((t, 4 * D), BF16), jax.ShapeDtypeStruct((D, D), F32),
                   jax.ShapeDtypeStruct((8, D), F32), jax.ShapeDtypeStruct((D_IN, D), F32),
                   jax.ShapeDtypeStruct((D_IN, D), BF16)],
        scratch_shapes=[pltpu.VMEM((4 * D, D), F32), pltpu.VMEM((STAGE_ROWS, D), BF16), pltpu.VMEM((8, D), F32),
                        pltpu.SemaphoreType.DMA((2,))],
        compiler_params=pltpu.CompilerParams(dimension_semantics=("arbitrary",), vmem_limit_bytes=V7X_VMEM_BYTES - (2 << 20)),
    )(dya, a4, a4, a4, a4, a4, a4, wconv8, wpc, h)
    return out[0], out[1], out[2], (out[3], out[4])


def _bwd_dh(da4, dh_part, dkv, dgab, wt, x, g_pre, dout, tm):
    t = x.shape[0]

    def body(da4_ref, dhp_ref, dkv_ref, dgab_ref, w_ref, x_ref, g_ref, dout_ref, gx_ref, dg_ref):
        @pl.when(pl.program_id(0) == 0)
        def _():
            dg_ref[...] = jnp.zeros_like(dg_ref)

        dh = dhp_ref[...] + _dot(da4_ref[...], w_ref[0:ROW_Q, :])
        dh += _dot(dkv_ref[...], w_ref[ROW_KV:ROW_ZA, :])
        dh += _dot(dgab_ref[...], w_ref[ROW_GA:D_IN, :])
        xf = x_ref[...]
        r = lax.rsqrt(jnp.mean(xf * xf, axis=-1, keepdims=True) + RMS_EPS)
        xn = xf * r
        dg_ref[0:1, :] += jnp.sum(dh * xn, axis=0, keepdims=True)
        dxn = dh * g_ref[...]
        gx_ref[...] = dout_ref[...] + r * (dxn - xn * jnp.mean(dxn * xn, axis=-1, keepdims=True))

    return pl.pallas_call(
        body, name="bwd_dh", grid=(t // tm,),
        in_specs=[_row_spec(tm, 4 * D), _row_spec(tm, D), _row_spec(tm, 256), _row_spec(tm, 2 * D),
                  _whole_vmem(), _row_spec(tm, D), pl.BlockSpec((1, D), lambda i: (0, 0)), _row_spec(tm, D)],
        out_specs=[_row_spec(tm, D), pl.BlockSpec((8, D), lambda i: (0, 0))],
        out_shape=[jax.ShapeDtypeStruct((t, D), F32), jax.ShapeDtypeStruct((8, D), F32)],
        compiler_params=_params("arbitrary"),
    )(da4, dh_part, dkv, dgab, wt, x, g_pre, dout)


def _bwd_dw_in(h, piece, row0, nb, tm, name, prev):
    t, n = piece.shape
    n_t = t // tm

    def body(*refs):
        h_ref, p_ref = refs[0], refs[1]
        o32_ref, o16_ref, acc_ref, acc16_ref, sems = refs[-5:]
        j, i = pl.program_id(0), pl.program_id(1)

        @pl.when(i == 0)
        def _():
            acc_ref[...] = jnp.zeros_like(acc_ref)

        acc_ref[...] += _dot_tn(p_ref[...], h_ref[...])

        @pl.when(i == n_t - 1)
        def _():
            acc16_ref[...] = acc_ref[...].astype(BF16)
            rows = pl.ds(pl.multiple_of(row0 + j * nb, 16), nb)
            c32 = pltpu.make_async_copy(acc_ref, o32_ref.at[rows], sems.at[0])
            c16 = pltpu.make_async_copy(acc16_ref, o16_ref.at[rows], sems.at[1])
            c32.start()
            c16.start()
            c32.wait()
            c16.wait()

    hbm = pl.BlockSpec(memory_space=pl.ANY)
    carried = [] if prev is None else list(prev)
    return pl.pallas_call(
        body, name=name, grid=(n // nb, n_t),
        in_specs=[pl.BlockSpec((tm, D), lambda j, i: (i, 0)), pl.BlockSpec((tm, nb), lambda j, i: (i, j))]
        + [hbm] * len(carried),
        out_specs=[hbm, hbm],
        out_shape=[jax.ShapeDtypeStruct((D_IN, D), F32), jax.ShapeDtypeStruct((D_IN, D), BF16)],
        scratch_shapes=[pltpu.VMEM((nb, D), F32), pltpu.VMEM((nb, D), BF16), pltpu.SemaphoreType.DMA((2,))],
        input_output_aliases={2: 0, 3: 1} if carried else {},
        compiler_params=_params("arbitrary", "arbitrary"),
    )(h, piece, *carried)


def _place():
    x, y, c = lax.axis_index("x"), lax.axis_index("y"), lax.axis_index("c")
    return x, y, c, 4 * x + 2 * y + c


def _peer(x, y, c, k):
    return (1 - x if k & 4 else x, 1 - y if k & 2 else y, 1 - c if k & 1 else c)


ICI_MASKS = (4, 2, 6)


def _all_gather(shards):
    n = len(shards)

    def body(*refs):
        src, dst = refs[:n], refs[n:2 * n]
        send_sems, recv_sems, local_sems = refs[2 * n:]
        x, y, c, me = _place()
        sibling = _peer(x, y, c, 1)

        def copy(a, s, block, to, own=False):
            return pltpu.make_async_remote_copy(
                src_ref=src[a] if own else dst[a].at[block], dst_ref=dst[a].at[block],
                send_sem=send_sems.at[a * 7 + s], recv_sem=recv_sems.at[a * 7 + s], device_id=to, device_id_type=MESH_ID)

        local = [pltpu.make_async_copy(src[a], dst[a].at[me], local_sems.at[a]) for a in range(n)]
        for cp in local:
            cp.start()
        started = [copy(a, 0, me, sibling, own=True) for a in range(n)]
        started += [copy(a, 1 + j, me, _peer(x, y, c, k), own=True) for j, k in enumerate(ICI_MASKS) for a in range(n)]
        for cp in started:
            cp.start()
        for j, k in enumerate(ICI_MASKS):
            for a in range(n):
                copy(a, 1 + j, me ^ k, sibling).wait_recv()
                fwd = copy(a, 4 + j, me ^ k, sibling)
                fwd.start()
                started.append(fwd)
        for a in range(n):
            copy(a, 0, me ^ 1, sibling).wait_recv()
        for j, k in enumerate(ICI_MASKS):
            for a in range(n):
                copy(a, 4 + j, me ^ 1 ^ k, sibling).wait_recv()
        for cp in started:
            cp.wait_send()
        for cp in local:
            cp.wait()

    hbm = pl.BlockSpec(memory_space=pl.ANY)
    return pl.pallas_call(
        body, name="all_gather_weights",
        in_specs=[hbm] * n, out_specs=[hbm] * n,
        out_shape=[jax.ShapeDtypeStruct((N_DEV,) + s.shape, s.dtype) for s in shards],
        scratch_shapes=[pltpu.SemaphoreType.DMA((7 * n,)), pltpu.SemaphoreType.DMA((7 * n,)),
                        pltpu.SemaphoreType.DMA((n,))],
    )(*shards)


def _direct_copies(src, land, send_sems, recv_sems):
    x, y, c, me = _place()
    return [pltpu.make_async_remote_copy(
        src_ref=src[a], dst_ref=land[a].at[me], send_sem=send_sems.at[a * 7 + k - 1],
        recv_sem=recv_sems.at[a * 7 + k - 1], device_id=_peer(x, y, c, k), device_id_type=MESH_ID)
        for k in range(1, N_DEV) for a in range(len(src))]


def _gather_start(shards, name):
    n = len(shards)

    def body(*refs):
        src, land = refs[:n], refs[n:2 * n]
        send_sems, recv_sems = refs[2 * n], refs[2 * n + 1]
        token_ref = refs[-1]
        for cp in _direct_copies(src, land, send_sems, recv_sems):
            cp.start()
        token_ref[...] = jnp.zeros_like(token_ref)

    hbm = pl.BlockSpec(memory_space=pltpu.HBM)
    sem = pl.BlockSpec(memory_space=pltpu.SEMAPHORE)
    lands = [lax.empty((N_DEV,) + s.shape, s.dtype) for s in shards]
    out = pl.pallas_call(
        body, name=name + "_start",
        out_shape=(pltpu.SemaphoreType.DMA((7 * n,)), pltpu.SemaphoreType.DMA((7 * n,)),
                   *[pltpu.HBM(s.shape, s.dtype) for s in shards], *[pltpu.HBM(s.shape, s.dtype) for s in lands],
                   jax.ShapeDtypeStruct((8, 128), F32)),
        in_specs=[hbm] * (2 * n), out_specs=(sem, sem, *[hbm] * (2 * n), _whole_vmem()),
        input_output_aliases={a: 2 + a for a in range(2 * n)},
        compiler_params=pltpu.CompilerParams(has_side_effects=pltpu.SideEffectType.DATAFLOW_SIDE_EFFECTING),
    )(*[pltpu.with_memory_space_constraint(s, pltpu.HBM) for s in list(shards) + lands])
    return out[0], out[1], out[2:2 + n], out[2 + n:2 + 2 * n], out[-1]


def _gather_wait(send_sems, recv_sems, flying, lands, after, name):
    n = len(flying)

    def body(*refs):
        src, land = refs[:n], refs[n:2 * n]
        for cp in _direct_copies(src, land, refs[2 * n], refs[2 * n + 1]):
            cp.wait_send()
            cp.wait_recv()

    hbm = pl.BlockSpec(memory_space=pltpu.HBM)
    sem = pl.BlockSpec(memory_space=pltpu.SEMAPHORE)
    out = pl.pallas_call(
        body, name=name + "_wait",
        out_shape=tuple(pltpu.HBM(s.shape, s.dtype) for s in list(flying) + list(lands)),
        in_specs=[hbm] * (2 * n) + [sem, sem, pl.BlockSpec(memory_space=pl.ANY)], out_specs=tuple([hbm] * (2 * n)),
        input_output_aliases={a: a for a in range(2 * n)},
        compiler_params=pltpu.CompilerParams(has_side_effects=pltpu.SideEffectType.DATAFLOW_SIDE_EFFECTING),
    )(*flying, *lands, send_sems, recv_sems, after)
    return out[n:]


def _exchange_sibling(by_dest):
    n = len(by_dest)

    def body(*refs):
        src, dst = refs[:n], refs[n:2 * n]
        send_sems, recv_sems = refs[2 * n:]
        x, y, c, _ = _place()
        sibling = _peer(x, y, c, 1)
        copies = [pltpu.make_async_remote_copy(
            src_ref=src[a].at[2 * p + (1 - c)], dst_ref=dst[a].at[p], send_sem=send_sems.at[a * 4 + p],
            recv_sem=recv_sems.at[a * 4 + p], device_id=sibling, device_id_type=MESH_ID)
            for a in range(n) for p in range(4)]
        for cp in copies:
            cp.start()
        for cp in copies:
            cp.wait_recv()
        for cp in copies:
            cp.wait_send()

    hbm = pl.BlockSpec(memory_space=pl.ANY)
    return pl.pallas_call(
        body, name="exchange_sibling", in_specs=[hbm] * n, out_specs=[hbm] * n,
        out_shape=[jax.ShapeDtypeStruct((4,) + s.shape[1:], s.dtype) for s in by_dest],
        scratch_shapes=[pltpu.SemaphoreType.DMA((4 * n,)), pltpu.SemaphoreType.DMA((4 * n,))],
    )(*by_dest)


def _chip_copies(src, land, send_sems, recv_sems):
    x, y, c, _ = _place()
    chip = 2 * x + y
    return [pltpu.make_async_remote_copy(
        src_ref=src[a].at[chip ^ (k >> 1)], dst_ref=land[a].at[j], send_sem=send_sems.at[a * 3 + j],
        recv_sem=recv_sems.at[a * 3 + j], device_id=_peer(x, y, c, k), device_id_type=MESH_ID)
        for j, k in enumerate(ICI_MASKS) for a in range(len(src))]


def _exchange_chips_start(by_chip):
    n = len(by_chip)

    def body(*refs):
        src, land = refs[:n], refs[n:2 * n]
        send_sems, recv_sems = refs[2 * n], refs[2 * n + 1]
        token_ref = refs[-1]
        for cp in _chip_copies(src, land, send_sems, recv_sems):
            cp.start()
        token_ref[...] = jnp.zeros_like(token_ref)

    hbm = pl.BlockSpec(memory_space=pltpu.HBM)
    sem = pl.BlockSpec(memory_space=pltpu.SEMAPHORE)
    lands = [lax.empty((3,) + s.shape[1:], s.dtype) for s in by_chip]
    out = pl.pallas_call(
        body, name="exchange_chips_start",
        out_shape=(pltpu.SemaphoreType.DMA((3 * n,)), pltpu.SemaphoreType.DMA((3 * n,)),
                   *[pltpu.HBM(s.shape, s.dtype) for s in by_chip], *[pltpu.HBM(s.shape, s.dtype) for s in lands],
                   jax.ShapeDtypeStruct((8, 128), F32)),
        in_specs=[hbm] * (2 * n), out_specs=(sem, sem, *[hbm] * (2 * n), _whole_vmem()),
        input_output_aliases={a: 2 + a for a in range(2 * n)},
        compiler_params=pltpu.CompilerParams(has_side_effects=pltpu.SideEffectType.DATAFLOW_SIDE_EFFECTING),
    )(*[pltpu.with_memory_space_constraint(s, pltpu.HBM) for s in list(by_chip) + lands])
    return out[0], out[1], out[2:2 + n], out[2 + n:2 + 2 * n], out[-1]


def _exchange_chips_wait(send_sems, recv_sems, flying, lands, after):
    n = len(flying)

    def body(*refs):
        src, land = refs[:n], refs[n:2 * n]
        send_sems_ref, recv_sems_ref = refs[2 * n], refs[2 * n + 1]
        for cp in _chip_copies(src, land, send_sems_ref, recv_sems_ref):
            cp.wait_send()
            cp.wait_recv()

    hbm = pl.BlockSpec(memory_space=pltpu.HBM)
    sem = pl.BlockSpec(memory_space=pltpu.SEMAPHORE)
    out = pl.pallas_call(
        body, name="exchange_chips_wait",
        out_shape=tuple(pltpu.HBM(s.shape, s.dtype) for s in list(flying) + list(lands)),
        in_specs=[hbm] * (2 * n) + [sem, sem, pl.BlockSpec(memory_space=pl.ANY)], out_specs=tuple([hbm] * (2 * n)),
        input_output_aliases={a: a for a in range(2 * n)},
        compiler_params=pltpu.CompilerParams(has_side_effects=pltpu.SideEffectType.DATAFLOW_SIDE_EFFECTING),
    )(*flying, *lands, send_sems, recv_sems, after)
    return out[n:]


def _adamw_math(w, g, m, v):
    m = ADAM_B1 * m + (1.0 - ADAM_B1) * g
    v = ADAM_B2 * v + (1.0 - ADAM_B2) * (g * g)
    m_hat = m / (1.0 - ADAM_B1 ** ADAM_STEP)
    v_hat = v / (1.0 - ADAM_B2 ** ADAM_STEP)
    return -ADAM_LR * (m_hat / (jnp.sqrt(v_hat) + ADAM_EPS) + ADAM_WD * w), m, v


def _pair_sum(owns, recvs, c_arr, tr, name):
    n = len(owns)
    _, rows, cols = owns[0].shape

    def body(c_ref, *refs):
        for a in range(n):
            s = refs[a][...] + refs[n + a][...].astype(F32)
            refs[2 * n + a][...] = s
            refs[3 * n + a][...] = s.astype(BF16)

    by_chip = pl.BlockSpec((None, tr, cols), lambda p, i, c_ref: (p, i, 0))
    mine = pl.BlockSpec((None, tr, cols), lambda p, i, c_ref: (2 * p + c_ref[0], i, 0))
    out = pl.pallas_call(
        body, name=name,
        grid_spec=pltpu.PrefetchScalarGridSpec(
            num_scalar_prefetch=1, grid=(4, rows // tr), in_specs=[mine] * n + [by_chip] * n, out_specs=[by_chip] * (2 * n)),
        out_shape=[jax.ShapeDtypeStruct((4, rows, cols), F32)] * n + [jax.ShapeDtypeStruct((4, rows, cols), BF16)] * n,
        compiler_params=_params("parallel", "parallel"),
    )(c_arr, *owns, *recvs)
    return out[:n], out[n:]


def _chip_sum(pairs, recvs, chip_arr, tr, name, adam=None):
    n = len(pairs)
    _, rows, cols = pairs[0].shape
    n_state = 0 if adam is None else 3 * n

    def body(chip_ref, *refs):
        outs = refs[2 * n + n_state:]
        for a in range(n):
            g = refs[a][...]
            for j in range(3):
                g = g + refs[n + a][j].astype(F32)
            outs[a][...] = g
            if adam is not None:
                w_ref, m_ref, v_ref = (refs[2 * n + s * n + a] for s in range(3))
                outs[n + a][...], outs[2 * n + a][...], outs[3 * n + a][...] = _adamw_math(w_ref[...], g, m_ref[...], v_ref[...])

    blk = pl.BlockSpec((tr, cols), lambda i, chip_ref: (i, 0))
    n_out = n if adam is None else 4 * n
    out = pl.pallas_call(
        body, name=name,
        grid_spec=pltpu.PrefetchScalarGridSpec(
            num_scalar_prefetch=1, grid=(rows // tr,),
            in_specs=[pl.BlockSpec((None, tr, cols), lambda i, chip_ref: (chip_ref[0], i, 0))] * n
            + [pl.BlockSpec((3, tr, cols), lambda i, chip_ref: (0, i, 0))] * n + [blk] * n_state,
            out_specs=[blk] * n_out),
        out_shape=[jax.ShapeDtypeStruct((rows, cols), F32)] * n_out,
        compiler_params=_params("parallel"),
    )(chip_arr, *pairs, *recvs, *([] if adam is None else [t for group in adam for t in group]))
    return out if adam is None else (out[:n], out[n:2 * n], out[2 * n:3 * n], out[3 * n:])


def _adamw(ws, gs, ms, vs, name):
    n = len(ws)

    def body(*refs):
        for a in range(n):
            w_ref, g_ref, m_ref, v_ref = (refs[s * n + a] for s in range(4))
            refs[4 * n + a][...], refs[5 * n + a][...], refs[6 * n + a][...] = _adamw_math(
                w_ref[...], g_ref[...], m_ref[...], v_ref[...])

    out = pl.pallas_call(body, name=name, out_shape=[jax.ShapeDtypeStruct(w.shape, F32) for w in ws] * 3)(
        *ws, *gs, *ms, *vs)
    return out[:n], out[n:2 * n], out[2 * n:]


def _sum_small(small_all):
    def body(s_ref, o_ref):
        g = s_ref[0]
        for d in range(1, N_DEV):
            g = g + s_ref[d]
        o_ref[...] = g

    return pl.pallas_call(body, name="sum_small", out_shape=jax.ShapeDtypeStruct(small_all.shape[1:], F32))(small_all)


def _rope_tables():
    inv_freq = ROPE_THETA ** (-jnp.arange(0, HEAD_DIM, 2, dtype=F32) / HEAD_DIM)
    ang = jnp.arange(SEQ_LEN).astype(F32)[:, None] * inv_freq[None, :]
    cos, sin = jnp.cos(ang), jnp.sin(ang)
    return jnp.tile(cos, (1, 4)), jnp.tile(jnp.concatenate([-sin, sin], axis=1), (1, 2))


def _local_step(x, target, g_pre, g_post, sinks, wt, wconv, squares, start_exchange=None):
    cos_t, sin_t = _rope_tables()
    wconv8 = jnp.pad(wconv, ((0, 5), (0, 0)))
    h, q, kv, g3 = _fwd_in_attn(x, g_pre, wt, cos_t, sin_t, 512)
    wpc, wpa, wout = squares(kv)
    a4, ya = _fwd_in_conv(h, wt, wconv8, wpc, 512)
    attn, ub = _fwd_attn(sinks, q, kv, g3)
    loss8, dout, dya, dub, dgab, dwout, dwpa, dgpost8 = _fwd_out_bwd_head(ya, ub, g3, x, target, g_post, wpa, wout, 512)
    dq, dza, dkv_own, dkv_prev, dsink8, dh_part = _bwd_attn(sinks, q, kv, attn, dub, g3, cos_t, sin_t, wt)
    dkv = _bwd_kv_finish(dkv_own, dkv_prev, cos_t, sin_t)
    da4, dwpc, dwconv8, dwt = _bwd_conv(dya, a4, h, wconv8, wpc, 512, 2)
    dwt = _bwd_dw_in(h, dq, ROW_Q, 1024, 1024, "bwd_dw_in_q", dwt)
    dwt = _bwd_dw_in(h, dkv, ROW_KV, 256, 1024, "bwd_dw_in_kv", dwt)
    dwt = _bwd_dw_in(h, dza, ROW_ZA, 1024, 1024, "bwd_dw_in_za", dwt)
    dwt32, dwt16 = _bwd_dw_in(h, dgab, ROW_GA, 1024, 1024, "bwd_dw_in_gates", dwt)
    token, pending = (None, None) if start_exchange is None else start_exchange(dwt32, dwt16, dwpc, dwpa, dwout)
    g_pre_after = g_pre if token is None else g_pre + token[0:1, 0:1]
    grad_x, dgpre8 = _bwd_dh(da4, dh_part, dkv, dgab, wt, x, g_pre_after, dout, 512)
    small = jnp.concatenate([dgpre8, dgpost8, jnp.pad(dsink8, ((0, 0), (0, D - 128))), dwconv8,
                             jnp.pad(loss8, ((0, 0), (0, D - 128)))], axis=0)
    return loss8[0, 0], grad_x, dwt32, dwt16, dwpc, dwpa, dwout, small, pending


def kernel(x, g_pre, g_post, w_in, w_conv, sinks, w_proj_conv, w_proj_attn, w_out, loss_target, m_g_pre, m_g_post, m_w_in, m_w_conv, m_sinks, m_w_proj_conv, m_w_proj_attn, m_w_out, v_g_pre, v_g_post, v_w_in, v_w_conv, v_sinks, v_w_proj_conv, v_w_proj_attn, v_w_out):
    batch = x.shape[0]
    mx, my, mc, me = _place()
    c_arr = jnp.reshape(mc, (1,)).astype(jnp.int32)
    chip_arr = jnp.reshape(2 * mx + my, (1,)).astype(jnp.int32)

    g_wt, g_conv = _all_gather([w_in[0].T.astype(BF16), jnp.pad(w_conv[0], ((0, 5), (0, 0)))])
    wt = g_wt.reshape(D_IN, D)
    wconv = g_conv[:, 0:3, :].transpose(1, 0, 2).reshape(3, D)
    sq_mine = [w.astype(BF16) for w in (w_proj_conv[0], w_proj_attn[0], w_out[0])]
    wt, sq_mine = lax.optimization_barrier((wt, sq_mine))
    sq_send, sq_recv, sq_flying, sq_lands, sq_token = _gather_start(sq_mine, "gather_squares")

    def squares(after):
        got = _gather_wait(sq_send, sq_recv, sq_flying, sq_lands, after, "gather_squares")
        return [lax.dynamic_update_index_in_dim(full, mine, me, 0).reshape(D, D) for full, mine in zip(got, sq_mine)]

    def start_exchange(dwt32, dwt16, dwpc, dwpa, dwout):
        own_sq = [g.reshape(N_DEV, SHARD_SQ, D) for g in (dwpc, dwpa, dwout)]
        own_in = dwt32.reshape(N_DEV, SHARD_IN, D)
        from_sibling = _exchange_sibling([dwt16.reshape(N_DEV, SHARD_IN, D)] + [g.astype(BF16) for g in own_sq])
        in32, in16 = _pair_sum([own_in], from_sibling[:1], c_arr, SHARD_IN // 2, "pair_sum_w_in")
        sq32, sq16 = _pair_sum(own_sq, from_sibling[1:], c_arr, SHARD_SQ, "pair_sum_squares")
        send_sems, recv_sems, flying, lands, token = _exchange_chips_start(list(in16) + list(sq16))
        return token, (send_sems, recv_sems, flying, lands, in32, sq32)

    _, grad_x, _, _, _, _, _, small, pending = _local_step(
        x.reshape(batch * SEQ_LEN, D), loss_target.reshape(batch * SEQ_LEN, D), g_pre + sq_token[0:1, 0:1], g_post,
        sinks, wt, wconv, squares, start_exchange)
    sm_send, sm_recv, sm_flying, sm_lands, sm_token = _gather_start([small], "gather_small")
    send_sems, recv_sems, flying, lands, in32, sq32 = pending
    from_chips = _exchange_chips_wait(send_sems, recv_sems, flying, lands, sm_token)

    o_in = [o[0].T for o in _chip_sum(
        in32, from_chips[:1], chip_arr, SHARD_IN // 3, "chip_sum_adamw_w_in",
        adam=([w_in[0].T], [m_w_in[0].T], [v_w_in[0].T]))]
    g_in_mine, o_in = o_in[0], o_in[1:]
    g_sq, d_sq, m_sq, v_sq = _chip_sum(
        sq32, from_chips[1:], chip_arr, SHARD_SQ, "chip_sum_adamw_squares",
        adam=([w_proj_conv[0], w_proj_attn[0], w_out[0]], [m_w_proj_conv[0], m_w_proj_attn[0], m_w_out[0]],
              [v_w_proj_conv[0], v_w_proj_attn[0], v_w_out[0]]))
    both_done, g_in_mine = lax.optimization_barrier((d_sq[0], g_in_mine))
    (small_all,) = _gather_wait(sm_send, sm_recv, sm_flying, sm_lands, both_done, "gather_small")
    gs = _sum_small(lax.dynamic_update_index_in_dim(small_all, small, me, 0))
    g_g_pre, g_g_post, g_sinks, loss = gs[0:1], gs[8:9], gs[16:17, 0:N_HEADS], gs[32, 0]
    g_conv_mine = lax.dynamic_slice_in_dim(gs[24:27], me * SHARD_SQ, SHARD_SQ, axis=1)
    o_small = _adamw([g_pre, g_post, sinks, w_conv[0]], [g_g_pre, g_g_post, g_sinks, g_conv_mine],
                     [m_g_pre, m_g_post, m_sinks, m_w_conv[0]], [v_g_pre, v_g_post, v_sinks, v_w_conv[0]], "adamw_small")

    grads = [g_g_pre, g_g_post, g_in_mine[None], g_conv_mine[None], g_sinks] + [g[None] for g in g_sq]
    rest = []
    for idx, sq in enumerate((d_sq, m_sq, v_sq)):
        gp, gq, sk, cv = o_small[idx]
        rest += [gp, gq, o_in[idx][None], cv[None], sk] + [s[None] for s in sq]
    return (loss, grad_x.reshape(batch, SEQ_LEN, D), *grads, *rest)
```

```python
import functools

import jax
import jax.numpy as jnp
from jax import lax
from jax.experimental import pallas as pl
from jax.experimental.pallas import tpu as pltpu

D = 1024
N_HEADS = 16
HEAD_DIM = 64
LOGIT_SCALE = HEAD_DIM ** -0.5
BLK = 128
SEQ_LEN = 2048
D_IN = 8448
ROW_Q, ROW_KV, ROW_ZA, ROW_GA = 4 * D, 5 * D, 5 * D + 256, 6 * D + 256
SHARD_IN = D_IN // 8
SHARD_SQ = D // 8
N_DEV = 8
V7X_VMEM_BYTES = 64 << 20
ROPE_THETA = 10000.0
RMS_EPS = 1e-6
NEG = -1e30
ADAM_LR, ADAM_B1, ADAM_B2, ADAM_EPS, ADAM_WD, ADAM_STEP = 0.001, 0.9, 0.999, 1e-08, 0.01, 10

F32 = jnp.float32
BF16 = jnp.bfloat16
MESH_ID = pl.DeviceIdType.MESH


def _dot(a, b):
    return jnp.dot(a, b, preferred_element_type=F32)


def _dot_nt(a, b):
    return lax.dot_general(a, b, (((1,), (1,)), ((), ())), preferred_element_type=F32)


def _dot_tn(a, b):
    return lax.dot_general(a, b, (((0,), (0,)), ((), ())), preferred_element_type=F32)


def _sig(z):
    return 1.0 / (1.0 + jnp.exp(-z))


def _swap_halves(z):
    lane = lax.broadcasted_iota(jnp.int32, z.shape, 1)
    return jnp.where((lane & 63) < 32, pltpu.roll(z, 96, 1), pltpu.roll(z, 32, 1))


def _row_spec(tm, width, col=0):
    return pl.BlockSpec((tm, width), lambda i: (i, col))


def _whole_vmem():
    return pl.BlockSpec(memory_space=pltpu.VMEM)


def _params(*sem, vmem_limit_bytes=None):
    return pltpu.CompilerParams(dimension_semantics=sem, vmem_limit_bytes=vmem_limit_bytes)


def _fwd_in_attn(x, g_pre, wt, cos_t, sin_t, tm):
    t = x.shape[0]
    seq_tiles = SEQ_LEN // tm

    def body(x_ref, g_ref, w_ref, c_ref, s_ref, h_ref, q_ref, kv_ref, g3_ref):
        xf = x_ref[...]
        r = lax.rsqrt(jnp.mean(xf * xf, axis=-1, keepdims=True) + RMS_EPS)
        hh = ((xf * r) * g_ref[...]).astype(BF16)
        h_ref[...] = hh
        c = c_ref[...]
        s = s_ref[...]

        def rope(z):
            return z * c + _swap_halves(z) * s

        q = _dot_nt(hh, w_ref[ROW_Q:ROW_Q + D, :])
        for j in range(D // 128):
            q_ref[:, j * 128:(j + 1) * 128] = (rope(q[:, j * 128:(j + 1) * 128]) * LOGIT_SCALE).astype(BF16)
        kv = _dot_nt(hh, w_ref[ROW_KV:ROW_KV + 256, :])
        kv_ref[:, 0:128] = rope(kv[:, 0:128]).astype(BF16)
        kv_ref[:, 128:256] = kv[:, 128:256].astype(BF16)
        for j in range(3):
            g3_ref[:, j * D:(j + 1) * D] = _dot_nt(hh, w_ref[ROW_ZA + j * D:ROW_ZA + (j + 1) * D, :])

    tab = pl.BlockSpec((tm, 128), lambda i: (i % seq_tiles, 0))
    return pl.pallas_call(
        body, name="fwd_in_attn", grid=(t // tm,),
        in_specs=[_row_spec(tm, D), pl.BlockSpec((1, D), lambda i: (0, 0)), _whole_vmem(), tab, tab],
        out_specs=[_row_spec(tm, D), _row_spec(tm, D), _row_spec(tm, 256), _row_spec(tm, 3 * D)],
        out_shape=[jax.ShapeDtypeStruct((t, D), BF16), jax.ShapeDtypeStruct((t, D), BF16),
                   jax.ShapeDtypeStruct((t, 256), BF16), jax.ShapeDtypeStruct((t, 3 * D), F32)],
        compiler_params=_params("parallel"),
    )(x, g_pre, wt, cos_t, sin_t)


def _conv_forward(xc, bg, cg, zc, up6, up7, w_ref):
    tm = xc.shape[0]
    rows = lax.broadcasted_iota(jnp.int32, xc.shape, 0)
    u = cg * xc
    u_m1 = jnp.where(rows == 0, up7, pltpu.roll(u, 1, 0))
    u_m2 = jnp.where(rows == 0, up6, jnp.where(rows == 1, up7, pltpu.roll(u, 2, 0)))
    yconv = w_ref[0:1, :] * u_m2 + w_ref[1:2, :] * u_m1 + w_ref[2:3, :] * u
    sg = _sig(zc)
    sz = zc * sg
    co = bg * yconv
    del tm
    return u, u_m1, u_m2, yconv, sg, sz, co


def _fwd_in_conv(h, wt, wconv8, wpc, tm):
    t = h.shape[0]
    seq_tiles = SEQ_LEN // tm

    def body(h_ref, w_ref, wc_ref, wpc_ref, a4_ref, ya_ref, last_u_ref):
        hh = h_ref[...]
        xc, bg, cg, zc = (_dot_nt(hh, w_ref[j * D:(j + 1) * D, :]) for j in range(4))
        for j, z in enumerate((xc, bg, cg, zc)):
            a4_ref[:, j * D:(j + 1) * D] = z.astype(BF16)
        first = pl.program_id(0) % seq_tiles == 0
        up6 = jnp.where(first, 0.0, last_u_ref[6:7, :])
        up7 = jnp.where(first, 0.0, last_u_ref[7:8, :])
        u, _, _, _, _, sz, co = _conv_forward(xc, bg, cg, zc, up6, up7, wc_ref)
        last_u_ref[...] = u[tm - 8:tm, :]
        ya_ref[...] = _dot((sz * co).astype(BF16), wpc_ref[...])

    return pl.pallas_call(
        body, name="fwd_in_conv", grid=(t // tm,),
        in_specs=[_row_spec(tm, D), _whole_vmem(), pl.BlockSpec((8, D), lambda i: (0, 0)), _whole_vmem()],
        out_specs=[_row_spec(tm, 4 * D), _row_spec(tm, D)],
        out_shape=[jax.ShapeDtypeStruct((t, 4 * D), BF16), jax.ShapeDtypeStruct((t, D), F32)],
        scratch_shapes=[pltpu.VMEM((8, D), F32)],
        compiler_params=_params("arbitrary"),
    )(h, wt, wconv8, wpc)


STACK = 4 * BLK


def _band_mask(first):
    qi = lax.broadcasted_iota(jnp.int32, (STACK, 2 * BLK), 0) & (BLK - 1)
    kj = lax.broadcasted_iota(jnp.int32, (STACK, 2 * BLK), 1)
    return (kj > qi) & (kj <= qi + BLK) & (kj >= jnp.where(first, BLK, 0))


def _masked_fill(sink_ref, g, e):
    kj = lax.broadcasted_iota(jnp.int32, (STACK, 2 * BLK), 1)
    sink = jnp.concatenate([jnp.full((BLK, 2 * BLK), sink_ref[0, 2 * (4 * g + jj) + e], F32) for jj in range(4)], axis=0)
    return jnp.where(kj == 0, sink, NEG)


def _padded_pair(before, own, other=0.0):
    z = jnp.concatenate([before, own], axis=0).astype(F32)
    z = jnp.where(lax.broadcasted_iota(jnp.int32, z.shape, 0) == 0, 0.0, z)
    zs = pltpu.roll(z, 64, 1)
    lo = lax.broadcasted_iota(jnp.int32, z.shape, 1) < 64
    fill = jnp.full_like(z, other)
    left = [jnp.where(lo, z, fill).astype(BF16), jnp.where(lo, zs, fill).astype(BF16)]
    right = [jnp.where(lo, fill, zs).astype(BF16), jnp.where(lo, fill, z).astype(BF16)]
    return left, right


def _exp_logits(s, valid, fill):
    s = jnp.where(valid, s, fill)
    return jnp.exp(s - jnp.max(s, axis=-1, keepdims=True))


ATTN_BLOCKS = 2


def _kv_blocks(kvc_ref, kvp_ref, b, col):
    own = kvc_ref[b * BLK:(b + 1) * BLK, col:col + 128]
    before = kvp_ref[:, col:col + 128] if b == 0 else kvc_ref[(b - 1) * BLK:b * BLK, col:col + 128]
    return before, own


def _fwd_attn(sinks, q, kv, g3):
    t = q.shape[0]
    tq = ATTN_BLOCKS * BLK
    seq_blocks = SEQ_LEN // BLK

    def body(sink_ref, q_ref, kvc_ref, kvp_ref, za_ref, attn_ref, ub_ref):
        lo = lax.broadcasted_iota(jnp.int32, (STACK, 128), 1) < 64
        for b in range(ATTN_BLOCKS):
            rows = slice(b * BLK, (b + 1) * BLK)
            valid = _band_mask((pl.program_id(0) * ATTN_BLOCKS + b) % seq_blocks == 0)
            k_pad = _padded_pair(*_kv_blocks(kvc_ref, kvp_ref, b, 0))
            v_one = _padded_pair(*_kv_blocks(kvc_ref, kvp_ref, b, 128), other=1.0)
            for g in range(2):
                qg = jnp.concatenate([q_ref[rows, j * 128:(j + 1) * 128] for j in range(4 * g, 4 * g + 4)], axis=0)
                pv = [_dot(_exp_logits(_dot_nt(qg, k_pad[e][g]), valid, _masked_fill(sink_ref, g, e)).astype(BF16),
                           v_one[e][g]) for e in range(2)]
                o = jnp.where(lo, pv[0], pv[1]) / pltpu.roll(jnp.where(lo, pv[1], pv[0]), 64, 1)
                for jj in range(4):
                    cols = slice((4 * g + jj) * 128, (4 * g + jj + 1) * 128)
                    oj = o[jj * BLK:(jj + 1) * BLK, :]
                    attn_ref[rows, cols] = oj
                    za = za_ref[rows, cols]
                    ub_ref[rows, cols] = (za * _sig(za) * oj).astype(BF16)

    return pl.pallas_call(
        body, name="fwd_attn", grid=(t // tq,),
        in_specs=[pl.BlockSpec(memory_space=pltpu.SMEM), _row_spec(tq, D), _row_spec(tq, 256),
                  pl.BlockSpec((BLK, 256), lambda i: (jnp.maximum(i * ATTN_BLOCKS - 1, 0), 0)), _row_spec(tq, D, 0)],
        out_specs=[_row_spec(tq, D), _row_spec(tq, D)],
        out_shape=[jax.ShapeDtypeStruct((t, D), F32), jax.ShapeDtypeStruct((t, D), BF16)],
        compiler_params=_params("parallel"),
    )(sinks, q, kv, kv, g3)


def _fwd_out_bwd_head(ya, ub, g3, x, target, g_post, wpa, wout, tm, parts=1):
    t = x.shape[0]

    def body(ya_ref, ub_ref, ga_ref, gb_ref, x_ref, tgt_ref, gp_ref, wpa_ref, wout_ref,
             loss_ref, dout_ref, dya_ref, dub_ref, dgab_ref, dwout_ref, dwpa_ref, dgp_ref):
        @pl.when(pl.program_id(0) == 0)
        def _():
            loss_ref[...] = jnp.zeros_like(loss_ref)
            dwout_ref[...] = jnp.zeros_like(dwout_ref)
            dwpa_ref[...] = jnp.zeros_like(dwpa_ref)
            dgp_ref[...] = jnp.zeros_like(dgp_ref)

        g = gp_ref[...]
        sq = jnp.zeros((1, 1), F32)
        dgp = jnp.zeros((1, D), F32)
        mbs, dys, dybs = [], [], []
        for part in range(parts):
            rows = slice(part * (tm // parts), (part + 1) * (tm // parts))
            ub = ub_ref[rows, :]
            ya = ya_ref[rows, :]
            yb = _dot(ub, wpa_ref[...])
            sa = _sig(ga_ref[rows, :])
            sb = _sig(gb_ref[rows, :])
            mb = (sa * ya + sb * yb).astype(BF16)
            y = _dot(mb, wout_ref[...])
            r = lax.rsqrt(jnp.mean(y * y, axis=-1, keepdims=True) + RMS_EPS)
            n = y * r
            err = (x_ref[rows, :] + n * g) - tgt_ref[rows, :]
            sq = sq + jnp.sum(jnp.sum(err * err, axis=0, keepdims=True), axis=1, keepdims=True)
            dout = err * (1.0 / D)
            dout_ref[rows, :] = dout
            dgp = dgp + jnp.sum(dout * n, axis=0, keepdims=True)
            dn = dout * g
            dy = (r * (dn - n * jnp.mean(dn * n, axis=-1, keepdims=True))).astype(BF16)
            dm = _dot_nt(dy, wout_ref[...])
            dya_ref[rows, :] = (dm * sa).astype(BF16)
            dyb = (dm * sb).astype(BF16)
            dgab_ref[rows, 0:D] = (dm * ya * (sa * (1.0 - sa))).astype(BF16)
            dgab_ref[rows, D:2 * D] = (dm * yb * (sb * (1.0 - sb))).astype(BF16)
            dub_ref[rows, :] = _dot_nt(dyb, wpa_ref[...])
            mbs.append(mb)
            dys.append(dy)
            dybs.append(dyb)
        loss_ref[...] += sq * (0.5 / D)
        dgp_ref[0:1, :] += dgp
        dwout_ref[...] += _dot_tn(jnp.concatenate(mbs, axis=0), jnp.concatenate(dys, axis=0))
        dwpa_ref[...] += _dot_tn(ub_ref[...], jnp.concatenate(dybs, axis=0))

    return pl.pallas_call(
        body, name="fwd_out_bwd_head", grid=(t // tm,),
        in_specs=[_row_spec(tm, D), _row_spec(tm, D), _row_spec(tm, D, 1), _row_spec(tm, D, 2),
                  _row_spec(tm, D), _row_spec(tm, D), pl.BlockSpec((1, D), lambda i: (0, 0)),
                  _whole_vmem(), _whole_vmem()],
        out_specs=[pl.BlockSpec((8, 128), lambda i: (0, 0)), _row_spec(tm, D), _row_spec(tm, D), _row_spec(tm, D),
                   _row_spec(tm, 2 * D), _whole_vmem(), _whole_vmem(), pl.BlockSpec((8, D), lambda i: (0, 0))],
        out_shape=[jax.ShapeDtypeStruct((8, 128), F32), jax.ShapeDtypeStruct((t, D), F32),
                   jax.ShapeDtypeStruct((t, D), BF16), jax.ShapeDtypeStruct((t, D), F32),
                   jax.ShapeDtypeStruct((t, 2 * D), BF16), jax.ShapeDtypeStruct((D, D), F32),
                   jax.ShapeDtypeStruct((D, D), F32), jax.ShapeDtypeStruct((8, D), F32)],
        compiler_params=_params("arbitrary", vmem_limit_bytes=V7X_VMEM_BYTES - (2 << 20)),
    )(ya, ub, g3, g3, x, target, g_post, wpa, wout)


def _bwd_attn(sinks, q, kv, attn, dub, g3, cos_t, sin_t, wt):
    t = q.shape[0]
    tq = ATTN_BLOCKS * BLK
    seq_blocks = SEQ_LEN // BLK

    def body(sink_ref, q_ref, kvc_ref, kvp_ref, attn_ref, dub_ref, za_ref, c_ref, s_ref, w_ref,
             dq_ref, dza_ref, dkv_own_ref, dkv_prev_ref, dsink_ref, dh_ref):
        @pl.when(pl.program_id(0) == 0)
        def _():
            dsink_ref[...] = jnp.zeros_like(dsink_ref)

        lo = lax.broadcasted_iota(jnp.int32, (STACK, 128), 1) < 64
        lane8 = lax.broadcasted_iota(jnp.int32, (8, 128), 1)
        lo2 = lax.broadcasted_iota(jnp.int32, (2 * BLK, 128), 1) < 64
        sink_row = lax.broadcasted_iota(jnp.int32, (2 * BLK, 128), 0) == 0
        dsink = jnp.zeros((8, 128), F32)
        for b in range(ATTN_BLOCKS):
            rows = slice(b * BLK, (b + 1) * BLK)
            valid = _band_mask((pl.program_id(0) * ATTN_BLOCKS + b) % seq_blocks == 0)
            k_pad = _padded_pair(*_kv_blocks(kvc_ref, kvp_ref, b, 0))
            v_pad = _padded_pair(*_kv_blocks(kvc_ref, kvp_ref, b, 128))
            c = c_ref[rows, :]
            s = s_ref[rows, :]
            dk_acc, dv_acc = [], []
            for g in range(2):
                qg, dog = [], []
                for j in range(4 * g, 4 * g + 4):
                    cols = slice(j * 128, (j + 1) * 128)
                    za = za_ref[rows, cols]
                    sg = _sig(za)
                    dub = dub_ref[rows, cols]
                    dza_ref[rows, cols] = (dub * attn_ref[rows, cols] * (sg * (1.0 + za * (1.0 - sg)))).astype(BF16)
                    dog.append((dub * (za * sg)).astype(BF16))
                    qg.append(q_ref[rows, cols])
                qg = jnp.concatenate(qg, axis=0)
                dog = jnp.concatenate(dog, axis=0)
                dq = jnp.zeros((STACK, 128), F32)
                ds_both, p_both = [], []
                for e in range(2):
                    p = _exp_logits(_dot_nt(qg, k_pad[e][g]), valid, _masked_fill(sink_ref, g, e))
                    p = p / jnp.sum(p, axis=-1, keepdims=True)
                    dp = _dot_nt(dog, v_pad[e][g])
                    ds = p * (dp - jnp.sum(p * dp, axis=-1, keepdims=True))
                    for jj in range(4):
                        tot = jnp.sum(ds[jj * BLK:(jj + 1) * BLK, 0:1], axis=0, keepdims=True)
                        dsink = dsink + jnp.where(lane8 == 2 * (4 * g + jj) + e, tot, 0.0)
                    ds = ds.astype(BF16)
                    dq = dq + _dot(ds, k_pad[e][g])
                    ds_both.append(ds)
                    p_both.append(p.astype(BF16))
                zero = jnp.zeros_like(qg)
                q2 = jnp.concatenate([jnp.where(lo, qg, zero), jnp.where(lo, zero, qg)], axis=0)
                do2 = jnp.concatenate([jnp.where(lo, dog, zero), jnp.where(lo, zero, dog)], axis=0)
                dk_acc.append(_dot_tn(q2, jnp.concatenate(ds_both, axis=0)).T)
                dv_acc.append(_dot_tn(do2, jnp.concatenate(p_both, axis=0)).T)
                for jj in range(4):
                    cols = slice((4 * g + jj) * 128, (4 * g + jj + 1) * 128)
                    dqj = dq[jj * BLK:(jj + 1) * BLK, :] * LOGIT_SCALE
                    dq_ref[rows, cols] = (dqj * c - _swap_halves(dqj) * s).astype(BF16)
            for col, acc in ((0, dk_acc), (128, dv_acc)):
                both = jnp.where(lo2, acc[0] + pltpu.roll(acc[0], 64, 1), acc[1] + pltpu.roll(acc[1], 64, 1))
                both = jnp.where(sink_row, 0.0, both)
                dkv_prev_ref[rows, col:col + 128] = both[0:BLK, :]
                dkv_own_ref[rows, col:col + 128] = both[BLK:2 * BLK, :]
        dsink_ref[...] += dsink
        dh_ref[...] = _dot(dq_ref[...], w_ref[ROW_Q:ROW_KV, :]) + _dot(dza_ref[...], w_ref[ROW_ZA:ROW_GA, :])

    tab = pl.BlockSpec((tq, 128), lambda i: (i % (SEQ_LEN // tq), 0))
    return pl.pallas_call(
        body, name="bwd_attn", grid=(t // tq,),
        in_specs=[pl.BlockSpec(memory_space=pltpu.SMEM), _row_spec(tq, D), _row_spec(tq, 256),
                  pl.BlockSpec((BLK, 256), lambda i: (jnp.maximum(i * ATTN_BLOCKS - 1, 0), 0)),
                  _row_spec(tq, D), _row_spec(tq, D), _row_spec(tq, D, 0), tab, tab, _whole_vmem()],
        out_specs=[_row_spec(tq, D), _row_spec(tq, D), _row_spec(tq, 256), _row_spec(tq, 256),
                   pl.BlockSpec((8, 128), lambda i: (0, 0)), _row_spec(tq, D)],
        out_shape=[jax.ShapeDtypeStruct((t, D), BF16), jax.ShapeDtypeStruct((t, D), BF16),
                   jax.ShapeDtypeStruct((t, 256), F32), jax.ShapeDtypeStruct((t, 256), F32),
                   jax.ShapeDtypeStruct((8, 128), F32), jax.ShapeDtypeStruct((t, D), F32)],
        compiler_params=_params("arbitrary"),
    )(sinks, q, kv, kv, attn, dub, g3, cos_t, sin_t, wt)


def _bwd_kv_finish(dkv_own, dkv_prev, cos_t, sin_t):
    t = dkv_own.shape[0]
    tm = 512
    seq_tiles = SEQ_LEN // tm
    n_blocks = t // BLK

    def body(own_ref, same_ref, nxt_ref, c_ref, s_ref, out_ref):
        keep = jnp.where(pl.program_id(0) % seq_tiles == seq_tiles - 1, 0.0, 1.0)
        shifted = jnp.concatenate([same_ref[BLK:tm, :], nxt_ref[...] * keep], axis=0)
        tot = own_ref[...] + shifted
        dk = tot[:, 0:128]
        out_ref[:, 0:128] = (dk * c_ref[...] - _swap_halves(dk) * s_ref[...]).astype(BF16)
        out_ref[:, 128:256] = tot[:, 128:256].astype(BF16)

    tab = pl.BlockSpec((tm, 128), lambda i: (i % seq_tiles, 0))
    return pl.pallas_call(
        body, name="bwd_kv_finish", grid=(t // tm,),
        in_specs=[_row_spec(tm, 256), _row_spec(tm, 256),
                  pl.BlockSpec((BLK, 256), lambda i: (jnp.minimum((i + 1) * (tm // BLK), n_blocks - 1), 0)), tab, tab],
        out_specs=_row_spec(tm, 256),
        out_shape=jax.ShapeDtypeStruct((t, 256), BF16),
        compiler_params=_params("parallel"),
    )(dkv_own, dkv_prev, dkv_prev, cos_t, sin_t)


STAGE_ROWS = 256


def _bwd_conv(dya, a4, h, wconv8, wpc, tm, parts):
    t = a4.shape[0]
    n_t = t // tm
    sub = tm // parts
    seq_tiles = SEQ_LEN // tm

    def body(dya_ref, xc_ref, bg_ref, cg_ref, zc_ref, xcp_ref, cgp_ref, w_ref, wpc_ref, h_ref,
             da4_ref, dwpc_ref, dwc_ref, o32_ref, o16_ref, acc_ref, stage_ref, later_ref, sems):
        step = pl.program_id(0)
        tile = n_t - 1 - step

        @pl.when(step == 0)
        def _():
            dwpc_ref[...] = jnp.zeros_like(dwpc_ref)
            dwc_ref[...] = jnp.zeros_like(dwc_ref)
            acc_ref[...] = jnp.zeros_like(acc_ref)

        keep_prev = jnp.where(tile % seq_tiles == 0, 0.0, 1.0)
        ends_sequence = tile % seq_tiles == seq_tiles - 1

        def part(p, later):
            r0 = p * sub
            here = slice(r0, r0 + sub)
            if p == 0:
                u_prev = cgp_ref[14:16, :].astype(F32) * xcp_ref[14:16, :].astype(F32) * keep_prev
            else:
                u_prev = cg_ref[r0 - 2:r0, :].astype(F32) * xc_ref[r0 - 2:r0, :].astype(F32)
            xc = xc_ref[here, :].astype(F32)
            bg = bg_ref[here, :].astype(F32)
            cg = cg_ref[here, :].astype(F32)
            zc = zc_ref[here, :].astype(F32)
            u, u_m1, u_m2, yconv, sg, sz, co = _conv_forward(xc, bg, cg, zc, u_prev[0:1, :], u_prev[1:2, :], w_ref)
            ua = (sz * co).astype(BF16)
            dua = _dot_nt(dya_ref[here, :], wpc_ref[...])
            da4_ref[here, 3 * D:4 * D] = (dua * co * (sg * (1.0 + zc * (1.0 - sg)))).astype(BF16)
            dco = dua * sz
            da4_ref[here, D:2 * D] = (dco * yconv).astype(BF16)
            dyc = dco * bg
            dwc = jnp.concatenate([jnp.sum(dyc * s, axis=0, keepdims=True) for s in (u_m2, u_m1, u)], axis=0)
            rows = lax.broadcasted_iota(jnp.int32, xc.shape, 0)
            n0 = later[0:1, :]
            n1 = later[1:2, :]
            dyc_p1 = jnp.where(rows == sub - 1, n0, pltpu.roll(dyc, sub - 1, 0))
            dyc_p2 = jnp.where(rows == sub - 2, n0, jnp.where(rows == sub - 1, n1, pltpu.roll(dyc, sub - 2, 0)))
            du = w_ref[2:3, :] * dyc + w_ref[1:2, :] * dyc_p1 + w_ref[0:1, :] * dyc_p2
            da4_ref[here, 0:D] = (du * cg).astype(BF16)
            da4_ref[here, 2 * D:3 * D] = (du * xc).astype(BF16)
            return ua, dwc, dyc[0:8, :]

        later = jnp.where(ends_sequence, 0.0, later_ref[...])
        uas, dwc = [], jnp.zeros((3, D), F32)
        for p in reversed(range(parts)):
            ua, dwc_p, later = part(p, later)
            uas.insert(0, ua)
            dwc = dwc + dwc_p
        later_ref[...] = later
        dwpc_ref[...] += _dot_tn(jnp.concatenate(uas, axis=0), dya_ref[...])
        dwc_ref[0:3, :] += dwc
        for j in range(4):
            acc_ref[j * D:(j + 1) * D, :] += _dot_tn(da4_ref[:, j * D:(j + 1) * D], h_ref[...])

        @pl.when(step == n_t - 1)
        def _():
            c32 = pltpu.make_async_copy(acc_ref, o32_ref.at[pl.ds(0, 4 * D)], sems.at[0])
            c32.start()
            for j in range(4 * D // STAGE_ROWS):
                rows = pl.ds(j * STAGE_ROWS, STAGE_ROWS)
                stage_ref[...] = acc_ref[rows, :].astype(BF16)
                c16 = pltpu.make_async_copy(stage_ref, o16_ref.at[rows], sems.at[1])
                c16.start()
                c16.wait()
            c32.wait()

    def rows_of_tile(width, col=0):
        return pl.BlockSpec((tm, width), lambda s: (n_t - 1 - s, col))

    def prev(col):
        return pl.BlockSpec((16, D), lambda s: (jnp.maximum((n_t - 1 - s) * (tm // 16) - 1, 0), col))

    hbm = pl.BlockSpec(memory_space=pl.ANY)
    out = pl.pallas_call(
        body, name="bwd_conv", grid=(n_t,),
        in_specs=[rows_of_tile(D), rows_of_tile(D, 0), rows_of_tile(D, 1), rows_of_tile(D, 2), rows_of_tile(D, 3),
                  prev(0), prev(2), pl.BlockSpec((8, D), lambda s: (0, 0)), _whole_vmem(), rows_of_tile(D)],
        out_specs=[rows_of_tile(4 * D), _whole_vmem(), pl.BlockSpec((8, D), lambda s: (0, 0)), hbm, hbm],
        out_shape=[jax.ShapeDtypeStruct((t, 4 * D), BF16), jax.ShapeDtypeStruct((D, D), F32),
                   jax.ShapeDtypeStruct((8, D), F32), jax.ShapeDtypeStruct((D_IN, D), F32),
                   jax.ShapeDtypeStruct((D_IN, D), BF16)],
        scratch_shapes=[pltpu.VMEM((4 * D, D), F32), pltpu.VMEM((STAGE_ROWS, D), BF16), pltpu.VMEM((8, D), F32),
                        pltpu.SemaphoreType.DMA((2,))],
        compiler_params=pltpu.CompilerParams(dimension_semantics=("arbitrary",), vmem_limit_bytes=V7X_VMEM_BYTES - (2 << 20)),
    )(dya, a4, a4, a4, a4, a4, a4, wconv8, wpc, h)
    return out[0], out[1], out[2], (out[3], out[4])


def _bwd_dh(da4, dh_part, dkv, dgab, wt, x, g_pre, dout, tm):
    t = x.shape[0]

    def body(da4_ref, dhp_ref, dkv_ref, dgab_ref, w_ref, x_ref, g_ref, dout_ref, gx_ref, dg_ref):
        @pl.when(pl.program_id(0) == 0)
        def _():
            dg_ref[...] = jnp.zeros_like(dg_ref)

        dh = dhp_ref[...] + _dot(da4_ref[...], w_ref[0:ROW_Q, :])
        dh += _dot(dkv_ref[...], w_ref[ROW_KV:ROW_ZA, :])
        dh += _dot(dgab_ref[...], w_ref[ROW_GA:D_IN, :])
        xf = x_ref[...]
        r = lax.rsqrt(jnp.mean(xf * xf, axis=-1, keepdims=True) + RMS_EPS)
        xn = xf * r
        dg_ref[0:1, :] += jnp.sum(dh * xn, axis=0, keepdims=True)
        dxn = dh * g_ref[...]
        gx_ref[...] = dout_ref[...] + r * (dxn - xn * jnp.mean(dxn * xn, axis=-1, keepdims=True))

    return pl.pallas_call(
        body, name="bwd_dh", grid=(t // tm,),
        in_specs=[_row_spec(tm, 4 * D), _row_spec(tm, D), _row_spec(tm, 256), _row_spec(tm, 2 * D),
                  _whole_vmem(), _row_spec(tm, D), pl.BlockSpec((1, D), lambda i: (0, 0)), _row_spec(tm, D)],
        out_specs=[_row_spec(tm, D), pl.BlockSpec((8, D), lambda i: (0, 0))],
        out_shape=[jax.ShapeDtypeStruct((t, D), F32), jax.ShapeDtypeStruct((8, D), F32)],
        compiler_params=_params("arbitrary"),
    )(da4, dh_part, dkv, dgab, wt, x, g_pre, dout)


def _bwd_dw_in(h, piece, row0, nb, tm, name, prev):
    t, n = piece.shape
    n_t = t // tm

    def body(*refs):
        h_ref, p_ref = refs[0], refs[1]
        o32_ref, o16_ref, acc_ref, acc16_ref, sems = refs[-5:]
        j, i = pl.program_id(0), pl.program_id(1)

        @pl.when(i == 0)
        def _():
            acc_ref[...] = jnp.zeros_like(acc_ref)

        acc_ref[...] += _dot_tn(p_ref[...], h_ref[...])

        @pl.when(i == n_t - 1)
        def _():
            acc16_ref[...] = acc_ref[...].astype(BF16)
            rows = pl.ds(pl.multiple_of(row0 + j * nb, 16), nb)
            c32 = pltpu.make_async_copy(acc_ref, o32_ref.at[rows], sems.at[0])
            c16 = pltpu.make_async_copy(acc16_ref, o16_ref.at[rows], sems.at[1])
            c32.start()
            c16.start()
            c32.wait()
            c16.wait()

    hbm = pl.BlockSpec(memory_space=pl.ANY)
    carried = [] if prev is None else list(prev)
    return pl.pallas_call(
        body, name=name, grid=(n // nb, n_t),
        in_specs=[pl.BlockSpec((tm, D), lambda j, i: (i, 0)), pl.BlockSpec((tm, nb), lambda j, i: (i, j))]
        + [hbm] * len(carried),
        out_specs=[hbm, hbm],
        out_shape=[jax.ShapeDtypeStruct((D_IN, D), F32), jax.ShapeDtypeStruct((D_IN, D), BF16)],
        scratch_shapes=[pltpu.VMEM((nb, D), F32), pltpu.VMEM((nb, D), BF16), pltpu.SemaphoreType.DMA((2,))],
        input_output_aliases={2: 0, 3: 1} if carried else {},
        compiler_params=_params("arbitrary", "arbitrary"),
    )(h, piece, *carried)


def _place():
    x, y, c = lax.axis_index("x"), lax.axis_index("y"), lax.axis_index("c")
    return x, y, c, 4 * x + 2 * y + c


def _peer(x, y, c, k):
    return (1 - x if k & 4 else x, 1 - y if k & 2 else y, 1 - c if k & 1 else c)


ICI_MASKS = (4, 2, 6)


def _all_gather(shards):
    n = len(shards)

    def body(*refs):
        src, dst = refs[:n], refs[n:2 * n]
        send_sems, recv_sems, local_sems = refs[2 * n:]
        x, y, c, me = _place()
        sibling = _peer(x, y, c, 1)

        def copy(a, s, block, to, own=False):
            return pltpu.make_async_remote_copy(
                src_ref=src[a] if own else dst[a].at[block], dst_ref=dst[a].at[block],
                send_sem=send_sems.at[a * 7 + s], recv_sem=recv_sems.at[a * 7 + s], device_id=to, device_id_type=MESH_ID)

        local = [pltpu.make_async_copy(src[a], dst[a].at[me], local_sems.at[a]) for a in range(n)]
        for cp in local:
            cp.start()
        started = [copy(a, 0, me, sibling, own=True) for a in range(n)]
        started += [copy(a, 1 + j, me, _peer(x, y, c, k), own=True) for j, k in enumerate(ICI_MASKS) for a in range(n)]
        for cp in started:
            cp.start()
        for j, k in enumerate(ICI_MASKS):
            for a in range(n):
                copy(a, 1 + j, me ^ k, sibling).wait_recv()
                fwd = copy(a, 4 + j, me ^ k, sibling)
                fwd.start()
                started.append(fwd)
        for a in range(n):
            copy(a, 0, me ^ 1, sibling).wait_recv()
        for j, k in enumerate(ICI_MASKS):
            for a in range(n):
                copy(a, 4 + j, me ^ 1 ^ k, sibling).wait_recv()
        for cp in started:
            cp.wait_send()
        for cp in local:
            cp.wait()

    hbm = pl.BlockSpec(memory_space=pl.ANY)
    return pl.pallas_call(
        body, name="all_gather_weights",
        in_specs=[hbm] * n, out_specs=[hbm] * n,
        out_shape=[jax.ShapeDtypeStruct((N_DEV,) + s.shape, s.dtype) for s in shards],
        scratch_shapes=[pltpu.SemaphoreType.DMA((7 * n,)), pltpu.SemaphoreType.DMA((7 * n,)),
                        pltpu.SemaphoreType.DMA((n,))],
    )(*shards)


def _direct_copies(src, land, send_sems, recv_sems):
    x, y, c, me = _place()
    return [pltpu.make_async_remote_copy(
        src_ref=src[a], dst_ref=land[a].at[me], send_sem=send_sems.at[a * 7 + k - 1],
        recv_sem=recv_sems.at[a * 7 + k - 1], device_id=_peer(x, y, c, k), device_id_type=MESH_ID)
        for k in range(1, N_DEV) for a in range(len(src))]


def _gather_start(shards, name):
    n = len(shards)

    def body(*refs):
        src, land = refs[:n], refs[n:2 * n]
        send_sems, recv_sems = refs[2 * n], refs[2 * n + 1]
        token_ref = refs[-1]
        for cp in _direct_copies(src, land, send_sems, recv_sems):
            cp.start()
        token_ref[...] = jnp.zeros_like(token_ref)

    hbm = pl.BlockSpec(memory_space=pltpu.HBM)
    sem = pl.BlockSpec(memory_space=pltpu.SEMAPHORE)
    lands = [lax.empty((N_DEV,) + s.shape, s.dtype) for s in shards]
    out = pl.pallas_call(
        body, name=name + "_start",
        out_shape=(pltpu.SemaphoreType.DMA((7 * n,)), pltpu.SemaphoreType.DMA((7 * n,)),
                   *[pltpu.HBM(s.shape, s.dtype) for s in shards], *[pltpu.HBM(s.shape, s.dtype) for s in lands],
                   jax.ShapeDtypeStruct((8, 128), F32)),
        in_specs=[hbm] * (2 * n), out_specs=(sem, sem, *[hbm] * (2 * n), _whole_vmem()),
        input_output_aliases={a: 2 + a for a in range(2 * n)},
        compiler_params=pltpu.CompilerParams(has_side_effects=pltpu.SideEffectType.DATAFLOW_SIDE_EFFECTING),
    )(*[pltpu.with_memory_space_constraint(s, pltpu.HBM) for s in list(shards) + lands])
    return out[0], out[1], out[2:2 + n], out[2 + n:2 + 2 * n], out[-1]


def _gather_wait(send_sems, recv_sems, flying, lands, after, name):
    n = len(flying)

    def body(*refs):
        src, land = refs[:n], refs[n:2 * n]
        for cp in _direct_copies(src, land, refs[2 * n], refs[2 * n + 1]):
            cp.wait_send()
            cp.wait_recv()

    hbm = pl.BlockSpec(memory_space=pltpu.HBM)
    sem = pl.BlockSpec(memory_space=pltpu.SEMAPHORE)
    out = pl.pallas_call(
        body, name=name + "_wait",
        out_shape=tuple(pltpu.HBM(s.shape, s.dtype) for s in list(flying) + list(lands)),
        in_specs=[hbm] * (2 * n) + [sem, sem, pl.BlockSpec(memory_space=pl.ANY)], out_specs=tuple([hbm] * (2 * n)),
        input_output_aliases={a: a for a in range(2 * n)},
        compiler_params=pltpu.CompilerParams(has_side_effects=pltpu.SideEffectType.DATAFLOW_SIDE_EFFECTING),
    )(*flying, *lands, send_sems, recv_sems, after)
    return out[n:]


def _exchange_sibling(by_dest):
    n = len(by_dest)

    def body(*refs):
        src, dst = refs[:n], refs[n:2 * n]
        send_sems, recv_sems = refs[2 * n:]
        x, y, c, _ = _place()
        sibling = _peer(x, y, c, 1)
        copies = [pltpu.make_async_remote_copy(
            src_ref=src[a].at[2 * p + (1 - c)], dst_ref=dst[a].at[p], send_sem=send_sems.at[a * 4 + p],
            recv_sem=recv_sems.at[a * 4 + p], device_id=sibling, device_id_type=MESH_ID)
            for a in range(n) for p in range(4)]
        for cp in copies:
            cp.start()
        for cp in copies:
            cp.wait_recv()
        for cp in copies:
            cp.wait_send()

    hbm = pl.BlockSpec(memory_space=pl.ANY)
    return pl.pallas_call(
        body, name="exchange_sibling", in_specs=[hbm] * n, out_specs=[hbm] * n,
        out_shape=[jax.ShapeDtypeStruct((4,) + s.shape[1:], s.dtype) for s in by_dest],
        scratch_shapes=[pltpu.SemaphoreType.DMA((4 * n,)), pltpu.SemaphoreType.DMA((4 * n,))],
    )(*by_dest)


def _chip_copies(src, land, send_sems, recv_sems):
    x, y, c, _ = _place()
    chip = 2 * x + y
    return [pltpu.make_async_remote_copy(
        src_ref=src[a].at[chip ^ (k >> 1)], dst_ref=land[a].at[j], send_sem=send_sems.at[a * 3 + j],
        recv_sem=recv_sems.at[a * 3 + j], device_id=_peer(x, y, c, k), device_id_type=MESH_ID)
        for j, k in enumerate(ICI_MASKS) for a in range(len(src))]


def _exchange_chips_start(by_chip):
    n = len(by_chip)

    def body(*refs):
        src, land = refs[:n], refs[n:2 * n]
        send_sems, recv_sems = refs[2 * n], refs[2 * n + 1]
        token_ref = refs[-1]
        for cp in _chip_copies(src, land, send_sems, recv_sems):
            cp.start()
        token_ref[...] = jnp.zeros_like(token_ref)

    hbm = pl.BlockSpec(memory_space=pltpu.HBM)
    sem = pl.BlockSpec(memory_space=pltpu.SEMAPHORE)
    lands = [lax.empty((3,) + s.shape[1:], s.dtype) for s in by_chip]
    out = pl.pallas_call(
        body, name="exchange_chips_start",
        out_shape=(pltpu.SemaphoreType.DMA((3 * n,)), pltpu.SemaphoreType.DMA((3 * n,)),
                   *[pltpu.HBM(s.shape, s.dtype) for s in by_chip], *[pltpu.HBM(s.shape, s.dtype) for s in lands],
                   jax.ShapeDtypeStruct((8, 128), F32)),
        in_specs=[hbm] * (2 * n), out_specs=(sem, sem, *[hbm] * (2 * n), _whole_vmem()),
        input_output_aliases={a: 2 + a for a in range(2 * n)},
        compiler_params=pltpu.CompilerParams(has_side_effects=pltpu.SideEffectType.DATAFLOW_SIDE_EFFECTING),
    )(*[pltpu.with_memory_space_constraint(s, pltpu.HBM) for s in list(by_chip) + lands])
    return out[0], out[1], out[2:2 + n], out[2 + n:2 + 2 * n], out[-1]


def _exchange_chips_wait(send_sems, recv_sems, flying, lands, after):
    n = len(flying)

    def body(*refs):
        src, land = refs[:n], refs[n:2 * n]
        send_sems_ref, recv_sems_ref = refs[2 * n], refs[2 * n + 1]
        for cp in _chip_copies(src, land, send_sems_ref, recv_sems_ref):
            cp.wait_send()
            cp.wait_recv()

    hbm = pl.BlockSpec(memory_space=pltpu.HBM)
    sem = pl.BlockSpec(memory_space=pltpu.SEMAPHORE)
    out = pl.pallas_call(
        body, name="exchange_chips_wait",
        out_shape=tuple(pltpu.HBM(s.shape, s.dtype) for s in list(flying) + list(lands)),
        in_specs=[hbm] * (2 * n) + [sem, sem, pl.BlockSpec(memory_space=pl.ANY)], out_specs=tuple([hbm] * (2 * n)),
        input_output_aliases={a: a for a in range(2 * n)},
        compiler_params=pltpu.CompilerParams(has_side_effects=pltpu.SideEffectType.DATAFLOW_SIDE_EFFECTING),
    )(*flying, *lands, send_sems, recv_sems, after)
    return out[n:]


def _adamw_math(w, g, m, v):
    m = ADAM_B1 * m + (1.0 - ADAM_B1) * g
    v = ADAM_B2 * v + (1.0 - ADAM_B2) * (g * g)
    m_hat = m / (1.0 - ADAM_B1 ** ADAM_STEP)
    v_hat = v / (1.0 - ADAM_B2 ** ADAM_STEP)
    return -ADAM_LR * (m_hat / (jnp.sqrt(v_hat) + ADAM_EPS) + ADAM_WD * w), m, v


def _pair_sum(owns, recvs, c_arr, tr, name):
    n = len(owns)
    _, rows, cols = owns[0].shape

    def body(c_ref, *refs):
        for a in range(n):
            s = refs[a][...] + refs[n + a][...].astype(F32)
            refs[2 * n + a][...] = s
            refs[3 * n + a][...] = s.astype(BF16)

    by_chip = pl.BlockSpec((None, tr, cols), lambda p, i, c_ref: (p, i, 0))
    mine = pl.BlockSpec((None, tr, cols), lambda p, i, c_ref: (2 * p + c_ref[0], i, 0))
    out = pl.pallas_call(
        body, name=name,
        grid_spec=pltpu.PrefetchScalarGridSpec(
            num_scalar_prefetch=1, grid=(4, rows // tr), in_specs=[mine] * n + [by_chip] * n, out_specs=[by_chip] * (2 * n)),
        out_shape=[jax.ShapeDtypeStruct((4, rows, cols), F32)] * n + [jax.ShapeDtypeStruct((4, rows, cols), BF16)] * n,
        compiler_params=_params("parallel", "parallel"),
    )(c_arr, *owns, *recvs)
    return out[:n], out[n:]


def _chip_sum(pairs, recvs, chip_arr, tr, name, adam=None):
    n = len(pairs)
    _, rows, cols = pairs[0].shape
    n_state = 0 if adam is None else 3 * n

    def body(chip_ref, *refs):
        outs = refs[2 * n + n_state:]
        for a in range(n):
            g = refs[a][...]
            for j in range(3):
                g = g + refs[n + a][j].astype(F32)
            outs[a][...] = g
            if adam is not None:
                w_ref, m_ref, v_ref = (refs[2 * n + s * n + a] for s in range(3))
                outs[n + a][...], outs[2 * n + a][...], outs[3 * n + a][...] = _adamw_math(w_ref[...], g, m_ref[...], v_ref[...])

    blk = pl.BlockSpec((tr, cols), lambda i, chip_ref: (i, 0))
    n_out = n if adam is None else 4 * n
    out = pl.pallas_call(
        body, name=name,
        grid_spec=pltpu.PrefetchScalarGridSpec(
            num_scalar_prefetch=1, grid=(rows // tr,),
            in_specs=[pl.BlockSpec((None, tr, cols), lambda i, chip_ref: (chip_ref[0], i, 0))] * n
            + [pl.BlockSpec((3, tr, cols), lambda i, chip_ref: (0, i, 0))] * n + [blk] * n_state,
            out_specs=[blk] * n_out),
        out_shape=[jax.ShapeDtypeStruct((rows, cols), F32)] * n_out,
        compiler_params=_params("parallel"),
    )(chip_arr, *pairs, *recvs, *([] if adam is None else [t for group in adam for t in group]))
    return out if adam is None else (out[:n], out[n:2 * n], out[2 * n:3 * n], out[3 * n:])


def _adamw(ws, gs, ms, vs, name):
    n = len(ws)

    def body(*refs):
        for a in range(n):
            w_ref, g_ref, m_ref, v_ref = (refs[s * n + a] for s in range(4))
            refs[4 * n + a][...], refs[5 * n + a][...], refs[6 * n + a][...] = _adamw_math(
                w_ref[...], g_ref[...], m_ref[...], v_ref[...])

    out = pl.pallas_call(body, name=name, out_shape=[jax.ShapeDtypeStruct(w.shape, F32) for w in ws] * 3)(
        *ws, *gs, *ms, *vs)
    return out[:n], out[n:2 * n], out[2 * n:]


def _sum_small(small_all):
    def body(s_ref, o_ref):
        g = s_ref[0]
        for d in range(1, N_DEV):
            g = g + s_ref[d]
        o_ref[...] = g

    return pl.pallas_call(body, name="sum_small", out_shape=jax.ShapeDtypeStruct(small_all.shape[1:], F32))(small_all)


def _rope_tables():
    inv_freq = ROPE_THETA ** (-jnp.arange(0, HEAD_DIM, 2, dtype=F32) / HEAD_DIM)
    ang = jnp.arange(SEQ_LEN).astype(F32)[:, None] * inv_freq[None, :]
    cos, sin = jnp.cos(ang), jnp.sin(ang)
    return jnp.tile(cos, (1, 4)), jnp.tile(jnp.concatenate([-sin, sin], axis=1), (1, 2))


def _local_step(x, target, g_pre, g_post, sinks, wt, wconv, squares, start_exchange=None):
    cos_t, sin_t = _rope_tables()
    wconv8 = jnp.pad(wconv, ((0, 5), (0, 0)))
    h, q, kv, g3 = _fwd_in_attn(x, g_pre, wt, cos_t, sin_t, 512)
    wpc, wpa, wout = squares(kv)
    a4, ya = _fwd_in_conv(h, wt, wconv8, wpc, 512)
    attn, ub = _fwd_attn(sinks, q, kv, g3)
    loss8, dout, dya, dub, dgab, dwout, dwpa, dgpost8 = _fwd_out_bwd_head(ya, ub, g3, x, target, g_post, wpa, wout, 512)
    dq, dza, dkv_own, dkv_prev, dsink8, dh_part = _bwd_attn(sinks, q, kv, attn, dub, g3, cos_t, sin_t, wt)
    dkv = _bwd_kv_finish(dkv_own, dkv_prev, cos_t, sin_t)
    da4, dwpc, dwconv8, dwt = _bwd_conv(dya, a4, h, wconv8, wpc, 512, 2)
    dwt = _bwd_dw_in(h, dq, ROW_Q, 1024, 1024, "bwd_dw_in_q", dwt)
    dwt = _bwd_dw_in(h, dkv, ROW_KV, 256, 1024, "bwd_dw_in_kv", dwt)
    dwt = _bwd_dw_in(h, dza, ROW_ZA, 1024, 1024, "bwd_dw_in_za", dwt)
    dwt32, dwt16 = _bwd_dw_in(h, dgab, ROW_GA, 1024, 1024, "bwd_dw_in_gates", dwt)
    token, pending = (None, None) if start_exchange is None else start_exchange(dwt32, dwt16, dwpc, dwpa, dwout)
    g_pre_after = g_pre if token is None else g_pre + token[0:1, 0:1]
    grad_x, dgpre8 = _bwd_dh(da4, dh_part, dkv, dgab, wt, x, g_pre_after, dout, 512)
    small = jnp.concatenate([dgpre8, dgpost8, jnp.pad(dsink8, ((0, 0), (0, D - 128))), dwconv8,
                             jnp.pad(loss8, ((0, 0), (0, D - 128)))], axis=0)
    return loss8[0, 0], grad_x, dwt32, dwt16, dwpc, dwpa, dwout, small, pending


def kernel(x, g_pre, g_post, w_in, w_conv, sinks, w_proj_conv, w_proj_attn, w_out, loss_target, m_g_pre, m_g_post, m_w_in, m_w_conv, m_sinks, m_w_proj_conv, m_w_proj_attn, m_w_out, v_g_pre, v_g_post, v_w_in, v_w_conv, v_sinks, v_w_proj_conv, v_w_proj_attn, v_w_out):
    batch = x.shape[0]
    mx, my, mc, me = _place()
    c_arr = jnp.reshape(mc, (1,)).astype(jnp.int32)
    chip_arr = jnp.reshape(2 * mx + my, (1,)).astype(jnp.int32)

    g_wt, g_conv = _all_gather([w_in[0].T.astype(BF16), jnp.pad(w_conv[0], ((0, 5), (0, 0)))])
    wt = g_wt.reshape(D_IN, D)
    wconv = g_conv[:, 0:3, :].transpose(1, 0, 2).reshape(3, D)
    sq_mine = [w.astype(BF16) for w in (w_proj_conv[0], w_proj_attn[0], w_out[0])]
    wt, sq_mine = lax.optimization_barrier((wt, sq_mine))
    sq_send, sq_recv, sq_flying, sq_lands, sq_token = _gather_start(sq_mine, "gather_squares")

    def squares(after):
        got = _gather_wait(sq_send, sq_recv, sq_flying, sq_lands, after, "gather_squares")
        return [lax.dynamic_update_index_in_dim(full, mine, me, 0).reshape(D, D) for full, mine in zip(got, sq_mine)]

    def start_exchange(dwt32, dwt16, dwpc, dwpa, dwout):
        own_sq = [g.reshape(N_DEV, SHARD_SQ, D) for g in (dwpc, dwpa, dwout)]
        own_in = dwt32.reshape(N_DEV, SHARD_IN, D)
        from_sibling = _exchange_sibling([dwt16.reshape(N_DEV, SHARD_IN, D)] + [g.astype(BF16) for g in own_sq])
        in32, in16 = _pair_sum([own_in], from_sibling[:1], c_arr, SHARD_IN // 2, "pair_sum_w_in")
        sq32, sq16 = _pair_sum(own_sq, from_sibling[1:], c_arr, SHARD_SQ, "pair_sum_squares")
        send_sems, recv_sems, flying, lands, token = _exchange_chips_start(list(in16) + list(sq16))
        return token, (send_sems, recv_sems, flying, lands, in32, sq32)

    _, grad_x, _, _, _, _, _, small, pending = _local_step(
        x.reshape(batch * SEQ_LEN, D), loss_target.reshape(batch * SEQ_LEN, D), g_pre + sq_token[0:1, 0:1], g_post,
        sinks, wt, wconv, squares, start_exchange)
    sm_send, sm_recv, sm_flying, sm_lands, sm_token = _gather_start([small], "gather_small")
    send_sems, recv_sems, flying, lands, in32, sq32 = pending
    from_chips = _exchange_chips_wait(send_sems, recv_sems, flying, lands, sm_token)

    o_in = [o[0].T for o in _chip_sum(
        in32, from_chips[:1], chip_arr, SHARD_IN // 3, "chip_sum_adamw_w_in",
        adam=([w_in[0].T], [m_w_in[0].T], [v_w_in[0].T]))]
    g_in_mine, o_in = o_in[0], o_in[1:]
    g_sq, d_sq, m_sq, v_sq = _chip_sum(
        sq32, from_chips[1:], chip_arr, SHARD_SQ, "chip_sum_adamw_squares",
        adam=([w_proj_conv[0], w_proj_attn[0], w_out[0]], [m_w_proj_conv[0], m_w_proj_attn[0], m_w_out[0]],
              [v_w_proj_conv[0], v_w_proj_attn[0], v_w_out[0]]))
    both_done, g_in_mine = lax.optimization_barrier((d_sq[0], g_in_mine))
    (small_all,) = _gather_wait(sm_send, sm_recv, sm_flying, sm_lands, both_done, "gather_small")
    gs = _sum_small(lax.dynamic_update_index_in_dim(small_all, small, me, 0))
    g_g_pre, g_g_post, g_sinks, loss = gs[0:1], gs[8:9], gs[16:17, 0:N_HEADS], gs[32, 0]
    g_conv_mine = lax.dynamic_slice_in_dim(gs[24:27], me * SHARD_SQ, SHARD_SQ, axis=1)
    o_small = _adamw([g_pre, g_post, sinks, w_conv[0]], [g_g_pre, g_g_post, g_sinks, g_conv_mine],
                     [m_g_pre, m_g_post, m_sinks, m_w_conv[0]], [v_g_pre, v_g_post, v_sinks, v_w_conv[0]], "adamw_small")

    grads = [g_g_pre, g_g_post, g_in_mine[None], g_conv_mine[None], g_sinks] + [g[None] for g in g_sq]
    rest = []
    for idx, sq in enumerate((d_sq, m_sq, v_sq)):
        gp, gq, sk, cv = o_small[idx]
        rest += [gp, gq, o_in[idx][None], cv[None], sk] + [s[None] for s in sq]
    return (loss, grad_x.reshape(batch, SEQ_LEN, D), *grads, *rest)
```

```python
import functools

import jax
import jax.numpy as jnp
from jax import lax
from jax.experimental import pallas as pl
from jax.experimental.pallas import tpu as pltpu

D = 1024
N_HEADS = 16
HEAD_DIM = 64
LOGIT_SCALE = HEAD_DIM ** -0.5
BLK = 128
SEQ_LEN = 2048
D_IN = 8448
ROW_Q, ROW_KV, ROW_ZA, ROW_GA = 4 * D, 5 * D, 5 * D + 256, 6 * D + 256
SHARD_IN = D_IN // 8
SHARD_SQ = D // 8
N_DEV = 8
V7X_VMEM_BYTES = 64 << 20
ROPE_THETA = 10000.0
RMS_EPS = 1e-6
NEG = -1e30
ADAM_LR, ADAM_B1, ADAM_B2, ADAM_EPS, ADAM_WD, ADAM_STEP = 0.001, 0.9, 0.999, 1e-08, 0.01, 10

F32 = jnp.float32
BF16 = jnp.bfloat16
MESH_ID = pl.DeviceIdType.MESH


def _dot(a, b):
    return jnp.dot(a, b, preferred_element_type=F32)


def _dot_nt(a, b):
    return lax.dot_general(a, b, (((1,), (1,)), ((), ())), preferred_element_type=F32)


def _dot_tn(a, b):
    return lax.dot_general(a, b, (((0,), (0,)), ((), ())), preferred_element_type=F32)


def _sig(z):
    return 1.0 / (1.0 + jnp.exp(-z))


def _swap_halves(z):
    lane = lax.broadcasted_iota(jnp.int32, z.shape, 1)
    return jnp.where((lane & 63) < 32, pltpu.roll(z, 96, 1), pltpu.roll(z, 32, 1))


def _row_spec(tm, width, col=0):
    return pl.BlockSpec((tm, width), lambda i: (i, col))


def _whole_vmem():
    return pl.BlockSpec(memory_space=pltpu.VMEM)


def _params(*sem, vmem_limit_bytes=None):
    return pltpu.CompilerParams(dimension_semantics=sem, vmem_limit_bytes=vmem_limit_bytes)


def _fwd_in_attn(x, g_pre, wt, cos_t, sin_t, tm):
    t = x.shape[0]
    seq_tiles = SEQ_LEN // tm

    def body(x_ref, g_ref, w_ref, c_ref, s_ref, h_ref, q_ref, kv_ref, g3_ref):
        xf = x_ref[...]
        r = lax.rsqrt(jnp.mean(xf * xf, axis=-1, keepdims=True) + RMS_EPS)
        hh = ((xf * r) * g_ref[...]).astype(BF16)
        h_ref[...] = hh
        c = c_ref[...]
        s = s_ref[...]

        def rope(z):
            return z * c + _swap_halves(z) * s

        q = _dot_nt(hh, w_ref[ROW_Q:ROW_Q + D, :])
        for j in range(D // 128):
            q_ref[:, j * 128:(j + 1) * 128] = (rope(q[:, j * 128:(j + 1) * 128]) * LOGIT_SCALE).astype(BF16)
        kv = _dot_nt(hh, w_ref[ROW_KV:ROW_KV + 256, :])
        kv_ref[:, 0:128] = rope(kv[:, 0:128]).astype(BF16)
        kv_ref[:, 128:256] = kv[:, 128:256].astype(BF16)
        for j in range(3):
            g3_ref[:, j * D:(j + 1) * D] = _dot_nt(hh, w_ref[ROW_ZA + j * D:ROW_ZA + (j + 1) * D, :]).astype(BF16)

    tab = pl.BlockSpec((tm, 128), lambda i: (i % seq_tiles, 0))
    return pl.pallas_call(
        body, name="fwd_in_attn", grid=(t // tm,),
        in_specs=[_row_spec(tm, D), pl.BlockSpec((1, D), lambda i: (0, 0)), _whole_vmem(), tab, tab],
        out_specs=[_row_spec(tm, D), _row_spec(tm, D), _row_spec(tm, 256), _row_spec(tm, 3 * D)],
        out_shape=[jax.ShapeDtypeStruct((t, D), BF16), jax.ShapeDtypeStruct((t, D), BF16),
                   jax.ShapeDtypeStruct((t, 256), BF16), jax.ShapeDtypeStruct((t, 3 * D), BF16)],
        compiler_params=_params("parallel"),
    )(x, g_pre, wt, cos_t, sin_t)


def _conv_forward(xc, bg, cg, zc, up6, up7, w_ref):
    tm = xc.shape[0]
    rows = lax.broadcasted_iota(jnp.int32, xc.shape, 0)
    u = cg * xc
    u_m1 = jnp.where(rows == 0, up7, pltpu.roll(u, 1, 0))
    u_m2 = jnp.where(rows == 0, up6, jnp.where(rows == 1, up7, pltpu.roll(u, 2, 0)))
    yconv = w_ref[0:1, :] * u_m2 + w_ref[1:2, :] * u_m1 + w_ref[2:3, :] * u
    sg = _sig(zc)
    sz = zc * sg
    co = bg * yconv
    del tm
    return u, u_m1, u_m2, yconv, sg, sz, co


def _fwd_in_conv(h, wt, wconv8, wpc, tm):
    t = h.shape[0]
    seq_tiles = SEQ_LEN // tm

    def body(h_ref, w_ref, wc_ref, wpc_ref, a4_ref, ya_ref, last_u_ref):
        hh = h_ref[...]
        xc, bg, cg, zc = (_dot_nt(hh, w_ref[j * D:(j + 1) * D, :]) for j in range(4))
        for j, z in enumerate((xc, bg, cg, zc)):
            a4_ref[:, j * D:(j + 1) * D] = z.astype(BF16)
        first = pl.program_id(0) % seq_tiles == 0
        up6 = jnp.where(first, 0.0, last_u_ref[6:7, :])
        up7 = jnp.where(first, 0.0, last_u_ref[7:8, :])
        u, _, _, _, _, sz, co = _conv_forward(xc, bg, cg, zc, up6, up7, wc_ref)
        last_u_ref[...] = u[tm - 8:tm, :]
        ya_ref[...] = _dot((sz * co).astype(BF16), wpc_ref[...]).astype(BF16)

    return pl.pallas_call(
        body, name="fwd_in_conv", grid=(t // tm,),
        in_specs=[_row_spec(tm, D), _whole_vmem(), pl.BlockSpec((8, D), lambda i: (0, 0)), _whole_vmem()],
        out_specs=[_row_spec(tm, 4 * D), _row_spec(tm, D)],
        out_shape=[jax.ShapeDtypeStruct((t, 4 * D), BF16), jax.ShapeDtypeStruct((t, D), BF16)],
        scratch_shapes=[pltpu.VMEM((8, D), F32)],
        compiler_params=_params("arbitrary"),
    )(h, wt, wconv8, wpc)


STACK = 4 * BLK


def _band_mask(first):
    qi = lax.broadcasted_iota(jnp.int32, (STACK, 2 * BLK), 0) & (BLK - 1)
    kj = lax.broadcasted_iota(jnp.int32, (STACK, 2 * BLK), 1)
    return (kj > qi) & (kj <= qi + BLK) & (kj >= jnp.where(first, BLK, 0))


def _masked_fill(sink_ref, g, e):
    kj = lax.broadcasted_iota(jnp.int32, (STACK, 2 * BLK), 1)
    sink = jnp.concatenate([jnp.full((BLK, 2 * BLK), sink_ref[0, 2 * (4 * g + jj) + e], F32) for jj in range(4)], axis=0)
    return jnp.where(kj == 0, sink, NEG)


def _padded_pair(before, own, other=0.0):
    z = jnp.concatenate([before, own], axis=0).astype(F32)
    z = jnp.where(lax.broadcasted_iota(jnp.int32, z.shape, 0) == 0, 0.0, z)
    zs = pltpu.roll(z, 64, 1)
    lo = lax.broadcasted_iota(jnp.int32, z.shape, 1) < 64
    fill = jnp.full_like(z, other)
    left = [jnp.where(lo, z, fill).astype(BF16), jnp.where(lo, zs, fill).astype(BF16)]
    right = [jnp.where(lo, fill, zs).astype(BF16), jnp.where(lo, fill, z).astype(BF16)]
    return left, right


def _exp_logits(s, valid, fill):
    s = jnp.where(valid, s, fill)
    return jnp.exp(s - jnp.max(s, axis=-1, keepdims=True))


ATTN_BLOCKS = 2


def _kv_blocks(kvc_ref, kvp_ref, b, col):
    own = kvc_ref[b * BLK:(b + 1) * BLK, col:col + 128]
    before = kvp_ref[:, col:col + 128] if b == 0 else kvc_ref[(b - 1) * BLK:b * BLK, col:col + 128]
    return before, own


def _fwd_attn(sinks, q, kv, g3):
    t = q.shape[0]
    tq = ATTN_BLOCKS * BLK
    seq_blocks = SEQ_LEN // BLK

    def body(sink_ref, q_ref, kvc_ref, kvp_ref, za_ref, attn_ref, ub_ref):
        lo = lax.broadcasted_iota(jnp.int32, (STACK, 128), 1) < 64
        for b in range(ATTN_BLOCKS):
            rows = slice(b * BLK, (b + 1) * BLK)
            valid = _band_mask((pl.program_id(0) * ATTN_BLOCKS + b) % seq_blocks == 0)
            k_pad = _padded_pair(*_kv_blocks(kvc_ref, kvp_ref, b, 0))
            v_one = _padded_pair(*_kv_blocks(kvc_ref, kvp_ref, b, 128), other=1.0)
            for g in range(2):
                qg = jnp.concatenate([q_ref[rows, j * 128:(j + 1) * 128] for j in range(4 * g, 4 * g + 4)], axis=0)
                pv = [_dot(_exp_logits(_dot_nt(qg, k_pad[e][g]), valid, _masked_fill(sink_ref, g, e)).astype(BF16),
                           v_one[e][g]) for e in range(2)]
                o = jnp.where(lo, pv[0], pv[1]) / pltpu.roll(jnp.where(lo, pv[1], pv[0]), 64, 1)
                for jj in range(4):
                    cols = slice((4 * g + jj) * 128, (4 * g + jj + 1) * 128)
                    oj = o[jj * BLK:(jj + 1) * BLK, :]
                    attn_ref[rows, cols] = oj.astype(BF16)
                    za = za_ref[rows, cols].astype(F32)
                    ub_ref[rows, cols] = (za * _sig(za) * oj).astype(BF16)

    return pl.pallas_call(
        body, name="fwd_attn", grid=(t // tq,),
        in_specs=[pl.BlockSpec(memory_space=pltpu.SMEM), _row_spec(tq, D), _row_spec(tq, 256),
                  pl.BlockSpec((BLK, 256), lambda i: (jnp.maximum(i * ATTN_BLOCKS - 1, 0), 0)), _row_spec(tq, D, 0)],
        out_specs=[_row_spec(tq, D), _row_spec(tq, D)],
        out_shape=[jax.ShapeDtypeStruct((t, D), BF16), jax.ShapeDtypeStruct((t, D), BF16)],
        compiler_params=_params("parallel"),
    )(sinks, q, kv, kv, g3)


def _fwd_out_bwd_head(ya, ub, g3, x, target, g_post, wpa, wout, tm, parts=1):
    t = x.shape[0]

    def body(ya_ref, ub_ref, ga_ref, gb_ref, x_ref, tgt_ref, gp_ref, wpa_ref, wout_ref,
             loss_ref, dout_ref, dya_ref, dub_ref, dgab_ref, dwout_ref, dwpa_ref, dgp_ref):
        @pl.when(pl.program_id(0) == 0)
        def _():
            loss_ref[...] = jnp.zeros_like(loss_ref)
            dwout_ref[...] = jnp.zeros_like(dwout_ref)
            dwpa_ref[...] = jnp.zeros_like(dwpa_ref)
            dgp_ref[...] = jnp.zeros_like(dgp_ref)

        g = gp_ref[...]
        sq = jnp.zeros((1, 1), F32)
        dgp = jnp.zeros((1, D), F32)
        mbs, dys, dybs = [], [], []
        for part in range(parts):
            rows = slice(part * (tm // parts), (part + 1) * (tm // parts))
            ub = ub_ref[rows, :]
            ya = ya_ref[rows, :].astype(F32)
            yb = _dot(ub, wpa_ref[...])
            sa = _sig(ga_ref[rows, :].astype(F32))
            sb = _sig(gb_ref[rows, :].astype(F32))
            mb = (sa * ya + sb * yb).astype(BF16)
            y = _dot(mb, wout_ref[...])
            r = lax.rsqrt(jnp.mean(y * y, axis=-1, keepdims=True) + RMS_EPS)
            n = y * r
            err = (x_ref[rows, :] + n * g) - tgt_ref[rows, :]
            sq = sq + jnp.sum(jnp.sum(err * err, axis=0, keepdims=True), axis=1, keepdims=True)
            dout = err * (1.0 / D)
            dout_ref[rows, :] = dout
            dgp = dgp + jnp.sum(dout * n, axis=0, keepdims=True)
            dn = dout * g
            dy = (r * (dn - n * jnp.mean(dn * n, axis=-1, keepdims=True))).astype(BF16)
            dm = _dot_nt(dy, wout_ref[...])
            dya_ref[rows, :] = (dm * sa).astype(BF16)
            dyb = (dm * sb).astype(BF16)
            dgab_ref[rows, 0:D] = (dm * ya * (sa * (1.0 - sa))).astype(BF16)
            dgab_ref[rows, D:2 * D] = (dm * yb * (sb * (1.0 - sb))).astype(BF16)
            dub_ref[rows, :] = _dot_nt(dyb, wpa_ref[...]).astype(BF16)
            mbs.append(mb)
            dys.append(dy)
            dybs.append(dyb)
        loss_ref[...] += sq * (0.5 / D)
        dgp_ref[0:1, :] += dgp
        dwout_ref[...] += _dot_tn(jnp.concatenate(mbs, axis=0), jnp.concatenate(dys, axis=0))
        dwpa_ref[...] += _dot_tn(ub_ref[...], jnp.concatenate(dybs, axis=0))

    return pl.pallas_call(
        body, name="fwd_out_bwd_head", grid=(t // tm,),
        in_specs=[_row_spec(tm, D), _row_spec(tm, D), _row_spec(tm, D, 1), _row_spec(tm, D, 2),
                  _row_spec(tm, D), _row_spec(tm, D), pl.BlockSpec((1, D), lambda i: (0, 0)),
                  _whole_vmem(), _whole_vmem()],
        out_specs=[pl.BlockSpec((8, 128), lambda i: (0, 0)), _row_spec(tm, D), _row_spec(tm, D), _row_spec(tm, D),
                   _row_spec(tm, 2 * D), _whole_vmem(), _whole_vmem(), pl.BlockSpec((8, D), lambda i: (0, 0))],
        out_shape=[jax.ShapeDtypeStruct((8, 128), F32), jax.ShapeDtypeStruct((t, D), F32),
                   jax.ShapeDtypeStruct((t, D), BF16), jax.ShapeDtypeStruct((t, D), BF16),
                   jax.ShapeDtypeStruct((t, 2 * D), BF16), jax.ShapeDtypeStruct((D, D), F32),
                   jax.ShapeDtypeStruct((D, D), F32), jax.ShapeDtypeStruct((8, D), F32)],
        compiler_params=_params("arbitrary", vmem_limit_bytes=V7X_VMEM_BYTES - (2 << 20)),
    )(ya, ub, g3, g3, x, target, g_post, wpa, wout)


def _bwd_attn(sinks, q, kv, attn, dub, g3, cos_t, sin_t, wt):
    t = q.shape[0]
    tq = ATTN_BLOCKS * BLK
    seq_blocks = SEQ_LEN // BLK

    def body(sink_ref, q_ref, kvc_ref, kvp_ref, attn_ref, dub_ref, za_ref, c_ref, s_ref, w_ref,
             dq_ref, dza_ref, dkv_own_ref, dkv_prev_ref, dsink_ref, dh_ref):
        @pl.when(pl.program_id(0) == 0)
        def _():
            dsink_ref[...] = jnp.zeros_like(dsink_ref)

        lo = lax.broadcasted_iota(jnp.int32, (STACK, 128), 1) < 64
        lane8 = lax.broadcasted_iota(jnp.int32, (8, 128), 1)
        lo2 = lax.broadcasted_iota(jnp.int32, (2 * BLK, 128), 1) < 64
        sink_row = lax.broadcasted_iota(jnp.int32, (2 * BLK, 128), 0) == 0
        dsink = jnp.zeros((8, 128), F32)
        for b in range(ATTN_BLOCKS):
            rows = slice(b * BLK, (b + 1) * BLK)
            valid = _band_mask((pl.program_id(0) * ATTN_BLOCKS + b) % seq_blocks == 0)
            k_pad = _padded_pair(*_kv_blocks(kvc_ref, kvp_ref, b, 0))
            v_pad = _padded_pair(*_kv_blocks(kvc_ref, kvp_ref, b, 128))
            c = c_ref[rows, :]
            s = s_ref[rows, :]
            dk_acc, dv_acc = [], []
            for g in range(2):
                qg, dog = [], []
                for j in range(4 * g, 4 * g + 4):
                    cols = slice(j * 128, (j + 1) * 128)
                    za = za_ref[rows, cols].astype(F32)
                    sg = _sig(za)
                    dub = dub_ref[rows, cols].astype(F32)
                    dza_ref[rows, cols] = (
                        dub * attn_ref[rows, cols].astype(F32) * (sg * (1.0 + za * (1.0 - sg)))).astype(BF16)
                    dog.append((dub * (za * sg)).astype(BF16))
                    qg.append(q_ref[rows, cols])
                qg = jnp.concatenate(qg, axis=0)
                dog = jnp.concatenate(dog, axis=0)
                dq = jnp.zeros((STACK, 128), F32)
                ds_both, p_both = [], []
                for e in range(2):
                    p = _exp_logits(_dot_nt(qg, k_pad[e][g]), valid, _masked_fill(sink_ref, g, e))
                    p = p / jnp.sum(p, axis=-1, keepdims=True)
                    dp = _dot_nt(dog, v_pad[e][g])
                    ds = p * (dp - jnp.sum(p * dp, axis=-1, keepdims=True))
                    for jj in range(4):
                        tot = jnp.sum(ds[jj * BLK:(jj + 1) * BLK, 0:1], axis=0, keepdims=True)
                        dsink = dsink + jnp.where(lane8 == 2 * (4 * g + jj) + e, tot, 0.0)
                    ds = ds.astype(BF16)
                    dq = dq + _dot(ds, k_pad[e][g])
                    ds_both.append(ds)
                    p_both.append(p.astype(BF16))
                zero = jnp.zeros_like(qg)
                q2 = jnp.concatenate([jnp.where(lo, qg, zero), jnp.where(lo, zero, qg)], axis=0)
                do2 = jnp.concatenate([jnp.where(lo, dog, zero), jnp.where(lo, zero, dog)], axis=0)
                dk_acc.append(_dot_tn(q2, jnp.concatenate(ds_both, axis=0)).T)
                dv_acc.append(_dot_tn(do2, jnp.concatenate(p_both, axis=0)).T)
                for jj in range(4):
                    cols = slice((4 * g + jj) * 128, (4 * g + jj + 1) * 128)
                    dqj = dq[jj * BLK:(jj + 1) * BLK, :] * LOGIT_SCALE
                    dq_ref[rows, cols] = (dqj * c - _swap_halves(dqj) * s).astype(BF16)
            for col, acc in ((0, dk_acc), (128, dv_acc)):
                both = jnp.where(lo2, acc[0] + pltpu.roll(acc[0], 64, 1), acc[1] + pltpu.roll(acc[1], 64, 1))
                both = jnp.where(sink_row, 0.0, both)
                dkv_prev_ref[rows, col:col + 128] = both[0:BLK, :]
                dkv_own_ref[rows, col:col + 128] = both[BLK:2 * BLK, :]
        dsink_ref[...] += dsink
        dh_ref[...] = _dot(dq_ref[...], w_ref[ROW_Q:ROW_KV, :]) + _dot(dza_ref[...], w_ref[ROW_ZA:ROW_GA, :])

    tab = pl.BlockSpec((tq, 128), lambda i: (i % (SEQ_LEN // tq), 0))
    return pl.pallas_call(
        body, name="bwd_attn", grid=(t // tq,),
        in_specs=[pl.BlockSpec(memory_space=pltpu.SMEM), _row_spec(tq, D), _row_spec(tq, 256),
                  pl.BlockSpec((BLK, 256), lambda i: (jnp.maximum(i * ATTN_BLOCKS - 1, 0), 0)),
                  _row_spec(tq, D), _row_spec(tq, D), _row_spec(tq, D, 0), tab, tab, _whole_vmem()],
        out_specs=[_row_spec(tq, D), _row_spec(tq, D), _row_spec(tq, 256), _row_spec(tq, 256),
                   pl.BlockSpec((8, 128), lambda i: (0, 0)), _row_spec(tq, D)],
        out_shape=[jax.ShapeDtypeStruct((t, D), BF16), jax.ShapeDtypeStruct((t, D), BF16),
                   jax.ShapeDtypeStruct((t, 256), F32), jax.ShapeDtypeStruct((t, 256), F32),
                   jax.ShapeDtypeStruct((8, 128), F32), jax.ShapeDtypeStruct((t, D), F32)],
        compiler_params=_params("arbitrary"),
    )(sinks, q, kv, kv, attn, dub, g3, cos_t, sin_t, wt)


def _bwd_kv_finish(dkv_own, dkv_prev, cos_t, sin_t):
    t = dkv_own.shape[0]
    tm = 512
    seq_tiles = SEQ_LEN // tm
    n_blocks = t // BLK

    def body(own_ref, same_ref, nxt_ref, c_ref, s_ref, out_ref):
        keep = jnp.where(pl.program_id(0) % seq_tiles == seq_tiles - 1, 0.0, 1.0)
        shifted = jnp.concatenate([same_ref[BLK:tm, :], nxt_ref[...] * keep], axis=0)
        tot = own_ref[...] + shifted
        dk = tot[:, 0:128]
        out_ref[:, 0:128] = (dk * c_ref[...] - _swap_halves(dk) * s_ref[...]).astype(BF16)
        out_ref[:, 128:256] = tot[:, 128:256].astype(BF16)

    tab = pl.BlockSpec((tm, 128), lambda i: (i % seq_tiles, 0))
    return pl.pallas_call(
        body, name="bwd_kv_finish", grid=(t // tm,),
        in_specs=[_row_spec(tm, 256), _row_spec(tm, 256),
                  pl.BlockSpec((BLK, 256), lambda i: (jnp.minimum((i + 1) * (tm // BLK), n_blocks - 1), 0)), tab, tab],
        out_specs=_row_spec(tm, 256),
        out_shape=jax.ShapeDtypeStruct((t, 256), BF16),
        compiler_params=_params("parallel"),
    )(dkv_own, dkv_prev, dkv_prev, cos_t, sin_t)


STAGE_ROWS = 256


def _bwd_conv(dya, a4, h, wconv8, wpc, tm, parts):
    t = a4.shape[0]
    n_t = t // tm
    sub = tm // parts
    seq_tiles = SEQ_LEN // tm

    def body(dya_ref, xc_ref, bg_ref, cg_ref, zc_ref, xcp_ref, cgp_ref, w_ref, wpc_ref, h_ref,
             da4_ref, dwpc_ref, dwc_ref, o32_ref, o16_ref, acc_ref, stage_ref, later_ref, sems):
        step = pl.program_id(0)
        tile = n_t - 1 - step

        @pl.when(step == 0)
        def _():
            dwpc_ref[...] = jnp.zeros_like(dwpc_ref)
            dwc_ref[...] = jnp.zeros_like(dwc_ref)
            acc_ref[...] = jnp.zeros_like(acc_ref)

        keep_prev = jnp.where(tile % seq_tiles == 0, 0.0, 1.0)
        ends_sequence = tile % seq_tiles == seq_tiles - 1

        def part(p, later):
            r0 = p * sub
            here = slice(r0, r0 + sub)
            if p == 0:
                u_prev = cgp_ref[14:16, :].astype(F32) * xcp_ref[14:16, :].astype(F32) * keep_prev
            else:
                u_prev = cg_ref[r0 - 2:r0, :].astype(F32) * xc_ref[r0 - 2:r0, :].astype(F32)
            xc = xc_ref[here, :].astype(F32)
            bg = bg_ref[here, :].astype(F32)
            cg = cg_ref[here, :].astype(F32)
            zc = zc_ref[here, :].astype(F32)
            u, u_m1, u_m2, yconv, sg, sz, co = _conv_forward(xc, bg, cg, zc, u_prev[0:1, :], u_prev[1:2, :], w_ref)
            ua = (sz * co).astype(BF16)
            dua = _dot_nt(dya_ref[here, :], wpc_ref[...])
            da4_ref[here, 3 * D:4 * D] = (dua * co * (sg * (1.0 + zc * (1.0 - sg)))).astype(BF16)
            dco = dua * sz
            da4_ref[here, D:2 * D] = (dco * yconv).astype(BF16)
            dyc = dco * bg
            dwc = jnp.concatenate([jnp.sum(dyc * s, axis=0, keepdims=True) for s in (u_m2, u_m1, u)], axis=0)
            rows = lax.broadcasted_iota(jnp.int32, xc.shape, 0)
            n0 = later[0:1, :]
            n1 = later[1:2, :]
            dyc_p1 = jnp.where(rows == sub - 1, n0, pltpu.roll(dyc, sub - 1, 0))
            dyc_p2 = jnp.where(rows == sub - 2, n0, jnp.where(rows == sub - 1, n1, pltpu.roll(dyc, sub - 2, 0)))
            du = w_ref[2:3, :] * dyc + w_ref[1:2, :] * dyc_p1 + w_ref[0:1, :] * dyc_p2
            da4_ref[here, 0:D] = (du * cg).astype(BF16)
            da4_ref[here, 2 * D:3 * D] = (du * xc).astype(BF16)
            return ua, dwc, dyc[0:8, :]

        later = jnp.where(ends_sequence, 0.0, later_ref[...])
        uas, dwc = [], jnp.zeros((3, D), F32)
        for p in reversed(range(parts)):
            ua, dwc_p, later = part(p, later)
            uas.insert(0, ua)
            dwc = dwc + dwc_p
        later_ref[...] = later
        dwpc_ref[...] += _dot_tn(jnp.concatenate(uas, axis=0), dya_ref[...])
        dwc_ref[0:3, :] += dwc
        for j in range(4):
            acc_ref[j * D:(j + 1) * D, :] += _dot_tn(da4_ref[:, j * D:(j + 1) * D], h_ref[...])

        @pl.when(step == n_t - 1)
        def _():
            c32 = pltpu.make_async_copy(acc_ref, o32_ref.at[pl.ds(0, 4 * D)], sems.at[0])
            c32.start()
            for j in range(4 * D // STAGE_ROWS):
                rows = pl.ds(j * STAGE_ROWS, STAGE_ROWS)
                stage_ref[...] = acc_ref[rows, :].astype(BF16)
                c16 = pltpu.make_async_copy(stage_ref, o16_ref.at[rows], sems.at[1])
                c16.start()
                c16.wait()
            c32.wait()

    def rows_of_tile(width, col=0):
        return pl.BlockSpec((tm, width), lambda s: (n_t - 1 - s, col))

    def prev(col):
        return pl.BlockSpec((16, D), lambda s: (jnp.maximum((n_t - 1 - s) * (tm // 16) - 1, 0), col))

    hbm = pl.BlockSpec(memory_space=pl.ANY)
    out = pl.pallas_call(
        body, name="bwd_conv", grid=(n_t,),
        in_specs=[rows_of_tile(D), rows_of_tile(D, 0), rows_of_tile(D, 1), rows_of_tile(D, 2), rows_of_tile(D, 3),
                  prev(0), prev(2), pl.BlockSpec((8, D), lambda s: (0, 0)), _whole_vmem(), rows_of_tile(D)],
        out_specs=[rows_of_tile(4 * D), _whole_vmem(), pl.BlockSpec((8, D), lambda s: (0, 0)), hbm, hbm],
        out_shape=[jax.ShapeDtypeStruct((t, 4 * D), BF16), jax.ShapeDtypeStruct((D, D), F32),
                   jax.ShapeDtypeStruct((8, D), F32), jax.ShapeDtypeStruct((D_IN, D), F32),
                   jax.ShapeDtypeStruct((D_IN, D), BF16)],
        scratch_shapes=[pltpu.VMEM((4 * D, D), F32), pltpu.VMEM((STAGE_ROWS, D), BF16), pltpu.VMEM((8, D), F32),
                        pltpu.SemaphoreType.DMA((2,))],
        compiler_params=pltpu.CompilerParams(dimension_semantics=("arbitrary",), vmem_limit_bytes=V7X_VMEM_BYTES - (2 << 20)),
    )(dya, a4, a4, a4, a4, a4, a4, wconv8, wpc, h)
    return out[0], out[1], out[2], (out[3], out[4])


def _bwd_dh(da4, dh_part, dkv, dgab, wt, x, g_pre, dout, tm):
    t = x.shape[0]

    def body(da4_ref, dhp_ref, dkv_ref, dgab_ref, w_ref, x_ref, g_ref, dout_ref, gx_ref, dg_ref):
        @pl.when(pl.program_id(0) == 0)
        def _():
            dg_ref[...] = jnp.zeros_like(dg_ref)

        dh = dhp_ref[...] + _dot(da4_ref[...], w_ref[0:ROW_Q, :])
        dh += _dot(dkv_ref[...], w_ref[ROW_KV:ROW_ZA, :])
        dh += _dot(dgab_ref[...], w_ref[ROW_GA:D_IN, :])
        xf = x_ref[...]
        r = lax.rsqrt(jnp.mean(xf * xf, axis=-1, keepdims=True) + RMS_EPS)
        xn = xf * r
        dg_ref[0:1, :] += jnp.sum(dh * xn, axis=0, keepdims=True)
        dxn = dh * g_ref[...]
        gx_ref[...] = dout_ref[...] + r * (dxn - xn * jnp.mean(dxn * xn, axis=-1, keepdims=True))

    return pl.pallas_call(
        body, name="bwd_dh", grid=(t // tm,),
        in_specs=[_row_spec(tm, 4 * D), _row_spec(tm, D), _row_spec(tm, 256), _row_spec(tm, 2 * D),
                  _whole_vmem(), _row_spec(tm, D), pl.BlockSpec((1, D), lambda i: (0, 0)), _row_spec(tm, D)],
        out_specs=[_row_spec(tm, D), pl.BlockSpec((8, D), lambda i: (0, 0))],
        out_shape=[jax.ShapeDtypeStruct((t, D), F32), jax.ShapeDtypeStruct((8, D), F32)],
        compiler_params=_params("arbitrary"),
    )(da4, dh_part, dkv, dgab, wt, x, g_pre, dout)


def _bwd_dw_in(h, piece, row0, nb, tm, name, prev):
    t, n = piece.shape
    n_t = t // tm

    def body(*refs):
        h_ref, p_ref = refs[0], refs[1]
        o32_ref, o16_ref, acc_ref, acc16_ref, sems = refs[-5:]
        j, i = pl.program_id(0), pl.program_id(1)

        @pl.when(i == 0)
        def _():
            acc_ref[...] = jnp.zeros_like(acc_ref)

        acc_ref[...] += _dot_tn(p_ref[...], h_ref[...])

        @pl.when(i == n_t - 1)
        def _():
            acc16_ref[...] = acc_ref[...].astype(BF16)
            rows = pl.ds(pl.multiple_of(row0 + j * nb, 16), nb)
            c32 = pltpu.make_async_copy(acc_ref, o32_ref.at[rows], sems.at[0])
            c16 = pltpu.make_async_copy(acc16_ref, o16_ref.at[rows], sems.at[1])
            c32.start()
            c16.start()
            c32.wait()
            c16.wait()

    hbm = pl.BlockSpec(memory_space=pl.ANY)
    carried = [] if prev is None else list(prev)
    return pl.pallas_call(
        body, name=name, grid=(n // nb, n_t),
        in_specs=[pl.BlockSpec((tm, D), lambda j, i: (i, 0)), pl.BlockSpec((tm, nb), lambda j, i: (i, j))]
        + [hbm] * len(carried),
        out_specs=[hbm, hbm],
        out_shape=[jax.ShapeDtypeStruct((D_IN, D), F32), jax.ShapeDtypeStruct((D_IN, D), BF16)],
        scratch_shapes=[pltpu.VMEM((nb, D), F32), pltpu.VMEM((nb, D), BF16), pltpu.SemaphoreType.DMA((2,))],
        input_output_aliases={2: 0, 3: 1} if carried else {},
        compiler_params=_params("arbitrary", "arbitrary"),
    )(h, piece, *carried)


def _place():
    x, y, c = lax.axis_index("x"), lax.axis_index("y"), lax.axis_index("c")
    return x, y, c, 4 * x + 2 * y + c


def _peer(x, y, c, k):
    return (1 - x if k & 4 else x, 1 - y if k & 2 else y, 1 - c if k & 1 else c)


ICI_MASKS = (4, 2, 6)


def _all_gather(shards):
    n = len(shards)

    def body(*refs):
        src, dst = refs[:n], refs[n:2 * n]
        send_sems, recv_sems, local_sems = refs[2 * n:]
        x, y, c, me = _place()
        sibling = _peer(x, y, c, 1)

        def copy(a, s, block, to, own=False):
            return pltpu.make_async_remote_copy(
                src_ref=src[a] if own else dst[a].at[block], dst_ref=dst[a].at[block],
                send_sem=send_sems.at[a * 7 + s], recv_sem=recv_sems.at[a * 7 + s], device_id=to, device_id_type=MESH_ID)

        local = [pltpu.make_async_copy(src[a], dst[a].at[me], local_sems.at[a]) for a in range(n)]
        for cp in local:
            cp.start()
        started = [copy(a, 0, me, sibling, own=True) for a in range(n)]
        started += [copy(a, 1 + j, me, _peer(x, y, c, k), own=True) for j, k in enumerate(ICI_MASKS) for a in range(n)]
        for cp in started:
            cp.start()
        for j, k in enumerate(ICI_MASKS):
            for a in range(n):
                copy(a, 1 + j, me ^ k, sibling).wait_recv()
                fwd = copy(a, 4 + j, me ^ k, sibling)
                fwd.start()
                started.append(fwd)
        for a in range(n):
            copy(a, 0, me ^ 1, sibling).wait_recv()
        for j, k in enumerate(ICI_MASKS):
            for a in range(n):
                copy(a, 4 + j, me ^ 1 ^ k, sibling).wait_recv()
        for cp in started:
            cp.wait_send()
        for cp in local:
            cp.wait()

    hbm = pl.BlockSpec(memory_space=pl.ANY)
    return pl.pallas_call(
        body, name="all_gather_weights",
        in_specs=[hbm] * n, out_specs=[hbm] * n,
        out_shape=[jax.ShapeDtypeStruct((N_DEV,) + s.shape, s.dtype) for s in shards],
        scratch_shapes=[pltpu.SemaphoreType.DMA((7 * n,)), pltpu.SemaphoreType.DMA((7 * n,)),
                        pltpu.SemaphoreType.DMA((n,))],
    )(*shards)


def _direct_copies(src, land, send_sems, recv_sems):
    x, y, c, me = _place()
    return [pltpu.make_async_remote_copy(
        src_ref=src[a], dst_ref=land[a].at[me], send_sem=send_sems.at[a * 7 + k - 1],
        recv_sem=recv_sems.at[a * 7 + k - 1], device_id=_peer(x, y, c, k), device_id_type=MESH_ID)
        for k in range(1, N_DEV) for a in range(len(src))]


def _gather_start(shards, name):
    n = len(shards)

    def body(*refs):
        src, land = refs[:n], refs[n:2 * n]
        send_sems, recv_sems = refs[2 * n], refs[2 * n + 1]
        token_ref = refs[-1]
        for cp in _direct_copies(src, land, send_sems, recv_sems):
            cp.start()
        token_ref[...] = jnp.zeros_like(token_ref)

    hbm = pl.BlockSpec(memory_space=pltpu.HBM)
    sem = pl.BlockSpec(memory_space=pltpu.SEMAPHORE)
    lands = [lax.empty((N_DEV,) + s.shape, s.dtype) for s in shards]
    out = pl.pallas_call(
        body, name=name + "_start",
        out_shape=(pltpu.SemaphoreType.DMA((7 * n,)), pltpu.SemaphoreType.DMA((7 * n,)),
                   *[pltpu.HBM(s.shape, s.dtype) for s in shards], *[pltpu.HBM(s.shape, s.dtype) for s in lands],
                   jax.ShapeDtypeStruct((8, 128), F32)),
        in_specs=[hbm] * (2 * n), out_specs=(sem, sem, *[hbm] * (2 * n), _whole_vmem()),
        input_output_aliases={a: 2 + a for a in range(2 * n)},
        compiler_params=pltpu.CompilerParams(has_side_effects=pltpu.SideEffectType.DATAFLOW_SIDE_EFFECTING),
    )(*[pltpu.with_memory_space_constraint(s, pltpu.HBM) for s in list(shards) + lands])
    return out[0], out[1], out[2:2 + n], out[2 + n:2 + 2 * n], out[-1]


def _gather_wait(send_sems, recv_sems, flying, lands, after, name):
    n = len(flying)

    def body(*refs):
        src, land = refs[:n], refs[n:2 * n]
        for cp in _direct_copies(src, land, refs[2 * n], refs[2 * n + 1]):
            cp.wait_send()
            cp.wait_recv()

    hbm = pl.BlockSpec(memory_space=pltpu.HBM)
    sem = pl.BlockSpec(memory_space=pltpu.SEMAPHORE)
    out = pl.pallas_call(
        body, name=name + "_wait",
        out_shape=tuple(pltpu.HBM(s.shape, s.dtype) for s in list(flying) + list(lands)),
        in_specs=[hbm] * (2 * n) + [sem, sem, pl.BlockSpec(memory_space=pl.ANY)], out_specs=tuple([hbm] * (2 * n)),
        input_output_aliases={a: a for a in range(2 * n)},
        compiler_params=pltpu.CompilerParams(has_side_effects=pltpu.SideEffectType.DATAFLOW_SIDE_EFFECTING),
    )(*flying, *lands, send_sems, recv_sems, after)
    return out[n:]


def _exchange_sibling(by_dest):
    n = len(by_dest)

    def body(*refs):
        src, dst = refs[:n], refs[n:2 * n]
        send_sems, recv_sems = refs[2 * n:]
        x, y, c, _ = _place()
        sibling = _peer(x, y, c, 1)
        copies = [pltpu.make_async_remote_copy(
            src_ref=src[a].at[2 * p + (1 - c)], dst_ref=dst[a].at[p], send_sem=send_sems.at[a * 4 + p],
            recv_sem=recv_sems.at[a * 4 + p], device_id=sibling, device_id_type=MESH_ID)
            for a in range(n) for p in range(4)]
        for cp in copies:
            cp.start()
        for cp in copies:
            cp.wait_recv()
        for cp in copies:
            cp.wait_send()

    hbm = pl.BlockSpec(memory_space=pl.ANY)
    return pl.pallas_call(
        body, name="exchange_sibling", in_specs=[hbm] * n, out_specs=[hbm] * n,
        out_shape=[jax.ShapeDtypeStruct((4,) + s.shape[1:], s.dtype) for s in by_dest],
        scratch_shapes=[pltpu.SemaphoreType.DMA((4 * n,)), pltpu.SemaphoreType.DMA((4 * n,))],
    )(*by_dest)


def _chip_copies(src, land, send_sems, recv_sems):
    x, y, c, _ = _place()
    chip = 2 * x + y
    return [pltpu.make_async_remote_copy(
        src_ref=src[a].at[chip ^ (k >> 1)], dst_ref=land[a].at[j], send_sem=send_sems.at[a * 3 + j],
        recv_sem=recv_sems.at[a * 3 + j], device_id=_peer(x, y, c, k), device_id_type=MESH_ID)
        for j, k in enumerate(ICI_MASKS) for a in range(len(src))]


def _exchange_chips_start(by_chip):
    n = len(by_chip)

    def body(*refs):
        src, land = refs[:n], refs[n:2 * n]
        send_sems, recv_sems = refs[2 * n], refs[2 * n + 1]
        token_ref = refs[-1]
        for cp in _chip_copies(src, land, send_sems, recv_sems):
            cp.start()
        token_ref[...] = jnp.zeros_like(token_ref)

    hbm = pl.BlockSpec(memory_space=pltpu.HBM)
    sem = pl.BlockSpec(memory_space=pltpu.SEMAPHORE)
    lands = [lax.empty((3,) + s.shape[1:], s.dtype) for s in by_chip]
    out = pl.pallas_call(
        body, name="exchange_chips_start",
        out_shape=(pltpu.SemaphoreType.DMA((3 * n,)), pltpu.SemaphoreType.DMA((3 * n,)),
                   *[pltpu.HBM(s.shape, s.dtype) for s in by_chip], *[pltpu.HBM(s.shape, s.dtype) for s in lands],
                   jax.ShapeDtypeStruct((8, 128), F32)),
        in_specs=[hbm] * (2 * n), out_specs=(sem, sem, *[hbm] * (2 * n), _whole_vmem()),
        input_output_aliases={a: 2 + a for a in range(2 * n)},
        compiler_params=pltpu.CompilerParams(has_side_effects=pltpu.SideEffectType.DATAFLOW_SIDE_EFFECTING),
    )(*[pltpu.with_memory_space_constraint(s, pltpu.HBM) for s in list(by_chip) + lands])
    return out[0], out[1], out[2:2 + n], out[2 + n:2 + 2 * n], out[-1]


def _exchange_chips_wait(send_sems, recv_sems, flying, lands, after):
    n = len(flying)

    def body(*refs):
        src, land = refs[:n], refs[n:2 * n]
        send_sems_ref, recv_sems_ref = refs[2 * n], refs[2 * n + 1]
        for cp in _chip_copies(src, land, send_sems_ref, recv_sems_ref):
            cp.wait_send()
            cp.wait_recv()

    hbm = pl.BlockSpec(memory_space=pltpu.HBM)
    sem = pl.BlockSpec(memory_space=pltpu.SEMAPHORE)
    out = pl.pallas_call(
        body, name="exchange_chips_wait",
        out_shape=tuple(pltpu.HBM(s.shape, s.dtype) for s in list(flying) + list(lands)),
        in_specs=[hbm] * (2 * n) + [sem, sem, pl.BlockSpec(memory_space=pl.ANY)], out_specs=tuple([hbm] * (2 * n)),
        input_output_aliases={a: a for a in range(2 * n)},
        compiler_params=pltpu.CompilerParams(has_side_effects=pltpu.SideEffectType.DATAFLOW_SIDE_EFFECTING),
    )(*flying, *lands, send_sems, recv_sems, after)
    return out[n:]


def _adamw_math(w, g, m, v):
    m = ADAM_B1 * m + (1.0 - ADAM_B1) * g
    v = ADAM_B2 * v + (1.0 - ADAM_B2) * (g * g)
    m_hat = m / (1.0 - ADAM_B1 ** ADAM_STEP)
    v_hat = v / (1.0 - ADAM_B2 ** ADAM_STEP)
    return -ADAM_LR * (m_hat / (jnp.sqrt(v_hat) + ADAM_EPS) + ADAM_WD * w), m, v


def _pair_sum(owns, recvs, c_arr, tr, name):
    n = len(owns)
    _, rows, cols = owns[0].shape

    def body(c_ref, *refs):
        for a in range(n):
            s = refs[a][...] + refs[n + a][...].astype(F32)
            refs[2 * n + a][...] = s
            refs[3 * n + a][...] = s.astype(BF16)

    by_chip = pl.BlockSpec((None, tr, cols), lambda p, i, c_ref: (p, i, 0))
    mine = pl.BlockSpec((None, tr, cols), lambda p, i, c_ref: (2 * p + c_ref[0], i, 0))
    out = pl.pallas_call(
        body, name=name,
        grid_spec=pltpu.PrefetchScalarGridSpec(
            num_scalar_prefetch=1, grid=(4, rows // tr), in_specs=[mine] * n + [by_chip] * n, out_specs=[by_chip] * (2 * n)),
        out_shape=[jax.ShapeDtypeStruct((4, rows, cols), F32)] * n + [jax.ShapeDtypeStruct((4, rows, cols), BF16)] * n,
        compiler_params=_params("parallel", "parallel"),
    )(c_arr, *owns, *recvs)
    return out[:n], out[n:]


def _chip_sum(pairs, recvs, chip_arr, tr, name, adam=None):
    n = len(pairs)
    _, rows, cols = pairs[0].shape
    n_state = 0 if adam is None else 3 * n

    def body(chip_ref, *refs):
        outs = refs[2 * n + n_state:]
        for a in range(n):
            g = refs[a][...]
            for j in range(3):
                g = g + refs[n + a][j].astype(F32)
            outs[a][...] = g
            if adam is not None:
                w_ref, m_ref, v_ref = (refs[2 * n + s * n + a] for s in range(3))
                outs[n + a][...], outs[2 * n + a][...], outs[3 * n + a][...] = _adamw_math(w_ref[...], g, m_ref[...], v_ref[...])

    blk = pl.BlockSpec((tr, cols), lambda i, chip_ref: (i, 0))
    n_out = n if adam is None else 4 * n
    out = pl.pallas_call(
        body, name=name,
        grid_spec=pltpu.PrefetchScalarGridSpec(
            num_scalar_prefetch=1, grid=(rows // tr,),
            in_specs=[pl.BlockSpec((None, tr, cols), lambda i, chip_ref: (chip_ref[0], i, 0))] * n
            + [pl.BlockSpec((3, tr, cols), lambda i, chip_ref: (0, i, 0))] * n + [blk] * n_state,
            out_specs=[blk] * n_out),
        out_shape=[jax.ShapeDtypeStruct((rows, cols), F32)] * n_out,
        compiler_params=_params("parallel"),
    )(chip_arr, *pairs, *recvs, *([] if adam is None else [t for group in adam for t in group]))
    return out if adam is None else (out[:n], out[n:2 * n], out[2 * n:3 * n], out[3 * n:])


def _adamw(ws, gs, ms, vs, name):
    n = len(ws)

    def body(*refs):
        for a in range(n):
            w_ref, g_ref, m_ref, v_ref = (refs[s * n + a] for s in range(4))
            refs[4 * n + a][...], refs[5 * n + a][...], refs[6 * n + a][...] = _adamw_math(
                w_ref[...], g_ref[...], m_ref[...], v_ref[...])

    out = pl.pallas_call(body, name=name, out_shape=[jax.ShapeDtypeStruct(w.shape, F32) for w in ws] * 3)(
        *ws, *gs, *ms, *vs)
    return out[:n], out[n:2 * n], out[2 * n:]


def _sum_small(small_all):
    def body(s_ref, o_ref):
        g = s_ref[0]
        for d in range(1, N_DEV):
            g = g + s_ref[d]
        o_ref[...] = g

    return pl.pallas_call(body, name="sum_small", out_shape=jax.ShapeDtypeStruct(small_all.shape[1:], F32))(small_all)


def _rope_tables():
    inv_freq = ROPE_THETA ** (-jnp.arange(0, HEAD_DIM, 2, dtype=F32) / HEAD_DIM)
    ang = jnp.arange(SEQ_LEN).astype(F32)[:, None] * inv_freq[None, :]
    cos, sin = jnp.cos(ang), jnp.sin(ang)
    return jnp.tile(cos, (1, 4)), jnp.tile(jnp.concatenate([-sin, sin], axis=1), (1, 2))


def _local_step(x, target, g_pre, g_post, sinks, wt, wconv, squares, start_exchange=None):
    cos_t, sin_t = _rope_tables()
    wconv8 = jnp.pad(wconv, ((0, 5), (0, 0)))
    h, q, kv, g3 = _fwd_in_attn(x, g_pre, wt, cos_t, sin_t, 512)
    wpc, wpa, wout = squares(kv)
    a4, ya = _fwd_in_conv(h, wt, wconv8, wpc, 512)
    attn, ub = _fwd_attn(sinks, q, kv, g3)
    loss8, dout, dya, dub, dgab, dwout, dwpa, dgpost8 = _fwd_out_bwd_head(ya, ub, g3, x, target, g_post, wpa, wout, 512)
    dq, dza, dkv_own, dkv_prev, dsink8, dh_part = _bwd_attn(sinks, q, kv, attn, dub, g3, cos_t, sin_t, wt)
    dkv = _bwd_kv_finish(dkv_own, dkv_prev, cos_t, sin_t)
    da4, dwpc, dwconv8, dwt = _bwd_conv(dya, a4, h, wconv8, wpc, 512, 2)
    dwt = _bwd_dw_in(h, dq, ROW_Q, 1024, 1024, "bwd_dw_in_q", dwt)
    dwt = _bwd_dw_in(h, dkv, ROW_KV, 256, 1024, "bwd_dw_in_kv", dwt)
    dwt = _bwd_dw_in(h, dza, ROW_ZA, 1024, 1024, "bwd_dw_in_za", dwt)
    dwt32, dwt16 = _bwd_dw_in(h, dgab, ROW_GA, 1024, 1024, "bwd_dw_in_gates", dwt)
    token, pending = (None, None) if start_exchange is None else start_exchange(dwt32, dwt16, dwpc, dwpa, dwout)
    g_pre_after = g_pre if token is None else g_pre + token[0:1, 0:1]
    grad_x, dgpre8 = _bwd_dh(da4, dh_part, dkv, dgab, wt, x, g_pre_after, dout, 512)
    small = jnp.concatenate([dgpre8, dgpost8, jnp.pad(dsink8, ((0, 0), (0, D - 128))), dwconv8,
                             jnp.pad(loss8, ((0, 0), (0, D - 128)))], axis=0)
    return loss8[0, 0], grad_x, dwt32, dwt16, dwpc, dwpa, dwout, small, pending


def kernel(x, g_pre, g_post, w_in, w_conv, sinks, w_proj_conv, w_proj_attn, w_out, loss_target, m_g_pre, m_g_post, m_w_in, m_w_conv, m_sinks, m_w_proj_conv, m_w_proj_attn, m_w_out, v_g_pre, v_g_post, v_w_in, v_w_conv, v_sinks, v_w_proj_conv, v_w_proj_attn, v_w_out):
    batch = x.shape[0]
    mx, my, mc, me = _place()
    c_arr = jnp.reshape(mc, (1,)).astype(jnp.int32)
    chip_arr = jnp.reshape(2 * mx + my, (1,)).astype(jnp.int32)

    g_wt, g_conv = _all_gather([w_in[0].T.astype(BF16), jnp.pad(w_conv[0], ((0, 5), (0, 0)))])
    wt = g_wt.reshape(D_IN, D)
    wconv = g_conv[:, 0:3, :].transpose(1, 0, 2).reshape(3, D)
    sq_mine = [w.astype(BF16) for w in (w_proj_conv[0], w_proj_attn[0], w_out[0])]
    wt, sq_mine = lax.optimization_barrier((wt, sq_mine))
    sq_send, sq_recv, sq_flying, sq_lands, sq_token = _gather_start(sq_mine, "gather_squares")

    def squares(after):
        got = _gather_wait(sq_send, sq_recv, sq_flying, sq_lands, after, "gather_squares")
        return [lax.dynamic_update_index_in_dim(full, mine, me, 0).reshape(D, D) for full, mine in zip(got, sq_mine)]

    def start_exchange(dwt32, dwt16, dwpc, dwpa, dwout):
        own_sq = [g.reshape(N_DEV, SHARD_SQ, D) for g in (dwpc, dwpa, dwout)]
        own_in = dwt32.reshape(N_DEV, SHARD_IN, D)
        from_sibling = _exchange_sibling([dwt16.reshape(N_DEV, SHARD_IN, D)] + [g.astype(BF16) for g in own_sq])
        in32, in16 = _pair_sum([own_in], from_sibling[:1], c_arr, SHARD_IN // 2, "pair_sum_w_in")
        sq32, sq16 = _pair_sum(own_sq, from_sibling[1:], c_arr, SHARD_SQ, "pair_sum_squares")
        send_sems, recv_sems, flying, lands, token = _exchange_chips_start(list(in16) + list(sq16))
        return token, (send_sems, recv_sems, flying, lands, in32, sq32)

    _, grad_x, _, _, _, _, _, small, pending = _local_step(
        x.reshape(batch * SEQ_LEN, D), loss_target.reshape(batch * SEQ_LEN, D), g_pre + sq_token[0:1, 0:1], g_post,
        sinks, wt, wconv, squares, start_exchange)
    sm_send, sm_recv, sm_flying, sm_lands, sm_token = _gather_start([small], "gather_small")
    send_sems, recv_sems, flying, lands, in32, sq32 = pending
    from_chips = _exchange_chips_wait(send_sems, recv_sems, flying, lands, sm_token)

    o_in = [o[0].T for o in _chip_sum(
        in32, from_chips[:1], chip_arr, SHARD_IN // 3, "chip_sum_adamw_w_in",
        adam=([w_in[0].T], [m_w_in[0].T], [v_w_in[0].T]))]
    g_in_mine, o_in = o_in[0], o_in[1:]
    g_sq, d_sq, m_sq, v_sq = _chip_sum(
        sq32, from_chips[1:], chip_arr, SHARD_SQ, "chip_sum_adamw_squares",
        adam=([w_proj_conv[0], w_proj_attn[0], w_out[0]], [m_w_proj_conv[0], m_w_proj_attn[0], m_w_out[0]],
              [v_w_proj_conv[0], v_w_proj_attn[0], v_w_out[0]]))
    both_done, g_in_mine = lax.optimization_barrier((d_sq[0], g_in_mine))
    (small_all,) = _gather_wait(sm_send, sm_recv, sm_flying, sm_lands, both_done, "gather_small")
    gs = _sum_small(lax.dynamic_update_index_in_dim(small_all, small, me, 0))
    g_g_pre, g_g_post, g_sinks, loss = gs[0:1], gs[8:9], gs[16:17, 0:N_HEADS], gs[32, 0]
    g_conv_mine = lax.dynamic_slice_in_dim(gs[24:27], me * SHARD_SQ, SHARD_SQ, axis=1)
    o_small = _adamw([g_pre, g_post, sinks, w_conv[0]], [g_g_pre, g_g_post, g_sinks, g_conv_mine],
                     [m_g_pre, m_g_post, m_sinks, m_w_conv[0]], [v_g_pre, v_g_post, v_sinks, v_w_conv[0]], "adamw_small")

    grads = [g_g_pre, g_g_post, g_in_mine[None], g_conv_mine[None], g_sinks] + [g[None] for g in g_sq]
    rest = []
    for idx, sq in enumerate((d_sq, m_sq, v_sq)):
        gp, gq, sk, cv = o_small[idx]
        rest += [gp, gq, o_in[idx][None], cv[None], sk] + [s[None] for s in sq]
    return (loss, grad_x.reshape(batch, SEQ_LEN, D), *grads, *rest)
```

```python
import jax
import jax.numpy as jnp
from jax import lax
from jax.experimental import pallas as pl
from jax.experimental.pallas import tpu as pltpu

D = 1024
N_HEADS = 16
HEAD_DIM = 64
LOGIT_SCALE = HEAD_DIM ** -0.5
BLK = 128
SEQ_LEN = 2048
D_IN = 8448
ROW_Q, ROW_KV, ROW_ZA, ROW_GA = 4 * D, 5 * D, 5 * D + 256, 6 * D + 256
SHARD_IN = D_IN // 8
SHARD_SQ = D // 8
N_DEV = 8
V7X_VMEM_BYTES = 64 << 20
ROPE_THETA = 10000.0
RMS_EPS = 1e-6
NEG = -1e30
ADAM_LR, ADAM_B1, ADAM_B2, ADAM_EPS, ADAM_WD, ADAM_STEP = 0.001, 0.9, 0.999, 1e-08, 0.01, 10

F32 = jnp.float32
BF16 = jnp.bfloat16
MESH_ID = pl.DeviceIdType.MESH


def _dot(a, b):
    return jnp.dot(a, b, preferred_element_type=F32)


def _dot_nt(a, b):
    return lax.dot_general(a, b, (((1,), (1,)), ((), ())), preferred_element_type=F32)


def _dot_tn(a, b):
    return lax.dot_general(a, b, (((0,), (0,)), ((), ())), preferred_element_type=F32)


def _sig(z):
    return 1.0 / (1.0 + jnp.exp(-z))


def _swap_halves(z):
    lane = lax.broadcasted_iota(jnp.int32, z.shape, 1)
    return jnp.where((lane & 63) < 32, pltpu.roll(z, 96, 1), pltpu.roll(z, 32, 1))


def _row_spec(tm, width, col=0):
    return pl.BlockSpec((tm, width), lambda i: (i, col))


def _whole_vmem():
    return pl.BlockSpec(memory_space=pltpu.VMEM)


def _params(*sem, vmem_limit_bytes=None):
    return pltpu.CompilerParams(dimension_semantics=sem, vmem_limit_bytes=vmem_limit_bytes)


def _fwd_in_attn(x, g_pre, wt, cos_t, sin_t, tm):
    t = x.shape[0]
    seq_tiles = SEQ_LEN // tm

    def body(x_ref, g_ref, w_ref, c_ref, s_ref, h_ref, q_ref, kv_ref, g3_ref):
        xf = x_ref[...]
        r = lax.rsqrt(jnp.mean(xf * xf, axis=-1, keepdims=True) + RMS_EPS)
        hh = ((xf * r) * g_ref[...]).astype(BF16)
        h_ref[...] = hh
        c = c_ref[...]
        s = s_ref[...]

        def rope(z):
            return z * c + _swap_halves(z) * s

        q = _dot_nt(hh, w_ref[ROW_Q:ROW_Q + D, :])
        for j in range(D // 128):
            q_ref[:, j * 128:(j + 1) * 128] = (rope(q[:, j * 128:(j + 1) * 128]) * LOGIT_SCALE).astype(BF16)
        kv = _dot_nt(hh, w_ref[ROW_KV:ROW_KV + 256, :])
        kv_ref[:, 0:128] = rope(kv[:, 0:128]).astype(BF16)
        kv_ref[:, 128:256] = kv[:, 128:256].astype(BF16)
        for j in range(3):
            g3_ref[:, j * D:(j + 1) * D] = _dot_nt(hh, w_ref[ROW_ZA + j * D:ROW_ZA + (j + 1) * D, :])

    tab = pl.BlockSpec((tm, 128), lambda i: (i % seq_tiles, 0))
    return pl.pallas_call(
        body, name="fwd_in_attn", grid=(t // tm,),
        in_specs=[_row_spec(tm, D), pl.BlockSpec((1, D), lambda i: (0, 0)), _whole_vmem(), tab, tab],
        out_specs=[_row_spec(tm, D), _row_spec(tm, D), _row_spec(tm, 256), _row_spec(tm, 3 * D)],
        out_shape=[jax.ShapeDtypeStruct((t, D), BF16), jax.ShapeDtypeStruct((t, D), BF16),
                   jax.ShapeDtypeStruct((t, 256), BF16), jax.ShapeDtypeStruct((t, 3 * D), F32)],
        compiler_params=_params("parallel"),
    )(x, g_pre, wt, cos_t, sin_t)


def _conv_forward(xc, bg, cg, zc, up6, up7, w_ref):
    rows = lax.broadcasted_iota(jnp.int32, xc.shape, 0)
    u = cg * xc
    u_m1 = jnp.where(rows == 0, up7, pltpu.roll(u, 1, 0))
    u_m2 = jnp.where(rows == 0, up6, jnp.where(rows == 1, up7, pltpu.roll(u, 2, 0)))
    yconv = w_ref[0:1, :] * u_m2 + w_ref[1:2, :] * u_m1 + w_ref[2:3, :] * u
    sg = _sig(zc)
    sz = zc * sg
    co = bg * yconv
    return u, u_m1, u_m2, yconv, sg, sz, co


def _fwd_in_conv(h, wt, wconv8, wpc, tm):
    t = h.shape[0]
    seq_tiles = SEQ_LEN // tm

    def body(h_ref, w_ref, wc_ref, wpc_ref, a4_ref, ya_ref, last_u_ref):
        hh = h_ref[...]
        xc, bg, cg, zc = (_dot_nt(hh, w_ref[j * D:(j + 1) * D, :]) for j in range(4))
        for j, z in enumerate((xc, bg, cg, zc)):
            a4_ref[:, j * D:(j + 1) * D] = z.astype(BF16)
        first = pl.program_id(0) % seq_tiles == 0
        up6 = jnp.where(first, 0.0, last_u_ref[6:7, :])
        up7 = jnp.where(first, 0.0, last_u_ref[7:8, :])
        u, _, _, _, _, sz, co = _conv_forward(xc, bg, cg, zc, up6, up7, wc_ref)
        last_u_ref[...] = u[tm - 8:tm, :]
        ya_ref[...] = _dot((sz * co).astype(BF16), wpc_ref[...])

    return pl.pallas_call(
        body, name="fwd_in_conv", grid=(t // tm,),
        in_specs=[_row_spec(tm, D), _whole_vmem(), pl.BlockSpec((8, D), lambda i: (0, 0)), _whole_vmem()],
        out_specs=[_row_spec(tm, 4 * D), _row_spec(tm, D)],
        out_shape=[jax.ShapeDtypeStruct((t, 4 * D), BF16), jax.ShapeDtypeStruct((t, D), F32)],
        scratch_shapes=[pltpu.VMEM((8, D), F32)],
        compiler_params=_params("arbitrary"),
    )(h, wt, wconv8, wpc)


STACK = 4 * BLK


def _band_mask(first):
    qi = lax.broadcasted_iota(jnp.int32, (STACK, 2 * BLK), 0) & (BLK - 1)
    kj = lax.broadcasted_iota(jnp.int32, (STACK, 2 * BLK), 1)
    return (kj > qi) & (kj <= qi + BLK) & (kj >= jnp.where(first, BLK, 0))


def _masked_fill(sink_ref, g, e):
    kj = lax.broadcasted_iota(jnp.int32, (STACK, 2 * BLK), 1)
    sink = jnp.concatenate([jnp.full((BLK, 2 * BLK), sink_ref[0, 2 * (4 * g + jj) + e], F32) for jj in range(4)], axis=0)
    return jnp.where(kj == 0, sink, NEG)


def _padded_pair(before, own, other=0.0):
    z = jnp.concatenate([before, own], axis=0).astype(F32)
    z = jnp.where(lax.broadcasted_iota(jnp.int32, z.shape, 0) == 0, 0.0, z)
    zs = pltpu.roll(z, 64, 1)
    lo = lax.broadcasted_iota(jnp.int32, z.shape, 1) < 64
    fill = jnp.full_like(z, other)
    left = [jnp.where(lo, z, fill).astype(BF16), jnp.where(lo, zs, fill).astype(BF16)]
    right = [jnp.where(lo, fill, zs).astype(BF16), jnp.where(lo, fill, z).astype(BF16)]
    return left, right


def _exp_logits(s, valid, fill):
    s = jnp.where(valid, s, fill)
    return jnp.exp(s - jnp.max(s, axis=-1, keepdims=True))


def _kv_blocks(kvc_ref, kvp_ref, b, col):
    own = kvc_ref[b * BLK:(b + 1) * BLK, col:col + 128]
    before = kvp_ref[:, col:col + 128] if b == 0 else kvc_ref[(b - 1) * BLK:b * BLK, col:col + 128]
    return before, own


def _fwd_attn(sinks, q, kv, g3, blocks):
    t = q.shape[0]
    tq = blocks * BLK
    seq_blocks = SEQ_LEN // BLK

    def body(sink_ref, q_ref, kvc_ref, kvp_ref, za_ref, attn_ref, ub_ref):
        lo = lax.broadcasted_iota(jnp.int32, (STACK, 128), 1) < 64
        for b in range(blocks):
            rows = slice(b * BLK, (b + 1) * BLK)
            valid = _band_mask((pl.program_id(0) * blocks + b) % seq_blocks == 0)
            k_pad = _padded_pair(*_kv_blocks(kvc_ref, kvp_ref, b, 0))
            v_one = _padded_pair(*_kv_blocks(kvc_ref, kvp_ref, b, 128), other=1.0)
            for g in range(2):
                qg = jnp.concatenate([q_ref[rows, j * 128:(j + 1) * 128] for j in range(4 * g, 4 * g + 4)], axis=0)
                pv = [_dot(_exp_logits(_dot_nt(qg, k_pad[e][g]), valid, _masked_fill(sink_ref, g, e)).astype(BF16),
                           v_one[e][g]) for e in range(2)]
                o = jnp.where(lo, pv[0], pv[1]) / pltpu.roll(jnp.where(lo, pv[1], pv[0]), 64, 1)
                for jj in range(4):
                    cols = slice((4 * g + jj) * 128, (4 * g + jj + 1) * 128)
                    oj = o[jj * BLK:(jj + 1) * BLK, :]
                    attn_ref[rows, cols] = oj
                    za = za_ref[rows, cols]
                    ub_ref[rows, cols] = (za * _sig(za) * oj).astype(BF16)

    return pl.pallas_call(
        body, name="fwd_attn", grid=(t // tq,),
        in_specs=[pl.BlockSpec(memory_space=pltpu.SMEM), _row_spec(tq, D), _row_spec(tq, 256),
                  pl.BlockSpec((BLK, 256), lambda i: (jnp.maximum(i * blocks - 1, 0), 0)), _row_spec(tq, D, 0)],
        out_specs=[_row_spec(tq, D), _row_spec(tq, D)],
        out_shape=[jax.ShapeDtypeStruct((t, D), F32), jax.ShapeDtypeStruct((t, D), BF16)],
        compiler_params=_params("parallel"),
    )(sinks, q, kv, kv, g3)


def _fwd_out_bwd_head(ya, ub, g3, x, target, g_post, wpa, wout, tm):
    t = x.shape[0]

    def body(ya_ref, ub_ref, ga_ref, gb_ref, x_ref, tgt_ref, gp_ref, wpa_ref, wout_ref,
             loss_ref, dout_ref, dya_ref, dub_ref, dgab_ref, dwout_ref, dwpa_ref, dgp_ref):
        @pl.when(pl.program_id(0) == 0)
        def _():
            loss_ref[...] = jnp.zeros_like(loss_ref)
            dwout_ref[...] = jnp.zeros_like(dwout_ref)
            dwpa_ref[...] = jnp.zeros_like(dwpa_ref)
            dgp_ref[...] = jnp.zeros_like(dgp_ref)

        g = gp_ref[...]
        ub = ub_ref[...]
        ya = ya_ref[...]
        yb = _dot(ub, wpa_ref[...])
        sa = _sig(ga_ref[...])
        sb = _sig(gb_ref[...])
        mb = (sa * ya + sb * yb).astype(BF16)
        y = _dot(mb, wout_ref[...])
        r = lax.rsqrt(jnp.mean(y * y, axis=-1, keepdims=True) + RMS_EPS)
        n = y * r
        err = (x_ref[...] + n * g) - tgt_ref[...]
        loss_ref[...] += jnp.sum(jnp.sum(err * err, axis=0, keepdims=True), axis=1, keepdims=True) * (0.5 / D)
        dout = err * (1.0 / D)
        dout_ref[...] = dout
        dgp_ref[0:1, :] += jnp.sum(dout * n, axis=0, keepdims=True)
        dn = dout * g
        dy = (r * (dn - n * jnp.mean(dn * n, axis=-1, keepdims=True))).astype(BF16)
        dm = _dot_nt(dy, wout_ref[...])
        dya_ref[...] = (dm * sa).astype(BF16)
        dyb = (dm * sb).astype(BF16)
        dgab_ref[:, 0:D] = (dm * ya * (sa * (1.0 - sa))).astype(BF16)
        dgab_ref[:, D:2 * D] = (dm * yb * (sb * (1.0 - sb))).astype(BF16)
        dub_ref[...] = _dot_nt(dyb, wpa_ref[...])
        dwout_ref[...] += _dot_tn(mb, dy)
        dwpa_ref[...] += _dot_tn(ub, dyb)

    return pl.pallas_call(
        body, name="fwd_out_bwd_head", grid=(t // tm,),
        in_specs=[_row_spec(tm, D), _row_spec(tm, D), _row_spec(tm, D, 1), _row_spec(tm, D, 2),
                  _row_spec(tm, D), _row_spec(tm, D), pl.BlockSpec((1, D), lambda i: (0, 0)),
                  _whole_vmem(), _whole_vmem()],
        out_specs=[pl.BlockSpec((8, 128), lambda i: (0, 0)), _row_spec(tm, D), _row_spec(tm, D), _row_spec(tm, D),
                   _row_spec(tm, 2 * D), _whole_vmem(), _whole_vmem(), pl.BlockSpec((8, D), lambda i: (0, 0))],
        out_shape=[jax.ShapeDtypeStruct((8, 128), F32), jax.ShapeDtypeStruct((t, D), F32),
                   jax.ShapeDtypeStruct((t, D), BF16), jax.ShapeDtypeStruct((t, D), F32),
                   jax.ShapeDtypeStruct((t, 2 * D), BF16), jax.ShapeDtypeStruct((D, D), F32),
                   jax.ShapeDtypeStruct((D, D), F32), jax.ShapeDtypeStruct((8, D), F32)],
        compiler_params=_params("arbitrary", vmem_limit_bytes=V7X_VMEM_BYTES - (2 << 20)),
    )(ya, ub, g3, g3, x, target, g_post, wpa, wout)


def _bwd_attn(sinks, q, kv, attn, dub, g3, cos_t, sin_t, wt, blocks):
    t = q.shape[0]
    tq = blocks * BLK
    seq_blocks = SEQ_LEN // BLK

    def body(sink_ref, q_ref, kvc_ref, kvp_ref, attn_ref, dub_ref, za_ref, c_ref, s_ref, w_ref,
             dq_ref, dza_ref, dkv_own_ref, dkv_prev_ref, dsink_ref, dh_ref):
        @pl.when(pl.program_id(0) == 0)
        def _():
            dsink_ref[...] = jnp.zeros_like(dsink_ref)

        lo = lax.broadcasted_iota(jnp.int32, (STACK, 128), 1) < 64
        lane8 = lax.broadcasted_iota(jnp.int32, (8, 128), 1)
        lo2 = lax.broadcasted_iota(jnp.int32, (2 * BLK, 128), 1) < 64
        sink_row = lax.broadcasted_iota(jnp.int32, (2 * BLK, 128), 0) == 0
        dsink = jnp.zeros((8, 128), F32)
        for b in range(blocks):
            rows = slice(b * BLK, (b + 1) * BLK)
            valid = _band_mask((pl.program_id(0) * blocks + b) % seq_blocks == 0)
            k_pad = _padded_pair(*_kv_blocks(kvc_ref, kvp_ref, b, 0))
            v_pad = _padded_pair(*_kv_blocks(kvc_ref, kvp_ref, b, 128))
            c = c_ref[rows, :]
            s = s_ref[rows, :]
            dk_acc, dv_acc = [], []
            for g in range(2):
                qg, dog = [], []
                for j in range(4 * g, 4 * g + 4):
                    cols = slice(j * 128, (j + 1) * 128)
                    za = za_ref[rows, cols]
                    sg = _sig(za)
                    dub = dub_ref[rows, cols]
                    dza_ref[rows, cols] = (dub * attn_ref[rows, cols] * (sg * (1.0 + za * (1.0 - sg)))).astype(BF16)
                    dog.append((dub * (za * sg)).astype(BF16))
                    qg.append(q_ref[rows, cols])
                qg = jnp.concatenate(qg, axis=0)
                dog = jnp.concatenate(dog, axis=0)
                dq = jnp.zeros((STACK, 128), F32)
                ds_both, p_both = [], []
                for e in range(2):
                    p = _exp_logits(_dot_nt(qg, k_pad[e][g]), valid, _masked_fill(sink_ref, g, e))
                    p = p / jnp.sum(p, axis=-1, keepdims=True)
                    dp = _dot_nt(dog, v_pad[e][g])
                    ds = p * (dp - jnp.sum(p * dp, axis=-1, keepdims=True))
                    for jj in range(4):
                        tot = jnp.sum(ds[jj * BLK:(jj + 1) * BLK, 0:1], axis=0, keepdims=True)
                        dsink = dsink + jnp.where(lane8 == 2 * (4 * g + jj) + e, tot, 0.0)
                    ds = ds.astype(BF16)
                    dq = dq + _dot(ds, k_pad[e][g])
                    ds_both.append(ds)
                    p_both.append(p.astype(BF16))
                zero = jnp.zeros_like(qg)
                q2 = jnp.concatenate([jnp.where(lo, qg, zero), jnp.where(lo, zero, qg)], axis=0)
                do2 = jnp.concatenate([jnp.where(lo, dog, zero), jnp.where(lo, zero, dog)], axis=0)
                dk_acc.append(_dot_tn(q2, jnp.concatenate(ds_both, axis=0)).T)
                dv_acc.append(_dot_tn(do2, jnp.concatenate(p_both, axis=0)).T)
                for jj in range(4):
                    cols = slice((4 * g + jj) * 128, (4 * g + jj + 1) * 128)
                    dqj = dq[jj * BLK:(jj + 1) * BLK, :] * LOGIT_SCALE
                    dq_ref[rows, cols] = (dqj * c - _swap_halves(dqj) * s).astype(BF16)
            for col, acc in ((0, dk_acc), (128, dv_acc)):
                both = jnp.where(lo2, acc[0] + pltpu.roll(acc[0], 64, 1), acc[1] + pltpu.roll(acc[1], 64, 1))
                both = jnp.where(sink_row, 0.0, both)
                dkv_prev_ref[rows, col:col + 128] = both[0:BLK, :]
                dkv_own_ref[rows, col:col + 128] = both[BLK:2 * BLK, :]
        dsink_ref[...] += dsink
        dh_ref[...] = _dot(dq_ref[...], w_ref[ROW_Q:ROW_KV, :]) + _dot(dza_ref[...], w_ref[ROW_ZA:ROW_GA, :])

    tab = pl.BlockSpec((tq, 128), lambda i: (i % (SEQ_LEN // tq), 0))
    return pl.pallas_call(
        body, name="bwd_attn", grid=(t // tq,),
        in_specs=[pl.BlockSpec(memory_space=pltpu.SMEM), _row_spec(tq, D), _row_spec(tq, 256),
                  pl.BlockSpec((BLK, 256), lambda i: (jnp.maximum(i * blocks - 1, 0), 0)),
                  _row_spec(tq, D), _row_spec(tq, D), _row_spec(tq, D, 0), tab, tab, _whole_vmem()],
        out_specs=[_row_spec(tq, D), _row_spec(tq, D), _row_spec(tq, 256), _row_spec(tq, 256),
                   pl.BlockSpec((8, 128), lambda i: (0, 0)), _row_spec(tq, D)],
        out_shape=[jax.ShapeDtypeStruct((t, D), BF16), jax.ShapeDtypeStruct((t, D), BF16),
                   jax.ShapeDtypeStruct((t, 256), F32), jax.ShapeDtypeStruct((t, 256), F32),
                   jax.ShapeDtypeStruct((8, 128), F32), jax.ShapeDtypeStruct((t, D), F32)],
        compiler_params=_params("arbitrary"),
    )(sinks, q, kv, kv, attn, dub, g3, cos_t, sin_t, wt)


def _bwd_kv_finish(dkv_own, dkv_prev, cos_t, sin_t):
    t = dkv_own.shape[0]
    tm = 512
    seq_tiles = SEQ_LEN // tm
    n_blocks = t // BLK

    def body(own_ref, same_ref, nxt_ref, c_ref, s_ref, out_ref):
        keep = jnp.where(pl.program_id(0) % seq_tiles == seq_tiles - 1, 0.0, 1.0)
        shifted = jnp.concatenate([same_ref[BLK:tm, :], nxt_ref[...] * keep], axis=0)
        tot = own_ref[...] + shifted
        dk = tot[:, 0:128]
        out_ref[:, 0:128] = (dk * c_ref[...] - _swap_halves(dk) * s_ref[...]).astype(BF16)
        out_ref[:, 128:256] = tot[:, 128:256].astype(BF16)

    tab = pl.BlockSpec((tm, 128), lambda i: (i % seq_tiles, 0))
    return pl.pallas_call(
        body, name="bwd_kv_finish", grid=(t // tm,),
        in_specs=[_row_spec(tm, 256), _row_spec(tm, 256),
                  pl.BlockSpec((BLK, 256), lambda i: (jnp.minimum((i + 1) * (tm // BLK), n_blocks - 1), 0)), tab, tab],
        out_specs=_row_spec(tm, 256),
        out_shape=jax.ShapeDtypeStruct((t, 256), BF16),
        compiler_params=_params("parallel"),
    )(dkv_own, dkv_prev, dkv_prev, cos_t, sin_t)


STAGE_ROWS = 256


def _bwd_conv(dya, a4, h, wconv8, wpc, tm, parts):
    t = a4.shape[0]
    n_t = t // tm
    sub = tm // parts
    seq_tiles = SEQ_LEN // tm

    def body(dya_ref, xc_ref, bg_ref, cg_ref, zc_ref, xcp_ref, cgp_ref, w_ref, wpc_ref, h_ref,
             da4_ref, dwpc_ref, dwc_ref, o32_ref, o16_ref, acc_ref, stage_ref, later_ref, sems):
        step = pl.program_id(0)
        tile = n_t - 1 - step

        @pl.when(step == 0)
        def _():
            dwpc_ref[...] = jnp.zeros_like(dwpc_ref)
            dwc_ref[...] = jnp.zeros_like(dwc_ref)
            acc_ref[...] = jnp.zeros_like(acc_ref)

        keep_prev = jnp.where(tile % seq_tiles == 0, 0.0, 1.0)
        ends_sequence = tile % seq_tiles == seq_tiles - 1

        def part(p, later):
            r0 = p * sub
            here = slice(r0, r0 + sub)
            if p == 0:
                u_prev = cgp_ref[14:16, :].astype(F32) * xcp_ref[14:16, :].astype(F32) * keep_prev
            else:
                u_prev = cg_ref[r0 - 2:r0, :].astype(F32) * xc_ref[r0 - 2:r0, :].astype(F32)
            xc = xc_ref[here, :].astype(F32)
            bg = bg_ref[here, :].astype(F32)
            cg = cg_ref[here, :].astype(F32)
            zc = zc_ref[here, :].astype(F32)
            u, u_m1, u_m2, yconv, sg, sz, co = _conv_forward(xc, bg, cg, zc, u_prev[0:1, :], u_prev[1:2, :], w_ref)
            ua = (sz * co).astype(BF16)
            dua = _dot_nt(dya_ref[here, :], wpc_ref[...])
            da4_ref[here, 3 * D:4 * D] = (dua * co * (sg * (1.0 + zc * (1.0 - sg)))).astype(BF16)
            dco = dua * sz
            da4_ref[here, D:2 * D] = (dco * yconv).astype(BF16)
            dyc = dco * bg
            dwc = jnp.concatenate([jnp.sum(dyc * s, axis=0, keepdims=True) for s in (u_m2, u_m1, u)], axis=0)
            rows = lax.broadcasted_iota(jnp.int32, xc.shape, 0)
            n0 = later[0:1, :]
            n1 = later[1:2, :]
            dyc_p1 = jnp.where(rows == sub - 1, n0, pltpu.roll(dyc, sub - 1, 0))
            dyc_p2 = jnp.where(rows == sub - 2, n0, jnp.where(rows == sub - 1, n1, pltpu.roll(dyc, sub - 2, 0)))
            du = w_ref[2:3, :] * dyc + w_ref[1:2, :] * dyc_p1 + w_ref[0:1, :] * dyc_p2
            da4_ref[here, 0:D] = (du * cg).astype(BF16)
            da4_ref[here, 2 * D:3 * D] = (du * xc).astype(BF16)
            return ua, dwc, dyc[0:8, :]

        later = jnp.where(ends_sequence, 0.0, later_ref[...])
        uas, dwc = [], jnp.zeros((3, D), F32)
        for p in reversed(range(parts)):
            ua, dwc_p, later = part(p, later)
            uas.insert(0, ua)
            dwc = dwc + dwc_p
        later_ref[...] = later
        dwpc_ref[...] += _dot_tn(jnp.concatenate(uas, axis=0), dya_ref[...])
        dwc_ref[0:3, :] += dwc
        for j in range(4):
            acc_ref[j * D:(j + 1) * D, :] += _dot_tn(da4_ref[:, j * D:(j + 1) * D], h_ref[...])

        @pl.when(step == n_t - 1)
        def _():
            c32 = pltpu.make_async_copy(acc_ref, o32_ref.at[pl.ds(0, 4 * D)], sems.at[0])
            c32.start()
            for j in range(4 * D // STAGE_ROWS):
                rows = pl.ds(j * STAGE_ROWS, STAGE_ROWS)
                stage_ref[...] = acc_ref[rows, :].astype(BF16)
                c16 = pltpu.make_async_copy(stage_ref, o16_ref.at[rows], sems.at[1])
                c16.start()
                c16.wait()
            c32.wait()

    def rows_of_tile(width, col=0):
        return pl.BlockSpec((tm, width), lambda s: (n_t - 1 - s, col))

    def prev(col):
        return pl.BlockSpec((16, D), lambda s: (jnp.maximum((n_t - 1 - s) * (tm // 16) - 1, 0), col))

    hbm = pl.BlockSpec(memory_space=pl.ANY)
    out = pl.pallas_call(
        body, name="bwd_conv", grid=(n_t,),
        in_specs=[rows_of_tile(D), rows_of_tile(D, 0), rows_of_tile(D, 1), rows_of_tile(D, 2), rows_of_tile(D, 3),
                  prev(0), prev(2), pl.BlockSpec((8, D), lambda s: (0, 0)), _whole_vmem(), rows_of_tile(D)],
        out_specs=[rows_of_tile(4 * D), _whole_vmem(), pl.BlockSpec((8, D), lambda s: (0, 0)), hbm, hbm],
        out_shape=[jax.ShapeDtypeStruct((t, 4 * D), BF16), jax.ShapeDtypeStruct((D, D), F32),
                   jax.ShapeDtypeStruct((8, D), F32), jax.ShapeDtypeStruct((D_IN, D), F32),
                   jax.ShapeDtypeStruct((D_IN, D), BF16)],
        scratch_shapes=[pltpu.VMEM((4 * D, D), F32), pltpu.VMEM((STAGE_ROWS, D), BF16), pltpu.VMEM((8, D), F32),
                        pltpu.SemaphoreType.DMA((2,))],
        compiler_params=pltpu.CompilerParams(dimension_semantics=("arbitrary",), vmem_limit_bytes=V7X_VMEM_BYTES - (2 << 20)),
    )(dya, a4, a4, a4, a4, a4, a4, wconv8, wpc, h)
    return out[0], out[1], out[2], (out[3], out[4])


def _bwd_dh(da4, dh_part, dkv, dgab, wt, x, g_pre, dout, tm):
    t = x.shape[0]

    def body(da4_ref, dhp_ref, dkv_ref, dgab_ref, w_ref, x_ref, g_ref, dout_ref, gx_ref, dg_ref):
        @pl.when(pl.program_id(0) == 0)
        def _():
            dg_ref[...] = jnp.zeros_like(dg_ref)

        dh = dhp_ref[...] + _dot(da4_ref[...], w_ref[0:ROW_Q, :])
        dh += _dot(dkv_ref[...], w_ref[ROW_KV:ROW_ZA, :])
        dh += _dot(dgab_ref[...], w_ref[ROW_GA:D_IN, :])
        xf = x_ref[...]
        r = lax.rsqrt(jnp.mean(xf * xf, axis=-1, keepdims=True) + RMS_EPS)
        xn = xf * r
        dg_ref[0:1, :] += jnp.sum(dh * xn, axis=0, keepdims=True)
        dxn = dh * g_ref[...]
        gx_ref[...] = dout_ref[...] + r * (dxn - xn * jnp.mean(dxn * xn, axis=-1, keepdims=True))

    return pl.pallas_call(
        body, name="bwd_dh", grid=(t // tm,),
        in_specs=[_row_spec(tm, 4 * D), _row_spec(tm, D), _row_spec(tm, 256), _row_spec(tm, 2 * D),
                  _whole_vmem(), _row_spec(tm, D), pl.BlockSpec((1, D), lambda i: (0, 0)), _row_spec(tm, D)],
        out_specs=[_row_spec(tm, D), pl.BlockSpec((8, D), lambda i: (0, 0))],
        out_shape=[jax.ShapeDtypeStruct((t, D), F32), jax.ShapeDtypeStruct((8, D), F32)],
        compiler_params=_params("arbitrary"),
    )(da4, dh_part, dkv, dgab, wt, x, g_pre, dout)


def _bwd_dw_in(h, piece, row0, nb, tm, name, prev):
    t, n = piece.shape
    n_t = t // tm

    def body(*refs):
        h_ref, p_ref = refs[0], refs[1]
        o32_ref, o16_ref, acc_ref, acc16_ref, sems = refs[-5:]
        j, i = pl.program_id(0), pl.program_id(1)

        @pl.when(i == 0)
        def _():
            acc_ref[...] = jnp.zeros_like(acc_ref)

        acc_ref[...] += _dot_tn(p_ref[...], h_ref[...])

        @pl.when(i == n_t - 1)
        def _():
            acc16_ref[...] = acc_ref[...].astype(BF16)
            rows = pl.ds(pl.multiple_of(row0 + j * nb, 16), nb)
            c32 = pltpu.make_async_copy(acc_ref, o32_ref.at[rows], sems.at[0])
            c16 = pltpu.make_async_copy(acc16_ref, o16_ref.at[rows], sems.at[1])
            c32.start()
            c16.start()
            c32.wait()
            c16.wait()

    hbm = pl.BlockSpec(memory_space=pl.ANY)
    carried = [] if prev is None else list(prev)
    return pl.pallas_call(
        body, name=name, grid=(n // nb, n_t),
        in_specs=[pl.BlockSpec((tm, D), lambda j, i: (i, 0)), pl.BlockSpec((tm, nb), lambda j, i: (i, j))]
        + [hbm] * len(carried),
        out_specs=[hbm, hbm],
        out_shape=[jax.ShapeDtypeStruct((D_IN, D), F32), jax.ShapeDtypeStruct((D_IN, D), BF16)],
        scratch_shapes=[pltpu.VMEM((nb, D), F32), pltpu.VMEM((nb, D), BF16), pltpu.SemaphoreType.DMA((2,))],
        input_output_aliases={2: 0, 3: 1} if carried else {},
        compiler_params=_params("arbitrary", "arbitrary"),
    )(h, piece, *carried)


def _place():
    x, y, c = lax.axis_index("x"), lax.axis_index("y"), lax.axis_index("c")
    return x, y, c, 4 * x + 2 * y + c


def _peer(x, y, c, k):
    return (1 - x if k & 4 else x, 1 - y if k & 2 else y, 1 - c if k & 1 else c)


ICI_MASKS = (4, 2, 6)


def _all_gather(shards):
    n = len(shards)

    def body(*refs):
        src, dst = refs[:n], refs[n:2 * n]
        send_sems, recv_sems, local_sems = refs[2 * n:]
        x, y, c, me = _place()
        sibling = _peer(x, y, c, 1)

        def copy(a, s, block, to, own=False):
            return pltpu.make_async_remote_copy(
                src_ref=src[a] if own else dst[a].at[block], dst_ref=dst[a].at[block],
                send_sem=send_sems.at[a * 7 + s], recv_sem=recv_sems.at[a * 7 + s], device_id=to, device_id_type=MESH_ID)

        local = [pltpu.make_async_copy(src[a], dst[a].at[me], local_sems.at[a]) for a in range(n)]
        for cp in local:
            cp.start()
        started = [copy(a, 0, me, sibling, own=True) for a in range(n)]
        started += [copy(a, 1 + j, me, _peer(x, y, c, k), own=True) for j, k in enumerate(ICI_MASKS) for a in range(n)]
        for cp in started:
            cp.start()
        for j, k in enumerate(ICI_MASKS):
            for a in range(n):
                copy(a, 1 + j, me ^ k, sibling).wait_recv()
                fwd = copy(a, 4 + j, me ^ k, sibling)
                fwd.start()
                started.append(fwd)
        for a in range(n):
            copy(a, 0, me ^ 1, sibling).wait_recv()
        for j, k in enumerate(ICI_MASKS):
            for a in range(n):
                copy(a, 4 + j, me ^ 1 ^ k, sibling).wait_recv()
        for cp in started:
            cp.wait_send()
        for cp in local:
            cp.wait()

    hbm = pl.BlockSpec(memory_space=pl.ANY)
    return pl.pallas_call(
        body, name="all_gather_weights",
        in_specs=[hbm] * n, out_specs=[hbm] * n,
        out_shape=[jax.ShapeDtypeStruct((N_DEV,) + s.shape, s.dtype) for s in shards],
        scratch_shapes=[pltpu.SemaphoreType.DMA((7 * n,)), pltpu.SemaphoreType.DMA((7 * n,)),
                        pltpu.SemaphoreType.DMA((n,))],
    )(*shards)


def _direct_copies(src, land, send_sems, recv_sems):
    x, y, c, me = _place()
    return [pltpu.make_async_remote_copy(
        src_ref=src[a], dst_ref=land[a].at[me], send_sem=send_sems.at[a * 7 + k - 1],
        recv_sem=recv_sems.at[a * 7 + k - 1], device_id=_peer(x, y, c, k), device_id_type=MESH_ID)
        for k in range(1, N_DEV) for a in range(len(src))]


def _gather_start(shards, name):
    n = len(shards)

    def body(*refs):
        src, land = refs[:n], refs[n:2 * n]
        send_sems, recv_sems = refs[2 * n], refs[2 * n + 1]
        token_ref = refs[-1]
        for cp in _direct_copies(src, land, send_sems, recv_sems):
            cp.start()
        token_ref[...] = jnp.zeros_like(token_ref)

    hbm = pl.BlockSpec(memory_space=pltpu.HBM)
    sem = pl.BlockSpec(memory_space=pltpu.SEMAPHORE)
    lands = [lax.empty((N_DEV,) + s.shape, s.dtype) for s in shards]
    out = pl.pallas_call(
        body, name=name + "_start",
        out_shape=(pltpu.SemaphoreType.DMA((7 * n,)), pltpu.SemaphoreType.DMA((7 * n,)),
                   *[pltpu.HBM(s.shape, s.dtype) for s in shards], *[pltpu.HBM(s.shape, s.dtype) for s in lands],
                   jax.ShapeDtypeStruct((8, 128), F32)),
        in_specs=[hbm] * (2 * n), out_specs=(sem, sem, *[hbm] * (2 * n), _whole_vmem()),
        input_output_aliases={a: 2 + a for a in range(2 * n)},
        compiler_params=pltpu.CompilerParams(has_side_effects=pltpu.SideEffectType.DATAFLOW_SIDE_EFFECTING),
    )(*[pltpu.with_memory_space_constraint(s, pltpu.HBM) for s in list(shards) + lands])
    return out[0], out[1], out[2:2 + n], out[2 + n:2 + 2 * n], out[-1]


def _gather_wait(send_sems, recv_sems, flying, lands, after, name):
    n = len(flying)

    def body(*refs):
        src, land = refs[:n], refs[n:2 * n]
        for cp in _direct_copies(src, land, refs[2 * n], refs[2 * n + 1]):
            cp.wait_send()
            cp.wait_recv()

    hbm = pl.BlockSpec(memory_space=pltpu.HBM)
    sem = pl.BlockSpec(memory_space=pltpu.SEMAPHORE)
    out = pl.pallas_call(
        body, name=name + "_wait",
        out_shape=tuple(pltpu.HBM(s.shape, s.dtype) for s in list(flying) + list(lands)),
        in_specs=[hbm] * (2 * n) + [sem, sem, pl.BlockSpec(memory_space=pl.ANY)], out_specs=tuple([hbm] * (2 * n)),
        input_output_aliases={a: a for a in range(2 * n)},
        compiler_params=pltpu.CompilerParams(has_side_effects=pltpu.SideEffectType.DATAFLOW_SIDE_EFFECTING),
    )(*flying, *lands, send_sems, recv_sems, after)
    return out[n:]


def _exchange_sibling(by_dest):
    n = len(by_dest)

    def body(*refs):
        src, dst = refs[:n], refs[n:2 * n]
        send_sems, recv_sems = refs[2 * n:]
        x, y, c, _ = _place()
        sibling = _peer(x, y, c, 1)
        copies = [pltpu.make_async_remote_copy(
            src_ref=src[a].at[2 * p + (1 - c)], dst_ref=dst[a].at[p], send_sem=send_sems.at[a * 4 + p],
            recv_sem=recv_sems.at[a * 4 + p], device_id=sibling, device_id_type=MESH_ID)
            for a in range(n) for p in range(4)]
        for cp in copies:
            cp.start()
        for cp in copies:
            cp.wait_recv()
        for cp in copies:
            cp.wait_send()

    hbm = pl.BlockSpec(memory_space=pl.ANY)
    return pl.pallas_call(
        body, name="exchange_sibling", in_specs=[hbm] * n, out_specs=[hbm] * n,
        out_shape=[jax.ShapeDtypeStruct((4,) + s.shape[1:], s.dtype) for s in by_dest],
        scratch_shapes=[pltpu.SemaphoreType.DMA((4 * n,)), pltpu.SemaphoreType.DMA((4 * n,))],
    )(*by_dest)


def _chip_copies(src, land, send_sems, recv_sems):
    x, y, c, _ = _place()
    chip = 2 * x + y
    return [pltpu.make_async_remote_copy(
        src_ref=src[a].at[chip ^ (k >> 1)], dst_ref=land[a].at[j], send_sem=send_sems.at[a * 3 + j],
        recv_sem=recv_sems.at[a * 3 + j], device_id=_peer(x, y, c, k), device_id_type=MESH_ID)
        for j, k in enumerate(ICI_MASKS) for a in range(len(src))]


def _exchange_chips_start(by_chip):
    n = len(by_chip)

    def body(*refs):
        src, land = refs[:n], refs[n:2 * n]
        send_sems, recv_sems = refs[2 * n], refs[2 * n + 1]
        token_ref = refs[-1]
        for cp in _chip_copies(src, land, send_sems, recv_sems):
            cp.start()
        token_ref[...] = jnp.zeros_like(token_ref)

    hbm = pl.BlockSpec(memory_space=pltpu.HBM)
    sem = pl.BlockSpec(memory_space=pltpu.SEMAPHORE)
    lands = [lax.empty((3,) + s.shape[1:], s.dtype) for s in by_chip]
    out = pl.pallas_call(
        body, name="exchange_chips_start",
        out_shape=(pltpu.SemaphoreType.DMA((3 * n,)), pltpu.SemaphoreType.DMA((3 * n,)),
                   *[pltpu.HBM(s.shape, s.dtype) for s in by_chip], *[pltpu.HBM(s.shape, s.dtype) for s in lands],
                   jax.ShapeDtypeStruct((8, 128), F32)),
        in_specs=[hbm] * (2 * n), out_specs=(sem, sem, *[hbm] * (2 * n), _whole_vmem()),
        input_output_aliases={a: 2 + a for a in range(2 * n)},
        compiler_params=pltpu.CompilerParams(has_side_effects=pltpu.SideEffectType.DATAFLOW_SIDE_EFFECTING),
    )(*[pltpu.with_memory_space_constraint(s, pltpu.HBM) for s in list(by_chip) + lands])
    return out[0], out[1], out[2:2 + n], out[2 + n:2 + 2 * n], out[-1]


def _exchange_chips_wait(send_sems, recv_sems, flying, lands, after):
    n = len(flying)

    def body(*refs):
        src, land = refs[:n], refs[n:2 * n]
        send_sems_ref, recv_sems_ref = refs[2 * n], refs[2 * n + 1]
        for cp in _chip_copies(src, land, send_sems_ref, recv_sems_ref):
            cp.wait_send()
            cp.wait_recv()

    hbm = pl.BlockSpec(memory_space=pltpu.HBM)
    sem = pl.BlockSpec(memory_space=pltpu.SEMAPHORE)
    out = pl.pallas_call(
        body, name="exchange_chips_wait",
        out_shape=tuple(pltpu.HBM(s.shape, s.dtype) for s in list(flying) + list(lands)),
        in_specs=[hbm] * (2 * n) + [sem, sem, pl.BlockSpec(memory_space=pl.ANY)], out_specs=tuple([hbm] * (2 * n)),
        input_output_aliases={a: a for a in range(2 * n)},
        compiler_params=pltpu.CompilerParams(has_side_effects=pltpu.SideEffectType.DATAFLOW_SIDE_EFFECTING),
    )(*flying, *lands, send_sems, recv_sems, after)
    return out[n:]


def _adamw_math(w, g, m, v):
    m = ADAM_B1 * m + (1.0 - ADAM_B1) * g
    v = ADAM_B2 * v + (1.0 - ADAM_B2) * (g * g)
    m_hat = m / (1.0 - ADAM_B1 ** ADAM_STEP)
    v_hat = v / (1.0 - ADAM_B2 ** ADAM_STEP)
    return -ADAM_LR * (m_hat / (jnp.sqrt(v_hat) + ADAM_EPS) + ADAM_WD * w), m, v


def _pair_sum(owns, recvs, c_arr, tr, name):
    n = len(owns)
    _, rows, cols = owns[0].shape

    def body(c_ref, *refs):
        for a in range(n):
            s = refs[a][...] + refs[n + a][...].astype(F32)
            refs[2 * n + a][...] = s
            refs[3 * n + a][...] = s.astype(BF16)

    by_chip = pl.BlockSpec((None, tr, cols), lambda p, i, c_ref: (p, i, 0))
    mine = pl.BlockSpec((None, tr, cols), lambda p, i, c_ref: (2 * p + c_ref[0], i, 0))
    out = pl.pallas_call(
        body, name=name,
        grid_spec=pltpu.PrefetchScalarGridSpec(
            num_scalar_prefetch=1, grid=(4, rows // tr), in_specs=[mine] * n + [by_chip] * n, out_specs=[by_chip] * (2 * n)),
        out_shape=[jax.ShapeDtypeStruct((4, rows, cols), F32)] * n + [jax.ShapeDtypeStruct((4, rows, cols), BF16)] * n,
        compiler_params=_params("parallel", "parallel"),
    )(c_arr, *owns, *recvs)
    return out[:n], out[n:]


def _chip_sum(pairs, recvs, chip_arr, tr, name, adam=None):
    n = len(pairs)
    _, rows, cols = pairs[0].shape
    n_state = 0 if adam is None else 3 * n

    def body(chip_ref, *refs):
        outs = refs[2 * n + n_state:]
        for a in range(n):
            g = refs[a][...]
            for j in range(3):
                g = g + refs[n + a][j].astype(F32)
            outs[a][...] = g
            if adam is not None:
                w_ref, m_ref, v_ref = (refs[2 * n + s * n + a] for s in range(3))
                outs[n + a][...], outs[2 * n + a][...], outs[3 * n + a][...] = _adamw_math(w_ref[...], g, m_ref[...], v_ref[...])

    blk = pl.BlockSpec((tr, cols), lambda i, chip_ref: (i, 0))
    n_out = n if adam is None else 4 * n
    out = pl.pallas_call(
        body, name=name,
        grid_spec=pltpu.PrefetchScalarGridSpec(
            num_scalar_prefetch=1, grid=(rows // tr,),
            in_specs=[pl.BlockSpec((None, tr, cols), lambda i, chip_ref: (chip_ref[0], i, 0))] * n
            + [pl.BlockSpec((3, tr, cols), lambda i, chip_ref: (0, i, 0))] * n + [blk] * n_state,
            out_specs=[blk] * n_out),
        out_shape=[jax.ShapeDtypeStruct((rows, cols), F32)] * n_out,
        compiler_params=_params("parallel"),
    )(chip_arr, *pairs, *recvs, *([] if adam is None else [t for group in adam for t in group]))
    return out if adam is None else (out[:n], out[n:2 * n], out[2 * n:3 * n], out[3 * n:])


def _adamw(ws, gs, ms, vs, name):
    n = len(ws)

    def body(*refs):
        for a in range(n):
            w_ref, g_ref, m_ref, v_ref = (refs[s * n + a] for s in range(4))
            refs[4 * n + a][...], refs[5 * n + a][...], refs[6 * n + a][...] = _adamw_math(
                w_ref[...], g_ref[...], m_ref[...], v_ref[...])

    out = pl.pallas_call(body, name=name, out_shape=[jax.ShapeDtypeStruct(w.shape, F32) for w in ws] * 3)(
        *ws, *gs, *ms, *vs)
    return out[:n], out[n:2 * n], out[2 * n:]


def _sum_small(small_all):
    def body(s_ref, o_ref):
        g = s_ref[0]
        for d in range(1, N_DEV):
            g = g + s_ref[d]
        o_ref[...] = g

    return pl.pallas_call(body, name="sum_small", out_shape=jax.ShapeDtypeStruct(small_all.shape[1:], F32))(small_all)


def _rope_tables():
    inv_freq = ROPE_THETA ** (-jnp.arange(0, HEAD_DIM, 2, dtype=F32) / HEAD_DIM)
    ang = jnp.arange(SEQ_LEN).astype(F32)[:, None] * inv_freq[None, :]
    cos, sin = jnp.cos(ang), jnp.sin(ang)
    return jnp.tile(cos, (1, 4)), jnp.tile(jnp.concatenate([-sin, sin], axis=1), (1, 2))


def _local_step(x, target, g_pre, g_post, sinks, wt, wconv, squares, start_exchange=None):
    cos_t, sin_t = _rope_tables()
    wconv8 = jnp.pad(wconv, ((0, 5), (0, 0)))
    h, q, kv, g3 = _fwd_in_attn(x, g_pre, wt, cos_t, sin_t, 512)
    wpc, wpa, wout = squares(kv)
    a4, ya = _fwd_in_conv(h, wt, wconv8, wpc, 512)
    attn, ub = _fwd_attn(sinks, q, kv, g3, 4)
    loss8, dout, dya, dub, dgab, dwout, dwpa, dgpost8 = _fwd_out_bwd_head(ya, ub, g3, x, target, g_post, wpa, wout, 512)
    dq, dza, dkv_own, dkv_prev, dsink8, dh_part = _bwd_attn(sinks, q, kv, attn, dub, g3, cos_t, sin_t, wt, 4)
    dkv = _bwd_kv_finish(dkv_own, dkv_prev, cos_t, sin_t)
    da4, dwpc, dwconv8, dwt = _bwd_conv(dya, a4, h, wconv8, wpc, 512, 2)
    dwt = _bwd_dw_in(h, dq, ROW_Q, 1024, 1024, "bwd_dw_in_q", dwt)
    dwt = _bwd_dw_in(h, dkv, ROW_KV, 256, 1024, "bwd_dw_in_kv", dwt)
    dwt = _bwd_dw_in(h, dza, ROW_ZA, 1024, 1024, "bwd_dw_in_za", dwt)
    dwt32, dwt16 = _bwd_dw_in(h, dgab, ROW_GA, 1024, 1024, "bwd_dw_in_gates", dwt)
    token, pending = (None, None) if start_exchange is None else start_exchange(dwt32, dwt16, dwpc, dwpa, dwout)
    g_pre_after = g_pre if token is None else g_pre + token[0:1, 0:1]
    grad_x, dgpre8 = _bwd_dh(da4, dh_part, dkv, dgab, wt, x, g_pre_after, dout, 512)
    small = jnp.concatenate([dgpre8, dgpost8, jnp.pad(dsink8, ((0, 0), (0, D - 128))), dwconv8,
                             jnp.pad(loss8, ((0, 0), (0, D - 128)))], axis=0)
    return loss8[0, 0], grad_x, dwt32, dwt16, dwpc, dwpa, dwout, small, pending


def kernel(x, g_pre, g_post, w_in, w_conv, sinks, w_proj_conv, w_proj_attn, w_out, loss_target, m_g_pre, m_g_post, m_w_in, m_w_conv, m_sinks, m_w_proj_conv, m_w_proj_attn, m_w_out, v_g_pre, v_g_post, v_w_in, v_w_conv, v_sinks, v_w_proj_conv, v_w_proj_attn, v_w_out):
    batch = x.shape[0]
    mx, my, mc, me = _place()
    c_arr = jnp.reshape(mc, (1,)).astype(jnp.int32)
    chip_arr = jnp.reshape(2 * mx + my, (1,)).astype(jnp.int32)

    g_wt, g_conv = _all_gather([w_in[0].T.astype(BF16), jnp.pad(w_conv[0], ((0, 5), (0, 0)))])
    wt = g_wt.reshape(D_IN, D)
    wconv = g_conv[:, 0:3, :].transpose(1, 0, 2).reshape(3, D)
    sq_mine = [w.astype(BF16) for w in (w_proj_conv[0], w_proj_attn[0], w_out[0])]
    wt, sq_mine = lax.optimization_barrier((wt, sq_mine))
    sq_send, sq_recv, sq_flying, sq_lands, sq_token = _gather_start(sq_mine, "gather_squares")

    def squares(after):
        got = _gather_wait(sq_send, sq_recv, sq_flying, sq_lands, after, "gather_squares")
        return [lax.dynamic_update_index_in_dim(full, mine, me, 0).reshape(D, D) for full, mine in zip(got, sq_mine)]

    def start_exchange(dwt32, dwt16, dwpc, dwpa, dwout):
        own_sq = [g.reshape(N_DEV, SHARD_SQ, D) for g in (dwpc, dwpa, dwout)]
        own_in = dwt32.reshape(N_DEV, SHARD_IN, D)
        from_sibling = _exchange_sibling([dwt16.reshape(N_DEV, SHARD_IN, D)] + [g.astype(BF16) for g in own_sq])
        in32, in16 = _pair_sum([own_in], from_sibling[:1], c_arr, SHARD_IN // 2, "pair_sum_w_in")
        sq32, sq16 = _pair_sum(own_sq, from_sibling[1:], c_arr, SHARD_SQ, "pair_sum_squares")
        send_sems, recv_sems, flying, lands, token = _exchange_chips_start(list(in16) + list(sq16))
        return token, (send_sems, recv_sems, flying, lands, in32, sq32)

    _, grad_x, _, _, _, _, _, small, pending = _local_step(
        x.reshape(batch * SEQ_LEN, D), loss_target.reshape(batch * SEQ_LEN, D), g_pre + sq_token[0:1, 0:1], g_post,
        sinks, wt, wconv, squares, start_exchange)
    sm_send, sm_recv, sm_flying, sm_lands, sm_token = _gather_start([small], "gather_small")
    send_sems, recv_sems, flying, lands, in32, sq32 = pending
    from_chips = _exchange_chips_wait(send_sems, recv_sems, flying, lands, sm_token)

    o_in = [o[0].T for o in _chip_sum(
        in32, from_chips[:1], chip_arr, SHARD_IN // 3, "chip_sum_adamw_w_in",
        adam=([w_in[0].T], [m_w_in[0].T], [v_w_in[0].T]))]
    g_in_mine, o_in = o_in[0], o_in[1:]
    g_sq, d_sq, m_sq, v_sq = _chip_sum(
        sq32, from_chips[1:], chip_arr, SHARD_SQ, "chip_sum_adamw_squares",
        adam=([w_proj_conv[0], w_proj_attn[0], w_out[0]], [m_w_proj_conv[0], m_w_proj_attn[0], m_w_out[0]],
              [v_w_proj_conv[0], v_w_proj_attn[0], v_w_out[0]]))
    both_done, g_in_mine = lax.optimization_barrier((d_sq[0], g_in_mine))
    (small_all,) = _gather_wait(sm_send, sm_recv, sm_flying, sm_lands, both_done, "gather_small")
    gs = _sum_small(lax.dynamic_update_index_in_dim(small_all, small, me, 0))
    g_g_pre, g_g_post, g_sinks, loss = gs[0:1], gs[8:9], gs[16:17, 0:N_HEADS], gs[32, 0]
    g_conv_mine = lax.dynamic_slice_in_dim(gs[24:27], me * SHARD_SQ, SHARD_SQ, axis=1)
    o_small = _adamw([g_pre, g_post, sinks, w_conv[0]], [g_g_pre, g_g_post, g_sinks, g_conv_mine],
                     [m_g_pre, m_g_post, m_sinks, m_w_conv[0]], [v_g_pre, v_g_post, v_sinks, v_w_conv[0]], "adamw_small")

    grads = [g_g_pre, g_g_post, g_in_mine[None], g_conv_mine[None], g_sinks] + [g[None] for g in g_sq]
    rest = []
    for idx, sq in enumerate((d_sq, m_sq, v_sq)):
        gp, gq, sk, cv = o_small[idx]
        rest += [gp, gq, o_in[idx][None], cv[None], sk] + [s[None] for s in sq]
    return (loss, grad_x.reshape(batch, SEQ_LEN, D), *grads, *rest)
```

```python
import jax
import jax.numpy as jnp
from jax import lax
from jax.experimental import pallas as pl
from jax.experimental.pallas import tpu as pltpu

D = 1024
N_HEADS = 16
HEAD_DIM = 64
LOGIT_SCALE = HEAD_DIM ** -0.5
BLK = 128
SEQ_LEN = 2048
D_IN = 8448
ROW_Q, ROW_KV, ROW_ZA, ROW_GA = 4 * D, 5 * D, 5 * D + 256, 6 * D + 256
SHARD_IN = D_IN // 8
SHARD_SQ = D // 8
N_DEV = 8
V7X_VMEM_BYTES = 64 << 20
ROPE_THETA = 10000.0
RMS_EPS = 1e-6
NEG = -1e30
ADAM_LR, ADAM_B1, ADAM_B2, ADAM_EPS, ADAM_WD, ADAM_STEP = 0.001, 0.9, 0.999, 1e-08, 0.01, 10

F32 = jnp.float32
BF16 = jnp.bfloat16
MESH_ID = pl.DeviceIdType.MESH


def _dot(a, b):
    return jnp.dot(a, b, preferred_element_type=F32)


def _dot_nt(a, b):
    return lax.dot_general(a, b, (((1,), (1,)), ((), ())), preferred_element_type=F32)


def _dot_tn(a, b):
    return lax.dot_general(a, b, (((0,), (0,)), ((), ())), preferred_element_type=F32)


def _sig(z):
    return 1.0 / (1.0 + jnp.exp(-z))


def _swap_halves(z):
    lane = lax.broadcasted_iota(jnp.int32, z.shape, 1)
    return jnp.where((lane & 63) < 32, pltpu.roll(z, 96, 1), pltpu.roll(z, 32, 1))


def _row_spec(tm, width, col=0):
    return pl.BlockSpec((tm, width), lambda i: (i, col))


def _whole_vmem():
    return pl.BlockSpec(memory_space=pltpu.VMEM)


def _params(*sem, vmem_limit_bytes=None):
    return pltpu.CompilerParams(dimension_semantics=sem, vmem_limit_bytes=vmem_limit_bytes)


def _fwd_in_attn(x, g_pre, wt, cos_t, sin_t, tm):
    t = x.shape[0]
    seq_tiles = SEQ_LEN // tm

    def body(x_ref, g_ref, w_ref, c_ref, s_ref, h_ref, q_ref, kv_ref, g3_ref):
        xf = x_ref[...]
        r = lax.rsqrt(jnp.mean(xf * xf, axis=-1, keepdims=True) + RMS_EPS)
        hh = ((xf * r) * g_ref[...]).astype(BF16)
        h_ref[...] = hh
        c = c_ref[...]
        s = s_ref[...]

        def rope(z):
            return z * c + _swap_halves(z) * s

        q = _dot_nt(hh, w_ref[ROW_Q:ROW_Q + D, :])
        for j in range(D // 128):
            q_ref[:, j * 128:(j + 1) * 128] = (rope(q[:, j * 128:(j + 1) * 128]) * LOGIT_SCALE).astype(BF16)
        kv = _dot_nt(hh, w_ref[ROW_KV:ROW_KV + 256, :])
        kv_ref[:, 0:128] = rope(kv[:, 0:128]).astype(BF16)
        kv_ref[:, 128:256] = kv[:, 128:256].astype(BF16)
        for j in range(3):
            g3_ref[:, j * D:(j + 1) * D] = _dot_nt(hh, w_ref[ROW_ZA + j * D:ROW_ZA + (j + 1) * D, :])

    tab = pl.BlockSpec((tm, 128), lambda i: (i % seq_tiles, 0))
    return pl.pallas_call(
        body, name="fwd_in_attn", grid=(t // tm,),
        in_specs=[_row_spec(tm, D), pl.BlockSpec((1, D), lambda i: (0, 0)), _whole_vmem(), tab, tab],
        out_specs=[_row_spec(tm, D), _row_spec(tm, D), _row_spec(tm, 256), _row_spec(tm, 3 * D)],
        out_shape=[jax.ShapeDtypeStruct((t, D), BF16), jax.ShapeDtypeStruct((t, D), BF16),
                   jax.ShapeDtypeStruct((t, 256), BF16), jax.ShapeDtypeStruct((t, 3 * D), F32)],
        compiler_params=_params("parallel"),
    )(x, g_pre, wt, cos_t, sin_t)


def _conv_forward(xc, bg, cg, zc, up6, up7, w_ref):
    rows = lax.broadcasted_iota(jnp.int32, xc.shape, 0)
    u = cg * xc
    u_m1 = jnp.where(rows == 0, up7, pltpu.roll(u, 1, 0))
    u_m2 = jnp.where(rows == 0, up6, jnp.where(rows == 1, up7, pltpu.roll(u, 2, 0)))
    yconv = w_ref[0:1, :] * u_m2 + w_ref[1:2, :] * u_m1 + w_ref[2:3, :] * u
    sg = _sig(zc)
    sz = zc * sg
    co = bg * yconv
    return u, u_m1, u_m2, yconv, sg, sz, co


def _fwd_in_conv(h, wt, wconv8, wpc, tm):
    t = h.shape[0]
    seq_tiles = SEQ_LEN // tm

    def body(h_ref, w_ref, wc_ref, wpc_ref, a4_ref, ya_ref, last_u_ref):
        hh = h_ref[...]
        xc, bg, cg, zc = (_dot_nt(hh, w_ref[j * D:(j + 1) * D, :]) for j in range(4))
        for j, z in enumerate((xc, bg, cg, zc)):
            a4_ref[:, j * D:(j + 1) * D] = z.astype(BF16)
        first = pl.program_id(0) % seq_tiles == 0
        up6 = jnp.where(first, 0.0, last_u_ref[6:7, :])
        up7 = jnp.where(first, 0.0, last_u_ref[7:8, :])
        u, _, _, _, _, sz, co = _conv_forward(xc, bg, cg, zc, up6, up7, wc_ref)
        last_u_ref[...] = u[tm - 8:tm, :]
        ya_ref[...] = _dot((sz * co).astype(BF16), wpc_ref[...])

    return pl.pallas_call(
        body, name="fwd_in_conv", grid=(t // tm,),
        in_specs=[_row_spec(tm, D), _whole_vmem(), pl.BlockSpec((8, D), lambda i: (0, 0)), _whole_vmem()],
        out_specs=[_row_spec(tm, 4 * D), _row_spec(tm, D)],
        out_shape=[jax.ShapeDtypeStruct((t, 4 * D), BF16), jax.ShapeDtypeStruct((t, D), F32)],
        scratch_shapes=[pltpu.VMEM((8, D), F32)],
        compiler_params=_params("arbitrary"),
    )(h, wt, wconv8, wpc)


STACK = 4 * BLK


def _band_mask(first):
    qi = lax.broadcasted_iota(jnp.int32, (STACK, 2 * BLK), 0) & (BLK - 1)
    kj = lax.broadcasted_iota(jnp.int32, (STACK, 2 * BLK), 1)
    return (kj > qi) & (kj <= qi + BLK) & (kj >= jnp.where(first, BLK, 0))


def _masked_fill(sink_ref, g, e):
    kj = lax.broadcasted_iota(jnp.int32, (STACK, 2 * BLK), 1)
    sink = jnp.concatenate([jnp.full((BLK, 2 * BLK), sink_ref[0, 2 * (4 * g + jj) + e], F32) for jj in range(4)], axis=0)
    return jnp.where(kj == 0, sink, NEG)


def _padded_pair(before, own, other=0.0):
    z = jnp.concatenate([before, own], axis=0).astype(F32)
    z = jnp.where(lax.broadcasted_iota(jnp.int32, z.shape, 0) == 0, 0.0, z)
    zs = pltpu.roll(z, 64, 1)
    lo = lax.broadcasted_iota(jnp.int32, z.shape, 1) < 64
    fill = jnp.full_like(z, other)
    left = [jnp.where(lo, z, fill).astype(BF16), jnp.where(lo, zs, fill).astype(BF16)]
    right = [jnp.where(lo, fill, zs).astype(BF16), jnp.where(lo, fill, z).astype(BF16)]
    return left, right


def _exp_logits(s, valid, fill):
    s = jnp.where(valid, s, fill)
    return jnp.exp(s - jnp.max(s, axis=-1, keepdims=True))


def _kv_blocks(kvc_ref, kvp_ref, b, col):
    own = kvc_ref[b * BLK:(b + 1) * BLK, col:col + 128]
    before = kvp_ref[:, col:col + 128] if b == 0 else kvc_ref[(b - 1) * BLK:b * BLK, col:col + 128]
    return before, own


def _fwd_attn(sinks, q, kv, g3, blocks):
    t = q.shape[0]
    tq = blocks * BLK
    seq_blocks = SEQ_LEN // BLK

    def body(sink_ref, q_ref, kvc_ref, kvp_ref, za_ref, attn_ref, ub_ref):
        lo = lax.broadcasted_iota(jnp.int32, (STACK, 128), 1) < 64
        for b in range(blocks):
            rows = slice(b * BLK, (b + 1) * BLK)
            valid = _band_mask((pl.program_id(0) * blocks + b) % seq_blocks == 0)
            k_pad = _padded_pair(*_kv_blocks(kvc_ref, kvp_ref, b, 0))
            v_one = _padded_pair(*_kv_blocks(kvc_ref, kvp_ref, b, 128), other=1.0)
            for g in range(2):
                qg = jnp.concatenate([q_ref[rows, j * 128:(j + 1) * 128] for j in range(4 * g, 4 * g + 4)], axis=0)
                pv = [_dot(_exp_logits(_dot_nt(qg, k_pad[e][g]), valid, _masked_fill(sink_ref, g, e)).astype(BF16),
                           v_one[e][g]) for e in range(2)]
                o = jnp.where(lo, pv[0], pv[1]) / pltpu.roll(jnp.where(lo, pv[1], pv[0]), 64, 1)
                for jj in range(4):
                    cols = slice((4 * g + jj) * 128, (4 * g + jj + 1) * 128)
                    oj = o[jj * BLK:(jj + 1) * BLK, :]
                    attn_ref[rows, cols] = oj
                    za = za_ref[rows, cols]
                    ub_ref[rows, cols] = (za * _sig(za) * oj).astype(BF16)

    return pl.pallas_call(
        body, name="fwd_attn", grid=(t // tq,),
        in_specs=[pl.BlockSpec(memory_space=pltpu.SMEM), _row_spec(tq, D), _row_spec(tq, 256),
                  pl.BlockSpec((BLK, 256), lambda i: (jnp.maximum(i * blocks - 1, 0), 0)), _row_spec(tq, D, 0)],
        out_specs=[_row_spec(tq, D), _row_spec(tq, D)],
        out_shape=[jax.ShapeDtypeStruct((t, D), F32), jax.ShapeDtypeStruct((t, D), BF16)],
        compiler_params=_params("parallel"),
    )(sinks, q, kv, kv, g3)


def _fwd_out_bwd_head(ya, ub, g3, x, target, g_post, wpa, wout, tm):
    t = x.shape[0]

    def body(ya_ref, ub_ref, ga_ref, gb_ref, x_ref, tgt_ref, gp_ref, wpa_ref, wout_ref,
             loss_ref, dout_ref, dya_ref, dub_ref, dgab_ref, dwout_ref, dwpa_ref, dgp_ref):
        @pl.when(pl.program_id(0) == 0)
        def _():
            loss_ref[...] = jnp.zeros_like(loss_ref)
            dwout_ref[...] = jnp.zeros_like(dwout_ref)
            dwpa_ref[...] = jnp.zeros_like(dwpa_ref)
            dgp_ref[...] = jnp.zeros_like(dgp_ref)

        g = gp_ref[...]
        halves = (slice(0, tm // 2), slice(tm // 2, tm))

        def stage1(rows):
            return _dot(ub_ref[rows, :], wpa_ref[...])

        def stage2(rows, yb):
            sa = _sig(ga_ref[rows, :])
            sb = _sig(gb_ref[rows, :])
            mb = (sa * ya_ref[rows, :] + sb * yb).astype(BF16)
            return sa, sb, mb, _dot(mb, wout_ref[...])

        def stage3(rows, y):
            r = lax.rsqrt(jnp.mean(y * y, axis=-1, keepdims=True) + RMS_EPS)
            n = y * r
            err = (x_ref[rows, :] + n * g) - tgt_ref[rows, :]
            sq = jnp.sum(jnp.sum(err * err, axis=0, keepdims=True), axis=1, keepdims=True)
            dout = err * (1.0 / D)
            dout_ref[rows, :] = dout
            dgp = jnp.sum(dout * n, axis=0, keepdims=True)
            dn = dout * g
            dy = (r * (dn - n * jnp.mean(dn * n, axis=-1, keepdims=True))).astype(BF16)
            return sq, dgp, dy, _dot_nt(dy, wout_ref[...])

        def stage4(rows, dm, sa, sb, yb):
            dya_ref[rows, :] = (dm * sa).astype(BF16)
            dyb = (dm * sb).astype(BF16)
            dgab_ref[rows, 0:D] = (dm * ya_ref[rows, :] * (sa * (1.0 - sa))).astype(BF16)
            dgab_ref[rows, D:2 * D] = (dm * yb * (sb * (1.0 - sb))).astype(BF16)
            dub_ref[rows, :] = _dot_nt(dyb, wpa_ref[...])
            return dyb

        yb = [stage1(rows) for rows in halves]
        s2 = [stage2(rows, yb[k]) for k, rows in enumerate(halves)]
        s3 = [stage3(rows, s2[k][3]) for k, rows in enumerate(halves)]
        dyb = [stage4(rows, s3[k][3], s2[k][0], s2[k][1], yb[k]) for k, rows in enumerate(halves)]
        loss_ref[...] += sum(s[0] for s in s3) * (0.5 / D)
        dgp_ref[0:1, :] += sum(s[1] for s in s3)
        dwout_ref[...] += _dot_tn(jnp.concatenate([s[2] for s in s2], axis=0), jnp.concatenate([s[2] for s in s3], axis=0))
        dwpa_ref[...] += _dot_tn(ub_ref[...], jnp.concatenate(dyb, axis=0))

    return pl.pallas_call(
        body, name="fwd_out_bwd_head", grid=(t // tm,),
        in_specs=[_row_spec(tm, D), _row_spec(tm, D), _row_spec(tm, D, 1), _row_spec(tm, D, 2),
                  _row_spec(tm, D), _row_spec(tm, D), pl.BlockSpec((1, D), lambda i: (0, 0)),
                  _whole_vmem(), _whole_vmem()],
        out_specs=[pl.BlockSpec((8, 128), lambda i: (0, 0)), _row_spec(tm, D), _row_spec(tm, D), _row_spec(tm, D),
                   _row_spec(tm, 2 * D), _whole_vmem(), _whole_vmem(), pl.BlockSpec((8, D), lambda i: (0, 0))],
        out_shape=[jax.ShapeDtypeStruct((8, 128), F32), jax.ShapeDtypeStruct((t, D), F32),
                   jax.ShapeDtypeStruct((t, D), BF16), jax.ShapeDtypeStruct((t, D), F32),
                   jax.ShapeDtypeStruct((t, 2 * D), BF16), jax.ShapeDtypeStruct((D, D), F32),
                   jax.ShapeDtypeStruct((D, D), F32), jax.ShapeDtypeStruct((8, D), F32)],
        compiler_params=_params("arbitrary", vmem_limit_bytes=V7X_VMEM_BYTES - (2 << 20)),
    )(ya, ub, g3, g3, x, target, g_post, wpa, wout)


def _bwd_attn(sinks, q, kv, attn, dub, g3, cos_t, sin_t, wt, blocks):
    t = q.shape[0]
    tq = blocks * BLK
    seq_blocks = SEQ_LEN // BLK

    def body(sink_ref, q_ref, kvc_ref, kvp_ref, attn_ref, dub_ref, za_ref, c_ref, s_ref, w_ref,
             dq_ref, dza_ref, dkv_own_ref, dkv_prev_ref, dsink_ref, dh_ref):
        @pl.when(pl.program_id(0) == 0)
        def _():
            dsink_ref[...] = jnp.zeros_like(dsink_ref)

        lo = lax.broadcasted_iota(jnp.int32, (STACK, 128), 1) < 64
        lane8 = lax.broadcasted_iota(jnp.int32, (8, 128), 1)
        lo2 = lax.broadcasted_iota(jnp.int32, (2 * BLK, 128), 1) < 64
        sink_row = lax.broadcasted_iota(jnp.int32, (2 * BLK, 128), 0) == 0
        dsink = jnp.zeros((8, 128), F32)
        for b in range(blocks):
            rows = slice(b * BLK, (b + 1) * BLK)
            valid = _band_mask((pl.program_id(0) * blocks + b) % seq_blocks == 0)
            k_pad = _padded_pair(*_kv_blocks(kvc_ref, kvp_ref, b, 0))
            v_pad = _padded_pair(*_kv_blocks(kvc_ref, kvp_ref, b, 128))
            c = c_ref[rows, :]
            s = s_ref[rows, :]
            dk_acc, dv_acc = [], []
            for g in range(2):
                qg, dog = [], []
                for j in range(4 * g, 4 * g + 4):
                    cols = slice(j * 128, (j + 1) * 128)
                    za = za_ref[rows, cols]
                    sg = _sig(za)
                    dub = dub_ref[rows, cols]
                    dza_ref[rows, cols] = (dub * attn_ref[rows, cols] * (sg * (1.0 + za * (1.0 - sg)))).astype(BF16)
                    dog.append((dub * (za * sg)).astype(BF16))
                    qg.append(q_ref[rows, cols])
                qg = jnp.concatenate(qg, axis=0)
                dog = jnp.concatenate(dog, axis=0)
                dq = jnp.zeros((STACK, 128), F32)
                ds_both, p_both = [], []
                for e in range(2):
                    p = _exp_logits(_dot_nt(qg, k_pad[e][g]), valid, _masked_fill(sink_ref, g, e))
                    p = p / jnp.sum(p, axis=-1, keepdims=True)
                    dp = _dot_nt(dog, v_pad[e][g])
                    ds = p * (dp - jnp.sum(p * dp, axis=-1, keepdims=True))
                    for jj in range(4):
                        tot = jnp.sum(ds[jj * BLK:(jj + 1) * BLK, 0:1], axis=0, keepdims=True)
                        dsink = dsink + jnp.where(lane8 == 2 * (4 * g + jj) + e, tot, 0.0)
                    ds = ds.astype(BF16)
                    dq = dq + _dot(ds, k_pad[e][g])
                    ds_both.append(ds)
                    p_both.append(p.astype(BF16))
                zero = jnp.zeros_like(qg)
                q2 = jnp.concatenate([jnp.where(lo, qg, zero), jnp.where(lo, zero, qg)], axis=0)
                do2 = jnp.concatenate([jnp.where(lo, dog, zero), jnp.where(lo, zero, dog)], axis=0)
                dk_acc.append(_dot_tn(q2, jnp.concatenate(ds_both, axis=0)).T)
                dv_acc.append(_dot_tn(do2, jnp.concatenate(p_both, axis=0)).T)
                for jj in range(4):
                    cols = slice((4 * g + jj) * 128, (4 * g + jj + 1) * 128)
                    dqj = dq[jj * BLK:(jj + 1) * BLK, :] * LOGIT_SCALE
                    dq_ref[rows, cols] = (dqj * c - _swap_halves(dqj) * s).astype(BF16)
            for col, acc in ((0, dk_acc), (128, dv_acc)):
                both = jnp.where(lo2, acc[0] + pltpu.roll(acc[0], 64, 1), acc[1] + pltpu.roll(acc[1], 64, 1))
                both = jnp.where(sink_row, 0.0, both)
                dkv_prev_ref[rows, col:col + 128] = both[0:BLK, :]
                dkv_own_ref[rows, col:col + 128] = both[BLK:2 * BLK, :]
        dsink_ref[...] += dsink
        dh_ref[...] = _dot(dq_ref[...], w_ref[ROW_Q:ROW_KV, :]) + _dot(dza_ref[...], w_ref[ROW_ZA:ROW_GA, :])

    tab = pl.BlockSpec((tq, 128), lambda i: (i % (SEQ_LEN // tq), 0))
    return pl.pallas_call(
        body, name="bwd_attn", grid=(t // tq,),
        in_specs=[pl.BlockSpec(memory_space=pltpu.SMEM), _row_spec(tq, D), _row_spec(tq, 256),
                  pl.BlockSpec((BLK, 256), lambda i: (jnp.maximum(i * blocks - 1, 0), 0)),
                  _row_spec(tq, D), _row_spec(tq, D), _row_spec(tq, D, 0), tab, tab, _whole_vmem()],
        out_specs=[_row_spec(tq, D), _row_spec(tq, D), _row_spec(tq, 256), _row_spec(tq, 256),
                   pl.BlockSpec((8, 128), lambda i: (0, 0)), _row_spec(tq, D)],
        out_shape=[jax.ShapeDtypeStruct((t, D), BF16), jax.ShapeDtypeStruct((t, D), BF16),
                   jax.ShapeDtypeStruct((t, 256), F32), jax.ShapeDtypeStruct((t, 256), F32),
                   jax.ShapeDtypeStruct((8, 128), F32), jax.ShapeDtypeStruct((t, D), F32)],
        compiler_params=_params("arbitrary"),
    )(sinks, q, kv, kv, attn, dub, g3, cos_t, sin_t, wt)


def _bwd_kv_finish(dkv_own, dkv_prev, cos_t, sin_t):
    t = dkv_own.shape[0]
    tm = 512
    seq_tiles = SEQ_LEN // tm
    n_blocks = t // BLK

    def body(own_ref, same_ref, nxt_ref, c_ref, s_ref, out_ref):
        keep = jnp.where(pl.program_id(0) % seq_tiles == seq_tiles - 1, 0.0, 1.0)
        shifted = jnp.concatenate([same_ref[BLK:tm, :], nxt_ref[...] * keep], axis=0)
        tot = own_ref[...] + shifted
        dk = tot[:, 0:128]
        out_ref[:, 0:128] = (dk * c_ref[...] - _swap_halves(dk) * s_ref[...]).astype(BF16)
        out_ref[:, 128:256] = tot[:, 128:256].astype(BF16)

    tab = pl.BlockSpec((tm, 128), lambda i: (i % seq_tiles, 0))
    return pl.pallas_call(
        body, name="bwd_kv_finish", grid=(t // tm,),
        in_specs=[_row_spec(tm, 256), _row_spec(tm, 256),
                  pl.BlockSpec((BLK, 256), lambda i: (jnp.minimum((i + 1) * (tm // BLK), n_blocks - 1), 0)), tab, tab],
        out_specs=_row_spec(tm, 256),
        out_shape=jax.ShapeDtypeStruct((t, 256), BF16),
        compiler_params=_params("parallel"),
    )(dkv_own, dkv_prev, dkv_prev, cos_t, sin_t)


STAGE_ROWS = 256


def _bwd_conv(dya, a4, h, wconv8, wpc, tm, parts):
    t = a4.shape[0]
    n_t = t // tm
    sub = tm // parts
    seq_tiles = SEQ_LEN // tm

    def body(dya_ref, xc_ref, bg_ref, cg_ref, zc_ref, xcp_ref, cgp_ref, w_ref, wpc_ref, h_ref,
             da4_ref, dwpc_ref, dwc_ref, o32_ref, o16_ref, acc_ref, stage_ref, later_ref, sems):
        step = pl.program_id(0)
        tile = n_t - 1 - step

        @pl.when(step == 0)
        def _():
            dwpc_ref[...] = jnp.zeros_like(dwpc_ref)
            dwc_ref[...] = jnp.zeros_like(dwc_ref)
            acc_ref[...] = jnp.zeros_like(acc_ref)

        keep_prev = jnp.where(tile % seq_tiles == 0, 0.0, 1.0)
        ends_sequence = tile % seq_tiles == seq_tiles - 1

        def part(p, later):
            r0 = p * sub
            here = slice(r0, r0 + sub)
            if p == 0:
                u_prev = cgp_ref[14:16, :].astype(F32) * xcp_ref[14:16, :].astype(F32) * keep_prev
            else:
                u_prev = cg_ref[r0 - 2:r0, :].astype(F32) * xc_ref[r0 - 2:r0, :].astype(F32)
            xc = xc_ref[here, :].astype(F32)
            bg = bg_ref[here, :].astype(F32)
            cg = cg_ref[here, :].astype(F32)
            zc = zc_ref[here, :].astype(F32)
            u, u_m1, u_m2, yconv, sg, sz, co = _conv_forward(xc, bg, cg, zc, u_prev[0:1, :], u_prev[1:2, :], w_ref)
            ua = (sz * co).astype(BF16)
            dua = _dot_nt(dya_ref[here, :], wpc_ref[...])
            da4_ref[here, 3 * D:4 * D] = (dua * co * (sg * (1.0 + zc * (1.0 - sg)))).astype(BF16)
            dco = dua * sz
            da4_ref[here, D:2 * D] = (dco * yconv).astype(BF16)
            dyc = dco * bg
            dwc = jnp.concatenate([jnp.sum(dyc * s, axis=0, keepdims=True) for s in (u_m2, u_m1, u)], axis=0)
            rows = lax.broadcasted_iota(jnp.int32, xc.shape, 0)
            n0 = later[0:1, :]
            n1 = later[1:2, :]
            dyc_p1 = jnp.where(rows == sub - 1, n0, pltpu.roll(dyc, sub - 1, 0))
            dyc_p2 = jnp.where(rows == sub - 2, n0, jnp.where(rows == sub - 1, n1, pltpu.roll(dyc, sub - 2, 0)))
            du = w_ref[2:3, :] * dyc + w_ref[1:2, :] * dyc_p1 + w_ref[0:1, :] * dyc_p2
            da4_ref[here, 0:D] = (du * cg).astype(BF16)
            da4_ref[here, 2 * D:3 * D] = (du * xc).astype(BF16)
            return ua, dwc, dyc[0:8, :]

        later = jnp.where(ends_sequence, 0.0, later_ref[...])
        uas, dwc = [], jnp.zeros((3, D), F32)
        for p in reversed(range(parts)):
            ua, dwc_p, later = part(p, later)
            uas.insert(0, ua)
            dwc = dwc + dwc_p
        later_ref[...] = later
        dwpc_ref[...] += _dot_tn(jnp.concatenate(uas, axis=0), dya_ref[...])
        dwc_ref[0:3, :] += dwc
        for j in range(4):
            acc_ref[j * D:(j + 1) * D, :] += _dot_tn(da4_ref[:, j * D:(j + 1) * D], h_ref[...])

        @pl.when(step == n_t - 1)
        def _():
            c32 = pltpu.make_async_copy(acc_ref, o32_ref.at[pl.ds(0, 4 * D)], sems.at[0])
            c32.start()
            for j in range(4 * D // STAGE_ROWS):
                rows = pl.ds(j * STAGE_ROWS, STAGE_ROWS)
                stage_ref[...] = acc_ref[rows, :].astype(BF16)
                c16 = pltpu.make_async_copy(stage_ref, o16_ref.at[rows], sems.at[1])
                c16.start()
                c16.wait()
            c32.wait()

    def rows_of_tile(width, col=0):
        return pl.BlockSpec((tm, width), lambda s: (n_t - 1 - s, col))

    def prev(col):
        return pl.BlockSpec((16, D), lambda s: (jnp.maximum((n_t - 1 - s) * (tm // 16) - 1, 0), col))

    hbm = pl.BlockSpec(memory_space=pl.ANY)
    out = pl.pallas_call(
        body, name="bwd_conv", grid=(n_t,),
        in_specs=[rows_of_tile(D), rows_of_tile(D, 0), rows_of_tile(D, 1), rows_of_tile(D, 2), rows_of_tile(D, 3),
                  prev(0), prev(2), pl.BlockSpec((8, D), lambda s: (0, 0)), _whole_vmem(), rows_of_tile(D)],
        out_specs=[rows_of_tile(4 * D), _whole_vmem(), pl.BlockSpec((8, D), lambda s: (0, 0)), hbm, hbm],
        out_shape=[jax.ShapeDtypeStruct((t, 4 * D), BF16), jax.ShapeDtypeStruct((D, D), F32),
                   jax.ShapeDtypeStruct((8, D), F32), jax.ShapeDtypeStruct((D_IN, D), F32),
                   jax.ShapeDtypeStruct((D_IN, D), BF16)],
        scratch_shapes=[pltpu.VMEM((4 * D, D), F32), pltpu.VMEM((STAGE_ROWS, D), BF16), pltpu.VMEM((8, D), F32),
                        pltpu.SemaphoreType.DMA((2,))],
        compiler_params=pltpu.CompilerParams(dimension_semantics=("arbitrary",), vmem_limit_bytes=V7X_VMEM_BYTES - (2 << 20)),
    )(dya, a4, a4, a4, a4, a4, a4, wconv8, wpc, h)
    return out[0], out[1], out[2], (out[3], out[4])


def _bwd_dh(da4, dh_part, dkv, dgab, wt, x, g_pre, dout, tm):
    t = x.shape[0]

    def body(da4_ref, dhp_ref, dkv_ref, dgab_ref, w_ref, x_ref, g_ref, dout_ref, gx_ref, dg_ref):
        @pl.when(pl.program_id(0) == 0)
        def _():
            dg_ref[...] = jnp.zeros_like(dg_ref)

        dh = dhp_ref[...] + _dot(da4_ref[...], w_ref[0:ROW_Q, :])
        dh += _dot(dkv_ref[...], w_ref[ROW_KV:ROW_ZA, :])
        dh += _dot(dgab_ref[...], w_ref[ROW_GA:D_IN, :])
        xf = x_ref[...]
        r = lax.rsqrt(jnp.mean(xf * xf, axis=-1, keepdims=True) + RMS_EPS)
        xn = xf * r
        dg_ref[0:1, :] += jnp.sum(dh * xn, axis=0, keepdims=True)
        dxn = dh * g_ref[...]
        gx_ref[...] = dout_ref[...] + r * (dxn - xn * jnp.mean(dxn * xn, axis=-1, keepdims=True))

    return pl.pallas_call(
        body, name="bwd_dh", grid=(t // tm,),
        in_specs=[_row_spec(tm, 4 * D), _row_spec(tm, D), _row_spec(tm, 256), _row_spec(tm, 2 * D),
                  _whole_vmem(), _row_spec(tm, D), pl.BlockSpec((1, D), lambda i: (0, 0)), _row_spec(tm, D)],
        out_specs=[_row_spec(tm, D), pl.BlockSpec((8, D), lambda i: (0, 0))],
        out_shape=[jax.ShapeDtypeStruct((t, D), F32), jax.ShapeDtypeStruct((8, D), F32)],
        compiler_params=_params("arbitrary"),
    )(da4, dh_part, dkv, dgab, wt, x, g_pre, dout)


def _bwd_dw_in(h, piece, row0, nb, tm, name, prev):
    t, n = piece.shape
    n_t = t // tm

    def body(*refs):
        h_ref, p_ref = refs[0], refs[1]
        o32_ref, o16_ref, acc_ref, acc16_ref, sems = refs[-5:]
        j, i = pl.program_id(0), pl.program_id(1)

        @pl.when(i == 0)
        def _():
            acc_ref[...] = jnp.zeros_like(acc_ref)

        acc_ref[...] += _dot_tn(p_ref[...], h_ref[...])

        @pl.when(i == n_t - 1)
        def _():
            acc16_ref[...] = acc_ref[...].astype(BF16)
            rows = pl.ds(pl.multiple_of(row0 + j * nb, 16), nb)
            c32 = pltpu.make_async_copy(acc_ref, o32_ref.at[rows], sems.at[0])
            c16 = pltpu.make_async_copy(acc16_ref, o16_ref.at[rows], sems.at[1])
            c32.start()
            c16.start()
            c32.wait()
            c16.wait()

    hbm = pl.BlockSpec(memory_space=pl.ANY)
    carried = [] if prev is None else list(prev)
    return pl.pallas_call(
        body, name=name, grid=(n // nb, n_t),
        in_specs=[pl.BlockSpec((tm, D), lambda j, i: (i, 0)), pl.BlockSpec((tm, nb), lambda j, i: (i, j))]
        + [hbm] * len(carried),
        out_specs=[hbm, hbm],
        out_shape=[jax.ShapeDtypeStruct((D_IN, D), F32), jax.ShapeDtypeStruct((D_IN, D), BF16)],
        scratch_shapes=[pltpu.VMEM((nb, D), F32), pltpu.VMEM((nb, D), BF16), pltpu.SemaphoreType.DMA((2,))],
        input_output_aliases={2: 0, 3: 1} if carried else {},
        compiler_params=_params("arbitrary", "arbitrary"),
    )(h, piece, *carried)


def _place():
    x, y, c = lax.axis_index("x"), lax.axis_index("y"), lax.axis_index("c")
    return x, y, c, 4 * x + 2 * y + c


def _peer(x, y, c, k):
    return (1 - x if k & 4 else x, 1 - y if k & 2 else y, 1 - c if k & 1 else c)


ICI_MASKS = (4, 2, 6)


def _all_gather(shards):
    n = len(shards)

    def body(*refs):
        src, dst = refs[:n], refs[n:2 * n]
        send_sems, recv_sems, local_sems = refs[2 * n:]
        x, y, c, me = _place()
        sibling = _peer(x, y, c, 1)

        def copy(a, s, block, to, own=False):
            return pltpu.make_async_remote_copy(
                src_ref=src[a] if own else dst[a].at[block], dst_ref=dst[a].at[block],
                send_sem=send_sems.at[a * 7 + s], recv_sem=recv_sems.at[a * 7 + s], device_id=to, device_id_type=MESH_ID)

        local = [pltpu.make_async_copy(src[a], dst[a].at[me], local_sems.at[a]) for a in range(n)]
        for cp in local:
            cp.start()
        started = [copy(a, 0, me, sibling, own=True) for a in range(n)]
        started += [copy(a, 1 + j, me, _peer(x, y, c, k), own=True) for j, k in enumerate(ICI_MASKS) for a in range(n)]
        for cp in started:
            cp.start()
        for j, k in enumerate(ICI_MASKS):
            for a in range(n):
                copy(a, 1 + j, me ^ k, sibling).wait_recv()
                fwd = copy(a, 4 + j, me ^ k, sibling)
                fwd.start()
                started.append(fwd)
        for a in range(n):
            copy(a, 0, me ^ 1, sibling).wait_recv()
        for j, k in enumerate(ICI_MASKS):
            for a in range(n):
                copy(a, 4 + j, me ^ 1 ^ k, sibling).wait_recv()
        for cp in started:
            cp.wait_send()
        for cp in local:
            cp.wait()

    hbm = pl.BlockSpec(memory_space=pl.ANY)
    return pl.pallas_call(
        body, name="all_gather_weights",
        in_specs=[hbm] * n, out_specs=[hbm] * n,
        out_shape=[jax.ShapeDtypeStruct((N_DEV,) + s.shape, s.dtype) for s in shards],
        scratch_shapes=[pltpu.SemaphoreType.DMA((7 * n,)), pltpu.SemaphoreType.DMA((7 * n,)),
                        pltpu.SemaphoreType.DMA((n,))],
    )(*shards)


def _direct_copies(src, land, send_sems, recv_sems):
    x, y, c, me = _place()
    return [pltpu.make_async_remote_copy(
        src_ref=src[a], dst_ref=land[a].at[me], send_sem=send_sems.at[a * 7 + k - 1],
        recv_sem=recv_sems.at[a * 7 + k - 1], device_id=_peer(x, y, c, k), device_id_type=MESH_ID)
        for k in range(1, N_DEV) for a in range(len(src))]


def _gather_start(shards, name):
    n = len(shards)

    def body(*refs):
        src, land = refs[:n], refs[n:2 * n]
        send_sems, recv_sems = refs[2 * n], refs[2 * n + 1]
        token_ref = refs[-1]
        for cp in _direct_copies(src, land, send_sems, recv_sems):
            cp.start()
        token_ref[...] = jnp.zeros_like(token_ref)

    hbm = pl.BlockSpec(memory_space=pltpu.HBM)
    sem = pl.BlockSpec(memory_space=pltpu.SEMAPHORE)
    lands = [lax.empty((N_DEV,) + s.shape, s.dtype) for s in shards]
    out = pl.pallas_call(
        body, name=name + "_start",
        out_shape=(pltpu.SemaphoreType.DMA((7 * n,)), pltpu.SemaphoreType.DMA((7 * n,)),
                   *[pltpu.HBM(s.shape, s.dtype) for s in shards], *[pltpu.HBM(s.shape, s.dtype) for s in lands],
                   jax.ShapeDtypeStruct((8, 128), F32)),
        in_specs=[hbm] * (2 * n), out_specs=(sem, sem, *[hbm] * (2 * n), _whole_vmem()),
        input_output_aliases={a: 2 + a for a in range(2 * n)},
        compiler_params=pltpu.CompilerParams(has_side_effects=pltpu.SideEffectType.DATAFLOW_SIDE_EFFECTING),
    )(*[pltpu.with_memory_space_constraint(s, pltpu.HBM) for s in list(shards) + lands])
    return out[0], out[1], out[2:2 + n], out[2 + n:2 + 2 * n], out[-1]


def _gather_wait(send_sems, recv_sems, flying, lands, after, name):
    n = len(flying)

    def body(*refs):
        src, land = refs[:n], refs[n:2 * n]
        for cp in _direct_copies(src, land, refs[2 * n], refs[2 * n + 1]):
            cp.wait_send()
            cp.wait_recv()

    hbm = pl.BlockSpec(memory_space=pltpu.HBM)
    sem = pl.BlockSpec(memory_space=pltpu.SEMAPHORE)
    out = pl.pallas_call(
        body, name=name + "_wait",
        out_shape=tuple(pltpu.HBM(s.shape, s.dtype) for s in list(flying) + list(lands)),
        in_specs=[hbm] * (2 * n) + [sem, sem, pl.BlockSpec(memory_space=pl.ANY)], out_specs=tuple([hbm] * (2 * n)),
        input_output_aliases={a: a for a in range(2 * n)},
        compiler_params=pltpu.CompilerParams(has_side_effects=pltpu.SideEffectType.DATAFLOW_SIDE_EFFECTING),
    )(*flying, *lands, send_sems, recv_sems, after)
    return out[n:]


def _exchange_sibling(by_dest):
    n = len(by_dest)

    def body(*refs):
        src, dst = refs[:n], refs[n:2 * n]
        send_sems, recv_sems = refs[2 * n:]
        x, y, c, _ = _place()
        sibling = _peer(x, y, c, 1)
        copies = [pltpu.make_async_remote_copy(
            src_ref=src[a].at[2 * p + (1 - c)], dst_ref=dst[a].at[p], send_sem=send_sems.at[a * 4 + p],
            recv_sem=recv_sems.at[a * 4 + p], device_id=sibling, device_id_type=MESH_ID)
            for a in range(n) for p in range(4)]
        for cp in copies:
            cp.start()
        for cp in copies:
            cp.wait_recv()
        for cp in copies:
            cp.wait_send()

    hbm = pl.BlockSpec(memory_space=pl.ANY)
    return pl.pallas_call(
        body, name="exchange_sibling", in_specs=[hbm] * n, out_specs=[hbm] * n,
        out_shape=[jax.ShapeDtypeStruct((4,) + s.shape[1:], s.dtype) for s in by_dest],
        scratch_shapes=[pltpu.SemaphoreType.DMA((4 * n,)), pltpu.SemaphoreType.DMA((4 * n,))],
    )(*by_dest)


def _chip_copies(src, land, send_sems, recv_sems):
    x, y, c, _ = _place()
    chip = 2 * x + y
    return [pltpu.make_async_remote_copy(
        src_ref=src[a].at[chip ^ (k >> 1)], dst_ref=land[a].at[j], send_sem=send_sems.at[a * 3 + j],
        recv_sem=recv_sems.at[a * 3 + j], device_id=_peer(x, y, c, k), device_id_type=MESH_ID)
        for j, k in enumerate(ICI_MASKS) for a in range(len(src))]


def _exchange_chips_start(by_chip):
    n = len(by_chip)

    def body(*refs):
        src, land = refs[:n], refs[n:2 * n]
        send_sems, recv_sems = refs[2 * n], refs[2 * n + 1]
        token_ref = refs[-1]
        for cp in _chip_copies(src, land, send_sems, recv_sems):
            cp.start()
        token_ref[...] = jnp.zeros_like(token_ref)

    hbm = pl.BlockSpec(memory_space=pltpu.HBM)
    sem = pl.BlockSpec(memory_space=pltpu.SEMAPHORE)
    lands = [lax.empty((3,) + s.shape[1:], s.dtype) for s in by_chip]
    out = pl.pallas_call(
        body, name="exchange_chips_start",
        out_shape=(pltpu.SemaphoreType.DMA((3 * n,)), pltpu.SemaphoreType.DMA((3 * n,)),
                   *[pltpu.HBM(s.shape, s.dtype) for s in by_chip], *[pltpu.HBM(s.shape, s.dtype) for s in lands],
                   jax.ShapeDtypeStruct((8, 128), F32)),
        in_specs=[hbm] * (2 * n), out_specs=(sem, sem, *[hbm] * (2 * n), _whole_vmem()),
        input_output_aliases={a: 2 + a for a in range(2 * n)},
        compiler_params=pltpu.CompilerParams(has_side_effects=pltpu.SideEffectType.DATAFLOW_SIDE_EFFECTING),
    )(*[pltpu.with_memory_space_constraint(s, pltpu.HBM) for s in list(by_chip) + lands])
    return out[0], out[1], out[2:2 + n], out[2 + n:2 + 2 * n], out[-1]


def _exchange_chips_wait(send_sems, recv_sems, flying, lands, after):
    n = len(flying)

    def body(*refs):
        src, land = refs[:n], refs[n:2 * n]
        send_sems_ref, recv_sems_ref = refs[2 * n], refs[2 * n + 1]
        for cp in _chip_copies(src, land, send_sems_ref, recv_sems_ref):
            cp.wait_send()
            cp.wait_recv()

    hbm = pl.BlockSpec(memory_space=pltpu.HBM)
    sem = pl.BlockSpec(memory_space=pltpu.SEMAPHORE)
    out = pl.pallas_call(
        body, name="exchange_chips_wait",
        out_shape=tuple(pltpu.HBM(s.shape, s.dtype) for s in list(flying) + list(lands)),
        in_specs=[hbm] * (2 * n) + [sem, sem, pl.BlockSpec(memory_space=pl.ANY)], out_specs=tuple([hbm] * (2 * n)),
        input_output_aliases={a: a for a in range(2 * n)},
        compiler_params=pltpu.CompilerParams(has_side_effects=pltpu.SideEffectType.DATAFLOW_SIDE_EFFECTING),
    )(*flying, *lands, send_sems, recv_sems, after)
    return out[n:]


def _adamw_math(w, g, m, v):
    m = ADAM_B1 * m + (1.0 - ADAM_B1) * g
    v = ADAM_B2 * v + (1.0 - ADAM_B2) * (g * g)
    m_hat = m / (1.0 - ADAM_B1 ** ADAM_STEP)
    v_hat = v / (1.0 - ADAM_B2 ** ADAM_STEP)
    return -ADAM_LR * (m_hat / (jnp.sqrt(v_hat) + ADAM_EPS) + ADAM_WD * w), m, v


def _pair_sum(owns, recvs, c_arr, tr, name):
    n = len(owns)
    _, rows, cols = owns[0].shape

    def body(c_ref, *refs):
        for a in range(n):
            s = refs[a][...] + refs[n + a][...].astype(F32)
            refs[2 * n + a][...] = s
            refs[3 * n + a][...] = s.astype(BF16)

    by_chip = pl.BlockSpec((None, tr, cols), lambda p, i, c_ref: (p, i, 0))
    mine = pl.BlockSpec((None, tr, cols), lambda p, i, c_ref: (2 * p + c_ref[0], i, 0))
    out = pl.pallas_call(
        body, name=name,
        grid_spec=pltpu.PrefetchScalarGridSpec(
            num_scalar_prefetch=1, grid=(4, rows // tr), in_specs=[mine] * n + [by_chip] * n, out_specs=[by_chip] * (2 * n)),
        out_shape=[jax.ShapeDtypeStruct((4, rows, cols), F32)] * n + [jax.ShapeDtypeStruct((4, rows, cols), BF16)] * n,
        compiler_params=_params("parallel", "parallel"),
    )(c_arr, *owns, *recvs)
    return out[:n], out[n:]


def _chip_sum(pairs, recvs, chip_arr, tr, name, adam=None):
    n = len(pairs)
    _, rows, cols = pairs[0].shape
    n_state = 0 if adam is None else 3 * n

    def body(chip_ref, *refs):
        outs = refs[2 * n + n_state:]
        for a in range(n):
            g = refs[a][...]
            for j in range(3):
                g = g + refs[n + a][j].astype(F32)
            outs[a][...] = g
            if adam is not None:
                w_ref, m_ref, v_ref = (refs[2 * n + s * n + a] for s in range(3))
                outs[n + a][...], outs[2 * n + a][...], outs[3 * n + a][...] = _adamw_math(w_ref[...], g, m_ref[...], v_ref[...])

    blk = pl.BlockSpec((tr, cols), lambda i, chip_ref: (i, 0))
    n_out = n if adam is None else 4 * n
    out = pl.pallas_call(
        body, name=name,
        grid_spec=pltpu.PrefetchScalarGridSpec(
            num_scalar_prefetch=1, grid=(rows // tr,),
            in_specs=[pl.BlockSpec((None, tr, cols), lambda i, chip_ref: (chip_ref[0], i, 0))] * n
            + [pl.BlockSpec((3, tr, cols), lambda i, chip_ref: (0, i, 0))] * n + [blk] * n_state,
            out_specs=[blk] * n_out),
        out_shape=[jax.ShapeDtypeStruct((rows, cols), F32)] * n_out,
        compiler_params=_params("parallel"),
    )(chip_arr, *pairs, *recvs, *([] if adam is None else [t for group in adam for t in group]))
    return out if adam is None else (out[:n], out[n:2 * n], out[2 * n:3 * n], out[3 * n:])


def _adamw(ws, gs, ms, vs, name):
    n = len(ws)

    def body(*refs):
        for a in range(n):
            w_ref, g_ref, m_ref, v_ref = (refs[s * n + a] for s in range(4))
            refs[4 * n + a][...], refs[5 * n + a][...], refs[6 * n + a][...] = _adamw_math(
                w_ref[...], g_ref[...], m_ref[...], v_ref[...])

    out = pl.pallas_call(body, name=name, out_shape=[jax.ShapeDtypeStruct(w.shape, F32) for w in ws] * 3)(
        *ws, *gs, *ms, *vs)
    return out[:n], out[n:2 * n], out[2 * n:]


def _sum_small(small_all):
    def body(s_ref, o_ref):
        g = s_ref[0]
        for d in range(1, N_DEV):
            g = g + s_ref[d]
        o_ref[...] = g

    return pl.pallas_call(body, name="sum_small", out_shape=jax.ShapeDtypeStruct(small_all.shape[1:], F32))(small_all)


def _rope_tables():
    inv_freq = ROPE_THETA ** (-jnp.arange(0, HEAD_DIM, 2, dtype=F32) / HEAD_DIM)
    ang = jnp.arange(SEQ_LEN).astype(F32)[:, None] * inv_freq[None, :]
    cos, sin = jnp.cos(ang), jnp.sin(ang)
    return jnp.tile(cos, (1, 4)), jnp.tile(jnp.concatenate([-sin, sin], axis=1), (1, 2))


def _local_step(x, target, g_pre, g_post, sinks, wt, wconv, squares, start_exchange=None):
    cos_t, sin_t = _rope_tables()
    wconv8 = jnp.pad(wconv, ((0, 5), (0, 0)))
    h, q, kv, g3 = _fwd_in_attn(x, g_pre, wt, cos_t, sin_t, 512)
    wpc, wpa, wout = squares(kv)
    a4, ya = _fwd_in_conv(h, wt, wconv8, wpc, 512)
    attn, ub = _fwd_attn(sinks, q, kv, g3, 4)
    loss8, dout, dya, dub, dgab, dwout, dwpa, dgpost8 = _fwd_out_bwd_head(ya, ub, g3, x, target, g_post, wpa, wout, 512)
    dq, dza, dkv_own, dkv_prev, dsink8, dh_part = _bwd_attn(sinks, q, kv, attn, dub, g3, cos_t, sin_t, wt, 4)
    dkv = _bwd_kv_finish(dkv_own, dkv_prev, cos_t, sin_t)
    da4, dwpc, dwconv8, dwt = _bwd_conv(dya, a4, h, wconv8, wpc, 512, 2)
    dwt = _bwd_dw_in(h, dq, ROW_Q, 1024, 1024, "bwd_dw_in_q", dwt)
    dwt = _bwd_dw_in(h, dkv, ROW_KV, 256, 1024, "bwd_dw_in_kv", dwt)
    dwt = _bwd_dw_in(h, dza, ROW_ZA, 1024, 1024, "bwd_dw_in_za", dwt)
    dwt32, dwt16 = _bwd_dw_in(h, dgab, ROW_GA, 1024, 1024, "bwd_dw_in_gates", dwt)
    token, pending = (None, None) if start_exchange is None else start_exchange(dwt32, dwt16, dwpc, dwpa, dwout)
    g_pre_after = g_pre if token is None else g_pre + token[0:1, 0:1]
    grad_x, dgpre8 = _bwd_dh(da4, dh_part, dkv, dgab, wt, x, g_pre_after, dout, 512)
    small = jnp.concatenate([dgpre8, dgpost8, jnp.pad(dsink8, ((0, 0), (0, D - 128))), dwconv8,
                             jnp.pad(loss8, ((0, 0), (0, D - 128)))], axis=0)
    return loss8[0, 0], grad_x, dwt32, dwt16, dwpc, dwpa, dwout, small, pending


def kernel(x, g_pre, g_post, w_in, w_conv, sinks, w_proj_conv, w_proj_attn, w_out, loss_target, m_g_pre, m_g_post, m_w_in, m_w_conv, m_sinks, m_w_proj_conv, m_w_proj_attn, m_w_out, v_g_pre, v_g_post, v_w_in, v_w_conv, v_sinks, v_w_proj_conv, v_w_proj_attn, v_w_out):
    batch = x.shape[0]
    mx, my, mc, me = _place()
    c_arr = jnp.reshape(mc, (1,)).astype(jnp.int32)
    chip_arr = jnp.reshape(2 * mx + my, (1,)).astype(jnp.int32)

    g_wt, g_conv = _all_gather([w_in[0].T.astype(BF16), jnp.pad(w_conv[0], ((0, 5), (0, 0)))])
    wt = g_wt.reshape(D_IN, D)
    wconv = g_conv[:, 0:3, :].transpose(1, 0, 2).reshape(3, D)
    sq_mine = [w.astype(BF16) for w in (w_proj_conv[0], w_proj_attn[0], w_out[0])]
    wt, sq_mine = lax.optimization_barrier((wt, sq_mine))
    sq_send, sq_recv, sq_flying, sq_lands, sq_token = _gather_start(sq_mine, "gather_squares")

    def squares(after):
        got = _gather_wait(sq_send, sq_recv, sq_flying, sq_lands, after, "gather_squares")
        return [lax.dynamic_update_index_in_dim(full, mine, me, 0).reshape(D, D) for full, mine in zip(got, sq_mine)]

    def start_exchange(dwt32, dwt16, dwpc, dwpa, dwout):
        own_sq = [g.reshape(N_DEV, SHARD_SQ, D) for g in (dwpc, dwpa, dwout)]
        own_in = dwt32.reshape(N_DEV, SHARD_IN, D)
        from_sibling = _exchange_sibling([dwt16.reshape(N_DEV, SHARD_IN, D)] + [g.astype(BF16) for g in own_sq])
        in32, in16 = _pair_sum([own_in], from_sibling[:1], c_arr, SHARD_IN // 2, "pair_sum_w_in")
        sq32, sq16 = _pair_sum(own_sq, from_sibling[1:], c_arr, SHARD_SQ, "pair_sum_squares")
        send_sems, recv_sems, flying, lands, token = _exchange_chips_start(list(in16) + list(sq16))
        return token, (send_sems, recv_sems, flying, lands, in32, sq32)

    _, grad_x, _, _, _, _, _, small, pending = _local_step(
        x.reshape(batch * SEQ_LEN, D), loss_target.reshape(batch * SEQ_LEN, D), g_pre + sq_token[0:1, 0:1], g_post,
        sinks, wt, wconv, squares, start_exchange)
    sm_send, sm_recv, sm_flying, sm_lands, sm_token = _gather_start([small], "gather_small")
    send_sems, recv_sems, flying, lands, in32, sq32 = pending
    from_chips = _exchange_chips_wait(send_sems, recv_sems, flying, lands, sm_token)

    o_in = [o[0].T for o in _chip_sum(
        in32, from_chips[:1], chip_arr, SHARD_IN // 3, "chip_sum_adamw_w_in",
        adam=([w_in[0].T], [m_w_in[0].T], [v_w_in[0].T]))]
    g_in_mine, o_in = o_in[0], o_in[1:]
    g_sq, d_sq, m_sq, v_sq = _chip_sum(
        sq32, from_chips[1:], chip_arr, SHARD_SQ, "chip_sum_adamw_squares",
        adam=([w_proj_conv[0], w_proj_attn[0], w_out[0]], [m_w_proj_conv[0], m_w_proj_attn[0], m_w_out[0]],
              [v_w_proj_conv[0], v_w_proj_attn[0], v_w_out[0]]))
    both_done, g_in_mine = lax.optimization_barrier((d_sq[0], g_in_mine))
    (small_all,) = _gather_wait(sm_send, sm_recv, sm_flying, sm_lands, both_done, "gather_small")
    gs = _sum_small(lax.dynamic_update_index_in_dim(small_all, small, me, 0))
    g_g_pre, g_g_post, g_sinks, loss = gs[0:1], gs[8:9], gs[16:17, 0:N_HEADS], gs[32, 0]
    g_conv_mine = lax.dynamic_slice_in_dim(gs[24:27], me * SHARD_SQ, SHARD_SQ, axis=1)
    o_small = _adamw([g_pre, g_post, sinks, w_conv[0]], [g_g_pre, g_g_post, g_sinks, g_conv_mine],
                     [m_g_pre, m_g_post, m_sinks, m_w_conv[0]], [v_g_pre, v_g_post, v_sinks, v_w_conv[0]], "adamw_small")

    grads = [g_g_pre, g_g_post, g_in_mine[None], g_conv_mine[None], g_sinks] + [g[None] for g in g_sq]
    rest = []
    for idx, sq in enumerate((d_sq, m_sq, v_sq)):
        gp, gq, sk, cv = o_small[idx]
        rest += [gp, gq, o_in[idx][None], cv[None], sk] + [s[None] for s in sq]
    return (loss, grad_x.reshape(batch, SEQ_LEN, D), *grads, *rest)
```

```python
import jax
import jax.numpy as jnp
from jax import lax
from jax.experimental import pallas as pl
from jax.experimental.pallas import tpu as pltpu

D = 1024
N_HEADS = 16
HEAD_DIM = 64
LOGIT_SCALE = HEAD_DIM ** -0.5
BLK = 128
SEQ_LEN = 2048
D_IN = 8448
ROW_Q, ROW_KV, ROW_ZA, ROW_GA = 4 * D, 5 * D, 5 * D + 256, 6 * D + 256
SHARD_IN = D_IN // 8
SHARD_SQ = D // 8
N_DEV = 8
V7X_VMEM_BYTES = 64 << 20
ROPE_THETA = 10000.0
RMS_EPS = 1e-6
NEG = -1e30
ADAM_LR, ADAM_B1, ADAM_B2, ADAM_EPS, ADAM_WD, ADAM_STEP = 0.001, 0.9, 0.999, 1e-08, 0.01, 10

F32 = jnp.float32
BF16 = jnp.bfloat16
MESH_ID = pl.DeviceIdType.MESH


def _dot(a, b):
    return jnp.dot(a, b, preferred_element_type=F32)


def _dot_nt(a, b):
    return lax.dot_general(a, b, (((1,), (1,)), ((), ())), preferred_element_type=F32)


def _dot_tn(a, b):
    return lax.dot_general(a, b, (((0,), (0,)), ((), ())), preferred_element_type=F32)


def _sig(z):
    return 1.0 / (1.0 + jnp.exp(-z))


def _swap_halves(z):
    lane = lax.broadcasted_iota(jnp.int32, z.shape, 1)
    return jnp.where((lane & 63) < 32, pltpu.roll(z, 96, 1), pltpu.roll(z, 32, 1))


def _row_spec(tm, width, col=0):
    return pl.BlockSpec((tm, width), lambda i: (i, col))


def _whole_vmem():
    return pl.BlockSpec(memory_space=pltpu.VMEM)


def _params(*sem, vmem_limit_bytes=None):
    return pltpu.CompilerParams(dimension_semantics=sem, vmem_limit_bytes=vmem_limit_bytes)


def _fwd_in_attn(x, g_pre, wt, cos_t, sin_t, tm):
    t = x.shape[0]
    seq_tiles = SEQ_LEN // tm

    def body(x_ref, g_ref, w_ref, c_ref, s_ref, h_ref, q_ref, kv_ref, g3_ref):
        xf = x_ref[...]
        r = lax.rsqrt(jnp.mean(xf * xf, axis=-1, keepdims=True) + RMS_EPS)
        hh = ((xf * r) * g_ref[...]).astype(BF16)
        h_ref[...] = hh
        c = c_ref[...]
        s = s_ref[...]

        def rope(z):
            return z * c + _swap_halves(z) * s

        q = _dot_nt(hh, w_ref[ROW_Q:ROW_Q + D, :])
        for j in range(D // 128):
            q_ref[:, j * 128:(j + 1) * 128] = (rope(q[:, j * 128:(j + 1) * 128]) * LOGIT_SCALE).astype(BF16)
        kv = _dot_nt(hh, w_ref[ROW_KV:ROW_KV + 256, :])
        kv_ref[:, 0:128] = rope(kv[:, 0:128]).astype(BF16)
        kv_ref[:, 128:256] = kv[:, 128:256].astype(BF16)
        for j in range(3):
            g3_ref[:, j * D:(j + 1) * D] = _dot_nt(hh, w_ref[ROW_ZA + j * D:ROW_ZA + (j + 1) * D, :])

    tab = pl.BlockSpec((tm, 128), lambda i: (i % seq_tiles, 0))
    return pl.pallas_call(
        body, name="fwd_in_attn", grid=(t // tm,),
        in_specs=[_row_spec(tm, D), pl.BlockSpec((1, D), lambda i: (0, 0)), _whole_vmem(), tab, tab],
        out_specs=[_row_spec(tm, D), _row_spec(tm, D), _row_spec(tm, 256), _row_spec(tm, 3 * D)],
        out_shape=[jax.ShapeDtypeStruct((t, D), BF16), jax.ShapeDtypeStruct((t, D), BF16),
                   jax.ShapeDtypeStruct((t, 256), BF16), jax.ShapeDtypeStruct((t, 3 * D), F32)],
        compiler_params=_params("parallel"),
    )(x, g_pre, wt, cos_t, sin_t)


def _conv_forward(xc, bg, cg, zc, up6, up7, w_ref):
    rows = lax.broadcasted_iota(jnp.int32, xc.shape, 0)
    u = cg * xc
    u_m1 = jnp.where(rows == 0, up7, pltpu.roll(u, 1, 0))
    u_m2 = jnp.where(rows == 0, up6, jnp.where(rows == 1, up7, pltpu.roll(u, 2, 0)))
    yconv = w_ref[0:1, :] * u_m2 + w_ref[1:2, :] * u_m1 + w_ref[2:3, :] * u
    sg = _sig(zc)
    sz = zc * sg
    co = bg * yconv
    return u, u_m1, u_m2, yconv, sg, sz, co


def _fwd_in_conv(h, wt, wconv8, wpc, tm):
    t = h.shape[0]
    seq_tiles = SEQ_LEN // tm

    def body(h_ref, w_ref, wc_ref, wpc_ref, a4_ref, ya_ref, last_u_ref):
        hh = h_ref[...]
        xc, bg, cg, zc = (_dot_nt(hh, w_ref[j * D:(j + 1) * D, :]) for j in range(4))
        for j, z in enumerate((xc, bg, cg, zc)):
            a4_ref[:, j * D:(j + 1) * D] = z.astype(BF16)
        first = pl.program_id(0) % seq_tiles == 0
        up6 = jnp.where(first, 0.0, last_u_ref[6:7, :])
        up7 = jnp.where(first, 0.0, last_u_ref[7:8, :])
        u, _, _, _, _, sz, co = _conv_forward(xc, bg, cg, zc, up6, up7, wc_ref)
        last_u_ref[...] = u[tm - 8:tm, :]
        ya_ref[...] = _dot((sz * co).astype(BF16), wpc_ref[...])

    return pl.pallas_call(
        body, name="fwd_in_conv", grid=(t // tm,),
        in_specs=[_row_spec(tm, D), _whole_vmem(), pl.BlockSpec((8, D), lambda i: (0, 0)), _whole_vmem()],
        out_specs=[_row_spec(tm, 4 * D), _row_spec(tm, D)],
        out_shape=[jax.ShapeDtypeStruct((t, 4 * D), BF16), jax.ShapeDtypeStruct((t, D), F32)],
        scratch_shapes=[pltpu.VMEM((8, D), F32)],
        compiler_params=_params("arbitrary"),
    )(h, wt, wconv8, wpc)


STACK = 4 * BLK


def _band_mask(first):
    qi = lax.broadcasted_iota(jnp.int32, (STACK, 2 * BLK), 0) & (BLK - 1)
    kj = lax.broadcasted_iota(jnp.int32, (STACK, 2 * BLK), 1)
    return (kj > qi) & (kj <= qi + BLK) & (kj >= jnp.where(first, BLK, 0))


def _masked_fill(sink_ref, g, e):
    kj = lax.broadcasted_iota(jnp.int32, (STACK, 2 * BLK), 1)
    sink = jnp.concatenate([jnp.full((BLK, 2 * BLK), sink_ref[0, 2 * (4 * g + jj) + e], F32) for jj in range(4)], axis=0)
    return jnp.where(kj == 0, sink, NEG)


def _padded_pair(before, own, other=0.0):
    z = jnp.concatenate([before, own], axis=0).astype(F32)
    z = jnp.where(lax.broadcasted_iota(jnp.int32, z.shape, 0) == 0, 0.0, z)
    zs = pltpu.roll(z, 64, 1)
    lo = lax.broadcasted_iota(jnp.int32, z.shape, 1) < 64
    fill = jnp.full_like(z, other)
    left = [jnp.where(lo, z, fill).astype(BF16), jnp.where(lo, zs, fill).astype(BF16)]
    right = [jnp.where(lo, fill, zs).astype(BF16), jnp.where(lo, fill, z).astype(BF16)]
    return left, right


def _exp_logits(s, valid, fill):
    s = jnp.where(valid, s, fill)
    m = jnp.max(s, axis=-1, keepdims=True)
    return jnp.exp(s - m), m


def _kv_blocks(kvc_ref, kvp_ref, b, col):
    own = kvc_ref[b * BLK:(b + 1) * BLK, col:col + 128]
    before = kvp_ref[:, col:col + 128] if b == 0 else kvc_ref[(b - 1) * BLK:b * BLK, col:col + 128]
    return before, own


def _fwd_attn(sinks, q, kv, g3, blocks):
    t = q.shape[0]
    tq = blocks * BLK
    seq_blocks = SEQ_LEN // BLK

    def body(sink_ref, q_ref, kvc_ref, kvp_ref, za_ref, attn_ref, ub_ref, lse_ref):
        lo = lax.broadcasted_iota(jnp.int32, (STACK, 128), 1) < 64
        for b in range(blocks):
            rows = slice(b * BLK, (b + 1) * BLK)
            valid = _band_mask((pl.program_id(0) * blocks + b) % seq_blocks == 0)
            k_pad = _padded_pair(*_kv_blocks(kvc_ref, kvp_ref, b, 0))
            v_one = _padded_pair(*_kv_blocks(kvc_ref, kvp_ref, b, 128), other=1.0)
            for g in range(2):
                qg = jnp.concatenate([q_ref[rows, j * 128:(j + 1) * 128] for j in range(4 * g, 4 * g + 4)], axis=0)
                pv = []
                for e in range(2):
                    p, m = _exp_logits(_dot_nt(qg, k_pad[e][g]), valid, _masked_fill(sink_ref, g, e))
                    both = _dot(p.astype(BF16), jnp.concatenate([v_one[e][g], jnp.ones((2 * BLK, 128), BF16)], axis=1))
                    pv.append(both[:, 0:128])
                    lse_ref[b, 2 * g + e] = m + jnp.log(both[:, 128:256])
                o = jnp.where(lo, pv[0], pv[1]) / pltpu.roll(jnp.where(lo, pv[1], pv[0]), 64, 1)
                for jj in range(4):
                    cols = slice((4 * g + jj) * 128, (4 * g + jj + 1) * 128)
                    oj = o[jj * BLK:(jj + 1) * BLK, :]
                    attn_ref[rows, cols] = oj
                    za = za_ref[rows, cols]
                    ub_ref[rows, cols] = (za * _sig(za) * oj).astype(BF16)

    return pl.pallas_call(
        body, name="fwd_attn", grid=(t // tq,),
        in_specs=[pl.BlockSpec(memory_space=pltpu.SMEM), _row_spec(tq, D), _row_spec(tq, 256),
                  pl.BlockSpec((BLK, 256), lambda i: (jnp.maximum(i * blocks - 1, 0), 0)), _row_spec(tq, D, 0)],
        out_specs=[_row_spec(tq, D), _row_spec(tq, D), pl.BlockSpec((blocks, 4, STACK, 128), lambda i: (i, 0, 0, 0))],
        out_shape=[jax.ShapeDtypeStruct((t, D), F32), jax.ShapeDtypeStruct((t, D), BF16),
                   jax.ShapeDtypeStruct((t // BLK, 4, STACK, 128), F32)],
        compiler_params=_params("parallel"),
    )(sinks, q, kv, kv, g3)


def _fwd_out_bwd_head(ya, ub, g3, x, target, g_post, wpa, wout, tm):
    t = x.shape[0]

    def body(ya_ref, ub_ref, ga_ref, gb_ref, x_ref, tgt_ref, gp_ref, wpa_ref, wout_ref,
             loss_ref, dout_ref, dya_ref, dub_ref, dgab_ref, dwout_ref, dwpa_ref, dgp_ref):
        @pl.when(pl.program_id(0) == 0)
        def _():
            loss_ref[...] = jnp.zeros_like(loss_ref)
            dwout_ref[...] = jnp.zeros_like(dwout_ref)
            dwpa_ref[...] = jnp.zeros_like(dwpa_ref)
            dgp_ref[...] = jnp.zeros_like(dgp_ref)

        g = gp_ref[...]
        halves = (slice(0, tm // 2), slice(tm // 2, tm))

        def stage1(rows):
            return _dot(ub_ref[rows, :], wpa_ref[...])

        def stage2(rows, yb):
            sa = _sig(ga_ref[rows, :])
            sb = _sig(gb_ref[rows, :])
            mb = (sa * ya_ref[rows, :] + sb * yb).astype(BF16)
            return sa, sb, mb, _dot(mb, wout_ref[...])

        def stage3(rows, y):
            r = lax.rsqrt(jnp.mean(y * y, axis=-1, keepdims=True) + RMS_EPS)
            n = y * r
            err = (x_ref[rows, :] + n * g) - tgt_ref[rows, :]
            sq = jnp.sum(jnp.sum(err * err, axis=0, keepdims=True), axis=1, keepdims=True)
            dout = err * (1.0 / D)
            dout_ref[rows, :] = dout
            dgp = jnp.sum(dout * n, axis=0, keepdims=True)
            dn = dout * g
            dy = (r * (dn - n * jnp.mean(dn * n, axis=-1, keepdims=True))).astype(BF16)
            return sq, dgp, dy, _dot_nt(dy, wout_ref[...])

        def stage4(rows, dm, sa, sb, yb):
            dya_ref[rows, :] = (dm * sa).astype(BF16)
            dyb = (dm * sb).astype(BF16)
            dgab_ref[rows, 0:D] = (dm * ya_ref[rows, :] * (sa * (1.0 - sa))).astype(BF16)
            dgab_ref[rows, D:2 * D] = (dm * yb * (sb * (1.0 - sb))).astype(BF16)
            dub_ref[rows, :] = _dot_nt(dyb, wpa_ref[...])
            return dyb

        yb = [stage1(rows) for rows in halves]
        s2 = [stage2(rows, yb[k]) for k, rows in enumerate(halves)]
        s3 = [stage3(rows, s2[k][3]) for k, rows in enumerate(halves)]
        dyb = [stage4(rows, s3[k][3], s2[k][0], s2[k][1], yb[k]) for k, rows in enumerate(halves)]
        loss_ref[...] += sum(s[0] for s in s3) * (0.5 / D)
        dgp_ref[0:1, :] += sum(s[1] for s in s3)
        dwout_ref[...] += _dot_tn(jnp.concatenate([s[2] for s in s2], axis=0), jnp.concatenate([s[2] for s in s3], axis=0))
        dwpa_ref[...] += _dot_tn(ub_ref[...], jnp.concatenate(dyb, axis=0))

    return pl.pallas_call(
        body, name="fwd_out_bwd_head", grid=(t // tm,),
        in_specs=[_row_spec(tm, D), _row_spec(tm, D), _row_spec(tm, D, 1), _row_spec(tm, D, 2),
                  _row_spec(tm, D), _row_spec(tm, D), pl.BlockSpec((1, D), lambda i: (0, 0)),
                  _whole_vmem(), _whole_vmem()],
        out_specs=[pl.BlockSpec((8, 128), lambda i: (0, 0)), _row_spec(tm, D), _row_spec(tm, D), _row_spec(tm, D),
                   _row_spec(tm, 2 * D), _whole_vmem(), _whole_vmem(), pl.BlockSpec((8, D), lambda i: (0, 0))],
        out_shape=[jax.ShapeDtypeStruct((8, 128), F32), jax.ShapeDtypeStruct((t, D), F32),
                   jax.ShapeDtypeStruct((t, D), BF16), jax.ShapeDtypeStruct((t, D), F32),
                   jax.ShapeDtypeStruct((t, 2 * D), BF16), jax.ShapeDtypeStruct((D, D), F32),
                   jax.ShapeDtypeStruct((D, D), F32), jax.ShapeDtypeStruct((8, D), F32)],
        compiler_params=_params("arbitrary", vmem_limit_bytes=V7X_VMEM_BYTES - (2 << 20)),
    )(ya, ub, g3, g3, x, target, g_post, wpa, wout)


def _bwd_attn(sinks, q, kv, attn, lse, dub, g3, cos_t, sin_t, wt, blocks):
    t = q.shape[0]
    tq = blocks * BLK
    seq_blocks = SEQ_LEN // BLK

    def body(sink_ref, q_ref, kvc_ref, kvp_ref, attn_ref, lse_ref, dub_ref, za_ref, c_ref, s_ref, w_ref,
             dq_ref, dza_ref, dkv_own_ref, dkv_prev_ref, dsink_ref, dh_ref):
        @pl.when(pl.program_id(0) == 0)
        def _():
            dsink_ref[...] = jnp.zeros_like(dsink_ref)

        lo = lax.broadcasted_iota(jnp.int32, (STACK, 128), 1) < 64
        lane8 = lax.broadcasted_iota(jnp.int32, (8, 128), 1)
        lo2 = lax.broadcasted_iota(jnp.int32, (2 * BLK, 128), 1) < 64
        sink_row = lax.broadcasted_iota(jnp.int32, (2 * BLK, 128), 0) == 0
        dsink = jnp.zeros((8, 128), F32)
        for b in range(blocks):
            rows = slice(b * BLK, (b + 1) * BLK)
            valid = _band_mask((pl.program_id(0) * blocks + b) % seq_blocks == 0)
            k_pad = _padded_pair(*_kv_blocks(kvc_ref, kvp_ref, b, 0))
            v_pad = _padded_pair(*_kv_blocks(kvc_ref, kvp_ref, b, 128))
            c = c_ref[rows, :]
            s = s_ref[rows, :]
            dk_acc, dv_acc = [], []
            for g in range(2):
                qg, dog = [], []
                for j in range(4 * g, 4 * g + 4):
                    cols = slice(j * 128, (j + 1) * 128)
                    za = za_ref[rows, cols]
                    sg = _sig(za)
                    dub = dub_ref[rows, cols]
                    dza_ref[rows, cols] = (dub * attn_ref[rows, cols] * (sg * (1.0 + za * (1.0 - sg)))).astype(BF16)
                    dog.append((dub * (za * sg)).astype(BF16))
                    qg.append(q_ref[rows, cols])
                qg = jnp.concatenate(qg, axis=0)
                dog = jnp.concatenate(dog, axis=0)
                dq = jnp.zeros((STACK, 128), F32)
                ds_both, p_both = [], []
                for e in range(2):
                    s_masked = jnp.where(valid, _dot_nt(qg, k_pad[e][g]), _masked_fill(sink_ref, g, e))
                    lse_rows = lse_ref[b, 2 * g + e]
                    p = jnp.exp(s_masked - jnp.concatenate([lse_rows, lse_rows], axis=1))
                    dp = _dot_nt(dog, v_pad[e][g])
                    ds = p * (dp - jnp.sum(p * dp, axis=-1, keepdims=True))
                    for jj in range(4):
                        tot = jnp.sum(ds[jj * BLK:(jj + 1) * BLK, 0:1], axis=0, keepdims=True)
                        dsink = dsink + jnp.where(lane8 == 2 * (4 * g + jj) + e, tot, 0.0)
                    ds = ds.astype(BF16)
                    dq = dq + _dot(ds, k_pad[e][g])
                    ds_both.append(ds)
                    p_both.append(p.astype(BF16))
                zero = jnp.zeros_like(qg)
                q2 = jnp.concatenate([jnp.where(lo, qg, zero), jnp.where(lo, zero, qg)], axis=0)
                do2 = jnp.concatenate([jnp.where(lo, dog, zero), jnp.where(lo, zero, dog)], axis=0)
                dk_acc.append(_dot_tn(q2, jnp.concatenate(ds_both, axis=0)).T)
                dv_acc.append(_dot_tn(do2, jnp.concatenate(p_both, axis=0)).T)
                for jj in range(4):
                    cols = slice((4 * g + jj) * 128, (4 * g + jj + 1) * 128)
                    dqj = dq[jj * BLK:(jj + 1) * BLK, :] * LOGIT_SCALE
                    dq_ref[rows, cols] = (dqj * c - _swap_halves(dqj) * s).astype(BF16)
            for col, acc in ((0, dk_acc), (128, dv_acc)):
                both = jnp.where(lo2, acc[0] + pltpu.roll(acc[0], 64, 1), acc[1] + pltpu.roll(acc[1], 64, 1))
                both = jnp.where(sink_row, 0.0, both)
                dkv_prev_ref[rows, col:col + 128] = both[0:BLK, :]
                dkv_own_ref[rows, col:col + 128] = both[BLK:2 * BLK, :]
        dsink_ref[...] += dsink
        dh_ref[...] = _dot(dq_ref[...], w_ref[ROW_Q:ROW_KV, :]) + _dot(dza_ref[...], w_ref[ROW_ZA:ROW_GA, :])

    tab = pl.BlockSpec((tq, 128), lambda i: (i % (SEQ_LEN // tq), 0))
    return pl.pallas_call(
        body, name="bwd_attn", grid=(t // tq,),
        in_specs=[pl.BlockSpec(memory_space=pltpu.SMEM), _row_spec(tq, D), _row_spec(tq, 256),
                  pl.BlockSpec((BLK, 256), lambda i: (jnp.maximum(i * blocks - 1, 0), 0)),
                  _row_spec(tq, D), pl.BlockSpec((blocks, 4, STACK, 128), lambda i: (i, 0, 0, 0)),
                  _row_spec(tq, D), _row_spec(tq, D, 0), tab, tab, _whole_vmem()],
        out_specs=[_row_spec(tq, D), _row_spec(tq, D), _row_spec(tq, 256), _row_spec(tq, 256),
                   pl.BlockSpec((8, 128), lambda i: (0, 0)), _row_spec(tq, D)],
        out_shape=[jax.ShapeDtypeStruct((t, D), BF16), jax.ShapeDtypeStruct((t, D), BF16),
                   jax.ShapeDtypeStruct((t, 256), F32), jax.ShapeDtypeStruct((t, 256), F32),
                   jax.ShapeDtypeStruct((8, 128), F32), jax.ShapeDtypeStruct((t, D), F32)],
        compiler_params=_params("arbitrary"),
    )(sinks, q, kv, kv, attn, lse, dub, g3, cos_t, sin_t, wt)


def _bwd_kv_finish(dkv_own, dkv_prev, cos_t, sin_t):
    t = dkv_own.shape[0]
    tm = 512
    seq_tiles = SEQ_LEN // tm
    n_blocks = t // BLK

    def body(own_ref, same_ref, nxt_ref, c_ref, s_ref, out_ref):
        keep = jnp.where(pl.program_id(0) % seq_tiles == seq_tiles - 1, 0.0, 1.0)
        shifted = jnp.concatenate([same_ref[BLK:tm, :], nxt_ref[...] * keep], axis=0)
        tot = own_ref[...] + shifted
        dk = tot[:, 0:128]
        out_ref[:, 0:128] = (dk * c_ref[...] - _swap_halves(dk) * s_ref[...]).astype(BF16)
        out_ref[:, 128:256] = tot[:, 128:256].astype(BF16)

    tab = pl.BlockSpec((tm, 128), lambda i: (i % seq_tiles, 0))
    return pl.pallas_call(
        body, name="bwd_kv_finish", grid=(t // tm,),
        in_specs=[_row_spec(tm, 256), _row_spec(tm, 256),
                  pl.BlockSpec((BLK, 256), lambda i: (jnp.minimum((i + 1) * (tm // BLK), n_blocks - 1), 0)), tab, tab],
        out_specs=_row_spec(tm, 256),
        out_shape=jax.ShapeDtypeStruct((t, 256), BF16),
        compiler_params=_params("parallel"),
    )(dkv_own, dkv_prev, dkv_prev, cos_t, sin_t)


STAGE_ROWS = 256


def _bwd_conv(dya, a4, h, wconv8, wpc, tm, parts):
    t = a4.shape[0]
    n_t = t // tm
    sub = tm // parts
    seq_tiles = SEQ_LEN // tm

    def body(dya_ref, xc_ref, bg_ref, cg_ref, zc_ref, xcp_ref, cgp_ref, w_ref, wpc_ref, h_ref,
             da4_ref, dwpc_ref, dwc_ref, o32_ref, o16_ref, acc_ref, stage_ref, later_ref, sems):
        step = pl.program_id(0)
        tile = n_t - 1 - step

        @pl.when(step == 0)
        def _():
            dwpc_ref[...] = jnp.zeros_like(dwpc_ref)
            dwc_ref[...] = jnp.zeros_like(dwc_ref)
            acc_ref[...] = jnp.zeros_like(acc_ref)

        keep_prev = jnp.where(tile % seq_tiles == 0, 0.0, 1.0)
        ends_sequence = tile % seq_tiles == seq_tiles - 1

        def part(p, later):
            r0 = p * sub
            here = slice(r0, r0 + sub)
            if p == 0:
                u_prev = cgp_ref[14:16, :].astype(F32) * xcp_ref[14:16, :].astype(F32) * keep_prev
            else:
                u_prev = cg_ref[r0 - 2:r0, :].astype(F32) * xc_ref[r0 - 2:r0, :].astype(F32)
            xc = xc_ref[here, :].astype(F32)
            bg = bg_ref[here, :].astype(F32)
            cg = cg_ref[here, :].astype(F32)
            zc = zc_ref[here, :].astype(F32)
            u, u_m1, u_m2, yconv, sg, sz, co = _conv_forward(xc, bg, cg, zc, u_prev[0:1, :], u_prev[1:2, :], w_ref)
            ua = (sz * co).astype(BF16)
            dua = _dot_nt(dya_ref[here, :], wpc_ref[...])
            da4_ref[here, 3 * D:4 * D] = (dua * co * (sg * (1.0 + zc * (1.0 - sg)))).astype(BF16)
            dco = dua * sz
            da4_ref[here, D:2 * D] = (dco * yconv).astype(BF16)
            dyc = dco * bg
            dwc = jnp.concatenate([jnp.sum(dyc * s, axis=0, keepdims=True) for s in (u_m2, u_m1, u)], axis=0)
            rows = lax.broadcasted_iota(jnp.int32, xc.shape, 0)
            n0 = later[0:1, :]
            n1 = later[1:2, :]
            dyc_p1 = jnp.where(rows == sub - 1, n0, pltpu.roll(dyc, sub - 1, 0))
            dyc_p2 = jnp.where(rows == sub - 2, n0, jnp.where(rows == sub - 1, n1, pltpu.roll(dyc, sub - 2, 0)))
            du = w_ref[2:3, :] * dyc + w_ref[1:2, :] * dyc_p1 + w_ref[0:1, :] * dyc_p2
            da4_ref[here, 0:D] = (du * cg).astype(BF16)
            da4_ref[here, 2 * D:3 * D] = (du * xc).astype(BF16)
            return ua, dwc, dyc[0:8, :]

        later = jnp.where(ends_sequence, 0.0, later_ref[...])
        uas, dwc = [], jnp.zeros((3, D), F32)
        for p in reversed(range(parts)):
            ua, dwc_p, later = part(p, later)
            uas.insert(0, ua)
            dwc = dwc + dwc_p
        later_ref[...] = later
        dwpc_ref[...] += _dot_tn(jnp.concatenate(uas, axis=0), dya_ref[...])
        dwc_ref[0:3, :] += dwc
        for j in range(4):
            acc_ref[j * D:(j + 1) * D, :] += _dot_tn(da4_ref[:, j * D:(j + 1) * D], h_ref[...])

        @pl.when(step == n_t - 1)
        def _():
            c32 = pltpu.make_async_copy(acc_ref, o32_ref.at[pl.ds(0, 4 * D)], sems.at[0])
            c32.start()
            for j in range(4 * D // STAGE_ROWS):
                rows = pl.ds(j * STAGE_ROWS, STAGE_ROWS)
                stage_ref[...] = acc_ref[rows, :].astype(BF16)
                c16 = pltpu.make_async_copy(stage_ref, o16_ref.at[rows], sems.at[1])
                c16.start()
                c16.wait()
            c32.wait()

    def rows_of_tile(width, col=0):
        return pl.BlockSpec((tm, width), lambda s: (n_t - 1 - s, col))

    def prev(col):
        return pl.BlockSpec((16, D), lambda s: (jnp.maximum((n_t - 1 - s) * (tm // 16) - 1, 0), col))

    hbm = pl.BlockSpec(memory_space=pl.ANY)
    out = pl.pallas_call(
        body, name="bwd_conv", grid=(n_t,),
        in_specs=[rows_of_tile(D), rows_of_tile(D, 0), rows_of_tile(D, 1), rows_of_tile(D, 2), rows_of_tile(D, 3),
                  prev(0), prev(2), pl.BlockSpec((8, D), lambda s: (0, 0)), _whole_vmem(), rows_of_tile(D)],
        out_specs=[rows_of_tile(4 * D), _whole_vmem(), pl.BlockSpec((8, D), lambda s: (0, 0)), hbm, hbm],
        out_shape=[jax.ShapeDtypeStruct((t, 4 * D), BF16), jax.ShapeDtypeStruct((D, D), F32),
                   jax.ShapeDtypeStruct((8, D), F32), jax.ShapeDtypeStruct((D_IN, D), F32),
                   jax.ShapeDtypeStruct((D_IN, D), BF16)],
        scratch_shapes=[pltpu.VMEM((4 * D, D), F32), pltpu.VMEM((STAGE_ROWS, D), BF16), pltpu.VMEM((8, D), F32),
                        pltpu.SemaphoreType.DMA((2,))],
        compiler_params=pltpu.CompilerParams(dimension_semantics=("arbitrary",), vmem_limit_bytes=V7X_VMEM_BYTES - (2 << 20)),
    )(dya, a4, a4, a4, a4, a4, a4, wconv8, wpc, h)
    return out[0], out[1], out[2], (out[3], out[4])


def _bwd_dh(da4, dh_part, dkv, dgab, wt, x, g_pre, dout, tm):
    t = x.shape[0]

    def body(da4_ref, dhp_ref, dkv_ref, dgab_ref, w_ref, x_ref, g_ref, dout_ref, gx_ref, dg_ref):
        @pl.when(pl.program_id(0) == 0)
        def _():
            dg_ref[...] = jnp.zeros_like(dg_ref)

        dh = dhp_ref[...] + _dot(da4_ref[...], w_ref[0:ROW_Q, :])
        dh += _dot(dkv_ref[...], w_ref[ROW_KV:ROW_ZA, :])
        dh += _dot(dgab_ref[...], w_ref[ROW_GA:D_IN, :])
        xf = x_ref[...]
        r = lax.rsqrt(jnp.mean(xf * xf, axis=-1, keepdims=True) + RMS_EPS)
        xn = xf * r
        dg_ref[0:1, :] += jnp.sum(dh * xn, axis=0, keepdims=True)
        dxn = dh * g_ref[...]
        gx_ref[...] = dout_ref[...] + r * (dxn - xn * jnp.mean(dxn * xn, axis=-1, keepdims=True))

    return pl.pallas_call(
        body, name="bwd_dh", grid=(t // tm,),
        in_specs=[_row_spec(tm, 4 * D), _row_spec(tm, D), _row_spec(tm, 256), _row_spec(tm, 2 * D),
                  _whole_vmem(), _row_spec(tm, D), pl.BlockSpec((1, D), lambda i: (0, 0)), _row_spec(tm, D)],
        out_specs=[_row_spec(tm, D), pl.BlockSpec((8, D), lambda i: (0, 0))],
        out_shape=[jax.ShapeDtypeStruct((t, D), F32), jax.ShapeDtypeStruct((8, D), F32)],
        compiler_params=_params("arbitrary"),
    )(da4, dh_part, dkv, dgab, wt, x, g_pre, dout)


def _bwd_dw_in(h, piece, row0, nb, tm, name, prev):
    t, n = piece.shape
    n_t = t // tm

    def body(*refs):
        h_ref, p_ref = refs[0], refs[1]
        o32_ref, o16_ref, acc_ref, acc16_ref, sems = refs[-5:]
        j, i = pl.program_id(0), pl.program_id(1)

        @pl.when(i == 0)
        def _():
            acc_ref[...] = jnp.zeros_like(acc_ref)

        acc_ref[...] += _dot_tn(p_ref[...], h_ref[...])

        @pl.when(i == n_t - 1)
        def _():
            acc16_ref[...] = acc_ref[...].astype(BF16)
            rows = pl.ds(pl.multiple_of(row0 + j * nb, 16), nb)
            c32 = pltpu.make_async_copy(acc_ref, o32_ref.at[rows], sems.at[0])
            c16 = pltpu.make_async_copy(acc16_ref, o16_ref.at[rows], sems.at[1])
            c32.start()
            c16.start()
            c32.wait()
            c16.wait()

    hbm = pl.BlockSpec(memory_space=pl.ANY)
    carried = [] if prev is None else list(prev)
    return pl.pallas_call(
        body, name=name, grid=(n // nb, n_t),
        in_specs=[pl.BlockSpec((tm, D), lambda j, i: (i, 0)), pl.BlockSpec((tm, nb), lambda j, i: (i, j))]
        + [hbm] * len(carried),
        out_specs=[hbm, hbm],
        out_shape=[jax.ShapeDtypeStruct((D_IN, D), F32), jax.ShapeDtypeStruct((D_IN, D), BF16)],
        scratch_shapes=[pltpu.VMEM((nb, D), F32), pltpu.VMEM((nb, D), BF16), pltpu.SemaphoreType.DMA((2,))],
        input_output_aliases={2: 0, 3: 1} if carried else {},
        compiler_params=_params("arbitrary", "arbitrary"),
    )(h, piece, *carried)


def _place():
    x, y, c = lax.axis_index("x"), lax.axis_index("y"), lax.axis_index("c")
    return x, y, c, 4 * x + 2 * y + c


def _peer(x, y, c, k):
    return (1 - x if k & 4 else x, 1 - y if k & 2 else y, 1 - c if k & 1 else c)


ICI_MASKS = (4, 2, 6)


def _all_gather(shards):
    n = len(shards)

    def body(*refs):
        src, dst = refs[:n], refs[n:2 * n]
        send_sems, recv_sems, local_sems = refs[2 * n:]
        x, y, c, me = _place()
        sibling = _peer(x, y, c, 1)

        def copy(a, s, block, to, own=False):
            return pltpu.make_async_remote_copy(
                src_ref=src[a] if own else dst[a].at[block], dst_ref=dst[a].at[block],
                send_sem=send_sems.at[a * 7 + s], recv_sem=recv_sems.at[a * 7 + s], device_id=to, device_id_type=MESH_ID)

        local = [pltpu.make_async_copy(src[a], dst[a].at[me], local_sems.at[a]) for a in range(n)]
        for cp in local:
            cp.start()
        started = [copy(a, 0, me, sibling, own=True) for a in range(n)]
        started += [copy(a, 1 + j, me, _peer(x, y, c, k), own=True) for j, k in enumerate(ICI_MASKS) for a in range(n)]
        for cp in started:
            cp.start()
        for j, k in enumerate(ICI_MASKS):
            for a in range(n):
                copy(a, 1 + j, me ^ k, sibling).wait_recv()
                fwd = copy(a, 4 + j, me ^ k, sibling)
                fwd.start()
                started.append(fwd)
        for a in range(n):
            copy(a, 0, me ^ 1, sibling).wait_recv()
        for j, k in enumerate(ICI_MASKS):
            for a in range(n):
                copy(a, 4 + j, me ^ 1 ^ k, sibling).wait_recv()
        for cp in started:
            cp.wait_send()
        for cp in local:
            cp.wait()

    hbm = pl.BlockSpec(memory_space=pl.ANY)
    return pl.pallas_call(
        body, name="all_gather_weights",
        in_specs=[hbm] * n, out_specs=[hbm] * n,
        out_shape=[jax.ShapeDtypeStruct((N_DEV,) + s.shape, s.dtype) for s in shards],
        scratch_shapes=[pltpu.SemaphoreType.DMA((7 * n,)), pltpu.SemaphoreType.DMA((7 * n,)),
                        pltpu.SemaphoreType.DMA((n,))],
    )(*shards)


def _direct_copies(src, land, send_sems, recv_sems):
    x, y, c, me = _place()
    return [pltpu.make_async_remote_copy(
        src_ref=src[a], dst_ref=land[a].at[me], send_sem=send_sems.at[a * 7 + k - 1],
        recv_sem=recv_sems.at[a * 7 + k - 1], device_id=_peer(x, y, c, k), device_id_type=MESH_ID)
        for k in range(1, N_DEV) for a in range(len(src))]


def _gather_start(shards, name):
    n = len(shards)

    def body(*refs):
        src, land = refs[:n], refs[n:2 * n]
        send_sems, recv_sems = refs[2 * n], refs[2 * n + 1]
        token_ref = refs[-1]
        for cp in _direct_copies(src, land, send_sems, recv_sems):
            cp.start()
        token_ref[...] = jnp.zeros_like(token_ref)

    hbm = pl.BlockSpec(memory_space=pltpu.HBM)
    sem = pl.BlockSpec(memory_space=pltpu.SEMAPHORE)
    lands = [lax.empty((N_DEV,) + s.shape, s.dtype) for s in shards]
    out = pl.pallas_call(
        body, name=name + "_start",
        out_shape=(pltpu.SemaphoreType.DMA((7 * n,)), pltpu.SemaphoreType.DMA((7 * n,)),
                   *[pltpu.HBM(s.shape, s.dtype) for s in shards], *[pltpu.HBM(s.shape, s.dtype) for s in lands],
                   jax.ShapeDtypeStruct((8, 128), F32)),
        in_specs=[hbm] * (2 * n), out_specs=(sem, sem, *[hbm] * (2 * n), _whole_vmem()),
        input_output_aliases={a: 2 + a for a in range(2 * n)},
        compiler_params=pltpu.CompilerParams(has_side_effects=pltpu.SideEffectType.DATAFLOW_SIDE_EFFECTING),
    )(*[pltpu.with_memory_space_constraint(s, pltpu.HBM) for s in list(shards) + lands])
    return out[0], out[1], out[2:2 + n], out[2 + n:2 + 2 * n], out[-1]


def _gather_wait(send_sems, recv_sems, flying, lands, after, name):
    n = len(flying)

    def body(*refs):
        src, land = refs[:n], refs[n:2 * n]
        for cp in _direct_copies(src, land, refs[2 * n], refs[2 * n + 1]):
            cp.wait_send()
            cp.wait_recv()

    hbm = pl.BlockSpec(memory_space=pltpu.HBM)
    sem = pl.BlockSpec(memory_space=pltpu.SEMAPHORE)
    out = pl.pallas_call(
        body, name=name + "_wait",
        out_shape=tuple(pltpu.HBM(s.shape, s.dtype) for s in list(flying) + list(lands)),
        in_specs=[hbm] * (2 * n) + [sem, sem, pl.BlockSpec(memory_space=pl.ANY)], out_specs=tuple([hbm] * (2 * n)),
        input_output_aliases={a: a for a in range(2 * n)},
        compiler_params=pltpu.CompilerParams(has_side_effects=pltpu.SideEffectType.DATAFLOW_SIDE_EFFECTING),
    )(*flying, *lands, send_sems, recv_sems, after)
    return out[n:]


def _exchange_sibling(by_dest):
    n = len(by_dest)

    def body(*refs):
        src, dst = refs[:n], refs[n:2 * n]
        send_sems, recv_sems = refs[2 * n:]
        x, y, c, _ = _place()
        sibling = _peer(x, y, c, 1)
        copies = [pltpu.make_async_remote_copy(
            src_ref=src[a].at[2 * p + (1 - c)], dst_ref=dst[a].at[p], send_sem=send_sems.at[a * 4 + p],
            recv_sem=recv_sems.at[a * 4 + p], device_id=sibling, device_id_type=MESH_ID)
            for a in range(n) for p in range(4)]
        for cp in copies:
            cp.start()
        for cp in copies:
            cp.wait_recv()
        for cp in copies:
            cp.wait_send()

    hbm = pl.BlockSpec(memory_space=pl.ANY)
    return pl.pallas_call(
        body, name="exchange_sibling", in_specs=[hbm] * n, out_specs=[hbm] * n,
        out_shape=[jax.ShapeDtypeStruct((4,) + s.shape[1:], s.dtype) for s in by_dest],
        scratch_shapes=[pltpu.SemaphoreType.DMA((4 * n,)), pltpu.SemaphoreType.DMA((4 * n,))],
    )(*by_dest)


def _chip_copies(src, land, send_sems, recv_sems):
    x, y, c, _ = _place()
    chip = 2 * x + y
    return [pltpu.make_async_remote_copy(
        src_ref=src[a].at[chip ^ (k >> 1)], dst_ref=land[a].at[j], send_sem=send_sems.at[a * 3 + j],
        recv_sem=recv_sems.at[a * 3 + j], device_id=_peer(x, y, c, k), device_id_type=MESH_ID)
        for j, k in enumerate(ICI_MASKS) for a in range(len(src))]


def _exchange_chips_start(by_chip):
    n = len(by_chip)

    def body(*refs):
        src, land = refs[:n], refs[n:2 * n]
        send_sems, recv_sems = refs[2 * n], refs[2 * n + 1]
        token_ref = refs[-1]
        for cp in _chip_copies(src, land, send_sems, recv_sems):
            cp.start()
        token_ref[...] = jnp.zeros_like(token_ref)

    hbm = pl.BlockSpec(memory_space=pltpu.HBM)
    sem = pl.BlockSpec(memory_space=pltpu.SEMAPHORE)
    lands = [lax.empty((3,) + s.shape[1:], s.dtype) for s in by_chip]
    out = pl.pallas_call(
        body, name="exchange_chips_start",
        out_shape=(pltpu.SemaphoreType.DMA((3 * n,)), pltpu.SemaphoreType.DMA((3 * n,)),
                   *[pltpu.HBM(s.shape, s.dtype) for s in by_chip], *[pltpu.HBM(s.shape, s.dtype) for s in lands],
                   jax.ShapeDtypeStruct((8, 128), F32)),
        in_specs=[hbm] * (2 * n), out_specs=(sem, sem, *[hbm] * (2 * n), _whole_vmem()),
        input_output_aliases={a: 2 + a for a in range(2 * n)},
        compiler_params=pltpu.CompilerParams(has_side_effects=pltpu.SideEffectType.DATAFLOW_SIDE_EFFECTING),
    )(*[pltpu.with_memory_space_constraint(s, pltpu.HBM) for s in list(by_chip) + lands])
    return out[0], out[1], out[2:2 + n], out[2 + n:2 + 2 * n], out[-1]


def _exchange_chips_wait(send_sems, recv_sems, flying, lands, after):
    n = len(flying)

    def body(*refs):
        src, land = refs[:n], refs[n:2 * n]
        send_sems_ref, recv_sems_ref = refs[2 * n], refs[2 * n + 1]
        for cp in _chip_copies(src, land, send_sems_ref, recv_sems_ref):
            cp.wait_send()
            cp.wait_recv()

    hbm = pl.BlockSpec(memory_space=pltpu.HBM)
    sem = pl.BlockSpec(memory_space=pltpu.SEMAPHORE)
    out = pl.pallas_call(
        body, name="exchange_chips_wait",
        out_shape=tuple(pltpu.HBM(s.shape, s.dtype) for s in list(flying) + list(lands)),
        in_specs=[hbm] * (2 * n) + [sem, sem, pl.BlockSpec(memory_space=pl.ANY)], out_specs=tuple([hbm] * (2 * n)),
        input_output_aliases={a: a for a in range(2 * n)},
        compiler_params=pltpu.CompilerParams(has_side_effects=pltpu.SideEffectType.DATAFLOW_SIDE_EFFECTING),
    )(*flying, *lands, send_sems, recv_sems, after)
    return out[n:]


def _adamw_math(w, g, m, v):
    m = ADAM_B1 * m + (1.0 - ADAM_B1) * g
    v = ADAM_B2 * v + (1.0 - ADAM_B2) * (g * g)
    m_hat = m / (1.0 - ADAM_B1 ** ADAM_STEP)
    v_hat = v / (1.0 - ADAM_B2 ** ADAM_STEP)
    return -ADAM_LR * (m_hat / (jnp.sqrt(v_hat) + ADAM_EPS) + ADAM_WD * w), m, v


def _pair_sum(owns, recvs, c_arr, tr, name):
    n = len(owns)
    _, rows, cols = owns[0].shape

    def body(c_ref, *refs):
        for a in range(n):
            s = refs[a][...] + refs[n + a][...].astype(F32)
            refs[2 * n + a][...] = s
            refs[3 * n + a][...] = s.astype(BF16)

    by_chip = pl.BlockSpec((None, tr, cols), lambda p, i, c_ref: (p, i, 0))
    mine = pl.BlockSpec((None, tr, cols), lambda p, i, c_ref: (2 * p + c_ref[0], i, 0))
    out = pl.pallas_call(
        body, name=name,
        grid_spec=pltpu.PrefetchScalarGridSpec(
            num_scalar_prefetch=1, grid=(4, rows // tr), in_specs=[mine] * n + [by_chip] * n, out_specs=[by_chip] * (2 * n)),
        out_shape=[jax.ShapeDtypeStruct((4, rows, cols), F32)] * n + [jax.ShapeDtypeStruct((4, rows, cols), BF16)] * n,
        compiler_params=_params("parallel", "parallel"),
    )(c_arr, *owns, *recvs)
    return out[:n], out[n:]


def _chip_sum(pairs, recvs, chip_arr, tr, name, adam=None):
    n = len(pairs)
    _, rows, cols = pairs[0].shape
    n_state = 0 if adam is None else 3 * n

    def body(chip_ref, *refs):
        outs = refs[2 * n + n_state:]
        for a in range(n):
            g = refs[a][...]
            for j in range(3):
                g = g + refs[n + a][j].astype(F32)
            outs[a][...] = g
            if adam is not None:
                w_ref, m_ref, v_ref = (refs[2 * n + s * n + a] for s in range(3))
                outs[n + a][...], outs[2 * n + a][...], outs[3 * n + a][...] = _adamw_math(w_ref[...], g, m_ref[...], v_ref[...])

    blk = pl.BlockSpec((tr, cols), lambda i, chip_ref: (i, 0))
    n_out = n if adam is None else 4 * n
    out = pl.pallas_call(
        body, name=name,
        grid_spec=pltpu.PrefetchScalarGridSpec(
            num_scalar_prefetch=1, grid=(rows // tr,),
            in_specs=[pl.BlockSpec((None, tr, cols), lambda i, chip_ref: (chip_ref[0], i, 0))] * n
            + [pl.BlockSpec((3, tr, cols), lambda i, chip_ref: (0, i, 0))] * n + [blk] * n_state,
            out_specs=[blk] * n_out),
        out_shape=[jax.ShapeDtypeStruct((rows, cols), F32)] * n_out,
        compiler_params=_params("parallel"),
    )(chip_arr, *pairs, *recvs, *([] if adam is None else [t for group in adam for t in group]))
    return out if adam is None else (out[:n], out[n:2 * n], out[2 * n:3 * n], out[3 * n:])


def _adamw(ws, gs, ms, vs, name):
    n = len(ws)

    def body(*refs):
        for a in range(n):
            w_ref, g_ref, m_ref, v_ref = (refs[s * n + a] for s in range(4))
            refs[4 * n + a][...], refs[5 * n + a][...], refs[6 * n + a][...] = _adamw_math(
                w_ref[...], g_ref[...], m_ref[...], v_ref[...])

    out = pl.pallas_call(body, name=name, out_shape=[jax.ShapeDtypeStruct(w.shape, F32) for w in ws] * 3)(
        *ws, *gs, *ms, *vs)
    return out[:n], out[n:2 * n], out[2 * n:]


def _sum_small(small_all):
    def body(s_ref, o_ref):
        g = s_ref[0]
        for d in range(1, N_DEV):
            g = g + s_ref[d]
        o_ref[...] = g

    return pl.pallas_call(body, name="sum_small", out_shape=jax.ShapeDtypeStruct(small_all.shape[1:], F32))(small_all)


def _rope_tables():
    inv_freq = ROPE_THETA ** (-jnp.arange(0, HEAD_DIM, 2, dtype=F32) / HEAD_DIM)
    ang = jnp.arange(SEQ_LEN).astype(F32)[:, None] * inv_freq[None, :]
    cos, sin = jnp.cos(ang), jnp.sin(ang)
    return jnp.tile(cos, (1, 4)), jnp.tile(jnp.concatenate([-sin, sin], axis=1), (1, 2))


def _local_step(x, target, g_pre, g_post, sinks, wt, wconv, squares, start_exchange=None):
    cos_t, sin_t = _rope_tables()
    wconv8 = jnp.pad(wconv, ((0, 5), (0, 0)))
    h, q, kv, g3 = _fwd_in_attn(x, g_pre, wt, cos_t, sin_t, 512)
    wpc, wpa, wout = squares(kv)
    a4, ya = _fwd_in_conv(h, wt, wconv8, wpc, 512)
    attn, ub, lse = _fwd_attn(sinks, q, kv, g3, 4)
    loss8, dout, dya, dub, dgab, dwout, dwpa, dgpost8 = _fwd_out_bwd_head(ya, ub, g3, x, target, g_post, wpa, wout, 512)
    dq, dza, dkv_own, dkv_prev, dsink8, dh_part = _bwd_attn(sinks, q, kv, attn, lse, dub, g3, cos_t, sin_t, wt, 4)
    dkv = _bwd_kv_finish(dkv_own, dkv_prev, cos_t, sin_t)
    da4, dwpc, dwconv8, dwt = _bwd_conv(dya, a4, h, wconv8, wpc, 512, 2)
    dwt = _bwd_dw_in(h, dq, ROW_Q, 1024, 1024, "bwd_dw_in_q", dwt)
    dwt = _bwd_dw_in(h, dkv, ROW_KV, 256, 1024, "bwd_dw_in_kv", dwt)
    dwt = _bwd_dw_in(h, dza, ROW_ZA, 1024, 1024, "bwd_dw_in_za", dwt)
    dwt32, dwt16 = _bwd_dw_in(h, dgab, ROW_GA, 1024, 1024, "bwd_dw_in_gates", dwt)
    token, pending = (None, None) if start_exchange is None else start_exchange(dwt32, dwt16, dwpc, dwpa, dwout)
    g_pre_after = g_pre if token is None else g_pre + token[0:1, 0:1]
    grad_x, dgpre8 = _bwd_dh(da4, dh_part, dkv, dgab, wt, x, g_pre_after, dout, 512)
    small = jnp.concatenate([dgpre8, dgpost8, jnp.pad(dsink8, ((0, 0), (0, D - 128))), dwconv8,
                             jnp.pad(loss8, ((0, 0), (0, D - 128)))], axis=0)
    return loss8[0, 0], grad_x, dwt32, dwt16, dwpc, dwpa, dwout, small, pending


def kernel(x, g_pre, g_post, w_in, w_conv, sinks, w_proj_conv, w_proj_attn, w_out, loss_target, m_g_pre, m_g_post, m_w_in, m_w_conv, m_sinks, m_w_proj_conv, m_w_proj_attn, m_w_out, v_g_pre, v_g_post, v_w_in, v_w_conv, v_sinks, v_w_proj_conv, v_w_proj_attn, v_w_out):
    batch = x.shape[0]
    mx, my, mc, me = _place()
    c_arr = jnp.reshape(mc, (1,)).astype(jnp.int32)
    chip_arr = jnp.reshape(2 * mx + my, (1,)).astype(jnp.int32)

    g_wt, g_conv = _all_gather([w_in[0].T.astype(BF16), jnp.pad(w_conv[0], ((0, 5), (0, 0)))])
    wt = g_wt.reshape(D_IN, D)
    wconv = g_conv[:, 0:3, :].transpose(1, 0, 2).reshape(3, D)
    sq_mine = [w.astype(BF16) for w in (w_proj_conv[0], w_proj_attn[0], w_out[0])]
    wt, sq_mine = lax.optimization_barrier((wt, sq_mine))
    sq_send, sq_recv, sq_flying, sq_lands, sq_token = _gather_start(sq_mine, "gather_squares")

    def squares(after):
        got = _gather_wait(sq_send, sq_recv, sq_flying, sq_lands, after, "gather_squares")
        return [lax.dynamic_update_index_in_dim(full, mine, me, 0).reshape(D, D) for full, mine in zip(got, sq_mine)]

    def start_exchange(dwt32, dwt16, dwpc, dwpa, dwout):
        own_sq = [g.reshape(N_DEV, SHARD_SQ, D) for g in (dwpc, dwpa, dwout)]
        own_in = dwt32.reshape(N_DEV, SHARD_IN, D)
        from_sibling = _exchange_sibling([dwt16.reshape(N_DEV, SHARD_IN, D)] + [g.astype(BF16) for g in own_sq])
        in32, in16 = _pair_sum([own_in], from_sibling[:1], c_arr, SHARD_IN // 2, "pair_sum_w_in")
        sq32, sq16 = _pair_sum(own_sq, from_sibling[1:], c_arr, SHARD_SQ, "pair_sum_squares")
        send_sems, recv_sems, flying, lands, token = _exchange_chips_start(list(in16) + list(sq16))
        return token, (send_sems, recv_sems, flying, lands, in32, sq32)

    _, grad_x, _, _, _, _, _, small, pending = _local_step(
        x.reshape(batch * SEQ_LEN, D), loss_target.reshape(batch * SEQ_LEN, D), g_pre + sq_token[0:1, 0:1], g_post,
        sinks, wt, wconv, squares, start_exchange)
    sm_send, sm_recv, sm_flying, sm_lands, sm_token = _gather_start([small], "gather_small")
    send_sems, recv_sems, flying, lands, in32, sq32 = pending
    from_chips = _exchange_chips_wait(send_sems, recv_sems, flying, lands, sm_token)

    o_in = [o[0].T for o in _chip_sum(
        in32, from_chips[:1], chip_arr, SHARD_IN // 3, "chip_sum_adamw_w_in",
        adam=([w_in[0].T], [m_w_in[0].T], [v_w_in[0].T]))]
    g_in_mine, o_in = o_in[0], o_in[1:]
    g_sq, d_sq, m_sq, v_sq = _chip_sum(
        sq32, from_chips[1:], chip_arr, SHARD_SQ, "chip_sum_adamw_squares",
        adam=([w_proj_conv[0], w_proj_attn[0], w_out[0]], [m_w_proj_conv[0], m_w_proj_attn[0], m_w_out[0]],
              [v_w_proj_conv[0], v_w_proj_attn[0], v_w_out[0]]))
    both_done, g_in_mine = lax.optimization_barrier((d_sq[0], g_in_mine))
    (small_all,) = _gather_wait(sm_send, sm_recv, sm_flying, sm_lands, both_done, "gather_small")
    gs = _sum_small(lax.dynamic_update_index_in_dim(small_all, small, me, 0))
    g_g_pre, g_g_post, g_sinks, loss = gs[0:1], gs[8:9], gs[16:17, 0:N_HEADS], gs[32, 0]
    g_conv_mine = lax.dynamic_slice_in_dim(gs[24:27], me * SHARD_SQ, SHARD_SQ, axis=1)
    o_small = _adamw([g_pre, g_post, sinks, w_conv[0]], [g_g_pre, g_g_post, g_sinks, g_conv_mine],
                     [m_g_pre, m_g_post, m_sinks, m_w_conv[0]], [v_g_pre, v_g_post, v_sinks, v_w_conv[0]], "adamw_small")

    grads = [g_g_pre, g_g_post, g_in_mine[None], g_conv_mine[None], g_sinks] + [g[None] for g in g_sq]
    rest = []
    for idx, sq in enumerate((d_sq, m_sq, v_sq)):
        gp, gq, sk, cv = o_small[idx]
        rest += [gp, gq, o_in[idx][None], cv[None], sk] + [s[None] for s in sq]
    return (loss, grad_x.reshape(batch, SEQ_LEN, D), *grads, *rest)
```

```python
import jax
import jax.numpy as jnp
from jax import lax
from jax.experimental import pallas as pl
from jax.experimental.pallas import tpu as pltpu

D = 1024
N_HEADS = 16
HEAD_DIM = 64
LOGIT_SCALE = HEAD_DIM ** -0.5
BLK = 128
SEQ_LEN = 2048
D_IN = 8448
ROW_Q, ROW_KV, ROW_ZA, ROW_GA = 4 * D, 5 * D, 5 * D + 256, 6 * D + 256
SHARD_IN = D_IN // 8
SHARD_SQ = D // 8
N_DEV = 8
V7X_VMEM_BYTES = 64 << 20
ROPE_THETA = 10000.0
RMS_EPS = 1e-6
NEG = -1e30
ADAM_LR, ADAM_B1, ADAM_B2, ADAM_EPS, ADAM_WD, ADAM_STEP = 0.001, 0.9, 0.999, 1e-08, 0.01, 10

F32 = jnp.float32
BF16 = jnp.bfloat16
MESH_ID = pl.DeviceIdType.MESH


def _dot(a, b):
    return jnp.dot(a, b, preferred_element_type=F32)


def _dot_nt(a, b):
    return lax.dot_general(a, b, (((1,), (1,)), ((), ())), preferred_element_type=F32)


def _dot_tn(a, b):
    return lax.dot_general(a, b, (((0,), (0,)), ((), ())), preferred_element_type=F32)


def _sig(z):
    return 1.0 / (1.0 + jnp.exp(-z))


def _swap_halves(z):
    lane = lax.broadcasted_iota(jnp.int32, z.shape, 1)
    return jnp.where((lane & 63) < 32, pltpu.roll(z, 96, 1), pltpu.roll(z, 32, 1))


def _row_spec(tm, width, col=0):
    return pl.BlockSpec((tm, width), lambda i: (i, col))


def _whole_vmem():
    return pl.BlockSpec(memory_space=pltpu.VMEM)


def _params(*sem, vmem_limit_bytes=None):
    return pltpu.CompilerParams(dimension_semantics=sem, vmem_limit_bytes=vmem_limit_bytes)


def _fwd_in_attn(x, g_pre, wt, cos_t, sin_t, tm):
    t = x.shape[0]
    seq_tiles = SEQ_LEN // tm

    def body(x_ref, g_ref, w_ref, c_ref, s_ref, h_ref, q_ref, kv_ref, g3_ref):
        xf = x_ref[...]
        r = lax.rsqrt(jnp.mean(xf * xf, axis=-1, keepdims=True) + RMS_EPS)
        hh = ((xf * r) * g_ref[...]).astype(BF16)
        h_ref[...] = hh
        c = c_ref[...]
        s = s_ref[...]

        def rope(z):
            return z * c + _swap_halves(z) * s

        q = _dot_nt(hh, w_ref[ROW_Q:ROW_Q + D, :])
        for j in range(D // 128):
            q_ref[:, j * 128:(j + 1) * 128] = (rope(q[:, j * 128:(j + 1) * 128]) * LOGIT_SCALE).astype(BF16)
        kv = _dot_nt(hh, w_ref[ROW_KV:ROW_KV + 256, :])
        kv_ref[:, 0:128] = rope(kv[:, 0:128]).astype(BF16)
        kv_ref[:, 128:256] = kv[:, 128:256].astype(BF16)
        for j in range(3):
            g3_ref[:, j * D:(j + 1) * D] = _dot_nt(hh, w_ref[ROW_ZA + j * D:ROW_ZA + (j + 1) * D, :])

    tab = pl.BlockSpec((tm, 128), lambda i: (i % seq_tiles, 0))
    return pl.pallas_call(
        body, name="fwd_in_attn", grid=(t // tm,),
        in_specs=[_row_spec(tm, D), pl.BlockSpec((1, D), lambda i: (0, 0)), _whole_vmem(), tab, tab],
        out_specs=[_row_spec(tm, D), _row_spec(tm, D), _row_spec(tm, 256), _row_spec(tm, 3 * D)],
        out_shape=[jax.ShapeDtypeStruct((t, D), BF16), jax.ShapeDtypeStruct((t, D), BF16),
                   jax.ShapeDtypeStruct((t, 256), BF16), jax.ShapeDtypeStruct((t, 3 * D), F32)],
        compiler_params=_params("parallel"),
    )(x, g_pre, wt, cos_t, sin_t)


def _conv_forward(xc, bg, cg, zc, up6, up7, w_ref):
    rows = lax.broadcasted_iota(jnp.int32, xc.shape, 0)
    u = cg * xc
    u_m1 = jnp.where(rows == 0, up7, pltpu.roll(u, 1, 0))
    u_m2 = jnp.where(rows == 0, up6, jnp.where(rows == 1, up7, pltpu.roll(u, 2, 0)))
    yconv = w_ref[0:1, :] * u_m2 + w_ref[1:2, :] * u_m1 + w_ref[2:3, :] * u
    sg = _sig(zc)
    sz = zc * sg
    co = bg * yconv
    return u, u_m1, u_m2, yconv, sg, sz, co


def _fwd_in_conv(h, wt, wconv8, wpc, tm):
    t = h.shape[0]
    seq_tiles = SEQ_LEN // tm

    def body(h_ref, w_ref, wc_ref, wpc_ref, a4_ref, ya_ref, last_u_ref):
        hh = h_ref[...]
        xc, bg, cg, zc = (_dot_nt(hh, w_ref[j * D:(j + 1) * D, :]) for j in range(4))
        for j, z in enumerate((xc, bg, cg, zc)):
            a4_ref[:, j * D:(j + 1) * D] = z.astype(BF16)
        first = pl.program_id(0) % seq_tiles == 0
        up6 = jnp.where(first, 0.0, last_u_ref[6:7, :])
        up7 = jnp.where(first, 0.0, last_u_ref[7:8, :])
        u, _, _, _, _, sz, co = _conv_forward(xc, bg, cg, zc, up6, up7, wc_ref)
        last_u_ref[...] = u[tm - 8:tm, :]
        ya_ref[...] = _dot((sz * co).astype(BF16), wpc_ref[...])

    return pl.pallas_call(
        body, name="fwd_in_conv", grid=(t // tm,),
        in_specs=[_row_spec(tm, D), _whole_vmem(), pl.BlockSpec((8, D), lambda i: (0, 0)), _whole_vmem()],
        out_specs=[_row_spec(tm, 4 * D), _row_spec(tm, D)],
        out_shape=[jax.ShapeDtypeStruct((t, 4 * D), BF16), jax.ShapeDtypeStruct((t, D), F32)],
        scratch_shapes=[pltpu.VMEM((8, D), F32)],
        compiler_params=_params("arbitrary"),
    )(h, wt, wconv8, wpc)


STACK = 4 * BLK


def _band_mask(first):
    qi = lax.broadcasted_iota(jnp.int32, (STACK, 2 * BLK), 0) & (BLK - 1)
    kj = lax.broadcasted_iota(jnp.int32, (STACK, 2 * BLK), 1)
    return (kj > qi) & (kj <= qi + BLK) & (kj >= jnp.where(first, BLK, 0))


def _masked_fill(sink_ref, g, e):
    kj = lax.broadcasted_iota(jnp.int32, (STACK, 2 * BLK), 1)
    sink = jnp.concatenate([jnp.full((BLK, 2 * BLK), sink_ref[0, 2 * (4 * g + jj) + e], F32) for jj in range(4)], axis=0)
    return jnp.where(kj == 0, sink, NEG)


def _padded_pair(before, own):
    z = jnp.concatenate([before, own], axis=0).astype(F32)
    z = jnp.where(lax.broadcasted_iota(jnp.int32, z.shape, 0) == 0, 0.0, z)
    zs = pltpu.roll(z, 64, 1)
    lo = lax.broadcasted_iota(jnp.int32, z.shape, 1) < 64
    zero = jnp.zeros_like(z)
    left = [jnp.where(lo, z, zero).astype(BF16), jnp.where(lo, zs, zero).astype(BF16)]
    right = [jnp.where(lo, zero, zs).astype(BF16), jnp.where(lo, zero, z).astype(BF16)]
    return left, right


def _exp_logits(s, valid, fill):
    s = jnp.where(valid, s, fill)
    m = jnp.max(s, axis=-1, keepdims=True)
    return jnp.exp(s - m), m


def _kv_blocks(kvc_ref, kvp_ref, b, col):
    own = kvc_ref[b * BLK:(b + 1) * BLK, col:col + 128]
    before = kvp_ref[:, col:col + 128] if b == 0 else kvc_ref[(b - 1) * BLK:b * BLK, col:col + 128]
    return before, own


def _fwd_attn(sinks, q, kv, g3, blocks):
    t = q.shape[0]
    tq = blocks * BLK
    seq_blocks = SEQ_LEN // BLK

    def body(sink_ref, q_ref, kvc_ref, kvp_ref, za_ref, attn_ref, ub_ref, lse_ref):
        lo = lax.broadcasted_iota(jnp.int32, (STACK, 128), 1) < 64
        for b in range(blocks):
            rows = slice(b * BLK, (b + 1) * BLK)
            valid = _band_mask((pl.program_id(0) * blocks + b) % seq_blocks == 0)
            k_pad = _padded_pair(*_kv_blocks(kvc_ref, kvp_ref, b, 0))
            v_pad = _padded_pair(*_kv_blocks(kvc_ref, kvp_ref, b, 128))
            for g in range(2):
                qg = jnp.concatenate([q_ref[rows, j * 128:(j + 1) * 128] for j in range(4 * g, 4 * g + 4)], axis=0)
                pv, den = [], []
                for e in range(2):
                    p, m = _exp_logits(_dot_nt(qg, k_pad[e][g]), valid, _masked_fill(sink_ref, g, e))
                    both = _dot(p.astype(BF16), jnp.concatenate([v_pad[e][g], jnp.ones((2 * BLK, 128), BF16)], axis=1))
                    pv.append(both[:, 0:128])
                    den.append(both[:, 128:256])
                    lse_ref[b, 2 * g + e] = m + jnp.log(den[e])
                o = jnp.where(lo, pv[0] / den[0], pv[1] / den[1])
                for jj in range(4):
                    cols = slice((4 * g + jj) * 128, (4 * g + jj + 1) * 128)
                    oj = o[jj * BLK:(jj + 1) * BLK, :]
                    attn_ref[rows, cols] = oj
                    za = za_ref[rows, cols]
                    ub_ref[rows, cols] = (za * _sig(za) * oj).astype(BF16)

    return pl.pallas_call(
        body, name="fwd_attn", grid=(t // tq,),
        in_specs=[pl.BlockSpec(memory_space=pltpu.SMEM), _row_spec(tq, D), _row_spec(tq, 256),
                  pl.BlockSpec((BLK, 256), lambda i: (jnp.maximum(i * blocks - 1, 0), 0)), _row_spec(tq, D, 0)],
        out_specs=[_row_spec(tq, D), _row_spec(tq, D), pl.BlockSpec((blocks, 4, STACK, 128), lambda i: (i, 0, 0, 0))],
        out_shape=[jax.ShapeDtypeStruct((t, D), F32), jax.ShapeDtypeStruct((t, D), BF16),
                   jax.ShapeDtypeStruct((t // BLK, 4, STACK, 128), F32)],
        compiler_params=_params("parallel"),
    )(sinks, q, kv, kv, g3)


def _fwd_out_bwd_head(ya, ub, g3, x, target, g_post, wpa, wout, tm):
    t = x.shape[0]

    def body(ya_ref, ub_ref, ga_ref, gb_ref, x_ref, tgt_ref, gp_ref, wpa_ref, wout_ref,
             loss_ref, dout_ref, dya_ref, dub_ref, dgab_ref, dwout_ref, dwpa_ref, dgp_ref):
        @pl.when(pl.program_id(0) == 0)
        def _():
            loss_ref[...] = jnp.zeros_like(loss_ref)
            dwout_ref[...] = jnp.zeros_like(dwout_ref)
            dwpa_ref[...] = jnp.zeros_like(dwpa_ref)
            dgp_ref[...] = jnp.zeros_like(dgp_ref)

        g = gp_ref[...]
        halves = (slice(0, tm // 2), slice(tm // 2, tm))

        def stage1(rows):
            return _dot(ub_ref[rows, :], wpa_ref[...])

        def stage2(rows, yb):
            sa = _sig(ga_ref[rows, :])
            sb = _sig(gb_ref[rows, :])
            mb = (sa * ya_ref[rows, :] + sb * yb).astype(BF16)
            return sa, sb, mb, _dot(mb, wout_ref[...])

        def stage3(rows, y):
            r = lax.rsqrt(jnp.mean(y * y, axis=-1, keepdims=True) + RMS_EPS)
            n = y * r
            err = (x_ref[rows, :] + n * g) - tgt_ref[rows, :]
            sq = jnp.sum(jnp.sum(err * err, axis=0, keepdims=True), axis=1, keepdims=True)
            dout = err * (1.0 / D)
            dout_ref[rows, :] = dout
            dgp = jnp.sum(dout * n, axis=0, keepdims=True)
            dn = dout * g
            dy = (r * (dn - n * jnp.mean(dn * n, axis=-1, keepdims=True))).astype(BF16)
            return sq, dgp, dy, _dot_nt(dy, wout_ref[...])

        def stage4(rows, dm, sa, sb, yb):
            dya_ref[rows, :] = (dm * sa).astype(BF16)
            dyb = (dm * sb).astype(BF16)
            dgab_ref[rows, 0:D] = (dm * ya_ref[rows, :] * (sa * (1.0 - sa))).astype(BF16)
            dgab_ref[rows, D:2 * D] = (dm * yb * (sb * (1.0 - sb))).astype(BF16)
            dub_ref[rows, :] = _dot_nt(dyb, wpa_ref[...])
            return dyb

        yb = [stage1(rows) for rows in halves]
        s2 = [stage2(rows, yb[k]) for k, rows in enumerate(halves)]
        s3 = [stage3(rows, s2[k][3]) for k, rows in enumerate(halves)]
        dyb = [stage4(rows, s3[k][3], s2[k][0], s2[k][1], yb[k]) for k, rows in enumerate(halves)]
        loss_ref[...] += sum(s[0] for s in s3) * (0.5 / D)
        dgp_ref[0:1, :] += sum(s[1] for s in s3)
        dwout_ref[...] += _dot_tn(jnp.concatenate([s[2] for s in s2], axis=0), jnp.concatenate([s[2] for s in s3], axis=0))
        dwpa_ref[...] += _dot_tn(ub_ref[...], jnp.concatenate(dyb, axis=0))

    return pl.pallas_call(
        body, name="fwd_out_bwd_head", grid=(t // tm,),
        in_specs=[_row_spec(tm, D), _row_spec(tm, D), _row_spec(tm, D, 1), _row_spec(tm, D, 2),
                  _row_spec(tm, D), _row_spec(tm, D), pl.BlockSpec((1, D), lambda i: (0, 0)),
                  _whole_vmem(), _whole_vmem()],
        out_specs=[pl.BlockSpec((8, 128), lambda i: (0, 0)), _row_spec(tm, D), _row_spec(tm, D), _row_spec(tm, D),
                   _row_spec(tm, 2 * D), _whole_vmem(), _whole_vmem(), pl.BlockSpec((8, D), lambda i: (0, 0))],
        out_shape=[jax.ShapeDtypeStruct((8, 128), F32), jax.ShapeDtypeStruct((t, D), F32),
                   jax.ShapeDtypeStruct((t, D), BF16), jax.ShapeDtypeStruct((t, D), F32),
                   jax.ShapeDtypeStruct((t, 2 * D), BF16), jax.ShapeDtypeStruct((D, D), F32),
                   jax.ShapeDtypeStruct((D, D), F32), jax.ShapeDtypeStruct((8, D), F32)],
        compiler_params=_params("arbitrary", vmem_limit_bytes=V7X_VMEM_BYTES - (2 << 20)),
    )(ya, ub, g3, g3, x, target, g_post, wpa, wout)


def _bwd_attn(sinks, q, kv, attn, lse, dub, g3, cos_t, sin_t, wt, blocks):
    t = q.shape[0]
    tq = blocks * BLK
    seq_blocks = SEQ_LEN // BLK

    def body(sink_ref, q_ref, kvc_ref, kvp_ref, attn_ref, lse_ref, dub_ref, za_ref, c_ref, s_ref, w_ref,
             dq_ref, dza_ref, dkv_own_ref, dkv_prev_ref, dsink_ref, dh_ref):
        @pl.when(pl.program_id(0) == 0)
        def _():
            dsink_ref[...] = jnp.zeros_like(dsink_ref)

        lo = lax.broadcasted_iota(jnp.int32, (STACK, 128), 1) < 64
        lane8 = lax.broadcasted_iota(jnp.int32, (8, 128), 1)
        lo2 = lax.broadcasted_iota(jnp.int32, (2 * BLK, 128), 1) < 64
        sink_row = lax.broadcasted_iota(jnp.int32, (2 * BLK, 128), 0) == 0
        dsink = jnp.zeros((8, 128), F32)
        for b in range(blocks):
            rows = slice(b * BLK, (b + 1) * BLK)
            valid = _band_mask((pl.program_id(0) * blocks + b) % seq_blocks == 0)
            k_pad = _padded_pair(*_kv_blocks(kvc_ref, kvp_ref, b, 0))
            v_pad = _padded_pair(*_kv_blocks(kvc_ref, kvp_ref, b, 128))
            c = c_ref[rows, :]
            s = s_ref[rows, :]
            dk_acc, dv_acc = [], []
            for g in range(2):
                qg, dog = [], []
                for j in range(4 * g, 4 * g + 4):
                    cols = slice(j * 128, (j + 1) * 128)
                    za = za_ref[rows, cols]
                    sg = _sig(za)
                    dub = dub_ref[rows, cols]
                    dza_ref[rows, cols] = (dub * attn_ref[rows, cols] * (sg * (1.0 + za * (1.0 - sg)))).astype(BF16)
                    dog.append((dub * (za * sg)).astype(BF16))
                    qg.append(q_ref[rows, cols])
                qg = jnp.concatenate(qg, axis=0)
                dog = jnp.concatenate(dog, axis=0)
                dq = jnp.zeros((STACK, 128), F32)
                ds_both, p_both = [], []
                for e in range(2):
                    s_masked = jnp.where(valid, _dot_nt(qg, k_pad[e][g]), _masked_fill(sink_ref, g, e))
                    lse_rows = lse_ref[b, 2 * g + e]
                    p = jnp.exp(s_masked - jnp.concatenate([lse_rows, lse_rows], axis=1))
                    dp = _dot_nt(dog, v_pad[e][g])
                    ds = p * (dp - jnp.sum(p * dp, axis=-1, keepdims=True))
                    for jj in range(4):
                        tot = jnp.sum(ds[jj * BLK:(jj + 1) * BLK, 0:1], axis=0, keepdims=True)
                        dsink = dsink + jnp.where(lane8 == 2 * (4 * g + jj) + e, tot, 0.0)
                    ds = ds.astype(BF16)
                    dq = dq + _dot(ds, k_pad[e][g])
                    ds_both.append(ds)
                    p_both.append(p.astype(BF16))
                zero = jnp.zeros_like(qg)
                q2 = jnp.concatenate([jnp.where(lo, qg, zero), jnp.where(lo, zero, qg)], axis=0)
                do2 = jnp.concatenate([jnp.where(lo, dog, zero), jnp.where(lo, zero, dog)], axis=0)
                dk_acc.append(_dot_tn(q2, jnp.concatenate(ds_both, axis=0)).T)
                dv_acc.append(_dot_tn(do2, jnp.concatenate(p_both, axis=0)).T)
                for jj in range(4):
                    cols = slice((4 * g + jj) * 128, (4 * g + jj + 1) * 128)
                    dqj = dq[jj * BLK:(jj + 1) * BLK, :] * LOGIT_SCALE
                    dq_ref[rows, cols] = (dqj * c - _swap_halves(dqj) * s).astype(BF16)
            for col, acc in ((0, dk_acc), (128, dv_acc)):
                both = jnp.where(lo2, acc[0] + pltpu.roll(acc[0], 64, 1), acc[1] + pltpu.roll(acc[1], 64, 1))
                both = jnp.where(sink_row, 0.0, both)
                dkv_prev_ref[rows, col:col + 128] = both[0:BLK, :]
                dkv_own_ref[rows, col:col + 128] = both[BLK:2 * BLK, :]
        dsink_ref[...] += dsink
        dh_ref[...] = _dot(dq_ref[...], w_ref[ROW_Q:ROW_KV, :]) + _dot(dza_ref[...], w_ref[ROW_ZA:ROW_GA, :])

    tab = pl.BlockSpec((tq, 128), lambda i: (i % (SEQ_LEN // tq), 0))
    return pl.pallas_call(
        body, name="bwd_attn", grid=(t // tq,),
        in_specs=[pl.BlockSpec(memory_space=pltpu.SMEM), _row_spec(tq, D), _row_spec(tq, 256),
                  pl.BlockSpec((BLK, 256), lambda i: (jnp.maximum(i * blocks - 1, 0), 0)),
                  _row_spec(tq, D), pl.BlockSpec((blocks, 4, STACK, 128), lambda i: (i, 0, 0, 0)),
                  _row_spec(tq, D), _row_spec(tq, D, 0), tab, tab, _whole_vmem()],
        out_specs=[_row_spec(tq, D), _row_spec(tq, D), _row_spec(tq, 256), _row_spec(tq, 256),
                   pl.BlockSpec((8, 128), lambda i: (0, 0)), _row_spec(tq, D)],
        out_shape=[jax.ShapeDtypeStruct((t, D), BF16), jax.ShapeDtypeStruct((t, D), BF16),
                   jax.ShapeDtypeStruct((t, 256), F32), jax.ShapeDtypeStruct((t, 256), F32),
                   jax.ShapeDtypeStruct((8, 128), F32), jax.ShapeDtypeStruct((t, D), F32)],
        compiler_params=_params("arbitrary"),
    )(sinks, q, kv, kv, attn, lse, dub, g3, cos_t, sin_t, wt)


def _bwd_kv_finish(dkv_own, dkv_prev, cos_t, sin_t):
    t = dkv_own.shape[0]
    tm = 512
    seq_tiles = SEQ_LEN // tm
    n_blocks = t // BLK

    def body(own_ref, same_ref, nxt_ref, c_ref, s_ref, out_ref):
        keep = jnp.where(pl.program_id(0) % seq_tiles == seq_tiles - 1, 0.0, 1.0)
        shifted = jnp.concatenate([same_ref[BLK:tm, :], nxt_ref[...] * keep], axis=0)
        tot = own_ref[...] + shifted
        dk = tot[:, 0:128]
        out_ref[:, 0:128] = (dk * c_ref[...] - _swap_halves(dk) * s_ref[...]).astype(BF16)
        out_ref[:, 128:256] = tot[:, 128:256].astype(BF16)

    tab = pl.BlockSpec((tm, 128), lambda i: (i % seq_tiles, 0))
    return pl.pallas_call(
        body, name="bwd_kv_finish", grid=(t // tm,),
        in_specs=[_row_spec(tm, 256), _row_spec(tm, 256),
                  pl.BlockSpec((BLK, 256), lambda i: (jnp.minimum((i + 1) * (tm // BLK), n_blocks - 1), 0)), tab, tab],
        out_specs=_row_spec(tm, 256),
        out_shape=jax.ShapeDtypeStruct((t, 256), BF16),
        compiler_params=_params("parallel"),
    )(dkv_own, dkv_prev, dkv_prev, cos_t, sin_t)


STAGE_ROWS = 256


def _bwd_conv(dya, a4, h, wconv8, wpc, tm, parts):
    t = a4.shape[0]
    n_t = t // tm
    sub = tm // parts
    seq_tiles = SEQ_LEN // tm

    def body(dya_ref, xc_ref, bg_ref, cg_ref, zc_ref, xcp_ref, cgp_ref, w_ref, wpc_ref, h_ref,
             da4_ref, dwpc_ref, dwc_ref, o32_ref, o16_ref, acc_ref, stage_ref, later_ref, sems):
        step = pl.program_id(0)
        tile = n_t - 1 - step

        @pl.when(step == 0)
        def _():
            dwpc_ref[...] = jnp.zeros_like(dwpc_ref)
            dwc_ref[...] = jnp.zeros_like(dwc_ref)
            acc_ref[...] = jnp.zeros_like(acc_ref)

        keep_prev = jnp.where(tile % seq_tiles == 0, 0.0, 1.0)
        ends_sequence = tile % seq_tiles == seq_tiles - 1

        def part(p, later):
            r0 = p * sub
            here = slice(r0, r0 + sub)
            if p == 0:
                u_prev = cgp_ref[14:16, :].astype(F32) * xcp_ref[14:16, :].astype(F32) * keep_prev
            else:
                u_prev = cg_ref[r0 - 2:r0, :].astype(F32) * xc_ref[r0 - 2:r0, :].astype(F32)
            xc = xc_ref[here, :].astype(F32)
            bg = bg_ref[here, :].astype(F32)
            cg = cg_ref[here, :].astype(F32)
            zc = zc_ref[here, :].astype(F32)
            u, u_m1, u_m2, yconv, sg, sz, co = _conv_forward(xc, bg, cg, zc, u_prev[0:1, :], u_prev[1:2, :], w_ref)
            ua = (sz * co).astype(BF16)
            dua = _dot_nt(dya_ref[here, :], wpc_ref[...])
            da4_ref[here, 3 * D:4 * D] = (dua * co * (sg * (1.0 + zc * (1.0 - sg)))).astype(BF16)
            dco = dua * sz
            da4_ref[here, D:2 * D] = (dco * yconv).astype(BF16)
            dyc = dco * bg
            dwc = jnp.concatenate([jnp.sum(dyc * s, axis=0, keepdims=True) for s in (u_m2, u_m1, u)], axis=0)
            rows = lax.broadcasted_iota(jnp.int32, xc.shape, 0)
            n0 = later[0:1, :]
            n1 = later[1:2, :]
            dyc_p1 = jnp.where(rows == sub - 1, n0, pltpu.roll(dyc, sub - 1, 0))
            dyc_p2 = jnp.where(rows == sub - 2, n0, jnp.where(rows == sub - 1, n1, pltpu.roll(dyc, sub - 2, 0)))
            du = w_ref[2:3, :] * dyc + w_ref[1:2, :] * dyc_p1 + w_ref[0:1, :] * dyc_p2
            da4_ref[here, 0:D] = (du * cg).astype(BF16)
            da4_ref[here, 2 * D:3 * D] = (du * xc).astype(BF16)
            return ua, dwc, dyc[0:8, :]

        later = jnp.where(ends_sequence, 0.0, later_ref[...])
        uas, dwc = [], jnp.zeros((3, D), F32)
        for p in reversed(range(parts)):
            ua, dwc_p, later = part(p, later)
            uas.insert(0, ua)
            dwc = dwc + dwc_p
        later_ref[...] = later
        dwpc_ref[...] += _dot_tn(jnp.concatenate(uas, axis=0), dya_ref[...])
        dwc_ref[0:3, :] += dwc
        for j in range(4):
            acc_ref[j * D:(j + 1) * D, :] += _dot_tn(da4_ref[:, j * D:(j + 1) * D], h_ref[...])

        @pl.when(step == n_t - 1)
        def _():
            c32 = pltpu.make_async_copy(acc_ref, o32_ref.at[pl.ds(0, 4 * D)], sems.at[0])
            c32.start()
            for j in range(4 * D // STAGE_ROWS):
                rows = pl.ds(j * STAGE_ROWS, STAGE_ROWS)
                stage_ref[...] = acc_ref[rows, :].astype(BF16)
                c16 = pltpu.make_async_copy(stage_ref, o16_ref.at[rows], sems.at[1])
                c16.start()
                c16.wait()
            c32.wait()

    def rows_of_tile(width, col=0):
        return pl.BlockSpec((tm, width), lambda s: (n_t - 1 - s, col))

    def prev(col):
        return pl.BlockSpec((16, D), lambda s: (jnp.maximum((n_t - 1 - s) * (tm // 16) - 1, 0), col))

    hbm = pl.BlockSpec(memory_space=pl.ANY)
    out = pl.pallas_call(
        body, name="bwd_conv", grid=(n_t,),
        in_specs=[rows_of_tile(D), rows_of_tile(D, 0), rows_of_tile(D, 1), rows_of_tile(D, 2), rows_of_tile(D, 3),
                  prev(0), prev(2), pl.BlockSpec((8, D), lambda s: (0, 0)), _whole_vmem(), rows_of_tile(D)],
        out_specs=[rows_of_tile(4 * D), _whole_vmem(), pl.BlockSpec((8, D), lambda s: (0, 0)), hbm, hbm],
        out_shape=[jax.ShapeDtypeStruct((t, 4 * D), BF16), jax.ShapeDtypeStruct((D, D), F32),
                   jax.ShapeDtypeStruct((8, D), F32), jax.ShapeDtypeStruct((D_IN, D), F32),
                   jax.ShapeDtypeStruct((D_IN, D), BF16)],
        scratch_shapes=[pltpu.VMEM((4 * D, D), F32), pltpu.VMEM((STAGE_ROWS, D), BF16), pltpu.VMEM((8, D), F32),
                        pltpu.SemaphoreType.DMA((2,))],
        compiler_params=pltpu.CompilerParams(dimension_semantics=("arbitrary",), vmem_limit_bytes=V7X_VMEM_BYTES - (2 << 20)),
    )(dya, a4, a4, a4, a4, a4, a4, wconv8, wpc, h)
    return out[0], out[1], out[2], (out[3], out[4])


def _bwd_dh(da4, dh_part, dkv, dgab, wt, x, g_pre, dout, tm):
    t = x.shape[0]

    def body(da4_ref, dhp_ref, dkv_ref, dgab_ref, w_ref, x_ref, g_ref, dout_ref, gx_ref, dg_ref):
        @pl.when(pl.program_id(0) == 0)
        def _():
            dg_ref[...] = jnp.zeros_like(dg_ref)

        dh = dhp_ref[...] + _dot(da4_ref[...], w_ref[0:ROW_Q, :])
        dh += _dot(dkv_ref[...], w_ref[ROW_KV:ROW_ZA, :])
        dh += _dot(dgab_ref[...], w_ref[ROW_GA:D_IN, :])
        xf = x_ref[...]
        r = lax.rsqrt(jnp.mean(xf * xf, axis=-1, keepdims=True) + RMS_EPS)
        xn = xf * r
        dg_ref[0:1, :] += jnp.sum(dh * xn, axis=0, keepdims=True)
        dxn = dh * g_ref[...]
        gx_ref[...] = dout_ref[...] + r * (dxn - xn * jnp.mean(dxn * xn, axis=-1, keepdims=True))

    return pl.pallas_call(
        body, name="bwd_dh", grid=(t // tm,),
        in_specs=[_row_spec(tm, 4 * D), _row_spec(tm, D), _row_spec(tm, 256), _row_spec(tm, 2 * D),
                  _whole_vmem(), _row_spec(tm, D), pl.BlockSpec((1, D), lambda i: (0, 0)), _row_spec(tm, D)],
        out_specs=[_row_spec(tm, D), pl.BlockSpec((8, D), lambda i: (0, 0))],
        out_shape=[jax.ShapeDtypeStruct((t, D), F32), jax.ShapeDtypeStruct((8, D), F32)],
        compiler_params=_params("arbitrary"),
    )(da4, dh_part, dkv, dgab, wt, x, g_pre, dout)


def _bwd_dw_in(h, piece, row0, nb, tm, name, prev):
    t, n = piece.shape
    n_t = t // tm

    def body(*refs):
        h_ref, p_ref = refs[0], refs[1]
        o32_ref, o16_ref, acc_ref, acc16_ref, sems = refs[-5:]
        j, i = pl.program_id(0), pl.program_id(1)

        @pl.when(i == 0)
        def _():
            acc_ref[...] = jnp.zeros_like(acc_ref)

        acc_ref[...] += _dot_tn(p_ref[...], h_ref[...])

        @pl.when(i == n_t - 1)
        def _():
            acc16_ref[...] = acc_ref[...].astype(BF16)
            rows = pl.ds(pl.multiple_of(row0 + j * nb, 16), nb)
            c32 = pltpu.make_async_copy(acc_ref, o32_ref.at[rows], sems.at[0])
            c16 = pltpu.make_async_copy(acc16_ref, o16_ref.at[rows], sems.at[1])
            c32.start()
            c16.start()
            c32.wait()
            c16.wait()

    hbm = pl.BlockSpec(memory_space=pl.ANY)
    carried = [] if prev is None else list(prev)
    return pl.pallas_call(
        body, name=name, grid=(n // nb, n_t),
        in_specs=[pl.BlockSpec((tm, D), lambda j, i: (i, 0)), pl.BlockSpec((tm, nb), lambda j, i: (i, j))]
        + [hbm] * len(carried),
        out_specs=[hbm, hbm],
        out_shape=[jax.ShapeDtypeStruct((D_IN, D), F32), jax.ShapeDtypeStruct((D_IN, D), BF16)],
        scratch_shapes=[pltpu.VMEM((nb, D), F32), pltpu.VMEM((nb, D), BF16), pltpu.SemaphoreType.DMA((2,))],
        input_output_aliases={2: 0, 3: 1} if carried else {},
        compiler_params=_params("arbitrary", "arbitrary"),
    )(h, piece, *carried)


def _place():
    x, y, c = lax.axis_index("x"), lax.axis_index("y"), lax.axis_index("c")
    return x, y, c, 4 * x + 2 * y + c


def _peer(x, y, c, k):
    return (1 - x if k & 4 else x, 1 - y if k & 2 else y, 1 - c if k & 1 else c)


ICI_MASKS = (4, 2, 6)


def _all_gather(shards):
    n = len(shards)

    def body(*refs):
        src, dst = refs[:n], refs[n:2 * n]
        send_sems, recv_sems, local_sems = refs[2 * n:]
        x, y, c, me = _place()
        sibling = _peer(x, y, c, 1)

        def copy(a, s, block, to, own=False):
            return pltpu.make_async_remote_copy(
                src_ref=src[a] if own else dst[a].at[block], dst_ref=dst[a].at[block],
                send_sem=send_sems.at[a * 7 + s], recv_sem=recv_sems.at[a * 7 + s], device_id=to, device_id_type=MESH_ID)

        local = [pltpu.make_async_copy(src[a], dst[a].at[me], local_sems.at[a]) for a in range(n)]
        for cp in local:
            cp.start()
        started = [copy(a, 0, me, sibling, own=True) for a in range(n)]
        started += [copy(a, 1 + j, me, _peer(x, y, c, k), own=True) for j, k in enumerate(ICI_MASKS) for a in range(n)]
        for cp in started:
            cp.start()
        for j, k in enumerate(ICI_MASKS):
            for a in range(n):
                copy(a, 1 + j, me ^ k, sibling).wait_recv()
                fwd = copy(a, 4 + j, me ^ k, sibling)
                fwd.start()
                started.append(fwd)
        for a in range(n):
            copy(a, 0, me ^ 1, sibling).wait_recv()
        for j, k in enumerate(ICI_MASKS):
            for a in range(n):
                copy(a, 4 + j, me ^ 1 ^ k, sibling).wait_recv()
        for cp in started:
            cp.wait_send()
        for cp in local:
            cp.wait()

    hbm = pl.BlockSpec(memory_space=pl.ANY)
    return pl.pallas_call(
        body, name="all_gather_weights",
        in_specs=[hbm] * n, out_specs=[hbm] * n,
        out_shape=[jax.ShapeDtypeStruct((N_DEV,) + s.shape, s.dtype) for s in shards],
        scratch_shapes=[pltpu.SemaphoreType.DMA((7 * n,)), pltpu.SemaphoreType.DMA((7 * n,)),
                        pltpu.SemaphoreType.DMA((n,))],
    )(*shards)


def _direct_copies(src, land, send_sems, recv_sems):
    x, y, c, me = _place()
    return [pltpu.make_async_remote_copy(
        src_ref=src[a], dst_ref=land[a].at[me], send_sem=send_sems.at[a * 7 + k - 1],
        recv_sem=recv_sems.at[a * 7 + k - 1], device_id=_peer(x, y, c, k), device_id_type=MESH_ID)
        for k in range(1, N_DEV) for a in range(len(src))]


def _gather_start(shards, name):
    n = len(shards)

    def body(*refs):
        src, land = refs[:n], refs[n:2 * n]
        send_sems, recv_sems = refs[2 * n], refs[2 * n + 1]
        token_ref = refs[-1]
        for cp in _direct_copies(src, land, send_sems, recv_sems):
            cp.start()
        token_ref[...] = jnp.zeros_like(token_ref)

    hbm = pl.BlockSpec(memory_space=pltpu.HBM)
    sem = pl.BlockSpec(memory_space=pltpu.SEMAPHORE)
    lands = [lax.empty((N_DEV,) + s.shape, s.dtype) for s in shards]
    out = pl.pallas_call(
        body, name=name + "_start",
        out_shape=(pltpu.SemaphoreType.DMA((7 * n,)), pltpu.SemaphoreType.DMA((7 * n,)),
                   *[pltpu.HBM(s.shape, s.dtype) for s in shards], *[pltpu.HBM(s.shape, s.dtype) for s in lands],
                   jax.ShapeDtypeStruct((8, 128), F32)),
        in_specs=[hbm] * (2 * n), out_specs=(sem, sem, *[hbm] * (2 * n), _whole_vmem()),
        input_output_aliases={a: 2 + a for a in range(2 * n)},
        compiler_params=pltpu.CompilerParams(has_side_effects=pltpu.SideEffectType.DATAFLOW_SIDE_EFFECTING),
    )(*[pltpu.with_memory_space_constraint(s, pltpu.HBM) for s in list(shards) + lands])
    return out[0], out[1], out[2:2 + n], out[2 + n:2 + 2 * n], out[-1]


def _gather_wait(send_sems, recv_sems, flying, lands, after, name):
    n = len(flying)

    def body(*refs):
        src, land = refs[:n], refs[n:2 * n]
        for cp in _direct_copies(src, land, refs[2 * n], refs[2 * n + 1]):
            cp.wait_send()
            cp.wait_recv()

    hbm = pl.BlockSpec(memory_space=pltpu.HBM)
    sem = pl.BlockSpec(memory_space=pltpu.SEMAPHORE)
    out = pl.pallas_call(
        body, name=name + "_wait",
        out_shape=tuple(pltpu.HBM(s.shape, s.dtype) for s in list(flying) + list(lands)),
        in_specs=[hbm] * (2 * n) + [sem, sem, pl.BlockSpec(memory_space=pl.ANY)], out_specs=tuple([hbm] * (2 * n)),
        input_output_aliases={a: a for a in range(2 * n)},
        compiler_params=pltpu.CompilerParams(has_side_effects=pltpu.SideEffectType.DATAFLOW_SIDE_EFFECTING),
    )(*flying, *lands, send_sems, recv_sems, after)
    return out[n:]


def _exchange_sibling(by_dest):
    n = len(by_dest)

    def body(*refs):
        src, dst = refs[:n], refs[n:2 * n]
        send_sems, recv_sems = refs[2 * n:]
        x, y, c, _ = _place()
        sibling = _peer(x, y, c, 1)
        copies = [pltpu.make_async_remote_copy(
            src_ref=src[a].at[2 * p + (1 - c)], dst_ref=dst[a].at[p], send_sem=send_sems.at[a * 4 + p],
            recv_sem=recv_sems.at[a * 4 + p], device_id=sibling, device_id_type=MESH_ID)
            for a in range(n) for p in range(4)]
        for cp in copies:
            cp.start()
        for cp in copies:
            cp.wait_recv()
        for cp in copies:
            cp.wait_send()

    hbm = pl.BlockSpec(memory_space=pl.ANY)
    return pl.pallas_call(
        body, name="exchange_sibling", in_specs=[hbm] * n, out_specs=[hbm] * n,
        out_shape=[jax.ShapeDtypeStruct((4,) + s.shape[1:], s.dtype) for s in by_dest],
        scratch_shapes=[pltpu.SemaphoreType.DMA((4 * n,)), pltpu.SemaphoreType.DMA((4 * n,))],
    )(*by_dest)


def _chip_copies(src, land, send_sems, recv_sems):
    x, y, c, _ = _place()
    chip = 2 * x + y
    return [pltpu.make_async_remote_copy(
        src_ref=src[a].at[chip ^ (k >> 1)], dst_ref=land[a].at[j], send_sem=send_sems.at[a * 3 + j],
        recv_sem=recv_sems.at[a * 3 + j], device_id=_peer(x, y, c, k), device_id_type=MESH_ID)
        for j, k in enumerate(ICI_MASKS) for a in range(len(src))]


def _exchange_chips_start(by_chip):
    n = len(by_chip)

    def body(*refs):
        src, land = refs[:n], refs[n:2 * n]
        send_sems, recv_sems = refs[2 * n], refs[2 * n + 1]
        token_ref = refs[-1]
        for cp in _chip_copies(src, land, send_sems, recv_sems):
            cp.start()
        token_ref[...] = jnp.zeros_like(token_ref)

    hbm = pl.BlockSpec(memory_space=pltpu.HBM)
    sem = pl.BlockSpec(memory_space=pltpu.SEMAPHORE)
    lands = [lax.empty((3,) + s.shape[1:], s.dtype) for s in by_chip]
    out = pl.pallas_call(
        body, name="exchange_chips_start",
        out_shape=(pltpu.SemaphoreType.DMA((3 * n,)), pltpu.SemaphoreType.DMA((3 * n,)),
                   *[pltpu.HBM(s.shape, s.dtype) for s in by_chip], *[pltpu.HBM(s.shape, s.dtype) for s in lands],
                   jax.ShapeDtypeStruct((8, 128), F32)),
        in_specs=[hbm] * (2 * n), out_specs=(sem, sem, *[hbm] * (2 * n), _whole_vmem()),
        input_output_aliases={a: 2 + a for a in range(2 * n)},
        compiler_params=pltpu.CompilerParams(has_side_effects=pltpu.SideEffectType.DATAFLOW_SIDE_EFFECTING),
    )(*[pltpu.with_memory_space_constraint(s, pltpu.HBM) for s in list(by_chip) + lands])
    return out[0], out[1], out[2:2 + n], out[2 + n:2 + 2 * n], out[-1]


def _exchange_chips_wait(send_sems, recv_sems, flying, lands, after):
    n = len(flying)

    def body(*refs):
        src, land = refs[:n], refs[n:2 * n]
        send_sems_ref, recv_sems_ref = refs[2 * n], refs[2 * n + 1]
        for cp in _chip_copies(src, land, send_sems_ref, recv_sems_ref):
            cp.wait_send()
            cp.wait_recv()

    hbm = pl.BlockSpec(memory_space=pltpu.HBM)
    sem = pl.BlockSpec(memory_space=pltpu.SEMAPHORE)
    out = pl.pallas_call(
        body, name="exchange_chips_wait",
        out_shape=tuple(pltpu.HBM(s.shape, s.dtype) for s in list(flying) + list(lands)),
        in_specs=[hbm] * (2 * n) + [sem, sem, pl.BlockSpec(memory_space=pl.ANY)], out_specs=tuple([hbm] * (2 * n)),
        input_output_aliases={a: a for a in range(2 * n)},
        compiler_params=pltpu.CompilerParams(has_side_effects=pltpu.SideEffectType.DATAFLOW_SIDE_EFFECTING),
    )(*flying, *lands, send_sems, recv_sems, after)
    return out[n:]


def _adamw_math(w, g, m, v):
    m = ADAM_B1 * m + (1.0 - ADAM_B1) * g
    v = ADAM_B2 * v + (1.0 - ADAM_B2) * (g * g)
    m_hat = m / (1.0 - ADAM_B1 ** ADAM_STEP)
    v_hat = v / (1.0 - ADAM_B2 ** ADAM_STEP)
    return -ADAM_LR * (m_hat / (jnp.sqrt(v_hat) + ADAM_EPS) + ADAM_WD * w), m, v


def _pair_sum(owns, recvs, c_arr, tr, name):
    n = len(owns)
    _, rows, cols = owns[0].shape

    def body(c_ref, *refs):
        for a in range(n):
            s = refs[a][...] + refs[n + a][...].astype(F32)
            refs[2 * n + a][...] = s
            refs[3 * n + a][...] = s.astype(BF16)

    by_chip = pl.BlockSpec((None, tr, cols), lambda p, i, c_ref: (p, i, 0))
    mine = pl.BlockSpec((None, tr, cols), lambda p, i, c_ref: (2 * p + c_ref[0], i, 0))
    out = pl.pallas_call(
        body, name=name,
        grid_spec=pltpu.PrefetchScalarGridSpec(
            num_scalar_prefetch=1, grid=(4, rows // tr), in_specs=[mine] * n + [by_chip] * n, out_specs=[by_chip] * (2 * n)),
        out_shape=[jax.ShapeDtypeStruct((4, rows, cols), F32)] * n + [jax.ShapeDtypeStruct((4, rows, cols), BF16)] * n,
        compiler_params=_params("parallel", "parallel"),
    )(c_arr, *owns, *recvs)
    return out[:n], out[n:]


def _chip_sum(pairs, recvs, chip_arr, tr, name, adam=None):
    n = len(pairs)
    _, rows, cols = pairs[0].shape
    n_state = 0 if adam is None else 3 * n

    def body(chip_ref, *refs):
        outs = refs[2 * n + n_state:]
        for a in range(n):
            g = refs[a][...]
            for j in range(3):
                g = g + refs[n + a][j].astype(F32)
            outs[a][...] = g
            if adam is not None:
                w_ref, m_ref, v_ref = (refs[2 * n + s * n + a] for s in range(3))
                outs[n + a][...], outs[2 * n + a][...], outs[3 * n + a][...] = _adamw_math(w_ref[...], g, m_ref[...], v_ref[...])

    blk = pl.BlockSpec((tr, cols), lambda i, chip_ref: (i, 0))
    n_out = n if adam is None else 4 * n
    out = pl.pallas_call(
        body, name=name,
        grid_spec=pltpu.PrefetchScalarGridSpec(
            num_scalar_prefetch=1, grid=(rows // tr,),
            in_specs=[pl.BlockSpec((None, tr, cols), lambda i, chip_ref: (chip_ref[0], i, 0))] * n
            + [pl.BlockSpec((3, tr, cols), lambda i, chip_ref: (0, i, 0))] * n + [blk] * n_state,
            out_specs=[blk] * n_out),
        out_shape=[jax.ShapeDtypeStruct((rows, cols), F32)] * n_out,
        compiler_params=_params("parallel"),
    )(chip_arr, *pairs, *recvs, *([] if adam is None else [t for group in adam for t in group]))
    return out if adam is None else (out[:n], out[n:2 * n], out[2 * n:3 * n], out[3 * n:])


def _adamw(ws, gs, ms, vs, name):
    n = len(ws)

    def body(*refs):
        for a in range(n):
            w_ref, g_ref, m_ref, v_ref = (refs[s * n + a] for s in range(4))
            refs[4 * n + a][...], refs[5 * n + a][...], refs[6 * n + a][...] = _adamw_math(
                w_ref[...], g_ref[...], m_ref[...], v_ref[...])

    out = pl.pallas_call(body, name=name, out_shape=[jax.ShapeDtypeStruct(w.shape, F32) for w in ws] * 3)(
        *ws, *gs, *ms, *vs)
    return out[:n], out[n:2 * n], out[2 * n:]


def _sum_small(small_all):
    def body(s_ref, o_ref):
        g = s_ref[0]
        for d in range(1, N_DEV):
            g = g + s_ref[d]
        o_ref[...] = g

    return pl.pallas_call(body, name="sum_small", out_shape=jax.ShapeDtypeStruct(small_all.shape[1:], F32))(small_all)


def _rope_tables():
    inv_freq = ROPE_THETA ** (-jnp.arange(0, HEAD_DIM, 2, dtype=F32) / HEAD_DIM)
    ang = jnp.arange(SEQ_LEN).astype(F32)[:, None] * inv_freq[None, :]
    cos, sin = jnp.cos(ang), jnp.sin(ang)
    return jnp.tile(cos, (1, 4)), jnp.tile(jnp.concatenate([-sin, sin], axis=1), (1, 2))


def _local_step(x, target, g_pre, g_post, sinks, wt, wconv, squares, start_exchange=None):
    cos_t, sin_t = _rope_tables()
    wconv8 = jnp.pad(wconv, ((0, 5), (0, 0)))
    h, q, kv, g3 = _fwd_in_attn(x, g_pre, wt, cos_t, sin_t, 512)
    wpc, wpa, wout = squares(kv)
    a4, ya = _fwd_in_conv(h, wt, wconv8, wpc, 512)
    attn, ub, lse = _fwd_attn(sinks, q, kv, g3, 4)
    loss8, dout, dya, dub, dgab, dwout, dwpa, dgpost8 = _fwd_out_bwd_head(ya, ub, g3, x, target, g_post, wpa, wout, 512)
    dq, dza, dkv_own, dkv_prev, dsink8, dh_part = _bwd_attn(sinks, q, kv, attn, lse, dub, g3, cos_t, sin_t, wt, 4)
    dkv = _bwd_kv_finish(dkv_own, dkv_prev, cos_t, sin_t)
    da4, dwpc, dwconv8, dwt = _bwd_conv(dya, a4, h, wconv8, wpc, 512, 2)
    dwt = _bwd_dw_in(h, dq, ROW_Q, 1024, 1024, "bwd_dw_in_q", dwt)
    dwt = _bwd_dw_in(h, dkv, ROW_KV, 256, 1024, "bwd_dw_in_kv", dwt)
    dwt = _bwd_dw_in(h, dza, ROW_ZA, 1024, 1024, "bwd_dw_in_za", dwt)
    dwt32, dwt16 = _bwd_dw_in(h, dgab, ROW_GA, 1024, 1024, "bwd_dw_in_gates", dwt)
    token, pending = (None, None) if start_exchange is None else start_exchange(dwt32, dwt16, dwpc, dwpa, dwout)
    g_pre_after = g_pre if token is None else g_pre + token[0:1, 0:1]
    grad_x, dgpre8 = _bwd_dh(da4, dh_part, dkv, dgab, wt, x, g_pre_after, dout, 512)
    small = jnp.concatenate([dgpre8, dgpost8, jnp.pad(dsink8, ((0, 0), (0, D - 128))), dwconv8,
                             jnp.pad(loss8, ((0, 0), (0, D - 128)))], axis=0)
    return loss8[0, 0], grad_x, dwt32, dwt16, dwpc, dwpa, dwout, small, pending


def kernel(x, g_pre, g_post, w_in, w_conv, sinks, w_proj_conv, w_proj_attn, w_out, loss_target, m_g_pre, m_g_post, m_w_in, m_w_conv, m_sinks, m_w_proj_conv, m_w_proj_attn, m_w_out, v_g_pre, v_g_post, v_w_in, v_w_conv, v_sinks, v_w_proj_conv, v_w_proj_attn, v_w_out):
    batch = x.shape[0]
    mx, my, mc, me = _place()
    c_arr = jnp.reshape(mc, (1,)).astype(jnp.int32)
    chip_arr = jnp.reshape(2 * mx + my, (1,)).astype(jnp.int32)

    g_wt, g_conv = _all_gather([w_in[0].T.astype(BF16), jnp.pad(w_conv[0], ((0, 5), (0, 0)))])
    wt = g_wt.reshape(D_IN, D)
    wconv = g_conv[:, 0:3, :].transpose(1, 0, 2).reshape(3, D)
    sq_mine = [w.astype(BF16) for w in (w_proj_conv[0], w_proj_attn[0], w_out[0])]
    wt, sq_mine = lax.optimization_barrier((wt, sq_mine))
    sq_send, sq_recv, sq_flying, sq_lands, sq_token = _gather_start(sq_mine, "gather_squares")

    def squares(after):
        got = _gather_wait(sq_send, sq_recv, sq_flying, sq_lands, after, "gather_squares")
        return [lax.dynamic_update_index_in_dim(full, mine, me, 0).reshape(D, D) for full, mine in zip(got, sq_mine)]

    def start_exchange(dwt32, dwt16, dwpc, dwpa, dwout):
        own_sq = [g.reshape(N_DEV, SHARD_SQ, D) for g in (dwpc, dwpa, dwout)]
        own_in = dwt32.reshape(N_DEV, SHARD_IN, D)
        from_sibling = _exchange_sibling([dwt16.reshape(N_DEV, SHARD_IN, D)] + [g.astype(BF16) for g in own_sq])
        in32, in16 = _pair_sum([own_in], from_sibling[:1], c_arr, SHARD_IN // 2, "pair_sum_w_in")
        sq32, sq16 = _pair_sum(own_sq, from_sibling[1:], c_arr, SHARD_SQ, "pair_sum_squares")
        send_sems, recv_sems, flying, lands, token = _exchange_chips_start(list(in16) + list(sq16))
        return token, (send_sems, recv_sems, flying, lands, in32, sq32)

    _, grad_x, _, _, _, _, _, small, pending = _local_step(
        x.reshape(batch * SEQ_LEN, D), loss_target.reshape(batch * SEQ_LEN, D), g_pre + sq_token[0:1, 0:1], g_post,
        sinks, wt, wconv, squares, start_exchange)
    sm_send, sm_recv, sm_flying, sm_lands, sm_token = _gather_start([small], "gather_small")
    send_sems, recv_sems, flying, lands, in32, sq32 = pending
    from_chips = _exchange_chips_wait(send_sems, recv_sems, flying, lands, sm_token)

    o_in = [o[0].T for o in _chip_sum(
        in32, from_chips[:1], chip_arr, SHARD_IN // 3, "chip_sum_adamw_w_in",
        adam=([w_in[0].T], [m_w_in[0].T], [v_w_in[0].T]))]
    g_in_mine, o_in = o_in[0], o_in[1:]
    g_sq, d_sq, m_sq, v_sq = _chip_sum(
        sq32, from_chips[1:], chip_arr, SHARD_SQ, "chip_sum_adamw_squares",
        adam=([w_proj_conv[0], w_proj_attn[0], w_out[0]], [m_w_proj_conv[0], m_w_proj_attn[0], m_w_out[0]],
              [v_w_proj_conv[0], v_w_proj_attn[0], v_w_out[0]]))
    both_done, g_in_mine = lax.optimization_barrier((d_sq[0], g_in_mine))
    (small_all,) = _gather_wait(sm_send, sm_recv, sm_flying, sm_lands, both_done, "gather_small")
    gs = _sum_small(lax.dynamic_update_index_in_dim(small_all, small, me, 0))
    g_g_pre, g_g_post, g_sinks, loss = gs[0:1], gs[8:9], gs[16:17, 0:N_HEADS], gs[32, 0]
    g_conv_mine = lax.dynamic_slice_in_dim(gs[24:27], me * SHARD_SQ, SHARD_SQ, axis=1)
    o_small = _adamw([g_pre, g_post, sinks, w_conv[0]], [g_g_pre, g_g_post, g_sinks, g_conv_mine],
                     [m_g_pre, m_g_post, m_sinks, m_w_conv[0]], [v_g_pre, v_g_post, v_sinks, v_w_conv[0]], "adamw_small")

    grads = [g_g_pre, g_g_post, g_in_mine[None], g_conv_mine[None], g_sinks] + [g[None] for g in g_sq]
    rest = []
    for idx, sq in enumerate((d_sq, m_sq, v_sq)):
        gp, gq, sk, cv = o_small[idx]
        rest += [gp, gq, o_in[idx][None], cv[None], sk] + [s[None] for s in sq]
    return (loss, grad_x.reshape(batch, SEQ_LEN, D), *grads, *rest)
```

```python
import jax
import jax.numpy as jnp
from jax import lax
from jax.experimental import pallas as pl
from jax.experimental.pallas import tpu as pltpu

D = 1024
N_HEADS = 16
HEAD_DIM = 64
LOGIT_SCALE = HEAD_DIM ** -0.5
BLK = 128
SEQ_LEN = 2048
D_IN = 8448
ROW_Q, ROW_KV, ROW_ZA, ROW_GA = 4 * D, 5 * D, 5 * D + 256, 6 * D + 256
SHARD_IN = D_IN // 8
SHARD_SQ = D // 8
N_DEV = 8
V7X_VMEM_BYTES = 64 << 20
ROPE_THETA = 10000.0
RMS_EPS = 1e-6
NEG = -1e30
ADAM_LR, ADAM_B1, ADAM_B2, ADAM_EPS, ADAM_WD, ADAM_STEP = 0.001, 0.9, 0.999, 1e-08, 0.01, 10

F32 = jnp.float32
BF16 = jnp.bfloat16
MESH_ID = pl.DeviceIdType.MESH


def _dot(a, b):
    return jnp.dot(a, b, preferred_element_type=F32)


def _dot_nt(a, b):
    return lax.dot_general(a, b, (((1,), (1,)), ((), ())), preferred_element_type=F32)


def _dot_tn(a, b):
    return lax.dot_general(a, b, (((0,), (0,)), ((), ())), preferred_element_type=F32)


def _sig(z):
    return 1.0 / (1.0 + jnp.exp(-z))


def _swap_halves(z):
    lane = lax.broadcasted_iota(jnp.int32, z.shape, 1)
    return jnp.where((lane & 63) < 32, pltpu.roll(z, 96, 1), pltpu.roll(z, 32, 1))


def _row_spec(tm, width, col=0):
    return pl.BlockSpec((tm, width), lambda i: (i, col))


def _whole_vmem():
    return pl.BlockSpec(memory_space=pltpu.VMEM)


def _params(*sem, vmem_limit_bytes=None):
    return pltpu.CompilerParams(dimension_semantics=sem, vmem_limit_bytes=vmem_limit_bytes)


def _fwd_in_attn(x, g_pre, wt, cos_t, sin_t, tm):
    t = x.shape[0]
    seq_tiles = SEQ_LEN // tm

    def body(x_ref, g_ref, w_ref, c_ref, s_ref, h_ref, q_ref, kv_ref, g3_ref):
        xf = x_ref[...]
        r = lax.rsqrt(jnp.mean(xf * xf, axis=-1, keepdims=True) + RMS_EPS)
        hh = ((xf * r) * g_ref[...]).astype(BF16)
        h_ref[...] = hh
        c = c_ref[...]
        s = s_ref[...]

        def rope(z):
            return z * c + _swap_halves(z) * s

        q = _dot_nt(hh, w_ref[ROW_Q:ROW_Q + D, :])
        for j in range(D // 128):
            q_ref[:, j * 128:(j + 1) * 128] = (rope(q[:, j * 128:(j + 1) * 128]) * LOGIT_SCALE).astype(BF16)
        kv = _dot_nt(hh, w_ref[ROW_KV:ROW_KV + 256, :])
        kv_ref[:, 0:128] = rope(kv[:, 0:128]).astype(BF16)
        kv_ref[:, 128:256] = kv[:, 128:256].astype(BF16)
        for j in range(3):
            g3_ref[:, j * D:(j + 1) * D] = _dot_nt(hh, w_ref[ROW_ZA + j * D:ROW_ZA + (j + 1) * D, :])

    tab = pl.BlockSpec((tm, 128), lambda i: (i % seq_tiles, 0))
    return pl.pallas_call(
        body, name="fwd_in_attn", grid=(t // tm,),
        in_specs=[_row_spec(tm, D), pl.BlockSpec((1, D), lambda i: (0, 0)), _whole_vmem(), tab, tab],
        out_specs=[_row_spec(tm, D), _row_spec(tm, D), _row_spec(tm, 256), _row_spec(tm, 3 * D)],
        out_shape=[jax.ShapeDtypeStruct((t, D), BF16), jax.ShapeDtypeStruct((t, D), BF16),
                   jax.ShapeDtypeStruct((t, 256), BF16), jax.ShapeDtypeStruct((t, 3 * D), F32)],
        compiler_params=_params("parallel"),
    )(x, g_pre, wt, cos_t, sin_t)


def _conv_forward(xc, bg, cg, zc, up6, up7, w_ref):
    rows = lax.broadcasted_iota(jnp.int32, xc.shape, 0)
    u = cg * xc
    u_m1 = jnp.where(rows == 0, up7, pltpu.roll(u, 1, 0))
    u_m2 = jnp.where(rows == 0, up6, jnp.where(rows == 1, up7, pltpu.roll(u, 2, 0)))
    yconv = w_ref[0:1, :] * u_m2 + w_ref[1:2, :] * u_m1 + w_ref[2:3, :] * u
    sg = _sig(zc)
    sz = zc * sg
    co = bg * yconv
    return u, u_m1, u_m2, yconv, sg, sz, co


def _fwd_in_conv(h, wt, wconv8, wpc, tm):
    t = h.shape[0]
    seq_tiles = SEQ_LEN // tm

    def body(h_ref, w_ref, wc_ref, wpc_ref, a4_ref, ya_ref, last_u_ref):
        hh = h_ref[...]
        xc, bg, cg, zc = (_dot_nt(hh, w_ref[j * D:(j + 1) * D, :]) for j in range(4))
        for j, z in enumerate((xc, bg, cg, zc)):
            a4_ref[:, j * D:(j + 1) * D] = z.astype(BF16)
        first = pl.program_id(0) % seq_tiles == 0
        up6 = jnp.where(first, 0.0, last_u_ref[6:7, :])
        up7 = jnp.where(first, 0.0, last_u_ref[7:8, :])
        u, _, _, _, _, sz, co = _conv_forward(xc, bg, cg, zc, up6, up7, wc_ref)
        last_u_ref[...] = u[tm - 8:tm, :]
        ya_ref[...] = _dot((sz * co).astype(BF16), wpc_ref[...])

    return pl.pallas_call(
        body, name="fwd_in_conv", grid=(t // tm,),
        in_specs=[_row_spec(tm, D), _whole_vmem(), pl.BlockSpec((8, D), lambda i: (0, 0)), _whole_vmem()],
        out_specs=[_row_spec(tm, 4 * D), _row_spec(tm, D)],
        out_shape=[jax.ShapeDtypeStruct((t, 4 * D), BF16), jax.ShapeDtypeStruct((t, D), F32)],
        scratch_shapes=[pltpu.VMEM((8, D), F32)],
        compiler_params=_params("arbitrary"),
    )(h, wt, wconv8, wpc)


STACK = 4 * BLK


def _band_mask(first):
    qi = lax.broadcasted_iota(jnp.int32, (STACK, 2 * BLK), 0) & (BLK - 1)
    kj = lax.broadcasted_iota(jnp.int32, (STACK, 2 * BLK), 1)
    return (kj > qi) & (kj <= qi + BLK) & (kj >= jnp.where(first, BLK, 0))


def _masked_fill(sink_ref, g, e):
    kj = lax.broadcasted_iota(jnp.int32, (STACK, 2 * BLK), 1)
    sink = jnp.concatenate([jnp.full((BLK, 2 * BLK), sink_ref[0, 2 * (4 * g + jj) + e], F32) for jj in range(4)], axis=0)
    return jnp.where(kj == 0, sink, NEG)


def _padded_pair(before, own):
    z = jnp.concatenate([before, own], axis=0).astype(F32)
    z = jnp.where(lax.broadcasted_iota(jnp.int32, z.shape, 0) == 0, 0.0, z)
    zs = pltpu.roll(z, 64, 1)
    lo = lax.broadcasted_iota(jnp.int32, z.shape, 1) < 64
    zero = jnp.zeros_like(z)
    left = [jnp.where(lo, z, zero).astype(BF16), jnp.where(lo, zs, zero).astype(BF16)]
    right = [jnp.where(lo, zero, zs).astype(BF16), jnp.where(lo, zero, z).astype(BF16)]
    return left, right


def _exp_logits(s, valid, fill):
    s = jnp.where(valid, s, fill)
    m = jnp.max(s, axis=-1, keepdims=True)
    return jnp.exp(s - m), m


def _kv_blocks(kvc_ref, kvp_ref, b, col):
    own = kvc_ref[b * BLK:(b + 1) * BLK, col:col + 128]
    before = kvp_ref[:, col:col + 128] if b == 0 else kvc_ref[(b - 1) * BLK:b * BLK, col:col + 128]
    return before, own


def _fwd_attn(sinks, q, kv, g3, blocks):
    t = q.shape[0]
    tq = blocks * BLK
    seq_blocks = SEQ_LEN // BLK

    def body(sink_ref, q_ref, kvc_ref, kvp_ref, za_ref, attn_ref, ub_ref, lse_ref):
        lo = lax.broadcasted_iota(jnp.int32, (STACK, 128), 1) < 64
        for b in range(blocks):
            rows = slice(b * BLK, (b + 1) * BLK)
            valid = _band_mask((pl.program_id(0) * blocks + b) % seq_blocks == 0)
            k_pad = _padded_pair(*_kv_blocks(kvc_ref, kvp_ref, b, 0))
            v_pad = _padded_pair(*_kv_blocks(kvc_ref, kvp_ref, b, 128))
            for g in range(2):
                qg = jnp.concatenate([q_ref[rows, j * 128:(j + 1) * 128] for j in range(4 * g, 4 * g + 4)], axis=0)
                pv, den = [], []
                for e in range(2):
                    p, m = _exp_logits(_dot_nt(qg, k_pad[e][g]), valid, _masked_fill(sink_ref, g, e))
                    both = _dot(p.astype(BF16), jnp.concatenate([v_pad[e][g], jnp.ones((2 * BLK, 128), BF16)], axis=1))
                    pv.append(both[:, 0:128])
                    den.append(both[:, 128:256])
                    lse_ref[b, 2 * g + e] = m + jnp.log(den[e])
                o = jnp.where(lo, pv[0] / den[0], pv[1] / den[1])
                for jj in range(4):
                    cols = slice((4 * g + jj) * 128, (4 * g + jj + 1) * 128)
                    oj = o[jj * BLK:(jj + 1) * BLK, :]
                    attn_ref[rows, cols] = oj
                    za = za_ref[rows, cols]
                    ub_ref[rows, cols] = (za * _sig(za) * oj).astype(BF16)

    return pl.pallas_call(
        body, name="fwd_attn", grid=(t // tq,),
        in_specs=[pl.BlockSpec(memory_space=pltpu.SMEM), _row_spec(tq, D), _row_spec(tq, 256),
                  pl.BlockSpec((BLK, 256), lambda i: (jnp.maximum(i * blocks - 1, 0), 0)), _row_spec(tq, D, 0)],
        out_specs=[_row_spec(tq, D), _row_spec(tq, D), pl.BlockSpec((blocks, 4, STACK, 128), lambda i: (i, 0, 0, 0))],
        out_shape=[jax.ShapeDtypeStruct((t, D), F32), jax.ShapeDtypeStruct((t, D), BF16),
                   jax.ShapeDtypeStruct((t // BLK, 4, STACK, 128), F32)],
        compiler_params=_params("parallel"),
    )(sinks, q, kv, kv, g3)


def _fwd_out_bwd_head(ya, ub, g3, x, target, g_post, wpa, wout, tm):
    t = x.shape[0]

    def body(ya_ref, ub_ref, ga_ref, gb_ref, x_ref, tgt_ref, gp_ref, wpa_ref, wout_ref,
             loss_ref, dout_ref, dya_ref, dub_ref, dgab_ref, dwout_ref, dwpa_ref, dgp_ref):
        @pl.when(pl.program_id(0) == 0)
        def _():
            loss_ref[...] = jnp.zeros_like(loss_ref)
            dwout_ref[...] = jnp.zeros_like(dwout_ref)
            dwpa_ref[...] = jnp.zeros_like(dwpa_ref)
            dgp_ref[...] = jnp.zeros_like(dgp_ref)

        g = gp_ref[...]
        halves = (slice(0, tm // 2), slice(tm // 2, tm))

        def stage1(rows):
            return _dot(ub_ref[rows, :], wpa_ref[...])

        def stage2(rows, yb):
            sa = _sig(ga_ref[rows, :])
            sb = _sig(gb_ref[rows, :])
            mb = (sa * ya_ref[rows, :] + sb * yb).astype(BF16)
            return sa, sb, mb, _dot(mb, wout_ref[...])

        def stage3(rows, y):
            r = lax.rsqrt(jnp.mean(y * y, axis=-1, keepdims=True) + RMS_EPS)
            n = y * r
            err = (x_ref[rows, :] + n * g) - tgt_ref[rows, :]
            sq = jnp.sum(jnp.sum(err * err, axis=0, keepdims=True), axis=1, keepdims=True)
            dout = err * (1.0 / D)
            dout_ref[rows, :] = dout
            dgp = jnp.sum(dout * n, axis=0, keepdims=True)
            dn = dout * g
            dy = (r * (dn - n * jnp.mean(dn * n, axis=-1, keepdims=True))).astype(BF16)
            return sq, dgp, dy, _dot_nt(dy, wout_ref[...])

        def stage4(rows, dm, sa, sb, yb):
            dya_ref[rows, :] = (dm * sa).astype(BF16)
            dyb = (dm * sb).astype(BF16)
            dgab_ref[rows, 0:D] = (dm * ya_ref[rows, :] * (sa * (1.0 - sa))).astype(BF16)
            dgab_ref[rows, D:2 * D] = (dm * yb * (sb * (1.0 - sb))).astype(BF16)
            dub_ref[rows, :] = _dot_nt(dyb, wpa_ref[...])
            return dyb

        yb = [stage1(rows) for rows in halves]
        s2 = [stage2(rows, yb[k]) for k, rows in enumerate(halves)]
        s3 = [stage3(rows, s2[k][3]) for k, rows in enumerate(halves)]
        dyb = [stage4(rows, s3[k][3], s2[k][0], s2[k][1], yb[k]) for k, rows in enumerate(halves)]
        loss_ref[...] += sum(s[0] for s in s3) * (0.5 / D)
        dgp_ref[0:1, :] += sum(s[1] for s in s3)
        dwout_ref[...] += _dot_tn(jnp.concatenate([s[2] for s in s2], axis=0), jnp.concatenate([s[2] for s in s3], axis=0))
        dwpa_ref[...] += _dot_tn(ub_ref[...], jnp.concatenate(dyb, axis=0))

    return pl.pallas_call(
        body, name="fwd_out_bwd_head", grid=(t // tm,),
        in_specs=[_row_spec(tm, D), _row_spec(tm, D), _row_spec(tm, D, 1), _row_spec(tm, D, 2),
                  _row_spec(tm, D), _row_spec(tm, D), pl.BlockSpec((1, D), lambda i: (0, 0)),
                  _whole_vmem(), _whole_vmem()],
        out_specs=[pl.BlockSpec((8, 128), lambda i: (0, 0)), _row_spec(tm, D), _row_spec(tm, D), _row_spec(tm, D),
                   _row_spec(tm, 2 * D), _whole_vmem(), _whole_vmem(), pl.BlockSpec((8, D), lambda i: (0, 0))],
        out_shape=[jax.ShapeDtypeStruct((8, 128), F32), jax.ShapeDtypeStruct((t, D), F32),
                   jax.ShapeDtypeStruct((t, D), BF16), jax.ShapeDtypeStruct((t, D), F32),
                   jax.ShapeDtypeStruct((t, 2 * D), BF16), jax.ShapeDtypeStruct((D, D), F32),
                   jax.ShapeDtypeStruct((D, D), F32), jax.ShapeDtypeStruct((8, D), F32)],
        compiler_params=_params("arbitrary", vmem_limit_bytes=V7X_VMEM_BYTES - (2 << 20)),
    )(ya, ub, g3, g3, x, target, g_post, wpa, wout)


def _bwd_attn(sinks, q, kv, attn, lse, dub, g3, cos_t, sin_t, wt, blocks):
    t = q.shape[0]
    tq = blocks * BLK
    seq_blocks = SEQ_LEN // BLK

    def body(sink_ref, q_ref, kvc_ref, kvp_ref, attn_ref, lse_ref, dub_ref, za_ref, c_ref, s_ref, w_ref,
             dq_ref, dza_ref, dkv_own_ref, dkv_prev_ref, dsink_ref, dh_ref):
        @pl.when(pl.program_id(0) == 0)
        def _():
            dsink_ref[...] = jnp.zeros_like(dsink_ref)

        lo = lax.broadcasted_iota(jnp.int32, (STACK, 128), 1) < 64
        lane8 = lax.broadcasted_iota(jnp.int32, (8, 128), 1)
        lo2 = lax.broadcasted_iota(jnp.int32, (2 * BLK, 128), 1) < 64
        sink_row = lax.broadcasted_iota(jnp.int32, (2 * BLK, 128), 0) == 0
        dsink = jnp.zeros((8, 128), F32)
        for b in range(blocks):
            rows = slice(b * BLK, (b + 1) * BLK)
            valid = _band_mask((pl.program_id(0) * blocks + b) % seq_blocks == 0)
            k_pad = _padded_pair(*_kv_blocks(kvc_ref, kvp_ref, b, 0))
            v_pad = _padded_pair(*_kv_blocks(kvc_ref, kvp_ref, b, 128))
            c = c_ref[rows, :]
            s = s_ref[rows, :]
            dk_acc, dv_acc = [], []
            for g in range(2):
                qg, dog = [], []
                for j in range(4 * g, 4 * g + 4):
                    cols = slice(j * 128, (j + 1) * 128)
                    za = za_ref[rows, cols]
                    sg = _sig(za)
                    dub = dub_ref[rows, cols]
                    dza_ref[rows, cols] = (dub * attn_ref[rows, cols] * (sg * (1.0 + za * (1.0 - sg)))).astype(BF16)
                    dog.append((dub * (za * sg)).astype(BF16))
                    qg.append(q_ref[rows, cols])
                qg = jnp.concatenate(qg, axis=0)
                dog = jnp.concatenate(dog, axis=0)
                dq = jnp.zeros((STACK, 128), F32)
                ds_both, p_both = [], []
                for e in range(2):
                    s_masked = jnp.where(valid, _dot_nt(qg, k_pad[e][g]), _masked_fill(sink_ref, g, e))
                    lse_rows = lse_ref[b, 2 * g + e]
                    p = jnp.exp(s_masked - jnp.concatenate([lse_rows, lse_rows], axis=1))
                    dp = _dot_nt(dog, v_pad[e][g])
                    ds = p * (dp - jnp.sum(p * dp, axis=-1, keepdims=True))
                    for jj in range(4):
                        tot = jnp.sum(ds[jj * BLK:(jj + 1) * BLK, 0:1], axis=0, keepdims=True)
                        dsink = dsink + jnp.where(lane8 == 2 * (4 * g + jj) + e, tot, 0.0)
                    ds = ds.astype(BF16)
                    dq = dq + _dot(ds, k_pad[e][g])
                    ds_both.append(ds)
                    p_both.append(p.astype(BF16))
                zero = jnp.zeros_like(qg)
                q2 = jnp.concatenate([jnp.where(lo, qg, zero), jnp.where(lo, zero, qg)], axis=0)
                do2 = jnp.concatenate([jnp.where(lo, dog, zero), jnp.where(lo, zero, dog)], axis=0)
                dk_acc.append(_dot_tn(q2, jnp.concatenate(ds_both, axis=0)).T)
                dv_acc.append(_dot_tn(do2, jnp.concatenate(p_both, axis=0)).T)
                for jj in range(4):
                    cols = slice((4 * g + jj) * 128, (4 * g + jj + 1) * 128)
                    dqj = dq[jj * BLK:(jj + 1) * BLK, :] * LOGIT_SCALE
                    dq_ref[rows, cols] = (dqj * c - _swap_halves(dqj) * s).astype(BF16)
            for col, acc in ((0, dk_acc), (128, dv_acc)):
                both = jnp.where(lo2, acc[0] + pltpu.roll(acc[0], 64, 1), acc[1] + pltpu.roll(acc[1], 64, 1))
                both = jnp.where(sink_row, 0.0, both)
                dkv_prev_ref[rows, col:col + 128] = both[0:BLK, :]
                dkv_own_ref[rows, col:col + 128] = both[BLK:2 * BLK, :]
        dsink_ref[...] += dsink
        dh_ref[...] = _dot(dq_ref[...], w_ref[ROW_Q:ROW_KV, :]) + _dot(dza_ref[...], w_ref[ROW_ZA:ROW_GA, :])

    tab = pl.BlockSpec((tq, 128), lambda i: (i % (SEQ_LEN // tq), 0))
    return pl.pallas_call(
        body, name="bwd_attn", grid=(t // tq,),
        in_specs=[pl.BlockSpec(memory_space=pltpu.SMEM), _row_spec(tq, D), _row_spec(tq, 256),
                  pl.BlockSpec((BLK, 256), lambda i: (jnp.maximum(i * blocks - 1, 0), 0)),
                  _row_spec(tq, D), pl.BlockSpec((blocks, 4, STACK, 128), lambda i: (i, 0, 0, 0)),
                  _row_spec(tq, D), _row_spec(tq, D, 0), tab, tab, _whole_vmem()],
        out_specs=[_row_spec(tq, D), _row_spec(tq, D), _row_spec(tq, 256), _row_spec(tq, 256),
                   pl.BlockSpec((8, 128), lambda i: (0, 0)), _row_spec(tq, D)],
        out_shape=[jax.ShapeDtypeStruct((t, D), BF16), jax.ShapeDtypeStruct((t, D), BF16),
                   jax.ShapeDtypeStruct((t, 256), F32), jax.ShapeDtypeStruct((t, 256), F32),
                   jax.ShapeDtypeStruct((8, 128), F32), jax.ShapeDtypeStruct((t, D), F32)],
        compiler_params=_params("arbitrary"),
    )(sinks, q, kv, kv, attn, lse, dub, g3, cos_t, sin_t, wt)


def _bwd_kv_finish(dkv_own, dkv_prev, cos_t, sin_t):
    t = dkv_own.shape[0]
    tm = SEQ_LEN
    seq_tiles = SEQ_LEN // tm
    n_blocks = t // BLK

    def body(own_ref, same_ref, nxt_ref, c_ref, s_ref, out_ref):
        keep = jnp.where(pl.program_id(0) % seq_tiles == seq_tiles - 1, 0.0, 1.0)
        shifted = jnp.concatenate([same_ref[BLK:tm, :], nxt_ref[...] * keep], axis=0)
        tot = own_ref[...] + shifted
        dk = tot[:, 0:128]
        out_ref[:, 0:128] = (dk * c_ref[...] - _swap_halves(dk) * s_ref[...]).astype(BF16)
        out_ref[:, 128:256] = tot[:, 128:256].astype(BF16)

    tab = pl.BlockSpec((tm, 128), lambda i: (i % seq_tiles, 0))
    return pl.pallas_call(
        body, name="bwd_kv_finish", grid=(t // tm,),
        in_specs=[_row_spec(tm, 256), _row_spec(tm, 256),
                  pl.BlockSpec((BLK, 256), lambda i: (jnp.minimum((i + 1) * (tm // BLK), n_blocks - 1), 0)), tab, tab],
        out_specs=_row_spec(tm, 256),
        out_shape=jax.ShapeDtypeStruct((t, 256), BF16),
        compiler_params=_params("parallel"),
    )(dkv_own, dkv_prev, dkv_prev, cos_t, sin_t)


STAGE_ROWS = 256


def _bwd_conv(dya, a4, h, wconv8, wpc, tm, parts):
    t = a4.shape[0]
    n_t = t // tm
    sub = tm // parts
    seq_tiles = SEQ_LEN // tm

    def body(dya_ref, xc_ref, bg_ref, cg_ref, zc_ref, xcp_ref, cgp_ref, w_ref, wpc_ref, h_ref,
             da4_ref, dwpc_ref, dwc_ref, o32_ref, o16_ref, acc_ref, stage_ref, later_ref, sems):
        step = pl.program_id(0)
        tile = n_t - 1 - step

        @pl.when(step == 0)
        def _():
            dwpc_ref[...] = jnp.zeros_like(dwpc_ref)
            dwc_ref[...] = jnp.zeros_like(dwc_ref)
            acc_ref[...] = jnp.zeros_like(acc_ref)

        keep_prev = jnp.where(tile % seq_tiles == 0, 0.0, 1.0)
        ends_sequence = tile % seq_tiles == seq_tiles - 1

        def part(p, later):
            r0 = p * sub
            here = slice(r0, r0 + sub)
            if p == 0:
                u_prev = cgp_ref[14:16, :].astype(F32) * xcp_ref[14:16, :].astype(F32) * keep_prev
            else:
                u_prev = cg_ref[r0 - 2:r0, :].astype(F32) * xc_ref[r0 - 2:r0, :].astype(F32)
            xc = xc_ref[here, :].astype(F32)
            bg = bg_ref[here, :].astype(F32)
            cg = cg_ref[here, :].astype(F32)
            zc = zc_ref[here, :].astype(F32)
            u, u_m1, u_m2, yconv, sg, sz, co = _conv_forward(xc, bg, cg, zc, u_prev[0:1, :], u_prev[1:2, :], w_ref)
            ua = (sz * co).astype(BF16)
            dua = _dot_nt(dya_ref[here, :], wpc_ref[...])
            da4_ref[here, 3 * D:4 * D] = (dua * co * (sg * (1.0 + zc * (1.0 - sg)))).astype(BF16)
            dco = dua * sz
            da4_ref[here, D:2 * D] = (dco * yconv).astype(BF16)
            dyc = dco * bg
            dwc = jnp.concatenate([jnp.sum(dyc * s, axis=0, keepdims=True) for s in (u_m2, u_m1, u)], axis=0)
            rows = lax.broadcasted_iota(jnp.int32, xc.shape, 0)
            n0 = later[0:1, :]
            n1 = later[1:2, :]
            dyc_p1 = jnp.where(rows == sub - 1, n0, pltpu.roll(dyc, sub - 1, 0))
            dyc_p2 = jnp.where(rows == sub - 2, n0, jnp.where(rows == sub - 1, n1, pltpu.roll(dyc, sub - 2, 0)))
            du = w_ref[2:3, :] * dyc + w_ref[1:2, :] * dyc_p1 + w_ref[0:1, :] * dyc_p2
            da4_ref[here, 0:D] = (du * cg).astype(BF16)
            da4_ref[here, 2 * D:3 * D] = (du * xc).astype(BF16)
            return ua, dwc, dyc[0:8, :]

        later = jnp.where(ends_sequence, 0.0, later_ref[...])
        uas, dwc = [], jnp.zeros((3, D), F32)
        for p in reversed(range(parts)):
            ua, dwc_p, later = part(p, later)
            uas.insert(0, ua)
            dwc = dwc + dwc_p
        later_ref[...] = later
        dwpc_ref[...] += _dot_tn(jnp.concatenate(uas, axis=0), dya_ref[...])
        dwc_ref[0:3, :] += dwc
        for j in range(4):
            acc_ref[j * D:(j + 1) * D, :] += _dot_tn(da4_ref[:, j * D:(j + 1) * D], h_ref[...])

        @pl.when(step == n_t - 1)
        def _():
            c32 = pltpu.make_async_copy(acc_ref, o32_ref.at[pl.ds(0, 4 * D)], sems.at[0])
            c32.start()
            for j in range(4 * D // STAGE_ROWS):
                rows = pl.ds(j * STAGE_ROWS, STAGE_ROWS)
                stage_ref[...] = acc_ref[rows, :].astype(BF16)
                c16 = pltpu.make_async_copy(stage_ref, o16_ref.at[rows], sems.at[1])
                c16.start()
                c16.wait()
            c32.wait()

    def rows_of_tile(width, col=0):
        return pl.BlockSpec((tm, width), lambda s: (n_t - 1 - s, col))

    def prev(col):
        return pl.BlockSpec((16, D), lambda s: (jnp.maximum((n_t - 1 - s) * (tm // 16) - 1, 0), col))

    hbm = pl.BlockSpec(memory_space=pl.ANY)
    out = pl.pallas_call(
        body, name="bwd_conv", grid=(n_t,),
        in_specs=[rows_of_tile(D), rows_of_tile(D, 0), rows_of_tile(D, 1), rows_of_tile(D, 2), rows_of_tile(D, 3),
                  prev(0), prev(2), pl.BlockSpec((8, D), lambda s: (0, 0)), _whole_vmem(), rows_of_tile(D)],
        out_specs=[rows_of_tile(4 * D), _whole_vmem(), pl.BlockSpec((8, D), lambda s: (0, 0)), hbm, hbm],
        out_shape=[jax.ShapeDtypeStruct((t, 4 * D), BF16), jax.ShapeDtypeStruct((D, D), F32),
                   jax.ShapeDtypeStruct((8, D), F32), jax.ShapeDtypeStruct((D_IN, D), F32),
                   jax.ShapeDtypeStruct((D_IN, D), BF16)],
        scratch_shapes=[pltpu.VMEM((4 * D, D), F32), pltpu.VMEM((STAGE_ROWS, D), BF16), pltpu.VMEM((8, D), F32),
                        pltpu.SemaphoreType.DMA((2,))],
        compiler_params=pltpu.CompilerParams(dimension_semantics=("arbitrary",), vmem_limit_bytes=V7X_VMEM_BYTES - (2 << 20)),
    )(dya, a4, a4, a4, a4, a4, a4, wconv8, wpc, h)
    return out[0], out[1], out[2], (out[3], out[4])


def _bwd_dh(da4, dh_part, dkv, dgab, wt, x, g_pre, dout, tm):
    t = x.shape[0]

    def body(da4_ref, dhp_ref, dkv_ref, dgab_ref, w_ref, x_ref, g_ref, dout_ref, gx_ref, dg_ref):
        @pl.when(pl.program_id(0) == 0)
        def _():
            dg_ref[...] = jnp.zeros_like(dg_ref)

        dh = dhp_ref[...] + _dot(da4_ref[...], w_ref[0:ROW_Q, :])
        dh += _dot(dkv_ref[...], w_ref[ROW_KV:ROW_ZA, :])
        dh += _dot(dgab_ref[...], w_ref[ROW_GA:D_IN, :])
        xf = x_ref[...]
        r = lax.rsqrt(jnp.mean(xf * xf, axis=-1, keepdims=True) + RMS_EPS)
        xn = xf * r
        dg_ref[0:1, :] += jnp.sum(dh * xn, axis=0, keepdims=True)
        dxn = dh * g_ref[...]
        gx_ref[...] = dout_ref[...] + r * (dxn - xn * jnp.mean(dxn * xn, axis=-1, keepdims=True))

    return pl.pallas_call(
        body, name="bwd_dh", grid=(t // tm,),
        in_specs=[_row_spec(tm, 4 * D), _row_spec(tm, D), _row_spec(tm, 256), _row_spec(tm, 2 * D),
                  _whole_vmem(), _row_spec(tm, D), pl.BlockSpec((1, D), lambda i: (0, 0)), _row_spec(tm, D)],
        out_specs=[_row_spec(tm, D), pl.BlockSpec((8, D), lambda i: (0, 0))],
        out_shape=[jax.ShapeDtypeStruct((t, D), F32), jax.ShapeDtypeStruct((8, D), F32)],
        compiler_params=_params("arbitrary"),
    )(da4, dh_part, dkv, dgab, wt, x, g_pre, dout)


def _bwd_dw_in(h, piece, row0, nb, tm, name, prev):
    t, n = piece.shape
    n_t = t // tm

    def body(*refs):
        h_ref, p_ref = refs[0], refs[1]
        o32_ref, o16_ref, acc_ref, acc16_ref, sems = refs[-5:]
        j, i = pl.program_id(0), pl.program_id(1)

        @pl.when(i == 0)
        def _():
            acc_ref[...] = jnp.zeros_like(acc_ref)

        acc_ref[...] += _dot_tn(p_ref[...], h_ref[...])

        @pl.when(i == n_t - 1)
        def _():
            acc16_ref[...] = acc_ref[...].astype(BF16)
            rows = pl.ds(pl.multiple_of(row0 + j * nb, 16), nb)
            c32 = pltpu.make_async_copy(acc_ref, o32_ref.at[rows], sems.at[0])
            c16 = pltpu.make_async_copy(acc16_ref, o16_ref.at[rows], sems.at[1])
            c32.start()
            c16.start()
            c32.wait()
            c16.wait()

    hbm = pl.BlockSpec(memory_space=pl.ANY)
    carried = [] if prev is None else list(prev)
    return pl.pallas_call(
        body, name=name, grid=(n // nb, n_t),
        in_specs=[pl.BlockSpec((tm, D), lambda j, i: (i, 0)), pl.BlockSpec((tm, nb), lambda j, i: (i, j))]
        + [hbm] * len(carried),
        out_specs=[hbm, hbm],
        out_shape=[jax.ShapeDtypeStruct((D_IN, D), F32), jax.ShapeDtypeStruct((D_IN, D), BF16)],
        scratch_shapes=[pltpu.VMEM((nb, D), F32), pltpu.VMEM((nb, D), BF16), pltpu.SemaphoreType.DMA((2,))],
        input_output_aliases={2: 0, 3: 1} if carried else {},
        compiler_params=_params("arbitrary", "arbitrary"),
    )(h, piece, *carried)


def _place():
    x, y, c = lax.axis_index("x"), lax.axis_index("y"), lax.axis_index("c")
    return x, y, c, 4 * x + 2 * y + c


def _peer(x, y, c, k):
    return (1 - x if k & 4 else x, 1 - y if k & 2 else y, 1 - c if k & 1 else c)


ICI_MASKS = (4, 2, 6)


def _all_gather(shards):
    n = len(shards)

    def body(*refs):
        src, dst = refs[:n], refs[n:2 * n]
        send_sems, recv_sems, local_sems = refs[2 * n:]
        x, y, c, me = _place()
        sibling = _peer(x, y, c, 1)

        def copy(a, s, block, to, own=False):
            return pltpu.make_async_remote_copy(
                src_ref=src[a] if own else dst[a].at[block], dst_ref=dst[a].at[block],
                send_sem=send_sems.at[a * 7 + s], recv_sem=recv_sems.at[a * 7 + s], device_id=to, device_id_type=MESH_ID)

        local = [pltpu.make_async_copy(src[a], dst[a].at[me], local_sems.at[a]) for a in range(n)]
        for cp in local:
            cp.start()
        started = [copy(a, 0, me, sibling, own=True) for a in range(n)]
        started += [copy(a, 1 + j, me, _peer(x, y, c, k), own=True) for j, k in enumerate(ICI_MASKS) for a in range(n)]
        for cp in started:
            cp.start()
        for j, k in enumerate(ICI_MASKS):
            for a in range(n):
                copy(a, 1 + j, me ^ k, sibling).wait_recv()
                fwd = copy(a, 4 + j, me ^ k, sibling)
                fwd.start()
                started.append(fwd)
        for a in range(n):
            copy(a, 0, me ^ 1, sibling).wait_recv()
        for j, k in enumerate(ICI_MASKS):
            for a in range(n):
                copy(a, 4 + j, me ^ 1 ^ k, sibling).wait_recv()
        for cp in started:
            cp.wait_send()
        for cp in local:
            cp.wait()

    hbm = pl.BlockSpec(memory_space=pl.ANY)
    return pl.pallas_call(
        body, name="all_gather_weights",
        in_specs=[hbm] * n, out_specs=[hbm] * n,
        out_shape=[jax.ShapeDtypeStruct((N_DEV,) + s.shape, s.dtype) for s in shards],
        scratch_shapes=[pltpu.SemaphoreType.DMA((7 * n,)), pltpu.SemaphoreType.DMA((7 * n,)),
                        pltpu.SemaphoreType.DMA((n,))],
    )(*shards)


def _direct_copies(src, land, send_sems, recv_sems):
    x, y, c, me = _place()
    return [pltpu.make_async_remote_copy(
        src_ref=src[a], dst_ref=land[a].at[me], send_sem=send_sems.at[a * 7 + k - 1],
        recv_sem=recv_sems.at[a * 7 + k - 1], device_id=_peer(x, y, c, k), device_id_type=MESH_ID)
        for k in range(1, N_DEV) for a in range(len(src))]


def _gather_start(shards, name):
    n = len(shards)

    def body(*refs):
        src, land = refs[:n], refs[n:2 * n]
        send_sems, recv_sems = refs[2 * n], refs[2 * n + 1]
        token_ref = refs[-1]
        for cp in _direct_copies(src, land, send_sems, recv_sems):
            cp.start()
        token_ref[...] = jnp.zeros_like(token_ref)

    hbm = pl.BlockSpec(memory_space=pltpu.HBM)
    sem = pl.BlockSpec(memory_space=pltpu.SEMAPHORE)
    lands = [lax.empty((N_DEV,) + s.shape, s.dtype) for s in shards]
    out = pl.pallas_call(
        body, name=name + "_start",
        out_shape=(pltpu.SemaphoreType.DMA((7 * n,)), pltpu.SemaphoreType.DMA((7 * n,)),
                   *[pltpu.HBM(s.shape, s.dtype) for s in shards], *[pltpu.HBM(s.shape, s.dtype) for s in lands],
                   jax.ShapeDtypeStruct((8, 128), F32)),
        in_specs=[hbm] * (2 * n), out_specs=(sem, sem, *[hbm] * (2 * n), _whole_vmem()),
        input_output_aliases={a: 2 + a for a in range(2 * n)},
        compiler_params=pltpu.CompilerParams(has_side_effects=pltpu.SideEffectType.DATAFLOW_SIDE_EFFECTING),
    )(*[pltpu.with_memory_space_constraint(s, pltpu.HBM) for s in list(shards) + lands])
    return out[0], out[1], out[2:2 + n], out[2 + n:2 + 2 * n], out[-1]


def _gather_wait(send_sems, recv_sems, flying, lands, after, name):
    n = len(flying)

    def body(*refs):
        src, land = refs[:n], refs[n:2 * n]
        for cp in _direct_copies(src, land, refs[2 * n], refs[2 * n + 1]):
            cp.wait_send()
            cp.wait_recv()

    hbm = pl.BlockSpec(memory_space=pltpu.HBM)
    sem = pl.BlockSpec(memory_space=pltpu.SEMAPHORE)
    out = pl.pallas_call(
        body, name=name + "_wait",
        out_shape=tuple(pltpu.HBM(s.shape, s.dtype) for s in list(flying) + list(lands)),
        in_specs=[hbm] * (2 * n) + [sem, sem, pl.BlockSpec(memory_space=pl.ANY)], out_specs=tuple([hbm] * (2 * n)),
        input_output_aliases={a: a for a in range(2 * n)},
        compiler_params=pltpu.CompilerParams(has_side_effects=pltpu.SideEffectType.DATAFLOW_SIDE_EFFECTING),
    )(*flying, *lands, send_sems, recv_sems, after)
    return out[n:]


def _exchange_sibling(by_dest):
    n = len(by_dest)

    def body(*refs):
        src, dst = refs[:n], refs[n:2 * n]
        send_sems, recv_sems = refs[2 * n:]
        x, y, c, _ = _place()
        sibling = _peer(x, y, c, 1)
        copies = [pltpu.make_async_remote_copy(
            src_ref=src[a].at[2 * p + (1 - c)], dst_ref=dst[a].at[p], send_sem=send_sems.at[a * 4 + p],
            recv_sem=recv_sems.at[a * 4 + p], device_id=sibling, device_id_type=MESH_ID)
            for a in range(n) for p in range(4)]
        for cp in copies:
            cp.start()
        for cp in copies:
            cp.wait_recv()
        for cp in copies:
            cp.wait_send()

    hbm = pl.BlockSpec(memory_space=pl.ANY)
    return pl.pallas_call(
        body, name="exchange_sibling", in_specs=[hbm] * n, out_specs=[hbm] * n,
        out_shape=[jax.ShapeDtypeStruct((4,) + s.shape[1:], s.dtype) for s in by_dest],
        scratch_shapes=[pltpu.SemaphoreType.DMA((4 * n,)), pltpu.SemaphoreType.DMA((4 * n,))],
    )(*by_dest)


def _chip_copies(src, land, send_sems, recv_sems):
    x, y, c, _ = _place()
    chip = 2 * x + y
    return [pltpu.make_async_remote_copy(
        src_ref=src[a].at[chip ^ (k >> 1)], dst_ref=land[a].at[j], send_sem=send_sems.at[a * 3 + j],
        recv_sem=recv_sems.at[a * 3 + j], device_id=_peer(x, y, c, k), device_id_type=MESH_ID)
        for j, k in enumerate(ICI_MASKS) for a in range(len(src))]


def _exchange_chips_start(by_chip):
    n = len(by_chip)

    def body(*refs):
        src, land = refs[:n], refs[n:2 * n]
        send_sems, recv_sems = refs[2 * n], refs[2 * n + 1]
        token_ref = refs[-1]
        for cp in _chip_copies(src, land, send_sems, recv_sems):
            cp.start()
        token_ref[...] = jnp.zeros_like(token_ref)

    hbm = pl.BlockSpec(memory_space=pltpu.HBM)
    sem = pl.BlockSpec(memory_space=pltpu.SEMAPHORE)
    lands = [lax.empty((3,) + s.shape[1:], s.dtype) for s in by_chip]
    out = pl.pallas_call(
        body, name="exchange_chips_start",
        out_shape=(pltpu.SemaphoreType.DMA((3 * n,)), pltpu.SemaphoreType.DMA((3 * n,)),
                   *[pltpu.HBM(s.shape, s.dtype) for s in by_chip], *[pltpu.HBM(s.shape, s.dtype) for s in lands],
                   jax.ShapeDtypeStruct((8, 128), F32)),
        in_specs=[hbm] * (2 * n), out_specs=(sem, sem, *[hbm] * (2 * n), _whole_vmem()),
        input_output_aliases={a: 2 + a for a in range(2 * n)},
        compiler_params=pltpu.CompilerParams(has_side_effects=pltpu.SideEffectType.DATAFLOW_SIDE_EFFECTING),
    )(*[pltpu.with_memory_space_constraint(s, pltpu.HBM) for s in list(by_chip) + lands])
    return out[0], out[1], out[2:2 + n], out[2 + n:2 + 2 * n], out[-1]


def _exchange_chips_wait(send_sems, recv_sems, flying, lands, after):
    n = len(flying)

    def body(*refs):
        src, land = refs[:n], refs[n:2 * n]
        send_sems_ref, recv_sems_ref = refs[2 * n], refs[2 * n + 1]
        for cp in _chip_copies(src, land, send_sems_ref, recv_sems_ref):
            cp.wait_send()
            cp.wait_recv()

    hbm = pl.BlockSpec(memory_space=pltpu.HBM)
    sem = pl.BlockSpec(memory_space=pltpu.SEMAPHORE)
    out = pl.pallas_call(
        body, name="exchange_chips_wait",
        out_shape=tuple(pltpu.HBM(s.shape, s.dtype) for s in list(flying) + list(lands)),
        in_specs=[hbm] * (2 * n) + [sem, sem, pl.BlockSpec(memory_space=pl.ANY)], out_specs=tuple([hbm] * (2 * n)),
        input_output_aliases={a: a for a in range(2 * n)},
        compiler_params=pltpu.CompilerParams(has_side_effects=pltpu.SideEffectType.DATAFLOW_SIDE_EFFECTING),
    )(*flying, *lands, send_sems, recv_sems, after)
    return out[n:]


def _adamw_math(w, g, m, v):
    m = ADAM_B1 * m + (1.0 - ADAM_B1) * g
    v = ADAM_B2 * v + (1.0 - ADAM_B2) * (g * g)
    m_hat = m / (1.0 - ADAM_B1 ** ADAM_STEP)
    v_hat = v / (1.0 - ADAM_B2 ** ADAM_STEP)
    return -ADAM_LR * (m_hat / (jnp.sqrt(v_hat) + ADAM_EPS) + ADAM_WD * w), m, v


def _pair_sum(owns, recvs, place_arr, tr, name):
    n = len(owns)
    _, rows, cols = owns[0].shape

    def body(place_ref, *refs):
        for a in range(n):
            s = refs[a][...] + refs[n + a][...].astype(F32)
            refs[3 * n + a][...] = s.astype(BF16)

            @pl.when(pl.program_id(1) == place_ref[1])
            def _(a=a, s=s):
                refs[2 * n + a][...] = s

    by_chip = pl.BlockSpec((None, tr, cols), lambda i, p, place_ref: (p, i, 0))
    mine = pl.BlockSpec((None, tr, cols), lambda i, p, place_ref: (2 * p + place_ref[0], i, 0))
    kept = pl.BlockSpec((tr, cols), lambda i, p, place_ref: (i, 0))
    out = pl.pallas_call(
        body, name=name,
        grid_spec=pltpu.PrefetchScalarGridSpec(
            num_scalar_prefetch=1, grid=(rows // tr, 4), in_specs=[mine] * n + [by_chip] * n,
            out_specs=[kept] * n + [by_chip] * n),
        out_shape=[jax.ShapeDtypeStruct((rows, cols), F32)] * n + [jax.ShapeDtypeStruct((4, rows, cols), BF16)] * n,
        compiler_params=_params("parallel", "arbitrary"),
    )(place_arr, *owns, *recvs)
    return out[:n], out[n:]


def _chip_sum(pairs, recvs, tr, name, adam=None):
    n = len(pairs)
    rows, cols = pairs[0].shape
    n_state = 0 if adam is None else 3 * n

    def body(*refs):
        outs = refs[2 * n + n_state:]
        for a in range(n):
            g = refs[a][...]
            for j in range(3):
                g = g + refs[n + a][j].astype(F32)
            outs[a][...] = g
            if adam is not None:
                w_ref, m_ref, v_ref = (refs[2 * n + s * n + a] for s in range(3))
                outs[n + a][...], outs[2 * n + a][...], outs[3 * n + a][...] = _adamw_math(w_ref[...], g, m_ref[...], v_ref[...])

    blk = pl.BlockSpec((tr, cols), lambda i: (i, 0))
    n_out = n if adam is None else 4 * n
    out = pl.pallas_call(
        body, name=name, grid=(rows // tr,),
        in_specs=[blk] * n + [pl.BlockSpec((3, tr, cols), lambda i: (0, i, 0))] * n + [blk] * n_state,
        out_specs=[blk] * n_out,
        out_shape=[jax.ShapeDtypeStruct((rows, cols), F32)] * n_out,
        compiler_params=_params("parallel"),
    )(*pairs, *recvs, *([] if adam is None else [t for group in adam for t in group]))
    return out if adam is None else (out[:n], out[n:2 * n], out[2 * n:3 * n], out[3 * n:])


def _adamw(ws, gs, ms, vs, name):
    n = len(ws)

    def body(*refs):
        for a in range(n):
            w_ref, g_ref, m_ref, v_ref = (refs[s * n + a] for s in range(4))
            refs[4 * n + a][...], refs[5 * n + a][...], refs[6 * n + a][...] = _adamw_math(
                w_ref[...], g_ref[...], m_ref[...], v_ref[...])

    out = pl.pallas_call(body, name=name, out_shape=[jax.ShapeDtypeStruct(w.shape, F32) for w in ws] * 3)(
        *ws, *gs, *ms, *vs)
    return out[:n], out[n:2 * n], out[2 * n:]


def _sum_small(small_all):
    def body(s_ref, o_ref):
        g = s_ref[0]
        for d in range(1, N_DEV):
            g = g + s_ref[d]
        o_ref[...] = g

    return pl.pallas_call(body, name="sum_small", out_shape=jax.ShapeDtypeStruct(small_all.shape[1:], F32))(small_all)


def _rope_tables():
    inv_freq = ROPE_THETA ** (-jnp.arange(0, HEAD_DIM, 2, dtype=F32) / HEAD_DIM)
    ang = jnp.arange(SEQ_LEN).astype(F32)[:, None] * inv_freq[None, :]
    cos, sin = jnp.cos(ang), jnp.sin(ang)
    return jnp.tile(cos, (1, 4)), jnp.tile(jnp.concatenate([-sin, sin], axis=1), (1, 2))


def _local_step(x, target, g_pre, g_post, sinks, wt, wconv, squares, start_exchange=None):
    cos_t, sin_t = _rope_tables()
    wconv8 = jnp.pad(wconv, ((0, 5), (0, 0)))
    h, q, kv, g3 = _fwd_in_attn(x, g_pre, wt, cos_t, sin_t, 512)
    wpc, wpa, wout = squares(kv)
    a4, ya = _fwd_in_conv(h, wt, wconv8, wpc, 512)
    attn, ub, lse = _fwd_attn(sinks, q, kv, g3, 4)
    loss8, dout, dya, dub, dgab, dwout, dwpa, dgpost8 = _fwd_out_bwd_head(ya, ub, g3, x, target, g_post, wpa, wout, 512)
    dq, dza, dkv_own, dkv_prev, dsink8, dh_part = _bwd_attn(sinks, q, kv, attn, lse, dub, g3, cos_t, sin_t, wt, 4)
    dkv = _bwd_kv_finish(dkv_own, dkv_prev, cos_t, sin_t)
    da4, dwpc, dwconv8, dwt = _bwd_conv(dya, a4, h, wconv8, wpc, 512, 2)
    dwt = _bwd_dw_in(h, dq, ROW_Q, 1024, 1024, "bwd_dw_in_q", dwt)
    dwt = _bwd_dw_in(h, dkv, ROW_KV, 256, 1024, "bwd_dw_in_kv", dwt)
    dwt = _bwd_dw_in(h, dza, ROW_ZA, 1024, 1024, "bwd_dw_in_za", dwt)
    dwt32, dwt16 = _bwd_dw_in(h, dgab, ROW_GA, 1024, 1024, "bwd_dw_in_gates", dwt)
    token, pending = (None, None) if start_exchange is None else start_exchange(dwt32, dwt16, dwpc, dwpa, dwout)
    g_pre_after = g_pre if token is None else g_pre + token[0:1, 0:1]
    grad_x, dgpre8 = _bwd_dh(da4, dh_part, dkv, dgab, wt, x, g_pre_after, dout, 512)
    small = jnp.concatenate([dgpre8, dgpost8, jnp.pad(dsink8, ((0, 0), (0, D - 128))), dwconv8,
                             jnp.pad(loss8, ((0, 0), (0, D - 128)))], axis=0)
    return loss8[0, 0], grad_x, dwt32, dwt16, dwpc, dwpa, dwout, small, pending


def kernel(x, g_pre, g_post, w_in, w_conv, sinks, w_proj_conv, w_proj_attn, w_out, loss_target, m_g_pre, m_g_post, m_w_in, m_w_conv, m_sinks, m_w_proj_conv, m_w_proj_attn, m_w_out, v_g_pre, v_g_post, v_w_in, v_w_conv, v_sinks, v_w_proj_conv, v_w_proj_attn, v_w_out):
    batch = x.shape[0]
    mx, my, mc, me = _place()
    place_arr = jnp.stack([mc, 2 * mx + my]).astype(jnp.int32)

    g_wt, g_conv = _all_gather([w_in[0].T.astype(BF16), jnp.pad(w_conv[0], ((0, 5), (0, 0)))])
    wt = g_wt.reshape(D_IN, D)
    wconv = g_conv[:, 0:3, :].transpose(1, 0, 2).reshape(3, D)
    sq_mine = [w.astype(BF16) for w in (w_proj_conv[0], w_proj_attn[0], w_out[0])]
    wt, sq_mine = lax.optimization_barrier((wt, sq_mine))
    sq_send, sq_recv, sq_flying, sq_lands, sq_token = _gather_start(sq_mine, "gather_squares")

    def squares(after):
        got = _gather_wait(sq_send, sq_recv, sq_flying, sq_lands, after, "gather_squares")
        return [lax.dynamic_update_index_in_dim(full, mine, me, 0).reshape(D, D) for full, mine in zip(got, sq_mine)]

    def start_exchange(dwt32, dwt16, dwpc, dwpa, dwout):
        own_sq = [g.reshape(N_DEV, SHARD_SQ, D) for g in (dwpc, dwpa, dwout)]
        own_in = dwt32.reshape(N_DEV, SHARD_IN, D)
        from_sibling = _exchange_sibling([dwt16.reshape(N_DEV, SHARD_IN, D)] + [g.astype(BF16) for g in own_sq])
        in32, in16 = _pair_sum([own_in], from_sibling[:1], place_arr, SHARD_IN // 2, "pair_sum_w_in")
        sq32, sq16 = _pair_sum(own_sq, from_sibling[1:], place_arr, SHARD_SQ, "pair_sum_squares")
        send_sems, recv_sems, flying, lands, token = _exchange_chips_start(list(in16) + list(sq16))
        return token, (send_sems, recv_sems, flying, lands, in32, sq32)

    _, grad_x, _, _, _, _, _, small, pending = _local_step(
        x.reshape(batch * SEQ_LEN, D), loss_target.reshape(batch * SEQ_LEN, D), g_pre + sq_token[0:1, 0:1], g_post,
        sinks, wt, wconv, squares, start_exchange)
    sm_send, sm_recv, sm_flying, sm_lands, sm_token = _gather_start([small], "gather_small")
    send_sems, recv_sems, flying, lands, in32, sq32 = pending
    from_chips = _exchange_chips_wait(send_sems, recv_sems, flying, lands, sm_token)

    o_in = [o[0].T for o in _chip_sum(
        in32, from_chips[:1], SHARD_IN // 3, "chip_sum_adamw_w_in",
        adam=([w_in[0].T], [m_w_in[0].T], [v_w_in[0].T]))]
    g_in_mine, o_in = o_in[0], o_in[1:]
    g_sq, d_sq, m_sq, v_sq = _chip_sum(
        sq32, from_chips[1:], SHARD_SQ, "chip_sum_adamw_squares",
        adam=([w_proj_conv[0], w_proj_attn[0], w_out[0]], [m_w_proj_conv[0], m_w_proj_attn[0], m_w_out[0]],
              [v_w_proj_conv[0], v_w_proj_attn[0], v_w_out[0]]))
    both_done, g_in_mine = lax.optimization_barrier((d_sq[0], g_in_mine))
    (small_all,) = _gather_wait(sm_send, sm_recv, sm_flying, sm_lands, both_done, "gather_small")
    gs = _sum_small(lax.dynamic_update_index_in_dim(small_all, small, me, 0))
    g_g_pre, g_g_post, g_sinks, loss = gs[0:1], gs[8:9], gs[16:17, 0:N_HEADS], gs[32, 0]
    g_conv_mine = lax.dynamic_slice_in_dim(gs[24:27], me * SHARD_SQ, SHARD_SQ, axis=1)
    o_small = _adamw([g_pre, g_post, sinks, w_conv[0]], [g_g_pre, g_g_post, g_sinks, g_conv_mine],
                     [m_g_pre, m_g_post, m_sinks, m_w_conv[0]], [v_g_pre, v_g_post, v_sinks, v_w_conv[0]], "adamw_small")

    grads = [g_g_pre, g_g_post, g_in_mine[None], g_conv_mine[None], g_sinks] + [g[None] for g in g_sq]
    rest = []
    for idx, sq in enumerate((d_sq, m_sq, v_sq)):
        gp, gq, sk, cv = o_small[idx]
        rest += [gp, gq, o_in[idx][None], cv[None], sk] + [s[None] for s in sq]
    return (loss, grad_x.reshape(batch, SEQ_LEN, D), *grads, *rest)
```

```python
import jax
import jax.numpy as jnp
from jax import lax
from jax.experimental import pallas as pl
from jax.experimental.pallas import tpu as pltpu

D = 1024
N_HEADS = 16
HEAD_DIM = 64
LOGIT_SCALE = HEAD_DIM ** -0.5
BLK = 128
SEQ_LEN = 2048
D_IN = 8448
ROW_Q, ROW_KV, ROW_ZA, ROW_GA = 4 * D, 5 * D, 5 * D + 256, 6 * D + 256
SHARD_IN = D_IN // 8
SHARD_SQ = D // 8
N_DEV = 8
V7X_VMEM_BYTES = 64 << 20
ROPE_THETA = 10000.0
RMS_EPS = 1e-6
NEG = -1e30
ADAM_LR, ADAM_B1, ADAM_B2, ADAM_EPS, ADAM_WD, ADAM_STEP = 0.001, 0.9, 0.999, 1e-08, 0.01, 10

F32 = jnp.float32
BF16 = jnp.bfloat16
MESH_ID = pl.DeviceIdType.MESH


def _dot(a, b):
    return jnp.dot(a, b, preferred_element_type=F32)


def _dot_nt(a, b):
    return lax.dot_general(a, b, (((1,), (1,)), ((), ())), preferred_element_type=F32)


def _dot_tn(a, b):
    return lax.dot_general(a, b, (((0,), (0,)), ((), ())), preferred_element_type=F32)


def _sig(z):
    return 1.0 / (1.0 + jnp.exp(-z))


def _swap_halves(z):
    lane = lax.broadcasted_iota(jnp.int32, z.shape, 1)
    return jnp.where((lane & 63) < 32, pltpu.roll(z, 96, 1), pltpu.roll(z, 32, 1))


def _row_spec(tm, width, col=0):
    return pl.BlockSpec((tm, width), lambda i: (i, col))


def _whole_vmem():
    return pl.BlockSpec(memory_space=pltpu.VMEM)


def _params(*sem, vmem_limit_bytes=None):
    return pltpu.CompilerParams(dimension_semantics=sem, vmem_limit_bytes=vmem_limit_bytes)


def _fwd_in_attn(x, g_pre, wt, cos_t, sin_t, tm):
    t = x.shape[0]
    seq_tiles = SEQ_LEN // tm

    def body(x_ref, g_ref, w_ref, c_ref, s_ref, h_ref, q_ref, kv_ref, g3_ref):
        xf = x_ref[...]
        r = lax.rsqrt(jnp.mean(xf * xf, axis=-1, keepdims=True) + RMS_EPS)
        hh = ((xf * r) * g_ref[...]).astype(BF16)
        h_ref[...] = hh
        c = c_ref[...]
        s = s_ref[...]

        def rope(z):
            return z * c + _swap_halves(z) * s

        q = _dot_nt(hh, w_ref[ROW_Q:ROW_Q + D, :])
        for j in range(D // 128):
            q_ref[:, j * 128:(j + 1) * 128] = (rope(q[:, j * 128:(j + 1) * 128]) * LOGIT_SCALE).astype(BF16)
        kv = _dot_nt(hh, w_ref[ROW_KV:ROW_KV + 256, :])
        kv_ref[:, 0:128] = rope(kv[:, 0:128]).astype(BF16)
        kv_ref[:, 128:256] = kv[:, 128:256].astype(BF16)
        for j in range(3):
            g3_ref[:, j * D:(j + 1) * D] = _dot_nt(hh, w_ref[ROW_ZA + j * D:ROW_ZA + (j + 1) * D, :])

    tab = pl.BlockSpec((tm, 128), lambda i: (i % seq_tiles, 0))
    return pl.pallas_call(
        body, name="fwd_in_attn", grid=(t // tm,),
        in_specs=[_row_spec(tm, D), pl.BlockSpec((1, D), lambda i: (0, 0)), _whole_vmem(), tab, tab],
        out_specs=[_row_spec(tm, D), _row_spec(tm, D), _row_spec(tm, 256), _row_spec(tm, 3 * D)],
        out_shape=[jax.ShapeDtypeStruct((t, D), BF16), jax.ShapeDtypeStruct((t, D), BF16),
                   jax.ShapeDtypeStruct((t, 256), BF16), jax.ShapeDtypeStruct((t, 3 * D), F32)],
        compiler_params=_params("parallel"),
    )(x, g_pre, wt, cos_t, sin_t)


def _conv_forward(xc, bg, cg, zc, up6, up7, w_ref):
    rows = lax.broadcasted_iota(jnp.int32, xc.shape, 0)
    u = cg * xc
    u_m1 = jnp.where(rows == 0, up7, pltpu.roll(u, 1, 0))
    u_m2 = jnp.where(rows == 0, up6, jnp.where(rows == 1, up7, pltpu.roll(u, 2, 0)))
    yconv = w_ref[0:1, :] * u_m2 + w_ref[1:2, :] * u_m1 + w_ref[2:3, :] * u
    sg = _sig(zc)
    sz = zc * sg
    co = bg * yconv
    return u, u_m1, u_m2, yconv, sg, sz, co


def _fwd_in_conv(h, wt, wconv8, wpc, tm):
    t = h.shape[0]
    seq_tiles = SEQ_LEN // tm

    def body(h_ref, w_ref, wc_ref, wpc_ref, a4_ref, ya_ref, last_u_ref):
        hh = h_ref[...]
        xc, bg, cg, zc = (_dot_nt(hh, w_ref[j * D:(j + 1) * D, :]) for j in range(4))
        for j, z in enumerate((xc, bg, cg, zc)):
            a4_ref[:, j * D:(j + 1) * D] = z.astype(BF16)
        first = pl.program_id(0) % seq_tiles == 0
        up6 = jnp.where(first, 0.0, last_u_ref[6:7, :])
        up7 = jnp.where(first, 0.0, last_u_ref[7:8, :])
        u, _, _, _, _, sz, co = _conv_forward(xc, bg, cg, zc, up6, up7, wc_ref)
        last_u_ref[...] = u[tm - 8:tm, :]
        ya_ref[...] = _dot((sz * co).astype(BF16), wpc_ref[...])

    return pl.pallas_call(
        body, name="fwd_in_conv", grid=(t // tm,),
        in_specs=[_row_spec(tm, D), _whole_vmem(), pl.BlockSpec((8, D), lambda i: (0, 0)), _whole_vmem()],
        out_specs=[_row_spec(tm, 4 * D), _row_spec(tm, D)],
        out_shape=[jax.ShapeDtypeStruct((t, 4 * D), BF16), jax.ShapeDtypeStruct((t, D), F32)],
        scratch_shapes=[pltpu.VMEM((8, D), F32)],
        compiler_params=_params("arbitrary"),
    )(h, wt, wconv8, wpc)


STACK = 4 * BLK


def _band_mask(first):
    qi = lax.broadcasted_iota(jnp.int32, (STACK, 2 * BLK), 0) & (BLK - 1)
    kj = lax.broadcasted_iota(jnp.int32, (STACK, 2 * BLK), 1)
    return (kj > qi) & (kj <= qi + BLK) & (kj >= jnp.where(first, BLK, 0))


def _masked_fill(sink_ref, g, e):
    kj = lax.broadcasted_iota(jnp.int32, (STACK, 2 * BLK), 1)
    sink = jnp.concatenate([jnp.full((BLK, 2 * BLK), sink_ref[0, 2 * (4 * g + jj) + e], F32) for jj in range(4)], axis=0)
    return jnp.where(kj == 0, sink, NEG)


def _padded_pair(before, own):
    z = jnp.concatenate([before, own], axis=0).astype(F32)
    z = jnp.where(lax.broadcasted_iota(jnp.int32, z.shape, 0) == 0, 0.0, z)
    zs = pltpu.roll(z, 64, 1)
    lo = lax.broadcasted_iota(jnp.int32, z.shape, 1) < 64
    zero = jnp.zeros_like(z)
    left = [jnp.where(lo, z, zero).astype(BF16), jnp.where(lo, zs, zero).astype(BF16)]
    right = [jnp.where(lo, zero, zs).astype(BF16), jnp.where(lo, zero, z).astype(BF16)]
    return left, right


def _exp_logits(s, valid, fill):
    s = jnp.where(valid, s, fill)
    m = jnp.max(s, axis=-1, keepdims=True)
    return jnp.exp(s - m), m


def _kv_blocks(kvc_ref, kvp_ref, b, col):
    own = kvc_ref[b * BLK:(b + 1) * BLK, col:col + 128]
    before = kvp_ref[:, col:col + 128] if b == 0 else kvc_ref[(b - 1) * BLK:b * BLK, col:col + 128]
    return before, own


def _fwd_attn(sinks, q, kv, g3, blocks):
    t = q.shape[0]
    tq = blocks * BLK
    seq_blocks = SEQ_LEN // BLK

    def body(sink_ref, q_ref, kvc_ref, kvp_ref, za_ref, attn_ref, ub_ref, lse_ref):
        lo = lax.broadcasted_iota(jnp.int32, (STACK, 128), 1) < 64
        for b in range(blocks):
            rows = slice(b * BLK, (b + 1) * BLK)
            valid = _band_mask((pl.program_id(0) * blocks + b) % seq_blocks == 0)
            k_pad = _padded_pair(*_kv_blocks(kvc_ref, kvp_ref, b, 0))
            v_pad = _padded_pair(*_kv_blocks(kvc_ref, kvp_ref, b, 128))
            for g in range(2):
                qg = jnp.concatenate([q_ref[rows, j * 128:(j + 1) * 128] for j in range(4 * g, 4 * g + 4)], axis=0)
                pv, den = [], []
                for e in range(2):
                    p, m = _exp_logits(_dot_nt(qg, k_pad[e][g]), valid, _masked_fill(sink_ref, g, e))
                    both = _dot(p.astype(BF16), jnp.concatenate([v_pad[e][g], jnp.ones((2 * BLK, 128), BF16)], axis=1))
                    pv.append(both[:, 0:128])
                    den.append(both[:, 128:256])
                    lse_ref[b, 2 * g + e] = m + jnp.log(den[e])
                o = jnp.where(lo, pv[0] / den[0], pv[1] / den[1])
                for jj in range(4):
                    cols = slice((4 * g + jj) * 128, (4 * g + jj + 1) * 128)
                    oj = o[jj * BLK:(jj + 1) * BLK, :]
                    attn_ref[rows, cols] = oj
                    za = za_ref[rows, cols]
                    ub_ref[rows, cols] = (za * _sig(za) * oj).astype(BF16)

    return pl.pallas_call(
        body, name="fwd_attn", grid=(t // tq,),
        in_specs=[pl.BlockSpec(memory_space=pltpu.SMEM), _row_spec(tq, D), _row_spec(tq, 256),
                  pl.BlockSpec((BLK, 256), lambda i: (jnp.maximum(i * blocks - 1, 0), 0)), _row_spec(tq, D, 0)],
        out_specs=[_row_spec(tq, D), _row_spec(tq, D), pl.BlockSpec((blocks, 4, STACK, 128), lambda i: (i, 0, 0, 0))],
        out_shape=[jax.ShapeDtypeStruct((t, D), F32), jax.ShapeDtypeStruct((t, D), BF16),
                   jax.ShapeDtypeStruct((t // BLK, 4, STACK, 128), F32)],
        compiler_params=_params("parallel"),
    )(sinks, q, kv, kv, g3)


def _fwd_out_bwd_head(ya, ub, g3, x, target, g_post, wpa, wout, tm):
    t = x.shape[0]

    def body(ya_ref, ub_ref, ga_ref, gb_ref, x_ref, tgt_ref, gp_ref, wpa_ref, wout_ref,
             loss_ref, dout_ref, dya_ref, dub_ref, dgab_ref, dwout_ref, dwpa_ref, dgp_ref):
        @pl.when(pl.program_id(0) == 0)
        def _():
            loss_ref[...] = jnp.zeros_like(loss_ref)
            dwout_ref[...] = jnp.zeros_like(dwout_ref)
            dwpa_ref[...] = jnp.zeros_like(dwpa_ref)
            dgp_ref[...] = jnp.zeros_like(dgp_ref)

        g = gp_ref[...]
        halves = (slice(0, tm // 2), slice(tm // 2, tm))

        def stage1(rows):
            return _dot(ub_ref[rows, :], wpa_ref[...])

        def stage2(rows, yb):
            sa = _sig(ga_ref[rows, :])
            sb = _sig(gb_ref[rows, :])
            mb = (sa * ya_ref[rows, :] + sb * yb).astype(BF16)
            return sa, sb, mb, _dot(mb, wout_ref[...])

        def stage3(rows, y):
            r = lax.rsqrt(jnp.mean(y * y, axis=-1, keepdims=True) + RMS_EPS)
            n = y * r
            err = (x_ref[rows, :] + n * g) - tgt_ref[rows, :]
            sq = jnp.sum(jnp.sum(err * err, axis=0, keepdims=True), axis=1, keepdims=True)
            dout = err * (1.0 / D)
            dout_ref[rows, :] = dout
            dgp = jnp.sum(dout * n, axis=0, keepdims=True)
            dn = dout * g
            dy = (r * (dn - n * jnp.mean(dn * n, axis=-1, keepdims=True))).astype(BF16)
            return sq, dgp, dy, _dot_nt(dy, wout_ref[...])

        def stage4(rows, dm, sa, sb, yb):
            dya_ref[rows, :] = (dm * sa).astype(BF16)
            dyb = (dm * sb).astype(BF16)
            dgab_ref[rows, 0:D] = (dm * ya_ref[rows, :] * (sa * (1.0 - sa))).astype(BF16)
            dgab_ref[rows, D:2 * D] = (dm * yb * (sb * (1.0 - sb))).astype(BF16)
            dub_ref[rows, :] = _dot_nt(dyb, wpa_ref[...])
            return dyb

        yb = [stage1(rows) for rows in halves]
        s2 = [stage2(rows, yb[k]) for k, rows in enumerate(halves)]
        s3 = [stage3(rows, s2[k][3]) for k, rows in enumerate(halves)]
        dyb = [stage4(rows, s3[k][3], s2[k][0], s2[k][1], yb[k]) for k, rows in enumerate(halves)]
        loss_ref[...] += sum(s[0] for s in s3) * (0.5 / D)
        dgp_ref[0:1, :] += sum(s[1] for s in s3)
        dwout_ref[...] += _dot_tn(jnp.concatenate([s[2] for s in s2], axis=0), jnp.concatenate([s[2] for s in s3], axis=0))
        dwpa_ref[...] += _dot_tn(ub_ref[...], jnp.concatenate(dyb, axis=0))

    return pl.pallas_call(
        body, name="fwd_out_bwd_head", grid=(t // tm,),
        in_specs=[_row_spec(tm, D), _row_spec(tm, D), _row_spec(tm, D, 1), _row_spec(tm, D, 2),
                  _row_spec(tm, D), _row_spec(tm, D), pl.BlockSpec((1, D), lambda i: (0, 0)),
                  _whole_vmem(), _whole_vmem()],
        out_specs=[pl.BlockSpec((8, 128), lambda i: (0, 0)), _row_spec(tm, D), _row_spec(tm, D), _row_spec(tm, D),
                   _row_spec(tm, 2 * D), _whole_vmem(), _whole_vmem(), pl.BlockSpec((8, D), lambda i: (0, 0))],
        out_shape=[jax.ShapeDtypeStruct((8, 128), F32), jax.ShapeDtypeStruct((t, D), F32),
                   jax.ShapeDtypeStruct((t, D), BF16), jax.ShapeDtypeStruct((t, D), F32),
                   jax.ShapeDtypeStruct((t, 2 * D), BF16), jax.ShapeDtypeStruct((D, D), F32),
                   jax.ShapeDtypeStruct((D, D), F32), jax.ShapeDtypeStruct((8, D), F32)],
        compiler_params=_params("arbitrary", vmem_limit_bytes=V7X_VMEM_BYTES - (2 << 20)),
    )(ya, ub, g3, g3, x, target, g_post, wpa, wout)


def _bwd_attn(sinks, q, kv, attn, lse, dub, g3, cos_t, sin_t, wt, blocks):
    t = q.shape[0]
    tq = blocks * BLK
    seq_blocks = SEQ_LEN // BLK

    def body(sink_ref, q_ref, kvc_ref, kvp_ref, attn_ref, lse_ref, dub_ref, za_ref, c_ref, s_ref, w_ref,
             dq_ref, dza_ref, dkv_own_ref, dkv_prev_ref, dsink_ref, dh_ref):
        @pl.when(pl.program_id(0) == 0)
        def _():
            dsink_ref[...] = jnp.zeros_like(dsink_ref)

        lo = lax.broadcasted_iota(jnp.int32, (STACK, 128), 1) < 64
        lane8 = lax.broadcasted_iota(jnp.int32, (8, 128), 1)
        lo2 = lax.broadcasted_iota(jnp.int32, (2 * BLK, 128), 1) < 64
        sink_row = lax.broadcasted_iota(jnp.int32, (2 * BLK, 128), 0) == 0
        dsink = jnp.zeros((8, 128), F32)
        for b in range(blocks):
            rows = slice(b * BLK, (b + 1) * BLK)
            valid = _band_mask((pl.program_id(0) * blocks + b) % seq_blocks == 0)
            k_pad = _padded_pair(*_kv_blocks(kvc_ref, kvp_ref, b, 0))
            v_pad = _padded_pair(*_kv_blocks(kvc_ref, kvp_ref, b, 128))
            c = c_ref[rows, :]
            s = s_ref[rows, :]
            dk_acc, dv_acc = [], []
            for g in range(2):
                qg, dog = [], []
                for j in range(4 * g, 4 * g + 4):
                    cols = slice(j * 128, (j + 1) * 128)
                    za = za_ref[rows, cols]
                    sg = _sig(za)
                    dub = dub_ref[rows, cols]
                    dza_ref[rows, cols] = (dub * attn_ref[rows, cols] * (sg * (1.0 + za * (1.0 - sg)))).astype(BF16)
                    dog.append((dub * (za * sg)).astype(BF16))
                    qg.append(q_ref[rows, cols])
                qg = jnp.concatenate(qg, axis=0)
                dog = jnp.concatenate(dog, axis=0)
                dq = jnp.zeros((STACK, 128), F32)
                ds_both, p_both = [], []
                for e in range(2):
                    s_masked = jnp.where(valid, _dot_nt(qg, k_pad[e][g]), _masked_fill(sink_ref, g, e))
                    lse_rows = lse_ref[b, 2 * g + e]
                    p = jnp.exp(s_masked - jnp.concatenate([lse_rows, lse_rows], axis=1))
                    dp = _dot_nt(dog, v_pad[e][g])
                    ds = p * (dp - jnp.sum(p * dp, axis=-1, keepdims=True))
                    for jj in range(4):
                        tot = jnp.sum(ds[jj * BLK:(jj + 1) * BLK, 0:1], axis=0, keepdims=True)
                        dsink = dsink + jnp.where(lane8 == 2 * (4 * g + jj) + e, tot, 0.0)
                    ds = ds.astype(BF16)
                    dq = dq + _dot(ds, k_pad[e][g])
                    ds_both.append(ds)
                    p_both.append(p.astype(BF16))
                zero = jnp.zeros_like(qg)
                q2 = jnp.concatenate([jnp.where(lo, qg, zero), jnp.where(lo, zero, qg)], axis=0)
                do2 = jnp.concatenate([jnp.where(lo, dog, zero), jnp.where(lo, zero, dog)], axis=0)
                dk_acc.append(_dot_tn(q2, jnp.concatenate(ds_both, axis=0)).T)
                dv_acc.append(_dot_tn(do2, jnp.concatenate(p_both, axis=0)).T)
                for jj in range(4):
                    cols = slice((4 * g + jj) * 128, (4 * g + jj + 1) * 128)
                    dqj = dq[jj * BLK:(jj + 1) * BLK, :] * LOGIT_SCALE
                    dq_ref[rows, cols] = (dqj * c - _swap_halves(dqj) * s).astype(BF16)
            for col, acc in ((0, dk_acc), (128, dv_acc)):
                both = jnp.where(lo2, acc[0] + pltpu.roll(acc[0], 64, 1), acc[1] + pltpu.roll(acc[1], 64, 1))
                both = jnp.where(sink_row, 0.0, both)
                dkv_prev_ref[rows, col:col + 128] = both[0:BLK, :]
                dkv_own_ref[rows, col:col + 128] = both[BLK:2 * BLK, :]
        dsink_ref[...] += dsink
        dh_ref[...] = _dot(dq_ref[...], w_ref[ROW_Q:ROW_KV, :]) + _dot(dza_ref[...], w_ref[ROW_ZA:ROW_GA, :])

    tab = pl.BlockSpec((tq, 128), lambda i: (i % (SEQ_LEN // tq), 0))
    return pl.pallas_call(
        body, name="bwd_attn", grid=(t // tq,),
        in_specs=[pl.BlockSpec(memory_space=pltpu.SMEM), _row_spec(tq, D), _row_spec(tq, 256),
                  pl.BlockSpec((BLK, 256), lambda i: (jnp.maximum(i * blocks - 1, 0), 0)),
                  _row_spec(tq, D), pl.BlockSpec((blocks, 4, STACK, 128), lambda i: (i, 0, 0, 0)),
                  _row_spec(tq, D), _row_spec(tq, D, 0), tab, tab, _whole_vmem()],
        out_specs=[_row_spec(tq, D), _row_spec(tq, D), _row_spec(tq, 256), _row_spec(tq, 256),
                   pl.BlockSpec((8, 128), lambda i: (0, 0)), _row_spec(tq, D)],
        out_shape=[jax.ShapeDtypeStruct((t, D), BF16), jax.ShapeDtypeStruct((t, D), BF16),
                   jax.ShapeDtypeStruct((t, 256), F32), jax.ShapeDtypeStruct((t, 256), F32),
                   jax.ShapeDtypeStruct((8, 128), F32), jax.ShapeDtypeStruct((t, D), F32)],
        compiler_params=_params("arbitrary"),
    )(sinks, q, kv, kv, attn, lse, dub, g3, cos_t, sin_t, wt)


def _bwd_kv_finish(dkv_own, dkv_prev, cos_t, sin_t, h, prev):
    t = dkv_own.shape[0]
    tm = SEQ_LEN
    n_t = t // tm
    seq_tiles = SEQ_LEN // tm
    n_blocks = t // BLK

    def body(own_ref, same_ref, nxt_ref, c_ref, s_ref, h_ref, o32_in, o16_in, out_ref, o32_ref, o16_ref,
             acc_ref, acc16_ref, sems):
        step = pl.program_id(0)

        @pl.when(step == 0)
        def _():
            acc_ref[...] = jnp.zeros_like(acc_ref)

        keep = jnp.where(step % seq_tiles == seq_tiles - 1, 0.0, 1.0)
        shifted = jnp.concatenate([same_ref[BLK:tm, :], nxt_ref[...] * keep], axis=0)
        tot = own_ref[...] + shifted
        dk = tot[:, 0:128]
        out_ref[:, 0:128] = (dk * c_ref[...] - _swap_halves(dk) * s_ref[...]).astype(BF16)
        out_ref[:, 128:256] = tot[:, 128:256].astype(BF16)
        acc_ref[...] += _dot_tn(out_ref[...], h_ref[...])

        @pl.when(step == n_t - 1)
        def _():
            acc16_ref[...] = acc_ref[...].astype(BF16)
            rows = pl.ds(ROW_KV, 256)
            c32 = pltpu.make_async_copy(acc_ref, o32_ref.at[rows], sems.at[0])
            c16 = pltpu.make_async_copy(acc16_ref, o16_ref.at[rows], sems.at[1])
            c32.start()
            c16.start()
            c32.wait()
            c16.wait()

    tab = pl.BlockSpec((tm, 128), lambda i: (i % seq_tiles, 0))
    hbm = pl.BlockSpec(memory_space=pl.ANY)
    out = pl.pallas_call(
        body, name="bwd_kv_finish", grid=(n_t,),
        in_specs=[_row_spec(tm, 256), _row_spec(tm, 256),
                  pl.BlockSpec((BLK, 256), lambda i: (jnp.minimum((i + 1) * (tm // BLK), n_blocks - 1), 0)), tab, tab,
                  _row_spec(tm, D), hbm, hbm],
        out_specs=[_row_spec(tm, 256), hbm, hbm],
        out_shape=[jax.ShapeDtypeStruct((t, 256), BF16), jax.ShapeDtypeStruct((D_IN, D), F32),
                   jax.ShapeDtypeStruct((D_IN, D), BF16)],
        scratch_shapes=[pltpu.VMEM((256, D), F32), pltpu.VMEM((256, D), BF16), pltpu.SemaphoreType.DMA((2,))],
        input_output_aliases={6: 1, 7: 2},
        compiler_params=_params("arbitrary"),
    )(dkv_own, dkv_prev, dkv_prev, cos_t, sin_t, h, *prev)
    return out[0], (out[1], out[2])


STAGE_ROWS = 256


def _bwd_conv(dya, a4, h, wconv8, wpc, tm, parts):
    t = a4.shape[0]
    n_t = t // tm
    sub = tm // parts
    seq_tiles = SEQ_LEN // tm

    def body(dya_ref, xc_ref, bg_ref, cg_ref, zc_ref, xcp_ref, cgp_ref, w_ref, wpc_ref, h_ref,
             da4_ref, dwpc_ref, dwc_ref, o32_ref, o16_ref, acc_ref, stage_ref, later_ref, sems):
        step = pl.program_id(0)
        tile = n_t - 1 - step

        @pl.when(step == 0)
        def _():
            dwpc_ref[...] = jnp.zeros_like(dwpc_ref)
            dwc_ref[...] = jnp.zeros_like(dwc_ref)
            acc_ref[...] = jnp.zeros_like(acc_ref)

        keep_prev = jnp.where(tile % seq_tiles == 0, 0.0, 1.0)
        ends_sequence = tile % seq_tiles == seq_tiles - 1

        def part(p, later):
            r0 = p * sub
            here = slice(r0, r0 + sub)
            if p == 0:
                u_prev = cgp_ref[14:16, :].astype(F32) * xcp_ref[14:16, :].astype(F32) * keep_prev
            else:
                u_prev = cg_ref[r0 - 2:r0, :].astype(F32) * xc_ref[r0 - 2:r0, :].astype(F32)
            xc = xc_ref[here, :].astype(F32)
            bg = bg_ref[here, :].astype(F32)
            cg = cg_ref[here, :].astype(F32)
            zc = zc_ref[here, :].astype(F32)
            u, u_m1, u_m2, yconv, sg, sz, co = _conv_forward(xc, bg, cg, zc, u_prev[0:1, :], u_prev[1:2, :], w_ref)
            ua = (sz * co).astype(BF16)
            dua = _dot_nt(dya_ref[here, :], wpc_ref[...])
            da4_ref[here, 3 * D:4 * D] = (dua * co * (sg * (1.0 + zc * (1.0 - sg)))).astype(BF16)
            dco = dua * sz
            da4_ref[here, D:2 * D] = (dco * yconv).astype(BF16)
            dyc = dco * bg
            dwc = jnp.concatenate([jnp.sum(dyc * s, axis=0, keepdims=True) for s in (u_m2, u_m1, u)], axis=0)
            rows = lax.broadcasted_iota(jnp.int32, xc.shape, 0)
            n0 = later[0:1, :]
            n1 = later[1:2, :]
            dyc_p1 = jnp.where(rows == sub - 1, n0, pltpu.roll(dyc, sub - 1, 0))
            dyc_p2 = jnp.where(rows == sub - 2, n0, jnp.where(rows == sub - 1, n1, pltpu.roll(dyc, sub - 2, 0)))
            du = w_ref[2:3, :] * dyc + w_ref[1:2, :] * dyc_p1 + w_ref[0:1, :] * dyc_p2
            da4_ref[here, 0:D] = (du * cg).astype(BF16)
            da4_ref[here, 2 * D:3 * D] = (du * xc).astype(BF16)
            return ua, dwc, dyc[0:8, :]

        later = jnp.where(ends_sequence, 0.0, later_ref[...])
        uas, dwc = [], jnp.zeros((3, D), F32)
        for p in reversed(range(parts)):
            ua, dwc_p, later = part(p, later)
            uas.insert(0, ua)
            dwc = dwc + dwc_p
        later_ref[...] = later
        dwpc_ref[...] += _dot_tn(jnp.concatenate(uas, axis=0), dya_ref[...])
        dwc_ref[0:3, :] += dwc
        for j in range(4):
            acc_ref[j * D:(j + 1) * D, :] += _dot_tn(da4_ref[:, j * D:(j + 1) * D], h_ref[...])

        @pl.when(step == n_t - 1)
        def _():
            c32 = pltpu.make_async_copy(acc_ref, o32_ref.at[pl.ds(0, 4 * D)], sems.at[0])
            c32.start()
            for j in range(4 * D // STAGE_ROWS):
                rows = pl.ds(j * STAGE_ROWS, STAGE_ROWS)
                stage_ref[...] = acc_ref[rows, :].astype(BF16)
                c16 = pltpu.make_async_copy(stage_ref, o16_ref.at[rows], sems.at[1])
                c16.start()
                c16.wait()
            c32.wait()

    def rows_of_tile(width, col=0):
        return pl.BlockSpec((tm, width), lambda s: (n_t - 1 - s, col))

    def prev(col):
        return pl.BlockSpec((16, D), lambda s: (jnp.maximum((n_t - 1 - s) * (tm // 16) - 1, 0), col))

    hbm = pl.BlockSpec(memory_space=pl.ANY)
    out = pl.pallas_call(
        body, name="bwd_conv", grid=(n_t,),
        in_specs=[rows_of_tile(D), rows_of_tile(D, 0), rows_of_tile(D, 1), rows_of_tile(D, 2), rows_of_tile(D, 3),
                  prev(0), prev(2), pl.BlockSpec((8, D), lambda s: (0, 0)), _whole_vmem(), rows_of_tile(D)],
        out_specs=[rows_of_tile(4 * D), _whole_vmem(), pl.BlockSpec((8, D), lambda s: (0, 0)), hbm, hbm],
        out_shape=[jax.ShapeDtypeStruct((t, 4 * D), BF16), jax.ShapeDtypeStruct((D, D), F32),
                   jax.ShapeDtypeStruct((8, D), F32), jax.ShapeDtypeStruct((D_IN, D), F32),
                   jax.ShapeDtypeStruct((D_IN, D), BF16)],
        scratch_shapes=[pltpu.VMEM((4 * D, D), F32), pltpu.VMEM((STAGE_ROWS, D), BF16), pltpu.VMEM((8, D), F32),
                        pltpu.SemaphoreType.DMA((2,))],
        compiler_params=pltpu.CompilerParams(dimension_semantics=("arbitrary",), vmem_limit_bytes=V7X_VMEM_BYTES - (2 << 20)),
    )(dya, a4, a4, a4, a4, a4, a4, wconv8, wpc, h)
    return out[0], out[1], out[2], (out[3], out[4])


def _bwd_dh(da4, dh_part, dkv, dgab, wt, x, g_pre, dout, tm):
    t = x.shape[0]

    def body(da4_ref, dhp_ref, dkv_ref, dgab_ref, w_ref, x_ref, g_ref, dout_ref, gx_ref, dg_ref):
        @pl.when(pl.program_id(0) == 0)
        def _():
            dg_ref[...] = jnp.zeros_like(dg_ref)

        dh = dhp_ref[...] + _dot(da4_ref[...], w_ref[0:ROW_Q, :])
        dh += _dot(dkv_ref[...], w_ref[ROW_KV:ROW_ZA, :])
        dh += _dot(dgab_ref[...], w_ref[ROW_GA:D_IN, :])
        xf = x_ref[...]
        r = lax.rsqrt(jnp.mean(xf * xf, axis=-1, keepdims=True) + RMS_EPS)
        xn = xf * r
        dg_ref[0:1, :] += jnp.sum(dh * xn, axis=0, keepdims=True)
        dxn = dh * g_ref[...]
        gx_ref[...] = dout_ref[...] + r * (dxn - xn * jnp.mean(dxn * xn, axis=-1, keepdims=True))

    return pl.pallas_call(
        body, name="bwd_dh", grid=(t // tm,),
        in_specs=[_row_spec(tm, 4 * D), _row_spec(tm, D), _row_spec(tm, 256), _row_spec(tm, 2 * D),
                  _whole_vmem(), _row_spec(tm, D), pl.BlockSpec((1, D), lambda i: (0, 0)), _row_spec(tm, D)],
        out_specs=[_row_spec(tm, D), pl.BlockSpec((8, D), lambda i: (0, 0))],
        out_shape=[jax.ShapeDtypeStruct((t, D), F32), jax.ShapeDtypeStruct((8, D), F32)],
        compiler_params=_params("arbitrary"),
    )(da4, dh_part, dkv, dgab, wt, x, g_pre, dout)


def _bwd_dw_in(h, piece, row0, nb, tm, name, prev):
    t, n = piece.shape
    n_t = t // tm

    def body(*refs):
        h_ref, p_ref = refs[0], refs[1]
        o32_ref, o16_ref, acc_ref, acc16_ref, sems = refs[-5:]
        j, i = pl.program_id(0), pl.program_id(1)

        @pl.when(i == 0)
        def _():
            acc_ref[...] = jnp.zeros_like(acc_ref)

        acc_ref[...] += _dot_tn(p_ref[...], h_ref[...])

        @pl.when(i == n_t - 1)
        def _():
            acc16_ref[...] = acc_ref[...].astype(BF16)
            rows = pl.ds(pl.multiple_of(row0 + j * nb, 16), nb)
            c32 = pltpu.make_async_copy(acc_ref, o32_ref.at[rows], sems.at[0])
            c16 = pltpu.make_async_copy(acc16_ref, o16_ref.at[rows], sems.at[1])
            c32.start()
            c16.start()
            c32.wait()
            c16.wait()

    hbm = pl.BlockSpec(memory_space=pl.ANY)
    carried = [] if prev is None else list(prev)
    return pl.pallas_call(
        body, name=name, grid=(n // nb, n_t),
        in_specs=[pl.BlockSpec((tm, D), lambda j, i: (i, 0)), pl.BlockSpec((tm, nb), lambda j, i: (i, j))]
        + [hbm] * len(carried),
        out_specs=[hbm, hbm],
        out_shape=[jax.ShapeDtypeStruct((D_IN, D), F32), jax.ShapeDtypeStruct((D_IN, D), BF16)],
        scratch_shapes=[pltpu.VMEM((nb, D), F32), pltpu.VMEM((nb, D), BF16), pltpu.SemaphoreType.DMA((2,))],
        input_output_aliases={2: 0, 3: 1} if carried else {},
        compiler_params=_params("arbitrary", "arbitrary"),
    )(h, piece, *carried)


def _place():
    x, y, c = lax.axis_index("x"), lax.axis_index("y"), lax.axis_index("c")
    return x, y, c, 4 * x + 2 * y + c


def _peer(x, y, c, k):
    return (1 - x if k & 4 else x, 1 - y if k & 2 else y, 1 - c if k & 1 else c)


ICI_MASKS = (4, 2, 6)


def _all_gather(shards):
    n = len(shards)

    def body(*refs):
        src, dst = refs[:n], refs[n:2 * n]
        send_sems, recv_sems, local_sems = refs[2 * n:]
        x, y, c, me = _place()
        sibling = _peer(x, y, c, 1)

        def copy(a, s, block, to, own=False):
            return pltpu.make_async_remote_copy(
                src_ref=src[a] if own else dst[a].at[block], dst_ref=dst[a].at[block],
                send_sem=send_sems.at[a * 7 + s], recv_sem=recv_sems.at[a * 7 + s], device_id=to, device_id_type=MESH_ID)

        local = [pltpu.make_async_copy(src[a], dst[a].at[me], local_sems.at[a]) for a in range(n)]
        for cp in local:
            cp.start()
        started = [copy(a, 0, me, sibling, own=True) for a in range(n)]
        started += [copy(a, 1 + j, me, _peer(x, y, c, k), own=True) for j, k in enumerate(ICI_MASKS) for a in range(n)]
        for cp in started:
            cp.start()
        for j, k in enumerate(ICI_MASKS):
            for a in range(n):
                copy(a, 1 + j, me ^ k, sibling).wait_recv()
                fwd = copy(a, 4 + j, me ^ k, sibling)
                fwd.start()
                started.append(fwd)
        for a in range(n):
            copy(a, 0, me ^ 1, sibling).wait_recv()
        for j, k in enumerate(ICI_MASKS):
            for a in range(n):
                copy(a, 4 + j, me ^ 1 ^ k, sibling).wait_recv()
        for cp in started:
            cp.wait_send()
        for cp in local:
            cp.wait()

    hbm = pl.BlockSpec(memory_space=pl.ANY)
    return pl.pallas_call(
        body, name="all_gather_weights",
        in_specs=[hbm] * n, out_specs=[hbm] * n,
        out_shape=[jax.ShapeDtypeStruct((N_DEV,) + s.shape, s.dtype) for s in shards],
        scratch_shapes=[pltpu.SemaphoreType.DMA((7 * n,)), pltpu.SemaphoreType.DMA((7 * n,)),
                        pltpu.SemaphoreType.DMA((n,))],
    )(*shards)


def _direct_copies(src, land, send_sems, recv_sems):
    x, y, c, me = _place()
    return [pltpu.make_async_remote_copy(
        src_ref=src[a], dst_ref=land[a].at[me], send_sem=send_sems.at[a * 7 + k - 1],
        recv_sem=recv_sems.at[a * 7 + k - 1], device_id=_peer(x, y, c, k), device_id_type=MESH_ID)
        for k in range(1, N_DEV) for a in range(len(src))]


def _gather_start(shards, name):
    n = len(shards)

    def body(*refs):
        src, land = refs[:n], refs[n:2 * n]
        send_sems, recv_sems = refs[2 * n], refs[2 * n + 1]
        token_ref = refs[-1]
        for cp in _direct_copies(src, land, send_sems, recv_sems):
            cp.start()
        token_ref[...] = jnp.zeros_like(token_ref)

    hbm = pl.BlockSpec(memory_space=pltpu.HBM)
    sem = pl.BlockSpec(memory_space=pltpu.SEMAPHORE)
    lands = [lax.empty((N_DEV,) + s.shape, s.dtype) for s in shards]
    out = pl.pallas_call(
        body, name=name + "_start",
        out_shape=(pltpu.SemaphoreType.DMA((7 * n,)), pltpu.SemaphoreType.DMA((7 * n,)),
                   *[pltpu.HBM(s.shape, s.dtype) for s in shards], *[pltpu.HBM(s.shape, s.dtype) for s in lands],
                   jax.ShapeDtypeStruct((8, 128), F32)),
        in_specs=[hbm] * (2 * n), out_specs=(sem, sem, *[hbm] * (2 * n), _whole_vmem()),
        input_output_aliases={a: 2 + a for a in range(2 * n)},
        compiler_params=pltpu.CompilerParams(has_side_effects=pltpu.SideEffectType.DATAFLOW_SIDE_EFFECTING),
    )(*[pltpu.with_memory_space_constraint(s, pltpu.HBM) for s in list(shards) + lands])
    return out[0], out[1], out[2:2 + n], out[2 + n:2 + 2 * n], out[-1]


def _gather_wait(send_sems, recv_sems, flying, lands, after, name):
    n = len(flying)

    def body(*refs):
        src, land = refs[:n], refs[n:2 * n]
        for cp in _direct_copies(src, land, refs[2 * n], refs[2 * n + 1]):
            cp.wait_send()
            cp.wait_recv()

    hbm = pl.BlockSpec(memory_space=pltpu.HBM)
    sem = pl.BlockSpec(memory_space=pltpu.SEMAPHORE)
    out = pl.pallas_call(
        body, name=name + "_wait",
        out_shape=tuple(pltpu.HBM(s.shape, s.dtype) for s in list(flying) + list(lands)),
        in_specs=[hbm] * (2 * n) + [sem, sem, pl.BlockSpec(memory_space=pl.ANY)], out_specs=tuple([hbm] * (2 * n)),
        input_output_aliases={a: a for a in range(2 * n)},
        compiler_params=pltpu.CompilerParams(has_side_effects=pltpu.SideEffectType.DATAFLOW_SIDE_EFFECTING),
    )(*flying, *lands, send_sems, recv_sems, after)
    return out[n:]


def _exchange_sibling(by_dest):
    n = len(by_dest)

    def body(*refs):
        src, dst = refs[:n], refs[n:2 * n]
        send_sems, recv_sems = refs[2 * n:]
        x, y, c, _ = _place()
        sibling = _peer(x, y, c, 1)
        copies = [pltpu.make_async_remote_copy(
            src_ref=src[a].at[2 * p + (1 - c)], dst_ref=dst[a].at[p], send_sem=send_sems.at[a * 4 + p],
            recv_sem=recv_sems.at[a * 4 + p], device_id=sibling, device_id_type=MESH_ID)
            for a in range(n) for p in range(4)]
        for cp in copies:
            cp.start()
        for cp in copies:
            cp.wait_recv()
        for cp in copies:
            cp.wait_send()

    hbm = pl.BlockSpec(memory_space=pl.ANY)
    return pl.pallas_call(
        body, name="exchange_sibling", in_specs=[hbm] * n, out_specs=[hbm] * n,
        out_shape=[jax.ShapeDtypeStruct((4,) + s.shape[1:], s.dtype) for s in by_dest],
        scratch_shapes=[pltpu.SemaphoreType.DMA((4 * n,)), pltpu.SemaphoreType.DMA((4 * n,))],
    )(*by_dest)


def _chip_copies(src, land, send_sems, recv_sems):
    x, y, c, _ = _place()
    chip = 2 * x + y
    return [pltpu.make_async_remote_copy(
        src_ref=src[a].at[chip ^ (k >> 1)], dst_ref=land[a].at[j], send_sem=send_sems.at[a * 3 + j],
        recv_sem=recv_sems.at[a * 3 + j], device_id=_peer(x, y, c, k), device_id_type=MESH_ID)
        for j, k in enumerate(ICI_MASKS) for a in range(len(src))]


def _exchange_chips_start(by_chip):
    n = len(by_chip)

    def body(*refs):
        src, land = refs[:n], refs[n:2 * n]
        send_sems, recv_sems = refs[2 * n], refs[2 * n + 1]
        token_ref = refs[-1]
        for cp in _chip_copies(src, land, send_sems, recv_sems):
            cp.start()
        token_ref[...] = jnp.zeros_like(token_ref)

    hbm = pl.BlockSpec(memory_space=pltpu.HBM)
    sem = pl.BlockSpec(memory_space=pltpu.SEMAPHORE)
    lands = [lax.empty((3,) + s.shape[1:], s.dtype) for s in by_chip]
    out = pl.pallas_call(
        body, name="exchange_chips_start",
        out_shape=(pltpu.SemaphoreType.DMA((3 * n,)), pltpu.SemaphoreType.DMA((3 * n,)),
                   *[pltpu.HBM(s.shape, s.dtype) for s in by_chip], *[pltpu.HBM(s.shape, s.dtype) for s in lands],
                   jax.ShapeDtypeStruct((8, 128), F32)),
        in_specs=[hbm] * (2 * n), out_specs=(sem, sem, *[hbm] * (2 * n), _whole_vmem()),
        input_output_aliases={a: 2 + a for a in range(2 * n)},
        compiler_params=pltpu.CompilerParams(has_side_effects=pltpu.SideEffectType.DATAFLOW_SIDE_EFFECTING),
    )(*[pltpu.with_memory_space_constraint(s, pltpu.HBM) for s in list(by_chip) + lands])
    return out[0], out[1], out[2:2 + n], out[2 + n:2 + 2 * n], out[-1]


def _exchange_chips_wait(send_sems, recv_sems, flying, lands, after):
    n = len(flying)

    def body(*refs):
        src, land = refs[:n], refs[n:2 * n]
        send_sems_ref, recv_sems_ref = refs[2 * n], refs[2 * n + 1]
        for cp in _chip_copies(src, land, send_sems_ref, recv_sems_ref):
            cp.wait_send()
            cp.wait_recv()

    hbm = pl.BlockSpec(memory_space=pltpu.HBM)
    sem = pl.BlockSpec(memory_space=pltpu.SEMAPHORE)
    out = pl.pallas_call(
        body, name="exchange_chips_wait",
        out_shape=tuple(pltpu.HBM(s.shape, s.dtype) for s in list(flying) + list(lands)),
        in_specs=[hbm] * (2 * n) + [sem, sem, pl.BlockSpec(memory_space=pl.ANY)], out_specs=tuple([hbm] * (2 * n)),
        input_output_aliases={a: a for a in range(2 * n)},
        compiler_params=pltpu.CompilerParams(has_side_effects=pltpu.SideEffectType.DATAFLOW_SIDE_EFFECTING),
    )(*flying, *lands, send_sems, recv_sems, after)
    return out[n:]


def _adamw_math(w, g, m, v):
    m = ADAM_B1 * m + (1.0 - ADAM_B1) * g
    v = ADAM_B2 * v + (1.0 - ADAM_B2) * (g * g)
    m_hat = m / (1.0 - ADAM_B1 ** ADAM_STEP)
    v_hat = v / (1.0 - ADAM_B2 ** ADAM_STEP)
    return -ADAM_LR * (m_hat / (jnp.sqrt(v_hat) + ADAM_EPS) + ADAM_WD * w), m, v


def _pair_sum(owns, recvs, place_arr, tr, name):
    n = len(owns)
    _, rows, cols = owns[0].shape

    def body(place_ref, *refs):
        for a in range(n):
            s = refs[a][...] + refs[n + a][...].astype(F32)
            refs[3 * n + a][...] = s.astype(BF16)

            @pl.when(pl.program_id(1) == place_ref[1])
            def _(a=a, s=s):
                refs[2 * n + a][...] = s

    by_chip = pl.BlockSpec((None, tr, cols), lambda i, p, place_ref: (p, i, 0))
    mine = pl.BlockSpec((None, tr, cols), lambda i, p, place_ref: (2 * p + place_ref[0], i, 0))
    kept = pl.BlockSpec((tr, cols), lambda i, p, place_ref: (i, 0))
    out = pl.pallas_call(
        body, name=name,
        grid_spec=pltpu.PrefetchScalarGridSpec(
            num_scalar_prefetch=1, grid=(rows // tr, 4), in_specs=[mine] * n + [by_chip] * n,
            out_specs=[kept] * n + [by_chip] * n),
        out_shape=[jax.ShapeDtypeStruct((rows, cols), F32)] * n + [jax.ShapeDtypeStruct((4, rows, cols), BF16)] * n,
        compiler_params=_params("parallel", "arbitrary"),
    )(place_arr, *owns, *recvs)
    return out[:n], out[n:]


def _chip_sum(pairs, recvs, tr, name, adam=None):
    n = len(pairs)
    rows, cols = pairs[0].shape
    n_state = 0 if adam is None else 3 * n

    def body(*refs):
        outs = refs[2 * n + n_state:]
        for a in range(n):
            g = refs[a][...]
            for j in range(3):
                g = g + refs[n + a][j].astype(F32)
            outs[a][...] = g
            if adam is not None:
                w_ref, m_ref, v_ref = (refs[2 * n + s * n + a] for s in range(3))
                outs[n + a][...], outs[2 * n + a][...], outs[3 * n + a][...] = _adamw_math(w_ref[...], g, m_ref[...], v_ref[...])

    blk = pl.BlockSpec((tr, cols), lambda i: (i, 0))
    n_out = n if adam is None else 4 * n
    out = pl.pallas_call(
        body, name=name, grid=(rows // tr,),
        in_specs=[blk] * n + [pl.BlockSpec((3, tr, cols), lambda i: (0, i, 0))] * n + [blk] * n_state,
        out_specs=[blk] * n_out,
        out_shape=[jax.ShapeDtypeStruct((rows, cols), F32)] * n_out,
        compiler_params=_params("parallel"),
    )(*pairs, *recvs, *([] if adam is None else [t for group in adam for t in group]))
    return out if adam is None else (out[:n], out[n:2 * n], out[2 * n:3 * n], out[3 * n:])


def _adamw(ws, gs, ms, vs, name):
    n = len(ws)

    def body(*refs):
        for a in range(n):
            w_ref, g_ref, m_ref, v_ref = (refs[s * n + a] for s in range(4))
            refs[4 * n + a][...], refs[5 * n + a][...], refs[6 * n + a][...] = _adamw_math(
                w_ref[...], g_ref[...], m_ref[...], v_ref[...])

    out = pl.pallas_call(body, name=name, out_shape=[jax.ShapeDtypeStruct(w.shape, F32) for w in ws] * 3)(
        *ws, *gs, *ms, *vs)
    return out[:n], out[n:2 * n], out[2 * n:]


def _sum_small(small_all):
    def body(s_ref, o_ref):
        g = s_ref[0]
        for d in range(1, N_DEV):
            g = g + s_ref[d]
        o_ref[...] = g

    return pl.pallas_call(body, name="sum_small", out_shape=jax.ShapeDtypeStruct(small_all.shape[1:], F32))(small_all)


def _rope_tables():
    inv_freq = ROPE_THETA ** (-jnp.arange(0, HEAD_DIM, 2, dtype=F32) / HEAD_DIM)
    ang = jnp.arange(SEQ_LEN).astype(F32)[:, None] * inv_freq[None, :]
    cos, sin = jnp.cos(ang), jnp.sin(ang)
    return jnp.tile(cos, (1, 4)), jnp.tile(jnp.concatenate([-sin, sin], axis=1), (1, 2))


def _local_step(x, target, g_pre, g_post, sinks, wt, wconv, squares, start_exchange=None):
    cos_t, sin_t = _rope_tables()
    wconv8 = jnp.pad(wconv, ((0, 5), (0, 0)))
    h, q, kv, g3 = _fwd_in_attn(x, g_pre, wt, cos_t, sin_t, 512)
    wpc, wpa, wout = squares(kv)
    a4, ya = _fwd_in_conv(h, wt, wconv8, wpc, 512)
    attn, ub, lse = _fwd_attn(sinks, q, kv, g3, 4)
    loss8, dout, dya, dub, dgab, dwout, dwpa, dgpost8 = _fwd_out_bwd_head(ya, ub, g3, x, target, g_post, wpa, wout, 512)
    dq, dza, dkv_own, dkv_prev, dsink8, dh_part = _bwd_attn(sinks, q, kv, attn, lse, dub, g3, cos_t, sin_t, wt, 4)
    da4, dwpc, dwconv8, dwt = _bwd_conv(dya, a4, h, wconv8, wpc, 512, 2)
    dkv, dwt = _bwd_kv_finish(dkv_own, dkv_prev, cos_t, sin_t, h, dwt)
    dwt = _bwd_dw_in(h, dq, ROW_Q, 1024, 1024, "bwd_dw_in_q", dwt)
    dwt = _bwd_dw_in(h, dza, ROW_ZA, 1024, 1024, "bwd_dw_in_za", dwt)
    dwt32, dwt16 = _bwd_dw_in(h, dgab, ROW_GA, 1024, 1024, "bwd_dw_in_gates", dwt)
    token, pending = (None, None) if start_exchange is None else start_exchange(dwt32, dwt16, dwpc, dwpa, dwout)
    g_pre_after = g_pre if token is None else g_pre + token[0:1, 0:1]
    grad_x, dgpre8 = _bwd_dh(da4, dh_part, dkv, dgab, wt, x, g_pre_after, dout, 512)
    small = jnp.concatenate([dgpre8, dgpost8, jnp.pad(dsink8, ((0, 0), (0, D - 128))), dwconv8,
                             jnp.pad(loss8, ((0, 0), (0, D - 128)))], axis=0)
    return loss8[0, 0], grad_x, dwt32, dwt16, dwpc, dwpa, dwout, small, pending


def kernel(x, g_pre, g_post, w_in, w_conv, sinks, w_proj_conv, w_proj_attn, w_out, loss_target, m_g_pre, m_g_post, m_w_in, m_w_conv, m_sinks, m_w_proj_conv, m_w_proj_attn, m_w_out, v_g_pre, v_g_post, v_w_in, v_w_conv, v_sinks, v_w_proj_conv, v_w_proj_attn, v_w_out):
    batch = x.shape[0]
    mx, my, mc, me = _place()
    place_arr = jnp.stack([mc, 2 * mx + my]).astype(jnp.int32)

    g_wt, g_conv = _all_gather([w_in[0].T.astype(BF16), jnp.pad(w_conv[0], ((0, 5), (0, 0)))])
    wt = g_wt.reshape(D_IN, D)
    wconv = g_conv[:, 0:3, :].transpose(1, 0, 2).reshape(3, D)
    sq_mine = [w.astype(BF16) for w in (w_proj_conv[0], w_proj_attn[0], w_out[0])]
    wt, sq_mine = lax.optimization_barrier((wt, sq_mine))
    sq_send, sq_recv, sq_flying, sq_lands, sq_token = _gather_start(sq_mine, "gather_squares")

    def squares(after):
        got = _gather_wait(sq_send, sq_recv, sq_flying, sq_lands, after, "gather_squares")
        return [lax.dynamic_update_index_in_dim(full, mine, me, 0).reshape(D, D) for full, mine in zip(got, sq_mine)]

    def start_exchange(dwt32, dwt16, dwpc, dwpa, dwout):
        own_sq = [g.reshape(N_DEV, SHARD_SQ, D) for g in (dwpc, dwpa, dwout)]
        own_in = dwt32.reshape(N_DEV, SHARD_IN, D)
        from_sibling = _exchange_sibling([dwt16.reshape(N_DEV, SHARD_IN, D)] + [g.astype(BF16) for g in own_sq])
        in32, in16 = _pair_sum([own_in], from_sibling[:1], place_arr, SHARD_IN // 2, "pair_sum_w_in")
        sq32, sq16 = _pair_sum(own_sq, from_sibling[1:], place_arr, SHARD_SQ, "pair_sum_squares")
        send_sems, recv_sems, flying, lands, token = _exchange_chips_start(list(in16) + list(sq16))
        return token, (send_sems, recv_sems, flying, lands, in32, sq32)

    _, grad_x, _, _, _, _, _, small, pending = _local_step(
        x.reshape(batch * SEQ_LEN, D), loss_target.reshape(batch * SEQ_LEN, D), g_pre + sq_token[0:1, 0:1], g_post,
        sinks, wt, wconv, squares, start_exchange)
    sm_send, sm_recv, sm_flying, sm_lands, sm_token = _gather_start([small], "gather_small")
    send_sems, recv_sems, flying, lands, in32, sq32 = pending
    from_chips = _exchange_chips_wait(send_sems, recv_sems, flying, lands, sm_token)

    o_in = [o[0].T for o in _chip_sum(
        in32, from_chips[:1], SHARD_IN // 3, "chip_sum_adamw_w_in",
        adam=([w_in[0].T], [m_w_in[0].T], [v_w_in[0].T]))]
    g_in_mine, o_in = o_in[0], o_in[1:]
    g_sq, d_sq, m_sq, v_sq = _chip_sum(
        sq32, from_chips[1:], SHARD_SQ, "chip_sum_adamw_squares",
        adam=([w_proj_conv[0], w_proj_attn[0], w_out[0]], [m_w_proj_conv[0], m_w_proj_attn[0], m_w_out[0]],
              [v_w_proj_conv[0], v_w_proj_attn[0], v_w_out[0]]))
    both_done, g_in_mine = lax.optimization_barrier((d_sq[0], g_in_mine))
    (small_all,) = _gather_wait(sm_send, sm_recv, sm_flying, sm_lands, both_done, "gather_small")
    gs = _sum_small(lax.dynamic_update_index_in_dim(small_all, small, me, 0))
    g_g_pre, g_g_post, g_sinks, loss = gs[0:1], gs[8:9], gs[16:17, 0:N_HEADS], gs[32, 0]
    g_conv_mine = lax.dynamic_slice_in_dim(gs[24:27], me * SHARD_SQ, SHARD_SQ, axis=1)
    o_small = _adamw([g_pre, g_post, sinks, w_conv[0]], [g_g_pre, g_g_post, g_sinks, g_conv_mine],
                     [m_g_pre, m_g_post, m_sinks, m_w_conv[0]], [v_g_pre, v_g_post, v_sinks, v_w_conv[0]], "adamw_small")

    grads = [g_g_pre, g_g_post, g_in_mine[None], g_conv_mine[None], g_sinks] + [g[None] for g in g_sq]
    rest = []
    for idx, sq in enumerate((d_sq, m_sq, v_sq)):
        gp, gq, sk, cv = o_small[idx]
        rest += [gp, gq, o_in[idx][None], cv[None], sk] + [s[None] for s in sq]
    return (loss, grad_x.reshape(batch, SEQ_LEN, D), *grads, *rest)
```

```python
import jax
import jax.numpy as jnp
from jax import lax
from jax.experimental import pallas as pl
from jax.experimental.pallas import tpu as pltpu

D = 1024
N_HEADS = 16
HEAD_DIM = 64
LOGIT_SCALE = HEAD_DIM ** -0.5
BLK = 128
SEQ_LEN = 2048
D_IN = 8448
ROW_Q, ROW_KV, ROW_ZA, ROW_GA = 4 * D, 5 * D, 5 * D + 256, 6 * D + 256
SHARD_IN = D_IN // 8
SHARD_SQ = D // 8
N_DEV = 8
V7X_VMEM_BYTES = 64 << 20
ROPE_THETA = 10000.0
RMS_EPS = 1e-6
NEG = -1e30
ADAM_LR, ADAM_B1, ADAM_B2, ADAM_EPS, ADAM_WD, ADAM_STEP = 0.001, 0.9, 0.999, 1e-08, 0.01, 10

F32 = jnp.float32
BF16 = jnp.bfloat16
MESH_ID = pl.DeviceIdType.MESH


def _dot(a, b):
    return jnp.dot(a, b, preferred_element_type=F32)


def _dot_nt(a, b):
    return lax.dot_general(a, b, (((1,), (1,)), ((), ())), preferred_element_type=F32)


def _dot_tn(a, b):
    return lax.dot_general(a, b, (((0,), (0,)), ((), ())), preferred_element_type=F32)


def _sig(z):
    return 1.0 / (1.0 + jnp.exp(-z))


def _swap_halves(z):
    lane = lax.broadcasted_iota(jnp.int32, z.shape, 1)
    return jnp.where((lane & 63) < 32, pltpu.roll(z, 96, 1), pltpu.roll(z, 32, 1))


def _row_spec(tm, width, col=0):
    return pl.BlockSpec((tm, width), lambda i: (i, col))


def _whole_vmem():
    return pl.BlockSpec(memory_space=pltpu.VMEM)


def _params(*sem, vmem_limit_bytes=None):
    return pltpu.CompilerParams(dimension_semantics=sem, vmem_limit_bytes=vmem_limit_bytes)


def _fwd_in_attn(x, g_pre, wt, cos_t, sin_t, tm):
    t = x.shape[0]
    seq_tiles = SEQ_LEN // tm

    def body(x_ref, g_ref, w_ref, c_ref, s_ref, h_ref, q_ref, kv_ref, g3_ref):
        xf = x_ref[...]
        r = lax.rsqrt(jnp.mean(xf * xf, axis=-1, keepdims=True) + RMS_EPS)
        hh = ((xf * r) * g_ref[...]).astype(BF16)
        h_ref[...] = hh
        c = c_ref[...]
        s = s_ref[...]

        def rope(z):
            return z * c + _swap_halves(z) * s

        q = _dot_nt(hh, w_ref[ROW_Q:ROW_Q + D, :])
        for j in range(D // 128):
            q_ref[:, j * 128:(j + 1) * 128] = (rope(q[:, j * 128:(j + 1) * 128]) * LOGIT_SCALE).astype(BF16)
        kv = _dot_nt(hh, w_ref[ROW_KV:ROW_KV + 256, :])
        kv_ref[:, 0:128] = rope(kv[:, 0:128]).astype(BF16)
        kv_ref[:, 128:256] = kv[:, 128:256].astype(BF16)
        for j in range(3):
            g3_ref[:, j * D:(j + 1) * D] = _dot_nt(hh, w_ref[ROW_ZA + j * D:ROW_ZA + (j + 1) * D, :])

    tab = pl.BlockSpec((tm, 128), lambda i: (i % seq_tiles, 0))
    return pl.pallas_call(
        body, name="fwd_in_attn", grid=(t // tm,),
        in_specs=[_row_spec(tm, D), pl.BlockSpec((1, D), lambda i: (0, 0)), _whole_vmem(), tab, tab],
        out_specs=[_row_spec(tm, D), _row_spec(tm, D), _row_spec(tm, 256), _row_spec(tm, 3 * D)],
        out_shape=[jax.ShapeDtypeStruct((t, D), BF16), jax.ShapeDtypeStruct((t, D), BF16),
                   jax.ShapeDtypeStruct((t, 256), BF16), jax.ShapeDtypeStruct((t, 3 * D), F32)],
        compiler_params=_params("parallel"),
    )(x, g_pre, wt, cos_t, sin_t)


def _conv_forward(xc, bg, cg, zc, up6, up7, w_ref):
    rows = lax.broadcasted_iota(jnp.int32, xc.shape, 0)
    u = cg * xc
    u_m1 = jnp.where(rows == 0, up7, pltpu.roll(u, 1, 0))
    u_m2 = jnp.where(rows == 0, up6, jnp.where(rows == 1, up7, pltpu.roll(u, 2, 0)))
    yconv = w_ref[0:1, :] * u_m2 + w_ref[1:2, :] * u_m1 + w_ref[2:3, :] * u
    sg = _sig(zc)
    sz = zc * sg
    co = bg * yconv
    return u, u_m1, u_m2, yconv, sg, sz, co


def _fwd_in_conv(h, wt, wconv8, wpc, tm):
    t = h.shape[0]
    seq_tiles = SEQ_LEN // tm

    def body(h_ref, w_ref, wc_ref, wpc_ref, a4_ref, ya_ref, last_u_ref):
        hh = h_ref[...]
        xc, bg, cg, zc = (_dot_nt(hh, w_ref[j * D:(j + 1) * D, :]) for j in range(4))
        for j, z in enumerate((xc, bg, cg, zc)):
            a4_ref[:, j * D:(j + 1) * D] = z.astype(BF16)
        first = pl.program_id(0) % seq_tiles == 0
        up6 = jnp.where(first, 0.0, last_u_ref[6:7, :])
        up7 = jnp.where(first, 0.0, last_u_ref[7:8, :])
        u, _, _, _, _, sz, co = _conv_forward(xc, bg, cg, zc, up6, up7, wc_ref)
        last_u_ref[...] = u[tm - 8:tm, :]
        ya_ref[...] = _dot((sz * co).astype(BF16), wpc_ref[...])

    return pl.pallas_call(
        body, name="fwd_in_conv", grid=(t // tm,),
        in_specs=[_row_spec(tm, D), _whole_vmem(), pl.BlockSpec((8, D), lambda i: (0, 0)), _whole_vmem()],
        out_specs=[_row_spec(tm, 4 * D), _row_spec(tm, D)],
        out_shape=[jax.ShapeDtypeStruct((t, 4 * D), BF16), jax.ShapeDtypeStruct((t, D), F32)],
        scratch_shapes=[pltpu.VMEM((8, D), F32)],
        compiler_params=_params("arbitrary"),
    )(h, wt, wconv8, wpc)


STACK = 4 * BLK


def _band_mask(first):
    qi = lax.broadcasted_iota(jnp.int32, (STACK, 2 * BLK), 0) & (BLK - 1)
    kj = lax.broadcasted_iota(jnp.int32, (STACK, 2 * BLK), 1)
    return (kj > qi) & (kj <= qi + BLK) & (kj >= jnp.where(first, BLK, 0))


def _masked_fill(sink_ref, g, e):
    kj = lax.broadcasted_iota(jnp.int32, (STACK, 2 * BLK), 1)
    sink = jnp.concatenate([jnp.full((BLK, 2 * BLK), sink_ref[0, 2 * (4 * g + jj) + e], F32) for jj in range(4)], axis=0)
    return jnp.where(kj == 0, sink, NEG)


def _padded_pair(before, own):
    z = jnp.concatenate([before, own], axis=0).astype(F32)
    z = jnp.where(lax.broadcasted_iota(jnp.int32, z.shape, 0) == 0, 0.0, z)
    zs = pltpu.roll(z, 64, 1)
    lo = lax.broadcasted_iota(jnp.int32, z.shape, 1) < 64
    zero = jnp.zeros_like(z)
    left = [jnp.where(lo, z, zero).astype(BF16), jnp.where(lo, zs, zero).astype(BF16)]
    right = [jnp.where(lo, zero, zs).astype(BF16), jnp.where(lo, zero, z).astype(BF16)]
    return left, right


def _exp_logits(s, valid, fill):
    s = jnp.where(valid, s, fill)
    m = jnp.max(s, axis=-1, keepdims=True)
    return jnp.exp(s - m), m


def _kv_blocks(kvc_ref, kvp_ref, b, col):
    own = kvc_ref[b * BLK:(b + 1) * BLK, col:col + 128]
    before = kvp_ref[:, col:col + 128] if b == 0 else kvc_ref[(b - 1) * BLK:b * BLK, col:col + 128]
    return before, own


def _fwd_attn(sinks, q, kv, g3, blocks):
    t = q.shape[0]
    tq = blocks * BLK
    seq_blocks = SEQ_LEN // BLK

    def body(sink_ref, q_ref, kvc_ref, kvp_ref, za_ref, attn_ref, ub_ref, lse_ref):
        lo = lax.broadcasted_iota(jnp.int32, (STACK, 128), 1) < 64
        for b in range(blocks):
            rows = slice(b * BLK, (b + 1) * BLK)
            valid = _band_mask((pl.program_id(0) * blocks + b) % seq_blocks == 0)
            k_pad = _padded_pair(*_kv_blocks(kvc_ref, kvp_ref, b, 0))
            v_pad = _padded_pair(*_kv_blocks(kvc_ref, kvp_ref, b, 128))
            for g in range(2):
                qg = jnp.concatenate([q_ref[rows, j * 128:(j + 1) * 128] for j in range(4 * g, 4 * g + 4)], axis=0)
                pv, den = [], []
                for e in range(2):
                    p, m = _exp_logits(_dot_nt(qg, k_pad[e][g]), valid, _masked_fill(sink_ref, g, e))
                    both = _dot(p.astype(BF16), jnp.concatenate([v_pad[e][g], jnp.ones((2 * BLK, 128), BF16)], axis=1))
                    pv.append(both[:, 0:128])
                    den.append(both[:, 128:256])
                    lse_ref[b, 2 * g + e] = m + jnp.log(den[e])
                o = jnp.where(lo, pv[0] / den[0], pv[1] / den[1])
                for jj in range(4):
                    cols = slice((4 * g + jj) * 128, (4 * g + jj + 1) * 128)
                    oj = o[jj * BLK:(jj + 1) * BLK, :]
                    attn_ref[rows, cols] = oj
                    za = za_ref[rows, cols]
                    ub_ref[rows, cols] = (za * _sig(za) * oj).astype(BF16)

    return pl.pallas_call(
        body, name="fwd_attn", grid=(t // tq,),
        in_specs=[pl.BlockSpec(memory_space=pltpu.SMEM), _row_spec(tq, D), _row_spec(tq, 256),
                  pl.BlockSpec((BLK, 256), lambda i: (jnp.maximum(i * blocks - 1, 0), 0)), _row_spec(tq, D, 0)],
        out_specs=[_row_spec(tq, D), _row_spec(tq, D), pl.BlockSpec((blocks, 4, STACK, 128), lambda i: (i, 0, 0, 0))],
        out_shape=[jax.ShapeDtypeStruct((t, D), F32), jax.ShapeDtypeStruct((t, D), BF16),
                   jax.ShapeDtypeStruct((t // BLK, 4, STACK, 128), F32)],
        compiler_params=_params("parallel"),
    )(sinks, q, kv, kv, g3)


def _fwd_out_bwd_head(ya, ub, g3, x, target, g_post, wpa, wout, tm):
    t = x.shape[0]

    def body(ya_ref, ub_ref, ga_ref, gb_ref, x_ref, tgt_ref, gp_ref, wpa_ref, wout_ref,
             loss_ref, dout_ref, dya_ref, dub_ref, dgab_ref, dwout_ref, dwpa_ref, dgp_ref):
        @pl.when(pl.program_id(0) == 0)
        def _():
            loss_ref[...] = jnp.zeros_like(loss_ref)
            dwout_ref[...] = jnp.zeros_like(dwout_ref)
            dwpa_ref[...] = jnp.zeros_like(dwpa_ref)
            dgp_ref[...] = jnp.zeros_like(dgp_ref)

        g = gp_ref[...]
        halves = (slice(0, tm // 2), slice(tm // 2, tm))

        def stage1(rows):
            return _dot(ub_ref[rows, :], wpa_ref[...])

        def stage2(rows, yb):
            sa = _sig(ga_ref[rows, :])
            sb = _sig(gb_ref[rows, :])
            mb = (sa * ya_ref[rows, :] + sb * yb).astype(BF16)
            return sa, sb, mb, _dot(mb, wout_ref[...])

        def stage3(rows, y):
            r = lax.rsqrt(jnp.mean(y * y, axis=-1, keepdims=True) + RMS_EPS)
            n = y * r
            err = (x_ref[rows, :] + n * g) - tgt_ref[rows, :]
            sq = jnp.sum(jnp.sum(err * err, axis=0, keepdims=True), axis=1, keepdims=True)
            dout = err * (1.0 / D)
            dout_ref[rows, :] = dout
            dgp = jnp.sum(dout * n, axis=0, keepdims=True)
            dn = dout * g
            dy = (r * (dn - n * jnp.mean(dn * n, axis=-1, keepdims=True))).astype(BF16)
            return sq, dgp, dy, _dot_nt(dy, wout_ref[...])

        def stage4(rows, dm, sa, sb, yb):
            dya_ref[rows, :] = (dm * sa).astype(BF16)
            dyb = (dm * sb).astype(BF16)
            dgab_ref[rows, 0:D] = (dm * ya_ref[rows, :] * (sa * (1.0 - sa))).astype(BF16)
            dgab_ref[rows, D:2 * D] = (dm * yb * (sb * (1.0 - sb))).astype(BF16)
            dub_ref[rows, :] = _dot_nt(dyb, wpa_ref[...])
            return dyb

        yb = [stage1(rows) for rows in halves]
        s2 = [stage2(rows, yb[k]) for k, rows in enumerate(halves)]
        s3 = [stage3(rows, s2[k][3]) for k, rows in enumerate(halves)]
        dyb = [stage4(rows, s3[k][3], s2[k][0], s2[k][1], yb[k]) for k, rows in enumerate(halves)]
        loss_ref[...] += sum(s[0] for s in s3) * (0.5 / D)
        dgp_ref[0:1, :] += sum(s[1] for s in s3)
        dwout_ref[...] += _dot_tn(jnp.concatenate([s[2] for s in s2], axis=0), jnp.concatenate([s[2] for s in s3], axis=0))
        dwpa_ref[...] += _dot_tn(ub_ref[...], jnp.concatenate(dyb, axis=0))

    return pl.pallas_call(
        body, name="fwd_out_bwd_head", grid=(t // tm,),
        in_specs=[_row_spec(tm, D), _row_spec(tm, D), _row_spec(tm, D, 1), _row_spec(tm, D, 2),
                  _row_spec(tm, D), _row_spec(tm, D), pl.BlockSpec((1, D), lambda i: (0, 0)),
                  _whole_vmem(), _whole_vmem()],
        out_specs=[pl.BlockSpec((8, 128), lambda i: (0, 0)), _row_spec(tm, D), _row_spec(tm, D), _row_spec(tm, D),
                   _row_spec(tm, 2 * D), _whole_vmem(), _whole_vmem(), pl.BlockSpec((8, D), lambda i: (0, 0))],
        out_shape=[jax.ShapeDtypeStruct((8, 128), F32), jax.ShapeDtypeStruct((t, D), F32),
                   jax.ShapeDtypeStruct((t, D), BF16), jax.ShapeDtypeStruct((t, D), F32),
                   jax.ShapeDtypeStruct((t, 2 * D), BF16), jax.ShapeDtypeStruct((D, D), F32),
                   jax.ShapeDtypeStruct((D, D), F32), jax.ShapeDtypeStruct((8, D), F32)],
        compiler_params=_params("arbitrary", vmem_limit_bytes=V7X_VMEM_BYTES - (2 << 20)),
    )(ya, ub, g3, g3, x, target, g_post, wpa, wout)


def _bwd_attn(sinks, q, kv, attn, lse, dub, g3, cos_t, sin_t, wt, blocks):
    t = q.shape[0]
    tq = blocks * BLK
    seq_blocks = SEQ_LEN // BLK

    def body(sink_ref, q_ref, kvc_ref, kvp_ref, attn_ref, lse_ref, dub_ref, za_ref, c_ref, s_ref, w_ref,
             dq_ref, dza_ref, dkv_own_ref, dkv_prev_ref, dsink_ref, dh_ref):
        @pl.when(pl.program_id(0) == 0)
        def _():
            dsink_ref[...] = jnp.zeros_like(dsink_ref)

        lo = lax.broadcasted_iota(jnp.int32, (STACK, 128), 1) < 64
        lane8 = lax.broadcasted_iota(jnp.int32, (8, 128), 1)
        lo2 = lax.broadcasted_iota(jnp.int32, (2 * BLK, 128), 1) < 64
        sink_row = lax.broadcasted_iota(jnp.int32, (2 * BLK, 128), 0) == 0
        dsink = jnp.zeros((8, 128), F32)
        for b in range(blocks):
            rows = slice(b * BLK, (b + 1) * BLK)
            valid = _band_mask((pl.program_id(0) * blocks + b) % seq_blocks == 0)
            k_pad = _padded_pair(*_kv_blocks(kvc_ref, kvp_ref, b, 0))
            v_pad = _padded_pair(*_kv_blocks(kvc_ref, kvp_ref, b, 128))
            c = c_ref[rows, :]
            s = s_ref[rows, :]
            dk_acc, dv_acc = [], []
            for g in range(2):
                qg, dog = [], []
                for j in range(4 * g, 4 * g + 4):
                    cols = slice(j * 128, (j + 1) * 128)
                    za = za_ref[rows, cols]
                    sg = _sig(za)
                    dub = dub_ref[rows, cols]
                    dza_ref[rows, cols] = (dub * attn_ref[rows, cols] * (sg * (1.0 + za * (1.0 - sg)))).astype(BF16)
                    dog.append((dub * (za * sg)).astype(BF16))
                    qg.append(q_ref[rows, cols])
                qg = jnp.concatenate(qg, axis=0)
                dog = jnp.concatenate(dog, axis=0)
                dq = jnp.zeros((STACK, 128), F32)
                ds_both, p_both = [], []
                for e in range(2):
                    s_masked = jnp.where(valid, _dot_nt(qg, k_pad[e][g]), _masked_fill(sink_ref, g, e))
                    lse_rows = lse_ref[b, 2 * g + e]
                    p = jnp.exp(s_masked - jnp.concatenate([lse_rows, lse_rows], axis=1))
                    dp = _dot_nt(dog, v_pad[e][g])
                    ds = p * (dp - jnp.sum(p * dp, axis=-1, keepdims=True))
                    for jj in range(4):
                        tot = jnp.sum(ds[jj * BLK:(jj + 1) * BLK, 0:1], axis=0, keepdims=True)
                        dsink = dsink + jnp.where(lane8 == 2 * (4 * g + jj) + e, tot, 0.0)
                    ds = ds.astype(BF16)
                    dq = dq + _dot(ds, k_pad[e][g])
                    ds_both.append(ds)
                    p_both.append(p.astype(BF16))
                zero = jnp.zeros_like(qg)
                q2 = jnp.concatenate([jnp.where(lo, qg, zero), jnp.where(lo, zero, qg)], axis=0)
                do2 = jnp.concatenate([jnp.where(lo, dog, zero), jnp.where(lo, zero, dog)], axis=0)
                dk_acc.append(_dot_tn(q2, jnp.concatenate(ds_both, axis=0)).T)
                dv_acc.append(_dot_tn(do2, jnp.concatenate(p_both, axis=0)).T)
                for jj in range(4):
                    cols = slice((4 * g + jj) * 128, (4 * g + jj + 1) * 128)
                    dqj = dq[jj * BLK:(jj + 1) * BLK, :] * LOGIT_SCALE
                    dq_ref[rows, cols] = (dqj * c - _swap_halves(dqj) * s).astype(BF16)
            for col, acc in ((0, dk_acc), (128, dv_acc)):
                both = jnp.where(lo2, acc[0] + pltpu.roll(acc[0], 64, 1), acc[1] + pltpu.roll(acc[1], 64, 1))
                both = jnp.where(sink_row, 0.0, both)
                dkv_prev_ref[rows, col:col + 128] = both[0:BLK, :]
                dkv_own_ref[rows, col:col + 128] = both[BLK:2 * BLK, :]
        dsink_ref[...] += dsink
        dh_ref[...] = _dot(dq_ref[...], w_ref[ROW_Q:ROW_KV, :]) + _dot(dza_ref[...], w_ref[ROW_ZA:ROW_GA, :])

    tab = pl.BlockSpec((tq, 128), lambda i: (i % (SEQ_LEN // tq), 0))
    return pl.pallas_call(
        body, name="bwd_attn", grid=(t // tq,),
        in_specs=[pl.BlockSpec(memory_space=pltpu.SMEM), _row_spec(tq, D), _row_spec(tq, 256),
                  pl.BlockSpec((BLK, 256), lambda i: (jnp.maximum(i * blocks - 1, 0), 0)),
                  _row_spec(tq, D), pl.BlockSpec((blocks, 4, STACK, 128), lambda i: (i, 0, 0, 0)),
                  _row_spec(tq, D), _row_spec(tq, D, 0), tab, tab, _whole_vmem()],
        out_specs=[_row_spec(tq, D), _row_spec(tq, D), _row_spec(tq, 256), _row_spec(tq, 256),
                   pl.BlockSpec((8, 128), lambda i: (0, 0)), _row_spec(tq, D)],
        out_shape=[jax.ShapeDtypeStruct((t, D), BF16), jax.ShapeDtypeStruct((t, D), BF16),
                   jax.ShapeDtypeStruct((t, 256), F32), jax.ShapeDtypeStruct((t, 256), F32),
                   jax.ShapeDtypeStruct((8, 128), F32), jax.ShapeDtypeStruct((t, D), F32)],
        compiler_params=_params("arbitrary"),
    )(sinks, q, kv, kv, attn, lse, dub, g3, cos_t, sin_t, wt)


def _bwd_kv_finish(dkv_own, dkv_prev, cos_t, sin_t, h, prev):
    t = dkv_own.shape[0]
    tm = SEQ_LEN
    n_t = t // tm
    seq_tiles = SEQ_LEN // tm
    n_blocks = t // BLK

    def body(own_ref, same_ref, nxt_ref, c_ref, s_ref, h_ref, o32_in, o16_in, out_ref, o32_ref, o16_ref,
             acc_ref, acc16_ref, sems):
        step = pl.program_id(0)

        @pl.when(step == 0)
        def _():
            acc_ref[...] = jnp.zeros_like(acc_ref)

        keep = jnp.where(step % seq_tiles == seq_tiles - 1, 0.0, 1.0)
        shifted = jnp.concatenate([same_ref[BLK:tm, :], nxt_ref[...] * keep], axis=0)
        tot = own_ref[...] + shifted
        dk = tot[:, 0:128]
        out_ref[:, 0:128] = (dk * c_ref[...] - _swap_halves(dk) * s_ref[...]).astype(BF16)
        out_ref[:, 128:256] = tot[:, 128:256].astype(BF16)
        acc_ref[...] += _dot_tn(out_ref[...], h_ref[...])

        @pl.when(step == n_t - 1)
        def _():
            acc16_ref[...] = acc_ref[...].astype(BF16)
            rows = pl.ds(ROW_KV, 256)
            c32 = pltpu.make_async_copy(acc_ref, o32_ref.at[rows], sems.at[0])
            c16 = pltpu.make_async_copy(acc16_ref, o16_ref.at[rows], sems.at[1])
            c32.start()
            c16.start()
            c32.wait()
            c16.wait()

    tab = pl.BlockSpec((tm, 128), lambda i: (i % seq_tiles, 0))
    hbm = pl.BlockSpec(memory_space=pl.ANY)
    out = pl.pallas_call(
        body, name="bwd_kv_finish", grid=(n_t,),
        in_specs=[_row_spec(tm, 256), _row_spec(tm, 256),
                  pl.BlockSpec((BLK, 256), lambda i: (jnp.minimum((i + 1) * (tm // BLK), n_blocks - 1), 0)), tab, tab,
                  _row_spec(tm, D), hbm, hbm],
        out_specs=[_row_spec(tm, 256), hbm, hbm],
        out_shape=[jax.ShapeDtypeStruct((t, 256), BF16), jax.ShapeDtypeStruct((D_IN, D), F32),
                   jax.ShapeDtypeStruct((D_IN, D), BF16)],
        scratch_shapes=[pltpu.VMEM((256, D), F32), pltpu.VMEM((256, D), BF16), pltpu.SemaphoreType.DMA((2,))],
        input_output_aliases={6: 1, 7: 2},
        compiler_params=_params("arbitrary"),
    )(dkv_own, dkv_prev, dkv_prev, cos_t, sin_t, h, *prev)
    return out[0], (out[1], out[2])


STAGE_ROWS = 256


def _bwd_conv(dya, a4, h, wconv8, wpc, tm, parts):
    t = a4.shape[0]
    n_t = t // tm
    sub = tm // parts
    seq_tiles = SEQ_LEN // tm

    def body(dya_ref, xc_ref, bg_ref, cg_ref, zc_ref, xcp_ref, cgp_ref, w_ref, wpc_ref, h_ref,
             da4_ref, dwpc_ref, dwc_ref, o32_ref, o16_ref, acc_ref, stage_ref, later_ref, sems):
        step = pl.program_id(0)
        tile = n_t - 1 - step

        @pl.when(step == 0)
        def _():
            dwpc_ref[...] = jnp.zeros_like(dwpc_ref)
            dwc_ref[...] = jnp.zeros_like(dwc_ref)
            acc_ref[...] = jnp.zeros_like(acc_ref)

        keep_prev = jnp.where(tile % seq_tiles == 0, 0.0, 1.0)
        ends_sequence = tile % seq_tiles == seq_tiles - 1

        def part(p, later):
            r0 = p * sub
            here = slice(r0, r0 + sub)
            if p == 0:
                u_prev = cgp_ref[14:16, :].astype(F32) * xcp_ref[14:16, :].astype(F32) * keep_prev
            else:
                u_prev = cg_ref[r0 - 2:r0, :].astype(F32) * xc_ref[r0 - 2:r0, :].astype(F32)
            xc = xc_ref[here, :].astype(F32)
            bg = bg_ref[here, :].astype(F32)
            cg = cg_ref[here, :].astype(F32)
            zc = zc_ref[here, :].astype(F32)
            u, u_m1, u_m2, yconv, sg, sz, co = _conv_forward(xc, bg, cg, zc, u_prev[0:1, :], u_prev[1:2, :], w_ref)
            ua = (sz * co).astype(BF16)
            dua = _dot_nt(dya_ref[here, :], wpc_ref[...])
            da4_ref[here, 3 * D:4 * D] = (dua * co * (sg * (1.0 + zc * (1.0 - sg)))).astype(BF16)
            dco = dua * sz
            da4_ref[here, D:2 * D] = (dco * yconv).astype(BF16)
            dyc = dco * bg
            dwc = jnp.concatenate([jnp.sum(dyc * s, axis=0, keepdims=True) for s in (u_m2, u_m1, u)], axis=0)
            rows = lax.broadcasted_iota(jnp.int32, xc.shape, 0)
            n0 = later[0:1, :]
            n1 = later[1:2, :]
            dyc_p1 = jnp.where(rows == sub - 1, n0, pltpu.roll(dyc, sub - 1, 0))
            dyc_p2 = jnp.where(rows == sub - 2, n0, jnp.where(rows == sub - 1, n1, pltpu.roll(dyc, sub - 2, 0)))
            du = w_ref[2:3, :] * dyc + w_ref[1:2, :] * dyc_p1 + w_ref[0:1, :] * dyc_p2
            da4_ref[here, 0:D] = (du * cg).astype(BF16)
            da4_ref[here, 2 * D:3 * D] = (du * xc).astype(BF16)
            return ua, dwc, dyc[0:8, :]

        later = jnp.where(ends_sequence, 0.0, later_ref[...])
        uas, dwc = [], jnp.zeros((3, D), F32)
        for p in reversed(range(parts)):
            ua, dwc_p, later = part(p, later)
            uas.insert(0, ua)
            dwc = dwc + dwc_p
        later_ref[...] = later
        dwpc_ref[...] += _dot_tn(jnp.concatenate(uas, axis=0), dya_ref[...])
        dwc_ref[0:3, :] += dwc
        for j in range(4):
            acc_ref[j * D:(j + 1) * D, :] += _dot_tn(da4_ref[:, j * D:(j + 1) * D], h_ref[...])

        @pl.when(step == n_t - 1)
        def _():
            c32 = pltpu.make_async_copy(acc_ref, o32_ref.at[pl.ds(0, 4 * D)], sems.at[0])
            c32.start()
            for j in range(4 * D // STAGE_ROWS):
                rows = pl.ds(j * STAGE_ROWS, STAGE_ROWS)
                stage_ref[...] = acc_ref[rows, :].astype(BF16)
                c16 = pltpu.make_async_copy(stage_ref, o16_ref.at[rows], sems.at[1])
                c16.start()
                c16.wait()
            c32.wait()

    def rows_of_tile(width, col=0):
        return pl.BlockSpec((tm, width), lambda s: (n_t - 1 - s, col))

    def prev(col):
        return pl.BlockSpec((16, D), lambda s: (jnp.maximum((n_t - 1 - s) * (tm // 16) - 1, 0), col))

    hbm = pl.BlockSpec(memory_space=pl.ANY)
    out = pl.pallas_call(
        body, name="bwd_conv", grid=(n_t,),
        in_specs=[rows_of_tile(D), rows_of_tile(D, 0), rows_of_tile(D, 1), rows_of_tile(D, 2), rows_of_tile(D, 3),
                  prev(0), prev(2), pl.BlockSpec((8, D), lambda s: (0, 0)), _whole_vmem(), rows_of_tile(D)],
        out_specs=[rows_of_tile(4 * D), _whole_vmem(), pl.BlockSpec((8, D), lambda s: (0, 0)), hbm, hbm],
        out_shape=[jax.ShapeDtypeStruct((t, 4 * D), BF16), jax.ShapeDtypeStruct((D, D), F32),
                   jax.ShapeDtypeStruct((8, D), F32), jax.ShapeDtypeStruct((D_IN, D), F32),
                   jax.ShapeDtypeStruct((D_IN, D), BF16)],
        scratch_shapes=[pltpu.VMEM((4 * D, D), F32), pltpu.VMEM((STAGE_ROWS, D), BF16), pltpu.VMEM((8, D), F32),
                        pltpu.SemaphoreType.DMA((2,))],
        compiler_params=pltpu.CompilerParams(dimension_semantics=("arbitrary",), vmem_limit_bytes=V7X_VMEM_BYTES - (2 << 20)),
    )(dya, a4, a4, a4, a4, a4, a4, wconv8, wpc, h)
    return out[0], out[1], out[2], (out[3], out[4])


def _bwd_dh(da4, dh_part, dkv, dgab, wt, x, g_pre, dout, tm):
    t = x.shape[0]

    def body(da4_ref, dhp_ref, dkv_ref, dgab_ref, w_ref, x_ref, g_ref, dout_ref, gx_ref, dg_ref):
        @pl.when(pl.program_id(0) == 0)
        def _():
            dg_ref[...] = jnp.zeros_like(dg_ref)

        dh = dhp_ref[...] + _dot(da4_ref[...], w_ref[0:ROW_Q, :])
        dh += _dot(dkv_ref[...], w_ref[ROW_KV:ROW_ZA, :])
        dh += _dot(dgab_ref[...], w_ref[ROW_GA:D_IN, :])
        xf = x_ref[...]
        r = lax.rsqrt(jnp.mean(xf * xf, axis=-1, keepdims=True) + RMS_EPS)
        xn = xf * r
        dg_ref[0:1, :] += jnp.sum(dh * xn, axis=0, keepdims=True)
        dxn = dh * g_ref[...]
        gx_ref[...] = dout_ref[...] + r * (dxn - xn * jnp.mean(dxn * xn, axis=-1, keepdims=True))

    return pl.pallas_call(
        body, name="bwd_dh", grid=(t // tm,),
        in_specs=[_row_spec(tm, 4 * D), _row_spec(tm, D), _row_spec(tm, 256), _row_spec(tm, 2 * D),
                  _whole_vmem(), _row_spec(tm, D), pl.BlockSpec((1, D), lambda i: (0, 0)), _row_spec(tm, D)],
        out_specs=[_row_spec(tm, D), pl.BlockSpec((8, D), lambda i: (0, 0))],
        out_shape=[jax.ShapeDtypeStruct((t, D), F32), jax.ShapeDtypeStruct((8, D), F32)],
        compiler_params=_params("arbitrary"),
    )(da4, dh_part, dkv, dgab, wt, x, g_pre, dout)


def _bwd_dw_in(h, piece, row0, nb, tm, name, prev):
    t, n = piece.shape
    n_t = t // tm

    def body(*refs):
        h_ref, p_ref = refs[0], refs[1]
        o32_ref, o16_ref, acc_ref, acc16_ref, sems = refs[-5:]
        j, i = pl.program_id(0), pl.program_id(1)

        @pl.when(i == 0)
        def _():
            acc_ref[...] = jnp.zeros_like(acc_ref)

        acc_ref[...] += _dot_tn(p_ref[...], h_ref[...])

        @pl.when(i == n_t - 1)
        def _():
            acc16_ref[...] = acc_ref[...].astype(BF16)
            rows = pl.ds(pl.multiple_of(row0 + j * nb, 16), nb)
            c32 = pltpu.make_async_copy(acc_ref, o32_ref.at[rows], sems.at[0])
            c16 = pltpu.make_async_copy(acc16_ref, o16_ref.at[rows], sems.at[1])
            c32.start()
            c16.start()
            c32.wait()
            c16.wait()

    hbm = pl.BlockSpec(memory_space=pl.ANY)
    carried = [] if prev is None else list(prev)
    return pl.pallas_call(
        body, name=name, grid=(n // nb, n_t),
        in_specs=[pl.BlockSpec((tm, D), lambda j, i: (i, 0)), pl.BlockSpec((tm, nb), lambda j, i: (i, j))]
        + [hbm] * len(carried),
        out_specs=[hbm, hbm],
        out_shape=[jax.ShapeDtypeStruct((D_IN, D), F32), jax.ShapeDtypeStruct((D_IN, D), BF16)],
        scratch_shapes=[pltpu.VMEM((nb, D), F32), pltpu.VMEM((nb, D), BF16), pltpu.SemaphoreType.DMA((2,))],
        input_output_aliases={2: 0, 3: 1} if carried else {},
        compiler_params=_params("arbitrary", "arbitrary"),
    )(h, piece, *carried)


def _place():
    x, y, c = lax.axis_index("x"), lax.axis_index("y"), lax.axis_index("c")
    return x, y, c, 4 * x + 2 * y + c


def _peer(x, y, c, k):
    return (1 - x if k & 4 else x, 1 - y if k & 2 else y, 1 - c if k & 1 else c)


ICI_MASKS = (4, 2, 6)


def _all_gather(shards):
    n = len(shards)

    def body(*refs):
        src, dst = refs[:n], refs[n:2 * n]
        send_sems, recv_sems, local_sems = refs[2 * n:]
        x, y, c, me = _place()
        sibling = _peer(x, y, c, 1)

        def copy(a, s, block, to, own=False):
            return pltpu.make_async_remote_copy(
                src_ref=src[a] if own else dst[a].at[block], dst_ref=dst[a].at[block],
                send_sem=send_sems.at[a * 7 + s], recv_sem=recv_sems.at[a * 7 + s], device_id=to, device_id_type=MESH_ID)

        local = [pltpu.make_async_copy(src[a], dst[a].at[me], local_sems.at[a]) for a in range(n)]
        for cp in local:
            cp.start()
        started = [copy(a, 0, me, sibling, own=True) for a in range(n)]
        started += [copy(a, 1 + j, me, _peer(x, y, c, k), own=True) for j, k in enumerate(ICI_MASKS) for a in range(n)]
        for cp in started:
            cp.start()
        for j, k in enumerate(ICI_MASKS):
            for a in range(n):
                copy(a, 1 + j, me ^ k, sibling).wait_recv()
                fwd = copy(a, 4 + j, me ^ k, sibling)
                fwd.start()
                started.append(fwd)
        for a in range(n):
            copy(a, 0, me ^ 1, sibling).wait_recv()
        for j, k in enumerate(ICI_MASKS):
            for a in range(n):
                copy(a, 4 + j, me ^ 1 ^ k, sibling).wait_recv()
        for cp in started:
            cp.wait_send()
        for cp in local:
            cp.wait()

    hbm = pl.BlockSpec(memory_space=pl.ANY)
    return pl.pallas_call(
        body, name="all_gather_weights",
        in_specs=[hbm] * n, out_specs=[hbm] * n,
        out_shape=[jax.ShapeDtypeStruct((N_DEV,) + s.shape, s.dtype) for s in shards],
        scratch_shapes=[pltpu.SemaphoreType.DMA((7 * n,)), pltpu.SemaphoreType.DMA((7 * n,)),
                        pltpu.SemaphoreType.DMA((n,))],
    )(*shards)


def _direct_copies(src, land, send_sems, recv_sems):
    x, y, c, me = _place()
    return [pltpu.make_async_remote_copy(
        src_ref=src[a], dst_ref=land[a].at[me], send_sem=send_sems.at[a * 7 + k - 1],
        recv_sem=recv_sems.at[a * 7 + k - 1], device_id=_peer(x, y, c, k), device_id_type=MESH_ID)
        for k in range(1, N_DEV) for a in range(len(src))]


def _gather_start(shards, name):
    n = len(shards)

    def body(*refs):
        src, land = refs[:n], refs[n:2 * n]
        send_sems, recv_sems = refs[2 * n], refs[2 * n + 1]
        token_ref = refs[-1]
        for cp in _direct_copies(src, land, send_sems, recv_sems):
            cp.start()
        token_ref[...] = jnp.zeros_like(token_ref)

    hbm = pl.BlockSpec(memory_space=pltpu.HBM)
    sem = pl.BlockSpec(memory_space=pltpu.SEMAPHORE)
    lands = [lax.empty((N_DEV,) + s.shape, s.dtype) for s in shards]
    out = pl.pallas_call(
        body, name=name + "_start",
        out_shape=(pltpu.SemaphoreType.DMA((7 * n,)), pltpu.SemaphoreType.DMA((7 * n,)),
                   *[pltpu.HBM(s.shape, s.dtype) for s in shards], *[pltpu.HBM(s.shape, s.dtype) for s in lands],
                   jax.ShapeDtypeStruct((8, 128), F32)),
        in_specs=[hbm] * (2 * n), out_specs=(sem, sem, *[hbm] * (2 * n), _whole_vmem()),
        input_output_aliases={a: 2 + a for a in range(2 * n)},
        compiler_params=pltpu.CompilerParams(has_side_effects=pltpu.SideEffectType.DATAFLOW_SIDE_EFFECTING),
    )(*[pltpu.with_memory_space_constraint(s, pltpu.HBM) for s in list(shards) + lands])
    return out[0], out[1], out[2:2 + n], out[2 + n:2 + 2 * n], out[-1]


def _gather_wait(send_sems, recv_sems, flying, lands, after, name):
    n = len(flying)

    def body(*refs):
        src, land = refs[:n], refs[n:2 * n]
        for cp in _direct_copies(src, land, refs[2 * n], refs[2 * n + 1]):
            cp.wait_send()
            cp.wait_recv()

    hbm = pl.BlockSpec(memory_space=pltpu.HBM)
    sem = pl.BlockSpec(memory_space=pltpu.SEMAPHORE)
    out = pl.pallas_call(
        body, name=name + "_wait",
        out_shape=tuple(pltpu.HBM(s.shape, s.dtype) for s in list(flying) + list(lands)),
        in_specs=[hbm] * (2 * n) + [sem, sem, pl.BlockSpec(memory_space=pl.ANY)], out_specs=tuple([hbm] * (2 * n)),
        input_output_aliases={a: a for a in range(2 * n)},
        compiler_params=pltpu.CompilerParams(has_side_effects=pltpu.SideEffectType.DATAFLOW_SIDE_EFFECTING),
    )(*flying, *lands, send_sems, recv_sems, after)
    return out[n:]


def _sibling_copies(src, land, send_sems, recv_sems, blocks):
    x, y, c, _ = _place()
    sibling = _peer(x, y, c, 1)
    return [pltpu.make_async_remote_copy(
        src_ref=src[a].at[2 * p + (1 - c)], dst_ref=land[a].at[p], send_sem=send_sems.at[a * 4 + p],
        recv_sem=recv_sems.at[a * 4 + p], device_id=sibling, device_id_type=MESH_ID)
        for a in range(len(src)) for p in blocks[a]]


def _exchange_sibling_start(by_dest, blocks):
    n = len(by_dest)

    def body(*refs):
        for cp in _sibling_copies(refs[:n], refs[n:2 * n], refs[2 * n], refs[2 * n + 1], blocks):
            cp.start()

    hbm = pl.BlockSpec(memory_space=pltpu.HBM)
    sem = pl.BlockSpec(memory_space=pltpu.SEMAPHORE)
    lands = [lax.empty((4,) + s.shape[1:], s.dtype) for s in by_dest]
    out = pl.pallas_call(
        body, name="exchange_sibling_start",
        out_shape=(pltpu.SemaphoreType.DMA((4 * n,)), pltpu.SemaphoreType.DMA((4 * n,)),
                   *[pltpu.HBM(s.shape, s.dtype) for s in by_dest], *[pltpu.HBM(s.shape, s.dtype) for s in lands]),
        in_specs=[hbm] * (2 * n), out_specs=(sem, sem, *[hbm] * (2 * n)),
        input_output_aliases={a: 2 + a for a in range(2 * n)},
        compiler_params=pltpu.CompilerParams(has_side_effects=pltpu.SideEffectType.DATAFLOW_SIDE_EFFECTING),
    )(*[pltpu.with_memory_space_constraint(s, pltpu.HBM) for s in list(by_dest) + lands])
    return out[0], out[1], out[2:2 + n], out[2 + n:]


def _exchange_sibling_last(by_dest, lands, blocks):
    n = len(by_dest)

    def body(*refs):
        copies = _sibling_copies(refs[:n], refs[n:2 * n], refs[-2], refs[-1], blocks)
        for cp in copies:
            cp.start()
        for cp in copies:
            cp.wait_recv()
        for cp in copies:
            cp.wait_send()

    hbm = pl.BlockSpec(memory_space=pl.ANY)
    return pl.pallas_call(
        body, name="exchange_sibling_last", in_specs=[hbm] * (2 * n), out_specs=[hbm] * n,
        out_shape=[jax.ShapeDtypeStruct(s.shape, s.dtype) for s in lands],
        scratch_shapes=[pltpu.SemaphoreType.DMA((4 * n,)), pltpu.SemaphoreType.DMA((4 * n,))],
        input_output_aliases={n + a: a for a in range(n)},
    )(*by_dest, *lands)


def _exchange_sibling_wait(send_sems, recv_sems, flying, lands, blocks):
    n = len(flying)

    def body(*refs):
        for cp in _sibling_copies(refs[:n], refs[n:2 * n], refs[2 * n], refs[2 * n + 1], blocks):
            cp.wait_recv()
            cp.wait_send()

    hbm = pl.BlockSpec(memory_space=pltpu.HBM)
    sem = pl.BlockSpec(memory_space=pltpu.SEMAPHORE)
    out = pl.pallas_call(
        body, name="exchange_sibling_wait",
        out_shape=tuple(pltpu.HBM(s.shape, s.dtype) for s in list(flying) + list(lands)),
        in_specs=[hbm] * (2 * n) + [sem, sem], out_specs=tuple([hbm] * (2 * n)),
        input_output_aliases={a: a for a in range(2 * n)},
        compiler_params=pltpu.CompilerParams(has_side_effects=pltpu.SideEffectType.DATAFLOW_SIDE_EFFECTING),
    )(*[pltpu.with_memory_space_constraint(s, pltpu.HBM) for s in list(flying) + list(lands)], send_sems, recv_sems)
    return out[n:]


def _chip_copies(src, land, send_sems, recv_sems):
    x, y, c, _ = _place()
    chip = 2 * x + y
    return [pltpu.make_async_remote_copy(
        src_ref=src[a].at[chip ^ (k >> 1)], dst_ref=land[a].at[j], send_sem=send_sems.at[a * 3 + j],
        recv_sem=recv_sems.at[a * 3 + j], device_id=_peer(x, y, c, k), device_id_type=MESH_ID)
        for j, k in enumerate(ICI_MASKS) for a in range(len(src))]


def _exchange_chips_start(by_chip):
    n = len(by_chip)

    def body(*refs):
        src, land = refs[:n], refs[n:2 * n]
        send_sems, recv_sems = refs[2 * n], refs[2 * n + 1]
        token_ref = refs[-1]
        for cp in _chip_copies(src, land, send_sems, recv_sems):
            cp.start()
        token_ref[...] = jnp.zeros_like(token_ref)

    hbm = pl.BlockSpec(memory_space=pltpu.HBM)
    sem = pl.BlockSpec(memory_space=pltpu.SEMAPHORE)
    lands = [lax.empty((3,) + s.shape[1:], s.dtype) for s in by_chip]
    out = pl.pallas_call(
        body, name="exchange_chips_start",
        out_shape=(pltpu.SemaphoreType.DMA((3 * n,)), pltpu.SemaphoreType.DMA((3 * n,)),
                   *[pltpu.HBM(s.shape, s.dtype) for s in by_chip], *[pltpu.HBM(s.shape, s.dtype) for s in lands],
                   jax.ShapeDtypeStruct((8, 128), F32)),
        in_specs=[hbm] * (2 * n), out_specs=(sem, sem, *[hbm] * (2 * n), _whole_vmem()),
        input_output_aliases={a: 2 + a for a in range(2 * n)},
        compiler_params=pltpu.CompilerParams(has_side_effects=pltpu.SideEffectType.DATAFLOW_SIDE_EFFECTING),
    )(*[pltpu.with_memory_space_constraint(s, pltpu.HBM) for s in list(by_chip) + lands])
    return out[0], out[1], out[2:2 + n], out[2 + n:2 + 2 * n], out[-1]


def _exchange_chips_wait(send_sems, recv_sems, flying, lands, after):
    n = len(flying)

    def body(*refs):
        src, land = refs[:n], refs[n:2 * n]
        send_sems_ref, recv_sems_ref = refs[2 * n], refs[2 * n + 1]
        for cp in _chip_copies(src, land, send_sems_ref, recv_sems_ref):
            cp.wait_send()
            cp.wait_recv()

    hbm = pl.BlockSpec(memory_space=pltpu.HBM)
    sem = pl.BlockSpec(memory_space=pltpu.SEMAPHORE)
    out = pl.pallas_call(
        body, name="exchange_chips_wait",
        out_shape=tuple(pltpu.HBM(s.shape, s.dtype) for s in list(flying) + list(lands)),
        in_specs=[hbm] * (2 * n) + [sem, sem, pl.BlockSpec(memory_space=pl.ANY)], out_specs=tuple([hbm] * (2 * n)),
        input_output_aliases={a: a for a in range(2 * n)},
        compiler_params=pltpu.CompilerParams(has_side_effects=pltpu.SideEffectType.DATAFLOW_SIDE_EFFECTING),
    )(*flying, *lands, send_sems, recv_sems, after)
    return out[n:]


def _adamw_math(w, g, m, v):
    m = ADAM_B1 * m + (1.0 - ADAM_B1) * g
    v = ADAM_B2 * v + (1.0 - ADAM_B2) * (g * g)
    m_hat = m / (1.0 - ADAM_B1 ** ADAM_STEP)
    v_hat = v / (1.0 - ADAM_B2 ** ADAM_STEP)
    return -ADAM_LR * (m_hat / (jnp.sqrt(v_hat) + ADAM_EPS) + ADAM_WD * w), m, v


def _pair_sum(owns, recvs, place_arr, tr, name):
    n = len(owns)
    _, rows, cols = owns[0].shape

    def body(place_ref, *refs):
        for a in range(n):
            s = refs[a][...] + refs[n + a][...].astype(F32)
            refs[3 * n + a][...] = s.astype(BF16)

            @pl.when(pl.program_id(1) == place_ref[1])
            def _(a=a, s=s):
                refs[2 * n + a][...] = s

    by_chip = pl.BlockSpec((None, tr, cols), lambda i, p, place_ref: (p, i, 0))
    mine = pl.BlockSpec((None, tr, cols), lambda i, p, place_ref: (2 * p + place_ref[0], i, 0))
    kept = pl.BlockSpec((tr, cols), lambda i, p, place_ref: (i, 0))
    out = pl.pallas_call(
        body, name=name,
        grid_spec=pltpu.PrefetchScalarGridSpec(
            num_scalar_prefetch=1, grid=(rows // tr, 4), in_specs=[mine] * n + [by_chip] * n,
            out_specs=[kept] * n + [by_chip] * n),
        out_shape=[jax.ShapeDtypeStruct((rows, cols), F32)] * n + [jax.ShapeDtypeStruct((4, rows, cols), BF16)] * n,
        compiler_params=_params("parallel", "arbitrary"),
    )(place_arr, *owns, *recvs)
    return out[:n], out[n:]


def _chip_sum(pairs, recvs, tr, name, adam=None):
    n = len(pairs)
    rows, cols = pairs[0].shape
    n_state = 0 if adam is None else 3 * n

    def body(*refs):
        outs = refs[2 * n + n_state:]
        for a in range(n):
            g = refs[a][...]
            for j in range(3):
                g = g + refs[n + a][j].astype(F32)
            outs[a][...] = g
            if adam is not None:
                w_ref, m_ref, v_ref = (refs[2 * n + s * n + a] for s in range(3))
                outs[n + a][...], outs[2 * n + a][...], outs[3 * n + a][...] = _adamw_math(w_ref[...], g, m_ref[...], v_ref[...])

    blk = pl.BlockSpec((tr, cols), lambda i: (i, 0))
    n_out = n if adam is None else 4 * n
    out = pl.pallas_call(
        body, name=name, grid=(rows // tr,),
        in_specs=[blk] * n + [pl.BlockSpec((3, tr, cols), lambda i: (0, i, 0))] * n + [blk] * n_state,
        out_specs=[blk] * n_out,
        out_shape=[jax.ShapeDtypeStruct((rows, cols), F32)] * n_out,
        compiler_params=_params("parallel"),
    )(*pairs, *recvs, *([] if adam is None else [t for group in adam for t in group]))
    return out if adam is None else (out[:n], out[n:2 * n], out[2 * n:3 * n], out[3 * n:])


def _adamw(ws, gs, ms, vs, name):
    n = len(ws)

    def body(*refs):
        for a in range(n):
            w_ref, g_ref, m_ref, v_ref = (refs[s * n + a] for s in range(4))
            refs[4 * n + a][...], refs[5 * n + a][...], refs[6 * n + a][...] = _adamw_math(
                w_ref[...], g_ref[...], m_ref[...], v_ref[...])

    out = pl.pallas_call(body, name=name, out_shape=[jax.ShapeDtypeStruct(w.shape, F32) for w in ws] * 3)(
        *ws, *gs, *ms, *vs)
    return out[:n], out[n:2 * n], out[2 * n:]


def _sum_small(small_all):
    def body(s_ref, o_ref):
        g = s_ref[0]
        for d in range(1, N_DEV):
            g = g + s_ref[d]
        o_ref[...] = g

    return pl.pallas_call(body, name="sum_small", out_shape=jax.ShapeDtypeStruct(small_all.shape[1:], F32))(small_all)


def _rope_tables():
    inv_freq = ROPE_THETA ** (-jnp.arange(0, HEAD_DIM, 2, dtype=F32) / HEAD_DIM)
    ang = jnp.arange(SEQ_LEN).astype(F32)[:, None] * inv_freq[None, :]
    cos, sin = jnp.cos(ang), jnp.sin(ang)
    return jnp.tile(cos, (1, 4)), jnp.tile(jnp.concatenate([-sin, sin], axis=1), (1, 2))


def _local_step(x, target, g_pre, g_post, sinks, wt, wconv, squares, start_exchange=None):
    cos_t, sin_t = _rope_tables()
    wconv8 = jnp.pad(wconv, ((0, 5), (0, 0)))
    h, q, kv, g3 = _fwd_in_attn(x, g_pre, wt, cos_t, sin_t, 512)
    wpc, wpa, wout = squares(kv)
    a4, ya = _fwd_in_conv(h, wt, wconv8, wpc, 512)
    attn, ub, lse = _fwd_attn(sinks, q, kv, g3, 4)
    loss8, dout, dya, dub, dgab, dwout, dwpa, dgpost8 = _fwd_out_bwd_head(ya, ub, g3, x, target, g_post, wpa, wout, 512)
    dq, dza, dkv_own, dkv_prev, dsink8, dh_part = _bwd_attn(sinks, q, kv, attn, lse, dub, g3, cos_t, sin_t, wt, 4)
    da4, dwpc, dwconv8, dwt = _bwd_conv(dya, a4, h, wconv8, wpc, 512, 2)
    dkv, dwt = _bwd_kv_finish(dkv_own, dkv_prev, cos_t, sin_t, h, dwt)
    dwt = _bwd_dw_in(h, dq, ROW_Q, 1024, 1024, "bwd_dw_in_q", dwt)
    dwt = _bwd_dw_in(h, dza, ROW_ZA, 1024, 1024, "bwd_dw_in_za", dwt)
    sent = None
    if start_exchange is not None:
        dwt, sent = start_exchange[0](dwt, dwpc, dwpa, dwout)
    dwt32, dwt16 = _bwd_dw_in(h, dgab, ROW_GA, 1024, 1024, "bwd_dw_in_gates", dwt)
    token, pending = (None, None) if start_exchange is None else start_exchange[1](dwt32, dwt16, dwpc, dwpa, dwout, sent)
    g_pre_after = g_pre if token is None else g_pre + token[0:1, 0:1]
    grad_x, dgpre8 = _bwd_dh(da4, dh_part, dkv, dgab, wt, x, g_pre_after, dout, 512)
    small = jnp.concatenate([dgpre8, dgpost8, jnp.pad(dsink8, ((0, 0), (0, D - 128))), dwconv8,
                             jnp.pad(loss8, ((0, 0), (0, D - 128)))], axis=0)
    return loss8[0, 0], grad_x, dwt32, dwt16, dwpc, dwpa, dwout, small, pending


def kernel(x, g_pre, g_post, w_in, w_conv, sinks, w_proj_conv, w_proj_attn, w_out, loss_target, m_g_pre, m_g_post, m_w_in, m_w_conv, m_sinks, m_w_proj_conv, m_w_proj_attn, m_w_out, v_g_pre, v_g_post, v_w_in, v_w_conv, v_sinks, v_w_proj_conv, v_w_proj_attn, v_w_out):
    batch = x.shape[0]
    mx, my, mc, me = _place()
    place_arr = jnp.stack([mc, 2 * mx + my]).astype(jnp.int32)

    g_wt, g_conv = _all_gather([w_in[0].T.astype(BF16), jnp.pad(w_conv[0], ((0, 5), (0, 0)))])
    wt = g_wt.reshape(D_IN, D)
    wconv = g_conv[:, 0:3, :].transpose(1, 0, 2).reshape(3, D)
    sq_mine = [w.astype(BF16) for w in (w_proj_conv[0], w_proj_attn[0], w_out[0])]
    wt, sq_mine = lax.optimization_barrier((wt, sq_mine))
    sq_send, sq_recv, sq_flying, sq_lands, sq_token = _gather_start(sq_mine, "gather_squares")

    def squares(after):
        got = _gather_wait(sq_send, sq_recv, sq_flying, sq_lands, after, "gather_squares")
        return [lax.dynamic_update_index_in_dim(full, mine, me, 0).reshape(D, D) for full, mine in zip(got, sq_mine)]

    early = [(0, 1, 2)] + [(0, 1, 2, 3)] * 3
    late = [(3,)]

    def send_early(dwt, dwpc, dwpa, dwout):
        sq16 = [g.astype(BF16).reshape(N_DEV, SHARD_SQ, D) for g in (dwpc, dwpa, dwout)]
        send_sems, recv_sems, flying, lands = _exchange_sibling_start([dwt[1].reshape(N_DEV, SHARD_IN, D)] + sq16, early)
        return (dwt[0], flying[0].reshape(D_IN, D)), (send_sems, recv_sems, flying[1:], lands)

    def send_late(dwt32, dwt16, dwpc, dwpa, dwout, sent):
        own_sq = [g.reshape(N_DEV, SHARD_SQ, D) for g in (dwpc, dwpa, dwout)]
        own_in = dwt32.reshape(N_DEV, SHARD_IN, D)
        send_sems, recv_sems, sq_flying, lands = sent
        dwt16 = dwt16.reshape(N_DEV, SHARD_IN, D)
        lands = list(_exchange_sibling_last([dwt16], lands[:1], late)) + list(lands[1:])
        from_sibling = _exchange_sibling_wait(send_sems, recv_sems, [dwt16] + list(sq_flying), lands, early)
        in32, in16 = _pair_sum([own_in], from_sibling[:1], place_arr, SHARD_IN // 2, "pair_sum_w_in")
        sq32, sq16 = _pair_sum(own_sq, from_sibling[1:], place_arr, SHARD_SQ, "pair_sum_squares")
        send_sems, recv_sems, flying, lands, token = _exchange_chips_start(list(in16) + list(sq16))
        return token, (send_sems, recv_sems, flying, lands, in32, sq32)

    _, grad_x, _, _, _, _, _, small, pending = _local_step(
        x.reshape(batch * SEQ_LEN, D), loss_target.reshape(batch * SEQ_LEN, D), g_pre + sq_token[0:1, 0:1], g_post,
        sinks, wt, wconv, squares, (send_early, send_late))
    sm_send, sm_recv, sm_flying, sm_lands, sm_token = _gather_start([small], "gather_small")
    send_sems, recv_sems, flying, lands, in32, sq32 = pending
    from_chips = _exchange_chips_wait(send_sems, recv_sems, flying, lands, sm_token)

    o_in = [o[0].T for o in _chip_sum(
        in32, from_chips[:1], SHARD_IN // 3, "chip_sum_adamw_w_in",
        adam=([w_in[0].T], [m_w_in[0].T], [v_w_in[0].T]))]
    g_in_mine, o_in = o_in[0], o_in[1:]
    g_sq, d_sq, m_sq, v_sq = _chip_sum(
        sq32, from_chips[1:], SHARD_SQ, "chip_sum_adamw_squares",
        adam=([w_proj_conv[0], w_proj_attn[0], w_out[0]], [m_w_proj_conv[0], m_w_proj_attn[0], m_w_out[0]],
              [v_w_proj_conv[0], v_w_proj_attn[0], v_w_out[0]]))
    both_done, g_in_mine = lax.optimization_barrier((d_sq[0], g_in_mine))
    (small_all,) = _gather_wait(sm_send, sm_recv, sm_flying, sm_lands, both_done, "gather_small")
    gs = _sum_small(lax.dynamic_update_index_in_dim(small_all, small, me, 0))
    g_g_pre, g_g_post, g_sinks, loss = gs[0:1], gs[8:9], gs[16:17, 0:N_HEADS], gs[32, 0]
    g_conv_mine = lax.dynamic_slice_in_dim(gs[24:27], me * SHARD_SQ, SHARD_SQ, axis=1)
    o_small = _adamw([g_pre, g_post, sinks, w_conv[0]], [g_g_pre, g_g_post, g_sinks, g_conv_mine],
                     [m_g_pre, m_g_post, m_sinks, m_w_conv[0]], [v_g_pre, v_g_post, v_sinks, v_w_conv[0]], "adamw_small")

    grads = [g_g_pre, g_g_post, g_in_mine[None], g_conv_mine[None], g_sinks] + [g[None] for g in g_sq]
    rest = []
    for idx, sq in enumerate((d_sq, m_sq, v_sq)):
        gp, gq, sk, cv = o_small[idx]
        rest += [gp, gq, o_in[idx][None], cv[None], sk] + [s[None] for s in sq]
    return (loss, grad_x.reshape(batch, SEQ_LEN, D), *grads, *rest)
```

```python
import jax
import jax.numpy as jnp
from jax import lax
from jax.experimental import pallas as pl
from jax.experimental.pallas import tpu as pltpu

D = 1024
N_HEADS = 16
HEAD_DIM = 64
LOGIT_SCALE = HEAD_DIM ** -0.5
BLK = 128
SEQ_LEN = 2048
D_IN = 8448
ROW_Q, ROW_KV, ROW_ZA, ROW_GA = 4 * D, 5 * D, 5 * D + 256, 6 * D + 256
SHARD_IN = D_IN // 8
SHARD_SQ = D // 8
N_DEV = 8
V7X_VMEM_BYTES = 64 << 20
ROPE_THETA = 10000.0
RMS_EPS = 1e-6
NEG = -1e30
ADAM_LR, ADAM_B1, ADAM_B2, ADAM_EPS, ADAM_WD, ADAM_STEP = 0.001, 0.9, 0.999, 1e-08, 0.01, 10

F32 = jnp.float32
BF16 = jnp.bfloat16
MESH_ID = pl.DeviceIdType.MESH


def _dot(a, b):
    return jnp.dot(a, b, preferred_element_type=F32)


def _dot_nt(a, b):
    return lax.dot_general(a, b, (((1,), (1,)), ((), ())), preferred_element_type=F32)


def _dot_tn(a, b):
    return lax.dot_general(a, b, (((0,), (0,)), ((), ())), preferred_element_type=F32)


def _sig(z):
    return 1.0 / (1.0 + jnp.exp(-z))


def _swap_halves(z):
    lane = lax.broadcasted_iota(jnp.int32, z.shape, 1)
    return jnp.where((lane & 63) < 32, pltpu.roll(z, 96, 1), pltpu.roll(z, 32, 1))


def _row_spec(tm, width, col=0):
    return pl.BlockSpec((tm, width), lambda i: (i, col))


def _whole_vmem():
    return pl.BlockSpec(memory_space=pltpu.VMEM)


def _params(*sem, vmem_limit_bytes=None):
    return pltpu.CompilerParams(dimension_semantics=sem, vmem_limit_bytes=vmem_limit_bytes)


def _fwd_in_attn(x, g_pre, wt, cos_t, sin_t, tm):
    t = x.shape[0]
    seq_tiles = SEQ_LEN // tm

    def body(x_ref, g_ref, w_ref, c_ref, s_ref, h_ref, q_ref, kv_ref, g3_ref):
        xf = x_ref[...]
        r = lax.rsqrt(jnp.mean(xf * xf, axis=-1, keepdims=True) + RMS_EPS)
        hh = ((xf * r) * g_ref[...]).astype(BF16)
        h_ref[...] = hh
        c = c_ref[...]
        s = s_ref[...]

        def rope(z):
            return z * c + _swap_halves(z) * s

        q = _dot_nt(hh, w_ref[ROW_Q:ROW_Q + D, :])
        for j in range(D // 128):
            q_ref[:, j * 128:(j + 1) * 128] = (rope(q[:, j * 128:(j + 1) * 128]) * LOGIT_SCALE).astype(BF16)
        kv = _dot_nt(hh, w_ref[ROW_KV:ROW_KV + 256, :])
        kv_ref[:, 0:128] = rope(kv[:, 0:128]).astype(BF16)
        kv_ref[:, 128:256] = kv[:, 128:256].astype(BF16)
        for j in range(3):
            g3_ref[:, j * D:(j + 1) * D] = _dot_nt(hh, w_ref[ROW_ZA + j * D:ROW_ZA + (j + 1) * D, :])

    tab = pl.BlockSpec((tm, 128), lambda i: (i % seq_tiles, 0))
    return pl.pallas_call(
        body, name="fwd_in_attn", grid=(t // tm,),
        in_specs=[_row_spec(tm, D), pl.BlockSpec((1, D), lambda i: (0, 0)), _whole_vmem(), tab, tab],
        out_specs=[_row_spec(tm, D), _row_spec(tm, D), _row_spec(tm, 256), _row_spec(tm, 3 * D)],
        out_shape=[jax.ShapeDtypeStruct((t, D), BF16), jax.ShapeDtypeStruct((t, D), BF16),
                   jax.ShapeDtypeStruct((t, 256), BF16), jax.ShapeDtypeStruct((t, 3 * D), F32)],
        compiler_params=_params("parallel"),
    )(x, g_pre, wt, cos_t, sin_t)


def _conv_forward(xc, bg, cg, zc, up6, up7, w_ref):
    rows = lax.broadcasted_iota(jnp.int32, xc.shape, 0)
    u = cg * xc
    u_m1 = jnp.where(rows == 0, up7, pltpu.roll(u, 1, 0))
    u_m2 = jnp.where(rows == 0, up6, jnp.where(rows == 1, up7, pltpu.roll(u, 2, 0)))
    yconv = w_ref[0:1, :] * u_m2 + w_ref[1:2, :] * u_m1 + w_ref[2:3, :] * u
    sg = _sig(zc)
    sz = zc * sg
    co = bg * yconv
    return u, u_m1, u_m2, yconv, sg, sz, co


def _fwd_in_conv(h, wt, wconv8, wpc, tm):
    t = h.shape[0]
    seq_tiles = SEQ_LEN // tm

    def body(h_ref, w_ref, wc_ref, wpc_ref, a4_ref, ya_ref, last_u_ref):
        hh = h_ref[...]
        xc, bg, cg, zc = (_dot_nt(hh, w_ref[j * D:(j + 1) * D, :]) for j in range(4))
        for j, z in enumerate((xc, bg, cg, zc)):
            a4_ref[:, j * D:(j + 1) * D] = z.astype(BF16)
        first = pl.program_id(0) % seq_tiles == 0
        up6 = jnp.where(first, 0.0, last_u_ref[6:7, :])
        up7 = jnp.where(first, 0.0, last_u_ref[7:8, :])
        u, _, _, _, _, sz, co = _conv_forward(xc, bg, cg, zc, up6, up7, wc_ref)
        last_u_ref[...] = u[tm - 8:tm, :]
        ya_ref[...] = _dot((sz * co).astype(BF16), wpc_ref[...])

    return pl.pallas_call(
        body, name="fwd_in_conv", grid=(t // tm,),
        in_specs=[_row_spec(tm, D), _whole_vmem(), pl.BlockSpec((8, D), lambda i: (0, 0)), _whole_vmem()],
        out_specs=[_row_spec(tm, 4 * D), _row_spec(tm, D)],
        out_shape=[jax.ShapeDtypeStruct((t, 4 * D), BF16), jax.ShapeDtypeStruct((t, D), F32)],
        scratch_shapes=[pltpu.VMEM((8, D), F32)],
        compiler_params=_params("arbitrary"),
    )(h, wt, wconv8, wpc)


STACK = 4 * BLK


def _band_mask(first):
    qi = lax.broadcasted_iota(jnp.int32, (STACK, 2 * BLK), 0) & (BLK - 1)
    kj = lax.broadcasted_iota(jnp.int32, (STACK, 2 * BLK), 1)
    return (kj > qi) & (kj <= qi + BLK) & (kj >= jnp.where(first, BLK, 0))


def _masked_fill(sink_ref, g, e):
    kj = lax.broadcasted_iota(jnp.int32, (STACK, 2 * BLK), 1)
    sink = jnp.concatenate([jnp.full((BLK, 2 * BLK), sink_ref[0, 2 * (4 * g + jj) + e], F32) for jj in range(4)], axis=0)
    return jnp.where(kj == 0, sink, NEG)


def _padded_pair(before, own):
    z = jnp.concatenate([before, own], axis=0).astype(F32)
    z = jnp.where(lax.broadcasted_iota(jnp.int32, z.shape, 0) == 0, 0.0, z)
    zs = pltpu.roll(z, 64, 1)
    lo = lax.broadcasted_iota(jnp.int32, z.shape, 1) < 64
    zero = jnp.zeros_like(z)
    left = [jnp.where(lo, z, zero).astype(BF16), jnp.where(lo, zs, zero).astype(BF16)]
    right = [jnp.where(lo, zero, zs).astype(BF16), jnp.where(lo, zero, z).astype(BF16)]
    return left, right


def _exp_logits(s, valid, fill):
    s = jnp.where(valid, s, fill)
    m = jnp.max(s, axis=-1, keepdims=True)
    return jnp.exp(s - m), m


def _kv_blocks(kvc_ref, kvp_ref, b, col):
    own = kvc_ref[b * BLK:(b + 1) * BLK, col:col + 128]
    before = kvp_ref[:, col:col + 128] if b == 0 else kvc_ref[(b - 1) * BLK:b * BLK, col:col + 128]
    return before, own


def _fwd_attn(sinks, q, kv, g3, blocks):
    t = q.shape[0]
    tq = blocks * BLK
    seq_blocks = SEQ_LEN // BLK

    def body(sink_ref, q_ref, kvc_ref, kvp_ref, za_ref, attn_ref, ub_ref, lse_ref):
        lo = lax.broadcasted_iota(jnp.int32, (STACK, 128), 1) < 64
        for b in range(blocks):
            rows = slice(b * BLK, (b + 1) * BLK)
            valid = _band_mask((pl.program_id(0) * blocks + b) % seq_blocks == 0)
            k_pad = _padded_pair(*_kv_blocks(kvc_ref, kvp_ref, b, 0))
            v_pad = _padded_pair(*_kv_blocks(kvc_ref, kvp_ref, b, 128))
            for g in range(2):
                qg = jnp.concatenate([q_ref[rows, j * 128:(j + 1) * 128] for j in range(4 * g, 4 * g + 4)], axis=0)
                pv, den = [], []
                for e in range(2):
                    p, m = _exp_logits(_dot_nt(qg, k_pad[e][g]), valid, _masked_fill(sink_ref, g, e))
                    both = _dot(p.astype(BF16), jnp.concatenate([v_pad[e][g], jnp.ones((2 * BLK, 128), BF16)], axis=1))
                    pv.append(both[:, 0:128])
                    den.append(both[:, 128:256])
                    lse_ref[b, 2 * g + e] = m + jnp.log(den[e])
                o = jnp.where(lo, pv[0] / den[0], pv[1] / den[1])
                for jj in range(4):
                    cols = slice((4 * g + jj) * 128, (4 * g + jj + 1) * 128)
                    oj = o[jj * BLK:(jj + 1) * BLK, :]
                    attn_ref[rows, cols] = oj
                    za = za_ref[rows, cols]
                    ub_ref[rows, cols] = (za * _sig(za) * oj).astype(BF16)

    return pl.pallas_call(
        body, name="fwd_attn", grid=(t // tq,),
        in_specs=[pl.BlockSpec(memory_space=pltpu.SMEM), _row_spec(tq, D), _row_spec(tq, 256),
                  pl.BlockSpec((BLK, 256), lambda i: (jnp.maximum(i * blocks - 1, 0), 0)), _row_spec(tq, D, 0)],
        out_specs=[_row_spec(tq, D), _row_spec(tq, D), pl.BlockSpec((blocks, 4, STACK, 128), lambda i: (i, 0, 0, 0))],
        out_shape=[jax.ShapeDtypeStruct((t, D), F32), jax.ShapeDtypeStruct((t, D), BF16),
                   jax.ShapeDtypeStruct((t // BLK, 4, STACK, 128), F32)],
        compiler_params=_params("parallel"),
    )(sinks, q, kv, kv, g3)


def _fwd_out_bwd_head(ya, ub, g3, x, target, g_post, wpa, wout, tm):
    t = x.shape[0]

    def body(ya_ref, ub_ref, ga_ref, gb_ref, x_ref, tgt_ref, gp_ref, wpa_ref, wout_ref,
             loss_ref, dout_ref, dya_ref, dub_ref, dgab_ref, dwout_ref, dwpa_ref, dgp_ref):
        @pl.when(pl.program_id(0) == 0)
        def _():
            loss_ref[...] = jnp.zeros_like(loss_ref)
            dwout_ref[...] = jnp.zeros_like(dwout_ref)
            dwpa_ref[...] = jnp.zeros_like(dwpa_ref)
            dgp_ref[...] = jnp.zeros_like(dgp_ref)

        g = gp_ref[...]
        halves = (slice(0, tm // 2), slice(tm // 2, tm))

        def stage1(rows):
            return _dot(ub_ref[rows, :], wpa_ref[...])

        def stage2(rows, yb):
            sa = _sig(ga_ref[rows, :])
            sb = _sig(gb_ref[rows, :])
            mb = (sa * ya_ref[rows, :] + sb * yb).astype(BF16)
            return sa, sb, mb, _dot(mb, wout_ref[...])

        def stage3(rows, y):
            r = lax.rsqrt(jnp.mean(y * y, axis=-1, keepdims=True) + RMS_EPS)
            n = y * r
            err = (x_ref[rows, :] + n * g) - tgt_ref[rows, :]
            sq = jnp.sum(jnp.sum(err * err, axis=0, keepdims=True), axis=1, keepdims=True)
            dout = err * (1.0 / D)
            dout_ref[rows, :] = dout
            dgp = jnp.sum(dout * n, axis=0, keepdims=True)
            dn = dout * g
            dy = (r * (dn - n * jnp.mean(dn * n, axis=-1, keepdims=True))).astype(BF16)
            return sq, dgp, dy, _dot_nt(dy, wout_ref[...])

        def stage4(rows, dm, sa, sb, yb):
            dya_ref[rows, :] = (dm * sa).astype(BF16)
            dyb = (dm * sb).astype(BF16)
            dgab_ref[rows, 0:D] = (dm * ya_ref[rows, :] * (sa * (1.0 - sa))).astype(BF16)
            dgab_ref[rows, D:2 * D] = (dm * yb * (sb * (1.0 - sb))).astype(BF16)
            dub_ref[rows, :] = _dot_nt(dyb, wpa_ref[...])
            return dyb

        yb = [stage1(rows) for rows in halves]
        s2 = [stage2(rows, yb[k]) for k, rows in enumerate(halves)]
        s3 = [stage3(rows, s2[k][3]) for k, rows in enumerate(halves)]
        dyb = [stage4(rows, s3[k][3], s2[k][0], s2[k][1], yb[k]) for k, rows in enumerate(halves)]
        loss_ref[...] += sum(s[0] for s in s3) * (0.5 / D)
        dgp_ref[0:1, :] += sum(s[1] for s in s3)
        dwout_ref[...] += _dot_tn(jnp.concatenate([s[2] for s in s2], axis=0), jnp.concatenate([s[2] for s in s3], axis=0))
        dwpa_ref[...] += _dot_tn(ub_ref[...], jnp.concatenate(dyb, axis=0))

    return pl.pallas_call(
        body, name="fwd_out_bwd_head", grid=(t // tm,),
        in_specs=[_row_spec(tm, D), _row_spec(tm, D), _row_spec(tm, D, 1), _row_spec(tm, D, 2),
                  _row_spec(tm, D), _row_spec(tm, D), pl.BlockSpec((1, D), lambda i: (0, 0)),
                  _whole_vmem(), _whole_vmem()],
        out_specs=[pl.BlockSpec((8, 128), lambda i: (0, 0)), _row_spec(tm, D), _row_spec(tm, D), _row_spec(tm, D),
                   _row_spec(tm, 2 * D), _whole_vmem(), _whole_vmem(), pl.BlockSpec((8, D), lambda i: (0, 0))],
        out_shape=[jax.ShapeDtypeStruct((8, 128), F32), jax.ShapeDtypeStruct((t, D), F32),
                   jax.ShapeDtypeStruct((t, D), BF16), jax.ShapeDtypeStruct((t, D), F32),
                   jax.ShapeDtypeStruct((t, 2 * D), BF16), jax.ShapeDtypeStruct((D, D), F32),
                   jax.ShapeDtypeStruct((D, D), F32), jax.ShapeDtypeStruct((8, D), F32)],
        compiler_params=_params("arbitrary", vmem_limit_bytes=V7X_VMEM_BYTES - (2 << 20)),
    )(ya, ub, g3, g3, x, target, g_post, wpa, wout)


def _bwd_attn(sinks, q, kv, attn, lse, dub, g3, cos_t, sin_t, wt, blocks):
    t = q.shape[0]
    tq = blocks * BLK
    seq_blocks = SEQ_LEN // BLK

    def body(sink_ref, q_ref, kvc_ref, kvp_ref, attn_ref, lse_ref, dub_ref, za_ref, c_ref, s_ref, w_ref,
             dq_ref, dza_ref, dkv_own_ref, dkv_prev_ref, dsink_ref, dh_ref):
        @pl.when(pl.program_id(0) == 0)
        def _():
            dsink_ref[...] = jnp.zeros_like(dsink_ref)

        lo = lax.broadcasted_iota(jnp.int32, (STACK, 128), 1) < 64
        lane8 = lax.broadcasted_iota(jnp.int32, (8, 128), 1)
        lo2 = lax.broadcasted_iota(jnp.int32, (2 * BLK, 128), 1) < 64
        sink_row = lax.broadcasted_iota(jnp.int32, (2 * BLK, 128), 0) == 0
        dsink = jnp.zeros((8, 128), F32)
        for b in range(blocks):
            rows = slice(b * BLK, (b + 1) * BLK)
            valid = _band_mask((pl.program_id(0) * blocks + b) % seq_blocks == 0)
            k_pad = _padded_pair(*_kv_blocks(kvc_ref, kvp_ref, b, 0))
            v_pad = _padded_pair(*_kv_blocks(kvc_ref, kvp_ref, b, 128))
            c = c_ref[rows, :]
            s = s_ref[rows, :]
            dk_acc, dv_acc = [], []
            for g in range(2):
                qg, dog = [], []
                for j in range(4 * g, 4 * g + 4):
                    cols = slice(j * 128, (j + 1) * 128)
                    za = za_ref[rows, cols]
                    sg = _sig(za)
                    dub = dub_ref[rows, cols]
                    dza_ref[rows, cols] = (dub * attn_ref[rows, cols] * (sg * (1.0 + za * (1.0 - sg)))).astype(BF16)
                    dog.append((dub * (za * sg)).astype(BF16))
                    qg.append(q_ref[rows, cols])
                qg = jnp.concatenate(qg, axis=0)
                dog = jnp.concatenate(dog, axis=0)
                dq = jnp.zeros((STACK, 128), F32)
                ds_both, p_both = [], []
                for e in range(2):
                    s_masked = jnp.where(valid, _dot_nt(qg, k_pad[e][g]), _masked_fill(sink_ref, g, e))
                    lse_rows = lse_ref[b, 2 * g + e]
                    p = jnp.exp(s_masked - jnp.concatenate([lse_rows, lse_rows], axis=1))
                    dp = _dot_nt(dog, v_pad[e][g])
                    ds = p * (dp - jnp.sum(p * dp, axis=-1, keepdims=True))
                    for jj in range(4):
                        tot = jnp.sum(ds[jj * BLK:(jj + 1) * BLK, 0:1], axis=0, keepdims=True)
                        dsink = dsink + jnp.where(lane8 == 2 * (4 * g + jj) + e, tot, 0.0)
                    ds = ds.astype(BF16)
                    dq = dq + _dot(ds, k_pad[e][g])
                    ds_both.append(ds)
                    p_both.append(p.astype(BF16))
                zero = jnp.zeros_like(qg)
                q2 = jnp.concatenate([jnp.where(lo, qg, zero), jnp.where(lo, zero, qg)], axis=0)
                do2 = jnp.concatenate([jnp.where(lo, dog, zero), jnp.where(lo, zero, dog)], axis=0)
                dk_acc.append(_dot_tn(q2, jnp.concatenate(ds_both, axis=0)).T)
                dv_acc.append(_dot_tn(do2, jnp.concatenate(p_both, axis=0)).T)
                for jj in range(4):
                    cols = slice((4 * g + jj) * 128, (4 * g + jj + 1) * 128)
                    dqj = dq[jj * BLK:(jj + 1) * BLK, :] * LOGIT_SCALE
                    dq_ref[rows, cols] = (dqj * c - _swap_halves(dqj) * s).astype(BF16)
            for col, acc in ((0, dk_acc), (128, dv_acc)):
                both = jnp.where(lo2, acc[0] + pltpu.roll(acc[0], 64, 1), acc[1] + pltpu.roll(acc[1], 64, 1))
                both = jnp.where(sink_row, 0.0, both)
                dkv_prev_ref[rows, col:col + 128] = both[0:BLK, :]
                dkv_own_ref[rows, col:col + 128] = both[BLK:2 * BLK, :]
        dsink_ref[...] += dsink
        dh_ref[...] = _dot(dq_ref[...], w_ref[ROW_Q:ROW_KV, :]) + _dot(dza_ref[...], w_ref[ROW_ZA:ROW_GA, :])

    tab = pl.BlockSpec((tq, 128), lambda i: (i % (SEQ_LEN // tq), 0))
    return pl.pallas_call(
        body, name="bwd_attn", grid=(t // tq,),
        in_specs=[pl.BlockSpec(memory_space=pltpu.SMEM), _row_spec(tq, D), _row_spec(tq, 256),
                  pl.BlockSpec((BLK, 256), lambda i: (jnp.maximum(i * blocks - 1, 0), 0)),
                  _row_spec(tq, D), pl.BlockSpec((blocks, 4, STACK, 128), lambda i: (i, 0, 0, 0)),
                  _row_spec(tq, D), _row_spec(tq, D, 0), tab, tab, _whole_vmem()],
        out_specs=[_row_spec(tq, D), _row_spec(tq, D), _row_spec(tq, 256), _row_spec(tq, 256),
                   pl.BlockSpec((8, 128), lambda i: (0, 0)), _row_spec(tq, D)],
        out_shape=[jax.ShapeDtypeStruct((t, D), BF16), jax.ShapeDtypeStruct((t, D), BF16),
                   jax.ShapeDtypeStruct((t, 256), F32), jax.ShapeDtypeStruct((t, 256), F32),
                   jax.ShapeDtypeStruct((8, 128), F32), jax.ShapeDtypeStruct((t, D), F32)],
        compiler_params=_params("arbitrary"),
    )(sinks, q, kv, kv, attn, lse, dub, g3, cos_t, sin_t, wt)


def _bwd_kv_finish(dkv_own, dkv_prev, cos_t, sin_t, h, prev):
    t = dkv_own.shape[0]
    tm = SEQ_LEN
    n_t = t // tm
    seq_tiles = SEQ_LEN // tm
    n_blocks = t // BLK

    def body(own_ref, same_ref, nxt_ref, c_ref, s_ref, h_ref, o32_in, o16_in, out_ref, o32_ref, o16_ref,
             acc_ref, acc16_ref, sems):
        step = pl.program_id(0)

        @pl.when(step == 0)
        def _():
            acc_ref[...] = jnp.zeros_like(acc_ref)

        keep = jnp.where(step % seq_tiles == seq_tiles - 1, 0.0, 1.0)
        shifted = jnp.concatenate([same_ref[BLK:tm, :], nxt_ref[...] * keep], axis=0)
        tot = own_ref[...] + shifted
        dk = tot[:, 0:128]
        out_ref[:, 0:128] = (dk * c_ref[...] - _swap_halves(dk) * s_ref[...]).astype(BF16)
        out_ref[:, 128:256] = tot[:, 128:256].astype(BF16)
        acc_ref[...] += _dot_tn(out_ref[...], h_ref[...])

        @pl.when(step == n_t - 1)
        def _():
            acc16_ref[...] = acc_ref[...].astype(BF16)
            rows = pl.ds(ROW_KV, 256)
            c32 = pltpu.make_async_copy(acc_ref, o32_ref.at[rows], sems.at[0])
            c16 = pltpu.make_async_copy(acc16_ref, o16_ref.at[rows], sems.at[1])
            c32.start()
            c16.start()
            c32.wait()
            c16.wait()

    tab = pl.BlockSpec((tm, 128), lambda i: (i % seq_tiles, 0))
    hbm = pl.BlockSpec(memory_space=pl.ANY)
    out = pl.pallas_call(
        body, name="bwd_kv_finish", grid=(n_t,),
        in_specs=[_row_spec(tm, 256), _row_spec(tm, 256),
                  pl.BlockSpec((BLK, 256), lambda i: (jnp.minimum((i + 1) * (tm // BLK), n_blocks - 1), 0)), tab, tab,
                  _row_spec(tm, D), hbm, hbm],
        out_specs=[_row_spec(tm, 256), hbm, hbm],
        out_shape=[jax.ShapeDtypeStruct((t, 256), BF16), jax.ShapeDtypeStruct((D_IN, D), F32),
                   jax.ShapeDtypeStruct((D_IN, D), BF16)],
        scratch_shapes=[pltpu.VMEM((256, D), F32), pltpu.VMEM((256, D), BF16), pltpu.SemaphoreType.DMA((2,))],
        input_output_aliases={6: 1, 7: 2},
        compiler_params=_params("arbitrary"),
    )(dkv_own, dkv_prev, dkv_prev, cos_t, sin_t, h, *prev)
    return out[0], (out[1], out[2])


STAGE_ROWS = 256


def _bwd_conv(dya, a4, h, wconv8, wpc, tm, parts):
    t = a4.shape[0]
    n_t = t // tm
    sub = tm // parts
    seq_tiles = SEQ_LEN // tm

    def body(dya_ref, xc_ref, bg_ref, cg_ref, zc_ref, xcp_ref, cgp_ref, w_ref, wpc_ref, h_ref,
             da4_ref, dwpc_ref, dwc_ref, o32_ref, o16_ref, acc_ref, stage_ref, later_ref, sems):
        step = pl.program_id(0)
        tile = n_t - 1 - step

        @pl.when(step == 0)
        def _():
            dwpc_ref[...] = jnp.zeros_like(dwpc_ref)
            dwc_ref[...] = jnp.zeros_like(dwc_ref)
            acc_ref[...] = jnp.zeros_like(acc_ref)

        keep_prev = jnp.where(tile % seq_tiles == 0, 0.0, 1.0)
        ends_sequence = tile % seq_tiles == seq_tiles - 1

        def part(p, later):
            r0 = p * sub
            here = slice(r0, r0 + sub)
            if p == 0:
                u_prev = cgp_ref[14:16, :].astype(F32) * xcp_ref[14:16, :].astype(F32) * keep_prev
            else:
                u_prev = cg_ref[r0 - 2:r0, :].astype(F32) * xc_ref[r0 - 2:r0, :].astype(F32)
            xc = xc_ref[here, :].astype(F32)
            bg = bg_ref[here, :].astype(F32)
            cg = cg_ref[here, :].astype(F32)
            zc = zc_ref[here, :].astype(F32)
            u, u_m1, u_m2, yconv, sg, sz, co = _conv_forward(xc, bg, cg, zc, u_prev[0:1, :], u_prev[1:2, :], w_ref)
            ua = (sz * co).astype(BF16)
            dua = _dot_nt(dya_ref[here, :], wpc_ref[...])
            da4_ref[here, 3 * D:4 * D] = (dua * co * (sg * (1.0 + zc * (1.0 - sg)))).astype(BF16)
            dco = dua * sz
            da4_ref[here, D:2 * D] = (dco * yconv).astype(BF16)
            dyc = dco * bg
            dwc = jnp.concatenate([jnp.sum(dyc * s, axis=0, keepdims=True) for s in (u_m2, u_m1, u)], axis=0)
            rows = lax.broadcasted_iota(jnp.int32, xc.shape, 0)
            n0 = later[0:1, :]
            n1 = later[1:2, :]
            dyc_p1 = jnp.where(rows == sub - 1, n0, pltpu.roll(dyc, sub - 1, 0))
            dyc_p2 = jnp.where(rows == sub - 2, n0, jnp.where(rows == sub - 1, n1, pltpu.roll(dyc, sub - 2, 0)))
            du = w_ref[2:3, :] * dyc + w_ref[1:2, :] * dyc_p1 + w_ref[0:1, :] * dyc_p2
            da4_ref[here, 0:D] = (du * cg).astype(BF16)
            da4_ref[here, 2 * D:3 * D] = (du * xc).astype(BF16)
            return ua, dwc, dyc[0:8, :]

        later = jnp.where(ends_sequence, 0.0, later_ref[...])
        uas, dwc = [], jnp.zeros((3, D), F32)
        for p in reversed(range(parts)):
            ua, dwc_p, later = part(p, later)
            uas.insert(0, ua)
            dwc = dwc + dwc_p
        later_ref[...] = later
        dwpc_ref[...] += _dot_tn(jnp.concatenate(uas, axis=0), dya_ref[...])
        dwc_ref[0:3, :] += dwc
        for j in range(4):
            acc_ref[j * D:(j + 1) * D, :] += _dot_tn(da4_ref[:, j * D:(j + 1) * D], h_ref[...])

        @pl.when(step == n_t - 1)
        def _():
            c32 = pltpu.make_async_copy(acc_ref, o32_ref.at[pl.ds(0, 4 * D)], sems.at[0])
            c32.start()
            for j in range(4 * D // STAGE_ROWS):
                rows = pl.ds(j * STAGE_ROWS, STAGE_ROWS)
                stage_ref[...] = acc_ref[rows, :].astype(BF16)
                c16 = pltpu.make_async_copy(stage_ref, o16_ref.at[rows], sems.at[1])
                c16.start()
                c16.wait()
            c32.wait()

    def rows_of_tile(width, col=0):
        return pl.BlockSpec((tm, width), lambda s: (n_t - 1 - s, col))

    def prev(col):
        return pl.BlockSpec((16, D), lambda s: (jnp.maximum((n_t - 1 - s) * (tm // 16) - 1, 0), col))

    hbm = pl.BlockSpec(memory_space=pl.ANY)
    out = pl.pallas_call(
        body, name="bwd_conv", grid=(n_t,),
        in_specs=[rows_of_tile(D), rows_of_tile(D, 0), rows_of_tile(D, 1), rows_of_tile(D, 2), rows_of_tile(D, 3),
                  prev(0), prev(2), pl.BlockSpec((8, D), lambda s: (0, 0)), _whole_vmem(), rows_of_tile(D)],
        out_specs=[rows_of_tile(4 * D), _whole_vmem(), pl.BlockSpec((8, D), lambda s: (0, 0)), hbm, hbm],
        out_shape=[jax.ShapeDtypeStruct((t, 4 * D), BF16), jax.ShapeDtypeStruct((D, D), F32),
                   jax.ShapeDtypeStruct((8, D), F32), jax.ShapeDtypeStruct((D_IN, D), F32),
                   jax.ShapeDtypeStruct((D_IN, D), BF16)],
        scratch_shapes=[pltpu.VMEM((4 * D, D), F32), pltpu.VMEM((STAGE_ROWS, D), BF16), pltpu.VMEM((8, D), F32),
                        pltpu.SemaphoreType.DMA((2,))],
        compiler_params=pltpu.CompilerParams(dimension_semantics=("arbitrary",), vmem_limit_bytes=V7X_VMEM_BYTES - (2 << 20)),
    )(dya, a4, a4, a4, a4, a4, a4, wconv8, wpc, h)
    return out[0], out[1], out[2], (out[3], out[4])


def _bwd_dh(da4, dh_part, dkv, dgab, wt, x, g_pre, dout, tm):
    t = x.shape[0]

    def body(da4_ref, dhp_ref, dkv_ref, dgab_ref, w_ref, x_ref, g_ref, dout_ref, gx_ref, dg_ref):
        @pl.when(pl.program_id(0) == 0)
        def _():
            dg_ref[...] = jnp.zeros_like(dg_ref)

        dh = dhp_ref[...] + _dot(da4_ref[...], w_ref[0:ROW_Q, :])
        dh += _dot(dkv_ref[...], w_ref[ROW_KV:ROW_ZA, :])
        dh += _dot(dgab_ref[...], w_ref[ROW_GA:D_IN, :])
        xf = x_ref[...]
        r = lax.rsqrt(jnp.mean(xf * xf, axis=-1, keepdims=True) + RMS_EPS)
        xn = xf * r
        dg_ref[0:1, :] += jnp.sum(dh * xn, axis=0, keepdims=True)
        dxn = dh * g_ref[...]
        gx_ref[...] = dout_ref[...] + r * (dxn - xn * jnp.mean(dxn * xn, axis=-1, keepdims=True))

    return pl.pallas_call(
        body, name="bwd_dh", grid=(t // tm,),
        in_specs=[_row_spec(tm, 4 * D), _row_spec(tm, D), _row_spec(tm, 256), _row_spec(tm, 2 * D),
                  _whole_vmem(), _row_spec(tm, D), pl.BlockSpec((1, D), lambda i: (0, 0)), _row_spec(tm, D)],
        out_specs=[_row_spec(tm, D), pl.BlockSpec((8, D), lambda i: (0, 0))],
        out_shape=[jax.ShapeDtypeStruct((t, D), F32), jax.ShapeDtypeStruct((8, D), F32)],
        compiler_params=_params("arbitrary"),
    )(da4, dh_part, dkv, dgab, wt, x, g_pre, dout)


def _bwd_dw_in(h, pieces, nb, tm, name, prev):
    n_t = h.shape[0] // tm
    n_a = len(pieces)
    jobs = [(a, row0 + b * nb) for a, (arr, row0) in enumerate(pieces) for b in range(arr.shape[1] // nb)]
    first = [min(k for k, (a, _) in enumerate(jobs) if a == b) for b in range(n_a)]
    n_j = len(jobs)

    def body(*refs):
        h_ref, p_refs = refs[0], refs[1:1 + n_a]
        o32_ref, o16_ref, acc_ref, acc16_ref, sems = refs[-5:]
        j, i = pl.program_id(0), pl.program_id(1)

        def copies(k):
            rows = pl.ds(jobs[k][1], nb)
            return (pltpu.make_async_copy(acc_ref.at[k], o32_ref.at[rows], sems.at[0, k]),
                    pltpu.make_async_copy(acc16_ref.at[k], o16_ref.at[rows], sems.at[1, k]))

        for k, (a, _) in enumerate(jobs):
            @pl.when(j == k)
            def _(k=k, a=a):
                @pl.when(i == 0)
                def _():
                    acc_ref[k] = jnp.zeros((nb, D), F32)

                acc_ref[k] += _dot_tn(p_refs[a][...], h_ref[...])

                @pl.when(i == n_t - 1)
                def _():
                    acc16_ref[k] = acc_ref[k].astype(BF16)
                    for cp in copies(k):
                        cp.start()

        @pl.when((j == n_j - 1) & (i == n_t - 1))
        def _():
            for k in range(n_j):
                for cp in copies(k):
                    cp.wait()

    def piece_spec(a):
        s, e = first[a], first[a] + pieces[a][0].shape[1] // nb
        return pl.BlockSpec((tm, nb), lambda j, i: (jnp.where(j < s, 0, jnp.where(j >= e, n_t - 1, i)),
                                                    jnp.clip(j - s, 0, e - s - 1)))

    hbm = pl.BlockSpec(memory_space=pl.ANY)
    return pl.pallas_call(
        body, name=name, grid=(n_j, n_t),
        in_specs=[pl.BlockSpec((tm, D), lambda j, i: (i, 0))] + [piece_spec(a) for a in range(n_a)] + [hbm, hbm],
        out_specs=[hbm, hbm],
        out_shape=[jax.ShapeDtypeStruct((D_IN, D), F32), jax.ShapeDtypeStruct((D_IN, D), BF16)],
        scratch_shapes=[pltpu.VMEM((n_j, nb, D), F32), pltpu.VMEM((n_j, nb, D), BF16),
                        pltpu.SemaphoreType.DMA((2, n_j))],
        input_output_aliases={1 + n_a: 0, 2 + n_a: 1},
        compiler_params=_params("arbitrary", "arbitrary", vmem_limit_bytes=48 << 20),
    )(h, *[arr for arr, _ in pieces], *prev)


def _place():
    x, y, c = lax.axis_index("x"), lax.axis_index("y"), lax.axis_index("c")
    return x, y, c, 4 * x + 2 * y + c


def _peer(x, y, c, k):
    return (1 - x if k & 4 else x, 1 - y if k & 2 else y, 1 - c if k & 1 else c)


ICI_MASKS = (4, 2, 6)


def _all_gather(shards):
    n = len(shards)

    def body(*refs):
        src, dst = refs[:n], refs[n:2 * n]
        send_sems, recv_sems, local_sems = refs[2 * n:]
        x, y, c, me = _place()
        sibling = _peer(x, y, c, 1)

        def copy(a, s, block, to, own=False):
            return pltpu.make_async_remote_copy(
                src_ref=src[a] if own else dst[a].at[block], dst_ref=dst[a].at[block],
                send_sem=send_sems.at[a * 7 + s], recv_sem=recv_sems.at[a * 7 + s], device_id=to, device_id_type=MESH_ID)

        local = [pltpu.make_async_copy(src[a], dst[a].at[me], local_sems.at[a]) for a in range(n)]
        for cp in local:
            cp.start()
        started = [copy(a, 0, me, sibling, own=True) for a in range(n)]
        started += [copy(a, 1 + j, me, _peer(x, y, c, k), own=True) for j, k in enumerate(ICI_MASKS) for a in range(n)]
        for cp in started:
            cp.start()
        for j, k in enumerate(ICI_MASKS):
            for a in range(n):
                copy(a, 1 + j, me ^ k, sibling).wait_recv()
                fwd = copy(a, 4 + j, me ^ k, sibling)
                fwd.start()
                started.append(fwd)
        for a in range(n):
            copy(a, 0, me ^ 1, sibling).wait_recv()
        for j, k in enumerate(ICI_MASKS):
            for a in range(n):
                copy(a, 4 + j, me ^ 1 ^ k, sibling).wait_recv()
        for cp in started:
            cp.wait_send()
        for cp in local:
            cp.wait()

    hbm = pl.BlockSpec(memory_space=pl.ANY)
    return pl.pallas_call(
        body, name="all_gather_weights",
        in_specs=[hbm] * n, out_specs=[hbm] * n,
        out_shape=[jax.ShapeDtypeStruct((N_DEV,) + s.shape, s.dtype) for s in shards],
        scratch_shapes=[pltpu.SemaphoreType.DMA((7 * n,)), pltpu.SemaphoreType.DMA((7 * n,)),
                        pltpu.SemaphoreType.DMA((n,))],
    )(*shards)


def _direct_copies(src, land, send_sems, recv_sems):
    x, y, c, me = _place()
    return [pltpu.make_async_remote_copy(
        src_ref=src[a], dst_ref=land[a].at[me], send_sem=send_sems.at[a * 7 + k - 1],
        recv_sem=recv_sems.at[a * 7 + k - 1], device_id=_peer(x, y, c, k), device_id_type=MESH_ID)
        for k in range(1, N_DEV) for a in range(len(src))]


def _gather_start(shards, name):
    n = len(shards)

    def body(*refs):
        src, land = refs[:n], refs[n:2 * n]
        send_sems, recv_sems = refs[2 * n], refs[2 * n + 1]
        token_ref = refs[-1]
        for cp in _direct_copies(src, land, send_sems, recv_sems):
            cp.start()
        token_ref[...] = jnp.zeros_like(token_ref)

    hbm = pl.BlockSpec(memory_space=pltpu.HBM)
    sem = pl.BlockSpec(memory_space=pltpu.SEMAPHORE)
    lands = [lax.empty((N_DEV,) + s.shape, s.dtype) for s in shards]
    out = pl.pallas_call(
        body, name=name + "_start",
        out_shape=(pltpu.SemaphoreType.DMA((7 * n,)), pltpu.SemaphoreType.DMA((7 * n,)),
                   *[pltpu.HBM(s.shape, s.dtype) for s in shards], *[pltpu.HBM(s.shape, s.dtype) for s in lands],
                   jax.ShapeDtypeStruct((8, 128), F32)),
        in_specs=[hbm] * (2 * n), out_specs=(sem, sem, *[hbm] * (2 * n), _whole_vmem()),
        input_output_aliases={a: 2 + a for a in range(2 * n)},
        compiler_params=pltpu.CompilerParams(has_side_effects=pltpu.SideEffectType.DATAFLOW_SIDE_EFFECTING),
    )(*[pltpu.with_memory_space_constraint(s, pltpu.HBM) for s in list(shards) + lands])
    return out[0], out[1], out[2:2 + n], out[2 + n:2 + 2 * n], out[-1]


def _gather_wait(send_sems, recv_sems, flying, lands, after, name):
    n = len(flying)

    def body(*refs):
        src, land = refs[:n], refs[n:2 * n]
        for cp in _direct_copies(src, land, refs[2 * n], refs[2 * n + 1]):
            cp.wait_send()
            cp.wait_recv()

    hbm = pl.BlockSpec(memory_space=pltpu.HBM)
    sem = pl.BlockSpec(memory_space=pltpu.SEMAPHORE)
    out = pl.pallas_call(
        body, name=name + "_wait",
        out_shape=tuple(pltpu.HBM(s.shape, s.dtype) for s in list(flying) + list(lands)),
        in_specs=[hbm] * (2 * n) + [sem, sem, pl.BlockSpec(memory_space=pl.ANY)], out_specs=tuple([hbm] * (2 * n)),
        input_output_aliases={a: a for a in range(2 * n)},
        compiler_params=pltpu.CompilerParams(has_side_effects=pltpu.SideEffectType.DATAFLOW_SIDE_EFFECTING),
    )(*flying, *lands, send_sems, recv_sems, after)
    return out[n:]


def _sibling_copies(src, land, send_sems, recv_sems, blocks):
    x, y, c, _ = _place()
    sibling = _peer(x, y, c, 1)
    return [pltpu.make_async_remote_copy(
        src_ref=src[a].at[2 * p + (1 - c)], dst_ref=land[a].at[p], send_sem=send_sems.at[a * 4 + p],
        recv_sem=recv_sems.at[a * 4 + p], device_id=sibling, device_id_type=MESH_ID)
        for a in range(len(src)) for p in blocks[a]]


def _exchange_sibling_start(by_dest, blocks):
    n = len(by_dest)

    def body(*refs):
        for cp in _sibling_copies(refs[:n], refs[n:2 * n], refs[2 * n], refs[2 * n + 1], blocks):
            cp.start()

    hbm = pl.BlockSpec(memory_space=pltpu.HBM)
    sem = pl.BlockSpec(memory_space=pltpu.SEMAPHORE)
    lands = [lax.empty((4,) + s.shape[1:], s.dtype) for s in by_dest]
    out = pl.pallas_call(
        body, name="exchange_sibling_start",
        out_shape=(pltpu.SemaphoreType.DMA((4 * n,)), pltpu.SemaphoreType.DMA((4 * n,)),
                   *[pltpu.HBM(s.shape, s.dtype) for s in by_dest], *[pltpu.HBM(s.shape, s.dtype) for s in lands]),
        in_specs=[hbm] * (2 * n), out_specs=(sem, sem, *[hbm] * (2 * n)),
        input_output_aliases={a: 2 + a for a in range(2 * n)},
        compiler_params=pltpu.CompilerParams(has_side_effects=pltpu.SideEffectType.DATAFLOW_SIDE_EFFECTING),
    )(*[pltpu.with_memory_space_constraint(s, pltpu.HBM) for s in list(by_dest) + lands])
    return out[0], out[1], out[2:2 + n], out[2 + n:]


def _exchange_sibling_last(by_dest, lands, blocks):
    n = len(by_dest)

    def body(*refs):
        copies = _sibling_copies(refs[:n], refs[n:2 * n], refs[-2], refs[-1], blocks)
        for cp in copies:
            cp.start()
        for cp in copies:
            cp.wait_recv()
        for cp in copies:
            cp.wait_send()

    hbm = pl.BlockSpec(memory_space=pl.ANY)
    return pl.pallas_call(
        body, name="exchange_sibling_last", in_specs=[hbm] * (2 * n), out_specs=[hbm] * n,
        out_shape=[jax.ShapeDtypeStruct(s.shape, s.dtype) for s in lands],
        scratch_shapes=[pltpu.SemaphoreType.DMA((4 * n,)), pltpu.SemaphoreType.DMA((4 * n,))],
        input_output_aliases={n + a: a for a in range(n)},
    )(*by_dest, *lands)


def _exchange_sibling_wait(send_sems, recv_sems, flying, lands, blocks):
    n = len(flying)

    def body(*refs):
        for cp in _sibling_copies(refs[:n], refs[n:2 * n], refs[2 * n], refs[2 * n + 1], blocks):
            cp.wait_recv()
            cp.wait_send()

    hbm = pl.BlockSpec(memory_space=pltpu.HBM)
    sem = pl.BlockSpec(memory_space=pltpu.SEMAPHORE)
    out = pl.pallas_call(
        body, name="exchange_sibling_wait",
        out_shape=tuple(pltpu.HBM(s.shape, s.dtype) for s in list(flying) + list(lands)),
        in_specs=[hbm] * (2 * n) + [sem, sem], out_specs=tuple([hbm] * (2 * n)),
        input_output_aliases={a: a for a in range(2 * n)},
        compiler_params=pltpu.CompilerParams(has_side_effects=pltpu.SideEffectType.DATAFLOW_SIDE_EFFECTING),
    )(*[pltpu.with_memory_space_constraint(s, pltpu.HBM) for s in list(flying) + list(lands)], send_sems, recv_sems)
    return out[n:]


def _chip_copies(src, land, send_sems, recv_sems):
    x, y, c, _ = _place()
    chip = 2 * x + y
    return [pltpu.make_async_remote_copy(
        src_ref=src[a].at[chip ^ (k >> 1)], dst_ref=land[a].at[j], send_sem=send_sems.at[a * 3 + j],
        recv_sem=recv_sems.at[a * 3 + j], device_id=_peer(x, y, c, k), device_id_type=MESH_ID)
        for j, k in enumerate(ICI_MASKS) for a in range(len(src))]


def _exchange_chips_start(by_chip):
    n = len(by_chip)

    def body(*refs):
        src, land = refs[:n], refs[n:2 * n]
        send_sems, recv_sems = refs[2 * n], refs[2 * n + 1]
        token_ref = refs[-1]
        for cp in _chip_copies(src, land, send_sems, recv_sems):
            cp.start()
        token_ref[...] = jnp.zeros_like(token_ref)

    hbm = pl.BlockSpec(memory_space=pltpu.HBM)
    sem = pl.BlockSpec(memory_space=pltpu.SEMAPHORE)
    lands = [lax.empty((3,) + s.shape[1:], s.dtype) for s in by_chip]
    out = pl.pallas_call(
        body, name="exchange_chips_start",
        out_shape=(pltpu.SemaphoreType.DMA((3 * n,)), pltpu.SemaphoreType.DMA((3 * n,)),
                   *[pltpu.HBM(s.shape, s.dtype) for s in by_chip], *[pltpu.HBM(s.shape, s.dtype) for s in lands],
                   jax.ShapeDtypeStruct((8, 128), F32)),
        in_specs=[hbm] * (2 * n), out_specs=(sem, sem, *[hbm] * (2 * n), _whole_vmem()),
        input_output_aliases={a: 2 + a for a in range(2 * n)},
        compiler_params=pltpu.CompilerParams(has_side_effects=pltpu.SideEffectType.DATAFLOW_SIDE_EFFECTING),
    )(*[pltpu.with_memory_space_constraint(s, pltpu.HBM) for s in list(by_chip) + lands])
    return out[0], out[1], out[2:2 + n], out[2 + n:2 + 2 * n], out[-1]


def _exchange_chips_wait(send_sems, recv_sems, flying, lands, after):
    n = len(flying)

    def body(*refs):
        src, land = refs[:n], refs[n:2 * n]
        send_sems_ref, recv_sems_ref = refs[2 * n], refs[2 * n + 1]
        for cp in _chip_copies(src, land, send_sems_ref, recv_sems_ref):
            cp.wait_send()
            cp.wait_recv()

    hbm = pl.BlockSpec(memory_space=pltpu.HBM)
    sem = pl.BlockSpec(memory_space=pltpu.SEMAPHORE)
    out = pl.pallas_call(
        body, name="exchange_chips_wait",
        out_shape=tuple(pltpu.HBM(s.shape, s.dtype) for s in list(flying) + list(lands)),
        in_specs=[hbm] * (2 * n) + [sem, sem, pl.BlockSpec(memory_space=pl.ANY)], out_specs=tuple([hbm] * (2 * n)),
        input_output_aliases={a: a for a in range(2 * n)},
        compiler_params=pltpu.CompilerParams(has_side_effects=pltpu.SideEffectType.DATAFLOW_SIDE_EFFECTING),
    )(*flying, *lands, send_sems, recv_sems, after)
    return out[n:]


def _adamw_math(w, g, m, v):
    m = ADAM_B1 * m + (1.0 - ADAM_B1) * g
    v = ADAM_B2 * v + (1.0 - ADAM_B2) * (g * g)
    m_hat = m / (1.0 - ADAM_B1 ** ADAM_STEP)
    v_hat = v / (1.0 - ADAM_B2 ** ADAM_STEP)
    return -ADAM_LR * (m_hat / (jnp.sqrt(v_hat) + ADAM_EPS) + ADAM_WD * w), m, v


def _pair_sum(owns, recvs, place_arr, tr, name):
    n = len(owns)
    _, rows, cols = owns[0].shape

    def body(place_ref, *refs):
        for a in range(n):
            s = refs[a][...] + refs[n + a][...].astype(F32)
            refs[3 * n + a][...] = s.astype(BF16)

            @pl.when(pl.program_id(1) == place_ref[1])
            def _(a=a, s=s):
                refs[2 * n + a][...] = s

    by_chip = pl.BlockSpec((None, tr, cols), lambda i, p, place_ref: (p, i, 0))
    mine = pl.BlockSpec((None, tr, cols), lambda i, p, place_ref: (2 * p + place_ref[0], i, 0))
    kept = pl.BlockSpec((tr, cols), lambda i, p, place_ref: (i, 0))
    out = pl.pallas_call(
        body, name=name,
        grid_spec=pltpu.PrefetchScalarGridSpec(
            num_scalar_prefetch=1, grid=(rows // tr, 4), in_specs=[mine] * n + [by_chip] * n,
            out_specs=[kept] * n + [by_chip] * n),
        out_shape=[jax.ShapeDtypeStruct((rows, cols), F32)] * n + [jax.ShapeDtypeStruct((4, rows, cols), BF16)] * n,
        compiler_params=_params("parallel", "arbitrary"),
    )(place_arr, *owns, *recvs)
    return out[:n], out[n:]


def _chip_sum(pairs, recvs, tr, name, adam=None):
    n = len(pairs)
    rows, cols = pairs[0].shape
    n_state = 0 if adam is None else 3 * n

    def body(*refs):
        outs = refs[2 * n + n_state:]
        for a in range(n):
            g = refs[a][...]
            for j in range(3):
                g = g + refs[n + a][j].astype(F32)
            outs[a][...] = g
            if adam is not None:
                w_ref, m_ref, v_ref = (refs[2 * n + s * n + a] for s in range(3))
                outs[n + a][...], outs[2 * n + a][...], outs[3 * n + a][...] = _adamw_math(w_ref[...], g, m_ref[...], v_ref[...])

    blk = pl.BlockSpec((tr, cols), lambda i: (i, 0))
    n_out = n if adam is None else 4 * n
    out = pl.pallas_call(
        body, name=name, grid=(rows // tr,),
        in_specs=[blk] * n + [pl.BlockSpec((3, tr, cols), lambda i: (0, i, 0))] * n + [blk] * n_state,
        out_specs=[blk] * n_out,
        out_shape=[jax.ShapeDtypeStruct((rows, cols), F32)] * n_out,
        compiler_params=_params("parallel"),
    )(*pairs, *recvs, *([] if adam is None else [t for group in adam for t in group]))
    return out if adam is None else (out[:n], out[n:2 * n], out[2 * n:3 * n], out[3 * n:])


def _adamw(ws, gs, ms, vs, name):
    n = len(ws)

    def body(*refs):
        for a in range(n):
            w_ref, g_ref, m_ref, v_ref = (refs[s * n + a] for s in range(4))
            refs[4 * n + a][...], refs[5 * n + a][...], refs[6 * n + a][...] = _adamw_math(
                w_ref[...], g_ref[...], m_ref[...], v_ref[...])

    out = pl.pallas_call(body, name=name, out_shape=[jax.ShapeDtypeStruct(w.shape, F32) for w in ws] * 3)(
        *ws, *gs, *ms, *vs)
    return out[:n], out[n:2 * n], out[2 * n:]


def _sum_small(small_all):
    def body(s_ref, o_ref):
        g = s_ref[0]
        for d in range(1, N_DEV):
            g = g + s_ref[d]
        o_ref[...] = g

    return pl.pallas_call(body, name="sum_small", out_shape=jax.ShapeDtypeStruct(small_all.shape[1:], F32))(small_all)


def _rope_tables():
    inv_freq = ROPE_THETA ** (-jnp.arange(0, HEAD_DIM, 2, dtype=F32) / HEAD_DIM)
    ang = jnp.arange(SEQ_LEN).astype(F32)[:, None] * inv_freq[None, :]
    cos, sin = jnp.cos(ang), jnp.sin(ang)
    return jnp.tile(cos, (1, 4)), jnp.tile(jnp.concatenate([-sin, sin], axis=1), (1, 2))


def _local_step(x, target, g_pre, g_post, sinks, wt, wconv, squares, start_exchange=None):
    cos_t, sin_t = _rope_tables()
    wconv8 = jnp.pad(wconv, ((0, 5), (0, 0)))
    h, q, kv, g3 = _fwd_in_attn(x, g_pre, wt, cos_t, sin_t, 512)
    wpc, wpa, wout = squares(kv)
    a4, ya = _fwd_in_conv(h, wt, wconv8, wpc, 512)
    attn, ub, lse = _fwd_attn(sinks, q, kv, g3, 4)
    loss8, dout, dya, dub, dgab, dwout, dwpa, dgpost8 = _fwd_out_bwd_head(ya, ub, g3, x, target, g_post, wpa, wout, 512)
    dq, dza, dkv_own, dkv_prev, dsink8, dh_part = _bwd_attn(sinks, q, kv, attn, lse, dub, g3, cos_t, sin_t, wt, 4)
    da4, dwpc, dwconv8, dwt = _bwd_conv(dya, a4, h, wconv8, wpc, 512, 2)
    dkv, dwt = _bwd_kv_finish(dkv_own, dkv_prev, cos_t, sin_t, h, dwt)
    dwt = _bwd_dw_in(h, [(dq, ROW_Q), (dza, ROW_ZA)], 1024, 1024, "bwd_dw_in_q_za", dwt)
    sent = None
    if start_exchange is not None:
        dwt, sent = start_exchange[0](dwt, dwpc, dwpa, dwout)
    dwt32, dwt16 = _bwd_dw_in(h, [(dgab, ROW_GA)], 1024, 1024, "bwd_dw_in_gates", dwt)
    token, pending = (None, None) if start_exchange is None else start_exchange[1](dwt32, dwt16, dwpc, dwpa, dwout, sent)
    g_pre_after = g_pre if token is None else g_pre + token[0:1, 0:1]
    grad_x, dgpre8 = _bwd_dh(da4, dh_part, dkv, dgab, wt, x, g_pre_after, dout, 512)
    small = jnp.concatenate([dgpre8, dgpost8, jnp.pad(dsink8, ((0, 0), (0, D - 128))), dwconv8,
                             jnp.pad(loss8, ((0, 0), (0, D - 128)))], axis=0)
    return loss8[0, 0], grad_x, dwt32, dwt16, dwpc, dwpa, dwout, small, pending


def kernel(x, g_pre, g_post, w_in, w_conv, sinks, w_proj_conv, w_proj_attn, w_out, loss_target, m_g_pre, m_g_post, m_w_in, m_w_conv, m_sinks, m_w_proj_conv, m_w_proj_attn, m_w_out, v_g_pre, v_g_post, v_w_in, v_w_conv, v_sinks, v_w_proj_conv, v_w_proj_attn, v_w_out):
    batch = x.shape[0]
    mx, my, mc, me = _place()
    place_arr = jnp.stack([mc, 2 * mx + my]).astype(jnp.int32)

    g_wt, g_conv = _all_gather([w_in[0].T.astype(BF16), jnp.pad(w_conv[0], ((0, 5), (0, 0)))])
    wt = g_wt.reshape(D_IN, D)
    wconv = g_conv[:, 0:3, :].transpose(1, 0, 2).reshape(3, D)
    sq_mine = [w.astype(BF16) for w in (w_proj_conv[0], w_proj_attn[0], w_out[0])]
    wt, sq_mine = lax.optimization_barrier((wt, sq_mine))
    sq_send, sq_recv, sq_flying, sq_lands, sq_token = _gather_start(sq_mine, "gather_squares")

    def squares(after):
        got = _gather_wait(sq_send, sq_recv, sq_flying, sq_lands, after, "gather_squares")
        return [lax.dynamic_update_index_in_dim(full, mine, me, 0).reshape(D, D) for full, mine in zip(got, sq_mine)]

    early = [(0, 1, 2)] + [(0, 1, 2, 3)] * 3
    late = [(3,)]

    def send_early(dwt, dwpc, dwpa, dwout):
        sq16 = [g.astype(BF16).reshape(N_DEV, SHARD_SQ, D) for g in (dwpc, dwpa, dwout)]
        send_sems, recv_sems, flying, lands = _exchange_sibling_start([dwt[1].reshape(N_DEV, SHARD_IN, D)] + sq16, early)
        return (dwt[0], flying[0].reshape(D_IN, D)), (send_sems, recv_sems, flying[1:], lands)

    def send_late(dwt32, dwt16, dwpc, dwpa, dwout, sent):
        own_sq = [g.reshape(N_DEV, SHARD_SQ, D) for g in (dwpc, dwpa, dwout)]
        own_in = dwt32.reshape(N_DEV, SHARD_IN, D)
        send_sems, recv_sems, sq_flying, lands = sent
        dwt16 = dwt16.reshape(N_DEV, SHARD_IN, D)
        lands = list(_exchange_sibling_last([dwt16], lands[:1], late)) + list(lands[1:])
        from_sibling = _exchange_sibling_wait(send_sems, recv_sems, [dwt16] + list(sq_flying), lands, early)
        in32, in16 = _pair_sum([own_in], from_sibling[:1], place_arr, SHARD_IN // 2, "pair_sum_w_in")
        sq32, sq16 = _pair_sum(own_sq, from_sibling[1:], place_arr, SHARD_SQ, "pair_sum_squares")
        send_sems, recv_sems, flying, lands, token = _exchange_chips_start(list(in16) + list(sq16))
        return token, (send_sems, recv_sems, flying, lands, in32, sq32)

    _, grad_x, _, _, _, _, _, small, pending = _local_step(
        x.reshape(batch * SEQ_LEN, D), loss_target.reshape(batch * SEQ_LEN, D), g_pre + sq_token[0:1, 0:1], g_post,
        sinks, wt, wconv, squares, (send_early, send_late))
    sm_send, sm_recv, sm_flying, sm_lands, sm_token = _gather_start([small], "gather_small")
    send_sems, recv_sems, flying, lands, in32, sq32 = pending
    from_chips = _exchange_chips_wait(send_sems, recv_sems, flying, lands, sm_token)

    o_in = [o[0].T for o in _chip_sum(
        in32, from_chips[:1], SHARD_IN // 3, "chip_sum_adamw_w_in",
        adam=([w_in[0].T], [m_w_in[0].T], [v_w_in[0].T]))]
    g_in_mine, o_in = o_in[0], o_in[1:]
    g_sq, d_sq, m_sq, v_sq = _chip_sum(
        sq32, from_chips[1:], SHARD_SQ, "chip_sum_adamw_squares",
        adam=([w_proj_conv[0], w_proj_attn[0], w_out[0]], [m_w_proj_conv[0], m_w_proj_attn[0], m_w_out[0]],
              [v_w_proj_conv[0], v_w_proj_attn[0], v_w_out[0]]))
    both_done, g_in_mine = lax.optimization_barrier((d_sq[0], g_in_mine))
    (small_all,) = _gather_wait(sm_send, sm_recv, sm_flying, sm_lands, both_done, "gather_small")
    gs = _sum_small(lax.dynamic_update_index_in_dim(small_all, small, me, 0))
    g_g_pre, g_g_post, g_sinks, loss = gs[0:1], gs[8:9], gs[16:17, 0:N_HEADS], gs[32, 0]
    g_conv_mine = lax.dynamic_slice_in_dim(gs[24:27], me * SHARD_SQ, SHARD_SQ, axis=1)
    o_small = _adamw([g_pre, g_post, sinks, w_conv[0]], [g_g_pre, g_g_post, g_sinks, g_conv_mine],
                     [m_g_pre, m_g_post, m_sinks, m_w_conv[0]], [v_g_pre, v_g_post, v_sinks, v_w_conv[0]], "adamw_small")

    grads = [g_g_pre, g_g_post, g_in_mine[None], g_conv_mine[None], g_sinks] + [g[None] for g in g_sq]
    rest = []
    for idx, sq in enumerate((d_sq, m_sq, v_sq)):
        gp, gq, sk, cv = o_small[idx]
        rest += [gp, gq, o_in[idx][None], cv[None], sk] + [s[None] for s in sq]
    return (loss, grad_x.reshape(batch, SEQ_LEN, D), *grads, *rest)
```

```python
import jax
import jax.numpy as jnp
from jax import lax
from jax.experimental import pallas as pl
from jax.experimental.pallas import tpu as pltpu

D = 1024
N_HEADS = 16
HEAD_DIM = 64
LOGIT_SCALE = HEAD_DIM ** -0.5
BLK = 128
SEQ_LEN = 2048
D_IN = 8448
ROW_Q, ROW_KV, ROW_ZA, ROW_GA = 4 * D, 5 * D, 5 * D + 256, 6 * D + 256
SHARD_IN = D_IN // 8
SHARD_SQ = D // 8
N_DEV = 8
V7X_VMEM_BYTES = 64 << 20
ROPE_THETA = 10000.0
RMS_EPS = 1e-6
NEG = -1e30
ADAM_LR, ADAM_B1, ADAM_B2, ADAM_EPS, ADAM_WD, ADAM_STEP = 0.001, 0.9, 0.999, 1e-08, 0.01, 10

F32 = jnp.float32
BF16 = jnp.bfloat16
MESH_ID = pl.DeviceIdType.MESH


def _dot(a, b):
    return jnp.dot(a, b, preferred_element_type=F32)


def _dot_nt(a, b):
    return lax.dot_general(a, b, (((1,), (1,)), ((), ())), preferred_element_type=F32)


def _dot_tn(a, b):
    return lax.dot_general(a, b, (((0,), (0,)), ((), ())), preferred_element_type=F32)


def _sig(z):
    return 1.0 / (1.0 + jnp.exp(-z))


def _swap_halves(z):
    lane = lax.broadcasted_iota(jnp.int32, z.shape, 1)
    return jnp.where((lane & 63) < 32, pltpu.roll(z, 96, 1), pltpu.roll(z, 32, 1))


def _row_spec(tm, width, col=0):
    return pl.BlockSpec((tm, width), lambda i: (i, col))


def _whole_vmem():
    return pl.BlockSpec(memory_space=pltpu.VMEM)


def _params(*sem, vmem_limit_bytes=None):
    return pltpu.CompilerParams(dimension_semantics=sem, vmem_limit_bytes=vmem_limit_bytes)


def _fwd_in_attn(x, g_pre, wt, cos_t, sin_t, tm):
    t = x.shape[0]
    seq_tiles = SEQ_LEN // tm

    def body(x_ref, g_ref, w_ref, c_ref, s_ref, h_ref, q_ref, kv_ref, g3_ref):
        xf = x_ref[...]
        r = lax.rsqrt(jnp.mean(xf * xf, axis=-1, keepdims=True) + RMS_EPS)
        hh = ((xf * r) * g_ref[...]).astype(BF16)
        h_ref[...] = hh
        c = c_ref[...]
        s = s_ref[...]

        def rope(z):
            return z * c + _swap_halves(z) * s

        q = _dot_nt(hh, w_ref[ROW_Q:ROW_Q + D, :])
        for j in range(D // 128):
            q_ref[:, j * 128:(j + 1) * 128] = (rope(q[:, j * 128:(j + 1) * 128]) * LOGIT_SCALE).astype(BF16)
        kv = _dot_nt(hh, w_ref[ROW_KV:ROW_KV + 256, :])
        kv_ref[:, 0:128] = rope(kv[:, 0:128]).astype(BF16)
        kv_ref[:, 128:256] = kv[:, 128:256].astype(BF16)
        for j in range(3):
            g3_ref[:, j * D:(j + 1) * D] = _dot_nt(hh, w_ref[ROW_ZA + j * D:ROW_ZA + (j + 1) * D, :])

    tab = pl.BlockSpec((tm, 128), lambda i: (i % seq_tiles, 0))
    return pl.pallas_call(
        body, name="fwd_in_attn", grid=(t // tm,),
        in_specs=[_row_spec(tm, D), pl.BlockSpec((1, D), lambda i: (0, 0)), _whole_vmem(), tab, tab],
        out_specs=[_row_spec(tm, D), _row_spec(tm, D), _row_spec(tm, 256), _row_spec(tm, 3 * D)],
        out_shape=[jax.ShapeDtypeStruct((t, D), BF16), jax.ShapeDtypeStruct((t, D), BF16),
                   jax.ShapeDtypeStruct((t, 256), BF16), jax.ShapeDtypeStruct((t, 3 * D), F32)],
        compiler_params=_params("parallel"),
    )(x, g_pre, wt, cos_t, sin_t)


def _conv_forward(xc, bg, cg, zc, up6, up7, w_ref):
    rows = lax.broadcasted_iota(jnp.int32, xc.shape, 0)
    u = cg * xc
    u_m1 = jnp.where(rows == 0, up7, pltpu.roll(u, 1, 0))
    u_m2 = jnp.where(rows == 0, up6, jnp.where(rows == 1, up7, pltpu.roll(u, 2, 0)))
    yconv = w_ref[0:1, :] * u_m2 + w_ref[1:2, :] * u_m1 + w_ref[2:3, :] * u
    sg = _sig(zc)
    sz = zc * sg
    co = bg * yconv
    return u, u_m1, u_m2, yconv, sg, sz, co


def _fwd_in_conv(h, wt, wconv8, wpc, tm):
    t = h.shape[0]
    seq_tiles = SEQ_LEN // tm

    def body(h_ref, w_ref, wc_ref, wpc_ref, a4_ref, ya_ref, last_u_ref):
        hh = h_ref[...]
        xc, bg, cg, zc = (_dot_nt(hh, w_ref[j * D:(j + 1) * D, :]) for j in range(4))
        for j, z in enumerate((xc, bg, cg, zc)):
            a4_ref[:, j * D:(j + 1) * D] = z.astype(BF16)
        first = pl.program_id(0) % seq_tiles == 0
        up6 = jnp.where(first, 0.0, last_u_ref[6:7, :])
        up7 = jnp.where(first, 0.0, last_u_ref[7:8, :])
        u, _, _, _, _, sz, co = _conv_forward(xc, bg, cg, zc, up6, up7, wc_ref)
        last_u_ref[...] = u[tm - 8:tm, :]
        ya_ref[...] = _dot((sz * co).astype(BF16), wpc_ref[...])

    return pl.pallas_call(
        body, name="fwd_in_conv", grid=(t // tm,),
        in_specs=[_row_spec(tm, D), _whole_vmem(), pl.BlockSpec((8, D), lambda i: (0, 0)), _whole_vmem()],
        out_specs=[_row_spec(tm, 4 * D), _row_spec(tm, D)],
        out_shape=[jax.ShapeDtypeStruct((t, 4 * D), BF16), jax.ShapeDtypeStruct((t, D), F32)],
        scratch_shapes=[pltpu.VMEM((8, D), F32)],
        compiler_params=_params("arbitrary"),
    )(h, wt, wconv8, wpc)


STACK = 4 * BLK


def _band_mask(first):
    qi = lax.broadcasted_iota(jnp.int32, (STACK, 2 * BLK), 0) & (BLK - 1)
    kj = lax.broadcasted_iota(jnp.int32, (STACK, 2 * BLK), 1)
    return (kj > qi) & (kj <= qi + BLK) & (kj >= jnp.where(first, BLK, 0))


def _masked_fill(sink_ref, g, e):
    kj = lax.broadcasted_iota(jnp.int32, (STACK, 2 * BLK), 1)
    sink = jnp.concatenate([jnp.full((BLK, 2 * BLK), sink_ref[0, 2 * (4 * g + jj) + e], F32) for jj in range(4)], axis=0)
    return jnp.where(kj == 0, sink, NEG)


def _padded_pair(before, own):
    z = jnp.concatenate([before, own], axis=0).astype(F32)
    z = jnp.where(lax.broadcasted_iota(jnp.int32, z.shape, 0) == 0, 0.0, z)
    zs = pltpu.roll(z, 64, 1)
    lo = lax.broadcasted_iota(jnp.int32, z.shape, 1) < 64
    zero = jnp.zeros_like(z)
    left = [jnp.where(lo, z, zero).astype(BF16), jnp.where(lo, zs, zero).astype(BF16)]
    right = [jnp.where(lo, zero, zs).astype(BF16), jnp.where(lo, zero, z).astype(BF16)]
    return left, right


def _exp_logits(s, valid, fill):
    s = jnp.where(valid, s, fill)
    m = jnp.max(s, axis=-1, keepdims=True)
    return jnp.exp(s - m), m


def _kv_blocks(kvc_ref, kvp_ref, b, col):
    own = kvc_ref[b * BLK:(b + 1) * BLK, col:col + 128]
    before = kvp_ref[:, col:col + 128] if b == 0 else kvc_ref[(b - 1) * BLK:b * BLK, col:col + 128]
    return before, own


def _fwd_attn(sinks, q, kv, g3, blocks):
    t = q.shape[0]
    tq = blocks * BLK
    seq_blocks = SEQ_LEN // BLK

    def body(sink_ref, q_ref, kvc_ref, kvp_ref, za_ref, attn_ref, ub_ref, lse_ref):
        lo = lax.broadcasted_iota(jnp.int32, (STACK, 128), 1) < 64
        for b in range(blocks):
            rows = slice(b * BLK, (b + 1) * BLK)
            valid = _band_mask((pl.program_id(0) * blocks + b) % seq_blocks == 0)
            k_pad = _padded_pair(*_kv_blocks(kvc_ref, kvp_ref, b, 0))
            v_pad = _padded_pair(*_kv_blocks(kvc_ref, kvp_ref, b, 128))
            for g in range(2):
                qg = jnp.concatenate([q_ref[rows, j * 128:(j + 1) * 128] for j in range(4 * g, 4 * g + 4)], axis=0)
                pv, den = [], []
                for e in range(2):
                    p, m = _exp_logits(_dot_nt(qg, k_pad[e][g]), valid, _masked_fill(sink_ref, g, e))
                    both = _dot(p.astype(BF16), jnp.concatenate([v_pad[e][g], jnp.ones((2 * BLK, 128), BF16)], axis=1))
                    pv.append(both[:, 0:128])
                    den.append(both[:, 128:256])
                    lse_ref[b, 2 * g + e] = m + jnp.log(den[e])
                o = jnp.where(lo, pv[0] / den[0], pv[1] / den[1])
                for jj in range(4):
                    cols = slice((4 * g + jj) * 128, (4 * g + jj + 1) * 128)
                    oj = o[jj * BLK:(jj + 1) * BLK, :]
                    attn_ref[rows, cols] = oj
                    za = za_ref[rows, cols]
                    ub_ref[rows, cols] = (za * _sig(za) * oj).astype(BF16)

    return pl.pallas_call(
        body, name="fwd_attn", grid=(t // tq,),
        in_specs=[pl.BlockSpec(memory_space=pltpu.SMEM), _row_spec(tq, D), _row_spec(tq, 256),
                  pl.BlockSpec((BLK, 256), lambda i: (jnp.maximum(i * blocks - 1, 0), 0)), _row_spec(tq, D, 0)],
        out_specs=[_row_spec(tq, D), _row_spec(tq, D), pl.BlockSpec((blocks, 4, STACK, 128), lambda i: (i, 0, 0, 0))],
        out_shape=[jax.ShapeDtypeStruct((t, D), F32), jax.ShapeDtypeStruct((t, D), BF16),
                   jax.ShapeDtypeStruct((t // BLK, 4, STACK, 128), F32)],
        compiler_params=_params("parallel"),
    )(sinks, q, kv, kv, g3)


def _fwd_out_bwd_head(ya, ub, g3, x, target, g_post, wpa, wout, tm):
    t = x.shape[0]

    def body(ya_ref, ub_ref, ga_ref, gb_ref, x_ref, tgt_ref, gp_ref, wpa_ref, wout_ref,
             loss_ref, dout_ref, dya_ref, dub_ref, dgab_ref, dwout_ref, dwpa_ref, dgp_ref):
        @pl.when(pl.program_id(0) == 0)
        def _():
            loss_ref[...] = jnp.zeros_like(loss_ref)
            dwout_ref[...] = jnp.zeros_like(dwout_ref)
            dwpa_ref[...] = jnp.zeros_like(dwpa_ref)
            dgp_ref[...] = jnp.zeros_like(dgp_ref)

        g = gp_ref[...]
        halves = (slice(0, tm // 2), slice(tm // 2, tm))

        def stage1(rows):
            return _dot(ub_ref[rows, :], wpa_ref[...])

        def stage2(rows, yb):
            sa = _sig(ga_ref[rows, :])
            sb = _sig(gb_ref[rows, :])
            mb = (sa * ya_ref[rows, :] + sb * yb).astype(BF16)
            return sa, sb, mb, _dot(mb, wout_ref[...])

        def stage3(rows, y):
            r = lax.rsqrt(jnp.mean(y * y, axis=-1, keepdims=True) + RMS_EPS)
            n = y * r
            err = (x_ref[rows, :] + n * g) - tgt_ref[rows, :]
            sq = jnp.sum(jnp.sum(err * err, axis=0, keepdims=True), axis=1, keepdims=True)
            dout = err * (1.0 / D)
            dout_ref[rows, :] = dout
            dgp = jnp.sum(dout * n, axis=0, keepdims=True)
            dn = dout * g
            dy = (r * (dn - n * jnp.mean(dn * n, axis=-1, keepdims=True))).astype(BF16)
            return sq, dgp, dy, _dot_nt(dy, wout_ref[...])

        def stage4(rows, dm, sa, sb, yb):
            dya_ref[rows, :] = (dm * sa).astype(BF16)
            dyb = (dm * sb).astype(BF16)
            dgab_ref[rows, 0:D] = (dm * ya_ref[rows, :] * (sa * (1.0 - sa))).astype(BF16)
            dgab_ref[rows, D:2 * D] = (dm * yb * (sb * (1.0 - sb))).astype(BF16)
            dub_ref[rows, :] = _dot_nt(dyb, wpa_ref[...])
            return dyb

        yb = [stage1(rows) for rows in halves]
        s2 = [stage2(rows, yb[k]) for k, rows in enumerate(halves)]
        s3 = [stage3(rows, s2[k][3]) for k, rows in enumerate(halves)]
        dyb = [stage4(rows, s3[k][3], s2[k][0], s2[k][1], yb[k]) for k, rows in enumerate(halves)]
        loss_ref[...] += sum(s[0] for s in s3) * (0.5 / D)
        dgp_ref[0:1, :] += sum(s[1] for s in s3)
        dwout_ref[...] += _dot_tn(jnp.concatenate([s[2] for s in s2], axis=0), jnp.concatenate([s[2] for s in s3], axis=0))
        dwpa_ref[...] += _dot_tn(ub_ref[...], jnp.concatenate(dyb, axis=0))

    return pl.pallas_call(
        body, name="fwd_out_bwd_head", grid=(t // tm,),
        in_specs=[_row_spec(tm, D), _row_spec(tm, D), _row_spec(tm, D, 1), _row_spec(tm, D, 2),
                  _row_spec(tm, D), _row_spec(tm, D), pl.BlockSpec((1, D), lambda i: (0, 0)),
                  _whole_vmem(), _whole_vmem()],
        out_specs=[pl.BlockSpec((8, 128), lambda i: (0, 0)), _row_spec(tm, D), _row_spec(tm, D), _row_spec(tm, D),
                   _row_spec(tm, 2 * D), _whole_vmem(), _whole_vmem(), pl.BlockSpec((8, D), lambda i: (0, 0))],
        out_shape=[jax.ShapeDtypeStruct((8, 128), F32), jax.ShapeDtypeStruct((t, D), F32),
                   jax.ShapeDtypeStruct((t, D), BF16), jax.ShapeDtypeStruct((t, D), F32),
                   jax.ShapeDtypeStruct((t, 2 * D), BF16), jax.ShapeDtypeStruct((D, D), F32),
                   jax.ShapeDtypeStruct((D, D), F32), jax.ShapeDtypeStruct((8, D), F32)],
        compiler_params=_params("arbitrary", vmem_limit_bytes=V7X_VMEM_BYTES - (2 << 20)),
    )(ya, ub, g3, g3, x, target, g_post, wpa, wout)


def _bwd_attn(sinks, q, kv, attn, lse, dub, g3, cos_t, sin_t, wt, blocks):
    t = q.shape[0]
    tq = blocks * BLK
    seq_blocks = SEQ_LEN // BLK

    def body(sink_ref, q_ref, kvc_ref, kvp_ref, attn_ref, lse_ref, dub_ref, za_ref, c_ref, s_ref, w_ref,
             dq_ref, dza_ref, dkv_own_ref, dkv_prev_ref, dsink_ref, dh_ref):
        @pl.when(pl.program_id(0) == 0)
        def _():
            dsink_ref[...] = jnp.zeros_like(dsink_ref)

        lo = lax.broadcasted_iota(jnp.int32, (STACK, 128), 1) < 64
        lane8 = lax.broadcasted_iota(jnp.int32, (8, 128), 1)
        lo2 = lax.broadcasted_iota(jnp.int32, (2 * BLK, 128), 1) < 64
        sink_row = lax.broadcasted_iota(jnp.int32, (2 * BLK, 128), 0) == 0
        dsink = jnp.zeros((8, 128), F32)
        for b in range(blocks):
            rows = slice(b * BLK, (b + 1) * BLK)
            valid = _band_mask((pl.program_id(0) * blocks + b) % seq_blocks == 0)
            k_pad = _padded_pair(*_kv_blocks(kvc_ref, kvp_ref, b, 0))
            v_pad = _padded_pair(*_kv_blocks(kvc_ref, kvp_ref, b, 128))
            c = c_ref[rows, :]
            s = s_ref[rows, :]
            dk_acc, dv_acc = [], []
            for g in range(2):
                qg, dog = [], []
                for j in range(4 * g, 4 * g + 4):
                    cols = slice(j * 128, (j + 1) * 128)
                    za = za_ref[rows, cols]
                    sg = _sig(za)
                    dub = dub_ref[rows, cols]
                    dza_ref[rows, cols] = (dub * attn_ref[rows, cols] * (sg * (1.0 + za * (1.0 - sg)))).astype(BF16)
                    dog.append((dub * (za * sg)).astype(BF16))
                    qg.append(q_ref[rows, cols])
                qg = jnp.concatenate(qg, axis=0)
                dog = jnp.concatenate(dog, axis=0)
                dq = jnp.zeros((STACK, 128), F32)
                ds_both, p_both = [], []
                for e in range(2):
                    s_masked = jnp.where(valid, _dot_nt(qg, k_pad[e][g]), _masked_fill(sink_ref, g, e))
                    lse_rows = lse_ref[b, 2 * g + e]
                    p = jnp.exp(s_masked - jnp.concatenate([lse_rows, lse_rows], axis=1))
                    dp = _dot_nt(dog, v_pad[e][g])
                    ds = p * (dp - jnp.sum(p * dp, axis=-1, keepdims=True))
                    for jj in range(4):
                        tot = jnp.sum(ds[jj * BLK:(jj + 1) * BLK, 0:1], axis=0, keepdims=True)
                        dsink = dsink + jnp.where(lane8 == 2 * (4 * g + jj) + e, tot, 0.0)
                    ds = ds.astype(BF16)
                    dq = dq + _dot(ds, k_pad[e][g])
                    ds_both.append(ds)
                    p_both.append(p.astype(BF16))
                zero = jnp.zeros_like(qg)
                q2 = jnp.concatenate([jnp.where(lo, qg, zero), jnp.where(lo, zero, qg)], axis=0)
                do2 = jnp.concatenate([jnp.where(lo, dog, zero), jnp.where(lo, zero, dog)], axis=0)
                dk_acc.append(_dot_tn(q2, jnp.concatenate(ds_both, axis=0)).T)
                dv_acc.append(_dot_tn(do2, jnp.concatenate(p_both, axis=0)).T)
                for jj in range(4):
                    cols = slice((4 * g + jj) * 128, (4 * g + jj + 1) * 128)
                    dqj = dq[jj * BLK:(jj + 1) * BLK, :] * LOGIT_SCALE
                    dq_ref[rows, cols] = (dqj * c - _swap_halves(dqj) * s).astype(BF16)
            for col, acc in ((0, dk_acc), (128, dv_acc)):
                both = jnp.where(lo2, acc[0] + pltpu.roll(acc[0], 64, 1), acc[1] + pltpu.roll(acc[1], 64, 1))
                both = jnp.where(sink_row, 0.0, both)
                dkv_prev_ref[rows, col:col + 128] = both[0:BLK, :]
                dkv_own_ref[rows, col:col + 128] = both[BLK:2 * BLK, :]
        dsink_ref[...] += dsink
        dh_ref[...] = _dot(dq_ref[...], w_ref[ROW_Q:ROW_KV, :]) + _dot(dza_ref[...], w_ref[ROW_ZA:ROW_GA, :])

    tab = pl.BlockSpec((tq, 128), lambda i: (i % (SEQ_LEN // tq), 0))
    return pl.pallas_call(
        body, name="bwd_attn", grid=(t // tq,),
        in_specs=[pl.BlockSpec(memory_space=pltpu.SMEM), _row_spec(tq, D), _row_spec(tq, 256),
                  pl.BlockSpec((BLK, 256), lambda i: (jnp.maximum(i * blocks - 1, 0), 0)),
                  _row_spec(tq, D), pl.BlockSpec((blocks, 4, STACK, 128), lambda i: (i, 0, 0, 0)),
                  _row_spec(tq, D), _row_spec(tq, D, 0), tab, tab, _whole_vmem()],
        out_specs=[_row_spec(tq, D), _row_spec(tq, D), _row_spec(tq, 256), _row_spec(tq, 256),
                   pl.BlockSpec((8, 128), lambda i: (0, 0)), _row_spec(tq, D)],
        out_shape=[jax.ShapeDtypeStruct((t, D), BF16), jax.ShapeDtypeStruct((t, D), BF16),
                   jax.ShapeDtypeStruct((t, 256), F32), jax.ShapeDtypeStruct((t, 256), F32),
                   jax.ShapeDtypeStruct((8, 128), F32), jax.ShapeDtypeStruct((t, D), F32)],
        compiler_params=_params("arbitrary"),
    )(sinks, q, kv, kv, attn, lse, dub, g3, cos_t, sin_t, wt)


def _bwd_kv_finish(dkv_own, dkv_prev, cos_t, sin_t, h, prev):
    t = dkv_own.shape[0]
    tm = SEQ_LEN
    n_t = t // tm
    seq_tiles = SEQ_LEN // tm
    n_blocks = t // BLK

    def body(own_ref, same_ref, nxt_ref, c_ref, s_ref, h_ref, o32_in, o16_in, out_ref, o32_ref, o16_ref,
             acc_ref, acc16_ref, sems):
        step = pl.program_id(0)

        @pl.when(step == 0)
        def _():
            acc_ref[...] = jnp.zeros_like(acc_ref)

        keep = jnp.where(step % seq_tiles == seq_tiles - 1, 0.0, 1.0)
        shifted = jnp.concatenate([same_ref[BLK:tm, :], nxt_ref[...] * keep], axis=0)
        tot = own_ref[...] + shifted
        dk = tot[:, 0:128]
        out_ref[:, 0:128] = (dk * c_ref[...] - _swap_halves(dk) * s_ref[...]).astype(BF16)
        out_ref[:, 128:256] = tot[:, 128:256].astype(BF16)
        acc_ref[...] += _dot_tn(out_ref[...], h_ref[...])

        @pl.when(step == n_t - 1)
        def _():
            acc16_ref[...] = acc_ref[...].astype(BF16)
            rows = pl.ds(ROW_KV, 256)
            c32 = pltpu.make_async_copy(acc_ref, o32_ref.at[rows], sems.at[0])
            c16 = pltpu.make_async_copy(acc16_ref, o16_ref.at[rows], sems.at[1])
            c32.start()
            c16.start()
            c32.wait()
            c16.wait()

    tab = pl.BlockSpec((tm, 128), lambda i: (i % seq_tiles, 0))
    hbm = pl.BlockSpec(memory_space=pl.ANY)
    out = pl.pallas_call(
        body, name="bwd_kv_finish", grid=(n_t,),
        in_specs=[_row_spec(tm, 256), _row_spec(tm, 256),
                  pl.BlockSpec((BLK, 256), lambda i: (jnp.minimum((i + 1) * (tm // BLK), n_blocks - 1), 0)), tab, tab,
                  _row_spec(tm, D), hbm, hbm],
        out_specs=[_row_spec(tm, 256), hbm, hbm],
        out_shape=[jax.ShapeDtypeStruct((t, 256), BF16), jax.ShapeDtypeStruct((D_IN, D), F32),
                   jax.ShapeDtypeStruct((D_IN, D), BF16)],
        scratch_shapes=[pltpu.VMEM((256, D), F32), pltpu.VMEM((256, D), BF16), pltpu.SemaphoreType.DMA((2,))],
        input_output_aliases={6: 1, 7: 2},
        compiler_params=_params("arbitrary"),
    )(dkv_own, dkv_prev, dkv_prev, cos_t, sin_t, h, *prev)
    return out[0], (out[1], out[2])


STAGE_ROWS = 256


def _bwd_conv(dya, a4, h, wconv8, wpc, tm, parts):
    t = a4.shape[0]
    n_t = t // tm
    sub = tm // parts
    seq_tiles = SEQ_LEN // tm

    def body(dya_ref, xc_ref, bg_ref, cg_ref, zc_ref, xcp_ref, cgp_ref, w_ref, wpc_ref, h_ref,
             da4_ref, dwpc_ref, dwc_ref, o32_ref, o16_ref, acc_ref, stage_ref, later_ref, sems):
        step = pl.program_id(0)
        tile = n_t - 1 - step

        @pl.when(step == 0)
        def _():
            dwpc_ref[...] = jnp.zeros_like(dwpc_ref)
            dwc_ref[...] = jnp.zeros_like(dwc_ref)
            acc_ref[...] = jnp.zeros_like(acc_ref)

        keep_prev = jnp.where(tile % seq_tiles == 0, 0.0, 1.0)
        ends_sequence = tile % seq_tiles == seq_tiles - 1

        def part(p, later):
            r0 = p * sub
            here = slice(r0, r0 + sub)
            if p == 0:
                u_prev = cgp_ref[14:16, :].astype(F32) * xcp_ref[14:16, :].astype(F32) * keep_prev
            else:
                u_prev = cg_ref[r0 - 2:r0, :].astype(F32) * xc_ref[r0 - 2:r0, :].astype(F32)
            xc = xc_ref[here, :].astype(F32)
            bg = bg_ref[here, :].astype(F32)
            cg = cg_ref[here, :].astype(F32)
            zc = zc_ref[here, :].astype(F32)
            u, u_m1, u_m2, yconv, sg, sz, co = _conv_forward(xc, bg, cg, zc, u_prev[0:1, :], u_prev[1:2, :], w_ref)
            ua = (sz * co).astype(BF16)
            dua = _dot_nt(dya_ref[here, :], wpc_ref[...])
            da4_ref[here, 3 * D:4 * D] = (dua * co * (sg * (1.0 + zc * (1.0 - sg)))).astype(BF16)
            dco = dua * sz
            da4_ref[here, D:2 * D] = (dco * yconv).astype(BF16)
            dyc = dco * bg
            dwc = jnp.concatenate([jnp.sum(dyc * s, axis=0, keepdims=True) for s in (u_m2, u_m1, u)], axis=0)
            rows = lax.broadcasted_iota(jnp.int32, xc.shape, 0)
            n0 = later[0:1, :]
            n1 = later[1:2, :]
            dyc_p1 = jnp.where(rows == sub - 1, n0, pltpu.roll(dyc, sub - 1, 0))
            dyc_p2 = jnp.where(rows == sub - 2, n0, jnp.where(rows == sub - 1, n1, pltpu.roll(dyc, sub - 2, 0)))
            du = w_ref[2:3, :] * dyc + w_ref[1:2, :] * dyc_p1 + w_ref[0:1, :] * dyc_p2
            da4_ref[here, 0:D] = (du * cg).astype(BF16)
            da4_ref[here, 2 * D:3 * D] = (du * xc).astype(BF16)
            return ua, dwc, dyc[0:8, :]

        later = jnp.where(ends_sequence, 0.0, later_ref[...])
        uas, dwc = [], jnp.zeros((3, D), F32)
        for p in reversed(range(parts)):
            ua, dwc_p, later = part(p, later)
            uas.insert(0, ua)
            dwc = dwc + dwc_p
        later_ref[...] = later
        dwpc_ref[...] += _dot_tn(jnp.concatenate(uas, axis=0), dya_ref[...])
        dwc_ref[0:3, :] += dwc
        for j in range(4):
            acc_ref[j * D:(j + 1) * D, :] += _dot_tn(da4_ref[:, j * D:(j + 1) * D], h_ref[...])

        @pl.when(step == n_t - 1)
        def _():
            c32 = pltpu.make_async_copy(acc_ref, o32_ref.at[pl.ds(0, 4 * D)], sems.at[0])
            c32.start()
            for j in range(4 * D // STAGE_ROWS):
                rows = pl.ds(j * STAGE_ROWS, STAGE_ROWS)
                stage_ref[...] = acc_ref[rows, :].astype(BF16)
                c16 = pltpu.make_async_copy(stage_ref, o16_ref.at[rows], sems.at[1])
                c16.start()
                c16.wait()
            c32.wait()

    def rows_of_tile(width, col=0):
        return pl.BlockSpec((tm, width), lambda s: (n_t - 1 - s, col))

    def prev(col):
        return pl.BlockSpec((16, D), lambda s: (jnp.maximum((n_t - 1 - s) * (tm // 16) - 1, 0), col))

    hbm = pl.BlockSpec(memory_space=pl.ANY)
    out = pl.pallas_call(
        body, name="bwd_conv", grid=(n_t,),
        in_specs=[rows_of_tile(D), rows_of_tile(D, 0), rows_of_tile(D, 1), rows_of_tile(D, 2), rows_of_tile(D, 3),
                  prev(0), prev(2), pl.BlockSpec((8, D), lambda s: (0, 0)), _whole_vmem(), rows_of_tile(D)],
        out_specs=[rows_of_tile(4 * D), _whole_vmem(), pl.BlockSpec((8, D), lambda s: (0, 0)), hbm, hbm],
        out_shape=[jax.ShapeDtypeStruct((t, 4 * D), BF16), jax.ShapeDtypeStruct((D, D), F32),
                   jax.ShapeDtypeStruct((8, D), F32), jax.ShapeDtypeStruct((D_IN, D), F32),
                   jax.ShapeDtypeStruct((D_IN, D), BF16)],
        scratch_shapes=[pltpu.VMEM((4 * D, D), F32), pltpu.VMEM((STAGE_ROWS, D), BF16), pltpu.VMEM((8, D), F32),
                        pltpu.SemaphoreType.DMA((2,))],
        compiler_params=pltpu.CompilerParams(dimension_semantics=("arbitrary",), vmem_limit_bytes=V7X_VMEM_BYTES - (2 << 20)),
    )(dya, a4, a4, a4, a4, a4, a4, wconv8, wpc, h)
    return out[0], out[1], out[2], (out[3], out[4])


def _bwd_dh(da4, dh_part, dkv, dgab, wt, x, g_pre, dout, tm):
    t = x.shape[0]

    def body(da4_ref, dhp_ref, dkv_ref, dgab_ref, w_ref, x_ref, g_ref, dout_ref, gx_ref, dg_ref):
        @pl.when(pl.program_id(0) == 0)
        def _():
            dg_ref[...] = jnp.zeros_like(dg_ref)

        dh = dhp_ref[...] + _dot(da4_ref[...], w_ref[0:ROW_Q, :])
        dh += _dot(dkv_ref[...], w_ref[ROW_KV:ROW_ZA, :])
        dh += _dot(dgab_ref[...], w_ref[ROW_GA:D_IN, :])
        xf = x_ref[...]
        r = lax.rsqrt(jnp.mean(xf * xf, axis=-1, keepdims=True) + RMS_EPS)
        xn = xf * r
        dg_ref[0:1, :] += jnp.sum(dh * xn, axis=0, keepdims=True)
        dxn = dh * g_ref[...]
        gx_ref[...] = dout_ref[...] + r * (dxn - xn * jnp.mean(dxn * xn, axis=-1, keepdims=True))

    return pl.pallas_call(
        body, name="bwd_dh", grid=(t // tm,),
        in_specs=[_row_spec(tm, 4 * D), _row_spec(tm, D), _row_spec(tm, 256), _row_spec(tm, 2 * D),
                  _whole_vmem(), _row_spec(tm, D), pl.BlockSpec((1, D), lambda i: (0, 0)), _row_spec(tm, D)],
        out_specs=[_row_spec(tm, D), pl.BlockSpec((8, D), lambda i: (0, 0))],
        out_shape=[jax.ShapeDtypeStruct((t, D), F32), jax.ShapeDtypeStruct((8, D), F32)],
        compiler_params=_params("arbitrary"),
    )(da4, dh_part, dkv, dgab, wt, x, g_pre, dout)


def _bwd_dw_in(h, pieces, nb, tm, name, prev):
    n_t = h.shape[0] // tm
    n_a = len(pieces)
    jobs = [(a, row0 + b * nb) for a, (arr, row0) in enumerate(pieces) for b in range(arr.shape[1] // nb)]
    first = [min(k for k, (a, _) in enumerate(jobs) if a == b) for b in range(n_a)]
    n_j = len(jobs)

    def body(*refs):
        h_ref, p_refs = refs[0], refs[1:1 + n_a]
        o32_ref, o16_ref, acc_ref, acc16_ref, sems = refs[-5:]
        j, i = pl.program_id(0), pl.program_id(1)

        def copies(k):
            rows = pl.ds(jobs[k][1], nb)
            return (pltpu.make_async_copy(acc_ref.at[k], o32_ref.at[rows], sems.at[0, k]),
                    pltpu.make_async_copy(acc16_ref.at[k], o16_ref.at[rows], sems.at[1, k]))

        for k, (a, _) in enumerate(jobs):
            @pl.when(j == k)
            def _(k=k, a=a):
                @pl.when(i == 0)
                def _():
                    acc_ref[k] = jnp.zeros((nb, D), F32)

                acc_ref[k] += _dot_tn(p_refs[a][...], h_ref[...])

                @pl.when(i == n_t - 1)
                def _():
                    acc16_ref[k] = acc_ref[k].astype(BF16)
                    for cp in copies(k):
                        cp.start()

        @pl.when((j == n_j - 1) & (i == n_t - 1))
        def _():
            for k in range(n_j):
                for cp in copies(k):
                    cp.wait()

    def piece_spec(a):
        s, e = first[a], first[a] + pieces[a][0].shape[1] // nb
        return pl.BlockSpec((tm, nb), lambda j, i: (jnp.where(j < s, 0, jnp.where(j >= e, n_t - 1, i)),
                                                    jnp.clip(j - s, 0, e - s - 1)))

    hbm = pl.BlockSpec(memory_space=pl.ANY)
    return pl.pallas_call(
        body, name=name, grid=(n_j, n_t),
        in_specs=[pl.BlockSpec((tm, D), lambda j, i: (i, 0))] + [piece_spec(a) for a in range(n_a)] + [hbm, hbm],
        out_specs=[hbm, hbm],
        out_shape=[jax.ShapeDtypeStruct((D_IN, D), F32), jax.ShapeDtypeStruct((D_IN, D), BF16)],
        scratch_shapes=[pltpu.VMEM((n_j, nb, D), F32), pltpu.VMEM((n_j, nb, D), BF16),
                        pltpu.SemaphoreType.DMA((2, n_j))],
        input_output_aliases={1 + n_a: 0, 2 + n_a: 1},
        compiler_params=_params("arbitrary", "arbitrary", vmem_limit_bytes=48 << 20),
    )(h, *[arr for arr, _ in pieces], *prev)


def _place():
    x, y, c = lax.axis_index("x"), lax.axis_index("y"), lax.axis_index("c")
    return x, y, c, 4 * x + 2 * y + c


def _peer(x, y, c, k):
    return (1 - x if k & 4 else x, 1 - y if k & 2 else y, 1 - c if k & 1 else c)


ICI_MASKS = (4, 2, 6)


def _all_gather(shards):
    n = len(shards)

    def body(*refs):
        src, dst = refs[:n], refs[n:2 * n]
        send_sems, recv_sems, local_sems = refs[2 * n:]
        x, y, c, me = _place()
        sibling = _peer(x, y, c, 1)

        def copy(a, s, block, to, own=False):
            return pltpu.make_async_remote_copy(
                src_ref=src[a] if own else dst[a].at[block], dst_ref=dst[a].at[block],
                send_sem=send_sems.at[a * 7 + s], recv_sem=recv_sems.at[a * 7 + s], device_id=to, device_id_type=MESH_ID)

        local = [pltpu.make_async_copy(src[a], dst[a].at[me], local_sems.at[a]) for a in range(n)]
        for cp in local:
            cp.start()
        started = [copy(a, 0, me, sibling, own=True) for a in range(n)]
        started += [copy(a, 1 + j, me, _peer(x, y, c, k), own=True) for j, k in enumerate(ICI_MASKS) for a in range(n)]
        for cp in started:
            cp.start()
        for j, k in enumerate(ICI_MASKS):
            for a in range(n):
                copy(a, 1 + j, me ^ k, sibling).wait_recv()
                fwd = copy(a, 4 + j, me ^ k, sibling)
                fwd.start()
                started.append(fwd)
        for a in range(n):
            copy(a, 0, me ^ 1, sibling).wait_recv()
        for j, k in enumerate(ICI_MASKS):
            for a in range(n):
                copy(a, 4 + j, me ^ 1 ^ k, sibling).wait_recv()
        for cp in started:
            cp.wait_send()
        for cp in local:
            cp.wait()

    hbm = pl.BlockSpec(memory_space=pl.ANY)
    return pl.pallas_call(
        body, name="all_gather_weights",
        in_specs=[hbm] * n, out_specs=[hbm] * n,
        out_shape=[jax.ShapeDtypeStruct((N_DEV,) + s.shape, s.dtype) for s in shards],
        scratch_shapes=[pltpu.SemaphoreType.DMA((7 * n,)), pltpu.SemaphoreType.DMA((7 * n,)),
                        pltpu.SemaphoreType.DMA((n,))],
    )(*shards)


def _direct_copies(src, land, send_sems, recv_sems):
    x, y, c, me = _place()
    return [pltpu.make_async_remote_copy(
        src_ref=src[a], dst_ref=land[a].at[me], send_sem=send_sems.at[a * 7 + k - 1],
        recv_sem=recv_sems.at[a * 7 + k - 1], device_id=_peer(x, y, c, k), device_id_type=MESH_ID)
        for k in range(1, N_DEV) for a in range(len(src))]


def _gather_start(shards, name):
    n = len(shards)

    def body(*refs):
        src, land = refs[:n], refs[n:2 * n]
        send_sems, recv_sems = refs[2 * n], refs[2 * n + 1]
        token_ref = refs[-1]
        for cp in _direct_copies(src, land, send_sems, recv_sems):
            cp.start()
        token_ref[...] = jnp.zeros_like(token_ref)

    hbm = pl.BlockSpec(memory_space=pltpu.HBM)
    sem = pl.BlockSpec(memory_space=pltpu.SEMAPHORE)
    lands = [lax.empty((N_DEV,) + s.shape, s.dtype) for s in shards]
    out = pl.pallas_call(
        body, name=name + "_start",
        out_shape=(pltpu.SemaphoreType.DMA((7 * n,)), pltpu.SemaphoreType.DMA((7 * n,)),
                   *[pltpu.HBM(s.shape, s.dtype) for s in shards], *[pltpu.HBM(s.shape, s.dtype) for s in lands],
                   jax.ShapeDtypeStruct((8, 128), F32)),
        in_specs=[hbm] * (2 * n), out_specs=(sem, sem, *[hbm] * (2 * n), _whole_vmem()),
        input_output_aliases={a: 2 + a for a in range(2 * n)},
        compiler_params=pltpu.CompilerParams(has_side_effects=pltpu.SideEffectType.DATAFLOW_SIDE_EFFECTING),
    )(*[pltpu.with_memory_space_constraint(s, pltpu.HBM) for s in list(shards) + lands])
    return out[0], out[1], out[2:2 + n], out[2 + n:2 + 2 * n], out[-1]


def _gather_wait(send_sems, recv_sems, flying, lands, after, name):
    n = len(flying)

    def body(*refs):
        src, land = refs[:n], refs[n:2 * n]
        for cp in _direct_copies(src, land, refs[2 * n], refs[2 * n + 1]):
            cp.wait_send()
            cp.wait_recv()

    hbm = pl.BlockSpec(memory_space=pltpu.HBM)
    sem = pl.BlockSpec(memory_space=pltpu.SEMAPHORE)
    out = pl.pallas_call(
        body, name=name + "_wait",
        out_shape=tuple(pltpu.HBM(s.shape, s.dtype) for s in list(flying) + list(lands)),
        in_specs=[hbm] * (2 * n) + [sem, sem, pl.BlockSpec(memory_space=pl.ANY)], out_specs=tuple([hbm] * (2 * n)),
        input_output_aliases={a: a for a in range(2 * n)},
        compiler_params=pltpu.CompilerParams(has_side_effects=pltpu.SideEffectType.DATAFLOW_SIDE_EFFECTING),
    )(*flying, *lands, send_sems, recv_sems, after)
    return out[n:]


def _sibling_copies(src, land, send_sems, recv_sems, blocks):
    x, y, c, _ = _place()
    sibling = _peer(x, y, c, 1)
    return [pltpu.make_async_remote_copy(
        src_ref=src[a].at[2 * p + (1 - c)], dst_ref=land[a].at[p], send_sem=send_sems.at[a * 4 + p],
        recv_sem=recv_sems.at[a * 4 + p], device_id=sibling, device_id_type=MESH_ID)
        for a in range(len(src)) for p in blocks[a]]


def _exchange_sibling_start(by_dest, blocks):
    n = len(by_dest)

    def body(*refs):
        for cp in _sibling_copies(refs[:n], refs[n:2 * n], refs[2 * n], refs[2 * n + 1], blocks):
            cp.start()

    hbm = pl.BlockSpec(memory_space=pltpu.HBM)
    sem = pl.BlockSpec(memory_space=pltpu.SEMAPHORE)
    lands = [lax.empty((4,) + s.shape[1:], s.dtype) for s in by_dest]
    out = pl.pallas_call(
        body, name="exchange_sibling_start",
        out_shape=(pltpu.SemaphoreType.DMA((4 * n,)), pltpu.SemaphoreType.DMA((4 * n,)),
                   *[pltpu.HBM(s.shape, s.dtype) for s in by_dest], *[pltpu.HBM(s.shape, s.dtype) for s in lands]),
        in_specs=[hbm] * (2 * n), out_specs=(sem, sem, *[hbm] * (2 * n)),
        input_output_aliases={a: 2 + a for a in range(2 * n)},
        compiler_params=pltpu.CompilerParams(has_side_effects=pltpu.SideEffectType.DATAFLOW_SIDE_EFFECTING),
    )(*[pltpu.with_memory_space_constraint(s, pltpu.HBM) for s in list(by_dest) + lands])
    return out[0], out[1], out[2:2 + n], out[2 + n:]


def _exchange_sibling_last(by_dest, lands, blocks):
    n = len(by_dest)

    def body(*refs):
        copies = _sibling_copies(refs[:n], refs[n:2 * n], refs[-2], refs[-1], blocks)
        for cp in copies:
            cp.start()
        for cp in copies:
            cp.wait_recv()
        for cp in copies:
            cp.wait_send()

    hbm = pl.BlockSpec(memory_space=pl.ANY)
    return pl.pallas_call(
        body, name="exchange_sibling_last", in_specs=[hbm] * (2 * n), out_specs=[hbm] * n,
        out_shape=[jax.ShapeDtypeStruct(s.shape, s.dtype) for s in lands],
        scratch_shapes=[pltpu.SemaphoreType.DMA((4 * n,)), pltpu.SemaphoreType.DMA((4 * n,))],
        input_output_aliases={n + a: a for a in range(n)},
    )(*by_dest, *lands)


def _exchange_sibling_wait(send_sems, recv_sems, flying, lands, blocks):
    n = len(flying)

    def body(*refs):
        for cp in _sibling_copies(refs[:n], refs[n:2 * n], refs[2 * n], refs[2 * n + 1], blocks):
            cp.wait_recv()
            cp.wait_send()

    hbm = pl.BlockSpec(memory_space=pltpu.HBM)
    sem = pl.BlockSpec(memory_space=pltpu.SEMAPHORE)
    out = pl.pallas_call(
        body, name="exchange_sibling_wait",
        out_shape=tuple(pltpu.HBM(s.shape, s.dtype) for s in list(flying) + list(lands)),
        in_specs=[hbm] * (2 * n) + [sem, sem], out_specs=tuple([hbm] * (2 * n)),
        input_output_aliases={a: a for a in range(2 * n)},
        compiler_params=pltpu.CompilerParams(has_side_effects=pltpu.SideEffectType.DATAFLOW_SIDE_EFFECTING),
    )(*[pltpu.with_memory_space_constraint(s, pltpu.HBM) for s in list(flying) + list(lands)], send_sems, recv_sems)
    return out[:n], out[n:]


def _chip_copies(src, land, send_sems, recv_sems):
    x, y, c, _ = _place()
    chip = 2 * x + y
    return [pltpu.make_async_remote_copy(
        src_ref=src[a].at[chip ^ (k >> 1)], dst_ref=land[a].at[j], send_sem=send_sems.at[a * 3 + j],
        recv_sem=recv_sems.at[a * 3 + j], device_id=_peer(x, y, c, k), device_id_type=MESH_ID)
        for j, k in enumerate(ICI_MASKS) for a in range(len(src))]


def _exchange_chips_start(by_chip):
    n = len(by_chip)

    def body(*refs):
        src, land = refs[:n], refs[n:2 * n]
        send_sems, recv_sems = refs[2 * n], refs[2 * n + 1]
        token_ref = refs[-1]
        for cp in _chip_copies(src, land, send_sems, recv_sems):
            cp.start()
        token_ref[...] = jnp.zeros_like(token_ref)

    hbm = pl.BlockSpec(memory_space=pltpu.HBM)
    sem = pl.BlockSpec(memory_space=pltpu.SEMAPHORE)
    lands = [lax.empty((3,) + s.shape[1:], s.dtype) for s in by_chip]
    out = pl.pallas_call(
        body, name="exchange_chips_start",
        out_shape=(pltpu.SemaphoreType.DMA((3 * n,)), pltpu.SemaphoreType.DMA((3 * n,)),
                   *[pltpu.HBM(s.shape, s.dtype) for s in by_chip], *[pltpu.HBM(s.shape, s.dtype) for s in lands],
                   jax.ShapeDtypeStruct((8, 128), F32)),
        in_specs=[hbm] * (2 * n), out_specs=(sem, sem, *[hbm] * (2 * n), _whole_vmem()),
        input_output_aliases={a: 2 + a for a in range(2 * n)},
        compiler_params=pltpu.CompilerParams(has_side_effects=pltpu.SideEffectType.DATAFLOW_SIDE_EFFECTING),
    )(*[pltpu.with_memory_space_constraint(s, pltpu.HBM) for s in list(by_chip) + lands])
    return out[0], out[1], out[2:2 + n], out[2 + n:2 + 2 * n], out[-1]


def _exchange_chips_wait(send_sems, recv_sems, flying, lands, after):
    n = len(flying)

    def body(*refs):
        src, land = refs[:n], refs[n:2 * n]
        send_sems_ref, recv_sems_ref = refs[2 * n], refs[2 * n + 1]
        for cp in _chip_copies(src, land, send_sems_ref, recv_sems_ref):
            cp.wait_send()
            cp.wait_recv()

    hbm = pl.BlockSpec(memory_space=pltpu.HBM)
    sem = pl.BlockSpec(memory_space=pltpu.SEMAPHORE)
    out = pl.pallas_call(
        body, name="exchange_chips_wait",
        out_shape=tuple(pltpu.HBM(s.shape, s.dtype) for s in list(flying) + list(lands)),
        in_specs=[hbm] * (2 * n) + [sem, sem, pl.BlockSpec(memory_space=pl.ANY)], out_specs=tuple([hbm] * (2 * n)),
        input_output_aliases={a: a for a in range(2 * n)},
        compiler_params=pltpu.CompilerParams(has_side_effects=pltpu.SideEffectType.DATAFLOW_SIDE_EFFECTING),
    )(*flying, *lands, send_sems, recv_sems, after)
    return out[n:]


def _adamw_math(w, g, m, v):
    m = ADAM_B1 * m + (1.0 - ADAM_B1) * g
    v = ADAM_B2 * v + (1.0 - ADAM_B2) * (g * g)
    m_hat = m / (1.0 - ADAM_B1 ** ADAM_STEP)
    v_hat = v / (1.0 - ADAM_B2 ** ADAM_STEP)
    return -ADAM_LR * (m_hat / (jnp.sqrt(v_hat) + ADAM_EPS) + ADAM_WD * w), m, v


def _pair_sum(owns, recvs, place_arr, tr, name):
    n = len(owns)
    _, rows, cols = owns[0].shape

    def body(place_ref, *refs):
        for a in range(n):
            s = refs[a][...] + refs[n + a][...].astype(F32)
            refs[3 * n + a][...] = s.astype(BF16)

            @pl.when(pl.program_id(1) == place_ref[1])
            def _(a=a, s=s):
                refs[2 * n + a][...] = s

    by_chip = pl.BlockSpec((None, tr, cols), lambda i, p, place_ref: (p, i, 0))
    mine = pl.BlockSpec((None, tr, cols), lambda i, p, place_ref: (2 * p + place_ref[0], i, 0))
    kept = pl.BlockSpec((tr, cols), lambda i, p, place_ref: (i, 0))
    out = pl.pallas_call(
        body, name=name,
        grid_spec=pltpu.PrefetchScalarGridSpec(
            num_scalar_prefetch=1, grid=(rows // tr, 4), in_specs=[mine] * n + [by_chip] * n,
            out_specs=[kept] * n + [by_chip] * n),
        out_shape=[jax.ShapeDtypeStruct((rows, cols), F32)] * n + [jax.ShapeDtypeStruct((4, rows, cols), BF16)] * n,
        compiler_params=_params("parallel", "arbitrary"),
    )(place_arr, *owns, *recvs)
    return out[:n], out[n:]


def _chip_sum(pairs, recvs, tr, name, adam=None):
    n = len(pairs)
    rows, cols = pairs[0].shape
    n_state = 0 if adam is None else 3 * n

    def body(*refs):
        outs = refs[2 * n + n_state:]
        for a in range(n):
            g = refs[a][...]
            for j in range(3):
                g = g + refs[n + a][j].astype(F32)
            outs[a][...] = g
            if adam is not None:
                w_ref, m_ref, v_ref = (refs[2 * n + s * n + a] for s in range(3))
                outs[n + a][...], outs[2 * n + a][...], outs[3 * n + a][...] = _adamw_math(w_ref[...], g, m_ref[...], v_ref[...])

    blk = pl.BlockSpec((tr, cols), lambda i: (i, 0))
    n_out = n if adam is None else 4 * n
    out = pl.pallas_call(
        body, name=name, grid=(rows // tr,),
        in_specs=[blk] * n + [pl.BlockSpec((3, tr, cols), lambda i: (0, i, 0))] * n + [blk] * n_state,
        out_specs=[blk] * n_out,
        out_shape=[jax.ShapeDtypeStruct((rows, cols), F32)] * n_out,
        compiler_params=_params("parallel"),
    )(*pairs, *recvs, *([] if adam is None else [t for group in adam for t in group]))
    return out if adam is None else (out[:n], out[n:2 * n], out[2 * n:3 * n], out[3 * n:])


def _finish_small(small_all, me, ws, ms, vs):
    n = len(ws)

    def body(me_ref, s_ref, c_ref, *refs):
        g, gc = s_ref[0], c_ref[0]
        for d in range(1, N_DEV):
            g = g + s_ref[d]
            gc = gc + c_ref[d]
        refs[3 * n][...] = g[32:33, 0:1]
        grads = [g[0:1], g[8:9], g[16:17, 0:N_HEADS], gc[0:3]]
        for a in range(n):
            w_ref, m_ref, v_ref = (refs[s * n + a] for s in range(3))
            refs[3 * n + 1 + a][...] = grads[a]
            refs[4 * n + 1 + a][...], refs[5 * n + 1 + a][...], refs[6 * n + 1 + a][...] = _adamw_math(
                w_ref[...], grads[a], m_ref[...], v_ref[...])

    def whole(shape):
        return pl.BlockSpec(shape, lambda i, me_ref: (0,) * len(shape))

    params = [whole(w.shape) for w in ws]
    out = pl.pallas_call(
        body, name="finish_small",
        grid_spec=pltpu.PrefetchScalarGridSpec(
            num_scalar_prefetch=1, grid=(1,),
            in_specs=[whole(small_all.shape), pl.BlockSpec((N_DEV, 8, SHARD_SQ), lambda i, me_ref: (0, 3, me_ref[0]))]
            + params * 3,
            out_specs=[whole((1, 1))] + params * 4),
        out_shape=[jax.ShapeDtypeStruct((1, 1), F32)] + [jax.ShapeDtypeStruct(w.shape, F32) for w in ws] * 4,
    )(me, small_all, small_all, *ws, *ms, *vs)
    return out[0], out[1:1 + n], out[1 + n:1 + 2 * n], out[1 + 2 * n:1 + 3 * n], out[1 + 3 * n:]


def _rope_tables():
    inv_freq = ROPE_THETA ** (-jnp.arange(0, HEAD_DIM, 2, dtype=F32) / HEAD_DIM)
    ang = jnp.arange(SEQ_LEN).astype(F32)[:, None] * inv_freq[None, :]
    cos, sin = jnp.cos(ang), jnp.sin(ang)
    return jnp.tile(cos, (1, 4)), jnp.tile(jnp.concatenate([-sin, sin], axis=1), (1, 2))


def _local_step(x, target, g_pre, g_post, sinks, wt, wconv, squares, start_exchange=None):
    cos_t, sin_t = _rope_tables()
    wconv8 = jnp.pad(wconv, ((0, 5), (0, 0)))
    h, q, kv, g3 = _fwd_in_attn(x, g_pre, wt, cos_t, sin_t, 512)
    wpc, wpa, wout = squares(kv)
    a4, ya = _fwd_in_conv(h, wt, wconv8, wpc, 512)
    attn, ub, lse = _fwd_attn(sinks, q, kv, g3, 4)
    loss8, dout, dya, dub, dgab, dwout, dwpa, dgpost8 = _fwd_out_bwd_head(ya, ub, g3, x, target, g_post, wpa, wout, 512)
    dq, dza, dkv_own, dkv_prev, dsink8, dh_part = _bwd_attn(sinks, q, kv, attn, lse, dub, g3, cos_t, sin_t, wt, 4)
    da4, dwpc, dwconv8, dwt = _bwd_conv(dya, a4, h, wconv8, wpc, 512, 2)
    dkv, dwt = _bwd_kv_finish(dkv_own, dkv_prev, cos_t, sin_t, h, dwt)
    dwt = _bwd_dw_in(h, [(dq, ROW_Q), (dza, ROW_ZA)], 1024, 1024, "bwd_dw_in_q_za", dwt)
    sent = None
    if start_exchange is not None:
        dwt, sent = start_exchange[0](dwt, dwpc, dwpa, dwout)
    dwt32, dwt16 = _bwd_dw_in(h, [(dgab, ROW_GA)], 1024, 1024, "bwd_dw_in_gates", dwt)
    token, pending = (None, None) if start_exchange is None else start_exchange[1](dwt32, dwt16, dwpc, dwpa, dwout, sent)
    g_pre_after = g_pre if token is None else g_pre + token[0:1, 0:1]
    grad_x, dgpre8 = _bwd_dh(da4, dh_part, dkv, dgab, wt, x, g_pre_after, dout, 512)
    small = jnp.concatenate([dgpre8, dgpost8, jnp.pad(dsink8, ((0, 0), (0, D - 128))), dwconv8,
                             jnp.pad(loss8, ((0, 0), (0, D - 128)))], axis=0)
    return loss8[0, 0], grad_x, dwt32, dwt16, dwpc, dwpa, dwout, small, pending


def kernel(x, g_pre, g_post, w_in, w_conv, sinks, w_proj_conv, w_proj_attn, w_out, loss_target, m_g_pre, m_g_post, m_w_in, m_w_conv, m_sinks, m_w_proj_conv, m_w_proj_attn, m_w_out, v_g_pre, v_g_post, v_w_in, v_w_conv, v_sinks, v_w_proj_conv, v_w_proj_attn, v_w_out):
    batch = x.shape[0]
    mx, my, mc, me = _place()
    place_arr = jnp.stack([mc, 2 * mx + my]).astype(jnp.int32)

    g_wt, g_conv = _all_gather([w_in[0].T.astype(BF16), jnp.pad(w_conv[0], ((0, 5), (0, 0)))])
    wt = g_wt.reshape(D_IN, D)
    wconv = g_conv[:, 0:3, :].transpose(1, 0, 2).reshape(3, D)
    sq_mine = [w.astype(BF16) for w in (w_proj_conv[0], w_proj_attn[0], w_out[0])]
    wt, sq_mine = lax.optimization_barrier((wt, sq_mine))
    sq_send, sq_recv, sq_flying, sq_lands, sq_token = _gather_start(sq_mine, "gather_squares")

    def squares(after):
        got = _gather_wait(sq_send, sq_recv, sq_flying, sq_lands, after, "gather_squares")
        return [lax.dynamic_update_index_in_dim(full, mine, me, 0).reshape(D, D) for full, mine in zip(got, sq_mine)]

    early = [(0, 1, 2)] + [(0, 1, 2, 3)] * 3
    late = [(3,)]

    def send_early(dwt, dwpc, dwpa, dwout):
        own_sq = [g.reshape(N_DEV, SHARD_SQ, D) for g in (dwpc, dwpa, dwout)]
        send_sems, recv_sems, flying, lands = _exchange_sibling_start([dwt[1].reshape(N_DEV, SHARD_IN, D)] + own_sq, early)
        return (dwt[0], flying[0].reshape(D_IN, D)), (send_sems, recv_sems, flying[1:], lands)

    def send_late(dwt32, dwt16, dwpc, dwpa, dwout, sent):
        own_in = dwt32.reshape(N_DEV, SHARD_IN, D)
        send_sems, recv_sems, sq_flying, lands = sent
        dwt16 = dwt16.reshape(N_DEV, SHARD_IN, D)
        lands = list(_exchange_sibling_last([dwt16], lands[:1], late)) + list(lands[1:])
        sent_arrays, from_sibling = _exchange_sibling_wait(
            send_sems, recv_sems, [dwt16] + list(sq_flying), lands, early)
        own_sq = sent_arrays[1:]
        in32, in16 = _pair_sum([own_in], from_sibling[:1], place_arr, SHARD_IN // 2, "pair_sum_w_in")
        sq32, sq16 = _pair_sum(own_sq, from_sibling[1:], place_arr, SHARD_SQ, "pair_sum_squares")
        send_sems, recv_sems, flying, lands, token = _exchange_chips_start(list(in16) + list(sq16))
        return token, (send_sems, recv_sems, flying, lands, in32, sq32)

    _, grad_x, _, _, _, _, _, small, pending = _local_step(
        x.reshape(batch * SEQ_LEN, D), loss_target.reshape(batch * SEQ_LEN, D), g_pre + sq_token[0:1, 0:1], g_post,
        sinks, wt, wconv, squares, (send_early, send_late))
    sm_send, sm_recv, sm_flying, sm_lands, sm_token = _gather_start([small], "gather_small")
    send_sems, recv_sems, flying, lands, in32, sq32 = pending
    from_chips = _exchange_chips_wait(send_sems, recv_sems, flying, lands, sm_token)

    o_in = [o[0].T for o in _chip_sum(
        in32, from_chips[:1], SHARD_IN // 3, "chip_sum_adamw_w_in",
        adam=([w_in[0].T], [m_w_in[0].T], [v_w_in[0].T]))]
    g_in_mine, o_in = o_in[0], o_in[1:]
    g_sq, d_sq, m_sq, v_sq = _chip_sum(
        sq32, from_chips[1:], SHARD_SQ, "chip_sum_adamw_squares",
        adam=([w_proj_conv[0], w_proj_attn[0], w_out[0]], [m_w_proj_conv[0], m_w_proj_attn[0], m_w_out[0]],
              [v_w_proj_conv[0], v_w_proj_attn[0], v_w_out[0]]))
    both_done, g_in_mine = lax.optimization_barrier((d_sq[0], g_in_mine))
    (small_all,) = _gather_wait(sm_send, sm_recv, sm_flying, sm_lands, both_done, "gather_small")
    loss, (g_g_pre, g_g_post, g_sinks, g_conv_mine), *o_small = _finish_small(
        lax.dynamic_update_index_in_dim(small_all, small, me, 0), jnp.reshape(me, (1,)).astype(jnp.int32),
        [g_pre, g_post, sinks, w_conv[0]], [m_g_pre, m_g_post, m_sinks, m_w_conv[0]],
        [v_g_pre, v_g_post, v_sinks, v_w_conv[0]])
    loss = loss.reshape(())

    grads = [g_g_pre, g_g_post, g_in_mine[None], g_conv_mine[None], g_sinks] + [g[None] for g in g_sq]
    rest = []
    for idx, sq in enumerate((d_sq, m_sq, v_sq)):
        gp, gq, sk, cv = o_small[idx]
        rest += [gp, gq, o_in[idx][None], cv[None], sk] + [s[None] for s in sq]
    return (loss, grad_x.reshape(batch, SEQ_LEN, D), *grads, *rest)
```

```python
import numpy as np
import jax
import jax.numpy as jnp
from jax import lax
from jax.experimental import pallas as pl
from jax.experimental.pallas import tpu as pltpu

D = 1024
N_HEADS = 16
HEAD_DIM = 64
LOGIT_SCALE = HEAD_DIM ** -0.5
BLK = 128
SEQ_LEN = 2048
D_IN = 8448
ROW_Q, ROW_KV, ROW_ZA, ROW_GA = 4 * D, 5 * D, 5 * D + 256, 6 * D + 256
SHARD_IN = D_IN // 8
SHARD_SQ = D // 8
N_DEV = 8
V7X_VMEM_BYTES = 64 << 20
ROPE_THETA = 10000.0
RMS_EPS = 1e-6
NEG = -1e30
ADAM_LR, ADAM_B1, ADAM_B2, ADAM_EPS, ADAM_WD, ADAM_STEP = 0.001, 0.9, 0.999, 1e-08, 0.01, 10

F32 = jnp.float32
BF16 = jnp.bfloat16
MESH_ID = pl.DeviceIdType.MESH


def _dot(a, b):
    return jnp.dot(a, b, preferred_element_type=F32)


def _dot_nt(a, b):
    return lax.dot_general(a, b, (((1,), (1,)), ((), ())), preferred_element_type=F32)


def _dot_tn(a, b):
    return lax.dot_general(a, b, (((0,), (0,)), ((), ())), preferred_element_type=F32)


def _sig(z):
    return 1.0 / (1.0 + jnp.exp(-z))


def _swap_halves(z):
    lane = lax.broadcasted_iota(jnp.int32, z.shape, 1)
    return jnp.where((lane & 63) < 32, pltpu.roll(z, 96, 1), pltpu.roll(z, 32, 1))


def _row_spec(tm, width, col=0):
    return pl.BlockSpec((tm, width), lambda i: (i, col))


def _whole_vmem():
    return pl.BlockSpec(memory_space=pltpu.VMEM)


def _params(*sem, vmem_limit_bytes=None):
    return pltpu.CompilerParams(dimension_semantics=sem, vmem_limit_bytes=vmem_limit_bytes)


def _fwd_in_attn(x, g_pre, wt, cos_t, sin_t, tm):
    t = x.shape[0]
    seq_tiles = SEQ_LEN // tm

    def body(x_ref, g_ref, w_ref, c_ref, s_ref, h_ref, q_ref, kv_ref, g3_ref):
        xf = x_ref[...]
        r = lax.rsqrt(jnp.mean(xf * xf, axis=-1, keepdims=True) + RMS_EPS)
        hh = ((xf * r) * g_ref[...]).astype(BF16)
        h_ref[...] = hh
        c = c_ref[...]
        s = s_ref[...]

        def rope(z):
            return z * c + _swap_halves(z) * s

        q = _dot_nt(hh, w_ref[ROW_Q:ROW_Q + D, :])
        for j in range(D // 128):
            q_ref[:, j * 128:(j + 1) * 128] = (rope(q[:, j * 128:(j + 1) * 128]) * LOGIT_SCALE).astype(BF16)
        kv = _dot_nt(hh, w_ref[ROW_KV:ROW_KV + 256, :])
        kv_ref[:, 0:128] = rope(kv[:, 0:128]).astype(BF16)
        kv_ref[:, 128:256] = kv[:, 128:256].astype(BF16)
        for j in range(3):
            g3_ref[:, j * D:(j + 1) * D] = _dot_nt(hh, w_ref[ROW_ZA + j * D:ROW_ZA + (j + 1) * D, :])

    tab = pl.BlockSpec((tm, 128), lambda i: (i % seq_tiles, 0))
    return pl.pallas_call(
        body, name="fwd_in_attn", grid=(t // tm,),
        in_specs=[_row_spec(tm, D), pl.BlockSpec((1, D), lambda i: (0, 0)), _whole_vmem(), tab, tab],
        out_specs=[_row_spec(tm, D), _row_spec(tm, D), _row_spec(tm, 256), _row_spec(tm, 3 * D)],
        out_shape=[jax.ShapeDtypeStruct((t, D), BF16), jax.ShapeDtypeStruct((t, D), BF16),
                   jax.ShapeDtypeStruct((t, 256), BF16), jax.ShapeDtypeStruct((t, 3 * D), F32)],
        compiler_params=_params("parallel"),
    )(x, g_pre, wt, cos_t, sin_t)


def _conv_forward(xc, bg, cg, zc, up6, up7, w_ref):
    rows = lax.broadcasted_iota(jnp.int32, xc.shape, 0)
    u = cg * xc
    u_m1 = jnp.where(rows == 0, up7, pltpu.roll(u, 1, 0))
    u_m2 = jnp.where(rows == 0, up6, jnp.where(rows == 1, up7, pltpu.roll(u, 2, 0)))
    yconv = w_ref[0:1, :] * u_m2 + w_ref[1:2, :] * u_m1 + w_ref[2:3, :] * u
    sg = _sig(zc)
    sz = zc * sg
    co = bg * yconv
    return u, u_m1, u_m2, yconv, sg, sz, co


def _fwd_in_conv(h, wt, wconv8, wpc, tm):
    t = h.shape[0]
    seq_tiles = SEQ_LEN // tm

    def body(h_ref, w_ref, wc_ref, wpc_ref, a4_ref, ya_ref, last_u_ref):
        hh = h_ref[...]
        xc, bg, cg, zc = (_dot_nt(hh, w_ref[j * D:(j + 1) * D, :]) for j in range(4))
        for j, z in enumerate((xc, bg, cg, zc)):
            a4_ref[:, j * D:(j + 1) * D] = z.astype(BF16)
        first = pl.program_id(0) % seq_tiles == 0
        up6 = jnp.where(first, 0.0, last_u_ref[6:7, :])
        up7 = jnp.where(first, 0.0, last_u_ref[7:8, :])
        u, _, _, _, _, sz, co = _conv_forward(xc, bg, cg, zc, up6, up7, wc_ref)
        last_u_ref[...] = u[tm - 8:tm, :]
        ya_ref[...] = _dot((sz * co).astype(BF16), wpc_ref[...])

    return pl.pallas_call(
        body, name="fwd_in_conv", grid=(t // tm,),
        in_specs=[_row_spec(tm, D), _whole_vmem(), pl.BlockSpec((8, D), lambda i: (0, 0)), _whole_vmem()],
        out_specs=[_row_spec(tm, 4 * D), _row_spec(tm, D)],
        out_shape=[jax.ShapeDtypeStruct((t, 4 * D), BF16), jax.ShapeDtypeStruct((t, D), F32)],
        scratch_shapes=[pltpu.VMEM((8, D), F32)],
        compiler_params=_params("arbitrary"),
    )(h, wt, wconv8, wpc)


STACK = 4 * BLK


def _band_mask(first):
    qi = lax.broadcasted_iota(jnp.int32, (STACK, 2 * BLK), 0) & (BLK - 1)
    kj = lax.broadcasted_iota(jnp.int32, (STACK, 2 * BLK), 1)
    return (kj > qi) & (kj <= qi + BLK) & (kj >= jnp.where(first, BLK, 0))


def _masked_fill(sink_ref, g, e):
    kj = lax.broadcasted_iota(jnp.int32, (STACK, 2 * BLK), 1)
    sink = jnp.concatenate([jnp.full((BLK, 2 * BLK), sink_ref[0, 2 * (4 * g + jj) + e], F32) for jj in range(4)], axis=0)
    return jnp.where(kj == 0, sink, NEG)


def _padded_pair(before, own):
    z = jnp.concatenate([before, own], axis=0).astype(F32)
    z = jnp.where(lax.broadcasted_iota(jnp.int32, z.shape, 0) == 0, 0.0, z)
    zs = pltpu.roll(z, 64, 1)
    lo = lax.broadcasted_iota(jnp.int32, z.shape, 1) < 64
    zero = jnp.zeros_like(z)
    left = [jnp.where(lo, z, zero).astype(BF16), jnp.where(lo, zs, zero).astype(BF16)]
    right = [jnp.where(lo, zero, zs).astype(BF16), jnp.where(lo, zero, z).astype(BF16)]
    return left, right


def _exp_logits(s, valid, fill):
    s = jnp.where(valid, s, fill)
    m = jnp.max(s, axis=-1, keepdims=True)
    return jnp.exp(s - m), m


def _kv_blocks(kvc_ref, kvp_ref, b, col):
    own = kvc_ref[b * BLK:(b + 1) * BLK, col:col + 128]
    before = kvp_ref[:, col:col + 128] if b == 0 else kvc_ref[(b - 1) * BLK:b * BLK, col:col + 128]
    return before, own


def _fwd_attn(sinks, q, kv, g3, blocks):
    t = q.shape[0]
    tq = blocks * BLK
    seq_blocks = SEQ_LEN // BLK

    def body(sink_ref, q_ref, kvc_ref, kvp_ref, za_ref, attn_ref, ub_ref, lse_ref):
        lo = lax.broadcasted_iota(jnp.int32, (STACK, 128), 1) < 64
        for b in range(blocks):
            rows = slice(b * BLK, (b + 1) * BLK)
            valid = _band_mask((pl.program_id(0) * blocks + b) % seq_blocks == 0)
            k_pad = _padded_pair(*_kv_blocks(kvc_ref, kvp_ref, b, 0))
            v_pad = _padded_pair(*_kv_blocks(kvc_ref, kvp_ref, b, 128))
            for g in range(2):
                qg = jnp.concatenate([q_ref[rows, j * 128:(j + 1) * 128] for j in range(4 * g, 4 * g + 4)], axis=0)
                pv, den = [], []
                for e in range(2):
                    p, m = _exp_logits(_dot_nt(qg, k_pad[e][g]), valid, _masked_fill(sink_ref, g, e))
                    both = _dot(p.astype(BF16), jnp.concatenate([v_pad[e][g], jnp.ones((2 * BLK, 128), BF16)], axis=1))
                    pv.append(both[:, 0:128])
                    den.append(both[:, 128:256])
                    lse_ref[b, 2 * g + e] = m + jnp.log(den[e])
                o = jnp.where(lo, pv[0] / den[0], pv[1] / den[1])
                for jj in range(4):
                    cols = slice((4 * g + jj) * 128, (4 * g + jj + 1) * 128)
                    oj = o[jj * BLK:(jj + 1) * BLK, :]
                    attn_ref[rows, cols] = oj
                    za = za_ref[rows, cols]
                    ub_ref[rows, cols] = (za * _sig(za) * oj).astype(BF16)

    return pl.pallas_call(
        body, name="fwd_attn", grid=(t // tq,),
        in_specs=[pl.BlockSpec(memory_space=pltpu.SMEM), _row_spec(tq, D), _row_spec(tq, 256),
                  pl.BlockSpec((BLK, 256), lambda i: (jnp.maximum(i * blocks - 1, 0), 0)), _row_spec(tq, D, 0)],
        out_specs=[_row_spec(tq, D), _row_spec(tq, D), pl.BlockSpec((blocks, 4, STACK, 128), lambda i: (i, 0, 0, 0))],
        out_shape=[jax.ShapeDtypeStruct((t, D), F32), jax.ShapeDtypeStruct((t, D), BF16),
                   jax.ShapeDtypeStruct((t // BLK, 4, STACK, 128), F32)],
        compiler_params=_params("parallel"),
    )(sinks, q, kv, kv, g3)


def _fwd_out_bwd_head(ya, ub, g3, x, target, g_post, wpa, wout, tm):
    t = x.shape[0]

    def body(ya_ref, ub_ref, ga_ref, gb_ref, x_ref, tgt_ref, gp_ref, wpa_ref, wout_ref,
             loss_ref, dout_ref, dya_ref, dub_ref, dgab_ref, dwout_ref, dwpa_ref, dgp_ref):
        @pl.when(pl.program_id(0) == 0)
        def _():
            loss_ref[...] = jnp.zeros_like(loss_ref)
            dwout_ref[...] = jnp.zeros_like(dwout_ref)
            dwpa_ref[...] = jnp.zeros_like(dwpa_ref)
            dgp_ref[...] = jnp.zeros_like(dgp_ref)

        g = gp_ref[...]
        halves = (slice(0, tm // 2), slice(tm // 2, tm))

        def stage1(rows):
            return _dot(ub_ref[rows, :], wpa_ref[...])

        def stage2(rows, yb):
            sa = _sig(ga_ref[rows, :])
            sb = _sig(gb_ref[rows, :])
            mb = (sa * ya_ref[rows, :] + sb * yb).astype(BF16)
            return sa, sb, mb, _dot(mb, wout_ref[...])

        def stage3(rows, y):
            r = lax.rsqrt(jnp.mean(y * y, axis=-1, keepdims=True) + RMS_EPS)
            n = y * r
            err = (x_ref[rows, :] + n * g) - tgt_ref[rows, :]
            sq = jnp.sum(jnp.sum(err * err, axis=0, keepdims=True), axis=1, keepdims=True)
            dout = err * (1.0 / D)
            dout_ref[rows, :] = dout
            dgp = jnp.sum(dout * n, axis=0, keepdims=True)
            dn = dout * g
            dy = (r * (dn - n * jnp.mean(dn * n, axis=-1, keepdims=True))).astype(BF16)
            return sq, dgp, dy, _dot_nt(dy, wout_ref[...])

        def stage4(rows, dm, sa, sb, yb):
            dya_ref[rows, :] = (dm * sa).astype(BF16)
            dyb = (dm * sb).astype(BF16)
            dgab_ref[rows, 0:D] = (dm * ya_ref[rows, :] * (sa * (1.0 - sa))).astype(BF16)
            dgab_ref[rows, D:2 * D] = (dm * yb * (sb * (1.0 - sb))).astype(BF16)
            dub_ref[rows, :] = _dot_nt(dyb, wpa_ref[...])
            return dyb

        yb = [stage1(rows) for rows in halves]
        s2 = [stage2(rows, yb[k]) for k, rows in enumerate(halves)]
        s3 = [stage3(rows, s2[k][3]) for k, rows in enumerate(halves)]
        dyb = [stage4(rows, s3[k][3], s2[k][0], s2[k][1], yb[k]) for k, rows in enumerate(halves)]
        loss_ref[...] += sum(s[0] for s in s3) * (0.5 / D)
        dgp_ref[0:1, :] += sum(s[1] for s in s3)
        dwout_ref[...] += _dot_tn(jnp.concatenate([s[2] for s in s2], axis=0), jnp.concatenate([s[2] for s in s3], axis=0))
        dwpa_ref[...] += _dot_tn(ub_ref[...], jnp.concatenate(dyb, axis=0))

    return pl.pallas_call(
        body, name="fwd_out_bwd_head", grid=(t // tm,),
        in_specs=[_row_spec(tm, D), _row_spec(tm, D), _row_spec(tm, D, 1), _row_spec(tm, D, 2),
                  _row_spec(tm, D), _row_spec(tm, D), pl.BlockSpec((1, D), lambda i: (0, 0)),
                  _whole_vmem(), _whole_vmem()],
        out_specs=[pl.BlockSpec((8, 128), lambda i: (0, 0)), _row_spec(tm, D), _row_spec(tm, D), _row_spec(tm, D),
                   _row_spec(tm, 2 * D), _whole_vmem(), _whole_vmem(), pl.BlockSpec((8, D), lambda i: (0, 0))],
        out_shape=[jax.ShapeDtypeStruct((8, 128), F32), jax.ShapeDtypeStruct((t, D), F32),
                   jax.ShapeDtypeStruct((t, D), BF16), jax.ShapeDtypeStruct((t, D), F32),
                   jax.ShapeDtypeStruct((t, 2 * D), BF16), jax.ShapeDtypeStruct((D, D), F32),
                   jax.ShapeDtypeStruct((D, D), F32), jax.ShapeDtypeStruct((8, D), F32)],
        compiler_params=_params("arbitrary", vmem_limit_bytes=V7X_VMEM_BYTES - (2 << 20)),
    )(ya, ub, g3, g3, x, target, g_post, wpa, wout)


def _bwd_attn(sinks, q, kv, attn, lse, dub, g3, cos_t, sin_t, wt, blocks):
    t = q.shape[0]
    tq = blocks * BLK
    seq_blocks = SEQ_LEN // BLK

    def body(sink_ref, q_ref, kvc_ref, kvp_ref, attn_ref, lse_ref, dub_ref, za_ref, c_ref, s_ref, w_ref,
             dq_ref, dza_ref, dkv_own_ref, dkv_prev_ref, dsink_ref, dh_ref):
        @pl.when(pl.program_id(0) == 0)
        def _():
            dsink_ref[...] = jnp.zeros_like(dsink_ref)

        lo = lax.broadcasted_iota(jnp.int32, (STACK, 128), 1) < 64
        lane8 = lax.broadcasted_iota(jnp.int32, (8, 128), 1)
        lo2 = lax.broadcasted_iota(jnp.int32, (2 * BLK, 128), 1) < 64
        sink_row = lax.broadcasted_iota(jnp.int32, (2 * BLK, 128), 0) == 0
        dsink = jnp.zeros((8, 128), F32)
        for b in range(blocks):
            rows = slice(b * BLK, (b + 1) * BLK)
            valid = _band_mask((pl.program_id(0) * blocks + b) % seq_blocks == 0)
            k_pad = _padded_pair(*_kv_blocks(kvc_ref, kvp_ref, b, 0))
            v_pad = _padded_pair(*_kv_blocks(kvc_ref, kvp_ref, b, 128))
            c = c_ref[rows, :]
            s = s_ref[rows, :]
            dk_acc, dv_acc = [], []
            for g in range(2):
                qg, dog = [], []
                for j in range(4 * g, 4 * g + 4):
                    cols = slice(j * 128, (j + 1) * 128)
                    za = za_ref[rows, cols]
                    sg = _sig(za)
                    dub = dub_ref[rows, cols]
                    dza_ref[rows, cols] = (dub * attn_ref[rows, cols] * (sg * (1.0 + za * (1.0 - sg)))).astype(BF16)
                    dog.append((dub * (za * sg)).astype(BF16))
                    qg.append(q_ref[rows, cols])
                qg = jnp.concatenate(qg, axis=0)
                dog = jnp.concatenate(dog, axis=0)
                dq = jnp.zeros((STACK, 128), F32)
                ds_both, p_both = [], []
                for e in range(2):
                    s_masked = jnp.where(valid, _dot_nt(qg, k_pad[e][g]), _masked_fill(sink_ref, g, e))
                    lse_rows = lse_ref[b, 2 * g + e]
                    p = jnp.exp(s_masked - jnp.concatenate([lse_rows, lse_rows], axis=1))
                    dp = _dot_nt(dog, v_pad[e][g])
                    ds = p * (dp - jnp.sum(p * dp, axis=-1, keepdims=True))
                    for jj in range(4):
                        tot = jnp.sum(ds[jj * BLK:(jj + 1) * BLK, 0:1], axis=0, keepdims=True)
                        dsink = dsink + jnp.where(lane8 == 2 * (4 * g + jj) + e, tot, 0.0)
                    ds = ds.astype(BF16)
                    dq = dq + _dot(ds, k_pad[e][g])
                    ds_both.append(ds)
                    p_both.append(p.astype(BF16))
                zero = jnp.zeros_like(qg)
                q2 = jnp.concatenate([jnp.where(lo, qg, zero), jnp.where(lo, zero, qg)], axis=0)
                do2 = jnp.concatenate([jnp.where(lo, dog, zero), jnp.where(lo, zero, dog)], axis=0)
                dk_acc.append(_dot_tn(q2, jnp.concatenate(ds_both, axis=0)).T)
                dv_acc.append(_dot_tn(do2, jnp.concatenate(p_both, axis=0)).T)
                for jj in range(4):
                    cols = slice((4 * g + jj) * 128, (4 * g + jj + 1) * 128)
                    dqj = dq[jj * BLK:(jj + 1) * BLK, :] * LOGIT_SCALE
                    dq_ref[rows, cols] = (dqj * c - _swap_halves(dqj) * s).astype(BF16)
            for col, acc in ((0, dk_acc), (128, dv_acc)):
                both = jnp.where(lo2, acc[0] + pltpu.roll(acc[0], 64, 1), acc[1] + pltpu.roll(acc[1], 64, 1))
                both = jnp.where(sink_row, 0.0, both)
                dkv_prev_ref[rows, col:col + 128] = both[0:BLK, :]
                dkv_own_ref[rows, col:col + 128] = both[BLK:2 * BLK, :]
        dsink_ref[...] += dsink
        dh_ref[...] = _dot(dq_ref[...], w_ref[ROW_Q:ROW_KV, :]) + _dot(dza_ref[...], w_ref[ROW_ZA:ROW_GA, :])

    tab = pl.BlockSpec((tq, 128), lambda i: (i % (SEQ_LEN // tq), 0))
    return pl.pallas_call(
        body, name="bwd_attn", grid=(t // tq,),
        in_specs=[pl.BlockSpec(memory_space=pltpu.SMEM), _row_spec(tq, D), _row_spec(tq, 256),
                  pl.BlockSpec((BLK, 256), lambda i: (jnp.maximum(i * blocks - 1, 0), 0)),
                  _row_spec(tq, D), pl.BlockSpec((blocks, 4, STACK, 128), lambda i: (i, 0, 0, 0)),
                  _row_spec(tq, D), _row_spec(tq, D, 0), tab, tab, _whole_vmem()],
        out_specs=[_row_spec(tq, D), _row_spec(tq, D), _row_spec(tq, 256), _row_spec(tq, 256),
                   pl.BlockSpec((8, 128), lambda i: (0, 0)), _row_spec(tq, D)],
        out_shape=[jax.ShapeDtypeStruct((t, D), BF16), jax.ShapeDtypeStruct((t, D), BF16),
                   jax.ShapeDtypeStruct((t, 256), F32), jax.ShapeDtypeStruct((t, 256), F32),
                   jax.ShapeDtypeStruct((8, 128), F32), jax.ShapeDtypeStruct((t, D), F32)],
        compiler_params=_params("arbitrary"),
    )(sinks, q, kv, kv, attn, lse, dub, g3, cos_t, sin_t, wt)


def _bwd_kv_finish(dkv_own, dkv_prev, cos_t, sin_t, h, prev):
    t = dkv_own.shape[0]
    tm = SEQ_LEN
    n_t = t // tm
    seq_tiles = SEQ_LEN // tm
    n_blocks = t // BLK

    def body(own_ref, same_ref, nxt_ref, c_ref, s_ref, h_ref, o32_in, o16_in, out_ref, o32_ref, o16_ref,
             acc_ref, acc16_ref, sems):
        step = pl.program_id(0)

        @pl.when(step == 0)
        def _():
            acc_ref[...] = jnp.zeros_like(acc_ref)

        keep = jnp.where(step % seq_tiles == seq_tiles - 1, 0.0, 1.0)
        shifted = jnp.concatenate([same_ref[BLK:tm, :], nxt_ref[...] * keep], axis=0)
        tot = own_ref[...] + shifted
        dk = tot[:, 0:128]
        out_ref[:, 0:128] = (dk * c_ref[...] - _swap_halves(dk) * s_ref[...]).astype(BF16)
        out_ref[:, 128:256] = tot[:, 128:256].astype(BF16)
        acc_ref[...] += _dot_tn(out_ref[...], h_ref[...])

        @pl.when(step == n_t - 1)
        def _():
            acc16_ref[...] = acc_ref[...].astype(BF16)
            rows = pl.ds(ROW_KV, 256)
            c32 = pltpu.make_async_copy(acc_ref, o32_ref.at[rows], sems.at[0])
            c16 = pltpu.make_async_copy(acc16_ref, o16_ref.at[rows], sems.at[1])
            c32.start()
            c16.start()
            c32.wait()
            c16.wait()

    tab = pl.BlockSpec((tm, 128), lambda i: (i % seq_tiles, 0))
    hbm = pl.BlockSpec(memory_space=pl.ANY)
    out = pl.pallas_call(
        body, name="bwd_kv_finish", grid=(n_t,),
        in_specs=[_row_spec(tm, 256), _row_spec(tm, 256),
                  pl.BlockSpec((BLK, 256), lambda i: (jnp.minimum((i + 1) * (tm // BLK), n_blocks - 1), 0)), tab, tab,
                  _row_spec(tm, D), hbm, hbm],
        out_specs=[_row_spec(tm, 256), hbm, hbm],
        out_shape=[jax.ShapeDtypeStruct((t, 256), BF16), jax.ShapeDtypeStruct((D_IN, D), F32),
                   jax.ShapeDtypeStruct((D_IN, D), BF16)],
        scratch_shapes=[pltpu.VMEM((256, D), F32), pltpu.VMEM((256, D), BF16), pltpu.SemaphoreType.DMA((2,))],
        input_output_aliases={6: 1, 7: 2},
        compiler_params=_params("arbitrary"),
    )(dkv_own, dkv_prev, dkv_prev, cos_t, sin_t, h, *prev)
    return out[0], (out[1], out[2])


STAGE_ROWS = 256


def _bwd_conv(dya, a4, h, wconv8, wpc, tm, parts):
    t = a4.shape[0]
    n_t = t // tm
    sub = tm // parts
    seq_tiles = SEQ_LEN // tm

    def body(dya_ref, xc_ref, bg_ref, cg_ref, zc_ref, xcp_ref, cgp_ref, w_ref, wpc_ref, h_ref,
             da4_ref, dwpc_ref, dwc_ref, o32_ref, o16_ref, acc_ref, stage_ref, later_ref, sems):
        step = pl.program_id(0)
        tile = n_t - 1 - step

        @pl.when(step == 0)
        def _():
            dwpc_ref[...] = jnp.zeros_like(dwpc_ref)
            dwc_ref[...] = jnp.zeros_like(dwc_ref)
            acc_ref[...] = jnp.zeros_like(acc_ref)

        keep_prev = jnp.where(tile % seq_tiles == 0, 0.0, 1.0)
        ends_sequence = tile % seq_tiles == seq_tiles - 1

        def part(p, later):
            r0 = p * sub
            here = slice(r0, r0 + sub)
            if p == 0:
                u_prev = cgp_ref[14:16, :].astype(F32) * xcp_ref[14:16, :].astype(F32) * keep_prev
            else:
                u_prev = cg_ref[r0 - 2:r0, :].astype(F32) * xc_ref[r0 - 2:r0, :].astype(F32)
            xc = xc_ref[here, :].astype(F32)
            bg = bg_ref[here, :].astype(F32)
            cg = cg_ref[here, :].astype(F32)
            zc = zc_ref[here, :].astype(F32)
            u, u_m1, u_m2, yconv, sg, sz, co = _conv_forward(xc, bg, cg, zc, u_prev[0:1, :], u_prev[1:2, :], w_ref)
            ua = (sz * co).astype(BF16)
            dua = _dot_nt(dya_ref[here, :], wpc_ref[...])
            da4_ref[here, 3 * D:4 * D] = (dua * co * (sg * (1.0 + zc * (1.0 - sg)))).astype(BF16)
            dco = dua * sz
            da4_ref[here, D:2 * D] = (dco * yconv).astype(BF16)
            dyc = dco * bg
            dwc = jnp.concatenate([jnp.sum(dyc * s, axis=0, keepdims=True) for s in (u_m2, u_m1, u)], axis=0)
            rows = lax.broadcasted_iota(jnp.int32, xc.shape, 0)
            n0 = later[0:1, :]
            n1 = later[1:2, :]
            dyc_p1 = jnp.where(rows == sub - 1, n0, pltpu.roll(dyc, sub - 1, 0))
            dyc_p2 = jnp.where(rows == sub - 2, n0, jnp.where(rows == sub - 1, n1, pltpu.roll(dyc, sub - 2, 0)))
            du = w_ref[2:3, :] * dyc + w_ref[1:2, :] * dyc_p1 + w_ref[0:1, :] * dyc_p2
            da4_ref[here, 0:D] = (du * cg).astype(BF16)
            da4_ref[here, 2 * D:3 * D] = (du * xc).astype(BF16)
            return ua, dwc, dyc[0:8, :]

        later = jnp.where(ends_sequence, 0.0, later_ref[...])
        uas, dwc = [], jnp.zeros((3, D), F32)
        for p in reversed(range(parts)):
            ua, dwc_p, later = part(p, later)
            uas.insert(0, ua)
            dwc = dwc + dwc_p
        later_ref[...] = later
        dwpc_ref[...] += _dot_tn(jnp.concatenate(uas, axis=0), dya_ref[...])
        dwc_ref[0:3, :] += dwc
        for j in range(4):
            acc_ref[j * D:(j + 1) * D, :] += _dot_tn(da4_ref[:, j * D:(j + 1) * D], h_ref[...])

        @pl.when(step == n_t - 1)
        def _():
            c32 = pltpu.make_async_copy(acc_ref, o32_ref.at[pl.ds(0, 4 * D)], sems.at[0])
            c32.start()
            for j in range(4 * D // STAGE_ROWS):
                rows = pl.ds(j * STAGE_ROWS, STAGE_ROWS)
                stage_ref[...] = acc_ref[rows, :].astype(BF16)
                c16 = pltpu.make_async_copy(stage_ref, o16_ref.at[rows], sems.at[1])
                c16.start()
                c16.wait()
            c32.wait()

    def rows_of_tile(width, col=0):
        return pl.BlockSpec((tm, width), lambda s: (n_t - 1 - s, col))

    def prev(col):
        return pl.BlockSpec((16, D), lambda s: (jnp.maximum((n_t - 1 - s) * (tm // 16) - 1, 0), col))

    hbm = pl.BlockSpec(memory_space=pl.ANY)
    out = pl.pallas_call(
        body, name="bwd_conv", grid=(n_t,),
        in_specs=[rows_of_tile(D), rows_of_tile(D, 0), rows_of_tile(D, 1), rows_of_tile(D, 2), rows_of_tile(D, 3),
                  prev(0), prev(2), pl.BlockSpec((8, D), lambda s: (0, 0)), _whole_vmem(), rows_of_tile(D)],
        out_specs=[rows_of_tile(4 * D), _whole_vmem(), pl.BlockSpec((8, D), lambda s: (0, 0)), hbm, hbm],
        out_shape=[jax.ShapeDtypeStruct((t, 4 * D), BF16), jax.ShapeDtypeStruct((D, D), F32),
                   jax.ShapeDtypeStruct((8, D), F32), jax.ShapeDtypeStruct((D_IN, D), F32),
                   jax.ShapeDtypeStruct((D_IN, D), BF16)],
        scratch_shapes=[pltpu.VMEM((4 * D, D), F32), pltpu.VMEM((STAGE_ROWS, D), BF16), pltpu.VMEM((8, D), F32),
                        pltpu.SemaphoreType.DMA((2,))],
        compiler_params=pltpu.CompilerParams(dimension_semantics=("arbitrary",), vmem_limit_bytes=V7X_VMEM_BYTES - (2 << 20)),
    )(dya, a4, a4, a4, a4, a4, a4, wconv8, wpc, h)
    return out[0], out[1], out[2], (out[3], out[4])


def _bwd_dh(da4, dh_part, dkv, dgab, wt, x, g_pre, dout, tm):
    t = x.shape[0]

    def body(da4_ref, dhp_ref, dkv_ref, dgab_ref, w_ref, x_ref, g_ref, dout_ref, gx_ref, dg_ref):
        @pl.when(pl.program_id(0) == 0)
        def _():
            dg_ref[...] = jnp.zeros_like(dg_ref)

        dh = dhp_ref[...] + _dot(da4_ref[...], w_ref[0:ROW_Q, :])
        dh += _dot(dkv_ref[...], w_ref[ROW_KV:ROW_ZA, :])
        dh += _dot(dgab_ref[...], w_ref[ROW_GA:D_IN, :])
        xf = x_ref[...]
        r = lax.rsqrt(jnp.mean(xf * xf, axis=-1, keepdims=True) + RMS_EPS)
        xn = xf * r
        dg_ref[0:1, :] += jnp.sum(dh * xn, axis=0, keepdims=True)
        dxn = dh * g_ref[...]
        gx_ref[...] = dout_ref[...] + r * (dxn - xn * jnp.mean(dxn * xn, axis=-1, keepdims=True))

    return pl.pallas_call(
        body, name="bwd_dh", grid=(t // tm,),
        in_specs=[_row_spec(tm, 4 * D), _row_spec(tm, D), _row_spec(tm, 256), _row_spec(tm, 2 * D),
                  _whole_vmem(), _row_spec(tm, D), pl.BlockSpec((1, D), lambda i: (0, 0)), _row_spec(tm, D)],
        out_specs=[_row_spec(tm, D), pl.BlockSpec((8, D), lambda i: (0, 0))],
        out_shape=[jax.ShapeDtypeStruct((t, D), F32), jax.ShapeDtypeStruct((8, D), F32)],
        compiler_params=_params("arbitrary"),
    )(da4, dh_part, dkv, dgab, wt, x, g_pre, dout)


def _bwd_dw_in(h, pieces, nb, tm, name, prev):
    n_t = h.shape[0] // tm
    n_a = len(pieces)
    jobs = [(a, row0 + b * nb) for a, (arr, row0) in enumerate(pieces) for b in range(arr.shape[1] // nb)]
    first = [min(k for k, (a, _) in enumerate(jobs) if a == b) for b in range(n_a)]
    n_j = len(jobs)

    def body(*refs):
        h_ref, p_refs = refs[0], refs[1:1 + n_a]
        o32_ref, o16_ref, acc_ref, acc16_ref, sems = refs[-5:]
        j, i = pl.program_id(0), pl.program_id(1)

        def copies(k):
            rows = pl.ds(jobs[k][1], nb)
            return (pltpu.make_async_copy(acc_ref.at[k], o32_ref.at[rows], sems.at[0, k]),
                    pltpu.make_async_copy(acc16_ref.at[k], o16_ref.at[rows], sems.at[1, k]))

        for k, (a, _) in enumerate(jobs):
            @pl.when(j == k)
            def _(k=k, a=a):
                @pl.when(i == 0)
                def _():
                    acc_ref[k] = jnp.zeros((nb, D), F32)

                acc_ref[k] += _dot_tn(p_refs[a][...], h_ref[...])

                @pl.when(i == n_t - 1)
                def _():
                    acc16_ref[k] = acc_ref[k].astype(BF16)
                    for cp in copies(k):
                        cp.start()

        @pl.when((j == n_j - 1) & (i == n_t - 1))
        def _():
            for k in range(n_j):
                for cp in copies(k):
                    cp.wait()

    def piece_spec(a):
        s, e = first[a], first[a] + pieces[a][0].shape[1] // nb
        return pl.BlockSpec((tm, nb), lambda j, i: (jnp.where(j < s, 0, jnp.where(j >= e, n_t - 1, i)),
                                                    jnp.clip(j - s, 0, e - s - 1)))

    hbm = pl.BlockSpec(memory_space=pl.ANY)
    return pl.pallas_call(
        body, name=name, grid=(n_j, n_t),
        in_specs=[pl.BlockSpec((tm, D), lambda j, i: (i, 0))] + [piece_spec(a) for a in range(n_a)] + [hbm, hbm],
        out_specs=[hbm, hbm],
        out_shape=[jax.ShapeDtypeStruct((D_IN, D), F32), jax.ShapeDtypeStruct((D_IN, D), BF16)],
        scratch_shapes=[pltpu.VMEM((n_j, nb, D), F32), pltpu.VMEM((n_j, nb, D), BF16),
                        pltpu.SemaphoreType.DMA((2, n_j))],
        input_output_aliases={1 + n_a: 0, 2 + n_a: 1},
        compiler_params=_params("arbitrary", "arbitrary", vmem_limit_bytes=48 << 20),
    )(h, *[arr for arr, _ in pieces], *prev)


def _place():
    x, y, c = lax.axis_index("x"), lax.axis_index("y"), lax.axis_index("c")
    return x, y, c, 4 * x + 2 * y + c


def _peer(x, y, c, k):
    return (1 - x if k & 4 else x, 1 - y if k & 2 else y, 1 - c if k & 1 else c)


ICI_MASKS = (4, 2, 6)


def _all_gather(shards):
    n = len(shards)

    def body(*refs):
        src, dst = refs[:n], refs[n:2 * n]
        send_sems, recv_sems, local_sems = refs[2 * n:]
        x, y, c, me = _place()
        sibling = _peer(x, y, c, 1)

        def copy(a, s, block, to, own=False):
            return pltpu.make_async_remote_copy(
                src_ref=src[a] if own else dst[a].at[block], dst_ref=dst[a].at[block],
                send_sem=send_sems.at[a * 7 + s], recv_sem=recv_sems.at[a * 7 + s], device_id=to, device_id_type=MESH_ID)

        local = [pltpu.make_async_copy(src[a], dst[a].at[me], local_sems.at[a]) for a in range(n)]
        for cp in local:
            cp.start()
        started = [copy(a, 0, me, sibling, own=True) for a in range(n)]
        started += [copy(a, 1 + j, me, _peer(x, y, c, k), own=True) for j, k in enumerate(ICI_MASKS) for a in range(n)]
        for cp in started:
            cp.start()
        for j, k in enumerate(ICI_MASKS):
            for a in range(n):
                copy(a, 1 + j, me ^ k, sibling).wait_recv()
                fwd = copy(a, 4 + j, me ^ k, sibling)
                fwd.start()
                started.append(fwd)
        for a in range(n):
            copy(a, 0, me ^ 1, sibling).wait_recv()
        for j, k in enumerate(ICI_MASKS):
            for a in range(n):
                copy(a, 4 + j, me ^ 1 ^ k, sibling).wait_recv()
        for cp in started:
            cp.wait_send()
        for cp in local:
            cp.wait()

    hbm = pl.BlockSpec(memory_space=pl.ANY)
    return pl.pallas_call(
        body, name="all_gather_weights",
        in_specs=[hbm] * n, out_specs=[hbm] * n,
        out_shape=[jax.ShapeDtypeStruct((N_DEV,) + s.shape, s.dtype) for s in shards],
        scratch_shapes=[pltpu.SemaphoreType.DMA((7 * n,)), pltpu.SemaphoreType.DMA((7 * n,)),
                        pltpu.SemaphoreType.DMA((n,))],
    )(*shards)


def _direct_copies(src, land, send_sems, recv_sems):
    x, y, c, me = _place()
    return [pltpu.make_async_remote_copy(
        src_ref=src[a], dst_ref=land[a].at[me], send_sem=send_sems.at[a * 7 + k - 1],
        recv_sem=recv_sems.at[a * 7 + k - 1], device_id=_peer(x, y, c, k), device_id_type=MESH_ID)
        for k in range(1, N_DEV) for a in range(len(src))]


def _gather_start(shards, name):
    n = len(shards)

    def body(*refs):
        src, land = refs[:n], refs[n:2 * n]
        send_sems, recv_sems = refs[2 * n], refs[2 * n + 1]
        token_ref = refs[-1]
        for cp in _direct_copies(src, land, send_sems, recv_sems):
            cp.start()
        token_ref[...] = jnp.zeros_like(token_ref)

    hbm = pl.BlockSpec(memory_space=pltpu.HBM)
    sem = pl.BlockSpec(memory_space=pltpu.SEMAPHORE)
    lands = [lax.empty((N_DEV,) + s.shape, s.dtype) for s in shards]
    out = pl.pallas_call(
        body, name=name + "_start",
        out_shape=(pltpu.SemaphoreType.DMA((7 * n,)), pltpu.SemaphoreType.DMA((7 * n,)),
                   *[pltpu.HBM(s.shape, s.dtype) for s in shards], *[pltpu.HBM(s.shape, s.dtype) for s in lands],
                   jax.ShapeDtypeStruct((1, D), F32)),
        in_specs=[hbm] * (2 * n), out_specs=(sem, sem, *[hbm] * (2 * n), _whole_vmem()),
        input_output_aliases={a: 2 + a for a in range(2 * n)},
        compiler_params=pltpu.CompilerParams(has_side_effects=pltpu.SideEffectType.DATAFLOW_SIDE_EFFECTING),
    )(*[pltpu.with_memory_space_constraint(s, pltpu.HBM) for s in list(shards) + lands])
    return out[0], out[1], out[2:2 + n], out[2 + n:2 + 2 * n], out[-1]


def _gather_wait(send_sems, recv_sems, flying, lands, after, name):
    n = len(flying)

    def body(*refs):
        src, land = refs[:n], refs[n:2 * n]
        for cp in _direct_copies(src, land, refs[2 * n], refs[2 * n + 1]):
            cp.wait_send()
            cp.wait_recv()

    hbm = pl.BlockSpec(memory_space=pltpu.HBM)
    sem = pl.BlockSpec(memory_space=pltpu.SEMAPHORE)
    out = pl.pallas_call(
        body, name=name + "_wait",
        out_shape=tuple(pltpu.HBM(s.shape, s.dtype) for s in list(flying) + list(lands)),
        in_specs=[hbm] * (2 * n) + [sem, sem, pl.BlockSpec(memory_space=pl.ANY)], out_specs=tuple([hbm] * (2 * n)),
        input_output_aliases={a: a for a in range(2 * n)},
        compiler_params=pltpu.CompilerParams(has_side_effects=pltpu.SideEffectType.DATAFLOW_SIDE_EFFECTING),
    )(*flying, *lands, send_sems, recv_sems, after)
    return out[n:]


def _sibling_copies(src, land, send_sems, recv_sems, blocks):
    x, y, c, _ = _place()
    sibling = _peer(x, y, c, 1)
    return [pltpu.make_async_remote_copy(
        src_ref=src[a].at[2 * p + (1 - c)], dst_ref=land[a].at[p], send_sem=send_sems.at[a * 4 + p],
        recv_sem=recv_sems.at[a * 4 + p], device_id=sibling, device_id_type=MESH_ID)
        for a in range(len(src)) for p in blocks[a]]


def _exchange_sibling_start(by_dest, blocks):
    n = len(by_dest)

    def body(*refs):
        for cp in _sibling_copies(refs[:n], refs[n:2 * n], refs[2 * n], refs[2 * n + 1], blocks):
            cp.start()

    hbm = pl.BlockSpec(memory_space=pltpu.HBM)
    sem = pl.BlockSpec(memory_space=pltpu.SEMAPHORE)
    lands = [lax.empty((4,) + s.shape[1:], s.dtype) for s in by_dest]
    out = pl.pallas_call(
        body, name="exchange_sibling_start",
        out_shape=(pltpu.SemaphoreType.DMA((4 * n,)), pltpu.SemaphoreType.DMA((4 * n,)),
                   *[pltpu.HBM(s.shape, s.dtype) for s in by_dest], *[pltpu.HBM(s.shape, s.dtype) for s in lands]),
        in_specs=[hbm] * (2 * n), out_specs=(sem, sem, *[hbm] * (2 * n)),
        input_output_aliases={a: 2 + a for a in range(2 * n)},
        compiler_params=pltpu.CompilerParams(has_side_effects=pltpu.SideEffectType.DATAFLOW_SIDE_EFFECTING),
    )(*[pltpu.with_memory_space_constraint(s, pltpu.HBM) for s in list(by_dest) + lands])
    return out[0], out[1], out[2:2 + n], out[2 + n:]


def _exchange_sibling_last(by_dest, lands, blocks):
    n = len(by_dest)

    def body(*refs):
        copies = _sibling_copies(refs[:n], refs[n:2 * n], refs[-2], refs[-1], blocks)
        for cp in copies:
            cp.start()
        for cp in copies:
            cp.wait_recv()
        for cp in copies:
            cp.wait_send()

    hbm = pl.BlockSpec(memory_space=pl.ANY)
    return pl.pallas_call(
        body, name="exchange_sibling_last", in_specs=[hbm] * (2 * n), out_specs=[hbm] * n,
        out_shape=[jax.ShapeDtypeStruct(s.shape, s.dtype) for s in lands],
        scratch_shapes=[pltpu.SemaphoreType.DMA((4 * n,)), pltpu.SemaphoreType.DMA((4 * n,))],
        input_output_aliases={n + a: a for a in range(n)},
    )(*by_dest, *lands)


def _exchange_sibling_wait(send_sems, recv_sems, flying, lands, blocks):
    n = len(flying)

    def body(*refs):
        for cp in _sibling_copies(refs[:n], refs[n:2 * n], refs[2 * n], refs[2 * n + 1], blocks):
            cp.wait_recv()
            cp.wait_send()

    hbm = pl.BlockSpec(memory_space=pltpu.HBM)
    sem = pl.BlockSpec(memory_space=pltpu.SEMAPHORE)
    out = pl.pallas_call(
        body, name="exchange_sibling_wait",
        out_shape=tuple(pltpu.HBM(s.shape, s.dtype) for s in list(flying) + list(lands)),
        in_specs=[hbm] * (2 * n) + [sem, sem], out_specs=tuple([hbm] * (2 * n)),
        input_output_aliases={a: a for a in range(2 * n)},
        compiler_params=pltpu.CompilerParams(has_side_effects=pltpu.SideEffectType.DATAFLOW_SIDE_EFFECTING),
    )(*[pltpu.with_memory_space_constraint(s, pltpu.HBM) for s in list(flying) + list(lands)], send_sems, recv_sems)
    return out[:n], out[n:]


def _chip_copies(src, land, send_sems, recv_sems):
    x, y, c, _ = _place()
    chip = 2 * x + y
    return [pltpu.make_async_remote_copy(
        src_ref=src[a].at[chip ^ (k >> 1)], dst_ref=land[a].at[j], send_sem=send_sems.at[a * 3 + j],
        recv_sem=recv_sems.at[a * 3 + j], device_id=_peer(x, y, c, k), device_id_type=MESH_ID)
        for j, k in enumerate(ICI_MASKS) for a in range(len(src))]


def _exchange_chips_start(by_chip):
    n = len(by_chip)

    def body(*refs):
        src, land = refs[:n], refs[n:2 * n]
        send_sems, recv_sems = refs[2 * n], refs[2 * n + 1]
        token_ref = refs[-1]
        for cp in _chip_copies(src, land, send_sems, recv_sems):
            cp.start()
        token_ref[...] = jnp.zeros_like(token_ref)

    hbm = pl.BlockSpec(memory_space=pltpu.HBM)
    sem = pl.BlockSpec(memory_space=pltpu.SEMAPHORE)
    lands = [lax.empty((3,) + s.shape[1:], s.dtype) for s in by_chip]
    out = pl.pallas_call(
        body, name="exchange_chips_start",
        out_shape=(pltpu.SemaphoreType.DMA((3 * n,)), pltpu.SemaphoreType.DMA((3 * n,)),
                   *[pltpu.HBM(s.shape, s.dtype) for s in by_chip], *[pltpu.HBM(s.shape, s.dtype) for s in lands],
                   jax.ShapeDtypeStruct((1, D), F32)),
        in_specs=[hbm] * (2 * n), out_specs=(sem, sem, *[hbm] * (2 * n), _whole_vmem()),
        input_output_aliases={a: 2 + a for a in range(2 * n)},
        compiler_params=pltpu.CompilerParams(has_side_effects=pltpu.SideEffectType.DATAFLOW_SIDE_EFFECTING),
    )(*[pltpu.with_memory_space_constraint(s, pltpu.HBM) for s in list(by_chip) + lands])
    return out[0], out[1], out[2:2 + n], out[2 + n:2 + 2 * n], out[-1]


def _exchange_chips_wait(send_sems, recv_sems, flying, lands, after):
    n = len(flying)

    def body(*refs):
        src, land = refs[:n], refs[n:2 * n]
        send_sems_ref, recv_sems_ref = refs[2 * n], refs[2 * n + 1]
        for cp in _chip_copies(src, land, send_sems_ref, recv_sems_ref):
            cp.wait_send()
            cp.wait_recv()

    hbm = pl.BlockSpec(memory_space=pltpu.HBM)
    sem = pl.BlockSpec(memory_space=pltpu.SEMAPHORE)
    out = pl.pallas_call(
        body, name="exchange_chips_wait",
        out_shape=tuple(pltpu.HBM(s.shape, s.dtype) for s in list(flying) + list(lands)),
        in_specs=[hbm] * (2 * n) + [sem, sem, pl.BlockSpec(memory_space=pl.ANY)], out_specs=tuple([hbm] * (2 * n)),
        input_output_aliases={a: a for a in range(2 * n)},
        compiler_params=pltpu.CompilerParams(has_side_effects=pltpu.SideEffectType.DATAFLOW_SIDE_EFFECTING),
    )(*flying, *lands, send_sems, recv_sems, after)
    return out[n:]


def _adamw_math(w, g, m, v):
    m = ADAM_B1 * m + (1.0 - ADAM_B1) * g
    v = ADAM_B2 * v + (1.0 - ADAM_B2) * (g * g)
    m_hat = m / (1.0 - ADAM_B1 ** ADAM_STEP)
    v_hat = v / (1.0 - ADAM_B2 ** ADAM_STEP)
    return -ADAM_LR * (m_hat / (jnp.sqrt(v_hat) + ADAM_EPS) + ADAM_WD * w), m, v


def _pair_sum(owns, recvs, place_arr, tr, name):
    n = len(owns)
    _, rows, cols = owns[0].shape

    def body(place_ref, *refs):
        for a in range(n):
            s = refs[a][...] + refs[n + a][...].astype(F32)
            refs[3 * n + a][...] = s.astype(BF16)

            @pl.when(pl.program_id(1) == place_ref[1])
            def _(a=a, s=s):
                refs[2 * n + a][...] = s

    by_chip = pl.BlockSpec((None, tr, cols), lambda i, p, place_ref: (p, i, 0))
    mine = pl.BlockSpec((None, tr, cols), lambda i, p, place_ref: (2 * p + place_ref[0], i, 0))
    kept = pl.BlockSpec((tr, cols), lambda i, p, place_ref: (i, 0))
    out = pl.pallas_call(
        body, name=name,
        grid_spec=pltpu.PrefetchScalarGridSpec(
            num_scalar_prefetch=1, grid=(rows // tr, 4), in_specs=[mine] * n + [by_chip] * n,
            out_specs=[kept] * n + [by_chip] * n),
        out_shape=[jax.ShapeDtypeStruct((rows, cols), F32)] * n + [jax.ShapeDtypeStruct((4, rows, cols), BF16)] * n,
        compiler_params=_params("parallel", "arbitrary"),
    )(place_arr, *owns, *recvs)
    return out[:n], out[n:]


def _chip_sum(pairs, recvs, tr, name, adam=None):
    n = len(pairs)
    rows, cols = pairs[0].shape
    n_state = 0 if adam is None else 3 * n

    def body(*refs):
        outs = refs[2 * n + n_state:]
        for a in range(n):
            g = refs[a][...]
            for j in range(3):
                g = g + refs[n + a][j].astype(F32)
            outs[a][...] = g
            if adam is not None:
                w_ref, m_ref, v_ref = (refs[2 * n + s * n + a] for s in range(3))
                outs[n + a][...], outs[2 * n + a][...], outs[3 * n + a][...] = _adamw_math(w_ref[...], g, m_ref[...], v_ref[...])

    blk = pl.BlockSpec((tr, cols), lambda i: (i, 0))
    n_out = n if adam is None else 4 * n
    out = pl.pallas_call(
        body, name=name, grid=(rows // tr,),
        in_specs=[blk] * n + [pl.BlockSpec((3, tr, cols), lambda i: (0, i, 0))] * n + [blk] * n_state,
        out_specs=[blk] * n_out,
        out_shape=[jax.ShapeDtypeStruct((rows, cols), F32)] * n_out,
        compiler_params=_params("parallel"),
    )(*pairs, *recvs, *([] if adam is None else [t for group in adam for t in group]))
    return out if adam is None else (out[:n], out[n:2 * n], out[2 * n:3 * n], out[3 * n:])


def _finish_small(small_all, me, ws, ms, vs):
    n = len(ws)

    def body(me_ref, s_ref, c_ref, *refs):
        g, gc = s_ref[0], c_ref[0]
        for d in range(1, N_DEV):
            g = g + s_ref[d]
            gc = gc + c_ref[d]
        refs[3 * n][...] = g[32:33, 0:1]
        grads = [g[0:1], g[8:9], g[16:17, 0:N_HEADS], gc[0:3]]
        for a in range(n):
            w_ref, m_ref, v_ref = (refs[s * n + a] for s in range(3))
            outs = [refs[(3 + s) * n + 1 + a] for s in range(4)]
            if len(w_ref.shape) == 3:
                for r in range(w_ref.shape[0]):
                    g_r = grads[a][r:r + 1]
                    outs[0][r] = g_r
                    outs[1][r], outs[2][r], outs[3][r] = _adamw_math(w_ref[r], g_r, m_ref[r], v_ref[r])
            else:
                outs[0][...] = grads[a]
                outs[1][...], outs[2][...], outs[3][...] = _adamw_math(w_ref[...], grads[a], m_ref[...], v_ref[...])

    def whole(shape):
        return pl.BlockSpec(shape, lambda i, me_ref: (0,) * len(shape))

    params = [whole(w.shape) for w in ws]
    out = pl.pallas_call(
        body, name="finish_small",
        grid_spec=pltpu.PrefetchScalarGridSpec(
            num_scalar_prefetch=1, grid=(1,),
            in_specs=[whole(small_all.shape), pl.BlockSpec((N_DEV, 8, SHARD_SQ), lambda i, me_ref: (0, 3, me_ref[0]))]
            + params * 3,
            out_specs=[whole((1, 1))] + params * 4),
        out_shape=[jax.ShapeDtypeStruct((1, 1), F32)] + [jax.ShapeDtypeStruct(w.shape, F32) for w in ws] * 4,
    )(me, small_all, small_all, *ws, *ms, *vs)
    return out[0], out[1:1 + n], out[1 + n:1 + 2 * n], out[1 + 2 * n:1 + 3 * n], out[1 + 3 * n:]


def _rope_tables():
    inv_freq = np.float32(ROPE_THETA) ** (-np.arange(0, HEAD_DIM, 2, dtype=np.float32) / np.float32(HEAD_DIM))
    ang = (np.arange(SEQ_LEN, dtype=np.float32)[:, None] * inv_freq.astype(np.float32)[None, :]).astype(np.float64)
    cos, sin = np.cos(ang).astype(np.float32), np.sin(ang).astype(np.float32)
    return jnp.asarray(np.tile(cos, (1, 4))), jnp.asarray(np.tile(np.concatenate([-sin, sin], axis=1), (1, 2)))


def _local_step(x, target, g_pre, g_post, sinks, wt, wconv, squares, start_exchange=None):
    cos_t, sin_t = _rope_tables()
    wconv8 = jnp.pad(wconv, ((0, 5), (0, 0)))
    h, q, kv, g3 = _fwd_in_attn(x, g_pre, wt, cos_t, sin_t, 512)
    wpc, wpa, wout = squares(kv)
    a4, ya = _fwd_in_conv(h, wt, wconv8, wpc, 512)
    attn, ub, lse = _fwd_attn(sinks, q, kv, g3, 4)
    loss8, dout, dya, dub, dgab, dwout, dwpa, dgpost8 = _fwd_out_bwd_head(ya, ub, g3, x, target, g_post, wpa, wout, 512)
    dq, dza, dkv_own, dkv_prev, dsink8, dh_part = _bwd_attn(sinks, q, kv, attn, lse, dub, g3, cos_t, sin_t, wt, 4)
    da4, dwpc, dwconv8, dwt = _bwd_conv(dya, a4, h, wconv8, wpc, 512, 2)
    dkv, dwt = _bwd_kv_finish(dkv_own, dkv_prev, cos_t, sin_t, h, dwt)
    dwt = _bwd_dw_in(h, [(dq, ROW_Q), (dza, ROW_ZA)], 1024, 1024, "bwd_dw_in_q_za", dwt)
    sent = None
    if start_exchange is not None:
        dwt, sent = start_exchange[0](dwt, dwpc, dwpa, dwout)
    dwt32, dwt16 = _bwd_dw_in(h, [(dgab, ROW_GA)], 1024, 1024, "bwd_dw_in_gates", dwt)
    token, pending = (None, None) if start_exchange is None else start_exchange[1](dwt32, dwt16, dwpc, dwpa, dwout, sent)
    g_pre_after = g_pre if token is None else g_pre + token
    grad_x, dgpre8 = _bwd_dh(da4, dh_part, dkv, dgab, wt, x, g_pre_after, dout, 512)
    small = jnp.concatenate([dgpre8, dgpost8, jnp.pad(dsink8, ((0, 0), (0, D - 128))), dwconv8,
                             jnp.pad(loss8, ((0, 0), (0, D - 128)))], axis=0)
    return loss8[0, 0], grad_x, dwt32, dwt16, dwpc, dwpa, dwout, small, pending


def kernel(x, g_pre, g_post, w_in, w_conv, sinks, w_proj_conv, w_proj_attn, w_out, loss_target, m_g_pre, m_g_post, m_w_in, m_w_conv, m_sinks, m_w_proj_conv, m_w_proj_attn, m_w_out, v_g_pre, v_g_post, v_w_in, v_w_conv, v_sinks, v_w_proj_conv, v_w_proj_attn, v_w_out):
    batch = x.shape[0]
    mx, my, mc, me = _place()
    place_arr = jnp.stack([mc, 2 * mx + my]).astype(jnp.int32)

    g_wt, g_conv = _all_gather([w_in[0].T.astype(BF16), jnp.pad(w_conv[0], ((0, 5), (0, 0)))])
    wt = g_wt.reshape(D_IN, D)
    wconv = g_conv[:, 0:3, :].transpose(1, 0, 2).reshape(3, D)
    sq_mine = [w.astype(BF16) for w in (w_proj_conv[0], w_proj_attn[0], w_out[0])]
    wt, sq_mine = lax.optimization_barrier((wt, sq_mine))
    sq_send, sq_recv, sq_flying, sq_lands, sq_token = _gather_start(sq_mine, "gather_squares")

    def squares(after):
        got = _gather_wait(sq_send, sq_recv, sq_flying, sq_lands, after, "gather_squares")
        return [lax.dynamic_update_index_in_dim(full, mine, me, 0).reshape(D, D) for full, mine in zip(got, sq_mine)]

    early = [(0, 1, 2)] + [(0, 1, 2, 3)] * 3
    late = [(3,)]

    def send_early(dwt, dwpc, dwpa, dwout):
        own_sq = [g.reshape(N_DEV, SHARD_SQ, D) for g in (dwpc, dwpa, dwout)]
        send_sems, recv_sems, flying, lands = _exchange_sibling_start([dwt[1].reshape(N_DEV, SHARD_IN, D)] + own_sq, early)
        return (dwt[0], flying[0].reshape(D_IN, D)), (send_sems, recv_sems, flying[1:], lands)

    def send_late(dwt32, dwt16, dwpc, dwpa, dwout, sent):
        own_in = dwt32.reshape(N_DEV, SHARD_IN, D)
        send_sems, recv_sems, sq_flying, lands = sent
        dwt16 = dwt16.reshape(N_DEV, SHARD_IN, D)
        lands = list(_exchange_sibling_last([dwt16], lands[:1], late)) + list(lands[1:])
        sent_arrays, from_sibling = _exchange_sibling_wait(
            send_sems, recv_sems, [dwt16] + list(sq_flying), lands, early)
        own_sq = sent_arrays[1:]
        in32, in16 = _pair_sum([own_in], from_sibling[:1], place_arr, SHARD_IN // 2, "pair_sum_w_in")
        sq32, sq16 = _pair_sum(own_sq, from_sibling[1:], place_arr, SHARD_SQ, "pair_sum_squares")
        send_sems, recv_sems, flying, lands, token = _exchange_chips_start(list(in16) + list(sq16))
        return token, (send_sems, recv_sems, flying, lands, in32, sq32)

    _, grad_x, _, _, _, _, _, small, pending = _local_step(
        x.reshape(batch * SEQ_LEN, D), loss_target.reshape(batch * SEQ_LEN, D), g_pre + sq_token, g_post,
        sinks, wt, wconv, squares, (send_early, send_late))
    sm_send, sm_recv, sm_flying, sm_lands, sm_token = _gather_start([small], "gather_small")
    send_sems, recv_sems, flying, lands, in32, sq32 = pending
    from_chips = _exchange_chips_wait(send_sems, recv_sems, flying, lands, sm_token)

    o_in = [o[0].T for o in _chip_sum(
        in32, from_chips[:1], SHARD_IN // 3, "chip_sum_adamw_w_in",
        adam=([w_in[0].T], [m_w_in[0].T], [v_w_in[0].T]))]
    g_in_mine, o_in = o_in[0], o_in[1:]
    g_sq, d_sq, m_sq, v_sq = _chip_sum(
        sq32, from_chips[1:], SHARD_SQ, "chip_sum_adamw_squares",
        adam=([w_proj_conv[0], w_proj_attn[0], w_out[0]], [m_w_proj_conv[0], m_w_proj_attn[0], m_w_out[0]],
              [v_w_proj_conv[0], v_w_proj_attn[0], v_w_out[0]]))
    both_done, g_in_mine = lax.optimization_barrier((d_sq[0], g_in_mine))
    (small_all,) = _gather_wait(sm_send, sm_recv, sm_flying, sm_lands, both_done, "gather_small")
    def by_row(a):
        return a.transpose(1, 0, 2)

    loss, (g_g_pre, g_g_post, g_sinks, g_conv_mine), *o_small = _finish_small(
        lax.dynamic_update_index_in_dim(small_all, small, me, 0), jnp.reshape(me, (1,)).astype(jnp.int32),
        [g_pre, g_post, sinks, by_row(w_conv)], [m_g_pre, m_g_post, m_sinks, by_row(m_w_conv)],
        [v_g_pre, v_g_post, v_sinks, by_row(v_w_conv)])
    loss = loss.reshape(())

    grads = [g_g_pre, g_g_post, g_in_mine[None], by_row(g_conv_mine), g_sinks] + [g[None] for g in g_sq]
    rest = []
    for idx, sq in enumerate((d_sq, m_sq, v_sq)):
        gp, gq, sk, cv = o_small[idx]
        rest += [gp, gq, o_in[idx][None], by_row(cv), sk] + [s[None] for s in sq]
    return (loss, grad_x.reshape(batch, SEQ_LEN, D), *grads, *rest)
```

```python
import numpy as np
import jax
import jax.numpy as jnp
from jax import lax
from jax.experimental import pallas as pl
from jax.experimental.pallas import tpu as pltpu

D = 1024
N_HEADS = 16
HEAD_DIM = 64
LOGIT_SCALE = HEAD_DIM ** -0.5
BLK = 128
SEQ_LEN = 2048
D_IN = 8448
ROW_Q, ROW_KV, ROW_ZA, ROW_GA = 4 * D, 5 * D, 5 * D + 256, 6 * D + 256
SHARD_IN = D_IN // 8
SHARD_SQ = D // 8
N_DEV = 8
V7X_VMEM_BYTES = 64 << 20
ROPE_THETA = 10000.0
RMS_EPS = 1e-6
NEG = -1e30
ADAM_LR, ADAM_B1, ADAM_B2, ADAM_EPS, ADAM_WD, ADAM_STEP = 0.001, 0.9, 0.999, 1e-08, 0.01, 10

F32 = jnp.float32
BF16 = jnp.bfloat16
MESH_ID = pl.DeviceIdType.MESH


def _dot(a, b):
    return jnp.dot(a, b, preferred_element_type=F32)


def _dot_nt(a, b):
    return lax.dot_general(a, b, (((1,), (1,)), ((), ())), preferred_element_type=F32)


def _dot_tn(a, b):
    return lax.dot_general(a, b, (((0,), (0,)), ((), ())), preferred_element_type=F32)


def _sig(z):
    return 1.0 / (1.0 + jnp.exp(-z))


def _swap_halves(z):
    lane = lax.broadcasted_iota(jnp.int32, z.shape, 1)
    return jnp.where((lane & 63) < 32, pltpu.roll(z, 96, 1), pltpu.roll(z, 32, 1))


def _row_spec(tm, width, col=0):
    return pl.BlockSpec((tm, width), lambda i: (i, col))


def _whole_vmem():
    return pl.BlockSpec(memory_space=pltpu.VMEM)


def _params(*sem, vmem_limit_bytes=None):
    return pltpu.CompilerParams(dimension_semantics=sem, vmem_limit_bytes=vmem_limit_bytes)


def _fwd_in_attn(x, g_pre, wt, cos_t, sin_t, tm):
    t = x.shape[0]
    seq_tiles = SEQ_LEN // tm

    def body(x_ref, g_ref, w_ref, c_ref, s_ref, h_ref, q_ref, kv_ref, g3_ref):
        xf = x_ref[...]
        r = lax.rsqrt(jnp.mean(xf * xf, axis=-1, keepdims=True) + RMS_EPS)
        hh = ((xf * r) * g_ref[...]).astype(BF16)
        h_ref[...] = hh
        c = c_ref[...]
        s = s_ref[...]

        def rope(z):
            return z * c + _swap_halves(z) * s

        q = _dot_nt(hh, w_ref[ROW_Q:ROW_Q + D, :])
        for j in range(D // 128):
            q_ref[:, j * 128:(j + 1) * 128] = (rope(q[:, j * 128:(j + 1) * 128]) * LOGIT_SCALE).astype(BF16)
        kv = _dot_nt(hh, w_ref[ROW_KV:ROW_KV + 256, :])
        kv_ref[:, 0:128] = rope(kv[:, 0:128]).astype(BF16)
        kv_ref[:, 128:256] = kv[:, 128:256].astype(BF16)
        for j in range(3):
            g3_ref[:, j * D:(j + 1) * D] = _dot_nt(hh, w_ref[ROW_ZA + j * D:ROW_ZA + (j + 1) * D, :])

    tab = pl.BlockSpec((tm, 128), lambda i: (i % seq_tiles, 0))
    return pl.pallas_call(
        body, name="fwd_in_attn", grid=(t // tm,),
        in_specs=[_row_spec(tm, D), pl.BlockSpec((1, D), lambda i: (0, 0)), _whole_vmem(), tab, tab],
        out_specs=[_row_spec(tm, D), _row_spec(tm, D), _row_spec(tm, 256), _row_spec(tm, 3 * D)],
        out_shape=[jax.ShapeDtypeStruct((t, D), BF16), jax.ShapeDtypeStruct((t, D), BF16),
                   jax.ShapeDtypeStruct((t, 256), BF16), jax.ShapeDtypeStruct((t, 3 * D), F32)],
        compiler_params=_params("parallel"),
    )(x, g_pre, wt, cos_t, sin_t)


def _conv_forward(xc, bg, cg, zc, up6, up7, w_ref):
    rows = lax.broadcasted_iota(jnp.int32, xc.shape, 0)
    u = cg * xc
    u_m1 = jnp.where(rows == 0, up7, pltpu.roll(u, 1, 0))
    u_m2 = jnp.where(rows == 0, up6, jnp.where(rows == 1, up7, pltpu.roll(u, 2, 0)))
    yconv = w_ref[0:1, :] * u_m2 + w_ref[1:2, :] * u_m1 + w_ref[2:3, :] * u
    sg = _sig(zc)
    sz = zc * sg
    co = bg * yconv
    return u, u_m1, u_m2, yconv, sg, sz, co


def _fwd_in_conv(h, wt, wconv8, wpc, tm):
    t = h.shape[0]
    seq_tiles = SEQ_LEN // tm

    def body(h_ref, w_ref, wc_ref, wpc_ref, a4_ref, ya_ref, last_u_ref):
        hh = h_ref[...]
        xc, bg, cg, zc = (_dot_nt(hh, w_ref[j * D:(j + 1) * D, :]) for j in range(4))
        for j, z in enumerate((xc, bg, cg, zc)):
            a4_ref[:, j * D:(j + 1) * D] = z.astype(BF16)
        first = pl.program_id(0) % seq_tiles == 0
        up6 = jnp.where(first, 0.0, last_u_ref[6:7, :])
        up7 = jnp.where(first, 0.0, last_u_ref[7:8, :])
        u, _, _, _, _, sz, co = _conv_forward(xc, bg, cg, zc, up6, up7, wc_ref)
        last_u_ref[...] = u[tm - 8:tm, :]
        ya_ref[...] = _dot((sz * co).astype(BF16), wpc_ref[...])

    return pl.pallas_call(
        body, name="fwd_in_conv", grid=(t // tm,),
        in_specs=[_row_spec(tm, D), _whole_vmem(), pl.BlockSpec((8, D), lambda i: (0, 0)), _whole_vmem()],
        out_specs=[_row_spec(tm, 4 * D), _row_spec(tm, D)],
        out_shape=[jax.ShapeDtypeStruct((t, 4 * D), BF16), jax.ShapeDtypeStruct((t, D), F32)],
        scratch_shapes=[pltpu.VMEM((8, D), F32)],
        compiler_params=_params("arbitrary"),
    )(h, wt, wconv8, wpc)


STACK = 4 * BLK


def _band_mask(first):
    qi = lax.broadcasted_iota(jnp.int32, (STACK, 2 * BLK), 0) & (BLK - 1)
    kj = lax.broadcasted_iota(jnp.int32, (STACK, 2 * BLK), 1)
    return (kj > qi) & (kj <= qi + BLK) & (kj >= jnp.where(first, BLK, 0))


def _masked_fill(sink_ref, g, e):
    kj = lax.broadcasted_iota(jnp.int32, (STACK, 2 * BLK), 1)
    sink = jnp.concatenate([jnp.full((BLK, 2 * BLK), sink_ref[0, 2 * (4 * g + jj) + e], F32) for jj in range(4)], axis=0)
    return jnp.where(kj == 0, sink, NEG)


def _padded_pair(before, own):
    z = jnp.concatenate([before, own], axis=0).astype(F32)
    z = jnp.where(lax.broadcasted_iota(jnp.int32, z.shape, 0) == 0, 0.0, z)
    zs = pltpu.roll(z, 64, 1)
    lo = lax.broadcasted_iota(jnp.int32, z.shape, 1) < 64
    zero = jnp.zeros_like(z)
    left = [jnp.where(lo, z, zero).astype(BF16), jnp.where(lo, zs, zero).astype(BF16)]
    right = [jnp.where(lo, zero, zs).astype(BF16), jnp.where(lo, zero, z).astype(BF16)]
    return left, right


def _exp_logits(s, valid, fill):
    s = jnp.where(valid, s, fill)
    m = jnp.max(s, axis=-1, keepdims=True)
    return jnp.exp(s - m), m


def _kv_blocks(kvc_ref, kvp_ref, b, col):
    own = kvc_ref[b * BLK:(b + 1) * BLK, col:col + 128]
    before = kvp_ref[:, col:col + 128] if b == 0 else kvc_ref[(b - 1) * BLK:b * BLK, col:col + 128]
    return before, own


def _fwd_attn(sinks, q, kv, g3, blocks):
    t = q.shape[0]
    tq = blocks * BLK
    seq_blocks = SEQ_LEN // BLK

    def body(sink_ref, q_ref, kvc_ref, kvp_ref, za_ref, attn_ref, ub_ref, lse_ref):
        lo = lax.broadcasted_iota(jnp.int32, (STACK, 128), 1) < 64
        for b in range(blocks):
            rows = slice(b * BLK, (b + 1) * BLK)
            valid = _band_mask((pl.program_id(0) * blocks + b) % seq_blocks == 0)
            k_pad = _padded_pair(*_kv_blocks(kvc_ref, kvp_ref, b, 0))
            v_pad = _padded_pair(*_kv_blocks(kvc_ref, kvp_ref, b, 128))
            for g in range(2):
                qg = jnp.concatenate([q_ref[rows, j * 128:(j + 1) * 128] for j in range(4 * g, 4 * g + 4)], axis=0)
                pv, den = [], []
                for e in range(2):
                    p, m = _exp_logits(_dot_nt(qg, k_pad[e][g]), valid, _masked_fill(sink_ref, g, e))
                    both = _dot(p.astype(BF16), jnp.concatenate([v_pad[e][g], jnp.ones((2 * BLK, 128), BF16)], axis=1))
                    pv.append(both[:, 0:128])
                    den.append(both[:, 128:256])
                    lse_ref[b, 2 * g + e] = m + jnp.log(den[e])
                o = jnp.where(lo, pv[0] / den[0], pv[1] / den[1])
                for jj in range(4):
                    cols = slice((4 * g + jj) * 128, (4 * g + jj + 1) * 128)
                    oj = o[jj * BLK:(jj + 1) * BLK, :]
                    attn_ref[rows, cols] = oj
                    za = za_ref[rows, cols]
                    ub_ref[rows, cols] = (za * _sig(za) * oj).astype(BF16)

    return pl.pallas_call(
        body, name="fwd_attn", grid=(t // tq,),
        in_specs=[pl.BlockSpec(memory_space=pltpu.SMEM), _row_spec(tq, D), _row_spec(tq, 256),
                  pl.BlockSpec((BLK, 256), lambda i: (jnp.maximum(i * blocks - 1, 0), 0)), _row_spec(tq, D, 0)],
        out_specs=[_row_spec(tq, D), _row_spec(tq, D), pl.BlockSpec((blocks, 4, STACK, 128), lambda i: (i, 0, 0, 0))],
        out_shape=[jax.ShapeDtypeStruct((t, D), F32), jax.ShapeDtypeStruct((t, D), BF16),
                   jax.ShapeDtypeStruct((t // BLK, 4, STACK, 128), F32)],
        compiler_params=_params("parallel"),
    )(sinks, q, kv, kv, g3)


def _fwd_out_bwd_head(ya, ub, g3, x, target, g_post, wpa, wout, tm):
    t = x.shape[0]

    def body(ya_ref, ub_ref, ga_ref, gb_ref, x_ref, tgt_ref, gp_ref, wpa_ref, wout_ref,
             loss_ref, dout_ref, dya_ref, dub_ref, dgab_ref, dwout_ref, dwpa_ref, dgp_ref):
        @pl.when(pl.program_id(0) == 0)
        def _():
            loss_ref[...] = jnp.zeros_like(loss_ref)
            dwout_ref[...] = jnp.zeros_like(dwout_ref)
            dwpa_ref[...] = jnp.zeros_like(dwpa_ref)
            dgp_ref[...] = jnp.zeros_like(dgp_ref)

        g = gp_ref[...]
        halves = (slice(0, tm // 2), slice(tm // 2, tm))

        def stage1(rows):
            return _dot(ub_ref[rows, :], wpa_ref[...])

        def stage2(rows, yb):
            sa = _sig(ga_ref[rows, :])
            sb = _sig(gb_ref[rows, :])
            mb = (sa * ya_ref[rows, :] + sb * yb).astype(BF16)
            return sa, sb, mb, _dot(mb, wout_ref[...])

        def stage3(rows, y):
            r = lax.rsqrt(jnp.mean(y * y, axis=-1, keepdims=True) + RMS_EPS)
            n = y * r
            err = (x_ref[rows, :] + n * g) - tgt_ref[rows, :]
            sq = jnp.sum(jnp.sum(err * err, axis=0, keepdims=True), axis=1, keepdims=True)
            dout = err * (1.0 / D)
            dout_ref[rows, :] = dout
            dgp = jnp.sum(dout * n, axis=0, keepdims=True)
            dn = dout * g
            dy = (r * (dn - n * jnp.mean(dn * n, axis=-1, keepdims=True))).astype(BF16)
            return sq, dgp, dy, _dot_nt(dy, wout_ref[...])

        def stage4(rows, dm, sa, sb, yb):
            dya_ref[rows, :] = (dm * sa).astype(BF16)
            dyb = (dm * sb).astype(BF16)
            dgab_ref[rows, 0:D] = (dm * ya_ref[rows, :] * (sa * (1.0 - sa))).astype(BF16)
            dgab_ref[rows, D:2 * D] = (dm * yb * (sb * (1.0 - sb))).astype(BF16)
            dub_ref[rows, :] = _dot_nt(dyb, wpa_ref[...])
            return dyb

        yb = [stage1(rows) for rows in halves]
        s2 = [stage2(rows, yb[k]) for k, rows in enumerate(halves)]
        s3 = [stage3(rows, s2[k][3]) for k, rows in enumerate(halves)]
        dyb = [stage4(rows, s3[k][3], s2[k][0], s2[k][1], yb[k]) for k, rows in enumerate(halves)]
        loss_ref[...] += sum(s[0] for s in s3) * (0.5 / D)
        dgp_ref[0:1, :] += sum(s[1] for s in s3)
        dwout_ref[...] += _dot_tn(jnp.concatenate([s[2] for s in s2], axis=0), jnp.concatenate([s[2] for s in s3], axis=0))
        dwpa_ref[...] += _dot_tn(ub_ref[...], jnp.concatenate(dyb, axis=0))

    return pl.pallas_call(
        body, name="fwd_out_bwd_head", grid=(t // tm,),
        in_specs=[_row_spec(tm, D), _row_spec(tm, D), _row_spec(tm, D, 1), _row_spec(tm, D, 2),
                  _row_spec(tm, D), _row_spec(tm, D), pl.BlockSpec((1, D), lambda i: (0, 0)),
                  _whole_vmem(), _whole_vmem()],
        out_specs=[pl.BlockSpec((8, 128), lambda i: (0, 0)), _row_spec(tm, D), _row_spec(tm, D), _row_spec(tm, D),
                   _row_spec(tm, 2 * D), _whole_vmem(), _whole_vmem(), pl.BlockSpec((8, D), lambda i: (0, 0))],
        out_shape=[jax.ShapeDtypeStruct((8, 128), F32), jax.ShapeDtypeStruct((t, D), F32),
                   jax.ShapeDtypeStruct((t, D), BF16), jax.ShapeDtypeStruct((t, D), F32),
                   jax.ShapeDtypeStruct((t, 2 * D), BF16), jax.ShapeDtypeStruct((D, D), F32),
                   jax.ShapeDtypeStruct((D, D), F32), jax.ShapeDtypeStruct((8, D), F32)],
        compiler_params=_params("arbitrary", vmem_limit_bytes=V7X_VMEM_BYTES - (2 << 20)),
    )(ya, ub, g3, g3, x, target, g_post, wpa, wout)


def _bwd_attn(sinks, q, kv, attn, lse, dub, g3, cos_t, sin_t, wt, blocks):
    t = q.shape[0]
    tq = blocks * BLK
    seq_blocks = SEQ_LEN // BLK

    def body(sink_ref, q_ref, kvc_ref, kvp_ref, attn_ref, lse_ref, dub_ref, za_ref, c_ref, s_ref, w_ref,
             dq_ref, dza_ref, dkv_own_ref, dkv_prev_ref, dsink_ref, dh_ref):
        @pl.when(pl.program_id(0) == 0)
        def _():
            dsink_ref[...] = jnp.zeros_like(dsink_ref)

        lo = lax.broadcasted_iota(jnp.int32, (STACK, 128), 1) < 64
        lane8 = lax.broadcasted_iota(jnp.int32, (8, 128), 1)
        lo2 = lax.broadcasted_iota(jnp.int32, (2 * BLK, 128), 1) < 64
        sink_row = lax.broadcasted_iota(jnp.int32, (2 * BLK, 128), 0) == 0
        dsink = jnp.zeros((8, 128), F32)
        for b in range(blocks):
            rows = slice(b * BLK, (b + 1) * BLK)
            valid = _band_mask((pl.program_id(0) * blocks + b) % seq_blocks == 0)
            k_pad = _padded_pair(*_kv_blocks(kvc_ref, kvp_ref, b, 0))
            v_pad = _padded_pair(*_kv_blocks(kvc_ref, kvp_ref, b, 128))
            c = c_ref[rows, :]
            s = s_ref[rows, :]
            dk_acc, dv_acc = [], []
            for g in range(2):
                qg, dog = [], []
                for j in range(4 * g, 4 * g + 4):
                    cols = slice(j * 128, (j + 1) * 128)
                    za = za_ref[rows, cols]
                    sg = _sig(za)
                    dub = dub_ref[rows, cols]
                    dza_ref[rows, cols] = (dub * attn_ref[rows, cols] * (sg * (1.0 + za * (1.0 - sg)))).astype(BF16)
                    dog.append((dub * (za * sg)).astype(BF16))
                    qg.append(q_ref[rows, cols])
                qg = jnp.concatenate(qg, axis=0)
                dog = jnp.concatenate(dog, axis=0)
                dq = jnp.zeros((STACK, 128), F32)
                ds_both, p_both = [], []
                for e in range(2):
                    s_masked = jnp.where(valid, _dot_nt(qg, k_pad[e][g]), _masked_fill(sink_ref, g, e))
                    lse_rows = lse_ref[b, 2 * g + e]
                    p = jnp.exp(s_masked - jnp.concatenate([lse_rows, lse_rows], axis=1))
                    dp = _dot_nt(dog, v_pad[e][g])
                    ds = p * (dp - jnp.sum(p * dp, axis=-1, keepdims=True))
                    for jj in range(4):
                        tot = jnp.sum(ds[jj * BLK:(jj + 1) * BLK, 0:1], axis=0, keepdims=True)
                        dsink = dsink + jnp.where(lane8 == 2 * (4 * g + jj) + e, tot, 0.0)
                    ds = ds.astype(BF16)
                    dq = dq + _dot(ds, k_pad[e][g])
                    ds_both.append(ds)
                    p_both.append(p.astype(BF16))
                zero = jnp.zeros_like(qg)
                q2 = jnp.concatenate([jnp.where(lo, qg, zero), jnp.where(lo, zero, qg)], axis=0)
                do2 = jnp.concatenate([jnp.where(lo, dog, zero), jnp.where(lo, zero, dog)], axis=0)
                dk_acc.append(_dot_tn(q2, jnp.concatenate(ds_both, axis=0)).T)
                dv_acc.append(_dot_tn(do2, jnp.concatenate(p_both, axis=0)).T)
                for jj in range(4):
                    cols = slice((4 * g + jj) * 128, (4 * g + jj + 1) * 128)
                    dqj = dq[jj * BLK:(jj + 1) * BLK, :] * LOGIT_SCALE
                    dq_ref[rows, cols] = (dqj * c - _swap_halves(dqj) * s).astype(BF16)
            for col, acc in ((0, dk_acc), (128, dv_acc)):
                both = jnp.where(lo2, acc[0] + pltpu.roll(acc[0], 64, 1), acc[1] + pltpu.roll(acc[1], 64, 1))
                both = jnp.where(sink_row, 0.0, both)
                dkv_prev_ref[rows, col:col + 128] = both[0:BLK, :]
                dkv_own_ref[rows, col:col + 128] = both[BLK:2 * BLK, :]
        dsink_ref[...] += dsink
        dh_ref[...] = _dot(dq_ref[...], w_ref[ROW_Q:ROW_KV, :]) + _dot(dza_ref[...], w_ref[ROW_ZA:ROW_GA, :])

    tab = pl.BlockSpec((tq, 128), lambda i: (i % (SEQ_LEN // tq), 0))
    return pl.pallas_call(
        body, name="bwd_attn", grid=(t // tq,),
        in_specs=[pl.BlockSpec(memory_space=pltpu.SMEM), _row_spec(tq, D), _row_spec(tq, 256),
                  pl.BlockSpec((BLK, 256), lambda i: (jnp.maximum(i * blocks - 1, 0), 0)),
                  _row_spec(tq, D), pl.BlockSpec((blocks, 4, STACK, 128), lambda i: (i, 0, 0, 0)),
                  _row_spec(tq, D), _row_spec(tq, D, 0), tab, tab, _whole_vmem()],
        out_specs=[_row_spec(tq, D), _row_spec(tq, D), _row_spec(tq, 256), _row_spec(tq, 256),
                   pl.BlockSpec((8, 128), lambda i: (0, 0)), _row_spec(tq, D)],
        out_shape=[jax.ShapeDtypeStruct((t, D), BF16), jax.ShapeDtypeStruct((t, D), BF16),
                   jax.ShapeDtypeStruct((t, 256), F32), jax.ShapeDtypeStruct((t, 256), F32),
                   jax.ShapeDtypeStruct((8, 128), F32), jax.ShapeDtypeStruct((t, D), F32)],
        compiler_params=_params("arbitrary"),
    )(sinks, q, kv, kv, attn, lse, dub, g3, cos_t, sin_t, wt)


def _bwd_kv_finish(dkv_own, dkv_prev, cos_t, sin_t, h, prev):
    t = dkv_own.shape[0]
    tm = SEQ_LEN
    n_t = t // tm
    seq_tiles = SEQ_LEN // tm
    n_blocks = t // BLK

    def body(own_ref, same_ref, nxt_ref, c_ref, s_ref, h_ref, o32_in, o16_in, out_ref, o32_ref, o16_ref,
             acc_ref, acc16_ref, sems):
        step = pl.program_id(0)

        @pl.when(step == 0)
        def _():
            acc_ref[...] = jnp.zeros_like(acc_ref)

        keep = jnp.where(step % seq_tiles == seq_tiles - 1, 0.0, 1.0)
        shifted = jnp.concatenate([same_ref[BLK:tm, :], nxt_ref[...] * keep], axis=0)
        tot = own_ref[...] + shifted
        dk = tot[:, 0:128]
        out_ref[:, 0:128] = (dk * c_ref[...] - _swap_halves(dk) * s_ref[...]).astype(BF16)
        out_ref[:, 128:256] = tot[:, 128:256].astype(BF16)
        acc_ref[...] += _dot_tn(out_ref[...], h_ref[...])

        @pl.when(step == n_t - 1)
        def _():
            acc16_ref[...] = acc_ref[...].astype(BF16)
            rows = pl.ds(ROW_KV, 256)
            c32 = pltpu.make_async_copy(acc_ref, o32_ref.at[rows], sems.at[0])
            c16 = pltpu.make_async_copy(acc16_ref, o16_ref.at[rows], sems.at[1])
            c32.start()
            c16.start()
            c32.wait()
            c16.wait()

    tab = pl.BlockSpec((tm, 128), lambda i: (i % seq_tiles, 0))
    hbm = pl.BlockSpec(memory_space=pl.ANY)
    out = pl.pallas_call(
        body, name="bwd_kv_finish", grid=(n_t,),
        in_specs=[_row_spec(tm, 256), _row_spec(tm, 256),
                  pl.BlockSpec((BLK, 256), lambda i: (jnp.minimum((i + 1) * (tm // BLK), n_blocks - 1), 0)), tab, tab,
                  _row_spec(tm, D), hbm, hbm],
        out_specs=[_row_spec(tm, 256), hbm, hbm],
        out_shape=[jax.ShapeDtypeStruct((t, 256), BF16), jax.ShapeDtypeStruct((D_IN, D), F32),
                   jax.ShapeDtypeStruct((D_IN, D), BF16)],
        scratch_shapes=[pltpu.VMEM((256, D), F32), pltpu.VMEM((256, D), BF16), pltpu.SemaphoreType.DMA((2,))],
        input_output_aliases={6: 1, 7: 2},
        compiler_params=_params("arbitrary"),
    )(dkv_own, dkv_prev, dkv_prev, cos_t, sin_t, h, *prev)
    return out[0], (out[1], out[2])


STAGE_ROWS = 256


def _bwd_conv(dya, a4, h, wconv8, wpc, tm, parts):
    t = a4.shape[0]
    n_t = t // tm
    sub = tm // parts
    seq_tiles = SEQ_LEN // tm

    def body(dya_ref, xc_ref, bg_ref, cg_ref, zc_ref, xcp_ref, cgp_ref, w_ref, wpc_ref, h_ref,
             da4_ref, dwpc_ref, dwc_ref, o32_ref, o16_ref, acc_ref, stage_ref, later_ref, sems):
        step = pl.program_id(0)
        tile = n_t - 1 - step

        @pl.when(step == 0)
        def _():
            dwpc_ref[...] = jnp.zeros_like(dwpc_ref)
            dwc_ref[...] = jnp.zeros_like(dwc_ref)
            acc_ref[...] = jnp.zeros_like(acc_ref)

        keep_prev = jnp.where(tile % seq_tiles == 0, 0.0, 1.0)
        ends_sequence = tile % seq_tiles == seq_tiles - 1

        def part(p, later):
            r0 = p * sub
            here = slice(r0, r0 + sub)
            if p == 0:
                u_prev = cgp_ref[14:16, :].astype(F32) * xcp_ref[14:16, :].astype(F32) * keep_prev
            else:
                u_prev = cg_ref[r0 - 2:r0, :].astype(F32) * xc_ref[r0 - 2:r0, :].astype(F32)
            xc = xc_ref[here, :].astype(F32)
            bg = bg_ref[here, :].astype(F32)
            cg = cg_ref[here, :].astype(F32)
            zc = zc_ref[here, :].astype(F32)
            u, u_m1, u_m2, yconv, sg, sz, co = _conv_forward(xc, bg, cg, zc, u_prev[0:1, :], u_prev[1:2, :], w_ref)
            ua = (sz * co).astype(BF16)
            dua = _dot_nt(dya_ref[here, :], wpc_ref[...])
            da4_ref[here, 3 * D:4 * D] = (dua * co * (sg * (1.0 + zc * (1.0 - sg)))).astype(BF16)
            dco = dua * sz
            da4_ref[here, D:2 * D] = (dco * yconv).astype(BF16)
            dyc = dco * bg
            dwc = jnp.concatenate([jnp.sum(dyc * s, axis=0, keepdims=True) for s in (u_m2, u_m1, u)], axis=0)
            rows = lax.broadcasted_iota(jnp.int32, xc.shape, 0)
            n0 = later[0:1, :]
            n1 = later[1:2, :]
            dyc_p1 = jnp.where(rows == sub - 1, n0, pltpu.roll(dyc, sub - 1, 0))
            dyc_p2 = jnp.where(rows == sub - 2, n0, jnp.where(rows == sub - 1, n1, pltpu.roll(dyc, sub - 2, 0)))
            du = w_ref[2:3, :] * dyc + w_ref[1:2, :] * dyc_p1 + w_ref[0:1, :] * dyc_p2
            da4_ref[here, 0:D] = (du * cg).astype(BF16)
            da4_ref[here, 2 * D:3 * D] = (du * xc).astype(BF16)
            return ua, dwc, dyc[0:8, :]

        later = jnp.where(ends_sequence, 0.0, later_ref[...])
        uas, dwc = [], jnp.zeros((3, D), F32)
        for p in reversed(range(parts)):
            ua, dwc_p, later = part(p, later)
            uas.insert(0, ua)
            dwc = dwc + dwc_p
        later_ref[...] = later
        dwpc_ref[...] += _dot_tn(jnp.concatenate(uas, axis=0), dya_ref[...])
        dwc_ref[0:3, :] += dwc
        for j in range(4):
            acc_ref[j * D:(j + 1) * D, :] += _dot_tn(da4_ref[:, j * D:(j + 1) * D], h_ref[...])

        @pl.when(step == n_t - 1)
        def _():
            c32 = pltpu.make_async_copy(acc_ref, o32_ref.at[pl.ds(0, 4 * D)], sems.at[0])
            c32.start()
            for j in range(4 * D // STAGE_ROWS):
                rows = pl.ds(j * STAGE_ROWS, STAGE_ROWS)
                stage_ref[...] = acc_ref[rows, :].astype(BF16)
                c16 = pltpu.make_async_copy(stage_ref, o16_ref.at[rows], sems.at[1])
                c16.start()
                c16.wait()
            c32.wait()

    def rows_of_tile(width, col=0):
        return pl.BlockSpec((tm, width), lambda s: (n_t - 1 - s, col))

    def prev(col):
        return pl.BlockSpec((16, D), lambda s: (jnp.maximum((n_t - 1 - s) * (tm // 16) - 1, 0), col))

    hbm = pl.BlockSpec(memory_space=pl.ANY)
    out = pl.pallas_call(
        body, name="bwd_conv", grid=(n_t,),
        in_specs=[rows_of_tile(D), rows_of_tile(D, 0), rows_of_tile(D, 1), rows_of_tile(D, 2), rows_of_tile(D, 3),
                  prev(0), prev(2), pl.BlockSpec((8, D), lambda s: (0, 0)), _whole_vmem(), rows_of_tile(D)],
        out_specs=[rows_of_tile(4 * D), _whole_vmem(), pl.BlockSpec((8, D), lambda s: (0, 0)), hbm, hbm],
        out_shape=[jax.ShapeDtypeStruct((t, 4 * D), BF16), jax.ShapeDtypeStruct((D, D), F32),
                   jax.ShapeDtypeStruct((8, D), F32), jax.ShapeDtypeStruct((D_IN, D), F32),
                   jax.ShapeDtypeStruct((D_IN, D), BF16)],
        scratch_shapes=[pltpu.VMEM((4 * D, D), F32), pltpu.VMEM((STAGE_ROWS, D), BF16), pltpu.VMEM((8, D), F32),
                        pltpu.SemaphoreType.DMA((2,))],
        compiler_params=pltpu.CompilerParams(dimension_semantics=("arbitrary",), vmem_limit_bytes=V7X_VMEM_BYTES - (2 << 20)),
    )(dya, a4, a4, a4, a4, a4, a4, wconv8, wpc, h)
    return out[0], out[1], out[2], (out[3], out[4])


def _bwd_dh(da4, dh_part, dkv, dgab, wt, x, g_pre, dout, tm):
    t = x.shape[0]

    def body(da4_ref, dhp_ref, dkv_ref, dgab_ref, w_ref, x_ref, g_ref, dout_ref, gx_ref, dg_ref):
        @pl.when(pl.program_id(0) == 0)
        def _():
            dg_ref[...] = jnp.zeros_like(dg_ref)

        dh = dhp_ref[...] + _dot(da4_ref[...], w_ref[0:ROW_Q, :])
        dh += _dot(dkv_ref[...], w_ref[ROW_KV:ROW_ZA, :])
        dh += _dot(dgab_ref[...], w_ref[ROW_GA:D_IN, :])
        xf = x_ref[...]
        r = lax.rsqrt(jnp.mean(xf * xf, axis=-1, keepdims=True) + RMS_EPS)
        xn = xf * r
        dg_ref[0:1, :] += jnp.sum(dh * xn, axis=0, keepdims=True)
        dxn = dh * g_ref[...]
        gx_ref[...] = dout_ref[...] + r * (dxn - xn * jnp.mean(dxn * xn, axis=-1, keepdims=True))

    return pl.pallas_call(
        body, name="bwd_dh", grid=(t // tm,),
        in_specs=[_row_spec(tm, 4 * D), _row_spec(tm, D), _row_spec(tm, 256), _row_spec(tm, 2 * D),
                  _whole_vmem(), _row_spec(tm, D), pl.BlockSpec((1, D), lambda i: (0, 0)), _row_spec(tm, D)],
        out_specs=[_row_spec(tm, D), pl.BlockSpec((8, D), lambda i: (0, 0))],
        out_shape=[jax.ShapeDtypeStruct((t, D), F32), jax.ShapeDtypeStruct((8, D), F32)],
        compiler_params=_params("arbitrary"),
    )(da4, dh_part, dkv, dgab, wt, x, g_pre, dout)


def _bwd_dw_in(h, pieces, nb, tm, name, prev):
    n_t = h.shape[0] // tm
    n_a = len(pieces)
    jobs = [(a, row0 + b * nb) for a, (arr, row0) in enumerate(pieces) for b in range(arr.shape[1] // nb)]
    first = [min(k for k, (a, _) in enumerate(jobs) if a == b) for b in range(n_a)]
    n_j = len(jobs)

    def body(*refs):
        h_ref, p_refs = refs[0], refs[1:1 + n_a]
        o32_ref, o16_ref, acc_ref, acc16_ref, sems = refs[-5:]
        j, i = pl.program_id(0), pl.program_id(1)

        def copies(k):
            rows = pl.ds(jobs[k][1], nb)
            return (pltpu.make_async_copy(acc_ref.at[k], o32_ref.at[rows], sems.at[0, k]),
                    pltpu.make_async_copy(acc16_ref.at[k], o16_ref.at[rows], sems.at[1, k]))

        for k, (a, _) in enumerate(jobs):
            @pl.when(j == k)
            def _(k=k, a=a):
                @pl.when(i == 0)
                def _():
                    acc_ref[k] = jnp.zeros((nb, D), F32)

                acc_ref[k] += _dot_tn(p_refs[a][...], h_ref[...])

                @pl.when(i == n_t - 1)
                def _():
                    acc16_ref[k] = acc_ref[k].astype(BF16)
                    for cp in copies(k):
                        cp.start()

        @pl.when((j == n_j - 1) & (i == n_t - 1))
        def _():
            for k in range(n_j):
                for cp in copies(k):
                    cp.wait()

    def piece_spec(a):
        s, e = first[a], first[a] + pieces[a][0].shape[1] // nb
        return pl.BlockSpec((tm, nb), lambda j, i: (jnp.where(j < s, 0, jnp.where(j >= e, n_t - 1, i)),
                                                    jnp.clip(j - s, 0, e - s - 1)))

    hbm = pl.BlockSpec(memory_space=pl.ANY)
    return pl.pallas_call(
        body, name=name, grid=(n_j, n_t),
        in_specs=[pl.BlockSpec((tm, D), lambda j, i: (i, 0))] + [piece_spec(a) for a in range(n_a)] + [hbm, hbm],
        out_specs=[hbm, hbm],
        out_shape=[jax.ShapeDtypeStruct((D_IN, D), F32), jax.ShapeDtypeStruct((D_IN, D), BF16)],
        scratch_shapes=[pltpu.VMEM((n_j, nb, D), F32), pltpu.VMEM((n_j, nb, D), BF16),
                        pltpu.SemaphoreType.DMA((2, n_j))],
        input_output_aliases={1 + n_a: 0, 2 + n_a: 1},
        compiler_params=_params("arbitrary", "arbitrary", vmem_limit_bytes=48 << 20),
    )(h, *[arr for arr, _ in pieces], *prev)


def _place():
    x, y, c = lax.axis_index("x"), lax.axis_index("y"), lax.axis_index("c")
    return x, y, c, 4 * x + 2 * y + c


def _peer(x, y, c, k):
    return (1 - x if k & 4 else x, 1 - y if k & 2 else y, 1 - c if k & 1 else c)


ICI_MASKS = (4, 2, 6)


def _all_gather(shards):
    n = len(shards)

    def body(*refs):
        src, dst = refs[:n], refs[n:2 * n]
        send_sems, recv_sems, local_sems = refs[2 * n:]
        x, y, c, me = _place()
        sibling = _peer(x, y, c, 1)

        def copy(a, s, block, to, own=False):
            return pltpu.make_async_remote_copy(
                src_ref=src[a] if own else dst[a].at[block], dst_ref=dst[a].at[block],
                send_sem=send_sems.at[a * 7 + s], recv_sem=recv_sems.at[a * 7 + s], device_id=to, device_id_type=MESH_ID)

        local = [pltpu.make_async_copy(src[a], dst[a].at[me], local_sems.at[a]) for a in range(n)]
        for cp in local:
            cp.start()
        started = [copy(a, 0, me, sibling, own=True) for a in range(n)]
        started += [copy(a, 1 + j, me, _peer(x, y, c, k), own=True) for j, k in enumerate(ICI_MASKS) for a in range(n)]
        for cp in started:
            cp.start()
        for j, k in enumerate(ICI_MASKS):
            for a in range(n):
                copy(a, 1 + j, me ^ k, sibling).wait_recv()
                fwd = copy(a, 4 + j, me ^ k, sibling)
                fwd.start()
                started.append(fwd)
        for a in range(n):
            copy(a, 0, me ^ 1, sibling).wait_recv()
        for j, k in enumerate(ICI_MASKS):
            for a in range(n):
                copy(a, 4 + j, me ^ 1 ^ k, sibling).wait_recv()
        for cp in started:
            cp.wait_send()
        for cp in local:
            cp.wait()

    hbm = pl.BlockSpec(memory_space=pl.ANY)
    return pl.pallas_call(
        body, name="all_gather_weights",
        in_specs=[hbm] * n, out_specs=[hbm] * n,
        out_shape=[jax.ShapeDtypeStruct((N_DEV,) + s.shape, s.dtype) for s in shards],
        scratch_shapes=[pltpu.SemaphoreType.DMA((7 * n,)), pltpu.SemaphoreType.DMA((7 * n,)),
                        pltpu.SemaphoreType.DMA((n,))],
    )(*shards)


def _direct_copies(src, land, send_sems, recv_sems):
    x, y, c, me = _place()
    return [pltpu.make_async_remote_copy(
        src_ref=src[a], dst_ref=land[a].at[me], send_sem=send_sems.at[a * 7 + k - 1],
        recv_sem=recv_sems.at[a * 7 + k - 1], device_id=_peer(x, y, c, k), device_id_type=MESH_ID)
        for k in range(1, N_DEV) for a in range(len(src))]


def _gather_start(shards, name, carry):
    n = len(shards)

    def body(*refs):
        src, land = refs[:n], refs[n:2 * n]
        send_sems, recv_sems = refs[2 * n + 1], refs[2 * n + 2]
        token_ref = refs[-1]
        for cp in _direct_copies(src, land, send_sems, recv_sems):
            cp.start()
        token_ref[...] = refs[2 * n][...]

    hbm = pl.BlockSpec(memory_space=pltpu.HBM)
    sem = pl.BlockSpec(memory_space=pltpu.SEMAPHORE)
    lands = [lax.empty((N_DEV,) + s.shape, s.dtype) for s in shards]
    out = pl.pallas_call(
        body, name=name + "_start",
        out_shape=(pltpu.SemaphoreType.DMA((7 * n,)), pltpu.SemaphoreType.DMA((7 * n,)),
                   *[pltpu.HBM(s.shape, s.dtype) for s in shards], *[pltpu.HBM(s.shape, s.dtype) for s in lands],
                   jax.ShapeDtypeStruct((1, D), F32)),
        in_specs=[hbm] * (2 * n) + [_whole_vmem()], out_specs=(sem, sem, *[hbm] * (2 * n), _whole_vmem()),
        input_output_aliases={a: 2 + a for a in range(2 * n)},
        compiler_params=pltpu.CompilerParams(has_side_effects=pltpu.SideEffectType.DATAFLOW_SIDE_EFFECTING),
    )(*[pltpu.with_memory_space_constraint(s, pltpu.HBM) for s in list(shards) + lands], carry)
    return out[0], out[1], out[2:2 + n], out[2 + n:2 + 2 * n], out[-1]


def _gather_wait(send_sems, recv_sems, flying, lands, after, name):
    n = len(flying)

    def body(*refs):
        src, land = refs[:n], refs[n:2 * n]
        for cp in _direct_copies(src, land, refs[2 * n], refs[2 * n + 1]):
            cp.wait_send()
            cp.wait_recv()

    hbm = pl.BlockSpec(memory_space=pltpu.HBM)
    sem = pl.BlockSpec(memory_space=pltpu.SEMAPHORE)
    out = pl.pallas_call(
        body, name=name + "_wait",
        out_shape=tuple(pltpu.HBM(s.shape, s.dtype) for s in list(flying) + list(lands)),
        in_specs=[hbm] * (2 * n) + [sem, sem, pl.BlockSpec(memory_space=pl.ANY)], out_specs=tuple([hbm] * (2 * n)),
        input_output_aliases={a: a for a in range(2 * n)},
        compiler_params=pltpu.CompilerParams(has_side_effects=pltpu.SideEffectType.DATAFLOW_SIDE_EFFECTING),
    )(*flying, *lands, send_sems, recv_sems, after)
    return out[n:]


def _sibling_copies(src, land, send_sems, recv_sems, blocks):
    x, y, c, _ = _place()
    sibling = _peer(x, y, c, 1)
    return [pltpu.make_async_remote_copy(
        src_ref=src[a].at[2 * p + (1 - c)], dst_ref=land[a].at[p], send_sem=send_sems.at[a * 4 + p],
        recv_sem=recv_sems.at[a * 4 + p], device_id=sibling, device_id_type=MESH_ID)
        for a in range(len(src)) for p in blocks[a]]


def _exchange_sibling_start(by_dest, blocks):
    n = len(by_dest)

    def body(*refs):
        for cp in _sibling_copies(refs[:n], refs[n:2 * n], refs[2 * n], refs[2 * n + 1], blocks):
            cp.start()

    hbm = pl.BlockSpec(memory_space=pltpu.HBM)
    sem = pl.BlockSpec(memory_space=pltpu.SEMAPHORE)
    lands = [lax.empty((4,) + s.shape[1:], s.dtype) for s in by_dest]
    out = pl.pallas_call(
        body, name="exchange_sibling_start",
        out_shape=(pltpu.SemaphoreType.DMA((4 * n,)), pltpu.SemaphoreType.DMA((4 * n,)),
                   *[pltpu.HBM(s.shape, s.dtype) for s in by_dest], *[pltpu.HBM(s.shape, s.dtype) for s in lands]),
        in_specs=[hbm] * (2 * n), out_specs=(sem, sem, *[hbm] * (2 * n)),
        input_output_aliases={a: 2 + a for a in range(2 * n)},
        compiler_params=pltpu.CompilerParams(has_side_effects=pltpu.SideEffectType.DATAFLOW_SIDE_EFFECTING),
    )(*[pltpu.with_memory_space_constraint(s, pltpu.HBM) for s in list(by_dest) + lands])
    return out[0], out[1], out[2:2 + n], out[2 + n:]


def _exchange_sibling_last(by_dest, lands, blocks):
    n = len(by_dest)

    def body(*refs):
        copies = _sibling_copies(refs[:n], refs[n:2 * n], refs[-2], refs[-1], blocks)
        for cp in copies:
            cp.start()
        for cp in copies:
            cp.wait_recv()
        for cp in copies:
            cp.wait_send()

    hbm = pl.BlockSpec(memory_space=pl.ANY)
    return pl.pallas_call(
        body, name="exchange_sibling_last", in_specs=[hbm] * (2 * n), out_specs=[hbm] * n,
        out_shape=[jax.ShapeDtypeStruct(s.shape, s.dtype) for s in lands],
        scratch_shapes=[pltpu.SemaphoreType.DMA((4 * n,)), pltpu.SemaphoreType.DMA((4 * n,))],
        input_output_aliases={n + a: a for a in range(n)},
    )(*by_dest, *lands)


def _exchange_sibling_wait(send_sems, recv_sems, flying, lands, blocks):
    n = len(flying)

    def body(*refs):
        for cp in _sibling_copies(refs[:n], refs[n:2 * n], refs[2 * n], refs[2 * n + 1], blocks):
            cp.wait_recv()
            cp.wait_send()

    hbm = pl.BlockSpec(memory_space=pltpu.HBM)
    sem = pl.BlockSpec(memory_space=pltpu.SEMAPHORE)
    out = pl.pallas_call(
        body, name="exchange_sibling_wait",
        out_shape=tuple(pltpu.HBM(s.shape, s.dtype) for s in list(flying) + list(lands)),
        in_specs=[hbm] * (2 * n) + [sem, sem], out_specs=tuple([hbm] * (2 * n)),
        input_output_aliases={a: a for a in range(2 * n)},
        compiler_params=pltpu.CompilerParams(has_side_effects=pltpu.SideEffectType.DATAFLOW_SIDE_EFFECTING),
    )(*[pltpu.with_memory_space_constraint(s, pltpu.HBM) for s in list(flying) + list(lands)], send_sems, recv_sems)
    return out[:n], out[n:]


def _chip_copies(src, land, send_sems, recv_sems):
    x, y, c, _ = _place()
    chip = 2 * x + y
    return [pltpu.make_async_remote_copy(
        src_ref=src[a].at[chip ^ (k >> 1)], dst_ref=land[a].at[j], send_sem=send_sems.at[a * 3 + j],
        recv_sem=recv_sems.at[a * 3 + j], device_id=_peer(x, y, c, k), device_id_type=MESH_ID)
        for j, k in enumerate(ICI_MASKS) for a in range(len(src))]


def _exchange_chips_start(by_chip, carry):
    n = len(by_chip)

    def body(*refs):
        src, land = refs[:n], refs[n:2 * n]
        send_sems, recv_sems = refs[2 * n + 1], refs[2 * n + 2]
        token_ref = refs[-1]
        for cp in _chip_copies(src, land, send_sems, recv_sems):
            cp.start()
        token_ref[...] = refs[2 * n][...]

    hbm = pl.BlockSpec(memory_space=pltpu.HBM)
    sem = pl.BlockSpec(memory_space=pltpu.SEMAPHORE)
    lands = [lax.empty((3,) + s.shape[1:], s.dtype) for s in by_chip]
    out = pl.pallas_call(
        body, name="exchange_chips_start",
        out_shape=(pltpu.SemaphoreType.DMA((3 * n,)), pltpu.SemaphoreType.DMA((3 * n,)),
                   *[pltpu.HBM(s.shape, s.dtype) for s in by_chip], *[pltpu.HBM(s.shape, s.dtype) for s in lands],
                   jax.ShapeDtypeStruct((1, D), F32)),
        in_specs=[hbm] * (2 * n) + [_whole_vmem()], out_specs=(sem, sem, *[hbm] * (2 * n), _whole_vmem()),
        input_output_aliases={a: 2 + a for a in range(2 * n)},
        compiler_params=pltpu.CompilerParams(has_side_effects=pltpu.SideEffectType.DATAFLOW_SIDE_EFFECTING),
    )(*[pltpu.with_memory_space_constraint(s, pltpu.HBM) for s in list(by_chip) + lands], carry)
    return out[0], out[1], out[2:2 + n], out[2 + n:2 + 2 * n], out[-1]


def _exchange_chips_wait(send_sems, recv_sems, flying, lands, after):
    n = len(flying)

    def body(*refs):
        src, land = refs[:n], refs[n:2 * n]
        send_sems_ref, recv_sems_ref = refs[2 * n], refs[2 * n + 1]
        for cp in _chip_copies(src, land, send_sems_ref, recv_sems_ref):
            cp.wait_send()
            cp.wait_recv()

    hbm = pl.BlockSpec(memory_space=pltpu.HBM)
    sem = pl.BlockSpec(memory_space=pltpu.SEMAPHORE)
    out = pl.pallas_call(
        body, name="exchange_chips_wait",
        out_shape=tuple(pltpu.HBM(s.shape, s.dtype) for s in list(flying) + list(lands)),
        in_specs=[hbm] * (2 * n) + [sem, sem, pl.BlockSpec(memory_space=pl.ANY)], out_specs=tuple([hbm] * (2 * n)),
        input_output_aliases={a: a for a in range(2 * n)},
        compiler_params=pltpu.CompilerParams(has_side_effects=pltpu.SideEffectType.DATAFLOW_SIDE_EFFECTING),
    )(*flying, *lands, send_sems, recv_sems, after)
    return out[n:]


def _adamw_math(w, g, m, v):
    m = ADAM_B1 * m + (1.0 - ADAM_B1) * g
    v = ADAM_B2 * v + (1.0 - ADAM_B2) * (g * g)
    m_hat = m / (1.0 - ADAM_B1 ** ADAM_STEP)
    v_hat = v / (1.0 - ADAM_B2 ** ADAM_STEP)
    return -ADAM_LR * (m_hat / (jnp.sqrt(v_hat) + ADAM_EPS) + ADAM_WD * w), m, v


def _pair_sum(owns, recvs, place_arr, tr, name):
    n = len(owns)
    _, rows, cols = owns[0].shape

    def body(place_ref, *refs):
        for a in range(n):
            s = refs[a][...] + refs[n + a][...].astype(F32)
            refs[3 * n + a][...] = s.astype(BF16)

            @pl.when(pl.program_id(1) == place_ref[1])
            def _(a=a, s=s):
                refs[2 * n + a][...] = s

    by_chip = pl.BlockSpec((None, tr, cols), lambda i, p, place_ref: (p, i, 0))
    mine = pl.BlockSpec((None, tr, cols), lambda i, p, place_ref: (2 * p + place_ref[0], i, 0))
    kept = pl.BlockSpec((tr, cols), lambda i, p, place_ref: (i, 0))
    out = pl.pallas_call(
        body, name=name,
        grid_spec=pltpu.PrefetchScalarGridSpec(
            num_scalar_prefetch=1, grid=(rows // tr, 4), in_specs=[mine] * n + [by_chip] * n,
            out_specs=[kept] * n + [by_chip] * n),
        out_shape=[jax.ShapeDtypeStruct((rows, cols), F32)] * n + [jax.ShapeDtypeStruct((4, rows, cols), BF16)] * n,
        compiler_params=_params("parallel", "arbitrary"),
    )(place_arr, *owns, *recvs)
    return out[:n], out[n:]


def _chip_sum(pairs, recvs, tr, name, adam=None):
    n = len(pairs)
    rows, cols = pairs[0].shape
    n_state = 0 if adam is None else 3 * n

    def body(*refs):
        outs = refs[2 * n + n_state:]
        for a in range(n):
            g = refs[a][...]
            for j in range(3):
                g = g + refs[n + a][j].astype(F32)
            outs[a][...] = g
            if adam is not None:
                w_ref, m_ref, v_ref = (refs[2 * n + s * n + a] for s in range(3))
                outs[n + a][...], outs[2 * n + a][...], outs[3 * n + a][...] = _adamw_math(w_ref[...], g, m_ref[...], v_ref[...])

    blk = pl.BlockSpec((tr, cols), lambda i: (i, 0))
    n_out = n if adam is None else 4 * n
    out = pl.pallas_call(
        body, name=name, grid=(rows // tr,),
        in_specs=[blk] * n + [pl.BlockSpec((3, tr, cols), lambda i: (0, i, 0))] * n + [blk] * n_state,
        out_specs=[blk] * n_out,
        out_shape=[jax.ShapeDtypeStruct((rows, cols), F32)] * n_out,
        compiler_params=_params("parallel"),
    )(*pairs, *recvs, *([] if adam is None else [t for group in adam for t in group]))
    return out if adam is None else (out[:n], out[n:2 * n], out[2 * n:3 * n], out[3 * n:])


def _finish_small(small_all, me, ws, ms, vs):
    n = len(ws)

    def body(me_ref, s_ref, c_ref, *refs):
        g, gc = s_ref[0], c_ref[0]
        for d in range(1, N_DEV):
            g = g + s_ref[d]
            gc = gc + c_ref[d]
        refs[3 * n][...] = g[32:33, 0:1]
        grads = [g[0:1], g[8:9], g[16:17, 0:N_HEADS], gc[0:3]]
        for a in range(n):
            w_ref, m_ref, v_ref = (refs[s * n + a] for s in range(3))
            outs = [refs[(3 + s) * n + 1 + a] for s in range(4)]
            if len(w_ref.shape) == 3:
                for r in range(w_ref.shape[0]):
                    g_r = grads[a][r:r + 1]
                    outs[0][r] = g_r
                    outs[1][r], outs[2][r], outs[3][r] = _adamw_math(w_ref[r], g_r, m_ref[r], v_ref[r])
            else:
                outs[0][...] = grads[a]
                outs[1][...], outs[2][...], outs[3][...] = _adamw_math(w_ref[...], grads[a], m_ref[...], v_ref[...])

    def whole(shape):
        return pl.BlockSpec(shape, lambda i, me_ref: (0,) * len(shape))

    params = [whole(w.shape) for w in ws]
    out = pl.pallas_call(
        body, name="finish_small",
        grid_spec=pltpu.PrefetchScalarGridSpec(
            num_scalar_prefetch=1, grid=(1,),
            in_specs=[whole(small_all.shape), pl.BlockSpec((N_DEV, 8, SHARD_SQ), lambda i, me_ref: (0, 3, me_ref[0]))]
            + params * 3,
            out_specs=[whole((1, 1))] + params * 4),
        out_shape=[jax.ShapeDtypeStruct((1, 1), F32)] + [jax.ShapeDtypeStruct(w.shape, F32) for w in ws] * 4,
    )(me, small_all, small_all, *ws, *ms, *vs)
    return out[0], out[1:1 + n], out[1 + n:1 + 2 * n], out[1 + 2 * n:1 + 3 * n], out[1 + 3 * n:]


def _rope_tables():
    inv_freq = np.float32(ROPE_THETA) ** (-np.arange(0, HEAD_DIM, 2, dtype=np.float32) / np.float32(HEAD_DIM))
    ang = (np.arange(SEQ_LEN, dtype=np.float32)[:, None] * inv_freq.astype(np.float32)[None, :]).astype(np.float64)
    cos, sin = np.cos(ang).astype(np.float32), np.sin(ang).astype(np.float32)
    return jnp.asarray(np.tile(cos, (1, 4))), jnp.asarray(np.tile(np.concatenate([-sin, sin], axis=1), (1, 2)))


def _local_step(x, target, g_pre, g_post, sinks, wt, wconv, squares, start_exchange=None):
    cos_t, sin_t = _rope_tables()
    wconv8 = jnp.pad(wconv, ((0, 5), (0, 0)))
    h, q, kv, g3 = _fwd_in_attn(x, g_pre, wt, cos_t, sin_t, 512)
    wpc, wpa, wout = squares(kv)
    a4, ya = _fwd_in_conv(h, wt, wconv8, wpc, 512)
    attn, ub, lse = _fwd_attn(sinks, q, kv, g3, 4)
    loss8, dout, dya, dub, dgab, dwout, dwpa, dgpost8 = _fwd_out_bwd_head(ya, ub, g3, x, target, g_post, wpa, wout, 512)
    dq, dza, dkv_own, dkv_prev, dsink8, dh_part = _bwd_attn(sinks, q, kv, attn, lse, dub, g3, cos_t, sin_t, wt, 4)
    da4, dwpc, dwconv8, dwt = _bwd_conv(dya, a4, h, wconv8, wpc, 512, 2)
    dkv, dwt = _bwd_kv_finish(dkv_own, dkv_prev, cos_t, sin_t, h, dwt)
    dwt = _bwd_dw_in(h, [(dq, ROW_Q), (dza, ROW_ZA)], 1024, 1024, "bwd_dw_in_q_za", dwt)
    sent = None
    if start_exchange is not None:
        dwt, sent = start_exchange[0](dwt, dwpc, dwpa, dwout)
    dwt32, dwt16 = _bwd_dw_in(h, [(dgab, ROW_GA)], 1024, 1024, "bwd_dw_in_gates", dwt)
    token, pending = (None, None) if start_exchange is None else start_exchange[1](dwt32, dwt16, dwpc, dwpa, dwout, sent)
    g_pre_after = g_pre if token is None else token
    grad_x, dgpre8 = _bwd_dh(da4, dh_part, dkv, dgab, wt, x, g_pre_after, dout, 512)
    small = jnp.concatenate([dgpre8, dgpost8, jnp.pad(dsink8, ((0, 0), (0, D - 128))), dwconv8,
                             jnp.pad(loss8, ((0, 0), (0, D - 128)))], axis=0)
    return loss8[0, 0], grad_x, dwt32, dwt16, dwpc, dwpa, dwout, small, pending


def kernel(x, g_pre, g_post, w_in, w_conv, sinks, w_proj_conv, w_proj_attn, w_out, loss_target, m_g_pre, m_g_post, m_w_in, m_w_conv, m_sinks, m_w_proj_conv, m_w_proj_attn, m_w_out, v_g_pre, v_g_post, v_w_in, v_w_conv, v_sinks, v_w_proj_conv, v_w_proj_attn, v_w_out):
    batch = x.shape[0]
    mx, my, mc, me = _place()
    place_arr = jnp.stack([mc, 2 * mx + my]).astype(jnp.int32)

    g_wt, g_conv = _all_gather([w_in[0].T.astype(BF16), jnp.pad(w_conv[0], ((0, 5), (0, 0)))])
    wt = g_wt.reshape(D_IN, D)
    wconv = g_conv[:, 0:3, :].transpose(1, 0, 2).reshape(3, D)
    sq_mine = [w.astype(BF16) for w in (w_proj_conv[0], w_proj_attn[0], w_out[0])]
    wt, sq_mine = lax.optimization_barrier((wt, sq_mine))
    sq_send, sq_recv, sq_flying, sq_lands, sq_token = _gather_start(sq_mine, "gather_squares", g_pre)

    def squares(after):
        got = _gather_wait(sq_send, sq_recv, sq_flying, sq_lands, after, "gather_squares")
        return [lax.dynamic_update_index_in_dim(full, mine, me, 0).reshape(D, D) for full, mine in zip(got, sq_mine)]

    early = [(0, 1, 2)] + [(0, 1, 2, 3)] * 3
    late = [(3,)]

    def send_early(dwt, dwpc, dwpa, dwout):
        own_sq = [g.reshape(N_DEV, SHARD_SQ, D) for g in (dwpc, dwpa, dwout)]
        send_sems, recv_sems, flying, lands = _exchange_sibling_start([dwt[1].reshape(N_DEV, SHARD_IN, D)] + own_sq, early)
        return (dwt[0], flying[0].reshape(D_IN, D)), (send_sems, recv_sems, flying[1:], lands)

    def send_late(dwt32, dwt16, dwpc, dwpa, dwout, sent):
        own_in = dwt32.reshape(N_DEV, SHARD_IN, D)
        send_sems, recv_sems, sq_flying, lands = sent
        dwt16 = dwt16.reshape(N_DEV, SHARD_IN, D)
        lands = list(_exchange_sibling_last([dwt16], lands[:1], late)) + list(lands[1:])
        sent_arrays, from_sibling = _exchange_sibling_wait(
            send_sems, recv_sems, [dwt16] + list(sq_flying), lands, early)
        own_sq = sent_arrays[1:]
        in32, in16 = _pair_sum([own_in], from_sibling[:1], place_arr, SHARD_IN // 2, "pair_sum_w_in")
        sq32, sq16 = _pair_sum(own_sq, from_sibling[1:], place_arr, SHARD_SQ, "pair_sum_squares")
        send_sems, recv_sems, flying, lands, token = _exchange_chips_start(list(in16) + list(sq16), g_pre)
        return token, (send_sems, recv_sems, flying, lands, in32, sq32)

    _, grad_x, _, _, _, _, _, small, pending = _local_step(
        x.reshape(batch * SEQ_LEN, D), loss_target.reshape(batch * SEQ_LEN, D), sq_token, g_post,
        sinks, wt, wconv, squares, (send_early, send_late))
    sm_send, sm_recv, sm_flying, sm_lands, sm_token = _gather_start([small], "gather_small", jnp.zeros((1, D), F32))
    send_sems, recv_sems, flying, lands, in32, sq32 = pending
    from_chips = _exchange_chips_wait(send_sems, recv_sems, flying, lands, sm_token)

    o_in = [o[0].T for o in _chip_sum(
        in32, from_chips[:1], SHARD_IN // 3, "chip_sum_adamw_w_in",
        adam=([w_in[0].T], [m_w_in[0].T], [v_w_in[0].T]))]
    g_in_mine, o_in = o_in[0], o_in[1:]
    g_sq, d_sq, m_sq, v_sq = _chip_sum(
        sq32, from_chips[1:], SHARD_SQ, "chip_sum_adamw_squares",
        adam=([w_proj_conv[0], w_proj_attn[0], w_out[0]], [m_w_proj_conv[0], m_w_proj_attn[0], m_w_out[0]],
              [v_w_proj_conv[0], v_w_proj_attn[0], v_w_out[0]]))
    both_done, g_in_mine = lax.optimization_barrier((d_sq[0], g_in_mine))
    (small_all,) = _gather_wait(sm_send, sm_recv, sm_flying, sm_lands, both_done, "gather_small")
    def by_row(a):
        return a.transpose(1, 0, 2)

    loss, (g_g_pre, g_g_post, g_sinks, g_conv_mine), *o_small = _finish_small(
        lax.dynamic_update_index_in_dim(small_all, small, me, 0), jnp.reshape(me, (1,)).astype(jnp.int32),
        [g_pre, g_post, sinks, by_row(w_conv)], [m_g_pre, m_g_post, m_sinks, by_row(m_w_conv)],
        [v_g_pre, v_g_post, v_sinks, by_row(v_w_conv)])
    loss = loss.reshape(())

    grads = [g_g_pre, g_g_post, g_in_mine[None], by_row(g_conv_mine), g_sinks] + [g[None] for g in g_sq]
    rest = []
    for idx, sq in enumerate((d_sq, m_sq, v_sq)):
        gp, gq, sk, cv = o_small[idx]
        rest += [gp, gq, o_in[idx][None], by_row(cv), sk] + [s[None] for s in sq]
    return (loss, grad_x.reshape(batch, SEQ_LEN, D), *grads, *rest)
```

```python
import numpy as np
import jax
import jax.numpy as jnp
from jax import lax
from jax.experimental import pallas as pl
from jax.experimental.pallas import tpu as pltpu

D = 1024
N_HEADS = 16
HEAD_DIM = 64
LOGIT_SCALE = HEAD_DIM ** -0.5
BLK = 128
SEQ_LEN = 2048
D_IN = 8448
ROW_Q, ROW_KV, ROW_ZA, ROW_GA = 4 * D, 5 * D, 5 * D + 256, 6 * D + 256
SHARD_IN = D_IN // 8
SHARD_SQ = D // 8
N_DEV = 8
V7X_VMEM_BYTES = 64 << 20
ROPE_THETA = 10000.0
RMS_EPS = 1e-6
NEG = -1e30
ADAM_LR, ADAM_B1, ADAM_B2, ADAM_EPS, ADAM_WD, ADAM_STEP = 0.001, 0.9, 0.999, 1e-08, 0.01, 10

F32 = jnp.float32
BF16 = jnp.bfloat16
MESH_ID = pl.DeviceIdType.MESH


def _dot(a, b):
    return jnp.dot(a, b, preferred_element_type=F32)


def _dot_nt(a, b):
    return lax.dot_general(a, b, (((1,), (1,)), ((), ())), preferred_element_type=F32)


def _dot_tn(a, b):
    return lax.dot_general(a, b, (((0,), (0,)), ((), ())), preferred_element_type=F32)


def _sig(z):
    return 1.0 / (1.0 + jnp.exp(-z))


def _swap_halves(z):
    lane = lax.broadcasted_iota(jnp.int32, z.shape, 1)
    return jnp.where((lane & 63) < 32, pltpu.roll(z, 96, 1), pltpu.roll(z, 32, 1))


def _row_spec(tm, width, col=0):
    return pl.BlockSpec((tm, width), lambda i: (i, col))


def _whole_vmem():
    return pl.BlockSpec(memory_space=pltpu.VMEM)


def _params(*sem, vmem_limit_bytes=None):
    return pltpu.CompilerParams(dimension_semantics=sem, vmem_limit_bytes=vmem_limit_bytes)


def _fwd_in_attn(x, g_pre, wt, cos_t, sin_t, tm):
    t = x.shape[0]
    seq_tiles = SEQ_LEN // tm

    def body(x_ref, g_ref, w_ref, c_ref, s_ref, h_ref, q_ref, kv_ref, g3_ref):
        xf = x_ref[...]
        r = lax.rsqrt(jnp.mean(xf * xf, axis=-1, keepdims=True) + RMS_EPS)
        hh = ((xf * r) * g_ref[...]).astype(BF16)
        h_ref[...] = hh
        c = c_ref[...]
        s = s_ref[...]

        def rope(z):
            return z * c + _swap_halves(z) * s

        q = _dot_nt(hh, w_ref[ROW_Q:ROW_Q + D, :])
        for j in range(D // 128):
            q_ref[:, j * 128:(j + 1) * 128] = (rope(q[:, j * 128:(j + 1) * 128]) * LOGIT_SCALE).astype(BF16)
        kv = _dot_nt(hh, w_ref[ROW_KV:ROW_KV + 256, :])
        kv_ref[:, 0:128] = rope(kv[:, 0:128]).astype(BF16)
        kv_ref[:, 128:256] = kv[:, 128:256].astype(BF16)
        for j in range(3):
            g3_ref[:, j * D:(j + 1) * D] = _dot_nt(hh, w_ref[ROW_ZA + j * D:ROW_ZA + (j + 1) * D, :])

    tab = pl.BlockSpec((tm, 128), lambda i: (i % seq_tiles, 0))
    return pl.pallas_call(
        body, name="fwd_in_attn", grid=(t // tm,),
        in_specs=[_row_spec(tm, D), pl.BlockSpec((1, D), lambda i: (0, 0)), _whole_vmem(), tab, tab],
        out_specs=[_row_spec(tm, D), _row_spec(tm, D), _row_spec(tm, 256), _row_spec(tm, 3 * D)],
        out_shape=[jax.ShapeDtypeStruct((t, D), BF16), jax.ShapeDtypeStruct((t, D), BF16),
                   jax.ShapeDtypeStruct((t, 256), BF16), jax.ShapeDtypeStruct((t, 3 * D), F32)],
        compiler_params=_params("parallel"),
    )(x, g_pre, wt, cos_t, sin_t)


def _conv_forward(xc, bg, cg, zc, up6, up7, w_ref):
    rows = lax.broadcasted_iota(jnp.int32, xc.shape, 0)
    u = cg * xc
    u_m1 = jnp.where(rows == 0, up7, pltpu.roll(u, 1, 0))
    u_m2 = jnp.where(rows == 0, up6, jnp.where(rows == 1, up7, pltpu.roll(u, 2, 0)))
    yconv = w_ref[0:1, :] * u_m2 + w_ref[1:2, :] * u_m1 + w_ref[2:3, :] * u
    sg = _sig(zc)
    sz = zc * sg
    co = bg * yconv
    return u, u_m1, u_m2, yconv, sg, sz, co


def _fwd_in_conv(h, wt, wconv8, wpc, tm):
    t = h.shape[0]
    seq_tiles = SEQ_LEN // tm

    def body(h_ref, w_ref, wc_ref, wpc_ref, a4_ref, ya_ref, last_u_ref):
        hh = h_ref[...]
        xc, bg, cg, zc = (_dot_nt(hh, w_ref[j * D:(j + 1) * D, :]) for j in range(4))
        for j, z in enumerate((xc, bg, cg, zc)):
            a4_ref[:, j * D:(j + 1) * D] = z.astype(BF16)
        first = pl.program_id(0) % seq_tiles == 0
        up6 = jnp.where(first, 0.0, last_u_ref[6:7, :])
        up7 = jnp.where(first, 0.0, last_u_ref[7:8, :])
        u, _, _, _, _, sz, co = _conv_forward(xc, bg, cg, zc, up6, up7, wc_ref)
        last_u_ref[...] = u[tm - 8:tm, :]
        ya_ref[...] = _dot((sz * co).astype(BF16), wpc_ref[...])

    return pl.pallas_call(
        body, name="fwd_in_conv", grid=(t // tm,),
        in_specs=[_row_spec(tm, D), _whole_vmem(), pl.BlockSpec((8, D), lambda i: (0, 0)), _whole_vmem()],
        out_specs=[_row_spec(tm, 4 * D), _row_spec(tm, D)],
        out_shape=[jax.ShapeDtypeStruct((t, 4 * D), BF16), jax.ShapeDtypeStruct((t, D), F32)],
        scratch_shapes=[pltpu.VMEM((8, D), F32)],
        compiler_params=_params("arbitrary"),
    )(h, wt, wconv8, wpc)


STACK = 4 * BLK


def _band_mask(first):
    qi = lax.broadcasted_iota(jnp.int32, (STACK, 2 * BLK), 0) & (BLK - 1)
    kj = lax.broadcasted_iota(jnp.int32, (STACK, 2 * BLK), 1)
    return (kj > qi) & (kj <= qi + BLK) & (kj >= jnp.where(first, BLK, 0))


def _masked_fill(sink_ref, g, e):
    kj = lax.broadcasted_iota(jnp.int32, (STACK, 2 * BLK), 1)
    sink = jnp.concatenate([jnp.full((BLK, 2 * BLK), sink_ref[0, 2 * (4 * g + jj) + e], F32) for jj in range(4)], axis=0)
    return jnp.where(kj == 0, sink, NEG)


def _padded_pair(before, own):
    z = jnp.concatenate([before, own], axis=0).astype(F32)
    z = jnp.where(lax.broadcasted_iota(jnp.int32, z.shape, 0) == 0, 0.0, z)
    zs = pltpu.roll(z, 64, 1)
    lo = lax.broadcasted_iota(jnp.int32, z.shape, 1) < 64
    zero = jnp.zeros_like(z)
    left = [jnp.where(lo, z, zero).astype(BF16), jnp.where(lo, zs, zero).astype(BF16)]
    right = [jnp.where(lo, zero, zs).astype(BF16), jnp.where(lo, zero, z).astype(BF16)]
    return left, right


def _exp_logits(s, valid, fill):
    s = jnp.where(valid, s, fill)
    m = jnp.max(s, axis=-1, keepdims=True)
    return jnp.exp(s - m), m


def _kv_blocks(kvc_ref, kvp_ref, b, col):
    own = kvc_ref[b * BLK:(b + 1) * BLK, col:col + 128]
    before = kvp_ref[:, col:col + 128] if b == 0 else kvc_ref[(b - 1) * BLK:b * BLK, col:col + 128]
    return before, own


def _fwd_attn(sinks, q, kv, g3, blocks):
    t = q.shape[0]
    tq = blocks * BLK
    seq_blocks = SEQ_LEN // BLK

    def body(sink_ref, q_ref, kvc_ref, kvp_ref, za_ref, attn_ref, ub_ref, lse_ref):
        lo = lax.broadcasted_iota(jnp.int32, (STACK, 128), 1) < 64
        for b in range(blocks):
            rows = slice(b * BLK, (b + 1) * BLK)
            valid = _band_mask((pl.program_id(0) * blocks + b) % seq_blocks == 0)
            k_pad = _padded_pair(*_kv_blocks(kvc_ref, kvp_ref, b, 0))
            v_pad = _padded_pair(*_kv_blocks(kvc_ref, kvp_ref, b, 128))
            for g in range(2):
                qg = jnp.concatenate([q_ref[rows, j * 128:(j + 1) * 128] for j in range(4 * g, 4 * g + 4)], axis=0)
                pv, den = [], []
                for e in range(2):
                    p, m = _exp_logits(_dot_nt(qg, k_pad[e][g]), valid, _masked_fill(sink_ref, g, e))
                    both = _dot(p.astype(BF16), jnp.concatenate([v_pad[e][g], jnp.ones((2 * BLK, 128), BF16)], axis=1))
                    pv.append(both[:, 0:128])
                    den.append(both[:, 128:256])
                    lse_ref[b, 2 * g + e] = m + jnp.log(den[e])
                o = jnp.where(lo, pv[0] / den[0], pv[1] / den[1])
                for jj in range(4):
                    cols = slice((4 * g + jj) * 128, (4 * g + jj + 1) * 128)
                    oj = o[jj * BLK:(jj + 1) * BLK, :]
                    attn_ref[rows, cols] = oj
                    za = za_ref[rows, cols]
                    ub_ref[rows, cols] = (za * _sig(za) * oj).astype(BF16)

    return pl.pallas_call(
        body, name="fwd_attn", grid=(t // tq,),
        in_specs=[pl.BlockSpec(memory_space=pltpu.SMEM), _row_spec(tq, D), _row_spec(tq, 256),
                  pl.BlockSpec((BLK, 256), lambda i: (jnp.maximum(i * blocks - 1, 0), 0)), _row_spec(tq, D, 0)],
        out_specs=[_row_spec(tq, D), _row_spec(tq, D), pl.BlockSpec((blocks, 4, STACK, 128), lambda i: (i, 0, 0, 0))],
        out_shape=[jax.ShapeDtypeStruct((t, D), F32), jax.ShapeDtypeStruct((t, D), BF16),
                   jax.ShapeDtypeStruct((t // BLK, 4, STACK, 128), F32)],
        compiler_params=_params("parallel"),
    )(sinks, q, kv, kv, g3)


def _fwd_out_bwd_head(ya, ub, g3, x, target, g_post, wpa, wout, tm):
    t = x.shape[0]

    def body(ya_ref, ub_ref, ga_ref, gb_ref, x_ref, tgt_ref, gp_ref, wpa_ref, wout_ref,
             loss_ref, dout_ref, dya_ref, dub_ref, dgab_ref, dwout_ref, dwpa_ref, dgp_ref):
        @pl.when(pl.program_id(0) == 0)
        def _():
            loss_ref[...] = jnp.zeros_like(loss_ref)
            dwout_ref[...] = jnp.zeros_like(dwout_ref)
            dwpa_ref[...] = jnp.zeros_like(dwpa_ref)
            dgp_ref[...] = jnp.zeros_like(dgp_ref)

        g = gp_ref[...]
        halves = (slice(0, tm // 2), slice(tm // 2, tm))

        def stage1(rows):
            return _dot(ub_ref[rows, :], wpa_ref[...])

        def stage2(rows, yb):
            sa = _sig(ga_ref[rows, :])
            sb = _sig(gb_ref[rows, :])
            mb = (sa * ya_ref[rows, :] + sb * yb).astype(BF16)
            return sa, sb, mb, _dot(mb, wout_ref[...])

        def stage3(rows, y):
            r = lax.rsqrt(jnp.mean(y * y, axis=-1, keepdims=True) + RMS_EPS)
            n = y * r
            err = (x_ref[rows, :] + n * g) - tgt_ref[rows, :]
            sq = jnp.sum(jnp.sum(err * err, axis=0, keepdims=True), axis=1, keepdims=True)
            dout = err * (1.0 / D)
            dout_ref[rows, :] = dout
            dgp = jnp.sum(dout * n, axis=0, keepdims=True)
            dn = dout * g
            dy = (r * (dn - n * jnp.mean(dn * n, axis=-1, keepdims=True))).astype(BF16)
            return sq, dgp, dy, _dot_nt(dy, wout_ref[...])

        def stage4(rows, dm, sa, sb, yb):
            dya_ref[rows, :] = (dm * sa).astype(BF16)
            dyb = (dm * sb).astype(BF16)
            dgab_ref[rows, 0:D] = (dm * ya_ref[rows, :] * (sa * (1.0 - sa))).astype(BF16)
            dgab_ref[rows, D:2 * D] = (dm * yb * (sb * (1.0 - sb))).astype(BF16)
            dub_ref[rows, :] = _dot_nt(dyb, wpa_ref[...])
            return dyb

        yb = [stage1(rows) for rows in halves]
        s2 = [stage2(rows, yb[k]) for k, rows in enumerate(halves)]
        s3 = [stage3(rows, s2[k][3]) for k, rows in enumerate(halves)]
        dyb = [stage4(rows, s3[k][3], s2[k][0], s2[k][1], yb[k]) for k, rows in enumerate(halves)]
        loss_ref[...] += sum(s[0] for s in s3) * (0.5 / D)
        dgp_ref[0:1, :] += sum(s[1] for s in s3)
        dwout_ref[...] += _dot_tn(jnp.concatenate([s[2] for s in s2], axis=0), jnp.concatenate([s[2] for s in s3], axis=0))
        dwpa_ref[...] += _dot_tn(ub_ref[...], jnp.concatenate(dyb, axis=0))

    return pl.pallas_call(
        body, name="fwd_out_bwd_head", grid=(t // tm,),
        in_specs=[_row_spec(tm, D), _row_spec(tm, D), _row_spec(tm, D, 1), _row_spec(tm, D, 2),
                  _row_spec(tm, D), _row_spec(tm, D), pl.BlockSpec((1, D), lambda i: (0, 0)),
                  _whole_vmem(), _whole_vmem()],
        out_specs=[pl.BlockSpec((8, 128), lambda i: (0, 0)), _row_spec(tm, D), _row_spec(tm, D), _row_spec(tm, D),
                   _row_spec(tm, 2 * D), _whole_vmem(), _whole_vmem(), pl.BlockSpec((8, D), lambda i: (0, 0))],
        out_shape=[jax.ShapeDtypeStruct((8, 128), F32), jax.ShapeDtypeStruct((t, D), F32),
                   jax.ShapeDtypeStruct((t, D), BF16), jax.ShapeDtypeStruct((t, D), F32),
                   jax.ShapeDtypeStruct((t, 2 * D), BF16), jax.ShapeDtypeStruct((D, D), F32),
                   jax.ShapeDtypeStruct((D, D), F32), jax.ShapeDtypeStruct((8, D), F32)],
        compiler_params=_params("arbitrary", vmem_limit_bytes=V7X_VMEM_BYTES - (2 << 20)),
    )(ya, ub, g3, g3, x, target, g_post, wpa, wout)


def _bwd_attn(sinks, q, kv, attn, lse, dub, g3, cos_t, sin_t, wt, blocks):
    t = q.shape[0]
    tq = blocks * BLK
    seq_blocks = SEQ_LEN // BLK

    def body(sink_ref, q_ref, kvc_ref, kvp_ref, attn_ref, lse_ref, dub_ref, za_ref, c_ref, s_ref, w_ref,
             dq_ref, dza_ref, dkv_own_ref, dkv_prev_ref, dsink_ref, dh_ref):
        @pl.when(pl.program_id(0) == 0)
        def _():
            dsink_ref[...] = jnp.zeros_like(dsink_ref)

        lo = lax.broadcasted_iota(jnp.int32, (STACK, 128), 1) < 64
        lane8 = lax.broadcasted_iota(jnp.int32, (8, 128), 1)
        lo2 = lax.broadcasted_iota(jnp.int32, (2 * BLK, 128), 1) < 64
        sink_row = lax.broadcasted_iota(jnp.int32, (2 * BLK, 128), 0) == 0
        dsink = jnp.zeros((8, 128), F32)
        for b in range(blocks):
            rows = slice(b * BLK, (b + 1) * BLK)
            valid = _band_mask((pl.program_id(0) * blocks + b) % seq_blocks == 0)
            k_pad = _padded_pair(*_kv_blocks(kvc_ref, kvp_ref, b, 0))
            v_pad = _padded_pair(*_kv_blocks(kvc_ref, kvp_ref, b, 128))
            c = c_ref[rows, :]
            s = s_ref[rows, :]
            dk_acc, dv_acc = [], []
            for g in range(2):
                qg, dog = [], []
                for j in range(4 * g, 4 * g + 4):
                    cols = slice(j * 128, (j + 1) * 128)
                    za = za_ref[rows, cols]
                    sg = _sig(za)
                    dub = dub_ref[rows, cols]
                    dza_ref[rows, cols] = (dub * attn_ref[rows, cols] * (sg * (1.0 + za * (1.0 - sg)))).astype(BF16)
                    dog.append((dub * (za * sg)).astype(BF16))
                    qg.append(q_ref[rows, cols])
                qg = jnp.concatenate(qg, axis=0)
                dog = jnp.concatenate(dog, axis=0)
                dq = jnp.zeros((STACK, 128), F32)
                ds_both, p_both = [], []
                for e in range(2):
                    s_masked = jnp.where(valid, _dot_nt(qg, k_pad[e][g]), _masked_fill(sink_ref, g, e))
                    lse_rows = lse_ref[b, 2 * g + e]
                    p = jnp.exp(s_masked - jnp.concatenate([lse_rows, lse_rows], axis=1))
                    dp = _dot_nt(dog, v_pad[e][g])
                    ds = p * (dp - jnp.sum(p * dp, axis=-1, keepdims=True))
                    for jj in range(4):
                        tot = jnp.sum(ds[jj * BLK:(jj + 1) * BLK, 0:1], axis=0, keepdims=True)
                        dsink = dsink + jnp.where(lane8 == 2 * (4 * g + jj) + e, tot, 0.0)
                    ds = ds.astype(BF16)
                    dq = dq + _dot(ds, k_pad[e][g])
                    ds_both.append(ds)
                    p_both.append(p.astype(BF16))
                zero = jnp.zeros_like(qg)
                q2 = jnp.concatenate([jnp.where(lo, qg, zero), jnp.where(lo, zero, qg)], axis=0)
                do2 = jnp.concatenate([jnp.where(lo, dog, zero), jnp.where(lo, zero, dog)], axis=0)
                dk_acc.append(_dot_tn(q2, jnp.concatenate(ds_both, axis=0)).T)
                dv_acc.append(_dot_tn(do2, jnp.concatenate(p_both, axis=0)).T)
                for jj in range(4):
                    cols = slice((4 * g + jj) * 128, (4 * g + jj + 1) * 128)
                    dqj = dq[jj * BLK:(jj + 1) * BLK, :] * LOGIT_SCALE
                    dq_ref[rows, cols] = (dqj * c - _swap_halves(dqj) * s).astype(BF16)
            for col, acc in ((0, dk_acc), (128, dv_acc)):
                both = jnp.where(lo2, acc[0] + pltpu.roll(acc[0], 64, 1), acc[1] + pltpu.roll(acc[1], 64, 1))
                both = jnp.where(sink_row, 0.0, both)
                dkv_prev_ref[rows, col:col + 128] = both[0:BLK, :]
                dkv_own_ref[rows, col:col + 128] = both[BLK:2 * BLK, :]
        dsink_ref[...] += dsink
        dh_ref[...] = _dot(dq_ref[...], w_ref[ROW_Q:ROW_KV, :]) + _dot(dza_ref[...], w_ref[ROW_ZA:ROW_GA, :])

    tab = pl.BlockSpec((tq, 128), lambda i: (i % (SEQ_LEN // tq), 0))
    return pl.pallas_call(
        body, name="bwd_attn", grid=(t // tq,),
        in_specs=[pl.BlockSpec(memory_space=pltpu.SMEM), _row_spec(tq, D), _row_spec(tq, 256),
                  pl.BlockSpec((BLK, 256), lambda i: (jnp.maximum(i * blocks - 1, 0), 0)),
                  _row_spec(tq, D), pl.BlockSpec((blocks, 4, STACK, 128), lambda i: (i, 0, 0, 0)),
                  _row_spec(tq, D), _row_spec(tq, D, 0), tab, tab, _whole_vmem()],
        out_specs=[_row_spec(tq, D), _row_spec(tq, D), _row_spec(tq, 256), _row_spec(tq, 256),
                   pl.BlockSpec((8, 128), lambda i: (0, 0)), _row_spec(tq, D)],
        out_shape=[jax.ShapeDtypeStruct((t, D), BF16), jax.ShapeDtypeStruct((t, D), BF16),
                   jax.ShapeDtypeStruct((t, 256), F32), jax.ShapeDtypeStruct((t, 256), F32),
                   jax.ShapeDtypeStruct((8, 128), F32), jax.ShapeDtypeStruct((t, D), F32)],
        compiler_params=_params("arbitrary"),
    )(sinks, q, kv, kv, attn, lse, dub, g3, cos_t, sin_t, wt)


def _bwd_kv_finish(dkv_own, dkv_prev, cos_t, sin_t, h, prev):
    t = dkv_own.shape[0]
    tm = SEQ_LEN
    n_t = t // tm
    seq_tiles = SEQ_LEN // tm
    n_blocks = t // BLK

    def body(own_ref, same_ref, nxt_ref, c_ref, s_ref, h_ref, o32_in, o16_in, out_ref, o32_ref, o16_ref,
             acc_ref, acc16_ref, sems):
        step = pl.program_id(0)

        @pl.when(step == 0)
        def _():
            acc_ref[...] = jnp.zeros_like(acc_ref)

        keep = jnp.where(step % seq_tiles == seq_tiles - 1, 0.0, 1.0)
        shifted = jnp.concatenate([same_ref[BLK:tm, :], nxt_ref[...] * keep], axis=0)
        tot = own_ref[...] + shifted
        dk = tot[:, 0:128]
        out_ref[:, 0:128] = (dk * c_ref[...] - _swap_halves(dk) * s_ref[...]).astype(BF16)
        out_ref[:, 128:256] = tot[:, 128:256].astype(BF16)
        acc_ref[...] += _dot_tn(out_ref[...], h_ref[...])

        @pl.when(step == n_t - 1)
        def _():
            acc16_ref[...] = acc_ref[...].astype(BF16)
            rows = pl.ds(ROW_KV, 256)
            c32 = pltpu.make_async_copy(acc_ref, o32_ref.at[rows], sems.at[0])
            c16 = pltpu.make_async_copy(acc16_ref, o16_ref.at[rows], sems.at[1])
            c32.start()
            c16.start()
            c32.wait()
            c16.wait()

    tab = pl.BlockSpec((tm, 128), lambda i: (i % seq_tiles, 0))
    hbm = pl.BlockSpec(memory_space=pl.ANY)
    out = pl.pallas_call(
        body, name="bwd_kv_finish", grid=(n_t,),
        in_specs=[_row_spec(tm, 256), _row_spec(tm, 256),
                  pl.BlockSpec((BLK, 256), lambda i: (jnp.minimum((i + 1) * (tm // BLK), n_blocks - 1), 0)), tab, tab,
                  _row_spec(tm, D), hbm, hbm],
        out_specs=[_row_spec(tm, 256), hbm, hbm],
        out_shape=[jax.ShapeDtypeStruct((t, 256), BF16), jax.ShapeDtypeStruct((D_IN, D), F32),
                   jax.ShapeDtypeStruct((D_IN, D), BF16)],
        scratch_shapes=[pltpu.VMEM((256, D), F32), pltpu.VMEM((256, D), BF16), pltpu.SemaphoreType.DMA((2,))],
        input_output_aliases={6: 1, 7: 2},
        compiler_params=_params("arbitrary"),
    )(dkv_own, dkv_prev, dkv_prev, cos_t, sin_t, h, *prev)
    return out[0], (out[1], out[2])


STAGE_ROWS = 256


def _bwd_conv(dya, a4, h, wconv8, wpc, tm, parts):
    t = a4.shape[0]
    n_t = t // tm
    sub = tm // parts
    seq_tiles = SEQ_LEN // tm

    def body(dya_ref, xc_ref, bg_ref, cg_ref, zc_ref, xcp_ref, cgp_ref, w_ref, wpc_ref, h_ref,
             da4_ref, dwpc_ref, dwc_ref, o32_ref, o16_ref, acc_ref, stage_ref, later_ref, sems):
        step = pl.program_id(0)
        tile = n_t - 1 - step

        @pl.when(step == 0)
        def _():
            dwpc_ref[...] = jnp.zeros_like(dwpc_ref)
            dwc_ref[...] = jnp.zeros_like(dwc_ref)
            acc_ref[...] = jnp.zeros_like(acc_ref)

        keep_prev = jnp.where(tile % seq_tiles == 0, 0.0, 1.0)
        ends_sequence = tile % seq_tiles == seq_tiles - 1

        def part(p, later):
            r0 = p * sub
            here = slice(r0, r0 + sub)
            if p == 0:
                u_prev = cgp_ref[14:16, :].astype(F32) * xcp_ref[14:16, :].astype(F32) * keep_prev
            else:
                u_prev = cg_ref[r0 - 2:r0, :].astype(F32) * xc_ref[r0 - 2:r0, :].astype(F32)
            xc = xc_ref[here, :].astype(F32)
            bg = bg_ref[here, :].astype(F32)
            cg = cg_ref[here, :].astype(F32)
            zc = zc_ref[here, :].astype(F32)
            u, u_m1, u_m2, yconv, sg, sz, co = _conv_forward(xc, bg, cg, zc, u_prev[0:1, :], u_prev[1:2, :], w_ref)
            ua = (sz * co).astype(BF16)
            dua = _dot_nt(dya_ref[here, :], wpc_ref[...])
            da4_ref[here, 3 * D:4 * D] = (dua * co * (sg * (1.0 + zc * (1.0 - sg)))).astype(BF16)
            dco = dua * sz
            da4_ref[here, D:2 * D] = (dco * yconv).astype(BF16)
            dyc = dco * bg
            dwc = jnp.concatenate([jnp.sum(dyc * s, axis=0, keepdims=True) for s in (u_m2, u_m1, u)], axis=0)
            rows = lax.broadcasted_iota(jnp.int32, xc.shape, 0)
            n0 = later[0:1, :]
            n1 = later[1:2, :]
            dyc_p1 = jnp.where(rows == sub - 1, n0, pltpu.roll(dyc, sub - 1, 0))
            dyc_p2 = jnp.where(rows == sub - 2, n0, jnp.where(rows == sub - 1, n1, pltpu.roll(dyc, sub - 2, 0)))
            du = w_ref[2:3, :] * dyc + w_ref[1:2, :] * dyc_p1 + w_ref[0:1, :] * dyc_p2
            da4_ref[here, 0:D] = (du * cg).astype(BF16)
            da4_ref[here, 2 * D:3 * D] = (du * xc).astype(BF16)
            return ua, dwc, dyc[0:8, :]

        later = jnp.where(ends_sequence, 0.0, later_ref[...])
        uas, dwc = [], jnp.zeros((3, D), F32)
        for p in reversed(range(parts)):
            ua, dwc_p, later = part(p, later)
            uas.insert(0, ua)
            dwc = dwc + dwc_p
        later_ref[...] = later
        dwpc_ref[...] += _dot_tn(jnp.concatenate(uas, axis=0), dya_ref[...])
        dwc_ref[0:3, :] += dwc
        for j in range(4):
            acc_ref[j * D:(j + 1) * D, :] += _dot_tn(da4_ref[:, j * D:(j + 1) * D], h_ref[...])

        @pl.when(step == n_t - 1)
        def _():
            c32 = pltpu.make_async_copy(acc_ref, o32_ref.at[pl.ds(0, 4 * D)], sems.at[0])
            c32.start()
            for j in range(4 * D // STAGE_ROWS):
                rows = pl.ds(j * STAGE_ROWS, STAGE_ROWS)
                stage_ref[...] = acc_ref[rows, :].astype(BF16)
                c16 = pltpu.make_async_copy(stage_ref, o16_ref.at[rows], sems.at[1])
                c16.start()
                c16.wait()
            c32.wait()

    def rows_of_tile(width, col=0):
        return pl.BlockSpec((tm, width), lambda s: (n_t - 1 - s, col))

    def prev(col):
        return pl.BlockSpec((16, D), lambda s: (jnp.maximum((n_t - 1 - s) * (tm // 16) - 1, 0), col))

    hbm = pl.BlockSpec(memory_space=pl.ANY)
    out = pl.pallas_call(
        body, name="bwd_conv", grid=(n_t,),
        in_specs=[rows_of_tile(D), rows_of_tile(D, 0), rows_of_tile(D, 1), rows_of_tile(D, 2), rows_of_tile(D, 3),
                  prev(0), prev(2), pl.BlockSpec((8, D), lambda s: (0, 0)), _whole_vmem(), rows_of_tile(D)],
        out_specs=[rows_of_tile(4 * D), _whole_vmem(), pl.BlockSpec((8, D), lambda s: (0, 0)), hbm, hbm],
        out_shape=[jax.ShapeDtypeStruct((t, 4 * D), BF16), jax.ShapeDtypeStruct((D, D), F32),
                   jax.ShapeDtypeStruct((8, D), F32), jax.ShapeDtypeStruct((D_IN, D), F32),
                   jax.ShapeDtypeStruct((D_IN, D), BF16)],
        scratch_shapes=[pltpu.VMEM((4 * D, D), F32), pltpu.VMEM((STAGE_ROWS, D), BF16), pltpu.VMEM((8, D), F32),
                        pltpu.SemaphoreType.DMA((2,))],
        compiler_params=pltpu.CompilerParams(dimension_semantics=("arbitrary",), vmem_limit_bytes=V7X_VMEM_BYTES - (2 << 20)),
    )(dya, a4, a4, a4, a4, a4, a4, wconv8, wpc, h)
    return out[0], out[1], out[2], (out[3], out[4])


def _bwd_dh(da4, dh_part, dkv, dgab, wt, x, g_pre, dout, tm):
    t = x.shape[0]

    def body(da4_ref, dhp_ref, dkv_ref, dgab_ref, w_ref, x_ref, g_ref, dout_ref, gx_ref, dg_ref):
        @pl.when(pl.program_id(0) == 0)
        def _():
            dg_ref[...] = jnp.zeros_like(dg_ref)

        dh = dhp_ref[...] + _dot(da4_ref[...], w_ref[0:ROW_Q, :])
        dh += _dot(dkv_ref[...], w_ref[ROW_KV:ROW_ZA, :])
        dh += _dot(dgab_ref[...], w_ref[ROW_GA:D_IN, :])
        xf = x_ref[...]
        r = lax.rsqrt(jnp.mean(xf * xf, axis=-1, keepdims=True) + RMS_EPS)
        xn = xf * r
        dg_ref[0:1, :] += jnp.sum(dh * xn, axis=0, keepdims=True)
        dxn = dh * g_ref[...]
        gx_ref[...] = dout_ref[...] + r * (dxn - xn * jnp.mean(dxn * xn, axis=-1, keepdims=True))

    return pl.pallas_call(
        body, name="bwd_dh", grid=(t // tm,),
        in_specs=[_row_spec(tm, 4 * D), _row_spec(tm, D), _row_spec(tm, 256), _row_spec(tm, 2 * D),
                  _whole_vmem(), _row_spec(tm, D), pl.BlockSpec((1, D), lambda i: (0, 0)), _row_spec(tm, D)],
        out_specs=[_row_spec(tm, D), pl.BlockSpec((8, D), lambda i: (0, 0))],
        out_shape=[jax.ShapeDtypeStruct((t, D), F32), jax.ShapeDtypeStruct((8, D), F32)],
        compiler_params=_params("arbitrary"),
    )(da4, dh_part, dkv, dgab, wt, x, g_pre, dout)


def _bwd_dw_in(h, pieces, nb, tm, name, prev, land=None):
    n_t = h.shape[0] // tm
    n_a = len(pieces)
    jobs = [(a, row0 + b * nb) for a, (arr, row0) in enumerate(pieces) for b in range(arr.shape[1] // nb)]
    first = [min(k for k, (a, _) in enumerate(jobs) if a == b) for b in range(n_a)]
    n_j = len(jobs)

    def body(*refs):
        h_ref, p_refs = refs[0], refs[1:1 + n_a]
        if land is None:
            o32_ref, o16_ref, acc_ref, acc16_ref, sems = refs[-5:]
        else:
            o32_ref, o16_ref, land_ref, acc_ref, acc16_ref, sems, far_sems = refs[-7:]
        j, i = pl.program_id(0), pl.program_id(1)

        def copies(k):
            rows = pl.ds(jobs[k][1], nb)
            return (pltpu.make_async_copy(acc_ref.at[k], o32_ref.at[rows], sems.at[0, k]),
                    pltpu.make_async_copy(acc16_ref.at[k], o16_ref.at[rows], sems.at[1, k]))

        for k, (a, _) in enumerate(jobs):
            @pl.when(j == k)
            def _(k=k, a=a):
                @pl.when(i == 0)
                def _():
                    acc_ref[k] = jnp.zeros((nb, D), F32)

                acc_ref[k] += _dot_tn(p_refs[a][...], h_ref[...])

                @pl.when(i == n_t - 1)
                def _():
                    acc16_ref[k] = acc_ref[k].astype(BF16)
                    for cp in copies(k):
                        cp.start()

        @pl.when((j == n_j - 1) & (i == n_t - 1))
        def _():
            for k in range(n_j):
                for cp in copies(k):
                    cp.wait()
            if land is not None:
                x, y, c, _ = _place()
                far = pltpu.make_async_remote_copy(
                    src_ref=o16_ref.at[pl.ds(pl.multiple_of((7 - c) * SHARD_IN, 16), SHARD_IN)], dst_ref=land_ref.at[3],
                    send_sem=far_sems.at[0], recv_sem=far_sems.at[1], device_id=_peer(x, y, c, 1),
                    device_id_type=MESH_ID)
                far.start()
                far.wait_recv()
                far.wait_send()

    def piece_spec(a):
        s, e = first[a], first[a] + pieces[a][0].shape[1] // nb
        return pl.BlockSpec((tm, nb), lambda j, i: (jnp.where(j < s, 0, jnp.where(j >= e, n_t - 1, i)),
                                                    jnp.clip(j - s, 0, e - s - 1)))

    hbm = pl.BlockSpec(memory_space=pl.ANY)
    carried = list(prev) + ([] if land is None else [land])
    return pl.pallas_call(
        body, name=name, grid=(n_j, n_t),
        in_specs=[pl.BlockSpec((tm, D), lambda j, i: (i, 0))] + [piece_spec(a) for a in range(n_a)]
        + [hbm] * len(carried),
        out_specs=[hbm] * len(carried),
        out_shape=[jax.ShapeDtypeStruct((D_IN, D), F32), jax.ShapeDtypeStruct((D_IN, D), BF16)]
        + ([] if land is None else [jax.ShapeDtypeStruct(land.shape, land.dtype)]),
        scratch_shapes=[pltpu.VMEM((n_j, nb, D), F32), pltpu.VMEM((n_j, nb, D), BF16),
                        pltpu.SemaphoreType.DMA((2, n_j))] + ([] if land is None else [pltpu.SemaphoreType.DMA((2,))]),
        input_output_aliases={1 + n_a + a: a for a in range(len(carried))},
        compiler_params=_params("arbitrary", "arbitrary", vmem_limit_bytes=48 << 20),
    )(h, *[arr for arr, _ in pieces], *carried)


def _place():
    x, y, c = lax.axis_index("x"), lax.axis_index("y"), lax.axis_index("c")
    return x, y, c, 4 * x + 2 * y + c


def _peer(x, y, c, k):
    return (1 - x if k & 4 else x, 1 - y if k & 2 else y, 1 - c if k & 1 else c)


ICI_MASKS = (4, 2, 6)


def _all_gather(shards):
    n = len(shards)

    def body(*refs):
        src, dst = refs[:n], refs[n:2 * n]
        send_sems, recv_sems, local_sems = refs[2 * n:]
        x, y, c, me = _place()
        sibling = _peer(x, y, c, 1)

        def copy(a, s, block, to, own=False):
            return pltpu.make_async_remote_copy(
                src_ref=src[a] if own else dst[a].at[block], dst_ref=dst[a].at[block],
                send_sem=send_sems.at[a * 7 + s], recv_sem=recv_sems.at[a * 7 + s], device_id=to, device_id_type=MESH_ID)

        local = [pltpu.make_async_copy(src[a], dst[a].at[me], local_sems.at[a]) for a in range(n)]
        for cp in local:
            cp.start()
        started = [copy(a, 0, me, sibling, own=True) for a in range(n)]
        started += [copy(a, 1 + j, me, _peer(x, y, c, k), own=True) for j, k in enumerate(ICI_MASKS) for a in range(n)]
        for cp in started:
            cp.start()
        for j, k in enumerate(ICI_MASKS):
            for a in range(n):
                copy(a, 1 + j, me ^ k, sibling).wait_recv()
                fwd = copy(a, 4 + j, me ^ k, sibling)
                fwd.start()
                started.append(fwd)
        for a in range(n):
            copy(a, 0, me ^ 1, sibling).wait_recv()
        for j, k in enumerate(ICI_MASKS):
            for a in range(n):
                copy(a, 4 + j, me ^ 1 ^ k, sibling).wait_recv()
        for cp in started:
            cp.wait_send()
        for cp in local:
            cp.wait()

    hbm = pl.BlockSpec(memory_space=pl.ANY)
    return pl.pallas_call(
        body, name="all_gather_weights",
        in_specs=[hbm] * n, out_specs=[hbm] * n,
        out_shape=[jax.ShapeDtypeStruct((N_DEV,) + s.shape, s.dtype) for s in shards],
        scratch_shapes=[pltpu.SemaphoreType.DMA((7 * n,)), pltpu.SemaphoreType.DMA((7 * n,)),
                        pltpu.SemaphoreType.DMA((n,))],
    )(*shards)


def _direct_copies(src, land, send_sems, recv_sems):
    x, y, c, me = _place()
    return [pltpu.make_async_remote_copy(
        src_ref=src[a], dst_ref=land[a].at[me], send_sem=send_sems.at[a * 7 + k - 1],
        recv_sem=recv_sems.at[a * 7 + k - 1], device_id=_peer(x, y, c, k), device_id_type=MESH_ID)
        for k in range(1, N_DEV) for a in range(len(src))]


def _gather_start(shards, name):
    n = len(shards)

    def body(*refs):
        src, land = refs[:n], refs[n:2 * n]
        send_sems, recv_sems = refs[2 * n], refs[2 * n + 1]
        token_ref = refs[-1]
        for cp in _direct_copies(src, land, send_sems, recv_sems):
            cp.start()
        token_ref[...] = jnp.zeros_like(token_ref)

    hbm = pl.BlockSpec(memory_space=pltpu.HBM)
    sem = pl.BlockSpec(memory_space=pltpu.SEMAPHORE)
    lands = [lax.empty((N_DEV,) + s.shape, s.dtype) for s in shards]
    out = pl.pallas_call(
        body, name=name + "_start",
        out_shape=(pltpu.SemaphoreType.DMA((7 * n,)), pltpu.SemaphoreType.DMA((7 * n,)),
                   *[pltpu.HBM(s.shape, s.dtype) for s in shards], *[pltpu.HBM(s.shape, s.dtype) for s in lands],
                   jax.ShapeDtypeStruct((1, D), F32)),
        in_specs=[hbm] * (2 * n), out_specs=(sem, sem, *[hbm] * (2 * n), _whole_vmem()),
        input_output_aliases={a: 2 + a for a in range(2 * n)},
        compiler_params=pltpu.CompilerParams(has_side_effects=pltpu.SideEffectType.DATAFLOW_SIDE_EFFECTING),
    )(*[pltpu.with_memory_space_constraint(s, pltpu.HBM) for s in list(shards) + lands])
    return out[0], out[1], out[2:2 + n], out[2 + n:2 + 2 * n], out[-1]


def _gather_wait(send_sems, recv_sems, flying, lands, after, name):
    n = len(flying)

    def body(*refs):
        src, land = refs[:n], refs[n:2 * n]
        for cp in _direct_copies(src, land, refs[2 * n], refs[2 * n + 1]):
            cp.wait_send()
            cp.wait_recv()

    hbm = pl.BlockSpec(memory_space=pltpu.HBM)
    sem = pl.BlockSpec(memory_space=pltpu.SEMAPHORE)
    out = pl.pallas_call(
        body, name=name + "_wait",
        out_shape=tuple(pltpu.HBM(s.shape, s.dtype) for s in list(flying) + list(lands)),
        in_specs=[hbm] * (2 * n) + [sem, sem, pl.BlockSpec(memory_space=pl.ANY)], out_specs=tuple([hbm] * (2 * n)),
        input_output_aliases={a: a for a in range(2 * n)},
        compiler_params=pltpu.CompilerParams(has_side_effects=pltpu.SideEffectType.DATAFLOW_SIDE_EFFECTING),
    )(*flying, *lands, send_sems, recv_sems, after)
    return out[n:]


def _sibling_copies(src, land, send_sems, recv_sems, blocks):
    x, y, c, _ = _place()
    sibling = _peer(x, y, c, 1)
    return [pltpu.make_async_remote_copy(
        src_ref=src[a].at[2 * p + (1 - c)], dst_ref=land[a].at[p], send_sem=send_sems.at[a * 4 + p],
        recv_sem=recv_sems.at[a * 4 + p], device_id=sibling, device_id_type=MESH_ID)
        for a in range(len(src)) for p in blocks[a]]


def _exchange_sibling_start(by_dest, blocks):
    n = len(by_dest)

    def body(*refs):
        for cp in _sibling_copies(refs[:n], refs[n:2 * n], refs[2 * n], refs[2 * n + 1], blocks):
            cp.start()

    hbm = pl.BlockSpec(memory_space=pltpu.HBM)
    sem = pl.BlockSpec(memory_space=pltpu.SEMAPHORE)
    lands = [lax.empty((4,) + s.shape[1:], s.dtype) for s in by_dest]
    out = pl.pallas_call(
        body, name="exchange_sibling_start",
        out_shape=(pltpu.SemaphoreType.DMA((4 * n,)), pltpu.SemaphoreType.DMA((4 * n,)),
                   *[pltpu.HBM(s.shape, s.dtype) for s in by_dest], *[pltpu.HBM(s.shape, s.dtype) for s in lands]),
        in_specs=[hbm] * (2 * n), out_specs=(sem, sem, *[hbm] * (2 * n)),
        input_output_aliases={a: 2 + a for a in range(2 * n)},
        compiler_params=pltpu.CompilerParams(has_side_effects=pltpu.SideEffectType.DATAFLOW_SIDE_EFFECTING),
    )(*[pltpu.with_memory_space_constraint(s, pltpu.HBM) for s in list(by_dest) + lands])
    return out[0], out[1], out[2:2 + n], out[2 + n:]


def _exchange_sibling_wait(send_sems, recv_sems, flying, lands, blocks):
    n = len(flying)

    def body(*refs):
        for cp in _sibling_copies(refs[:n], refs[n:2 * n], refs[2 * n], refs[2 * n + 1], blocks):
            cp.wait_recv()
            cp.wait_send()

    hbm = pl.BlockSpec(memory_space=pltpu.HBM)
    sem = pl.BlockSpec(memory_space=pltpu.SEMAPHORE)
    out = pl.pallas_call(
        body, name="exchange_sibling_wait",
        out_shape=tuple(pltpu.HBM(s.shape, s.dtype) for s in list(flying) + list(lands)),
        in_specs=[hbm] * (2 * n) + [sem, sem], out_specs=tuple([hbm] * (2 * n)),
        input_output_aliases={a: a for a in range(2 * n)},
        compiler_params=pltpu.CompilerParams(has_side_effects=pltpu.SideEffectType.DATAFLOW_SIDE_EFFECTING),
    )(*[pltpu.with_memory_space_constraint(s, pltpu.HBM) for s in list(flying) + list(lands)], send_sems, recv_sems)
    return out[:n], out[n:]


def _chip_copies(src, land, send_sems, recv_sems):
    x, y, c, _ = _place()
    chip = 2 * x + y
    return [pltpu.make_async_remote_copy(
        src_ref=src[a].at[chip ^ (k >> 1)], dst_ref=land[a].at[j], send_sem=send_sems.at[a * 3 + j],
        recv_sem=recv_sems.at[a * 3 + j], device_id=_peer(x, y, c, k), device_id_type=MESH_ID)
        for j, k in enumerate(ICI_MASKS) for a in range(len(src))]


def _exchange_chips_start(by_chip):
    n = len(by_chip)

    def body(*refs):
        src, land = refs[:n], refs[n:2 * n]
        send_sems, recv_sems = refs[2 * n], refs[2 * n + 1]
        token_ref = refs[-1]
        for cp in _chip_copies(src, land, send_sems, recv_sems):
            cp.start()
        token_ref[...] = jnp.zeros_like(token_ref)

    hbm = pl.BlockSpec(memory_space=pltpu.HBM)
    sem = pl.BlockSpec(memory_space=pltpu.SEMAPHORE)
    lands = [lax.empty((3,) + s.shape[1:], s.dtype) for s in by_chip]
    out = pl.pallas_call(
        body, name="exchange_chips_start",
        out_shape=(pltpu.SemaphoreType.DMA((3 * n,)), pltpu.SemaphoreType.DMA((3 * n,)),
                   *[pltpu.HBM(s.shape, s.dtype) for s in by_chip], *[pltpu.HBM(s.shape, s.dtype) for s in lands],
                   jax.ShapeDtypeStruct((1, D), F32)),
        in_specs=[hbm] * (2 * n), out_specs=(sem, sem, *[hbm] * (2 * n), _whole_vmem()),
        input_output_aliases={a: 2 + a for a in range(2 * n)},
        compiler_params=pltpu.CompilerParams(has_side_effects=pltpu.SideEffectType.DATAFLOW_SIDE_EFFECTING),
    )(*[pltpu.with_memory_space_constraint(s, pltpu.HBM) for s in list(by_chip) + lands])
    return out[0], out[1], out[2:2 + n], out[2 + n:2 + 2 * n], out[-1]


def _exchange_chips_wait(send_sems, recv_sems, flying, lands, after):
    n = len(flying)

    def body(*refs):
        src, land = refs[:n], refs[n:2 * n]
        send_sems_ref, recv_sems_ref = refs[2 * n], refs[2 * n + 1]
        for cp in _chip_copies(src, land, send_sems_ref, recv_sems_ref):
            cp.wait_send()
            cp.wait_recv()

    hbm = pl.BlockSpec(memory_space=pltpu.HBM)
    sem = pl.BlockSpec(memory_space=pltpu.SEMAPHORE)
    out = pl.pallas_call(
        body, name="exchange_chips_wait",
        out_shape=tuple(pltpu.HBM(s.shape, s.dtype) for s in list(flying) + list(lands)),
        in_specs=[hbm] * (2 * n) + [sem, sem, pl.BlockSpec(memory_space=pl.ANY)], out_specs=tuple([hbm] * (2 * n)),
        input_output_aliases={a: a for a in range(2 * n)},
        compiler_params=pltpu.CompilerParams(has_side_effects=pltpu.SideEffectType.DATAFLOW_SIDE_EFFECTING),
    )(*flying, *lands, send_sems, recv_sems, after)
    return out[n:]


def _adamw_math(w, g, m, v):
    m = ADAM_B1 * m + (1.0 - ADAM_B1) * g
    v = ADAM_B2 * v + (1.0 - ADAM_B2) * (g * g)
    m_hat = m / (1.0 - ADAM_B1 ** ADAM_STEP)
    v_hat = v / (1.0 - ADAM_B2 ** ADAM_STEP)
    return -ADAM_LR * (m_hat / (jnp.sqrt(v_hat) + ADAM_EPS) + ADAM_WD * w), m, v


def _pair_sum(owns, recvs, place_arr, tr, name):
    n = len(owns)
    _, rows, cols = owns[0].shape

    def body(place_ref, *refs):
        for a in range(n):
            s = refs[a][...] + refs[n + a][...].astype(F32)
            refs[3 * n + a][...] = s.astype(BF16)

            @pl.when(pl.program_id(1) == place_ref[1])
            def _(a=a, s=s):
                refs[2 * n + a][...] = s

    by_chip = pl.BlockSpec((None, tr, cols), lambda i, p, place_ref: (p, i, 0))
    mine = pl.BlockSpec((None, tr, cols), lambda i, p, place_ref: (2 * p + place_ref[0], i, 0))
    kept = pl.BlockSpec((tr, cols), lambda i, p, place_ref: (i, 0))
    out = pl.pallas_call(
        body, name=name,
        grid_spec=pltpu.PrefetchScalarGridSpec(
            num_scalar_prefetch=1, grid=(rows // tr, 4), in_specs=[mine] * n + [by_chip] * n,
            out_specs=[kept] * n + [by_chip] * n),
        out_shape=[jax.ShapeDtypeStruct((rows, cols), F32)] * n + [jax.ShapeDtypeStruct((4, rows, cols), BF16)] * n,
        compiler_params=_params("parallel", "arbitrary"),
    )(place_arr, *owns, *recvs)
    return out[:n], out[n:]


def _chip_sum(pairs, recvs, tr, name, adam=None):
    n = len(pairs)
    rows, cols = pairs[0].shape
    n_state = 0 if adam is None else 3 * n

    def body(*refs):
        outs = refs[2 * n + n_state:]
        for a in range(n):
            g = refs[a][...]
            for j in range(3):
                g = g + refs[n + a][j].astype(F32)
            outs[a][...] = g
            if adam is not None:
                w_ref, m_ref, v_ref = (refs[2 * n + s * n + a] for s in range(3))
                outs[n + a][...], outs[2 * n + a][...], outs[3 * n + a][...] = _adamw_math(w_ref[...], g, m_ref[...], v_ref[...])

    blk = pl.BlockSpec((tr, cols), lambda i: (i, 0))
    n_out = n if adam is None else 4 * n
    out = pl.pallas_call(
        body, name=name, grid=(rows // tr,),
        in_specs=[blk] * n + [pl.BlockSpec((3, tr, cols), lambda i: (0, i, 0))] * n + [blk] * n_state,
        out_specs=[blk] * n_out,
        out_shape=[jax.ShapeDtypeStruct((rows, cols), F32)] * n_out,
        compiler_params=_params("parallel"),
    )(*pairs, *recvs, *([] if adam is None else [t for group in adam for t in group]))
    return out if adam is None else (out[:n], out[n:2 * n], out[2 * n:3 * n], out[3 * n:])


def _finish_small(small_all, me, ws, ms, vs):
    n = len(ws)

    def body(me_ref, s_ref, c_ref, *refs):
        g, gc = s_ref[0], c_ref[0]
        for d in range(1, N_DEV):
            g = g + s_ref[d]
            gc = gc + c_ref[d]
        refs[3 * n][...] = g[32:33, 0:1]
        grads = [g[0:1], g[8:9], g[16:17, 0:N_HEADS], gc[0:3]]
        for a in range(n):
            w_ref, m_ref, v_ref = (refs[s * n + a] for s in range(3))
            outs = [refs[(3 + s) * n + 1 + a] for s in range(4)]
            if len(w_ref.shape) == 3:
                for r in range(w_ref.shape[0]):
                    g_r = grads[a][r:r + 1]
                    outs[0][r] = g_r
                    outs[1][r], outs[2][r], outs[3][r] = _adamw_math(w_ref[r], g_r, m_ref[r], v_ref[r])
            else:
                outs[0][...] = grads[a]
                outs[1][...], outs[2][...], outs[3][...] = _adamw_math(w_ref[...], grads[a], m_ref[...], v_ref[...])

    def whole(shape):
        return pl.BlockSpec(shape, lambda i, me_ref: (0,) * len(shape))

    params = [whole(w.shape) for w in ws]
    out = pl.pallas_call(
        body, name="finish_small",
        grid_spec=pltpu.PrefetchScalarGridSpec(
            num_scalar_prefetch=1, grid=(1,),
            in_specs=[whole(small_all.shape), pl.BlockSpec((N_DEV, 8, SHARD_SQ), lambda i, me_ref: (0, 3, me_ref[0]))]
            + params * 3,
            out_specs=[whole((1, 1))] + params * 4),
        out_shape=[jax.ShapeDtypeStruct((1, 1), F32)] + [jax.ShapeDtypeStruct(w.shape, F32) for w in ws] * 4,
    )(me, small_all, small_all, *ws, *ms, *vs)
    return out[0], out[1:1 + n], out[1 + n:1 + 2 * n], out[1 + 2 * n:1 + 3 * n], out[1 + 3 * n:]


def _rope_tables():
    inv_freq = np.float32(ROPE_THETA) ** (-np.arange(0, HEAD_DIM, 2, dtype=np.float32) / np.float32(HEAD_DIM))
    ang = (np.arange(SEQ_LEN, dtype=np.float32)[:, None] * inv_freq.astype(np.float32)[None, :]).astype(np.float64)
    cos, sin = np.cos(ang).astype(np.float32), np.sin(ang).astype(np.float32)
    return jnp.asarray(np.tile(cos, (1, 4))), jnp.asarray(np.tile(np.concatenate([-sin, sin], axis=1), (1, 2)))


def _local_step(x, target, g_pre, g_post, sinks, wt, wconv, squares, start_exchange=None):
    cos_t, sin_t = _rope_tables()
    wconv8 = jnp.pad(wconv, ((0, 5), (0, 0)))
    h, q, kv, g3 = _fwd_in_attn(x, g_pre, wt, cos_t, sin_t, 512)
    wpc, wpa, wout = squares(kv)
    a4, ya = _fwd_in_conv(h, wt, wconv8, wpc, 512)
    attn, ub, lse = _fwd_attn(sinks, q, kv, g3, 4)
    loss8, dout, dya, dub, dgab, dwout, dwpa, dgpost8 = _fwd_out_bwd_head(ya, ub, g3, x, target, g_post, wpa, wout, 512)
    dq, dza, dkv_own, dkv_prev, dsink8, dh_part = _bwd_attn(sinks, q, kv, attn, lse, dub, g3, cos_t, sin_t, wt, 4)
    da4, dwpc, dwconv8, dwt = _bwd_conv(dya, a4, h, wconv8, wpc, 512, 2)
    dkv, dwt = _bwd_kv_finish(dkv_own, dkv_prev, cos_t, sin_t, h, dwt)
    dwt = _bwd_dw_in(h, [(dq, ROW_Q), (dza, ROW_ZA)], 1024, 1024, "bwd_dw_in_q_za", dwt)
    land, sent = None, None
    if start_exchange is not None:
        dwt, land, sent = start_exchange[0](dwt, dwpc, dwpa, dwout)
    dwt32, dwt16, *land = _bwd_dw_in(h, [(dgab, ROW_GA)], 1024, 1024, "bwd_dw_in_gates", dwt, land)
    token, pending = (None, None) if start_exchange is None else start_exchange[1](dwt32, dwt16, land[0], sent)
    g_pre_after = g_pre if token is None else g_pre + token
    grad_x, dgpre8 = _bwd_dh(da4, dh_part, dkv, dgab, wt, x, g_pre_after, dout, 512)
    small = jnp.concatenate([dgpre8, dgpost8, jnp.pad(dsink8, ((0, 0), (0, D - 128))), dwconv8,
                             jnp.pad(loss8, ((0, 0), (0, D - 128)))], axis=0)
    return loss8[0, 0], grad_x, dwt32, dwt16, dwpc, dwpa, dwout, small, pending


def kernel(x, g_pre, g_post, w_in, w_conv, sinks, w_proj_conv, w_proj_attn, w_out, loss_target, m_g_pre, m_g_post, m_w_in, m_w_conv, m_sinks, m_w_proj_conv, m_w_proj_attn, m_w_out, v_g_pre, v_g_post, v_w_in, v_w_conv, v_sinks, v_w_proj_conv, v_w_proj_attn, v_w_out):
    batch = x.shape[0]
    mx, my, mc, me = _place()
    place_arr = jnp.stack([mc, 2 * mx + my]).astype(jnp.int32)

    g_wt, g_conv = _all_gather([w_in[0].T.astype(BF16), jnp.pad(w_conv[0], ((0, 5), (0, 0)))])
    wt = g_wt.reshape(D_IN, D)
    wconv = g_conv[:, 0:3, :].transpose(1, 0, 2).reshape(3, D)
    sq_mine = [w.astype(BF16) for w in (w_proj_conv[0], w_proj_attn[0], w_out[0])]
    wt, sq_mine = lax.optimization_barrier((wt, sq_mine))
    sq_send, sq_recv, sq_flying, sq_lands, sq_token = _gather_start(sq_mine, "gather_squares")

    def squares(after):
        got = _gather_wait(sq_send, sq_recv, sq_flying, sq_lands, after, "gather_squares")
        return [lax.dynamic_update_index_in_dim(full, mine, me, 0).reshape(D, D) for full, mine in zip(got, sq_mine)]

    early = [(0, 1, 2)] + [(0, 1, 2, 3)] * 3

    def send_early(dwt, dwpc, dwpa, dwout):
        own_sq = [g.reshape(N_DEV, SHARD_SQ, D) for g in (dwpc, dwpa, dwout)]
        send_sems, recv_sems, flying, lands = _exchange_sibling_start([dwt[1].reshape(N_DEV, SHARD_IN, D)] + own_sq, early)
        return (dwt[0], flying[0].reshape(D_IN, D)), lands[0], (send_sems, recv_sems, flying[1:], lands[1:])

    def send_late(dwt32, dwt16, land, sent):
        own_in = dwt32.reshape(N_DEV, SHARD_IN, D)
        send_sems, recv_sems, sq_flying, sq_lands = sent
        dwt16 = dwt16.reshape(N_DEV, SHARD_IN, D)
        sent_arrays, from_sibling = _exchange_sibling_wait(
            send_sems, recv_sems, [dwt16] + list(sq_flying), [land] + list(sq_lands), early)
        own_sq = sent_arrays[1:]
        in32, in16 = _pair_sum([own_in], from_sibling[:1], place_arr, SHARD_IN // 2, "pair_sum_w_in")
        sq32, sq16 = _pair_sum(own_sq, from_sibling[1:], place_arr, SHARD_SQ, "pair_sum_squares")
        send_sems, recv_sems, flying, lands, token = _exchange_chips_start(list(in16) + list(sq16))
        return token, (send_sems, recv_sems, flying, lands, in32, sq32)

    _, grad_x, _, _, _, _, _, small, pending = _local_step(
        x.reshape(batch * SEQ_LEN, D), loss_target.reshape(batch * SEQ_LEN, D), g_pre + sq_token, g_post,
        sinks, wt, wconv, squares, (send_early, send_late))
    sm_send, sm_recv, sm_flying, sm_lands, sm_token = _gather_start([small], "gather_small")
    send_sems, recv_sems, flying, lands, in32, sq32 = pending
    from_chips = _exchange_chips_wait(send_sems, recv_sems, flying, lands, sm_token)

    o_in = [o[0].T for o in _chip_sum(
        in32, from_chips[:1], SHARD_IN // 3, "chip_sum_adamw_w_in",
        adam=([w_in[0].T], [m_w_in[0].T], [v_w_in[0].T]))]
    g_in_mine, o_in = o_in[0], o_in[1:]
    g_sq, d_sq, m_sq, v_sq = _chip_sum(
        sq32, from_chips[1:], SHARD_SQ, "chip_sum_adamw_squares",
        adam=([w_proj_conv[0], w_proj_attn[0], w_out[0]], [m_w_proj_conv[0], m_w_proj_attn[0], m_w_out[0]],
              [v_w_proj_conv[0], v_w_proj_attn[0], v_w_out[0]]))
    both_done, g_in_mine = lax.optimization_barrier((d_sq[0], g_in_mine))
    (small_all,) = _gather_wait(sm_send, sm_recv, sm_flying, sm_lands, both_done, "gather_small")
    def by_row(a):
        return a.transpose(1, 0, 2)

    loss, (g_g_pre, g_g_post, g_sinks, g_conv_mine), *o_small = _finish_small(
        lax.dynamic_update_index_in_dim(small_all, small, me, 0), jnp.reshape(me, (1,)).astype(jnp.int32),
        [g_pre, g_post, sinks, by_row(w_conv)], [m_g_pre, m_g_post, m_sinks, by_row(m_w_conv)],
        [v_g_pre, v_g_post, v_sinks, by_row(v_w_conv)])
    loss = loss.reshape(())

    grads = [g_g_pre, g_g_post, g_in_mine[None], by_row(g_conv_mine), g_sinks] + [g[None] for g in g_sq]
    rest = []
    for idx, sq in enumerate((d_sq, m_sq, v_sq)):
        gp, gq, sk, cv = o_small[idx]
        rest += [gp, gq, o_in[idx][None], by_row(cv), sk] + [s[None] for s in sq]
    return (loss, grad_x.reshape(batch, SEQ_LEN, D), *grads, *rest)
```

```python
import numpy as np
import jax
import jax.numpy as jnp
from jax import lax
from jax.experimental import pallas as pl
from jax.experimental.pallas import tpu as pltpu

D = 1024
N_HEADS = 16
HEAD_DIM = 64
LOGIT_SCALE = HEAD_DIM ** -0.5
BLK = 128
SEQ_LEN = 2048
D_IN = 8448
ROW_Q, ROW_KV, ROW_ZA, ROW_GA = 4 * D, 5 * D, 5 * D + 256, 6 * D + 256
SHARD_IN = D_IN // 8
SHARD_SQ = D // 8
N_DEV = 8
V7X_VMEM_BYTES = 64 << 20
ROPE_THETA = 10000.0
RMS_EPS = 1e-6
NEG = -1e30
ADAM_LR, ADAM_B1, ADAM_B2, ADAM_EPS, ADAM_WD, ADAM_STEP = 0.001, 0.9, 0.999, 1e-08, 0.01, 10

F32 = jnp.float32
BF16 = jnp.bfloat16
MESH_ID = pl.DeviceIdType.MESH


def _dot(a, b):
    return jnp.dot(a, b, preferred_element_type=F32)


def _dot_nt(a, b):
    return lax.dot_general(a, b, (((1,), (1,)), ((), ())), preferred_element_type=F32)


def _dot_tn(a, b):
    return lax.dot_general(a, b, (((0,), (0,)), ((), ())), preferred_element_type=F32)


def _sig(z):
    return 1.0 / (1.0 + jnp.exp(-z))


def _swap_halves(z):
    lane = lax.broadcasted_iota(jnp.int32, z.shape, 1)
    return jnp.where((lane & 63) < 32, pltpu.roll(z, 96, 1), pltpu.roll(z, 32, 1))


def _row_spec(tm, width, col=0):
    return pl.BlockSpec((tm, width), lambda i: (i, col))


def _whole_vmem():
    return pl.BlockSpec(memory_space=pltpu.VMEM)


def _params(*sem, vmem_limit_bytes=None):
    return pltpu.CompilerParams(dimension_semantics=sem, vmem_limit_bytes=vmem_limit_bytes)


def _fwd_in_attn(x, g_pre, wt, cos_t, sin_t, tm):
    t = x.shape[0]
    seq_tiles = SEQ_LEN // tm

    def body(x_ref, g_ref, w_ref, c_ref, s_ref, h_ref, q_ref, kv_ref, g3_ref):
        xf = x_ref[...]
        r = lax.rsqrt(jnp.mean(xf * xf, axis=-1, keepdims=True) + RMS_EPS)
        hh = ((xf * r) * g_ref[...]).astype(BF16)
        h_ref[...] = hh
        c = c_ref[...]
        s = s_ref[...]

        def rope(z):
            return z * c + _swap_halves(z) * s

        q = _dot_nt(hh, w_ref[ROW_Q:ROW_Q + D, :])
        for j in range(D // 128):
            q_ref[:, j * 128:(j + 1) * 128] = (rope(q[:, j * 128:(j + 1) * 128]) * LOGIT_SCALE).astype(BF16)
        kv = _dot_nt(hh, w_ref[ROW_KV:ROW_KV + 256, :])
        kv_ref[:, 0:128] = rope(kv[:, 0:128]).astype(BF16)
        kv_ref[:, 128:256] = kv[:, 128:256].astype(BF16)
        for j in range(3):
            g3_ref[:, j * D:(j + 1) * D] = _dot_nt(hh, w_ref[ROW_ZA + j * D:ROW_ZA + (j + 1) * D, :])

    tab = pl.BlockSpec((tm, 128), lambda i: (i % seq_tiles, 0))
    return pl.pallas_call(
        body, name="fwd_in_attn", grid=(t // tm,),
        in_specs=[_row_spec(tm, D), pl.BlockSpec((1, D), lambda i: (0, 0)), _whole_vmem(), tab, tab],
        out_specs=[_row_spec(tm, D), _row_spec(tm, D), _row_spec(tm, 256), _row_spec(tm, 3 * D)],
        out_shape=[jax.ShapeDtypeStruct((t, D), BF16), jax.ShapeDtypeStruct((t, D), BF16),
                   jax.ShapeDtypeStruct((t, 256), BF16), jax.ShapeDtypeStruct((t, 3 * D), F32)],
        compiler_params=_params("parallel"),
    )(x, g_pre, wt, cos_t, sin_t)


def _conv_forward(xc, bg, cg, zc, up6, up7, w_ref):
    rows = lax.broadcasted_iota(jnp.int32, xc.shape, 0)
    u = cg * xc
    u_m1 = jnp.where(rows == 0, up7, pltpu.roll(u, 1, 0))
    u_m2 = jnp.where(rows == 0, up6, jnp.where(rows == 1, up7, pltpu.roll(u, 2, 0)))
    yconv = w_ref[0:1, :] * u_m2 + w_ref[1:2, :] * u_m1 + w_ref[2:3, :] * u
    sg = _sig(zc)
    sz = zc * sg
    co = bg * yconv
    return u, u_m1, u_m2, yconv, sg, sz, co


def _fwd_in_conv(h, wt, wconv8, wpc, tm):
    t = h.shape[0]
    seq_tiles = SEQ_LEN // tm

    def body(h_ref, w_ref, wc_ref, wpc_ref, a4_ref, ya_ref, last_u_ref):
        hh = h_ref[...]
        xc, bg, cg, zc = (_dot_nt(hh, w_ref[j * D:(j + 1) * D, :]) for j in range(4))
        for j, z in enumerate((xc, bg, cg, zc)):
            a4_ref[:, j * D:(j + 1) * D] = z.astype(BF16)
        first = pl.program_id(0) % seq_tiles == 0
        up6 = jnp.where(first, 0.0, last_u_ref[6:7, :])
        up7 = jnp.where(first, 0.0, last_u_ref[7:8, :])
        u, _, _, _, _, sz, co = _conv_forward(xc, bg, cg, zc, up6, up7, wc_ref)
        last_u_ref[...] = u[tm - 8:tm, :]
        ya_ref[...] = _dot((sz * co).astype(BF16), wpc_ref[...])

    return pl.pallas_call(
        body, name="fwd_in_conv", grid=(t // tm,),
        in_specs=[_row_spec(tm, D), _whole_vmem(), pl.BlockSpec((8, D), lambda i: (0, 0)), _whole_vmem()],
        out_specs=[_row_spec(tm, 4 * D), _row_spec(tm, D)],
        out_shape=[jax.ShapeDtypeStruct((t, 4 * D), BF16), jax.ShapeDtypeStruct((t, D), F32)],
        scratch_shapes=[pltpu.VMEM((8, D), F32)],
        compiler_params=_params("arbitrary"),
    )(h, wt, wconv8, wpc)


STACK = 4 * BLK


def _band_mask(first):
    qi = lax.broadcasted_iota(jnp.int32, (STACK, 2 * BLK), 0) & (BLK - 1)
    kj = lax.broadcasted_iota(jnp.int32, (STACK, 2 * BLK), 1)
    return (kj > qi) & (kj <= qi + BLK) & (kj >= jnp.where(first, BLK, 0))


def _masked_fill(sink_ref, g, e):
    kj = lax.broadcasted_iota(jnp.int32, (STACK, 2 * BLK), 1)
    sink = jnp.concatenate([jnp.full((BLK, 2 * BLK), sink_ref[0, 2 * (4 * g + jj) + e], F32) for jj in range(4)], axis=0)
    return jnp.where(kj == 0, sink, NEG)


def _padded_pair(before, own):
    z = jnp.concatenate([before, own], axis=0).astype(F32)
    z = jnp.where(lax.broadcasted_iota(jnp.int32, z.shape, 0) == 0, 0.0, z)
    zs = pltpu.roll(z, 64, 1)
    lo = lax.broadcasted_iota(jnp.int32, z.shape, 1) < 64
    zero = jnp.zeros_like(z)
    left = [jnp.where(lo, z, zero).astype(BF16), jnp.where(lo, zs, zero).astype(BF16)]
    right = [jnp.where(lo, zero, zs).astype(BF16), jnp.where(lo, zero, z).astype(BF16)]
    return left, right


def _exp_logits(s, valid, fill):
    s = jnp.where(valid, s, fill)
    m = jnp.max(s, axis=-1, keepdims=True)
    return jnp.exp(s - m), m


def _kv_blocks(kvc_ref, kvp_ref, b, col):
    own = kvc_ref[b * BLK:(b + 1) * BLK, col:col + 128]
    before = kvp_ref[:, col:col + 128] if b == 0 else kvc_ref[(b - 1) * BLK:b * BLK, col:col + 128]
    return before, own


def _fwd_attn(sinks, q, kv, g3, blocks):
    t = q.shape[0]
    tq = blocks * BLK
    seq_blocks = SEQ_LEN // BLK

    def body(sink_ref, q_ref, kvc_ref, kvp_ref, za_ref, attn_ref, ub_ref, lse_ref):
        lo = lax.broadcasted_iota(jnp.int32, (STACK, 128), 1) < 64
        for b in range(blocks):
            rows = slice(b * BLK, (b + 1) * BLK)
            valid = _band_mask((pl.program_id(0) * blocks + b) % seq_blocks == 0)
            k_pad = _padded_pair(*_kv_blocks(kvc_ref, kvp_ref, b, 0))
            v_pad = _padded_pair(*_kv_blocks(kvc_ref, kvp_ref, b, 128))
            for g in range(2):
                qg = jnp.concatenate([q_ref[rows, j * 128:(j + 1) * 128] for j in range(4 * g, 4 * g + 4)], axis=0)
                pv, den = [], []
                for e in range(2):
                    p, m = _exp_logits(_dot_nt(qg, k_pad[e][g]), valid, _masked_fill(sink_ref, g, e))
                    both = _dot(p.astype(BF16), jnp.concatenate([v_pad[e][g], jnp.ones((2 * BLK, 128), BF16)], axis=1))
                    pv.append(both[:, 0:128])
                    den.append(both[:, 128:256])
                    lse_ref[b, 2 * g + e] = m + jnp.log(den[e])
                o = jnp.where(lo, pv[0] / den[0], pv[1] / den[1])
                for jj in range(4):
                    cols = slice((4 * g + jj) * 128, (4 * g + jj + 1) * 128)
                    oj = o[jj * BLK:(jj + 1) * BLK, :]
                    attn_ref[rows, cols] = oj
                    za = za_ref[rows, cols]
                    ub_ref[rows, cols] = (za * _sig(za) * oj).astype(BF16)

    return pl.pallas_call(
        body, name="fwd_attn", grid=(t // tq,),
        in_specs=[pl.BlockSpec(memory_space=pltpu.SMEM), _row_spec(tq, D), _row_spec(tq, 256),
                  pl.BlockSpec((BLK, 256), lambda i: (jnp.maximum(i * blocks - 1, 0), 0)), _row_spec(tq, D, 0)],
        out_specs=[_row_spec(tq, D), _row_spec(tq, D), pl.BlockSpec((blocks, 4, STACK, 128), lambda i: (i, 0, 0, 0))],
        out_shape=[jax.ShapeDtypeStruct((t, D), F32), jax.ShapeDtypeStruct((t, D), BF16),
                   jax.ShapeDtypeStruct((t // BLK, 4, STACK, 128), F32)],
        compiler_params=_params("parallel"),
    )(sinks, q, kv, kv, g3)


def _fwd_out_bwd_head(ya, ub, g3, x, target, g_post, wpa, wout, tm):
    t = x.shape[0]

    def body(ya_ref, ub_ref, ga_ref, gb_ref, x_ref, tgt_ref, gp_ref, wpa_ref, wout_ref,
             loss_ref, dout_ref, dya_ref, dub_ref, dgab_ref, dwout_ref, dwpa_ref, dgp_ref):
        @pl.when(pl.program_id(0) == 0)
        def _():
            loss_ref[...] = jnp.zeros_like(loss_ref)
            dwout_ref[...] = jnp.zeros_like(dwout_ref)
            dwpa_ref[...] = jnp.zeros_like(dwpa_ref)
            dgp_ref[...] = jnp.zeros_like(dgp_ref)

        g = gp_ref[...]
        halves = (slice(0, tm // 2), slice(tm // 2, tm))

        def stage1(rows):
            return _dot(ub_ref[rows, :], wpa_ref[...])

        def stage2(rows, yb):
            sa = _sig(ga_ref[rows, :])
            sb = _sig(gb_ref[rows, :])
            mb = (sa * ya_ref[rows, :] + sb * yb).astype(BF16)
            return sa, sb, mb, _dot(mb, wout_ref[...])

        def stage3(rows, y):
            r = lax.rsqrt(jnp.mean(y * y, axis=-1, keepdims=True) + RMS_EPS)
            n = y * r
            err = (x_ref[rows, :] + n * g) - tgt_ref[rows, :]
            sq = jnp.sum(jnp.sum(err * err, axis=0, keepdims=True), axis=1, keepdims=True)
            dout = err * (1.0 / D)
            dout_ref[rows, :] = dout
            dgp = jnp.sum(dout * n, axis=0, keepdims=True)
            dn = dout * g
            dy = (r * (dn - n * jnp.mean(dn * n, axis=-1, keepdims=True))).astype(BF16)
            return sq, dgp, dy, _dot_nt(dy, wout_ref[...])

        def stage4(rows, dm, sa, sb, yb):
            dya_ref[rows, :] = (dm * sa).astype(BF16)
            dyb = (dm * sb).astype(BF16)
            dgab_ref[rows, 0:D] = (dm * ya_ref[rows, :] * (sa * (1.0 - sa))).astype(BF16)
            dgab_ref[rows, D:2 * D] = (dm * yb * (sb * (1.0 - sb))).astype(BF16)
            dub_ref[rows, :] = _dot_nt(dyb, wpa_ref[...])
            return dyb

        yb = [stage1(rows) for rows in halves]
        s2 = [stage2(rows, yb[k]) for k, rows in enumerate(halves)]
        s3 = [stage3(rows, s2[k][3]) for k, rows in enumerate(halves)]
        dyb = [stage4(rows, s3[k][3], s2[k][0], s2[k][1], yb[k]) for k, rows in enumerate(halves)]
        loss_ref[...] += sum(s[0] for s in s3) * (0.5 / D)
        dgp_ref[0:1, :] += sum(s[1] for s in s3)
        dwout_ref[...] += _dot_tn(jnp.concatenate([s[2] for s in s2], axis=0), jnp.concatenate([s[2] for s in s3], axis=0))
        dwpa_ref[...] += _dot_tn(ub_ref[...], jnp.concatenate(dyb, axis=0))

    return pl.pallas_call(
        body, name="fwd_out_bwd_head", grid=(t // tm,),
        in_specs=[_row_spec(tm, D), _row_spec(tm, D), _row_spec(tm, D, 1), _row_spec(tm, D, 2),
                  _row_spec(tm, D), _row_spec(tm, D), pl.BlockSpec((1, D), lambda i: (0, 0)),
                  _whole_vmem(), _whole_vmem()],
        out_specs=[pl.BlockSpec((8, 128), lambda i: (0, 0)), _row_spec(tm, D), _row_spec(tm, D), _row_spec(tm, D),
                   _row_spec(tm, 2 * D), _whole_vmem(), _whole_vmem(), pl.BlockSpec((8, D), lambda i: (0, 0))],
        out_shape=[jax.ShapeDtypeStruct((8, 128), F32), jax.ShapeDtypeStruct((t, D), F32),
                   jax.ShapeDtypeStruct((t, D), BF16), jax.ShapeDtypeStruct((t, D), F32),
                   jax.ShapeDtypeStruct((t, 2 * D), BF16), jax.ShapeDtypeStruct((D, D), F32),
                   jax.ShapeDtypeStruct((D, D), F32), jax.ShapeDtypeStruct((8, D), F32)],
        compiler_params=_params("arbitrary", vmem_limit_bytes=V7X_VMEM_BYTES - (2 << 20)),
    )(ya, ub, g3, g3, x, target, g_post, wpa, wout)


def _bwd_attn(sinks, q, kv, attn, lse, dub, g3, cos_t, sin_t, wt, blocks):
    t = q.shape[0]
    tq = blocks * BLK
    seq_blocks = SEQ_LEN // BLK

    def body(sink_ref, q_ref, kvc_ref, kvp_ref, attn_ref, lse_ref, dub_ref, za_ref, c_ref, s_ref, w_ref,
             dq_ref, dza_ref, dkv_own_ref, dkv_prev_ref, dsink_ref, dh_ref):
        @pl.when(pl.program_id(0) == 0)
        def _():
            dsink_ref[...] = jnp.zeros_like(dsink_ref)

        lo = lax.broadcasted_iota(jnp.int32, (STACK, 128), 1) < 64
        lane8 = lax.broadcasted_iota(jnp.int32, (8, 128), 1)
        lo2 = lax.broadcasted_iota(jnp.int32, (2 * BLK, 128), 1) < 64
        sink_row = lax.broadcasted_iota(jnp.int32, (2 * BLK, 128), 0) == 0
        dsink = jnp.zeros((8, 128), F32)
        for b in range(blocks):
            rows = slice(b * BLK, (b + 1) * BLK)
            valid = _band_mask((pl.program_id(0) * blocks + b) % seq_blocks == 0)
            k_pad = _padded_pair(*_kv_blocks(kvc_ref, kvp_ref, b, 0))
            v_pad = _padded_pair(*_kv_blocks(kvc_ref, kvp_ref, b, 128))
            c = c_ref[rows, :]
            s = s_ref[rows, :]
            dk_acc, dv_acc = [], []
            for g in range(2):
                qg, dog = [], []
                for j in range(4 * g, 4 * g + 4):
                    cols = slice(j * 128, (j + 1) * 128)
                    za = za_ref[rows, cols]
                    sg = _sig(za)
                    dub = dub_ref[rows, cols]
                    dza_ref[rows, cols] = (dub * attn_ref[rows, cols] * (sg * (1.0 + za * (1.0 - sg)))).astype(BF16)
                    dog.append((dub * (za * sg)).astype(BF16))
                    qg.append(q_ref[rows, cols])
                qg = jnp.concatenate(qg, axis=0)
                dog = jnp.concatenate(dog, axis=0)
                dq = jnp.zeros((STACK, 128), F32)
                ds_both, p_both = [], []
                for e in range(2):
                    s_masked = jnp.where(valid, _dot_nt(qg, k_pad[e][g]), _masked_fill(sink_ref, g, e))
                    lse_rows = lse_ref[b, 2 * g + e]
                    p = jnp.exp(s_masked - jnp.concatenate([lse_rows, lse_rows], axis=1))
                    dp = _dot_nt(dog, v_pad[e][g])
                    ds = p * (dp - jnp.sum(p * dp, axis=-1, keepdims=True))
                    for jj in range(4):
                        tot = jnp.sum(ds[jj * BLK:(jj + 1) * BLK, 0:1], axis=0, keepdims=True)
                        dsink = dsink + jnp.where(lane8 == 2 * (4 * g + jj) + e, tot, 0.0)
                    ds = ds.astype(BF16)
                    dq = dq + _dot(ds, k_pad[e][g])
                    ds_both.append(ds)
                    p_both.append(p.astype(BF16))
                zero = jnp.zeros_like(qg)
                q2 = jnp.concatenate([jnp.where(lo, qg, zero), jnp.where(lo, zero, qg)], axis=0)
                do2 = jnp.concatenate([jnp.where(lo, dog, zero), jnp.where(lo, zero, dog)], axis=0)
                dk_acc.append(_dot_tn(q2, jnp.concatenate(ds_both, axis=0)).T)
                dv_acc.append(_dot_tn(do2, jnp.concatenate(p_both, axis=0)).T)
                for jj in range(4):
                    cols = slice((4 * g + jj) * 128, (4 * g + jj + 1) * 128)
                    dqj = dq[jj * BLK:(jj + 1) * BLK, :] * LOGIT_SCALE
                    dq_ref[rows, cols] = (dqj * c - _swap_halves(dqj) * s).astype(BF16)
            for col, acc in ((0, dk_acc), (128, dv_acc)):
                both = jnp.where(lo2, acc[0] + pltpu.roll(acc[0], 64, 1), acc[1] + pltpu.roll(acc[1], 64, 1))
                both = jnp.where(sink_row, 0.0, both)
                dkv_prev_ref[rows, col:col + 128] = both[0:BLK, :]
                dkv_own_ref[rows, col:col + 128] = both[BLK:2 * BLK, :]
        dsink_ref[...] += dsink
        dh_ref[...] = _dot(dq_ref[...], w_ref[ROW_Q:ROW_KV, :]) + _dot(dza_ref[...], w_ref[ROW_ZA:ROW_GA, :])

    tab = pl.BlockSpec((tq, 128), lambda i: (i % (SEQ_LEN // tq), 0))
    return pl.pallas_call(
        body, name="bwd_attn", grid=(t // tq,),
        in_specs=[pl.BlockSpec(memory_space=pltpu.SMEM), _row_spec(tq, D), _row_spec(tq, 256),
                  pl.BlockSpec((BLK, 256), lambda i: (jnp.maximum(i * blocks - 1, 0), 0)),
                  _row_spec(tq, D), pl.BlockSpec((blocks, 4, STACK, 128), lambda i: (i, 0, 0, 0)),
                  _row_spec(tq, D), _row_spec(tq, D, 0), tab, tab, _whole_vmem()],
        out_specs=[_row_spec(tq, D), _row_spec(tq, D), _row_spec(tq, 256), _row_spec(tq, 256),
                   pl.BlockSpec((8, 128), lambda i: (0, 0)), _row_spec(tq, D)],
        out_shape=[jax.ShapeDtypeStruct((t, D), BF16), jax.ShapeDtypeStruct((t, D), BF16),
                   jax.ShapeDtypeStruct((t, 256), F32), jax.ShapeDtypeStruct((t, 256), F32),
                   jax.ShapeDtypeStruct((8, 128), F32), jax.ShapeDtypeStruct((t, D), F32)],
        compiler_params=_params("arbitrary"),
    )(sinks, q, kv, kv, attn, lse, dub, g3, cos_t, sin_t, wt)


def _bwd_kv_finish(dkv_own, dkv_prev, cos_t, sin_t, h, prev):
    t = dkv_own.shape[0]
    tm = SEQ_LEN
    n_t = t // tm
    seq_tiles = SEQ_LEN // tm
    n_blocks = t // BLK

    def body(own_ref, same_ref, nxt_ref, c_ref, s_ref, h_ref, o32_in, o16_in, out_ref, o32_ref, o16_ref,
             acc_ref, acc16_ref, sems):
        step = pl.program_id(0)

        @pl.when(step == 0)
        def _():
            acc_ref[...] = jnp.zeros_like(acc_ref)

        keep = jnp.where(step % seq_tiles == seq_tiles - 1, 0.0, 1.0)
        shifted = jnp.concatenate([same_ref[BLK:tm, :], nxt_ref[...] * keep], axis=0)
        tot = own_ref[...] + shifted
        dk = tot[:, 0:128]
        out_ref[:, 0:128] = (dk * c_ref[...] - _swap_halves(dk) * s_ref[...]).astype(BF16)
        out_ref[:, 128:256] = tot[:, 128:256].astype(BF16)
        acc_ref[...] += _dot_tn(out_ref[...], h_ref[...])

        @pl.when(step == n_t - 1)
        def _():
            acc16_ref[...] = acc_ref[...].astype(BF16)
            rows = pl.ds(ROW_KV, 256)
            c32 = pltpu.make_async_copy(acc_ref, o32_ref.at[rows], sems.at[0])
            c16 = pltpu.make_async_copy(acc16_ref, o16_ref.at[rows], sems.at[1])
            c32.start()
            c16.start()
            c32.wait()
            c16.wait()

    tab = pl.BlockSpec((tm, 128), lambda i: (i % seq_tiles, 0))
    hbm = pl.BlockSpec(memory_space=pl.ANY)
    out = pl.pallas_call(
        body, name="bwd_kv_finish", grid=(n_t,),
        in_specs=[_row_spec(tm, 256), _row_spec(tm, 256),
                  pl.BlockSpec((BLK, 256), lambda i: (jnp.minimum((i + 1) * (tm // BLK), n_blocks - 1), 0)), tab, tab,
                  _row_spec(tm, D), hbm, hbm],
        out_specs=[_row_spec(tm, 256), hbm, hbm],
        out_shape=[jax.ShapeDtypeStruct((t, 256), BF16), jax.ShapeDtypeStruct((D_IN, D), F32),
                   jax.ShapeDtypeStruct((D_IN, D), BF16)],
        scratch_shapes=[pltpu.VMEM((256, D), F32), pltpu.VMEM((256, D), BF16), pltpu.SemaphoreType.DMA((2,))],
        input_output_aliases={6: 1, 7: 2},
        compiler_params=_params("arbitrary"),
    )(dkv_own, dkv_prev, dkv_prev, cos_t, sin_t, h, *prev)
    return out[0], (out[1], out[2])


STAGE_ROWS = 256


def _bwd_conv(dya, a4, h, wconv8, wpc, tm, parts):
    t = a4.shape[0]
    n_t = t // tm
    sub = tm // parts
    seq_tiles = SEQ_LEN // tm

    def body(dya_ref, xc_ref, bg_ref, cg_ref, zc_ref, xcp_ref, cgp_ref, w_ref, wpc_ref, h_ref,
             da4_ref, dwpc_ref, dwc_ref, o32_ref, o16_ref, acc_ref, stage_ref, later_ref, sems):
        step = pl.program_id(0)
        tile = n_t - 1 - step

        @pl.when(step == 0)
        def _():
            dwpc_ref[...] = jnp.zeros_like(dwpc_ref)
            dwc_ref[...] = jnp.zeros_like(dwc_ref)
            acc_ref[...] = jnp.zeros_like(acc_ref)

        keep_prev = jnp.where(tile % seq_tiles == 0, 0.0, 1.0)
        ends_sequence = tile % seq_tiles == seq_tiles - 1

        def part(p, later):
            r0 = p * sub
            here = slice(r0, r0 + sub)
            if p == 0:
                u_prev = cgp_ref[14:16, :].astype(F32) * xcp_ref[14:16, :].astype(F32) * keep_prev
            else:
                u_prev = cg_ref[r0 - 2:r0, :].astype(F32) * xc_ref[r0 - 2:r0, :].astype(F32)
            xc = xc_ref[here, :].astype(F32)
            bg = bg_ref[here, :].astype(F32)
            cg = cg_ref[here, :].astype(F32)
            zc = zc_ref[here, :].astype(F32)
            u, u_m1, u_m2, yconv, sg, sz, co = _conv_forward(xc, bg, cg, zc, u_prev[0:1, :], u_prev[1:2, :], w_ref)
            ua = (sz * co).astype(BF16)
            dua = _dot_nt(dya_ref[here, :], wpc_ref[...])
            da4_ref[here, 3 * D:4 * D] = (dua * co * (sg * (1.0 + zc * (1.0 - sg)))).astype(BF16)
            dco = dua * sz
            da4_ref[here, D:2 * D] = (dco * yconv).astype(BF16)
            dyc = dco * bg
            dwc = jnp.concatenate([jnp.sum(dyc * s, axis=0, keepdims=True) for s in (u_m2, u_m1, u)], axis=0)
            rows = lax.broadcasted_iota(jnp.int32, xc.shape, 0)
            n0 = later[0:1, :]
            n1 = later[1:2, :]
            dyc_p1 = jnp.where(rows == sub - 1, n0, pltpu.roll(dyc, sub - 1, 0))
            dyc_p2 = jnp.where(rows == sub - 2, n0, jnp.where(rows == sub - 1, n1, pltpu.roll(dyc, sub - 2, 0)))
            du = w_ref[2:3, :] * dyc + w_ref[1:2, :] * dyc_p1 + w_ref[0:1, :] * dyc_p2
            da4_ref[here, 0:D] = (du * cg).astype(BF16)
            da4_ref[here, 2 * D:3 * D] = (du * xc).astype(BF16)
            return ua, dwc, dyc[0:8, :]

        later = jnp.where(ends_sequence, 0.0, later_ref[...])
        uas, dwc = [], jnp.zeros((3, D), F32)
        for p in reversed(range(parts)):
            ua, dwc_p, later = part(p, later)
            uas.insert(0, ua)
            dwc = dwc + dwc_p
        later_ref[...] = later
        dwpc_ref[...] += _dot_tn(jnp.concatenate(uas, axis=0), dya_ref[...])
        dwc_ref[0:3, :] += dwc
        for j in range(4):
            acc_ref[j * D:(j + 1) * D, :] += _dot_tn(da4_ref[:, j * D:(j + 1) * D], h_ref[...])

        @pl.when(step == n_t - 1)
        def _():
            c32 = pltpu.make_async_copy(acc_ref, o32_ref.at[pl.ds(0, 4 * D)], sems.at[0])
            c32.start()
            for j in range(4 * D // STAGE_ROWS):
                rows = pl.ds(j * STAGE_ROWS, STAGE_ROWS)
                stage_ref[...] = acc_ref[rows, :].astype(BF16)
                c16 = pltpu.make_async_copy(stage_ref, o16_ref.at[rows], sems.at[1])
                c16.start()
                c16.wait()
            c32.wait()

    def rows_of_tile(width, col=0):
        return pl.BlockSpec((tm, width), lambda s: (n_t - 1 - s, col))

    def prev(col):
        return pl.BlockSpec((16, D), lambda s: (jnp.maximum((n_t - 1 - s) * (tm // 16) - 1, 0), col))

    hbm = pl.BlockSpec(memory_space=pl.ANY)
    out = pl.pallas_call(
        body, name="bwd_conv", grid=(n_t,),
        in_specs=[rows_of_tile(D), rows_of_tile(D, 0), rows_of_tile(D, 1), rows_of_tile(D, 2), rows_of_tile(D, 3),
                  prev(0), prev(2), pl.BlockSpec((8, D), lambda s: (0, 0)), _whole_vmem(), rows_of_tile(D)],
        out_specs=[rows_of_tile(4 * D), _whole_vmem(), pl.BlockSpec((8, D), lambda s: (0, 0)), hbm, hbm],
        out_shape=[jax.ShapeDtypeStruct((t, 4 * D), BF16), jax.ShapeDtypeStruct((D, D), F32),
                   jax.ShapeDtypeStruct((8, D), F32), jax.ShapeDtypeStruct((D_IN, D), F32),
                   jax.ShapeDtypeStruct((D_IN, D), BF16)],
        scratch_shapes=[pltpu.VMEM((4 * D, D), F32), pltpu.VMEM((STAGE_ROWS, D), BF16), pltpu.VMEM((8, D), F32),
                        pltpu.SemaphoreType.DMA((2,))],
        compiler_params=pltpu.CompilerParams(dimension_semantics=("arbitrary",), vmem_limit_bytes=V7X_VMEM_BYTES - (2 << 20)),
    )(dya, a4, a4, a4, a4, a4, a4, wconv8, wpc, h)
    return out[0], out[1], out[2], (out[3], out[4])


def _bwd_dh(da4, dh_part, dkv, dgab, wt, x, g_pre, dout, tm):
    t = x.shape[0]

    def body(da4_ref, dhp_ref, dkv_ref, dgab_ref, w_ref, x_ref, g_ref, dout_ref, gx_ref, dg_ref):
        @pl.when(pl.program_id(0) == 0)
        def _():
            dg_ref[...] = jnp.zeros_like(dg_ref)

        dh = dhp_ref[...] + _dot(da4_ref[...], w_ref[0:ROW_Q, :])
        dh += _dot(dkv_ref[...], w_ref[ROW_KV:ROW_ZA, :])
        dh += _dot(dgab_ref[...], w_ref[ROW_GA:D_IN, :])
        xf = x_ref[...]
        r = lax.rsqrt(jnp.mean(xf * xf, axis=-1, keepdims=True) + RMS_EPS)
        xn = xf * r
        dg_ref[0:1, :] += jnp.sum(dh * xn, axis=0, keepdims=True)
        dxn = dh * g_ref[...]
        gx_ref[...] = dout_ref[...] + r * (dxn - xn * jnp.mean(dxn * xn, axis=-1, keepdims=True))

    return pl.pallas_call(
        body, name="bwd_dh", grid=(t // tm,),
        in_specs=[_row_spec(tm, 4 * D), _row_spec(tm, D), _row_spec(tm, 256), _row_spec(tm, 2 * D),
                  _whole_vmem(), _row_spec(tm, D), pl.BlockSpec((1, D), lambda i: (0, 0)), _row_spec(tm, D)],
        out_specs=[_row_spec(tm, D), pl.BlockSpec((8, D), lambda i: (0, 0))],
        out_shape=[jax.ShapeDtypeStruct((t, D), F32), jax.ShapeDtypeStruct((8, D), F32)],
        compiler_params=_params("arbitrary"),
    )(da4, dh_part, dkv, dgab, wt, x, g_pre, dout)


def _bwd_dw_in(h, pieces, nb, tm, name, prev, land=None):
    n_t = h.shape[0] // tm
    n_a = len(pieces)
    jobs = [(a, row0 + b * nb) for a, (arr, row0) in enumerate(pieces) for b in range(arr.shape[1] // nb)]
    first = [min(k for k, (a, _) in enumerate(jobs) if a == b) for b in range(n_a)]
    n_j = len(jobs)

    def body(*refs):
        h_ref, p_refs = refs[0], refs[1:1 + n_a]
        if land is None:
            o32_ref, o16_ref, acc_ref, acc16_ref, sems = refs[-5:]
        else:
            o32_ref, o16_ref, land_ref, acc_ref, acc16_ref, sems, far_sems = refs[-7:]
        j, i = pl.program_id(0), pl.program_id(1)

        def copies(k):
            rows = pl.ds(jobs[k][1], nb)
            return (pltpu.make_async_copy(acc_ref.at[k], o32_ref.at[rows], sems.at[0, k]),
                    pltpu.make_async_copy(acc16_ref.at[k], o16_ref.at[rows], sems.at[1, k]))

        for k, (a, _) in enumerate(jobs):
            @pl.when(j == k)
            def _(k=k, a=a):
                @pl.when(i == 0)
                def _():
                    acc_ref[k] = jnp.zeros((nb, D), F32)

                acc_ref[k] += _dot_tn(p_refs[a][...], h_ref[...])

                @pl.when(i == n_t - 1)
                def _():
                    acc16_ref[k] = acc_ref[k].astype(BF16)
                    for cp in copies(k):
                        cp.start()

        @pl.when((j == n_j - 1) & (i == n_t - 1))
        def _():
            for k in range(n_j):
                for cp in copies(k):
                    cp.wait()
            if land is not None:
                x, y, c, _ = _place()
                far = pltpu.make_async_remote_copy(
                    src_ref=o16_ref.at[pl.ds(pl.multiple_of((7 - c) * SHARD_IN, 16), SHARD_IN)], dst_ref=land_ref.at[3],
                    send_sem=far_sems.at[0], recv_sem=far_sems.at[1], device_id=_peer(x, y, c, 1),
                    device_id_type=MESH_ID)
                far.start()
                far.wait_recv()
                far.wait_send()

    def piece_spec(a):
        s, e = first[a], first[a] + pieces[a][0].shape[1] // nb
        return pl.BlockSpec((tm, nb), lambda j, i: (jnp.where(j < s, 0, jnp.where(j >= e, n_t - 1, i)),
                                                    jnp.clip(j - s, 0, e - s - 1)))

    hbm = pl.BlockSpec(memory_space=pl.ANY)
    carried = list(prev) + ([] if land is None else [land])
    return pl.pallas_call(
        body, name=name, grid=(n_j, n_t),
        in_specs=[pl.BlockSpec((tm, D), lambda j, i: (i, 0))] + [piece_spec(a) for a in range(n_a)]
        + [hbm] * len(carried),
        out_specs=[hbm] * len(carried),
        out_shape=[jax.ShapeDtypeStruct((D_IN, D), F32), jax.ShapeDtypeStruct((D_IN, D), BF16)]
        + ([] if land is None else [jax.ShapeDtypeStruct(land.shape, land.dtype)]),
        scratch_shapes=[pltpu.VMEM((n_j, nb, D), F32), pltpu.VMEM((n_j, nb, D), BF16),
                        pltpu.SemaphoreType.DMA((2, n_j))] + ([] if land is None else [pltpu.SemaphoreType.DMA((2,))]),
        input_output_aliases={1 + n_a + a: a for a in range(len(carried))},
        compiler_params=_params("arbitrary", "arbitrary", vmem_limit_bytes=48 << 20),
    )(h, *[arr for arr, _ in pieces], *carried)


def _place():
    x, y, c = lax.axis_index("x"), lax.axis_index("y"), lax.axis_index("c")
    return x, y, c, 4 * x + 2 * y + c


def _peer(x, y, c, k):
    return (1 - x if k & 4 else x, 1 - y if k & 2 else y, 1 - c if k & 1 else c)


ICI_MASKS = (4, 2, 6)


def _all_gather(shards):
    n = len(shards)

    def body(*refs):
        src, dst = refs[:n], refs[n:2 * n]
        send_sems, recv_sems, local_sems = refs[2 * n:]
        x, y, c, me = _place()
        sibling = _peer(x, y, c, 1)

        def copy(a, s, block, to, own=False):
            return pltpu.make_async_remote_copy(
                src_ref=src[a] if own else dst[a].at[block], dst_ref=dst[a].at[block],
                send_sem=send_sems.at[a * 7 + s], recv_sem=recv_sems.at[a * 7 + s], device_id=to, device_id_type=MESH_ID)

        local = [pltpu.make_async_copy(src[a], dst[a].at[me], local_sems.at[a]) for a in range(n)]
        for cp in local:
            cp.start()
        started = [copy(a, 0, me, sibling, own=True) for a in range(n)]
        started += [copy(a, 1 + j, me, _peer(x, y, c, k), own=True) for j, k in enumerate(ICI_MASKS) for a in range(n)]
        for cp in started:
            cp.start()
        for j, k in enumerate(ICI_MASKS):
            for a in range(n):
                copy(a, 1 + j, me ^ k, sibling).wait_recv()
                fwd = copy(a, 4 + j, me ^ k, sibling)
                fwd.start()
                started.append(fwd)
        for a in range(n):
            copy(a, 0, me ^ 1, sibling).wait_recv()
        for j, k in enumerate(ICI_MASKS):
            for a in range(n):
                copy(a, 4 + j, me ^ 1 ^ k, sibling).wait_recv()
        for cp in started:
            cp.wait_send()
        for cp in local:
            cp.wait()

    hbm = pl.BlockSpec(memory_space=pl.ANY)
    return pl.pallas_call(
        body, name="all_gather_weights",
        in_specs=[hbm] * n, out_specs=[hbm] * n,
        out_shape=[jax.ShapeDtypeStruct((N_DEV,) + s.shape, s.dtype) for s in shards],
        scratch_shapes=[pltpu.SemaphoreType.DMA((7 * n,)), pltpu.SemaphoreType.DMA((7 * n,)),
                        pltpu.SemaphoreType.DMA((n,))],
    )(*shards)


def _direct_copies(src, land, send_sems, recv_sems):
    x, y, c, me = _place()
    return [pltpu.make_async_remote_copy(
        src_ref=src[a], dst_ref=land[a].at[me], send_sem=send_sems.at[a * 7 + k - 1],
        recv_sem=recv_sems.at[a * 7 + k - 1], device_id=_peer(x, y, c, k), device_id_type=MESH_ID)
        for k in range(1, N_DEV) for a in range(len(src))]


def _gather_start(shards, name):
    n = len(shards)

    def body(*refs):
        src, land = refs[:n], refs[n:2 * n]
        send_sems, recv_sems = refs[2 * n], refs[2 * n + 1]
        token_ref = refs[-1]
        for cp in _direct_copies(src, land, send_sems, recv_sems):
            cp.start()
        token_ref[...] = jnp.zeros_like(token_ref)

    hbm = pl.BlockSpec(memory_space=pltpu.HBM)
    sem = pl.BlockSpec(memory_space=pltpu.SEMAPHORE)
    lands = [lax.empty((N_DEV,) + s.shape, s.dtype) for s in shards]
    out = pl.pallas_call(
        body, name=name + "_start",
        out_shape=(pltpu.SemaphoreType.DMA((7 * n,)), pltpu.SemaphoreType.DMA((7 * n,)),
                   *[pltpu.HBM(s.shape, s.dtype) for s in shards], *[pltpu.HBM(s.shape, s.dtype) for s in lands],
                   jax.ShapeDtypeStruct((1, D), F32)),
        in_specs=[hbm] * (2 * n), out_specs=(sem, sem, *[hbm] * (2 * n), _whole_vmem()),
        input_output_aliases={a: 2 + a for a in range(2 * n)},
        compiler_params=pltpu.CompilerParams(has_side_effects=pltpu.SideEffectType.DATAFLOW_SIDE_EFFECTING),
    )(*[pltpu.with_memory_space_constraint(s, pltpu.HBM) for s in list(shards) + lands])
    return out[0], out[1], out[2:2 + n], out[2 + n:2 + 2 * n], out[-1]


def _gather_wait(send_sems, recv_sems, flying, lands, after, name):
    n = len(flying)

    def body(*refs):
        src, land = refs[:n], refs[n:2 * n]
        for cp in _direct_copies(src, land, refs[2 * n], refs[2 * n + 1]):
            cp.wait_send()
            cp.wait_recv()

    hbm = pl.BlockSpec(memory_space=pltpu.HBM)
    sem = pl.BlockSpec(memory_space=pltpu.SEMAPHORE)
    out = pl.pallas_call(
        body, name=name + "_wait",
        out_shape=tuple(pltpu.HBM(s.shape, s.dtype) for s in list(flying) + list(lands)),
        in_specs=[hbm] * (2 * n) + [sem, sem, pl.BlockSpec(memory_space=pl.ANY)], out_specs=tuple([hbm] * (2 * n)),
        input_output_aliases={a: a for a in range(2 * n)},
        compiler_params=pltpu.CompilerParams(has_side_effects=pltpu.SideEffectType.DATAFLOW_SIDE_EFFECTING),
    )(*flying, *lands, send_sems, recv_sems, after)
    return out[n:]


def _sibling_copies(src, land, send_sems, recv_sems, blocks):
    x, y, c, _ = _place()
    sibling = _peer(x, y, c, 1)
    return [pltpu.make_async_remote_copy(
        src_ref=src[a].at[2 * p + (1 - c)], dst_ref=land[a].at[p], send_sem=send_sems.at[a * 4 + p],
        recv_sem=recv_sems.at[a * 4 + p], device_id=sibling, device_id_type=MESH_ID)
        for a in range(len(src)) for p in blocks[a]]


def _exchange_sibling_start(by_dest, blocks):
    n = len(by_dest)

    def body(*refs):
        for cp in _sibling_copies(refs[:n], refs[n:2 * n], refs[2 * n], refs[2 * n + 1], blocks):
            cp.start()

    hbm = pl.BlockSpec(memory_space=pltpu.HBM)
    sem = pl.BlockSpec(memory_space=pltpu.SEMAPHORE)
    lands = [lax.empty((4,) + s.shape[1:], s.dtype) for s in by_dest]
    out = pl.pallas_call(
        body, name="exchange_sibling_start",
        out_shape=(pltpu.SemaphoreType.DMA((4 * n,)), pltpu.SemaphoreType.DMA((4 * n,)),
                   *[pltpu.HBM(s.shape, s.dtype) for s in by_dest], *[pltpu.HBM(s.shape, s.dtype) for s in lands]),
        in_specs=[hbm] * (2 * n), out_specs=(sem, sem, *[hbm] * (2 * n)),
        input_output_aliases={a: 2 + a for a in range(2 * n)},
        compiler_params=pltpu.CompilerParams(has_side_effects=pltpu.SideEffectType.DATAFLOW_SIDE_EFFECTING),
    )(*[pltpu.with_memory_space_constraint(s, pltpu.HBM) for s in list(by_dest) + lands])
    return out[0], out[1], out[2:2 + n], out[2 + n:]


def _exchange_sibling_wait(send_sems, recv_sems, flying, lands, blocks):
    n = len(flying)

    def body(*refs):
        for cp in _sibling_copies(refs[:n], refs[n:2 * n], refs[2 * n], refs[2 * n + 1], blocks):
            cp.wait_recv()
            cp.wait_send()

    hbm = pl.BlockSpec(memory_space=pltpu.HBM)
    sem = pl.BlockSpec(memory_space=pltpu.SEMAPHORE)
    out = pl.pallas_call(
        body, name="exchange_sibling_wait",
        out_shape=tuple(pltpu.HBM(s.shape, s.dtype) for s in list(flying) + list(lands)),
        in_specs=[hbm] * (2 * n) + [sem, sem], out_specs=tuple([hbm] * (2 * n)),
        input_output_aliases={a: a for a in range(2 * n)},
        compiler_params=pltpu.CompilerParams(has_side_effects=pltpu.SideEffectType.DATAFLOW_SIDE_EFFECTING),
    )(*[pltpu.with_memory_space_constraint(s, pltpu.HBM) for s in list(flying) + list(lands)], send_sems, recv_sems)
    return out[:n], out[n:]


def _chip_copies(src, land, send_sems, recv_sems):
    x, y, c, _ = _place()
    chip = 2 * x + y
    return [pltpu.make_async_remote_copy(
        src_ref=src[a].at[chip ^ (k >> 1)], dst_ref=land[a].at[j], send_sem=send_sems.at[a * 3 + j],
        recv_sem=recv_sems.at[a * 3 + j], device_id=_peer(x, y, c, k), device_id_type=MESH_ID)
        for j, k in enumerate(ICI_MASKS) for a in range(len(src))]


def _exchange_chips_start(by_chip):
    n = len(by_chip)

    def body(*refs):
        src, land = refs[:n], refs[n:2 * n]
        send_sems, recv_sems = refs[2 * n], refs[2 * n + 1]
        token_ref = refs[-1]
        for cp in _chip_copies(src, land, send_sems, recv_sems):
            cp.start()
        token_ref[...] = jnp.zeros_like(token_ref)

    hbm = pl.BlockSpec(memory_space=pltpu.HBM)
    sem = pl.BlockSpec(memory_space=pltpu.SEMAPHORE)
    lands = [lax.empty((3,) + s.shape[1:], s.dtype) for s in by_chip]
    out = pl.pallas_call(
        body, name="exchange_chips_start",
        out_shape=(pltpu.SemaphoreType.DMA((3 * n,)), pltpu.SemaphoreType.DMA((3 * n,)),
                   *[pltpu.HBM(s.shape, s.dtype) for s in by_chip], *[pltpu.HBM(s.shape, s.dtype) for s in lands],
                   jax.ShapeDtypeStruct((1, D), F32)),
        in_specs=[hbm] * (2 * n), out_specs=(sem, sem, *[hbm] * (2 * n), _whole_vmem()),
        input_output_aliases={a: 2 + a for a in range(2 * n)},
        compiler_params=pltpu.CompilerParams(has_side_effects=pltpu.SideEffectType.DATAFLOW_SIDE_EFFECTING),
    )(*[pltpu.with_memory_space_constraint(s, pltpu.HBM) for s in list(by_chip) + lands])
    return out[0], out[1], out[2:2 + n], out[2 + n:2 + 2 * n], out[-1]


def _exchange_chips_wait(send_sems, recv_sems, flying, lands, after):
    n = len(flying)

    def body(*refs):
        src, land = refs[:n], refs[n:2 * n]
        send_sems_ref, recv_sems_ref = refs[2 * n], refs[2 * n + 1]
        for cp in _chip_copies(src, land, send_sems_ref, recv_sems_ref):
            cp.wait_send()
            cp.wait_recv()

    hbm = pl.BlockSpec(memory_space=pltpu.HBM)
    sem = pl.BlockSpec(memory_space=pltpu.SEMAPHORE)
    out = pl.pallas_call(
        body, name="exchange_chips_wait",
        out_shape=tuple(pltpu.HBM(s.shape, s.dtype) for s in list(flying) + list(lands)),
        in_specs=[hbm] * (2 * n) + [sem, sem, pl.BlockSpec(memory_space=pl.ANY)], out_specs=tuple([hbm] * (2 * n)),
        input_output_aliases={a: a for a in range(2 * n)},
        compiler_params=pltpu.CompilerParams(has_side_effects=pltpu.SideEffectType.DATAFLOW_SIDE_EFFECTING),
    )(*flying, *lands, send_sems, recv_sems, after)
    return out[n:]


def _adamw_math(w, g, m, v):
    m = ADAM_B1 * m + (1.0 - ADAM_B1) * g
    v = ADAM_B2 * v + (1.0 - ADAM_B2) * (g * g)
    m_hat = m / (1.0 - ADAM_B1 ** ADAM_STEP)
    v_hat = v / (1.0 - ADAM_B2 ** ADAM_STEP)
    return -ADAM_LR * (m_hat / (jnp.sqrt(v_hat) + ADAM_EPS) + ADAM_WD * w), m, v


def _pair_sum(owns, recvs, place_arr, tr, name):
    n = len(owns)
    _, rows, cols = owns[0].shape

    def body(place_ref, *refs):
        for a in range(n):
            s = refs[a][...] + refs[n + a][...].astype(F32)
            refs[3 * n + a][...] = s.astype(BF16)

            @pl.when(pl.program_id(1) == place_ref[1])
            def _(a=a, s=s):
                refs[2 * n + a][...] = s

    by_chip = pl.BlockSpec((None, tr, cols), lambda i, p, place_ref: (p, i, 0))
    mine = pl.BlockSpec((None, tr, cols), lambda i, p, place_ref: (2 * p + place_ref[0], i, 0))
    kept = pl.BlockSpec((tr, cols), lambda i, p, place_ref: (i, 0))
    out = pl.pallas_call(
        body, name=name,
        grid_spec=pltpu.PrefetchScalarGridSpec(
            num_scalar_prefetch=1, grid=(rows // tr, 4), in_specs=[mine] * n + [by_chip] * n,
            out_specs=[kept] * n + [by_chip] * n),
        out_shape=[jax.ShapeDtypeStruct((rows, cols), F32)] * n + [jax.ShapeDtypeStruct((4, rows, cols), BF16)] * n,
        compiler_params=_params("parallel", "arbitrary"),
    )(place_arr, *owns, *recvs)
    return out[:n], out[n:]


def _chip_sum(pairs, recvs, tr, name, adam=None):
    n = len(pairs)
    rows, cols = pairs[0].shape
    n_state = 0 if adam is None else 3 * n

    def body(*refs):
        outs = refs[2 * n + n_state:]
        for a in range(n):
            g = refs[a][...]
            for j in range(3):
                g = g + refs[n + a][j].astype(F32)
            outs[a][...] = g
            if adam is not None:
                w_ref, m_ref, v_ref = (refs[2 * n + s * n + a] for s in range(3))
                outs[n + a][...], outs[2 * n + a][...], outs[3 * n + a][...] = _adamw_math(w_ref[...], g, m_ref[...], v_ref[...])

    blk = pl.BlockSpec((tr, cols), lambda i: (i, 0))
    n_out = n if adam is None else 4 * n
    out = pl.pallas_call(
        body, name=name, grid=(rows // tr,),
        in_specs=[blk] * n + [pl.BlockSpec((3, tr, cols), lambda i: (0, i, 0))] * n + [blk] * n_state,
        out_specs=[blk] * n_out,
        out_shape=[jax.ShapeDtypeStruct((rows, cols), F32)] * n_out,
        compiler_params=_params("parallel"),
    )(*pairs, *recvs, *([] if adam is None else [t for group in adam for t in group]))
    return out if adam is None else (out[:n], out[n:2 * n], out[2 * n:3 * n], out[3 * n:])


def _finish_small(small_all, me, ws, ms, vs):
    n = len(ws)

    def body(me_ref, s_ref, c_ref, *refs):
        g, gc = s_ref[0], c_ref[0]
        for d in range(1, N_DEV):
            g = g + s_ref[d]
            gc = gc + c_ref[d]
        refs[3 * n][...] = g[32:33, 0:1]
        grads = [g[0:1], g[8:9], g[16:17, 0:N_HEADS], gc[0:3]]
        for a in range(n):
            w_ref, m_ref, v_ref = (refs[s * n + a] for s in range(3))
            outs = [refs[(3 + s) * n + 1 + a] for s in range(4)]
            if len(w_ref.shape) == 3:
                for r in range(w_ref.shape[0]):
                    g_r = grads[a][r:r + 1]
                    outs[0][r] = g_r
                    outs[1][r], outs[2][r], outs[3][r] = _adamw_math(w_ref[r], g_r, m_ref[r], v_ref[r])
            else:
                outs[0][...] = grads[a]
                outs[1][...], outs[2][...], outs[3][...] = _adamw_math(w_ref[...], grads[a], m_ref[...], v_ref[...])

    def whole(shape):
        return pl.BlockSpec(shape, lambda i, me_ref: (0,) * len(shape))

    params = [whole(w.shape) for w in ws]
    out = pl.pallas_call(
        body, name="finish_small",
        grid_spec=pltpu.PrefetchScalarGridSpec(
            num_scalar_prefetch=1, grid=(1,),
            in_specs=[whole(small_all.shape), pl.BlockSpec((N_DEV, 8, SHARD_SQ), lambda i, me_ref: (0, 3, me_ref[0]))]
            + params * 3,
            out_specs=[whole((1, 1))] + params * 4),
        out_shape=[jax.ShapeDtypeStruct((1, 1), F32)] + [jax.ShapeDtypeStruct(w.shape, F32) for w in ws] * 4,
    )(me, small_all, small_all, *ws, *ms, *vs)
    return out[0], out[1:1 + n], out[1 + n:1 + 2 * n], out[1 + 2 * n:1 + 3 * n], out[1 + 3 * n:]


def _rope_tables():
    inv_freq = np.float32(ROPE_THETA) ** (-np.arange(0, HEAD_DIM, 2, dtype=np.float32) / np.float32(HEAD_DIM))
    ang = (np.arange(SEQ_LEN, dtype=np.float32)[:, None] * inv_freq.astype(np.float32)[None, :]).astype(np.float64)
    cos, sin = np.cos(ang).astype(np.float32), np.sin(ang).astype(np.float32)
    return jnp.asarray(np.tile(cos, (1, 4))), jnp.asarray(np.tile(np.concatenate([-sin, sin], axis=1), (1, 2)))


def _local_step(x, target, g_pre, g_post, sinks, wt, wconv, squares, start_exchange=None):
    cos_t, sin_t = _rope_tables()
    wconv8 = jnp.pad(wconv, ((0, 5), (0, 0)))
    h, q, kv, g3 = _fwd_in_attn(x, g_pre, wt, cos_t, sin_t, 512)
    wpc, wpa, wout = squares(kv)
    a4, ya = _fwd_in_conv(h, wt, wconv8, wpc, 512)
    attn, ub, lse = _fwd_attn(sinks, q, kv, g3, 4)
    loss8, dout, dya, dub, dgab, dwout, dwpa, dgpost8 = _fwd_out_bwd_head(ya, ub, g3, x, target, g_post, wpa, wout, 512)
    dq, dza, dkv_own, dkv_prev, dsink8, dh_part = _bwd_attn(sinks, q, kv, attn, lse, dub, g3, cos_t, sin_t, wt, 4)
    da4, dwpc, dwconv8, dwt = _bwd_conv(dya, a4, h, wconv8, wpc, 512, 2)
    dkv, dwt = _bwd_kv_finish(dkv_own, dkv_prev, cos_t, sin_t, h, dwt)
    dwt = _bwd_dw_in(h, [(dq, ROW_Q), (dza, ROW_ZA)], 1024, 1024, "bwd_dw_in_q_za", dwt)
    land, sent = None, None
    if start_exchange is not None:
        dwt, land, sent = start_exchange[0](dwt, dwpc, dwpa, dwout)
    dwt32, dwt16, *land = _bwd_dw_in(h, [(dgab, ROW_GA)], 1024, 1024, "bwd_dw_in_gates", dwt, land)
    token, pending = (None, None) if start_exchange is None else start_exchange[1](dwt32, dwt16, land[0], sent)
    g_pre_after = g_pre if token is None else g_pre + token
    grad_x, dgpre8 = _bwd_dh(da4, dh_part, dkv, dgab, wt, x, g_pre_after, dout, 512)
    small = jnp.concatenate([dgpre8, dgpost8, jnp.pad(dsink8, ((0, 0), (0, D - 128))), dwconv8,
                             jnp.pad(loss8, ((0, 0), (0, D - 128)))], axis=0)
    return loss8[0, 0], grad_x, dwt32, dwt16, dwpc, dwpa, dwout, small, pending


def kernel(x, g_pre, g_post, w_in, w_conv, sinks, w_proj_conv, w_proj_attn, w_out, loss_target, m_g_pre, m_g_post, m_w_in, m_w_conv, m_sinks, m_w_proj_conv, m_w_proj_attn, m_w_out, v_g_pre, v_g_post, v_w_in, v_w_conv, v_sinks, v_w_proj_conv, v_w_proj_attn, v_w_out):
    batch = x.shape[0]
    mx, my, mc, me = _place()
    place_arr = jnp.stack([mc, 2 * mx + my]).astype(jnp.int32)

    g_wt, g_conv = _all_gather([w_in[0].T.astype(BF16), jnp.pad(w_conv[0], ((0, 5), (0, 0)))])
    wt = g_wt.reshape(D_IN, D)
    wconv = g_conv[:, 0:3, :].transpose(1, 0, 2).reshape(3, D)
    sq_mine = [w.astype(BF16) for w in (w_proj_conv[0], w_proj_attn[0], w_out[0])]
    wt, sq_mine = lax.optimization_barrier((wt, sq_mine))
    sq_send, sq_recv, sq_flying, sq_lands, sq_token = _gather_start(sq_mine, "gather_squares")

    def squares(after):
        got = _gather_wait(sq_send, sq_recv, sq_flying, sq_lands, after, "gather_squares")
        return [lax.dynamic_update_index_in_dim(full, mine, me, 0).reshape(D, D) for full, mine in zip(got, sq_mine)]

    early = [(0, 1, 2)] + [(0, 1, 2, 3)] * 3

    def send_early(dwt, dwpc, dwpa, dwout):
        own_sq = [g.reshape(N_DEV, SHARD_SQ, D) for g in (dwpc, dwpa, dwout)]
        send_sems, recv_sems, flying, lands = _exchange_sibling_start([dwt[1].reshape(N_DEV, SHARD_IN, D)] + own_sq, early)
        return (dwt[0], flying[0].reshape(D_IN, D)), lands[0], (send_sems, recv_sems, flying[1:], lands[1:])

    def send_late(dwt32, dwt16, land, sent):
        own_in = dwt32.reshape(N_DEV, SHARD_IN, D)
        send_sems, recv_sems, sq_flying, sq_lands = sent
        dwt16 = dwt16.reshape(N_DEV, SHARD_IN, D)
        sent_arrays, from_sibling = _exchange_sibling_wait(
            send_sems, recv_sems, [dwt16] + list(sq_flying), [land] + list(sq_lands), early)
        own_sq = sent_arrays[1:]
        in32, in16 = _pair_sum([own_in], from_sibling[:1], place_arr, SHARD_IN // 2, "pair_sum_w_in")
        sq32, sq16 = _pair_sum(own_sq, from_sibling[1:], place_arr, SHARD_SQ, "pair_sum_squares")
        send_sems, recv_sems, flying, lands, token = _exchange_chips_start(list(in16) + list(sq16))
        return token, (send_sems, recv_sems, flying, lands, in32, sq32)

    _, grad_x, _, _, _, _, _, small, pending = _local_step(
        x.reshape(batch * SEQ_LEN, D), loss_target.reshape(batch * SEQ_LEN, D), g_pre + sq_token, g_post,
        sinks, wt, wconv, squares, (send_early, send_late))
    sm_send, sm_recv, sm_flying, sm_lands, sm_token = _gather_start([small], "gather_small")
    send_sems, recv_sems, flying, lands, in32, sq32 = pending
    from_chips = _exchange_chips_wait(send_sems, recv_sems, flying, lands, sm_token)

    o_in = [o[0].T for o in _chip_sum(
        in32, from_chips[:1], SHARD_IN // 6, "chip_sum_adamw_w_in",
        adam=([w_in[0].T], [m_w_in[0].T], [v_w_in[0].T]))]
    g_in_mine, o_in = o_in[0], o_in[1:]
    g_sq, d_sq, m_sq, v_sq = _chip_sum(
        sq32, from_chips[1:], SHARD_SQ, "chip_sum_adamw_squares",
        adam=([w_proj_conv[0], w_proj_attn[0], w_out[0]], [m_w_proj_conv[0], m_w_proj_attn[0], m_w_out[0]],
              [v_w_proj_conv[0], v_w_proj_attn[0], v_w_out[0]]))
    both_done, g_in_mine = lax.optimization_barrier((d_sq[0], g_in_mine))
    (small_all,) = _gather_wait(sm_send, sm_recv, sm_flying, sm_lands, both_done, "gather_small")
    def by_row(a):
        return a.transpose(1, 0, 2)

    loss, (g_g_pre, g_g_post, g_sinks, g_conv_mine), *o_small = _finish_small(
        lax.dynamic_update_index_in_dim(small_all, small, me, 0), jnp.reshape(me, (1,)).astype(jnp.int32),
        [g_pre, g_post, sinks, by_row(w_conv)], [m_g_pre, m_g_post, m_sinks, by_row(m_w_conv)],
        [v_g_pre, v_g_post, v_sinks, by_row(v_w_conv)])
    loss = loss.reshape(())

    grads = [g_g_pre, g_g_post, g_in_mine[None], by_row(g_conv_mine), g_sinks] + [g[None] for g in g_sq]
    rest = []
    for idx, sq in enumerate((d_sq, m_sq, v_sq)):
        gp, gq, sk, cv = o_small[idx]
        rest += [gp, gq, o_in[idx][None], by_row(cv), sk] + [s[None] for s in sq]
    return (loss, grad_x.reshape(batch, SEQ_LEN, D), *grads, *rest)
```

```python
import numpy as np
import jax
import jax.numpy as jnp
from jax import lax
from jax.experimental import pallas as pl
from jax.experimental.pallas import tpu as pltpu

D = 1024
N_HEADS = 16
HEAD_DIM = 64
LOGIT_SCALE = HEAD_DIM ** -0.5
BLK = 128
SEQ_LEN = 2048
D_IN = 8448
ROW_Q, ROW_KV, ROW_ZA, ROW_GA = 4 * D, 5 * D, 5 * D + 256, 6 * D + 256
SHARD_IN = D_IN // 8
SHARD_SQ = D // 8
N_DEV = 8
V7X_VMEM_BYTES = 64 << 20
ROPE_THETA = 10000.0
RMS_EPS = 1e-6
NEG = -1e30
ADAM_LR, ADAM_B1, ADAM_B2, ADAM_EPS, ADAM_WD, ADAM_STEP = 0.001, 0.9, 0.999, 1e-08, 0.01, 10

F32 = jnp.float32
BF16 = jnp.bfloat16
MESH_ID = pl.DeviceIdType.MESH


def _dot(a, b):
    return jnp.dot(a, b, preferred_element_type=F32)


def _dot_nt(a, b):
    return lax.dot_general(a, b, (((1,), (1,)), ((), ())), preferred_element_type=F32)


def _dot_tn(a, b):
    return lax.dot_general(a, b, (((0,), (0,)), ((), ())), preferred_element_type=F32)


def _sig(z):
    return 1.0 / (1.0 + jnp.exp(-z))


def _swap_halves(z):
    lane = lax.broadcasted_iota(jnp.int32, z.shape, 1)
    return jnp.where((lane & 63) < 32, pltpu.roll(z, 96, 1), pltpu.roll(z, 32, 1))


def _row_spec(tm, width, col=0):
    return pl.BlockSpec((tm, width), lambda i: (i, col))


def _whole_vmem():
    return pl.BlockSpec(memory_space=pltpu.VMEM)


def _params(*sem, vmem_limit_bytes=None):
    return pltpu.CompilerParams(dimension_semantics=sem, vmem_limit_bytes=vmem_limit_bytes)


def _fwd_in_attn(x, g_pre, wt, cos_t, sin_t, tm):
    t = x.shape[0]
    seq_tiles = SEQ_LEN // tm

    def body(x_ref, g_ref, w_ref, c_ref, s_ref, h_ref, q_ref, kv_ref, g3_ref):
        xf = x_ref[...]
        r = lax.rsqrt(jnp.mean(xf * xf, axis=-1, keepdims=True) + RMS_EPS)
        hh = ((xf * r) * g_ref[...]).astype(BF16)
        h_ref[...] = hh
        c = c_ref[...]
        s = s_ref[...]

        def rope(z):
            return z * c + _swap_halves(z) * s

        q = _dot_nt(hh, w_ref[ROW_Q:ROW_Q + D, :])
        for j in range(D // 128):
            q_ref[:, j * 128:(j + 1) * 128] = (rope(q[:, j * 128:(j + 1) * 128]) * LOGIT_SCALE).astype(BF16)
        kv = _dot_nt(hh, w_ref[ROW_KV:ROW_KV + 256, :])
        kv_ref[:, 0:128] = rope(kv[:, 0:128]).astype(BF16)
        kv_ref[:, 128:256] = kv[:, 128:256].astype(BF16)
        for j in range(3):
            g3_ref[:, j * D:(j + 1) * D] = _dot_nt(hh, w_ref[ROW_ZA + j * D:ROW_ZA + (j + 1) * D, :])

    tab = pl.BlockSpec((tm, 128), lambda i: (i % seq_tiles, 0))
    return pl.pallas_call(
        body, name="fwd_in_attn", grid=(t // tm,),
        in_specs=[_row_spec(tm, D), pl.BlockSpec((1, D), lambda i: (0, 0)), _whole_vmem(), tab, tab],
        out_specs=[_row_spec(tm, D), _row_spec(tm, D), _row_spec(tm, 256), _row_spec(tm, 3 * D)],
        out_shape=[jax.ShapeDtypeStruct((t, D), BF16), jax.ShapeDtypeStruct((t, D), BF16),
                   jax.ShapeDtypeStruct((t, 256), BF16), jax.ShapeDtypeStruct((t, 3 * D), F32)],
        compiler_params=_params("parallel"),
    )(x, g_pre, wt, cos_t, sin_t)


def _conv_forward(xc, bg, cg, zc, up6, up7, w_ref):
    rows = lax.broadcasted_iota(jnp.int32, xc.shape, 0)
    u = cg * xc
    u_m1 = jnp.where(rows == 0, up7, pltpu.roll(u, 1, 0))
    u_m2 = jnp.where(rows == 0, up6, jnp.where(rows == 1, up7, pltpu.roll(u, 2, 0)))
    yconv = w_ref[0:1, :] * u_m2 + w_ref[1:2, :] * u_m1 + w_ref[2:3, :] * u
    sg = _sig(zc)
    sz = zc * sg
    co = bg * yconv
    return u, u_m1, u_m2, yconv, sg, sz, co


def _fwd_in_conv(h, wt, wconv8, wpc, tm):
    t = h.shape[0]
    seq_tiles = SEQ_LEN // tm

    def body(h_ref, w_ref, wc_ref, wpc_ref, a4_ref, ya_ref, last_u_ref):
        hh = h_ref[...]
        xc, bg, cg, zc = (_dot_nt(hh, w_ref[j * D:(j + 1) * D, :]) for j in range(4))
        for j, z in enumerate((xc, bg, cg, zc)):
            a4_ref[:, j * D:(j + 1) * D] = z.astype(BF16)
        first = pl.program_id(0) % seq_tiles == 0
        up6 = jnp.where(first, 0.0, last_u_ref[6:7, :])
        up7 = jnp.where(first, 0.0, last_u_ref[7:8, :])
        u, _, _, _, _, sz, co = _conv_forward(xc, bg, cg, zc, up6, up7, wc_ref)
        last_u_ref[...] = u[tm - 8:tm, :]
        ya_ref[...] = _dot((sz * co).astype(BF16), wpc_ref[...])

    return pl.pallas_call(
        body, name="fwd_in_conv", grid=(t // tm,),
        in_specs=[_row_spec(tm, D), _whole_vmem(), pl.BlockSpec((8, D), lambda i: (0, 0)), _whole_vmem()],
        out_specs=[_row_spec(tm, 4 * D), _row_spec(tm, D)],
        out_shape=[jax.ShapeDtypeStruct((t, 4 * D), BF16), jax.ShapeDtypeStruct((t, D), F32)],
        scratch_shapes=[pltpu.VMEM((8, D), F32)],
        compiler_params=_params("arbitrary"),
    )(h, wt, wconv8, wpc)


STACK = 4 * BLK


def _band_mask(first):
    qi = lax.broadcasted_iota(jnp.int32, (STACK, 2 * BLK), 0) & (BLK - 1)
    kj = lax.broadcasted_iota(jnp.int32, (STACK, 2 * BLK), 1)
    return (kj > qi) & (kj <= qi + BLK) & (kj >= jnp.where(first, BLK, 0))


def _masked_fill(sink_ref, g, e):
    kj = lax.broadcasted_iota(jnp.int32, (STACK, 2 * BLK), 1)
    sink = jnp.concatenate([jnp.full((BLK, 2 * BLK), sink_ref[0, 2 * (4 * g + jj) + e], F32) for jj in range(4)], axis=0)
    return jnp.where(kj == 0, sink, NEG)


def _padded_pair(before, own):
    z = jnp.concatenate([before, own], axis=0).astype(F32)
    z = jnp.where(lax.broadcasted_iota(jnp.int32, z.shape, 0) == 0, 0.0, z)
    zs = pltpu.roll(z, 64, 1)
    lo = lax.broadcasted_iota(jnp.int32, z.shape, 1) < 64
    zero = jnp.zeros_like(z)
    left = [jnp.where(lo, z, zero).astype(BF16), jnp.where(lo, zs, zero).astype(BF16)]
    right = [jnp.where(lo, zero, zs).astype(BF16), jnp.where(lo, zero, z).astype(BF16)]
    return left, right


def _exp_logits(s, valid, fill):
    s = jnp.where(valid, s, fill)
    m = jnp.max(s, axis=-1, keepdims=True)
    return jnp.exp(s - m), m


def _kv_blocks(kvc_ref, kvp_ref, b, col):
    own = kvc_ref[b * BLK:(b + 1) * BLK, col:col + 128]
    before = kvp_ref[:, col:col + 128] if b == 0 else kvc_ref[(b - 1) * BLK:b * BLK, col:col + 128]
    return before, own


def _fwd_attn(sinks, q, kv, g3, blocks):
    t = q.shape[0]
    tq = blocks * BLK
    seq_blocks = SEQ_LEN // BLK

    def body(sink_ref, q_ref, kvc_ref, kvp_ref, za_ref, attn_ref, ub_ref, lse_ref):
        lo = lax.broadcasted_iota(jnp.int32, (STACK, 128), 1) < 64
        for b in range(blocks):
            rows = slice(b * BLK, (b + 1) * BLK)
            valid = _band_mask((pl.program_id(0) * blocks + b) % seq_blocks == 0)
            k_pad = _padded_pair(*_kv_blocks(kvc_ref, kvp_ref, b, 0))
            v_pad = _padded_pair(*_kv_blocks(kvc_ref, kvp_ref, b, 128))
            for g in range(2):
                qg = jnp.concatenate([q_ref[rows, j * 128:(j + 1) * 128] for j in range(4 * g, 4 * g + 4)], axis=0)
                pv, den = [], []
                for e in range(2):
                    p, m = _exp_logits(_dot_nt(qg, k_pad[e][g]), valid, _masked_fill(sink_ref, g, e))
                    both = _dot(p.astype(BF16), jnp.concatenate([v_pad[e][g], jnp.ones((2 * BLK, 128), BF16)], axis=1))
                    pv.append(both[:, 0:128])
                    den.append(both[:, 128:256])
                    lse_ref[b, 2 * g + e] = m + jnp.log(den[e])
                o = jnp.where(lo, pv[0] / den[0], pv[1] / den[1])
                for jj in range(4):
                    cols = slice((4 * g + jj) * 128, (4 * g + jj + 1) * 128)
                    oj = o[jj * BLK:(jj + 1) * BLK, :]
                    attn_ref[rows, cols] = oj
                    za = za_ref[rows, cols]
                    ub_ref[rows, cols] = (za * _sig(za) * oj).astype(BF16)

    return pl.pallas_call(
        body, name="fwd_attn", grid=(t // tq,),
        in_specs=[pl.BlockSpec(memory_space=pltpu.SMEM), _row_spec(tq, D), _row_spec(tq, 256),
                  pl.BlockSpec((BLK, 256), lambda i: (jnp.maximum(i * blocks - 1, 0), 0)), _row_spec(tq, D, 0)],
        out_specs=[_row_spec(tq, D), _row_spec(tq, D), pl.BlockSpec((blocks, 4, STACK, 128), lambda i: (i, 0, 0, 0))],
        out_shape=[jax.ShapeDtypeStruct((t, D), F32), jax.ShapeDtypeStruct((t, D), BF16),
                   jax.ShapeDtypeStruct((t // BLK, 4, STACK, 128), F32)],
        compiler_params=_params("parallel"),
    )(sinks, q, kv, kv, g3)


def _fwd_out_bwd_head(ya, ub, g3, x, target, g_post, wpa, wout, tm):
    t = x.shape[0]

    def body(ya_ref, ub_ref, ga_ref, gb_ref, x_ref, tgt_ref, gp_ref, wpa_ref, wout_ref,
             loss_ref, dout_ref, dya_ref, dub_ref, dgab_ref, dwout_ref, dwpa_ref, dgp_ref):
        @pl.when(pl.program_id(0) == 0)
        def _():
            loss_ref[...] = jnp.zeros_like(loss_ref)
            dwout_ref[...] = jnp.zeros_like(dwout_ref)
            dwpa_ref[...] = jnp.zeros_like(dwpa_ref)
            dgp_ref[...] = jnp.zeros_like(dgp_ref)

        g = gp_ref[...]
        halves = (slice(0, tm // 2), slice(tm // 2, tm))

        def stage1(rows):
            return _dot(ub_ref[rows, :], wpa_ref[...])

        def stage2(rows, yb):
            sa = _sig(ga_ref[rows, :])
            sb = _sig(gb_ref[rows, :])
            mb = (sa * ya_ref[rows, :] + sb * yb).astype(BF16)
            return sa, sb, mb, _dot(mb, wout_ref[...])

        def stage3(rows, y):
            r = lax.rsqrt(jnp.mean(y * y, axis=-1, keepdims=True) + RMS_EPS)
            n = y * r
            err = (x_ref[rows, :] + n * g) - tgt_ref[rows, :]
            sq = jnp.sum(jnp.sum(err * err, axis=0, keepdims=True), axis=1, keepdims=True)
            dout = err * (1.0 / D)
            dout_ref[rows, :] = dout
            dgp = jnp.sum(dout * n, axis=0, keepdims=True)
            dn = dout * g
            dy = (r * (dn - n * jnp.mean(dn * n, axis=-1, keepdims=True))).astype(BF16)
            return sq, dgp, dy, _dot_nt(dy, wout_ref[...])

        def stage4(rows, dm, sa, sb, yb):
            dya_ref[rows, :] = (dm * sa).astype(BF16)
            dyb = (dm * sb).astype(BF16)
            dgab_ref[rows, 0:D] = (dm * ya_ref[rows, :] * (sa * (1.0 - sa))).astype(BF16)
            dgab_ref[rows, D:2 * D] = (dm * yb * (sb * (1.0 - sb))).astype(BF16)
            dub_ref[rows, :] = _dot_nt(dyb, wpa_ref[...])
            return dyb

        yb = [stage1(rows) for rows in halves]
        s2 = [stage2(rows, yb[k]) for k, rows in enumerate(halves)]
        s3 = [stage3(rows, s2[k][3]) for k, rows in enumerate(halves)]
        dyb = [stage4(rows, s3[k][3], s2[k][0], s2[k][1], yb[k]) for k, rows in enumerate(halves)]
        loss_ref[...] += sum(s[0] for s in s3) * (0.5 / D)
        dgp_ref[0:1, :] += sum(s[1] for s in s3)
        dwout_ref[...] += _dot_tn(jnp.concatenate([s[2] for s in s2], axis=0), jnp.concatenate([s[2] for s in s3], axis=0))
        dwpa_ref[...] += _dot_tn(ub_ref[...], jnp.concatenate(dyb, axis=0))

    return pl.pallas_call(
        body, name="fwd_out_bwd_head", grid=(t // tm,),
        in_specs=[_row_spec(tm, D), _row_spec(tm, D), _row_spec(tm, D, 1), _row_spec(tm, D, 2),
                  _row_spec(tm, D), _row_spec(tm, D), pl.BlockSpec((1, D), lambda i: (0, 0)),
                  _whole_vmem(), _whole_vmem()],
        out_specs=[pl.BlockSpec((8, 128), lambda i: (0, 0)), _row_spec(tm, D), _row_spec(tm, D), _row_spec(tm, D),
                   _row_spec(tm, 2 * D), _whole_vmem(), _whole_vmem(), pl.BlockSpec((8, D), lambda i: (0, 0))],
        out_shape=[jax.ShapeDtypeStruct((8, 128), F32), jax.ShapeDtypeStruct((t, D), F32),
                   jax.ShapeDtypeStruct((t, D), BF16), jax.ShapeDtypeStruct((t, D), F32),
                   jax.ShapeDtypeStruct((t, 2 * D), BF16), jax.ShapeDtypeStruct((D, D), F32),
                   jax.ShapeDtypeStruct((D, D), F32), jax.ShapeDtypeStruct((8, D), F32)],
        compiler_params=_params("arbitrary", vmem_limit_bytes=V7X_VMEM_BYTES - (2 << 20)),
    )(ya, ub, g3, g3, x, target, g_post, wpa, wout)


def _bwd_attn(sinks, q, kv, attn, lse, dub, g3, cos_t, sin_t, wt, blocks):
    t = q.shape[0]
    tq = blocks * BLK
    seq_blocks = SEQ_LEN // BLK

    def body(sink_ref, q_ref, kvc_ref, kvp_ref, attn_ref, lse_ref, dub_ref, za_ref, c_ref, s_ref, w_ref,
             dq_ref, dza_ref, dkv_own_ref, dkv_prev_ref, dsink_ref, dh_ref):
        @pl.when(pl.program_id(0) == 0)
        def _():
            dsink_ref[...] = jnp.zeros_like(dsink_ref)

        lo = lax.broadcasted_iota(jnp.int32, (STACK, 128), 1) < 64
        lane8 = lax.broadcasted_iota(jnp.int32, (8, 128), 1)
        lo2 = lax.broadcasted_iota(jnp.int32, (2 * BLK, 128), 1) < 64
        sink_row = lax.broadcasted_iota(jnp.int32, (2 * BLK, 128), 0) == 0
        dsink = jnp.zeros((8, 128), F32)
        for b in range(blocks):
            rows = slice(b * BLK, (b + 1) * BLK)
            valid = _band_mask((pl.program_id(0) * blocks + b) % seq_blocks == 0)
            k_pad = _padded_pair(*_kv_blocks(kvc_ref, kvp_ref, b, 0))
            v_pad = _padded_pair(*_kv_blocks(kvc_ref, kvp_ref, b, 128))
            c = c_ref[rows, :]
            s = s_ref[rows, :]
            dk_acc, dv_acc = [], []
            for g in range(2):
                qg, dog = [], []
                for j in range(4 * g, 4 * g + 4):
                    cols = slice(j * 128, (j + 1) * 128)
                    za = za_ref[rows, cols]
                    sg = _sig(za)
                    dub = dub_ref[rows, cols]
                    dza_ref[rows, cols] = (dub * attn_ref[rows, cols] * (sg * (1.0 + za * (1.0 - sg)))).astype(BF16)
                    dog.append((dub * (za * sg)).astype(BF16))
                    qg.append(q_ref[rows, cols])
                qg = jnp.concatenate(qg, axis=0)
                dog = jnp.concatenate(dog, axis=0)
                dq = jnp.zeros((STACK, 128), F32)
                ds_both, p_both = [], []
                for e in range(2):
                    s_masked = jnp.where(valid, _dot_nt(qg, k_pad[e][g]), _masked_fill(sink_ref, g, e))
                    lse_rows = lse_ref[b, 2 * g + e]
                    p = jnp.exp(s_masked - jnp.concatenate([lse_rows, lse_rows], axis=1))
                    dp = _dot_nt(dog, v_pad[e][g])
                    ds = p * (dp - jnp.sum(p * dp, axis=-1, keepdims=True))
                    for jj in range(4):
                        tot = jnp.sum(ds[jj * BLK:(jj + 1) * BLK, 0:1], axis=0, keepdims=True)
                        dsink = dsink + jnp.where(lane8 == 2 * (4 * g + jj) + e, tot, 0.0)
                    ds = ds.astype(BF16)
                    dq = dq + _dot(ds, k_pad[e][g])
                    ds_both.append(ds)
                    p_both.append(p.astype(BF16))
                zero = jnp.zeros_like(qg)
                q2 = jnp.concatenate([jnp.where(lo, qg, zero), jnp.where(lo, zero, qg)], axis=0)
                do2 = jnp.concatenate([jnp.where(lo, dog, zero), jnp.where(lo, zero, dog)], axis=0)
                dk_acc.append(_dot_tn(q2, jnp.concatenate(ds_both, axis=0)).T)
                dv_acc.append(_dot_tn(do2, jnp.concatenate(p_both, axis=0)).T)
                for jj in range(4):
                    cols = slice((4 * g + jj) * 128, (4 * g + jj + 1) * 128)
                    dqj = dq[jj * BLK:(jj + 1) * BLK, :] * LOGIT_SCALE
                    dq_ref[rows, cols] = (dqj * c - _swap_halves(dqj) * s).astype(BF16)
            for col, acc in ((0, dk_acc), (128, dv_acc)):
                both = jnp.where(lo2, acc[0] + pltpu.roll(acc[0], 64, 1), acc[1] + pltpu.roll(acc[1], 64, 1))
                both = jnp.where(sink_row, 0.0, both)
                dkv_prev_ref[rows, col:col + 128] = both[0:BLK, :]
                dkv_own_ref[rows, col:col + 128] = both[BLK:2 * BLK, :]
        dsink_ref[...] += dsink
        dh_ref[...] = _dot(dq_ref[...], w_ref[ROW_Q:ROW_KV, :]) + _dot(dza_ref[...], w_ref[ROW_ZA:ROW_GA, :])

    tab = pl.BlockSpec((tq, 128), lambda i: (i % (SEQ_LEN // tq), 0))
    return pl.pallas_call(
        body, name="bwd_attn", grid=(t // tq,),
        in_specs=[pl.BlockSpec(memory_space=pltpu.SMEM), _row_spec(tq, D), _row_spec(tq, 256),
                  pl.BlockSpec((BLK, 256), lambda i: (jnp.maximum(i * blocks - 1, 0), 0)),
                  _row_spec(tq, D), pl.BlockSpec((blocks, 4, STACK, 128), lambda i: (i, 0, 0, 0)),
                  _row_spec(tq, D), _row_spec(tq, D, 0), tab, tab, _whole_vmem()],
        out_specs=[_row_spec(tq, D), _row_spec(tq, D), _row_spec(tq, 256), _row_spec(tq, 256),
                   pl.BlockSpec((8, 128), lambda i: (0, 0)), _row_spec(tq, D)],
        out_shape=[jax.ShapeDtypeStruct((t, D), BF16), jax.ShapeDtypeStruct((t, D), BF16),
                   jax.ShapeDtypeStruct((t, 256), F32), jax.ShapeDtypeStruct((t, 256), F32),
                   jax.ShapeDtypeStruct((8, 128), F32), jax.ShapeDtypeStruct((t, D), F32)],
        compiler_params=_params("arbitrary"),
    )(sinks, q, kv, kv, attn, lse, dub, g3, cos_t, sin_t, wt)


def _bwd_kv_finish(dkv_own, dkv_prev, cos_t, sin_t, h, prev):
    t = dkv_own.shape[0]
    tm = SEQ_LEN
    n_t = t // tm
    seq_tiles = SEQ_LEN // tm
    n_blocks = t // BLK

    def body(own_ref, same_ref, nxt_ref, c_ref, s_ref, h_ref, o32_in, o16_in, out_ref, o32_ref, o16_ref,
             acc_ref, acc16_ref, sems):
        step = pl.program_id(0)

        @pl.when(step == 0)
        def _():
            acc_ref[...] = jnp.zeros_like(acc_ref)

        keep = jnp.where(step % seq_tiles == seq_tiles - 1, 0.0, 1.0)
        shifted = jnp.concatenate([same_ref[BLK:tm, :], nxt_ref[...] * keep], axis=0)
        tot = own_ref[...] + shifted
        dk = tot[:, 0:128]
        out_ref[:, 0:128] = (dk * c_ref[...] - _swap_halves(dk) * s_ref[...]).astype(BF16)
        out_ref[:, 128:256] = tot[:, 128:256].astype(BF16)
        acc_ref[...] += _dot_tn(out_ref[...], h_ref[...])

        @pl.when(step == n_t - 1)
        def _():
            acc16_ref[...] = acc_ref[...].astype(BF16)
            rows = pl.ds(ROW_KV, 256)
            c32 = pltpu.make_async_copy(acc_ref, o32_ref.at[rows], sems.at[0])
            c16 = pltpu.make_async_copy(acc16_ref, o16_ref.at[rows], sems.at[1])
            c32.start()
            c16.start()
            c32.wait()
            c16.wait()

    tab = pl.BlockSpec((tm, 128), lambda i: (i % seq_tiles, 0))
    hbm = pl.BlockSpec(memory_space=pl.ANY)
    out = pl.pallas_call(
        body, name="bwd_kv_finish", grid=(n_t,),
        in_specs=[_row_spec(tm, 256), _row_spec(tm, 256),
                  pl.BlockSpec((BLK, 256), lambda i: (jnp.minimum((i + 1) * (tm // BLK), n_blocks - 1), 0)), tab, tab,
                  _row_spec(tm, D), hbm, hbm],
        out_specs=[_row_spec(tm, 256), hbm, hbm],
        out_shape=[jax.ShapeDtypeStruct((t, 256), BF16), jax.ShapeDtypeStruct((D_IN, D), F32),
                   jax.ShapeDtypeStruct((D_IN, D), BF16)],
        scratch_shapes=[pltpu.VMEM((256, D), F32), pltpu.VMEM((256, D), BF16), pltpu.SemaphoreType.DMA((2,))],
        input_output_aliases={6: 1, 7: 2},
        compiler_params=_params("arbitrary"),
    )(dkv_own, dkv_prev, dkv_prev, cos_t, sin_t, h, *prev)
    return out[0], (out[1], out[2])


STAGE_ROWS = 256


def _bwd_conv(dya, a4, h, wconv8, wpc, tm, parts):
    t = a4.shape[0]
    n_t = t // tm
    sub = tm // parts
    seq_tiles = SEQ_LEN // tm

    def body(dya_ref, xc_ref, bg_ref, cg_ref, zc_ref, xcp_ref, cgp_ref, w_ref, wpc_ref, h_ref,
             da4_ref, dwpc_ref, dwc_ref, o32_ref, o16_ref, acc_ref, stage_ref, later_ref, sems):
        step = pl.program_id(0)
        tile = n_t - 1 - step

        @pl.when(step == 0)
        def _():
            dwpc_ref[...] = jnp.zeros_like(dwpc_ref)
            dwc_ref[...] = jnp.zeros_like(dwc_ref)
            acc_ref[...] = jnp.zeros_like(acc_ref)

        keep_prev = jnp.where(tile % seq_tiles == 0, 0.0, 1.0)
        ends_sequence = tile % seq_tiles == seq_tiles - 1

        def part(p, later):
            r0 = p * sub
            here = slice(r0, r0 + sub)
            if p == 0:
                u_prev = cgp_ref[14:16, :].astype(F32) * xcp_ref[14:16, :].astype(F32) * keep_prev
            else:
                u_prev = cg_ref[r0 - 2:r0, :].astype(F32) * xc_ref[r0 - 2:r0, :].astype(F32)
            xc = xc_ref[here, :].astype(F32)
            bg = bg_ref[here, :].astype(F32)
            cg = cg_ref[here, :].astype(F32)
            zc = zc_ref[here, :].astype(F32)
            u, u_m1, u_m2, yconv, sg, sz, co = _conv_forward(xc, bg, cg, zc, u_prev[0:1, :], u_prev[1:2, :], w_ref)
            ua = (sz * co).astype(BF16)
            dua = _dot_nt(dya_ref[here, :], wpc_ref[...])
            da4_ref[here, 3 * D:4 * D] = (dua * co * (sg * (1.0 + zc * (1.0 - sg)))).astype(BF16)
            dco = dua * sz
            da4_ref[here, D:2 * D] = (dco * yconv).astype(BF16)
            dyc = dco * bg
            dwc = jnp.concatenate([jnp.sum(dyc * s, axis=0, keepdims=True) for s in (u_m2, u_m1, u)], axis=0)
            rows = lax.broadcasted_iota(jnp.int32, xc.shape, 0)
            n0 = later[0:1, :]
            n1 = later[1:2, :]
            dyc_p1 = jnp.where(rows == sub - 1, n0, pltpu.roll(dyc, sub - 1, 0))
            dyc_p2 = jnp.where(rows == sub - 2, n0, jnp.where(rows == sub - 1, n1, pltpu.roll(dyc, sub - 2, 0)))
            du = w_ref[2:3, :] * dyc + w_ref[1:2, :] * dyc_p1 + w_ref[0:1, :] * dyc_p2
            da4_ref[here, 0:D] = (du * cg).astype(BF16)
            da4_ref[here, 2 * D:3 * D] = (du * xc).astype(BF16)
            return ua, dwc, dyc[0:8, :]

        later = jnp.where(ends_sequence, 0.0, later_ref[...])
        uas, dwc = [], jnp.zeros((3, D), F32)
        for p in reversed(range(parts)):
            ua, dwc_p, later = part(p, later)
            uas.insert(0, ua)
            dwc = dwc + dwc_p
        later_ref[...] = later
        dwpc_ref[...] += _dot_tn(jnp.concatenate(uas, axis=0), dya_ref[...])
        dwc_ref[0:3, :] += dwc
        for j in range(4):
            acc_ref[j * D:(j + 1) * D, :] += _dot_tn(da4_ref[:, j * D:(j + 1) * D], h_ref[...])

        @pl.when(step == n_t - 1)
        def _():
            c32 = pltpu.make_async_copy(acc_ref, o32_ref.at[pl.ds(0, 4 * D)], sems.at[0])
            c32.start()
            for j in range(4 * D // STAGE_ROWS):
                rows = pl.ds(j * STAGE_ROWS, STAGE_ROWS)
                stage_ref[...] = acc_ref[rows, :].astype(BF16)
                c16 = pltpu.make_async_copy(stage_ref, o16_ref.at[rows], sems.at[1])
                c16.start()
                c16.wait()
            c32.wait()

    def rows_of_tile(width, col=0):
        return pl.BlockSpec((tm, width), lambda s: (n_t - 1 - s, col))

    def prev(col):
        return pl.BlockSpec((16, D), lambda s: (jnp.maximum((n_t - 1 - s) * (tm // 16) - 1, 0), col))

    hbm = pl.BlockSpec(memory_space=pl.ANY)
    out = pl.pallas_call(
        body, name="bwd_conv", grid=(n_t,),
        in_specs=[rows_of_tile(D), rows_of_tile(D, 0), rows_of_tile(D, 1), rows_of_tile(D, 2), rows_of_tile(D, 3),
                  prev(0), prev(2), pl.BlockSpec((8, D), lambda s: (0, 0)), _whole_vmem(), rows_of_tile(D)],
        out_specs=[rows_of_tile(4 * D), _whole_vmem(), pl.BlockSpec((8, D), lambda s: (0, 0)), hbm, hbm],
        out_shape=[jax.ShapeDtypeStruct((t, 4 * D), BF16), jax.ShapeDtypeStruct((D, D), F32),
                   jax.ShapeDtypeStruct((8, D), F32), jax.ShapeDtypeStruct((D_IN, D), F32),
                   jax.ShapeDtypeStruct((D_IN, D), BF16)],
        scratch_shapes=[pltpu.VMEM((4 * D, D), F32), pltpu.VMEM((STAGE_ROWS, D), BF16), pltpu.VMEM((8, D), F32),
                        pltpu.SemaphoreType.DMA((2,))],
        compiler_params=pltpu.CompilerParams(dimension_semantics=("arbitrary",), vmem_limit_bytes=V7X_VMEM_BYTES - (2 << 20)),
    )(dya, a4, a4, a4, a4, a4, a4, wconv8, wpc, h)
    return out[0], out[1], out[2], (out[3], out[4])


def _bwd_dh(da4, dh_part, dkv, dgab, wt, x, g_pre, dout, tm):
    t = x.shape[0]

    def body(da4_ref, dhp_ref, dkv_ref, dgab_ref, w_ref, x_ref, g_ref, dout_ref, gx_ref, dg_ref):
        @pl.when(pl.program_id(0) == 0)
        def _():
            dg_ref[...] = jnp.zeros_like(dg_ref)

        dh = dhp_ref[...] + _dot(da4_ref[...], w_ref[0:ROW_Q, :])
        dh += _dot(dkv_ref[...], w_ref[ROW_KV:ROW_ZA, :])
        dh += _dot(dgab_ref[...], w_ref[ROW_GA:D_IN, :])
        xf = x_ref[...]
        r = lax.rsqrt(jnp.mean(xf * xf, axis=-1, keepdims=True) + RMS_EPS)
        xn = xf * r
        dg_ref[0:1, :] += jnp.sum(dh * xn, axis=0, keepdims=True)
        dxn = dh * g_ref[...]
        gx_ref[...] = dout_ref[...] + r * (dxn - xn * jnp.mean(dxn * xn, axis=-1, keepdims=True))

    return pl.pallas_call(
        body, name="bwd_dh", grid=(t // tm,),
        in_specs=[_row_spec(tm, 4 * D), _row_spec(tm, D), _row_spec(tm, 256), _row_spec(tm, 2 * D),
                  _whole_vmem(), _row_spec(tm, D), pl.BlockSpec((1, D), lambda i: (0, 0)), _row_spec(tm, D)],
        out_specs=[_row_spec(tm, D), pl.BlockSpec((8, D), lambda i: (0, 0))],
        out_shape=[jax.ShapeDtypeStruct((t, D), F32), jax.ShapeDtypeStruct((8, D), F32)],
        compiler_params=_params("arbitrary"),
    )(da4, dh_part, dkv, dgab, wt, x, g_pre, dout)


def _bwd_dw_in(h, pieces, nb, tm, name, prev, land=None):
    n_t = h.shape[0] // tm
    n_a = len(pieces)
    jobs = [(a, row0 + b * nb) for a, (arr, row0) in enumerate(pieces) for b in range(arr.shape[1] // nb)]
    first = [min(k for k, (a, _) in enumerate(jobs) if a == b) for b in range(n_a)]
    n_j = len(jobs)

    def body(*refs):
        h_ref, p_refs = refs[0], refs[1:1 + n_a]
        if land is None:
            o32_ref, o16_ref, acc_ref, acc16_ref, sems = refs[-5:]
        else:
            o32_ref, o16_ref, land_ref, acc_ref, acc16_ref, sems, far_sems = refs[-7:]
        j, i = pl.program_id(0), pl.program_id(1)

        def copies(k):
            rows = pl.ds(jobs[k][1], nb)
            return (pltpu.make_async_copy(acc_ref.at[k], o32_ref.at[rows], sems.at[0, k]),
                    pltpu.make_async_copy(acc16_ref.at[k], o16_ref.at[rows], sems.at[1, k]))

        for k, (a, _) in enumerate(jobs):
            @pl.when(j == k)
            def _(k=k, a=a):
                @pl.when(i == 0)
                def _():
                    acc_ref[k] = jnp.zeros((nb, D), F32)

                acc_ref[k] += _dot_tn(p_refs[a][...], h_ref[...])

                @pl.when(i == n_t - 1)
                def _():
                    acc16_ref[k] = acc_ref[k].astype(BF16)
                    for cp in copies(k):
                        cp.start()

        @pl.when((j == n_j - 1) & (i == n_t - 1))
        def _():
            for k in range(n_j):
                for cp in copies(k):
                    cp.wait()
            if land is not None:
                x, y, c, _ = _place()
                far = pltpu.make_async_remote_copy(
                    src_ref=o16_ref.at[pl.ds(pl.multiple_of((7 - c) * SHARD_IN, 16), SHARD_IN)], dst_ref=land_ref.at[3],
                    send_sem=far_sems.at[0], recv_sem=far_sems.at[1], device_id=_peer(x, y, c, 1),
                    device_id_type=MESH_ID)
                far.start()
                far.wait_recv()
                far.wait_send()

    def piece_spec(a):
        s, e = first[a], first[a] + pieces[a][0].shape[1] // nb
        return pl.BlockSpec((tm, nb), lambda j, i: (jnp.where(j < s, 0, jnp.where(j >= e, n_t - 1, i)),
                                                    jnp.clip(j - s, 0, e - s - 1)))

    hbm = pl.BlockSpec(memory_space=pl.ANY)
    carried = list(prev) + ([] if land is None else [land])
    return pl.pallas_call(
        body, name=name, grid=(n_j, n_t),
        in_specs=[pl.BlockSpec((tm, D), lambda j, i: (i, 0))] + [piece_spec(a) for a in range(n_a)]
        + [hbm] * len(carried),
        out_specs=[hbm] * len(carried),
        out_shape=[jax.ShapeDtypeStruct((D_IN, D), F32), jax.ShapeDtypeStruct((D_IN, D), BF16)]
        + ([] if land is None else [jax.ShapeDtypeStruct(land.shape, land.dtype)]),
        scratch_shapes=[pltpu.VMEM((n_j, nb, D), F32), pltpu.VMEM((n_j, nb, D), BF16),
                        pltpu.SemaphoreType.DMA((2, n_j))] + ([] if land is None else [pltpu.SemaphoreType.DMA((2,))]),
        input_output_aliases={1 + n_a + a: a for a in range(len(carried))},
        compiler_params=_params("arbitrary", "arbitrary", vmem_limit_bytes=48 << 20),
    )(h, *[arr for arr, _ in pieces], *carried)


def _place():
    x, y, c = lax.axis_index("x"), lax.axis_index("y"), lax.axis_index("c")
    return x, y, c, 4 * x + 2 * y + c


def _peer(x, y, c, k):
    return (1 - x if k & 4 else x, 1 - y if k & 2 else y, 1 - c if k & 1 else c)


ICI_MASKS = (4, 2, 6)


def _all_gather(shards):
    n = len(shards)

    def body(*refs):
        src, dst = refs[:n], refs[n:2 * n]
        send_sems, recv_sems, local_sems = refs[2 * n:]
        x, y, c, me = _place()
        sibling = _peer(x, y, c, 1)

        def copy(a, s, block, to, own=False):
            return pltpu.make_async_remote_copy(
                src_ref=src[a] if own else dst[a].at[block], dst_ref=dst[a].at[block],
                send_sem=send_sems.at[a * 7 + s], recv_sem=recv_sems.at[a * 7 + s], device_id=to, device_id_type=MESH_ID)

        local = [pltpu.make_async_copy(src[a], dst[a].at[me], local_sems.at[a]) for a in range(n)]
        for cp in local:
            cp.start()
        started = [copy(a, 0, me, sibling, own=True) for a in range(n)]
        started += [copy(a, 1 + j, me, _peer(x, y, c, k), own=True) for j, k in enumerate(ICI_MASKS) for a in range(n)]
        for cp in started:
            cp.start()
        for j, k in enumerate(ICI_MASKS):
            for a in range(n):
                copy(a, 1 + j, me ^ k, sibling).wait_recv()
                fwd = copy(a, 4 + j, me ^ k, sibling)
                fwd.start()
                started.append(fwd)
        for a in range(n):
            copy(a, 0, me ^ 1, sibling).wait_recv()
        for j, k in enumerate(ICI_MASKS):
            for a in range(n):
                copy(a, 4 + j, me ^ 1 ^ k, sibling).wait_recv()
        for cp in started:
            cp.wait_send()
        for cp in local:
            cp.wait()

    hbm = pl.BlockSpec(memory_space=pl.ANY)
    return pl.pallas_call(
        body, name="all_gather_weights",
        in_specs=[hbm] * n, out_specs=[hbm] * n,
        out_shape=[jax.ShapeDtypeStruct((N_DEV,) + s.shape, s.dtype) for s in shards],
        scratch_shapes=[pltpu.SemaphoreType.DMA((7 * n,)), pltpu.SemaphoreType.DMA((7 * n,)),
                        pltpu.SemaphoreType.DMA((n,))],
    )(*shards)


def _direct_copies(src, land, send_sems, recv_sems):
    x, y, c, me = _place()
    return [pltpu.make_async_remote_copy(
        src_ref=src[a], dst_ref=land[a].at[me], send_sem=send_sems.at[a * 7 + k - 1],
        recv_sem=recv_sems.at[a * 7 + k - 1], device_id=_peer(x, y, c, k), device_id_type=MESH_ID)
        for k in range(1, N_DEV) for a in range(len(src))]


def _gather_start(shards, name):
    n = len(shards)

    def body(*refs):
        src, land = refs[:n], refs[n:2 * n]
        send_sems, recv_sems = refs[2 * n], refs[2 * n + 1]
        token_ref = refs[-1]
        for cp in _direct_copies(src, land, send_sems, recv_sems):
            cp.start()
        token_ref[...] = jnp.zeros_like(token_ref)

    hbm = pl.BlockSpec(memory_space=pltpu.HBM)
    sem = pl.BlockSpec(memory_space=pltpu.SEMAPHORE)
    lands = [lax.empty((N_DEV,) + s.shape, s.dtype) for s in shards]
    out = pl.pallas_call(
        body, name=name + "_start",
        out_shape=(pltpu.SemaphoreType.DMA((7 * n,)), pltpu.SemaphoreType.DMA((7 * n,)),
                   *[pltpu.HBM(s.shape, s.dtype) for s in shards], *[pltpu.HBM(s.shape, s.dtype) for s in lands],
                   jax.ShapeDtypeStruct((1, D), F32)),
        in_specs=[hbm] * (2 * n), out_specs=(sem, sem, *[hbm] * (2 * n), _whole_vmem()),
        input_output_aliases={a: 2 + a for a in range(2 * n)},
        compiler_params=pltpu.CompilerParams(has_side_effects=pltpu.SideEffectType.DATAFLOW_SIDE_EFFECTING),
    )(*[pltpu.with_memory_space_constraint(s, pltpu.HBM) for s in list(shards) + lands])
    return out[0], out[1], out[2:2 + n], out[2 + n:2 + 2 * n], out[-1]


def _gather_wait(send_sems, recv_sems, flying, lands, after, name):
    n = len(flying)

    def body(*refs):
        src, land = refs[:n], refs[n:2 * n]
        for cp in _direct_copies(src, land, refs[2 * n], refs[2 * n + 1]):
            cp.wait_send()
            cp.wait_recv()

    hbm = pl.BlockSpec(memory_space=pltpu.HBM)
    sem = pl.BlockSpec(memory_space=pltpu.SEMAPHORE)
    out = pl.pallas_call(
        body, name=name + "_wait",
        out_shape=tuple(pltpu.HBM(s.shape, s.dtype) for s in list(flying) + list(lands)),
        in_specs=[hbm] * (2 * n) + [sem, sem, pl.BlockSpec(memory_space=pl.ANY)], out_specs=tuple([hbm] * (2 * n)),
        input_output_aliases={a: a for a in range(2 * n)},
        compiler_params=pltpu.CompilerParams(has_side_effects=pltpu.SideEffectType.DATAFLOW_SIDE_EFFECTING),
    )(*flying, *lands, send_sems, recv_sems, after)
    return out[n:]


def _sibling_copies(src, land, send_sems, recv_sems, blocks):
    x, y, c, _ = _place()
    sibling = _peer(x, y, c, 1)
    return [pltpu.make_async_remote_copy(
        src_ref=src[a].at[2 * p + (1 - c)], dst_ref=land[a].at[p], send_sem=send_sems.at[a * 4 + p],
        recv_sem=recv_sems.at[a * 4 + p], device_id=sibling, device_id_type=MESH_ID)
        for a in range(len(src)) for p in blocks[a]]


def _exchange_sibling_start(by_dest, blocks):
    n = len(by_dest)

    def body(*refs):
        for cp in _sibling_copies(refs[:n], refs[n:2 * n], refs[2 * n], refs[2 * n + 1], blocks):
            cp.start()

    hbm = pl.BlockSpec(memory_space=pltpu.HBM)
    sem = pl.BlockSpec(memory_space=pltpu.SEMAPHORE)
    lands = [lax.empty((4,) + s.shape[1:], s.dtype) for s in by_dest]
    out = pl.pallas_call(
        body, name="exchange_sibling_start",
        out_shape=(pltpu.SemaphoreType.DMA((4 * n,)), pltpu.SemaphoreType.DMA((4 * n,)),
                   *[pltpu.HBM(s.shape, s.dtype) for s in by_dest], *[pltpu.HBM(s.shape, s.dtype) for s in lands]),
        in_specs=[hbm] * (2 * n), out_specs=(sem, sem, *[hbm] * (2 * n)),
        input_output_aliases={a: 2 + a for a in range(2 * n)},
        compiler_params=pltpu.CompilerParams(has_side_effects=pltpu.SideEffectType.DATAFLOW_SIDE_EFFECTING),
    )(*[pltpu.with_memory_space_constraint(s, pltpu.HBM) for s in list(by_dest) + lands])
    return out[0], out[1], out[2:2 + n], out[2 + n:]


def _exchange_sibling_wait(send_sems, recv_sems, flying, lands, blocks):
    n = len(flying)

    def body(*refs):
        for cp in _sibling_copies(refs[:n], refs[n:2 * n], refs[2 * n], refs[2 * n + 1], blocks):
            cp.wait_recv()
            cp.wait_send()

    hbm = pl.BlockSpec(memory_space=pltpu.HBM)
    sem = pl.BlockSpec(memory_space=pltpu.SEMAPHORE)
    out = pl.pallas_call(
        body, name="exchange_sibling_wait",
        out_shape=tuple(pltpu.HBM(s.shape, s.dtype) for s in list(flying) + list(lands)),
        in_specs=[hbm] * (2 * n) + [sem, sem], out_specs=tuple([hbm] * (2 * n)),
        input_output_aliases={a: a for a in range(2 * n)},
        compiler_params=pltpu.CompilerParams(has_side_effects=pltpu.SideEffectType.DATAFLOW_SIDE_EFFECTING),
    )(*[pltpu.with_memory_space_constraint(s, pltpu.HBM) for s in list(flying) + list(lands)], send_sems, recv_sems)
    return out[:n], out[n:]


def _chip_copies(src, land, send_sems, recv_sems):
    x, y, c, _ = _place()
    chip = 2 * x + y
    return [pltpu.make_async_remote_copy(
        src_ref=src[a].at[chip ^ (k >> 1)], dst_ref=land[a].at[j], send_sem=send_sems.at[a * 3 + j],
        recv_sem=recv_sems.at[a * 3 + j], device_id=_peer(x, y, c, k), device_id_type=MESH_ID)
        for j, k in enumerate(ICI_MASKS) for a in range(len(src))]


def _exchange_chips_start(by_chip):
    n = len(by_chip)

    def body(*refs):
        src, land = refs[:n], refs[n:2 * n]
        send_sems, recv_sems = refs[2 * n], refs[2 * n + 1]
        token_ref = refs[-1]
        for cp in _chip_copies(src, land, send_sems, recv_sems):
            cp.start()
        token_ref[...] = jnp.zeros_like(token_ref)

    hbm = pl.BlockSpec(memory_space=pltpu.HBM)
    sem = pl.BlockSpec(memory_space=pltpu.SEMAPHORE)
    lands = [lax.empty((3,) + s.shape[1:], s.dtype) for s in by_chip]
    out = pl.pallas_call(
        body, name="exchange_chips_start",
        out_shape=(pltpu.SemaphoreType.DMA((3 * n,)), pltpu.SemaphoreType.DMA((3 * n,)),
                   *[pltpu.HBM(s.shape, s.dtype) for s in by_chip], *[pltpu.HBM(s.shape, s.dtype) for s in lands],
                   jax.ShapeDtypeStruct((1, D), F32)),
        in_specs=[hbm] * (2 * n), out_specs=(sem, sem, *[hbm] * (2 * n), _whole_vmem()),
        input_output_aliases={a: 2 + a for a in range(2 * n)},
        compiler_params=pltpu.CompilerParams(has_side_effects=pltpu.SideEffectType.DATAFLOW_SIDE_EFFECTING),
    )(*[pltpu.with_memory_space_constraint(s, pltpu.HBM) for s in list(by_chip) + lands])
    return out[0], out[1], out[2:2 + n], out[2 + n:2 + 2 * n], out[-1]


def _exchange_chips_wait(send_sems, recv_sems, flying, lands, after):
    n = len(flying)

    def body(*refs):
        src, land = refs[:n], refs[n:2 * n]
        send_sems_ref, recv_sems_ref = refs[2 * n], refs[2 * n + 1]
        for cp in _chip_copies(src, land, send_sems_ref, recv_sems_ref):
            cp.wait_send()
            cp.wait_recv()

    hbm = pl.BlockSpec(memory_space=pltpu.HBM)
    sem = pl.BlockSpec(memory_space=pltpu.SEMAPHORE)
    out = pl.pallas_call(
        body, name="exchange_chips_wait",
        out_shape=tuple(pltpu.HBM(s.shape, s.dtype) for s in list(flying) + list(lands)),
        in_specs=[hbm] * (2 * n) + [sem, sem, pl.BlockSpec(memory_space=pl.ANY)], out_specs=tuple([hbm] * (2 * n)),
        input_output_aliases={a: a for a in range(2 * n)},
        compiler_params=pltpu.CompilerParams(has_side_effects=pltpu.SideEffectType.DATAFLOW_SIDE_EFFECTING),
    )(*flying, *lands, send_sems, recv_sems, after)
    return out[n:]


def _adamw_math(w, g, m, v):
    m = ADAM_B1 * m + (1.0 - ADAM_B1) * g
    v = ADAM_B2 * v + (1.0 - ADAM_B2) * (g * g)
    m_hat = m / (1.0 - ADAM_B1 ** ADAM_STEP)
    v_hat = v / (1.0 - ADAM_B2 ** ADAM_STEP)
    return -ADAM_LR * (m_hat / (jnp.sqrt(v_hat) + ADAM_EPS) + ADAM_WD * w), m, v


def _pair_sum(owns, recvs, place_arr, tr, name):
    n = len(owns)
    _, rows, cols = owns[0].shape

    def body(place_ref, *refs):
        for a in range(n):
            s = refs[a][...] + refs[n + a][...].astype(F32)
            refs[3 * n + a][...] = s.astype(BF16)

            @pl.when(pl.program_id(1) == place_ref[1])
            def _(a=a, s=s):
                refs[2 * n + a][...] = s

    by_chip = pl.BlockSpec((None, tr, cols), lambda i, p, place_ref: (p, i, 0))
    mine = pl.BlockSpec((None, tr, cols), lambda i, p, place_ref: (2 * p + place_ref[0], i, 0))
    kept = pl.BlockSpec((tr, cols), lambda i, p, place_ref: (i, 0))
    out = pl.pallas_call(
        body, name=name,
        grid_spec=pltpu.PrefetchScalarGridSpec(
            num_scalar_prefetch=1, grid=(rows // tr, 4), in_specs=[mine] * n + [by_chip] * n,
            out_specs=[kept] * n + [by_chip] * n),
        out_shape=[jax.ShapeDtypeStruct((rows, cols), F32)] * n + [jax.ShapeDtypeStruct((4, rows, cols), BF16)] * n,
        compiler_params=_params("parallel", "arbitrary"),
    )(place_arr, *owns, *recvs)
    return out[:n], out[n:]


def _chip_sum(pairs, recvs, tr, name, adam=None):
    n = len(pairs)
    rows, cols = pairs[0].shape
    n_state = 0 if adam is None else 3 * n

    def body(*refs):
        outs = refs[2 * n + n_state:]
        for a in range(n):
            g = refs[a][...]
            for j in range(3):
                g = g + refs[n + a][j].astype(F32)
            outs[a][...] = g
            if adam is not None:
                w_ref, m_ref, v_ref = (refs[2 * n + s * n + a] for s in range(3))
                outs[n + a][...], outs[2 * n + a][...], outs[3 * n + a][...] = _adamw_math(w_ref[...], g, m_ref[...], v_ref[...])

    blk = pl.BlockSpec((tr, cols), lambda i: (i, 0))
    n_out = n if adam is None else 4 * n
    out = pl.pallas_call(
        body, name=name, grid=(rows // tr,),
        in_specs=[blk] * n + [pl.BlockSpec((3, tr, cols), lambda i: (0, i, 0))] * n + [blk] * n_state,
        out_specs=[blk] * n_out,
        out_shape=[jax.ShapeDtypeStruct((rows, cols), F32)] * n_out,
        compiler_params=_params("parallel"),
    )(*pairs, *recvs, *([] if adam is None else [t for group in adam for t in group]))
    return out if adam is None else (out[:n], out[n:2 * n], out[2 * n:3 * n], out[3 * n:])


def _finish_small(small_all, me, ws, ms, vs):
    n = len(ws)

    def body(me_ref, s_ref, c_ref, *refs):
        g, gc = s_ref[0], c_ref[0]
        for d in range(1, N_DEV):
            g = g + s_ref[d]
            gc = gc + c_ref[d]
        refs[3 * n][...] = g[32:33, 0:1]
        grads = [g[0:1], g[8:9], g[16:17, 0:N_HEADS], gc[0:3]]
        for a in range(n):
            w_ref, m_ref, v_ref = (refs[s * n + a] for s in range(3))
            outs = [refs[(3 + s) * n + 1 + a] for s in range(4)]
            if len(w_ref.shape) == 3:
                for r in range(w_ref.shape[0]):
                    g_r = grads[a][r:r + 1]
                    outs[0][r] = g_r
                    outs[1][r], outs[2][r], outs[3][r] = _adamw_math(w_ref[r], g_r, m_ref[r], v_ref[r])
            else:
                outs[0][...] = grads[a]
                outs[1][...], outs[2][...], outs[3][...] = _adamw_math(w_ref[...], grads[a], m_ref[...], v_ref[...])

    def whole(shape):
        return pl.BlockSpec(shape, lambda i, me_ref: (0,) * len(shape))

    params = [whole(w.shape) for w in ws]
    out = pl.pallas_call(
        body, name="finish_small",
        grid_spec=pltpu.PrefetchScalarGridSpec(
            num_scalar_prefetch=1, grid=(1,),
            in_specs=[whole(small_all.shape), pl.BlockSpec((N_DEV, 8, SHARD_SQ), lambda i, me_ref: (0, 3, me_ref[0]))]
            + params * 3,
            out_specs=[whole((1, 1))] + params * 4),
        out_shape=[jax.ShapeDtypeStruct((1, 1), F32)] + [jax.ShapeDtypeStruct(w.shape, F32) for w in ws] * 4,
    )(me, small_all, small_all, *ws, *ms, *vs)
    return out[0], out[1:1 + n], out[1 + n:1 + 2 * n], out[1 + 2 * n:1 + 3 * n], out[1 + 3 * n:]


def _rope_tables():
    inv_freq = np.float32(ROPE_THETA) ** (-np.arange(0, HEAD_DIM, 2, dtype=np.float32) / np.float32(HEAD_DIM))
    ang = (np.arange(SEQ_LEN, dtype=np.float32)[:, None] * inv_freq.astype(np.float32)[None, :]).astype(np.float64)
    cos, sin = np.cos(ang).astype(np.float32), np.sin(ang).astype(np.float32)
    return jnp.asarray(np.tile(cos, (1, 4))), jnp.asarray(np.tile(np.concatenate([-sin, sin], axis=1), (1, 2)))


def _local_step(x, target, g_pre, g_post, sinks, wt, wconv, squares, start_exchange=None):
    cos_t, sin_t = _rope_tables()
    wconv8 = jnp.pad(wconv, ((0, 5), (0, 0)))
    h, q, kv, g3 = _fwd_in_attn(x, g_pre, wt, cos_t, sin_t, 512)
    wpc, wpa, wout = squares(kv)
    a4, ya = _fwd_in_conv(h, wt, wconv8, wpc, 512)
    attn, ub, lse = _fwd_attn(sinks, q, kv, g3, 4)
    loss8, dout, dya, dub, dgab, dwout, dwpa, dgpost8 = _fwd_out_bwd_head(ya, ub, g3, x, target, g_post, wpa, wout, 512)
    dq, dza, dkv_own, dkv_prev, dsink8, dh_part = _bwd_attn(sinks, q, kv, attn, lse, dub, g3, cos_t, sin_t, wt, 4)
    da4, dwpc, dwconv8, dwt = _bwd_conv(dya, a4, h, wconv8, wpc, 512, 2)
    dkv, dwt = _bwd_kv_finish(dkv_own, dkv_prev, cos_t, sin_t, h, dwt)
    dwt = _bwd_dw_in(h, [(dq, ROW_Q), (dza, ROW_ZA)], 1024, 1024, "bwd_dw_in_q_za", dwt)
    land, sent = None, None
    if start_exchange is not None:
        dwt, land, sent = start_exchange[0](dwt, dwpc, dwpa, dwout)
    dwt32, dwt16, *land = _bwd_dw_in(h, [(dgab, ROW_GA)], 1024, 1024, "bwd_dw_in_gates", dwt, land)
    token, pending = (None, None) if start_exchange is None else start_exchange[1](dwt32, dwt16, land[0], sent)
    g_pre_after = g_pre if token is None else g_pre + token
    grad_x, dgpre8 = _bwd_dh(da4, dh_part, dkv, dgab, wt, x, g_pre_after, dout, 512)
    small = jnp.concatenate([dgpre8, dgpost8, jnp.pad(dsink8, ((0, 0), (0, D - 128))), dwconv8,
                             jnp.pad(loss8, ((0, 0), (0, D - 128)))], axis=0)
    return loss8[0, 0], grad_x, dwt32, dwt16, dwpc, dwpa, dwout, small, pending


def kernel(x, g_pre, g_post, w_in, w_conv, sinks, w_proj_conv, w_proj_attn, w_out, loss_target, m_g_pre, m_g_post, m_w_in, m_w_conv, m_sinks, m_w_proj_conv, m_w_proj_attn, m_w_out, v_g_pre, v_g_post, v_w_in, v_w_conv, v_sinks, v_w_proj_conv, v_w_proj_attn, v_w_out):
    batch = x.shape[0]
    mx, my, mc, me = _place()
    place_arr = jnp.stack([mc, 2 * mx + my]).astype(jnp.int32)

    g_wt, g_conv = _all_gather([w_in[0].T.astype(BF16), jnp.pad(w_conv[0], ((0, 5), (0, 0)))])
    wt = g_wt.reshape(D_IN, D)
    wconv = g_conv[:, 0:3, :].transpose(1, 0, 2).reshape(3, D)
    sq_mine = [w.astype(BF16) for w in (w_proj_conv[0], w_proj_attn[0], w_out[0])]
    wt, sq_mine = lax.optimization_barrier((wt, sq_mine))
    sq_send, sq_recv, sq_flying, sq_lands, sq_token = _gather_start(sq_mine, "gather_squares")

    def squares(after):
        got = _gather_wait(sq_send, sq_recv, sq_flying, sq_lands, after, "gather_squares")
        return [lax.dynamic_update_index_in_dim(full, mine, me, 0).reshape(D, D) for full, mine in zip(got, sq_mine)]

    early = [(0, 1, 2)] + [(0, 1, 2, 3)] * 3

    def send_early(dwt, dwpc, dwpa, dwout):
        own_sq = [g.reshape(N_DEV, SHARD_SQ, D) for g in (dwpc, dwpa, dwout)]
        send_sems, recv_sems, flying, lands = _exchange_sibling_start([dwt[1].reshape(N_DEV, SHARD_IN, D)] + own_sq, early)
        return (dwt[0], flying[0].reshape(D_IN, D)), lands[0], (send_sems, recv_sems, flying[1:], lands[1:])

    def send_late(dwt32, dwt16, land, sent):
        own_in = dwt32.reshape(N_DEV, SHARD_IN, D)
        send_sems, recv_sems, sq_flying, sq_lands = sent
        dwt16 = dwt16.reshape(N_DEV, SHARD_IN, D)
        sent_arrays, from_sibling = _exchange_sibling_wait(
            send_sems, recv_sems, [dwt16] + list(sq_flying), [land] + list(sq_lands), early)
        own_sq = sent_arrays[1:]
        in32, in16 = _pair_sum([own_in], from_sibling[:1], place_arr, SHARD_IN // 2, "pair_sum_w_in")
        sq32, sq16 = _pair_sum(own_sq, from_sibling[1:], place_arr, SHARD_SQ, "pair_sum_squares")
        send_sems, recv_sems, flying, lands, token = _exchange_chips_start(list(in16) + list(sq16))
        return token, (send_sems, recv_sems, flying, lands, in32, sq32)

    _, grad_x, _, _, _, _, _, small, pending = _local_step(
        x.reshape(batch * SEQ_LEN, D), loss_target.reshape(batch * SEQ_LEN, D), g_pre + sq_token, g_post,
        sinks, wt, wconv, squares, (send_early, send_late))
    sm_send, sm_recv, sm_flying, sm_lands, sm_token = _gather_start([small], "gather_small")
    send_sems, recv_sems, flying, lands, in32, sq32 = pending
    from_chips = _exchange_chips_wait(send_sems, recv_sems, flying, lands, sm_token)

    o_in = [o[0].T for o in _chip_sum(
        in32, from_chips[:1], SHARD_IN // 6, "chip_sum_adamw_w_in",
        adam=([w_in[0].T], [m_w_in[0].T], [v_w_in[0].T]))]
    g_in_mine, o_in = o_in[0], o_in[1:]
    g_sq, d_sq, m_sq, v_sq = _chip_sum(
        sq32, from_chips[1:], SHARD_SQ // 4, "chip_sum_adamw_squares",
        adam=([w_proj_conv[0], w_proj_attn[0], w_out[0]], [m_w_proj_conv[0], m_w_proj_attn[0], m_w_out[0]],
              [v_w_proj_conv[0], v_w_proj_attn[0], v_w_out[0]]))
    both_done, g_in_mine = lax.optimization_barrier((d_sq[0], g_in_mine))
    (small_all,) = _gather_wait(sm_send, sm_recv, sm_flying, sm_lands, both_done, "gather_small")
    def by_row(a):
        return a.transpose(1, 0, 2)

    loss, (g_g_pre, g_g_post, g_sinks, g_conv_mine), *o_small = _finish_small(
        lax.dynamic_update_index_in_dim(small_all, small, me, 0), jnp.reshape(me, (1,)).astype(jnp.int32),
        [g_pre, g_post, sinks, by_row(w_conv)], [m_g_pre, m_g_post, m_sinks, by_row(m_w_conv)],
        [v_g_pre, v_g_post, v_sinks, by_row(v_w_conv)])
    loss = loss.reshape(())

    grads = [g_g_pre, g_g_post, g_in_mine[None], by_row(g_conv_mine), g_sinks] + [g[None] for g in g_sq]
    rest = []
    for idx, sq in enumerate((d_sq, m_sq, v_sq)):
        gp, gq, sk, cv = o_small[idx]
        rest += [gp, gq, o_in[idx][None], by_row(cv), sk] + [s[None] for s in sq]
    return (loss, grad_x.reshape(batch, SEQ_LEN, D), *grads, *rest)
```
